```python
import jax, jax.numpy as jnp
from jax import lax
import numpy as np

D_MODEL = 2048
BATCH = 1
SEQ = 8192
DEPTH = 2

GRID_W = 64
CTX_LEN = 256
HEAD_DIM = 128
N_HEADS = D_MODEL // HEAD_DIM
A_HEADS = 6
A_KV_HEADS = 2
A_GROUP = A_HEADS // A_KV_HEADS
A_WINDOW = 128
A_BLOCK = 128
B_HEADS = 6
NA_ROWS = 8
NA_COLS = 16
C_GROUPS = N_HEADS - A_HEADS - B_HEADS
C_GROUP_W = HEAD_DIM
C_W = C_GROUPS * C_GROUP_W
CHUNK = 128
N_EXPERTS = 32
N_EXPERT_GROUPS = 4
EXPERTS_PER_GROUP = N_EXPERTS // N_EXPERT_GROUPS
TOP_K = 2
EXPERT_FF = 512
ROPE_BASE = 10000.0
LN_EPS = 1e-5
NEG_INF = -1e30
DEEPNORM_ALPHA = (2 * DEPTH) ** 0.25
DEEPNORM_BETA = (8 * DEPTH) ** -0.25

A_Q_W = A_HEADS * HEAD_DIM
A_KV_W = A_KV_HEADS * HEAD_DIM
B_W = B_HEADS * HEAD_DIM
OFF_AK = A_Q_W
OFF_AV = OFF_AK + A_KV_W
OFF_BQ = OFF_AV + A_KV_W
OFF_BK = OFF_BQ + B_W
OFF_BV = OFF_BK + B_W
OFF_C = OFF_BV + B_W
IN_COLS = OFF_C + 2 * C_W
MIX_W = A_Q_W + B_W + C_W

kernel_name = 'hybrid_dit_window_natten_sgu_moe'


def layer_norm(x, g, b):
    xf = x.astype(jnp.float32)
    mu = jnp.mean(xf, axis=-1, keepdims=True)
    var = jnp.mean(jnp.square(xf - mu), axis=-1, keepdims=True)
    return ((xf - mu) * lax.rsqrt(var + LN_EPS)).astype(x.dtype) * g + b


def heads(p, lo, hi, n):
    return p[..., lo:hi].reshape(p.shape[:-1] + (n, HEAD_DIM))


def axial_rope(n):
    t = jnp.arange(n)
    row = (t // GRID_W).astype(jnp.float32)
    col = (t % GRID_W).astype(jnp.float32)
    n_freq = HEAD_DIM // 4
    inv_freq = ROPE_BASE ** (-jnp.arange(n_freq, dtype=jnp.float32) / n_freq)
    ang = jnp.concatenate([row[:, None] * inv_freq, col[:, None] * inv_freq], axis=-1)
    return jnp.cos(ang), jnp.sin(ang)


def apply_rope(x, cos, sin):
    half = HEAD_DIM // 2
    c = cos[None, :, None, :].astype(x.dtype)
    s = sin[None, :, None, :].astype(x.dtype)
    x1, x2 = x[..., :half], x[..., half:]
    return jnp.concatenate([x1 * c - x2 * s, x2 * c + x1 * s], axis=-1)


def window_attention(q, k, v, kc, vc, sink):
    B, n, H, dh = q.shape
    L = kc.shape[1]
    nb = n // A_BLOCK
    scale = dh ** -0.5
    qb = q.reshape(B, nb, A_BLOCK, A_KV_HEADS, A_GROUP, dh)

    def band(t):
        tp = jnp.pad(t, ((0, 0), (A_BLOCK, A_BLOCK), (0, 0), (0, 0)))
        tp = tp.reshape(B, nb + 2, A_BLOCK, A_KV_HEADS, dh)
        return jnp.concatenate([tp[:, :-2], tp[:, 1:-1], tp[:, 2:]], axis=2)

    k_band, v_band = band(k), band(v)
    blk = np.arange(nb)[:, None, None]
    q_pos = blk * A_BLOCK + np.arange(A_BLOCK)[None, :, None]
    k_pos = (blk - 1) * A_BLOCK + np.arange(3 * A_BLOCK)[None, None, :]
    valid = (np.abs(k_pos - q_pos) <= A_WINDOW) & (k_pos >= 0) & (k_pos < n)
    s_loc = jnp.einsum('bnqkgd,bnjkd->bnkgqj', qb, k_band).astype(jnp.float32) * scale
    s_loc = jnp.where(valid[None, :, None, None], s_loc, NEG_INF)
    s_ctx = jnp.einsum('bnqkgd,bckd->bnkgqc', qb, kc).astype(jnp.float32) * scale
    s_sink = jnp.broadcast_to(sink.astype(jnp.float32).reshape(1, 1, A_KV_HEADS, A_GROUP, 1, 1),
                              s_ctx.shape[:-1] + (1,))
    p = jax.nn.softmax(jnp.concatenate([s_sink, s_ctx, s_loc], axis=-1), axis=-1).astype(v.dtype)
    out = (jnp.einsum('bnkgqc,bckd->bnqkgd', p[..., 1:1 + L], vc)
           + jnp.einsum('bnkgqj,bnjkd->bnqkgd', p[..., 1 + L:], v_band))
    return out.reshape(B, n, H * dh)


def context_sink_attention(qc, kc, vc, sink):
    B, L, H, dh = qc.shape
    qg = qc.reshape(B, L, A_KV_HEADS, A_GROUP, dh)
    s = jnp.einsum('blkgd,bckd->bkglc', qg, kc).astype(jnp.float32) * (dh ** -0.5)
    s_sink = jnp.broadcast_to(sink.astype(jnp.float32).reshape(1, A_KV_HEADS, A_GROUP, 1, 1),
                              s.shape[:-1] + (1,))
    p = jax.nn.softmax(jnp.concatenate([s_sink, s], axis=-1), axis=-1).astype(vc.dtype)
    return jnp.einsum('bkglc,bckd->blkgd', p[..., 1:], vc).reshape(B, L, H * dh)


def context_attention(qc, kc, vc):
    B, L, H, dh = qc.shape
    s = jnp.einsum('blhd,bchd->bhlc', qc, kc).astype(jnp.float32) * (dh ** -0.5)
    p = jax.nn.softmax(s, axis=-1).astype(vc.dtype)
    return jnp.einsum('bhlc,bchd->blhd', p, vc).reshape(B, L, H * dh)


def neighbourhood_attention(q, k, v, kc, vc, rpb):
    B, n, H, dh = q.shape
    L = kc.shape[1]
    rows = n // GRID_W
    kh = min(NA_ROWS, rows)
    scale = dh ** -0.5
    kg = k.reshape(B, rows, GRID_W, H, dh)
    vg = v.reshape(B, rows, GRID_W, H, dh)
    cols = np.arange(GRID_W)
    col_idx = np.clip(cols - NA_COLS // 2, 0, GRID_W - NA_COLS)[:, None] + np.arange(NA_COLS)[None, :]
    bias_cols = rpb[:, :, col_idx - cols[:, None] + NA_COLS - 1]

    def row_block(args):
        r, q_row = args
        r0 = jnp.clip(r - kh // 2, 0, rows - kh)
        k_win = lax.dynamic_slice_in_dim(kg, r0, kh, axis=1)[:, :, col_idx]
        v_win = lax.dynamic_slice_in_dim(vg, r0, kh, axis=1)[:, :, col_idx]
        bias = bias_cols[:, r0 + jnp.arange(kh) - r + NA_ROWS - 1].transpose(2, 0, 1, 3)
        s_loc = (jnp.einsum('bqhd,brqwhd->bqhrw', q_row, k_win).astype(jnp.float32) * scale
                 + bias[None].astype(jnp.float32))
        s_ctx = jnp.einsum('bqhd,bchd->bqhc', q_row, kc).astype(jnp.float32) * scale
        logits = jnp.concatenate([s_ctx, s_loc.reshape(B, GRID_W, H, kh * NA_COLS)], axis=-1)
        p = jax.nn.softmax(logits, axis=-1).astype(v.dtype)
        p_loc = p[..., L:].reshape(B, GRID_W, H, kh, NA_COLS)
        return (jnp.einsum('bqhc,bchd->bqhd', p[..., :L], vc)
                + jnp.einsum('bqhrw,brqwhd->bqhd', p_loc, v_win))

    q_rows = jnp.moveaxis(q.reshape(B, rows, GRID_W, H, dh), 1, 0)
    out = lax.map(row_block, (jnp.arange(rows), q_rows))
    return jnp.moveaxis(out, 0, 1).reshape(B, n, H * dh)


def spatial_gating(z, ln_g, ln_b, w_s, b_s):
    u, v = z[..., :C_W], z[..., C_W:]
    B, n, _ = v.shape
    vg = v.reshape(B, n // CHUNK, CHUNK, C_GROUPS, C_GROUP_W)
    vg = layer_norm(vg, ln_g.reshape(C_GROUPS, C_GROUP_W), ln_b.reshape(C_GROUPS, C_GROUP_W))
    mixed = jnp.einsum('gpq,bnqgc->bnpgc', w_s, vg) + b_s.T[None, None, :, :, None]
    return u * mixed.reshape(B, n, C_W)


def moe(h, w_router, router_bias, w_gate, w_up, w_down):
    B, n, D = h.shape
    t = h.reshape(B * n, D)
    scores = jax.nn.sigmoid((t @ w_router).astype(jnp.float32))
    biased = (scores + router_bias.astype(jnp.float32)).reshape(-1, N_EXPERT_GROUPS, EXPERTS_PER_GROUP)
    group_score = jnp.sum(lax.top_k(biased, TOP_K)[0], axis=-1)
    grp = jnp.argmax(group_score, axis=-1)
    in_group = jnp.take_along_axis(biased, grp[:, None, None], axis=1)[:, 0]
    _, local = lax.top_k(in_group, TOP_K)
    e_idx = grp[:, None] * EXPERTS_PER_GROUP + local
    w = jnp.take_along_axis(scores, e_idx, axis=-1)
    w = w / jnp.sum(w, axis=-1, keepdims=True)
    combine = jnp.einsum('tk,tke->te', w, jax.nn.one_hot(e_idx, N_EXPERTS, dtype=jnp.float32)).astype(h.dtype)
    gate = jnp.einsum('td,edf->tef', t, w_gate)
    up = jnp.einsum('td,edf->tef', t, w_up)
    hid = jax.nn.silu(gate) * up * combine[:, :, None]
    return jnp.einsum('tef,efd->td', hid, w_down).reshape(B, n, D)


def setup_inputs(seed: int = 0) -> dict:
    key = jax.random.key(seed)
    ks = jax.random.split(key, 23)
    D = D_MODEL

    def nrm(k, shape, std):
        return jax.random.normal(k, shape, jnp.float32) * std

    return {
        'x': nrm(ks[0], (BATCH, SEQ, D), 1.0),
        'c': nrm(ks[1], (BATCH, D), 1.0),
        'ctx': nrm(ks[2], (BATCH, CTX_LEN, D), 1.0),
        'c_ctx': nrm(ks[3], (D,), 1.0),
        'w_mod': nrm(ks[4], (DEPTH, D, 6 * D), 0.5 * D ** -0.5),
        'b_mod': nrm(ks[5], (DEPTH, 6 * D), 0.01),
        'w_in': nrm(ks[6], (DEPTH, D, IN_COLS), D ** -0.5),
        'attn_sink': nrm(ks[7], (DEPTH, A_HEADS), 0.5),
        'na_rpb': nrm(ks[8], (DEPTH, B_HEADS, 2 * NA_ROWS - 1, 2 * NA_COLS - 1), 0.02),
        'sgu_ln_g': 1.0 + nrm(ks[9], (DEPTH, C_W), 0.01),
        'sgu_ln_b': nrm(ks[10], (DEPTH, C_W), 0.01),
        'sgu_w': nrm(ks[11], (DEPTH, C_GROUPS, CHUNK, CHUNK), CHUNK ** -0.5),
        'sgu_b': nrm(ks[12], (DEPTH, C_GROUPS, CHUNK), 0.01),
        'w_out': nrm(ks[13], (DEPTH, MIX_W, D), DEEPNORM_BETA * MIX_W ** -0.5),
        'ln1_g': 1.0 + nrm(ks[14], (DEPTH, D), 0.01),
        'ln1_b': nrm(ks[15], (DEPTH, D), 0.01),
        'w_router': nrm(ks[16], (D, N_EXPERTS), D ** -0.5),
        'router_bias': nrm(ks[17], (N_EXPERTS,), 0.01),
        'w_gate': nrm(ks[18], (DEPTH, N_EXPERTS, D, EXPERT_FF), D ** -0.5),
        'w_up': nrm(ks[19], (DEPTH, N_EXPERTS, D, EXPERT_FF), D ** -0.5),
        'w_down': nrm(ks[20], (DEPTH, N_EXPERTS, EXPERT_FF, D), DEEPNORM_BETA * EXPERT_FF ** -0.5),
        'ln2_g': 1.0 + nrm(ks[21], (DEPTH, D), 0.01),
        'ln2_b': nrm(ks[22], (DEPTH, D), 0.01),
    }


def reference(x, c, ctx, c_ctx, w_mod, b_mod, w_in, attn_sink, na_rpb, sgu_ln_g, sgu_ln_b, sgu_w, sgu_b,
              w_out, ln1_g, ln1_b, w_router, router_bias, w_gate, w_up, w_down, ln2_g, ln2_b):
    B, n, D = x.shape
    L = ctx.shape[1]
    cos, sin = axial_rope(n)
    silu_c = jax.nn.silu(c)
    silu_cc = jax.nn.silu(c_ctx)
    for l in range(DEPTH):
        last = l == DEPTH - 1
        sh1, sc1, g1, sh2, sc2, g2 = jnp.split(silu_c @ w_mod[l] + b_mod[l], 6, axis=-1)
        n_mod_c = 2 if last else 6
        mod_c = jnp.split(silu_cc @ w_mod[l][:, :n_mod_c * D] + b_mod[l][:n_mod_c * D], n_mod_c)
        h = x * (1.0 + sc1[:, None]) + sh1[:, None]
        hc = ctx * (1.0 + mod_c[1]) + mod_c[0]

        p = h @ w_in[l]
        if last:
            pc_a = hc @ w_in[l][:, OFF_AK:OFF_BQ]
            pc_b = hc @ w_in[l][:, OFF_BK:OFF_C]
        else:
            pc = hc @ w_in[l]
            pc_a = pc[..., OFF_AK:OFF_BQ]
            pc_b = pc[..., OFF_BK:OFF_C]
        kc_a = heads(pc_a, 0, A_KV_W, A_KV_HEADS)
        vc_a = heads(pc_a, A_KV_W, 2 * A_KV_W, A_KV_HEADS)
        kc_b = heads(pc_b, 0, B_W, B_HEADS)
        vc_b = heads(pc_b, B_W, 2 * B_W, B_HEADS)

        o_a = window_attention(apply_rope(heads(p, 0, OFF_AK, A_HEADS), cos, sin),
                               apply_rope(heads(p, OFF_AK, OFF_AV, A_KV_HEADS), cos, sin),
                               heads(p, OFF_AV, OFF_BQ, A_KV_HEADS), kc_a, vc_a, attn_sink[l])
        o_b = neighbourhood_attention(heads(p, OFF_BQ, OFF_BK, B_HEADS), heads(p, OFF_BK, OFF_BV, B_HEADS),
                                      heads(p, OFF_BV, OFF_C, B_HEADS), kc_b, vc_b, na_rpb[l])
        o_c = spatial_gating(jax.nn.gelu(p[..., OFF_C:]), sgu_ln_g[l], sgu_ln_b[l], sgu_w[l], sgu_b[l])
        mix = jnp.concatenate([o_a, o_b, o_c], axis=-1) @ w_out[l]
        x = layer_norm(DEEPNORM_ALPHA * x + g1[:, None] * mix, ln1_g[l], ln1_b[l])
        h2 = x * (1.0 + sc2[:, None]) + sh2[:, None]

        if last:
            y = moe(h2, w_router, router_bias, w_gate[l], w_up[l], w_down[l])
        else:
            oc_a = context_sink_attention(heads(pc, 0, OFF_AK, A_HEADS), kc_a, vc_a, attn_sink[l])
            oc_b = context_attention(heads(pc, OFF_BQ, OFF_BK, B_HEADS), kc_b, vc_b)
            oc_c = spatial_gating(jax.nn.gelu(pc[..., OFF_C:]), sgu_ln_g[l], sgu_ln_b[l], sgu_w[l], sgu_b[l])
            mix_c = jnp.concatenate([oc_a, oc_b, oc_c], axis=-1) @ w_out[l]
            ctx = layer_norm(DEEPNORM_ALPHA * ctx + mod_c[2] * mix_c, ln1_g[l], ln1_b[l])
            h2c = ctx * (1.0 + mod_c[4]) + mod_c[3]
            y_all = moe(jnp.concatenate([h2c, h2], axis=1), w_router, router_bias, w_gate[l], w_up[l], w_down[l])
            ctx = layer_norm(DEEPNORM_ALPHA * ctx + mod_c[5] * y_all[:, :L], ln2_g[l], ln2_b[l])
            y = y_all[:, L:]
        x = layer_norm(DEEPNORM_ALPHA * x + g2[:, None] * y, ln2_g[l], ln2_b[l])
    return x
```

```python
import functools

import numpy as np
import jax
import jax.numpy as jnp
from jax import lax
from jax.experimental import pallas as pl
from jax.experimental.pallas import tpu as pltpu

F32 = jnp.float32
BF16 = jnp.bfloat16

D_MODEL = 2048
DEPTH = 2
GRID_W = 64
HEAD_DIM = 128
A_HEADS = 6
A_KV_HEADS = 2
A_GROUP = A_HEADS // A_KV_HEADS
A_BLOCK = 128
B_HEADS = 6
NA_ROWS = 8
NA_COLS = 16
C_GROUPS = 4
C_W = C_GROUPS * HEAD_DIM
CHUNK = 128
N_EXPERTS = 32
N_EXPERT_GROUPS = 4
EXPERTS_PER_GROUP = N_EXPERTS // N_EXPERT_GROUPS
TOP_K = 2
EXPERT_FF = 512
ROPE_BASE = 10000.0
LN_EPS = 1e-5
NEG_INF = -1e30
DEEPNORM_ALPHA = (2 * DEPTH) ** 0.25
ATTN_SCALE = HEAD_DIM ** -0.5

A_Q_W = A_HEADS * HEAD_DIM
A_KV_W = A_KV_HEADS * HEAD_DIM
B_W = B_HEADS * HEAD_DIM
OFF_AK = A_Q_W
OFF_AV = OFF_AK + A_KV_W
OFF_BQ = OFF_AV + A_KV_W
OFF_BK = OFF_BQ + B_W
OFF_BV = OFF_BK + B_W
OFF_C = OFF_BV + B_W
IN_COLS = OFF_C + 2 * C_W

VMEM_LIMIT_BYTES = 56 * 1024 * 1024
LANES = 128

ROW_TILE = 256
PROJ_COL_TILE = 512
MOD_COL_TILE = 1024
MOE_TILE = 256

MOD_SH1, MOD_SC1, MOD_G1, MOD_SH2, MOD_SC2, MOD_G2 = range(6)


def _params(*sem):
    return pltpu.CompilerParams(dimension_semantics=sem, vmem_limit_bytes=VMEM_LIMIT_BYTES)


def _layer_norm(t, g, b):
    mu = jnp.mean(t, axis=-1, keepdims=True)
    d = t - mu
    var = jnp.mean(d * d, axis=-1, keepdims=True)
    return d * lax.rsqrt(var + LN_EPS) * g + b


def _sigmoid(v):
    return 1.0 / (1.0 + jnp.exp(-v))


def _dot_nt(a, b):
    return lax.dot_general(a, b, (((1,), (1,)), ((), ())), preferred_element_type=F32)


def _mod_kernel(c_ref, w_ref, b_ref, o_ref):
    w = w_ref[...]
    reps = w.shape[1] // LANES
    rows = []
    for r in range(2):
        cv = c_ref[r]
        s = cv * _sigmoid(cv)
        sb = jnp.concatenate([s] * reps, axis=1)
        rows.append(jnp.sum(w * sb, axis=0, keepdims=True) + b_ref[...])
    rows.append(jnp.zeros((6, w.shape[1]), F32))
    o_ref[...] = jnp.concatenate(rows, axis=0)


def _modulation(c, c_ctx, w_mod, b_mod):
    d = D_MODEL
    cb = jnp.stack([jnp.broadcast_to(c.reshape(d, 1), (d, LANES)),
                    jnp.broadcast_to(c_ctx.reshape(d, 1), (d, LANES))])
    n_out = 6 * d
    return pl.pallas_call(
        _mod_kernel,
        grid=(DEPTH, n_out // MOD_COL_TILE),
        in_specs=[pl.BlockSpec((2, d, LANES), lambda l, j: (0, 0, 0)),
                  pl.BlockSpec((None, d, MOD_COL_TILE), lambda l, j: (l, 0, j)),
                  pl.BlockSpec((None, 1, MOD_COL_TILE), lambda l, j: (l, 0, j))],
        out_specs=pl.BlockSpec((None, 8, MOD_COL_TILE), lambda l, j: (l, 0, j)),
        out_shape=jax.ShapeDtypeStruct((DEPTH, 8, n_out), F32),
        compiler_params=_params("arbitrary", "arbitrary"),
        name="modulation",
    )(cb, w_mod, b_mod.reshape(DEPTH, 1, n_out))


def _gelu_tanh(v):
    return 0.5 * v * (1.0 + jnp.tanh(np.sqrt(2.0 / np.pi).astype(np.float32) * (v + 0.044715 * (v * v * v))))


def _proj_kernel(x_ref, mod_ref, cos_ref, sin_ref, w_ref, o_ref, *, rope):
    x = x_ref[...]
    h = (x * (1.0 + mod_ref[MOD_SC1:MOD_SC1 + 1, :]) + mod_ref[MOD_SH1:MOD_SH1 + 1, :]).astype(BF16)
    tn = PROJ_COL_TILE
    for j in range(IN_COLS // tn):
        c0 = j * tn
        acc = jnp.dot(h, w_ref[:, c0:c0 + tn], preferred_element_type=F32)
        if c0 < OFF_AV:
            if rope:
                cos = cos_ref[...]
                sin = sin_ref[...]
                parts = []
                for hh in range(tn // HEAD_DIM):
                    a = acc[:, hh * HEAD_DIM:(hh + 1) * HEAD_DIM]
                    parts.append(a * cos + pltpu.roll(a, HEAD_DIM // 2, 1) * sin)
                acc = jnp.concatenate(parts, axis=1)
        elif c0 >= OFF_C:
            acc = _gelu_tanh(acc)
        o_ref[:, c0:c0 + tn] = acc.astype(BF16)


def _proj(x, x_off, rows, mod, cos, sin, w_bf, rope):
    d = D_MODEL
    return pl.pallas_call(
        functools.partial(_proj_kernel, rope=rope),
        grid=(rows // ROW_TILE,),
        in_specs=[pl.BlockSpec((ROW_TILE, d), lambda i: (i + x_off, 0)),
                  pl.BlockSpec((8, d), lambda i: (0, 0)),
                  pl.BlockSpec((ROW_TILE, HEAD_DIM), lambda i: (i, 0)),
                  pl.BlockSpec((ROW_TILE, HEAD_DIM), lambda i: (i, 0)),
                  pl.BlockSpec((d, IN_COLS), lambda i: (0, 0), pipeline_mode=pl.Buffered(1))],
        out_specs=pl.BlockSpec((ROW_TILE, IN_COLS), lambda i: (i, 0)),
        out_shape=jax.ShapeDtypeStruct((rows, IN_COLS), BF16),
        compiler_params=_params("arbitrary"),
        name="proj_rope" if rope else "proj_ctx",
    )(x, mod, cos, sin, w_bf)


def _softmax_pv(s_parts, v_parts, sink):
    m = s_parts[0].max(axis=-1, keepdims=True)
    for s in s_parts[1:]:
        m = jnp.maximum(m, s.max(axis=-1, keepdims=True))
    if sink is not None:
        m = jnp.maximum(m, sink)
    denom = None if sink is None else jnp.exp(sink - m)
    out = None
    for s, v in zip(s_parts, v_parts):
        e = jnp.exp(s - m)
        es = e.sum(axis=-1, keepdims=True)
        denom = es if denom is None else denom + es
        pv = jnp.dot(e.astype(BF16), v, preferred_element_type=F32)
        out = pv if out is None else out + pv
    return out / denom


def _attn_a_kernel(sink_ref, q_ref, kp_ref, kc_ref, kn_ref, vp_ref, vc_ref, vn_ref, kx_ref, vx_ref, o_ref,
                   mask_ref, *, latent):
    i = pl.program_id(0)
    nb = pl.num_programs(0)
    nq = A_GROUP * A_BLOCK

    if latent:
        @pl.when(i == 0)
        def _():
            qi = lax.broadcasted_iota(jnp.int32, (nq, 3 * A_BLOCK), 0) % A_BLOCK
            jj = lax.broadcasted_iota(jnp.int32, (nq, 3 * A_BLOCK), 1)
            ok = (jj >= qi) & (jj <= qi + 2 * A_BLOCK)
            mask_ref[...] = jnp.where(ok, 0.0, NEG_INF).astype(F32)

        col = lax.broadcasted_iota(jnp.int32, (nq, 3 * A_BLOCK), 1)
        off_band = ((col < A_BLOCK) & (i == 0)) | ((col >= 2 * A_BLOCK) & (i == nb - 1))

    for kh in range(A_KV_HEADS):
        hs = [kh * A_GROUP + g for g in range(A_GROUP)]
        q = jnp.concatenate([q_ref[:, h * HEAD_DIM:(h + 1) * HEAD_DIM] for h in hs], axis=0)
        sink = jnp.concatenate([jnp.full((A_BLOCK, 1), sink_ref[h], F32) for h in hs], axis=0)
        ks = slice(kh * HEAD_DIM, (kh + 1) * HEAD_DIM)
        s_parts = [_dot_nt(q, kx_ref[:, ks]) * ATTN_SCALE]
        v_parts = [vx_ref[:, ks]]
        if latent:
            kband = jnp.concatenate([kp_ref[:, ks], kc_ref[:, ks], kn_ref[:, ks]], axis=0)
            vband = jnp.concatenate([vp_ref[:, ks], vc_ref[:, ks], vn_ref[:, ks]], axis=0)
            s_loc = _dot_nt(q, kband) * ATTN_SCALE + mask_ref[...]
            s_parts.append(jnp.where(off_band, NEG_INF, s_loc))
            v_parts.append(vband)
        out = _softmax_pv(s_parts, v_parts, sink)
        for g, h in enumerate(hs):
            o_ref[:, h * HEAD_DIM:(h + 1) * HEAD_DIM] = out[g * A_BLOCK:(g + 1) * A_BLOCK].astype(BF16)


def _attn_a(p, pc, sink, latent):
    rows = p.shape[0]
    nb = rows // A_BLOCK
    kcol = OFF_AK // A_KV_W
    vcol = OFF_AV // A_KV_W

    def band(col, shift):
        return pl.BlockSpec((A_BLOCK, A_KV_W), lambda i, s: (jnp.clip(i + shift, 0, nb - 1), col))

    lctx = pc.shape[0]
    return pl.pallas_call(
        functools.partial(_attn_a_kernel, latent=latent),
        grid_spec=pltpu.PrefetchScalarGridSpec(
            num_scalar_prefetch=1,
            grid=(nb,),
            in_specs=[pl.BlockSpec((A_BLOCK, A_Q_W), lambda i, s: (i, 0)),
                      band(kcol, -1), band(kcol, 0), band(kcol, 1),
                      band(vcol, -1), band(vcol, 0), band(vcol, 1),
                      pl.BlockSpec((lctx, A_KV_W), lambda i, s: (0, kcol)),
                      pl.BlockSpec((lctx, A_KV_W), lambda i, s: (0, vcol))],
            out_specs=pl.BlockSpec((A_BLOCK, A_Q_W), lambda i, s: (i, 0)),
            scratch_shapes=[pltpu.VMEM((A_GROUP * A_BLOCK, 3 * A_BLOCK), F32)]),
        out_shape=jax.ShapeDtypeStruct((rows, A_Q_W), BF16),
        compiler_params=_params("arbitrary"),
        name="attn_a_latent" if latent else "attn_a_ctx",
    )(sink, p, p, p, p, p, p, p, pc, pc)


B_PAIR_W = 2 * HEAD_DIM
NA_GROUP_ROWS = ROW_TILE // GRID_W
NA_BIAS_TILES = 2 * NA_ROWS - 2


def _attn_b_kernel(q_ref, k_ref, v_ref, kx_ref, vx_ref, bias_ref, o_ref, *, latent, grid_rows):
    g = pl.program_id(1)
    kh = min(NA_ROWS, grid_rows)
    for rr in range(NA_GROUP_ROWS):
        rs = slice(rr * GRID_W, (rr + 1) * GRID_W)
        if latent:
            r = g * NA_GROUP_ROWS + rr
            r0 = jnp.clip(r - kh // 2, 0, grid_rows - kh)
            start = pl.multiple_of(r0 * GRID_W, GRID_W)
        for hh in range(2):
            hs = slice(hh * HEAD_DIM, (hh + 1) * HEAD_DIM)
            q = q_ref[rs, hs]
            s_parts = [_dot_nt(q, kx_ref[:, hs]) * ATTN_SCALE]
            v_parts = [vx_ref[:, hs]]
            if latent:
                kwin = k_ref[pl.ds(start, kh * GRID_W), hs]
                vwin = v_ref[pl.ds(start, kh * GRID_W), hs]
                bias = jnp.concatenate(
                    [bias_ref[hh, r0 - r + (NA_ROWS - 1) + 2 * j] for j in range(kh // 2)], axis=1)
                s_parts.append(_dot_nt(q, kwin) * ATTN_SCALE + bias)
                v_parts.append(vwin)
            out = _softmax_pv(s_parts, v_parts, None)
            o_ref[rs, hs] = out.astype(BF16)


def _na_bias_table(rpb):
    cols = np.arange(GRID_W)
    c0 = np.clip(cols - NA_COLS // 2, 0, GRID_W - NA_COLS)
    rel = cols[None, :] - cols[:, None] + NA_COLS - 1
    ok = (cols[None, :] >= c0[:, None]) & (cols[None, :] < c0[:, None] + NA_COLS)
    rel = np.where(ok, rel, 0)
    t = jnp.where(ok[None, None], rpb[:, :, rel], NEG_INF).astype(F32)
    return jnp.concatenate([t[:, :-1], t[:, 1:]], axis=-1)


def _attn_b(p, pc, bias_tab, latent):
    rows = p.shape[0]
    lctx = pc.shape[0]
    qcol = OFF_BQ // B_PAIR_W
    kcol = OFF_BK // B_PAIR_W
    vcol = OFF_BV // B_PAIR_W
    return pl.pallas_call(
        functools.partial(_attn_b_kernel, latent=latent, grid_rows=rows // GRID_W),
        grid=(B_HEADS // 2, rows // ROW_TILE),
        in_specs=[pl.BlockSpec((ROW_TILE, B_PAIR_W), lambda hp, g: (g, qcol + hp)),
                  pl.BlockSpec((rows, B_PAIR_W), lambda hp, g: (0, kcol + hp)),
                  pl.BlockSpec((rows, B_PAIR_W), lambda hp, g: (0, vcol + hp)),
                  pl.BlockSpec((lctx, B_PAIR_W), lambda hp, g: (0, kcol + hp)),
                  pl.BlockSpec((lctx, B_PAIR_W), lambda hp, g: (0, vcol + hp)),
                  pl.BlockSpec((2, NA_BIAS_TILES, GRID_W, 2 * GRID_W), lambda hp, g: (hp, 0, 0, 0))],
        out_specs=pl.BlockSpec((ROW_TILE, B_PAIR_W), lambda hp, g: (g, hp)),
        out_shape=jax.ShapeDtypeStruct((rows, B_W), BF16),
        compiler_params=_params("arbitrary", "arbitrary"),
        name="attn_b_latent" if latent else "attn_b_ctx",
    )(p, p, p, pc, pc, bias_tab)


def _sgu_kernel(u_ref, v_ref, g_ref, b_ref, w_ref, bs_ref, o_ref):
    for ch in range(ROW_TILE // CHUNK):
        rs = slice(ch * CHUNK, (ch + 1) * CHUNK)
        for grp in range(C_GROUPS):
            cs = slice(grp * HEAD_DIM, (grp + 1) * HEAD_DIM)
            vn = _layer_norm(v_ref[rs, cs].astype(F32), g_ref[:, cs], b_ref[:, cs])
            mixed = jnp.dot(w_ref[grp], vn.astype(BF16), preferred_element_type=F32) + bs_ref[grp]
            o_ref[rs, cs] = (u_ref[rs, cs].astype(F32) * mixed).astype(BF16)


def _sgu(p, ln_g, ln_b, w_bf, bs_b):
    rows = p.shape[0]
    ucol = OFF_C // C_W
    return pl.pallas_call(
        _sgu_kernel,
        grid=(rows // ROW_TILE,),
        in_specs=[pl.BlockSpec((ROW_TILE, C_W), lambda i: (i, ucol)),
                  pl.BlockSpec((ROW_TILE, C_W), lambda i: (i, ucol + 1)),
                  pl.BlockSpec((1, C_W), lambda i: (0, 0)),
                  pl.BlockSpec((1, C_W), lambda i: (0, 0)),
                  pl.BlockSpec((C_GROUPS, CHUNK, CHUNK), lambda i: (0, 0, 0)),
                  pl.BlockSpec((C_GROUPS, CHUNK, LANES), lambda i: (0, 0, 0))],
        out_specs=pl.BlockSpec((ROW_TILE, C_W), lambda i: (i, 0)),
        out_shape=jax.ShapeDtypeStruct((rows, C_W), BF16),
        compiler_params=_params("arbitrary"),
        name="sgu",
    )(p, p, ln_g, ln_b, w_bf, bs_b)


def _outproj_kernel(oa_ref, ob_ref, oc_ref, w_ref, x_ref, mod_ref, g_ref, b_ref, *rest):
    x1_ref, h2_ref = rest[-2], rest[-1]
    mix = jnp.dot(oa_ref[...], w_ref[0:A_Q_W, :], preferred_element_type=F32)
    mix += jnp.dot(ob_ref[...], w_ref[A_Q_W:A_Q_W + B_W, :], preferred_element_type=F32)
    mix += jnp.dot(oc_ref[...], w_ref[A_Q_W + B_W:, :], preferred_element_type=F32)
    t = DEEPNORM_ALPHA * x_ref[...] + mod_ref[MOD_G1:MOD_G1 + 1, :] * mix
    x1 = _layer_norm(t, g_ref[...], b_ref[...])
    x1_ref[...] = x1
    h2_ref[...] = x1 * (1.0 + mod_ref[MOD_SC2:MOD_SC2 + 1, :]) + mod_ref[MOD_SH2:MOD_SH2 + 1, :]


def _outproj(o_a, o_b, o_c, w_bf, x, x_off, mod, ln_g, ln_b, total_rows, out_off, prev):
    rows = o_a.shape[0]
    d = D_MODEL
    in_specs = [pl.BlockSpec((ROW_TILE, A_Q_W), lambda i: (i, 0)),
                pl.BlockSpec((ROW_TILE, B_W), lambda i: (i, 0)),
                pl.BlockSpec((ROW_TILE, C_W), lambda i: (i, 0)),
                pl.BlockSpec((d, d), lambda i: (0, 0), pipeline_mode=pl.Buffered(1)),
                pl.BlockSpec((ROW_TILE, d), lambda i: (i + x_off, 0)),
                pl.BlockSpec((8, d), lambda i: (0, 0)),
                pl.BlockSpec((1, d), lambda i: (0, 0)),
                pl.BlockSpec((1, d), lambda i: (0, 0))]
    args = [o_a, o_b, o_c, w_bf, x, mod, ln_g, ln_b]
    aliases = {}
    if prev is not None:
        in_specs += [pl.BlockSpec(memory_space=pl.ANY), pl.BlockSpec(memory_space=pl.ANY)]
        aliases = {len(args): 0, len(args) + 1: 1}
        args += list(prev)
    out_spec = pl.BlockSpec((ROW_TILE, d), lambda i: (i + out_off, 0))
    return pl.pallas_call(
        _outproj_kernel,
        grid=(rows // ROW_TILE,),
        in_specs=in_specs,
        out_specs=[out_spec, out_spec],
        out_shape=[jax.ShapeDtypeStruct((total_rows, d), F32)] * 2,
        input_output_aliases=aliases,
        compiler_params=_params("arbitrary"),
        name="outproj",
    )(*args)


def _top2_sublanes(vals, sub):
    m1 = vals.max(axis=0, keepdims=True)
    i1 = jnp.where(vals == m1, sub, vals.shape[0]).min(axis=0, keepdims=True)
    rest = jnp.where(sub == i1, -jnp.inf, vals)
    m2 = rest.max(axis=0, keepdims=True)
    i2 = jnp.where(rest == m2, sub, vals.shape[0]).min(axis=0, keepdims=True)
    return m1, i1, m2, i2


def _router_kernel(h_ref, wr_ref, rb_ref, e_ref, w_ref, rank_ref, cnt_ref, run_ref):
    i = pl.program_id(0)
    tm = h_ref.shape[0]
    epg = EXPERTS_PER_GROUP

    @pl.when(i == 0)
    def _():
        run_ref[...] = jnp.zeros_like(run_ref)

    logits = lax.dot_general(wr_ref[...], h_ref[...], (((1,), (1,)), ((), ())),
                             precision=lax.Precision.HIGHEST, preferred_element_type=F32)
    scores = _sigmoid(logits)
    biased = scores + jnp.concatenate([rb_ref[...]] * (tm // LANES), axis=1)
    sub = lax.broadcasted_iota(jnp.int32, (epg, tm), 0)

    best = None
    for g in range(N_EXPERT_GROUPS):
        m1, _, m2, _ = _top2_sublanes(biased[g * epg:(g + 1) * epg], sub)
        gs = m1 + m2
        if best is None:
            best, grp = gs, jnp.zeros((1, tm), jnp.int32)
            bsel, ssel = biased[0:epg], scores[0:epg]
        else:
            better = gs > best
            best = jnp.where(better, gs, best)
            grp = jnp.where(better, g, grp)
            bsel = jnp.where(better, biased[g * epg:(g + 1) * epg], bsel)
            ssel = jnp.where(better, scores[g * epg:(g + 1) * epg], ssel)
    _, i1, _, i2 = _top2_sublanes(bsel, sub)
    w1 = jnp.where(sub == i1, ssel, 0.0).sum(axis=0, keepdims=True)
    w2 = jnp.where(sub == i2, ssel, 0.0).sum(axis=0, keepdims=True)
    tot = w1 + w2
    e1 = grp * epg + i1
    e2 = grp * epg + i2

    eiota = lax.broadcasted_iota(jnp.int32, (N_EXPERTS, tm), 0)
    oh1 = (eiota == e1).astype(F32)
    oh2 = (eiota == e2).astype(F32)
    ohb = oh1 + oh2
    before = (lax.broadcasted_iota(jnp.int32, (tm, tm), 0) < lax.broadcasted_iota(jnp.int32, (tm, tm), 1))
    prefix = jnp.dot(ohb.astype(BF16), before.astype(BF16), preferred_element_type=F32)
    pos = run_ref[...] + prefix
    r1 = (oh1 * pos).sum(axis=0, keepdims=True)
    r2 = (oh2 * pos).sum(axis=0, keepdims=True)
    run_ref[...] = run_ref[...] + ohb.sum(axis=1, keepdims=True)

    e_ref[...] = jnp.concatenate([e1, e2], axis=0)
    w_ref[...] = jnp.concatenate([w1 / tot, w2 / tot], axis=0)
    rank_ref[...] = jnp.concatenate([r1, r2], axis=0).astype(jnp.int32)
    cnt_ref[...] = run_ref[:, 0:LANES]


def _router(h2, wr_t, rb_b):
    t = h2.shape[0]
    d = D_MODEL
    row2 = pl.BlockSpec((TOP_K, ROW_TILE), lambda i: (0, i))
    return pl.pallas_call(
        _router_kernel,
        grid=(t // ROW_TILE,),
        in_specs=[pl.BlockSpec((ROW_TILE, d), lambda i: (i, 0)),
                  pl.BlockSpec((N_EXPERTS, d), lambda i: (0, 0)),
                  pl.BlockSpec((N_EXPERTS, LANES), lambda i: (0, 0))],
        out_specs=[row2, row2, row2, pl.BlockSpec((N_EXPERTS, LANES), lambda i: (0, 0))],
        out_shape=[jax.ShapeDtypeStruct((TOP_K, t), jnp.int32),
                   jax.ShapeDtypeStruct((TOP_K, t), F32),
                   jax.ShapeDtypeStruct((TOP_K, t), jnp.int32),
                   jax.ShapeDtypeStruct((N_EXPERTS, LANES), F32)],
        scratch_shapes=[pltpu.VMEM((N_EXPERTS, ROW_TILE), F32)],
        compiler_params=_params("arbitrary"),
        name="router",
    )(h2, wr_t, rb_b)


def _row_copy(src_hbm, row, dst, sem):
    return pltpu.make_async_copy(src_hbm.at[pl.ds(row, 1), :], dst, sem)


def _experts_kernel(te_ref, nu_ref, src_ref, h_hbm, wg_ref, wu_ref, wd_ref, o_ref,
                    xbuf, sem, wg_bf, wu_bf, wd_bf):
    i = pl.program_id(0)
    n_used = nu_ref[0]
    tm = MOE_TILE

    def issue(tile, slot):
        def body(r, carry):
            tok = src_ref[tile * tm + r]
            _row_copy(h_hbm, tok, xbuf.at[slot, pl.ds(r, 1), :], sem.at[slot]).start()
            return carry
        lax.fori_loop(0, tm, body, 0)

    @pl.when(i == 0)
    def _():
        issue(0, 0)

    @pl.when(i + 1 < n_used)
    def _():
        issue(i + 1, (i + 1) % 2)

    @pl.when(i < n_used)
    def _():
        slot = i % 2
        pltpu.make_async_copy(h_hbm.at[pl.ds(0, tm), :], xbuf.at[slot], sem.at[slot]).wait()

        @pl.when((i == 0) | (te_ref[i] != te_ref[jnp.maximum(i - 1, 0)]))
        def _():
            wg_bf[...] = wg_ref[...].astype(BF16)
            wu_bf[...] = wu_ref[...].astype(BF16)
            wd_bf[...] = wd_ref[...].astype(BF16)

        x = xbuf[slot].astype(BF16)
        gate = jnp.dot(x, wg_bf[...], preferred_element_type=F32)
        up = jnp.dot(x, wu_bf[...], preferred_element_type=F32)
        hid = (gate * _sigmoid(gate)) * up
        o_ref[...] = jnp.dot(hid.astype(BF16), wd_bf[...], preferred_element_type=F32)


def _experts(h2, tile_expert, n_used, src_tok, w_gate, w_up, w_down):
    d = D_MODEL
    max_tiles = tile_expert.shape[0]
    tm = MOE_TILE
    return pl.pallas_call(
        _experts_kernel,
        grid_spec=pltpu.PrefetchScalarGridSpec(
            num_scalar_prefetch=3,
            grid=(max_tiles,),
            in_specs=[pl.BlockSpec(memory_space=pl.ANY),
                      pl.BlockSpec((None, d, EXPERT_FF), lambda i, te, nu, src: (te[i], 0, 0)),
                      pl.BlockSpec((None, d, EXPERT_FF), lambda i, te, nu, src: (te[i], 0, 0)),
                      pl.BlockSpec((None, EXPERT_FF, d), lambda i, te, nu, src: (te[i], 0, 0))],
            out_specs=pl.BlockSpec((tm, d), lambda i, te, nu, src: (jnp.where(i < nu[0], i, max_tiles), 0)),
            scratch_shapes=[pltpu.VMEM((2, tm, d), F32),
                            pltpu.SemaphoreType.DMA((2,)),
                            pltpu.VMEM((d, EXPERT_FF), BF16),
                            pltpu.VMEM((d, EXPERT_FF), BF16),
                            pltpu.VMEM((EXPERT_FF, d), BF16)]),
        out_shape=jax.ShapeDtypeStruct(((max_tiles + 1) * tm, d), F32),
        compiler_params=_params("arbitrary"),
        name="experts",
    )(tile_expert, n_used, src_tok, h2, w_gate, w_up, w_down)


def _combine_kernel(slot_ref, ys_hbm, w_ref, x_ref, mod_ref, g_ref, b_ref, o_ref, ybuf, sem, *, n_tok):
    i = pl.program_id(0)
    nt = pl.num_programs(0)
    tm = ROW_TILE

    def issue(blk, buf):
        def body(r, carry):
            for k in range(TOP_K):
                row = slot_ref[k * n_tok + blk * tm + r]
                _row_copy(ys_hbm, row, ybuf.at[buf, k, pl.ds(r, 1), :], sem.at[buf]).start()
            return carry
        lax.fori_loop(0, tm, body, 0)

    @pl.when(i == 0)
    def _():
        issue(0, 0)

    @pl.when(i + 1 < nt)
    def _():
        issue(i + 1, (i + 1) % 2)

    buf = i % 2
    for k in range(TOP_K):
        pltpu.make_async_copy(ys_hbm.at[pl.ds(0, tm), :], ybuf.at[buf, k], sem.at[buf]).wait()
    y = w_ref[:, 0:1] * ybuf[buf, 0] + w_ref[:, 1:2] * ybuf[buf, 1]
    t = DEEPNORM_ALPHA * x_ref[...] + mod_ref[MOD_G2:MOD_G2 + 1, :] * y
    o_ref[...] = _layer_norm(t, g_ref[...], b_ref[...])


def _combine(ys, slots, w_tok, x1, mods, n_ctx_blocks, ln_g, ln_b):
    t = x1.shape[0]
    d = D_MODEL
    return pl.pallas_call(
        functools.partial(_combine_kernel, n_tok=t),
        grid_spec=pltpu.PrefetchScalarGridSpec(
            num_scalar_prefetch=1,
            grid=(t // ROW_TILE,),
            in_specs=[pl.BlockSpec(memory_space=pl.ANY),
                      pl.BlockSpec((ROW_TILE, TOP_K), lambda i, s: (i, 0)),
                      pl.BlockSpec((ROW_TILE, d), lambda i, s: (i, 0)),
                      pl.BlockSpec((None, 8, d), lambda i, s: (jnp.where(i < n_ctx_blocks, 0, 1), 0, 0)),
                      pl.BlockSpec((1, d), lambda i, s: (0, 0)),
                      pl.BlockSpec((1, d), lambda i, s: (0, 0))],
            out_specs=pl.BlockSpec((ROW_TILE, d), lambda i, s: (i, 0)),
            scratch_shapes=[pltpu.VMEM((2, TOP_K, ROW_TILE, d), F32),
                            pltpu.SemaphoreType.DMA((2,))]),
        out_shape=jax.ShapeDtypeStruct((t, d), F32),
        compiler_params=_params("arbitrary"),
        name="combine",
    )(slots, ys, w_tok, x1, mods, ln_g, ln_b)


def _moe(h2, x1, mods, n_ctx_blocks, wr_t, rb_b, w_gate, w_up, w_down, ln_g, ln_b):
    t = h2.shape[0]
    tm = MOE_TILE
    e_idx, w_tok, rank, cnt = _router(h2, wr_t, rb_b)
    counts = cnt[:, 0].astype(jnp.int32)
    tiles_per = (counts + tm - 1) // tm
    tile_end = jnp.cumsum(tiles_per)
    n_used = tile_end[-1]
    row_off = (tile_end - tiles_per) * tm
    slots = row_off[e_idx] + rank
    max_tiles = (TOP_K * t + N_EXPERTS * (tm - 1)) // tm + 1
    tile_ids = jnp.arange(max_tiles, dtype=jnp.int32)
    tile_expert = jnp.searchsorted(tile_end, jnp.minimum(tile_ids, n_used - 1), side="right").astype(jnp.int32)
    tok = jnp.tile(jnp.arange(t, dtype=jnp.int32), TOP_K)
    src_tok = jnp.zeros((max_tiles * tm,), jnp.int32).at[slots.reshape(-1)].set(tok)
    ys = _experts(h2, tile_expert, n_used.reshape(1).astype(jnp.int32), src_tok, w_gate, w_up, w_down)
    return _combine(ys, slots.reshape(-1), w_tok.T, x1, mods, n_ctx_blocks, ln_g, ln_b)


def _rope_tables(n):
    t = jnp.arange(n)
    row = (t // GRID_W).astype(F32)
    col = (t % GRID_W).astype(F32)
    n_freq = HEAD_DIM // 4
    inv_freq = ROPE_BASE ** (-jnp.arange(n_freq, dtype=F32) / n_freq)
    ang = jnp.concatenate([row[:, None] * inv_freq, col[:, None] * inv_freq], axis=-1)
    cos, sin = jnp.cos(ang), jnp.sin(ang)
    return jnp.concatenate([cos, cos], axis=-1), jnp.concatenate([-sin, sin], axis=-1)


def kernel(x, c, ctx, c_ctx, w_mod, b_mod, w_in, attn_sink, na_rpb, sgu_ln_g, sgu_ln_b, sgu_w, sgu_b,
           w_out, ln1_g, ln1_b, w_router, router_bias, w_gate, w_up, w_down, ln2_g, ln2_b):
    batch, n, d = x.shape
    lctx = ctx.shape[1]
    assert batch == 1 and d == D_MODEL and n % ROW_TILE == 0 and lctx % ROW_TILE == 0
    n_ctx_blocks = lctx // ROW_TILE

    mods = _modulation(c, c_ctx, w_mod, b_mod).reshape(DEPTH, 8, 6, d)
    mod_lat = jnp.pad(mods[:, 0], ((0, 0), (0, 2), (0, 0)))
    mod_ctx = jnp.pad(mods[:, 1], ((0, 0), (0, 2), (0, 0)))
    cos, sin = _rope_tables(n)
    wr_t = w_router.T
    rb_b = jnp.broadcast_to(router_bias.reshape(N_EXPERTS, 1), (N_EXPERTS, LANES))

    x_lat, lat_off = x[0], 0
    x_ctx = ctx[0]
    for l in range(DEPTH):
        last = l == DEPTH - 1
        w_in_bf = w_in[l].astype(BF16)
        w_out_bf = w_out[l].astype(BF16)
        sgu_w_bf = sgu_w[l].astype(BF16)
        sgu_b_b = jnp.broadcast_to(sgu_b[l][:, :, None], (C_GROUPS, CHUNK, LANES))
        ln_g_c, ln_b_c = sgu_ln_g[l].reshape(1, C_W), sgu_ln_b[l].reshape(1, C_W)
        bias_tab = _na_bias_table(na_rpb[l])
        g1, b1 = ln1_g[l].reshape(1, d), ln1_b[l].reshape(1, d)
        g2, b2 = ln2_g[l].reshape(1, d), ln2_b[l].reshape(1, d)

        p = _proj(x_lat, lat_off, n, mod_lat[l], cos, sin, w_in_bf, rope=True)
        pc = _proj(x_ctx, 0, lctx, mod_ctx[l], cos, sin, w_in_bf, rope=False)
        o_a = _attn_a(p, pc, attn_sink[l], latent=True)
        o_b = _attn_b(p, pc, bias_tab, latent=True)
        o_c = _sgu(p, ln_g_c, ln_b_c, sgu_w_bf, sgu_b_b)
        if last:
            x1, h2 = _outproj(o_a, o_b, o_c, w_out_bf, x_lat, lat_off, mod_lat[l], g1, b1, n, 0, None)
            x_lat = _moe(h2, x1, jnp.stack([mod_ctx[l], mod_lat[l]]), 0, wr_t, rb_b,
                         w_gate[l], w_up[l], w_down[l], g2, b2)
        else:
            oc_a = _attn_a(pc, pc, attn_sink[l], latent=False)
            oc_b = _attn_b(pc, pc, bias_tab, latent=False)
            oc_c = _sgu(pc, ln_g_c, ln_b_c, sgu_w_bf, sgu_b_b)
            total = lctx + n
            prev = _outproj(oc_a, oc_b, oc_c, w_out_bf, x_ctx, 0, mod_ctx[l], g1, b1, total, 0, None)
            x1, h2 = _outproj(o_a, o_b, o_c, w_out_bf, x_lat, lat_off, mod_lat[l], g1, b1, total,
                              n_ctx_blocks, prev)
            x_all = _moe(h2, x1, jnp.stack([mod_ctx[l], mod_lat[l]]), n_ctx_blocks, wr_t, rb_b,
                         w_gate[l], w_up[l], w_down[l], g2, b2)
            x_lat, lat_off, x_ctx = x_all, n_ctx_blocks, x_all
    return x_lat.reshape(batch, n, d)
```

```python
import functools

import numpy as np
import jax
import jax.numpy as jnp
from jax import lax
from jax.experimental import pallas as pl
from jax.experimental.pallas import tpu as pltpu

F32 = jnp.float32
BF16 = jnp.bfloat16

D_MODEL = 2048
DEPTH = 2
GRID_W = 64
HEAD_DIM = 128
A_HEADS = 6
A_KV_HEADS = 2
A_GROUP = A_HEADS // A_KV_HEADS
A_BLOCK = 128
B_HEADS = 6
NA_ROWS = 8
NA_COLS = 16
C_GROUPS = 4
C_W = C_GROUPS * HEAD_DIM
CHUNK = 128
N_EXPERTS = 32
N_EXPERT_GROUPS = 4
EXPERTS_PER_GROUP = N_EXPERTS // N_EXPERT_GROUPS
TOP_K = 2
EXPERT_FF = 512
ROPE_BASE = 10000.0
LN_EPS = 1e-5
NEG_INF = -1e30
DEEPNORM_ALPHA = (2 * DEPTH) ** 0.25
ATTN_SCALE = HEAD_DIM ** -0.5

A_Q_W = A_HEADS * HEAD_DIM
A_KV_W = A_KV_HEADS * HEAD_DIM
B_W = B_HEADS * HEAD_DIM
OFF_AK = A_Q_W
OFF_AV = OFF_AK + A_KV_W
OFF_BQ = OFF_AV + A_KV_W
OFF_BK = OFF_BQ + B_W
OFF_BV = OFF_BK + B_W
OFF_C = OFF_BV + B_W
IN_COLS = OFF_C + 2 * C_W

VMEM_LIMIT_BYTES = 56 * 1024 * 1024
LANES = 128

ROW_TILE = 256
PROJ_COL_TILE = 512
MOD_COL_TILE = 1024
MOE_TILE = 256

MOD_SH1, MOD_SC1, MOD_G1, MOD_SH2, MOD_SC2, MOD_G2 = range(6)


def _params(*sem):
    return pltpu.CompilerParams(dimension_semantics=sem, vmem_limit_bytes=VMEM_LIMIT_BYTES)


def _layer_norm(t, g, b):
    mu = jnp.mean(t, axis=-1, keepdims=True)
    d = t - mu
    var = jnp.mean(d * d, axis=-1, keepdims=True)
    return d * lax.rsqrt(var + LN_EPS) * g + b


def _sigmoid(v):
    return 1.0 / (1.0 + jnp.exp(-v))


def _dot_nt(a, b):
    return lax.dot_general(a, b, (((1,), (1,)), ((), ())), preferred_element_type=F32)


def _mod_kernel(c_ref, w_ref, b_ref, o_ref):
    w = w_ref[...]
    reps = w.shape[1] // LANES
    rows = []
    for r in range(2):
        cv = c_ref[r]
        s = cv * _sigmoid(cv)
        sb = jnp.concatenate([s] * reps, axis=1)
        rows.append(jnp.sum(w * sb, axis=0, keepdims=True) + b_ref[...])
    rows.append(jnp.zeros((6, w.shape[1]), F32))
    o_ref[...] = jnp.concatenate(rows, axis=0)


def _modulation(c, c_ctx, w_mod, b_mod):
    d = D_MODEL
    cb = jnp.stack([jnp.broadcast_to(c.reshape(d, 1), (d, LANES)),
                    jnp.broadcast_to(c_ctx.reshape(d, 1), (d, LANES))])
    n_out = 6 * d
    return pl.pallas_call(
        _mod_kernel,
        grid=(DEPTH, n_out // MOD_COL_TILE),
        in_specs=[pl.BlockSpec((2, d, LANES), lambda l, j: (0, 0, 0)),
                  pl.BlockSpec((None, d, MOD_COL_TILE), lambda l, j: (l, 0, j)),
                  pl.BlockSpec((None, 1, MOD_COL_TILE), lambda l, j: (l, 0, j))],
        out_specs=pl.BlockSpec((None, 8, MOD_COL_TILE), lambda l, j: (l, 0, j)),
        out_shape=jax.ShapeDtypeStruct((DEPTH, 8, n_out), F32),
        compiler_params=_params("arbitrary", "arbitrary"),
        name="modulation",
    )(cb, w_mod, b_mod.reshape(DEPTH, 1, n_out))


def _gelu_tanh(v):
    return 0.5 * v * (1.0 + jnp.tanh(np.sqrt(2.0 / np.pi).astype(np.float32) * (v + 0.044715 * (v * v * v))))


def _proj_kernel(x_ref, mod_ref, cos_ref, sin_ref, w_ref, o_ref, *, rope):
    x = x_ref[...]
    h = (x * (1.0 + mod_ref[MOD_SC1:MOD_SC1 + 1, :]) + mod_ref[MOD_SH1:MOD_SH1 + 1, :]).astype(BF16)
    tn = PROJ_COL_TILE
    for j in range(IN_COLS // tn):
        c0 = j * tn
        acc = jnp.dot(h, w_ref[:, c0:c0 + tn], preferred_element_type=F32)
        if c0 < OFF_AV:
            if rope:
                cos = cos_ref[...]
                sin = sin_ref[...]
                parts = []
                for hh in range(tn // HEAD_DIM):
                    a = acc[:, hh * HEAD_DIM:(hh + 1) * HEAD_DIM]
                    parts.append(a * cos + pltpu.roll(a, HEAD_DIM // 2, 1) * sin)
                acc = jnp.concatenate(parts, axis=1)
        elif c0 >= OFF_C:
            acc = _gelu_tanh(acc)
        o_ref[:, c0:c0 + tn] = acc.astype(BF16)


def _proj(x, x_off, rows, mod, cos, sin, w_bf, rope):
    d = D_MODEL
    return pl.pallas_call(
        functools.partial(_proj_kernel, rope=rope),
        grid=(rows // ROW_TILE,),
        in_specs=[pl.BlockSpec((ROW_TILE, d), lambda i: (i + x_off, 0)),
                  pl.BlockSpec((8, d), lambda i: (0, 0)),
                  pl.BlockSpec((ROW_TILE, HEAD_DIM), lambda i: (i, 0)),
                  pl.BlockSpec((ROW_TILE, HEAD_DIM), lambda i: (i, 0)),
                  pl.BlockSpec((d, IN_COLS), lambda i: (0, 0), pipeline_mode=pl.Buffered(1))],
        out_specs=pl.BlockSpec((ROW_TILE, IN_COLS), lambda i: (i, 0)),
        out_shape=jax.ShapeDtypeStruct((rows, IN_COLS), BF16),
        compiler_params=_params("arbitrary"),
        name="proj_rope" if rope else "proj_ctx",
    )(x, mod, cos, sin, w_bf)


def _softmax_pv(s_parts, v_parts, sink):
    m = s_parts[0].max(axis=-1, keepdims=True)
    for s in s_parts[1:]:
        m = jnp.maximum(m, s.max(axis=-1, keepdims=True))
    if sink is not None:
        m = jnp.maximum(m, sink)
    denom = None if sink is None else jnp.exp(sink - m)
    out = None
    for s, v in zip(s_parts, v_parts):
        e = jnp.exp(s - m)
        es = e.sum(axis=-1, keepdims=True)
        denom = es if denom is None else denom + es
        pv = jnp.dot(e.astype(BF16), v, preferred_element_type=F32)
        out = pv if out is None else out + pv
    return out / denom


def _attn_a_kernel(sink_ref, q_ref, kp_ref, kc_ref, kn_ref, vp_ref, vc_ref, vn_ref, kx_ref, vx_ref, o_ref,
                   mask_ref, *, latent):
    i = pl.program_id(0)
    nb = pl.num_programs(0)
    nq = A_GROUP * A_BLOCK

    if latent:
        @pl.when(i == 0)
        def _():
            qi = lax.broadcasted_iota(jnp.int32, (nq, 3 * A_BLOCK), 0) % A_BLOCK
            jj = lax.broadcasted_iota(jnp.int32, (nq, 3 * A_BLOCK), 1)
            ok = (jj >= qi) & (jj <= qi + 2 * A_BLOCK)
            mask_ref[...] = jnp.where(ok, 0.0, NEG_INF).astype(F32)

        col = lax.broadcasted_iota(jnp.int32, (nq, 3 * A_BLOCK), 1)
        off_band = ((col < A_BLOCK) & (i == 0)) | ((col >= 2 * A_BLOCK) & (i == nb - 1))

    for kh in range(A_KV_HEADS):
        hs = [kh * A_GROUP + g for g in range(A_GROUP)]
        q = jnp.concatenate([q_ref[:, h * HEAD_DIM:(h + 1) * HEAD_DIM] for h in hs], axis=0)
        sink = jnp.concatenate([jnp.full((A_BLOCK, 1), sink_ref[h], F32) for h in hs], axis=0)
        ks = slice(kh * HEAD_DIM, (kh + 1) * HEAD_DIM)
        s_parts = [_dot_nt(q, kx_ref[:, ks]) * ATTN_SCALE]
        v_parts = [vx_ref[:, ks]]
        if latent:
            kband = jnp.concatenate([kp_ref[:, ks], kc_ref[:, ks], kn_ref[:, ks]], axis=0)
            vband = jnp.concatenate([vp_ref[:, ks], vc_ref[:, ks], vn_ref[:, ks]], axis=0)
            s_loc = _dot_nt(q, kband) * ATTN_SCALE + mask_ref[...]
            s_parts.append(jnp.where(off_band, NEG_INF, s_loc))
            v_parts.append(vband)
        out = _softmax_pv(s_parts, v_parts, sink)
        for g, h in enumerate(hs):
            o_ref[:, h * HEAD_DIM:(h + 1) * HEAD_DIM] = out[g * A_BLOCK:(g + 1) * A_BLOCK].astype(BF16)


def _attn_a(p, pc, sink, latent):
    rows = p.shape[0]
    nb = rows // A_BLOCK
    kcol = OFF_AK // A_KV_W
    vcol = OFF_AV // A_KV_W

    def band(col, shift):
        return pl.BlockSpec((A_BLOCK, A_KV_W), lambda i, s: (jnp.clip(i + shift, 0, nb - 1), col))

    lctx = pc.shape[0]
    return pl.pallas_call(
        functools.partial(_attn_a_kernel, latent=latent),
        grid_spec=pltpu.PrefetchScalarGridSpec(
            num_scalar_prefetch=1,
            grid=(nb,),
            in_specs=[pl.BlockSpec((A_BLOCK, A_Q_W), lambda i, s: (i, 0)),
                      band(kcol, -1), band(kcol, 0), band(kcol, 1),
                      band(vcol, -1), band(vcol, 0), band(vcol, 1),
                      pl.BlockSpec((lctx, A_KV_W), lambda i, s: (0, kcol)),
                      pl.BlockSpec((lctx, A_KV_W), lambda i, s: (0, vcol))],
            out_specs=pl.BlockSpec((A_BLOCK, A_Q_W), lambda i, s: (i, 0)),
            scratch_shapes=[pltpu.VMEM((A_GROUP * A_BLOCK, 3 * A_BLOCK), F32)]),
        out_shape=jax.ShapeDtypeStruct((rows, A_Q_W), BF16),
        compiler_params=_params("arbitrary"),
        name="attn_a_latent" if latent else "attn_a_ctx",
    )(sink, p, p, p, p, p, p, p, pc, pc)


B_PAIR_W = 2 * HEAD_DIM
NA_GROUP_ROWS = ROW_TILE // GRID_W
NA_BIAS_TILES = 2 * NA_ROWS - 2


def _attn_b_kernel(q_ref, k_ref, v_ref, kx_ref, vx_ref, bias_ref, o_ref, *, latent, grid_rows):
    g = pl.program_id(1)
    kh = min(NA_ROWS, grid_rows)
    for rr in range(NA_GROUP_ROWS):
        rs = slice(rr * GRID_W, (rr + 1) * GRID_W)
        if latent:
            r = g * NA_GROUP_ROWS + rr
            r0 = jnp.clip(r - kh // 2, 0, grid_rows - kh)
            start = pl.multiple_of(r0 * GRID_W, GRID_W)
        for hh in range(2):
            hs = slice(hh * HEAD_DIM, (hh + 1) * HEAD_DIM)
            q = q_ref[rs, hs]
            s_parts = [_dot_nt(q, kx_ref[:, hs]) * ATTN_SCALE]
            v_parts = [vx_ref[:, hs]]
            if latent:
                kwin = k_ref[pl.ds(start, kh * GRID_W), hs]
                vwin = v_ref[pl.ds(start, kh * GRID_W), hs]
                bias = jnp.concatenate(
                    [bias_ref[hh, r0 - r + (NA_ROWS - 1) + 2 * j] for j in range(kh // 2)], axis=1)
                s_parts.append(_dot_nt(q, kwin) * ATTN_SCALE + bias)
                v_parts.append(vwin)
            out = _softmax_pv(s_parts, v_parts, None)
            o_ref[rs, hs] = out.astype(BF16)


def _na_bias_table(rpb):
    cols = np.arange(GRID_W)
    c0 = np.clip(cols - NA_COLS // 2, 0, GRID_W - NA_COLS)
    rel = cols[None, :] - cols[:, None] + NA_COLS - 1
    ok = (cols[None, :] >= c0[:, None]) & (cols[None, :] < c0[:, None] + NA_COLS)
    onehot = (rel[None] == np.arange(2 * NA_COLS - 1)[:, None, None]).astype(np.float32)
    t = jnp.einsum("hrd,dqk->hrqk", rpb, onehot, precision=lax.Precision.HIGHEST)
    t = jnp.where(ok[None, None], t, NEG_INF).astype(F32)
    return jnp.concatenate([t[:, :-1], t[:, 1:]], axis=-1)


def _attn_b(p, pc, bias_tab, latent):
    rows = p.shape[0]
    lctx = pc.shape[0]
    qcol = OFF_BQ // B_PAIR_W
    kcol = OFF_BK // B_PAIR_W
    vcol = OFF_BV // B_PAIR_W
    return pl.pallas_call(
        functools.partial(_attn_b_kernel, latent=latent, grid_rows=rows // GRID_W),
        grid=(B_HEADS // 2, rows // ROW_TILE),
        in_specs=[pl.BlockSpec((ROW_TILE, B_PAIR_W), lambda hp, g: (g, qcol + hp)),
                  pl.BlockSpec((rows, B_PAIR_W), lambda hp, g: (0, kcol + hp)),
                  pl.BlockSpec((rows, B_PAIR_W), lambda hp, g: (0, vcol + hp)),
                  pl.BlockSpec((lctx, B_PAIR_W), lambda hp, g: (0, kcol + hp)),
                  pl.BlockSpec((lctx, B_PAIR_W), lambda hp, g: (0, vcol + hp)),
                  pl.BlockSpec((2, NA_BIAS_TILES, GRID_W, 2 * GRID_W), lambda hp, g: (hp, 0, 0, 0))],
        out_specs=pl.BlockSpec((ROW_TILE, B_PAIR_W), lambda hp, g: (g, hp)),
        out_shape=jax.ShapeDtypeStruct((rows, B_W), BF16),
        compiler_params=_params("arbitrary", "arbitrary"),
        name="attn_b_latent" if latent else "attn_b_ctx",
    )(p, p, p, pc, pc, bias_tab)


def _sgu_kernel(u_ref, v_ref, g_ref, b_ref, w_ref, bs_ref, o_ref):
    for ch in range(ROW_TILE // CHUNK):
        rs = slice(ch * CHUNK, (ch + 1) * CHUNK)
        for grp in range(C_GROUPS):
            cs = slice(grp * HEAD_DIM, (grp + 1) * HEAD_DIM)
            vn = _layer_norm(v_ref[rs, cs].astype(F32), g_ref[:, cs], b_ref[:, cs])
            mixed = jnp.dot(w_ref[grp], vn.astype(BF16), preferred_element_type=F32) + bs_ref[grp]
            o_ref[rs, cs] = (u_ref[rs, cs].astype(F32) * mixed).astype(BF16)


def _sgu(p, ln_g, ln_b, w_bf, bs_b):
    rows = p.shape[0]
    ucol = OFF_C // C_W
    return pl.pallas_call(
        _sgu_kernel,
        grid=(rows // ROW_TILE,),
        in_specs=[pl.BlockSpec((ROW_TILE, C_W), lambda i: (i, ucol)),
                  pl.BlockSpec((ROW_TILE, C_W), lambda i: (i, ucol + 1)),
                  pl.BlockSpec((1, C_W), lambda i: (0, 0)),
                  pl.BlockSpec((1, C_W), lambda i: (0, 0)),
                  pl.BlockSpec((C_GROUPS, CHUNK, CHUNK), lambda i: (0, 0, 0)),
                  pl.BlockSpec((C_GROUPS, CHUNK, LANES), lambda i: (0, 0, 0))],
        out_specs=pl.BlockSpec((ROW_TILE, C_W), lambda i: (i, 0)),
        out_shape=jax.ShapeDtypeStruct((rows, C_W), BF16),
        compiler_params=_params("arbitrary"),
        name="sgu",
    )(p, p, ln_g, ln_b, w_bf, bs_b)


def _outproj_kernel(oa_ref, ob_ref, oc_ref, w_ref, x_ref, mod_ref, g_ref, b_ref, *rest):
    x1_ref, h2_ref = rest[-2], rest[-1]
    mix = jnp.dot(oa_ref[...], w_ref[0:A_Q_W, :], preferred_element_type=F32)
    mix += jnp.dot(ob_ref[...], w_ref[A_Q_W:A_Q_W + B_W, :], preferred_element_type=F32)
    mix += jnp.dot(oc_ref[...], w_ref[A_Q_W + B_W:, :], preferred_element_type=F32)
    t = DEEPNORM_ALPHA * x_ref[...] + mod_ref[MOD_G1:MOD_G1 + 1, :] * mix
    x1 = _layer_norm(t, g_ref[...], b_ref[...])
    x1_ref[...] = x1
    h2_ref[...] = x1 * (1.0 + mod_ref[MOD_SC2:MOD_SC2 + 1, :]) + mod_ref[MOD_SH2:MOD_SH2 + 1, :]


def _outproj(o_a, o_b, o_c, w_bf, x, x_off, mod, ln_g, ln_b, total_rows, out_off, prev):
    rows = o_a.shape[0]
    d = D_MODEL
    in_specs = [pl.BlockSpec((ROW_TILE, A_Q_W), lambda i: (i, 0)),
                pl.BlockSpec((ROW_TILE, B_W), lambda i: (i, 0)),
                pl.BlockSpec((ROW_TILE, C_W), lambda i: (i, 0)),
                pl.BlockSpec((d, d), lambda i: (0, 0), pipeline_mode=pl.Buffered(1)),
                pl.BlockSpec((ROW_TILE, d), lambda i: (i + x_off, 0)),
                pl.BlockSpec((8, d), lambda i: (0, 0)),
                pl.BlockSpec((1, d), lambda i: (0, 0)),
                pl.BlockSpec((1, d), lambda i: (0, 0))]
    args = [o_a, o_b, o_c, w_bf, x, mod, ln_g, ln_b]
    aliases = {}
    if prev is not None:
        in_specs += [pl.BlockSpec(memory_space=pl.ANY), pl.BlockSpec(memory_space=pl.ANY)]
        aliases = {len(args): 0, len(args) + 1: 1}
        args += list(prev)
    out_spec = pl.BlockSpec((ROW_TILE, d), lambda i: (i + out_off, 0))
    return pl.pallas_call(
        _outproj_kernel,
        grid=(rows // ROW_TILE,),
        in_specs=in_specs,
        out_specs=[out_spec, out_spec],
        out_shape=[jax.ShapeDtypeStruct((total_rows, d), F32)] * 2,
        input_output_aliases=aliases,
        compiler_params=_params("arbitrary"),
        name="outproj",
    )(*args)


def _top2_sublanes(vals, sub):
    m1 = vals.max(axis=0, keepdims=True)
    i1 = jnp.where(vals == m1, sub, vals.shape[0]).min(axis=0, keepdims=True)
    rest = jnp.where(sub == i1, -jnp.inf, vals)
    m2 = rest.max(axis=0, keepdims=True)
    i2 = jnp.where(rest == m2, sub, vals.shape[0]).min(axis=0, keepdims=True)
    return m1, i1, m2, i2


def _router_kernel(h_ref, wr_ref, rb_ref, e_ref, w_ref, rank_ref, cnt_ref, run_ref):
    i = pl.program_id(0)
    tm = h_ref.shape[0]
    epg = EXPERTS_PER_GROUP

    @pl.when(i == 0)
    def _():
        run_ref[...] = jnp.zeros_like(run_ref)

    logits = lax.dot_general(wr_ref[...], h_ref[...], (((1,), (1,)), ((), ())),
                             precision=lax.Precision.HIGHEST, preferred_element_type=F32)
    scores = _sigmoid(logits)
    biased = scores + jnp.concatenate([rb_ref[...]] * (tm // LANES), axis=1)
    sub = lax.broadcasted_iota(jnp.int32, (epg, tm), 0)

    best = None
    for g in range(N_EXPERT_GROUPS):
        m1, _, m2, _ = _top2_sublanes(biased[g * epg:(g + 1) * epg], sub)
        gs = m1 + m2
        if best is None:
            best, grp = gs, jnp.zeros((1, tm), jnp.int32)
            bsel, ssel = biased[0:epg], scores[0:epg]
        else:
            better = gs > best
            best = jnp.where(better, gs, best)
            grp = jnp.where(better, g, grp)
            bsel = jnp.where(better, biased[g * epg:(g + 1) * epg], bsel)
            ssel = jnp.where(better, scores[g * epg:(g + 1) * epg], ssel)
    _, i1, _, i2 = _top2_sublanes(bsel, sub)
    w1 = jnp.where(sub == i1, ssel, 0.0).sum(axis=0, keepdims=True)
    w2 = jnp.where(sub == i2, ssel, 0.0).sum(axis=0, keepdims=True)
    tot = w1 + w2
    e1 = grp * epg + i1
    e2 = grp * epg + i2

    eiota = lax.broadcasted_iota(jnp.int32, (N_EXPERTS, tm), 0)
    oh1 = (eiota == e1).astype(F32)
    oh2 = (eiota == e2).astype(F32)
    ohb = oh1 + oh2
    before = (lax.broadcasted_iota(jnp.int32, (tm, tm), 0) < lax.broadcasted_iota(jnp.int32, (tm, tm), 1))
    prefix = jnp.dot(ohb.astype(BF16), before.astype(BF16), preferred_element_type=F32)
    pos = run_ref[...] + prefix
    r1 = (oh1 * pos).sum(axis=0, keepdims=True)
    r2 = (oh2 * pos).sum(axis=0, keepdims=True)
    run_ref[...] = run_ref[...] + ohb.sum(axis=1, keepdims=True)

    e_ref[...] = jnp.concatenate([e1, e2], axis=0)
    w_ref[...] = jnp.concatenate([w1 / tot, w2 / tot], axis=0)
    rank_ref[...] = jnp.concatenate([r1, r2], axis=0).astype(jnp.int32)
    cnt_ref[...] = run_ref[:, 0:LANES]


def _router(h2, wr_t, rb_b):
    t = h2.shape[0]
    d = D_MODEL
    row2 = pl.BlockSpec((TOP_K, ROW_TILE), lambda i: (0, i))
    return pl.pallas_call(
        _router_kernel,
        grid=(t // ROW_TILE,),
        in_specs=[pl.BlockSpec((ROW_TILE, d), lambda i: (i, 0)),
                  pl.BlockSpec((N_EXPERTS, d), lambda i: (0, 0)),
                  pl.BlockSpec((N_EXPERTS, LANES), lambda i: (0, 0))],
        out_specs=[row2, row2, row2, pl.BlockSpec((N_EXPERTS, LANES), lambda i: (0, 0))],
        out_shape=[jax.ShapeDtypeStruct((TOP_K, t), jnp.int32),
                   jax.ShapeDtypeStruct((TOP_K, t), F32),
                   jax.ShapeDtypeStruct((TOP_K, t), jnp.int32),
                   jax.ShapeDtypeStruct((N_EXPERTS, LANES), F32)],
        scratch_shapes=[pltpu.VMEM((N_EXPERTS, ROW_TILE), F32)],
        compiler_params=_params("arbitrary"),
        name="router",
    )(h2, wr_t, rb_b)


def _row_copy(src_hbm, row, dst, sem):
    return pltpu.make_async_copy(src_hbm.at[pl.ds(row, 1), :], dst, sem)


def _experts_kernel(te_ref, nu_ref, src_ref, h_hbm, wg_ref, wu_ref, wd_ref, o_ref,
                    xbuf, sem, wg_bf, wu_bf, wd_bf):
    i = pl.program_id(0)
    n_used = nu_ref[0]
    tm = MOE_TILE

    def issue(tile, slot):
        def body(r, carry):
            tok = src_ref[tile * tm + r]
            _row_copy(h_hbm, tok, xbuf.at[slot, pl.ds(r, 1), :], sem.at[slot]).start()
            return carry
        lax.fori_loop(0, tm, body, 0)

    @pl.when(i == 0)
    def _():
        issue(0, 0)

    @pl.when(i + 1 < n_used)
    def _():
        issue(i + 1, (i + 1) % 2)

    @pl.when(i < n_used)
    def _():
        slot = i % 2
        pltpu.make_async_copy(h_hbm.at[pl.ds(0, tm), :], xbuf.at[slot], sem.at[slot]).wait()

        @pl.when((i == 0) | (te_ref[i] != te_ref[jnp.maximum(i - 1, 0)]))
        def _():
            wg_bf[...] = wg_ref[...].astype(BF16)
            wu_bf[...] = wu_ref[...].astype(BF16)
            wd_bf[...] = wd_ref[...].astype(BF16)

        x = xbuf[slot].astype(BF16)
        gate = jnp.dot(x, wg_bf[...], preferred_element_type=F32)
        up = jnp.dot(x, wu_bf[...], preferred_element_type=F32)
        hid = (gate * _sigmoid(gate)) * up
        o_ref[...] = jnp.dot(hid.astype(BF16), wd_bf[...], preferred_element_type=F32)


def _experts(h2, tile_expert, n_used, src_tok, w_gate, w_up, w_down, layer):
    d = D_MODEL
    max_tiles = tile_expert.shape[0]
    tm = MOE_TILE
    return pl.pallas_call(
        _experts_kernel,
        grid_spec=pltpu.PrefetchScalarGridSpec(
            num_scalar_prefetch=3,
            grid=(max_tiles,),
            in_specs=[pl.BlockSpec(memory_space=pl.ANY),
                      pl.BlockSpec((None, None, d, EXPERT_FF), lambda i, te, nu, src: (layer, te[i], 0, 0)),
                      pl.BlockSpec((None, None, d, EXPERT_FF), lambda i, te, nu, src: (layer, te[i], 0, 0)),
                      pl.BlockSpec((None, None, EXPERT_FF, d), lambda i, te, nu, src: (layer, te[i], 0, 0))],
            out_specs=pl.BlockSpec((tm, d), lambda i, te, nu, src: (jnp.where(i < nu[0], i, max_tiles), 0)),
            scratch_shapes=[pltpu.VMEM((2, tm, d), F32),
                            pltpu.SemaphoreType.DMA((2,)),
                            pltpu.VMEM((d, EXPERT_FF), BF16),
                            pltpu.VMEM((d, EXPERT_FF), BF16),
                            pltpu.VMEM((EXPERT_FF, d), BF16)]),
        out_shape=jax.ShapeDtypeStruct(((max_tiles + 1) * tm, d), F32),
        compiler_params=_params("arbitrary"),
        name="experts",
    )(tile_expert, n_used, src_tok, h2, w_gate, w_up, w_down)


def _combine_kernel(slot_ref, ys_hbm, w_ref, x_ref, mod_ref, g_ref, b_ref, o_ref, ybuf, sem, *, n_tok):
    i = pl.program_id(0)
    nt = pl.num_programs(0)
    tm = ROW_TILE

    def issue(blk, buf):
        def body(r, carry):
            for k in range(TOP_K):
                row = slot_ref[k * n_tok + blk * tm + r]
                _row_copy(ys_hbm, row, ybuf.at[buf, k, pl.ds(r, 1), :], sem.at[buf]).start()
            return carry
        lax.fori_loop(0, tm, body, 0)

    @pl.when(i == 0)
    def _():
        issue(0, 0)

    @pl.when(i + 1 < nt)
    def _():
        issue(i + 1, (i + 1) % 2)

    buf = i % 2
    for k in range(TOP_K):
        pltpu.make_async_copy(ys_hbm.at[pl.ds(0, tm), :], ybuf.at[buf, k], sem.at[buf]).wait()
    y = w_ref[:, 0:1] * ybuf[buf, 0] + w_ref[:, 1:2] * ybuf[buf, 1]
    t = DEEPNORM_ALPHA * x_ref[...] + mod_ref[MOD_G2:MOD_G2 + 1, :] * y
    o_ref[...] = _layer_norm(t, g_ref[...], b_ref[...])


def _combine(ys, slots, w_tok, x1, mods, n_ctx_blocks, ln_g, ln_b):
    t = x1.shape[0]
    d = D_MODEL
    return pl.pallas_call(
        functools.partial(_combine_kernel, n_tok=t),
        grid_spec=pltpu.PrefetchScalarGridSpec(
            num_scalar_prefetch=1,
            grid=(t // ROW_TILE,),
            in_specs=[pl.BlockSpec(memory_space=pl.ANY),
                      pl.BlockSpec((ROW_TILE, TOP_K), lambda i, s: (i, 0)),
                      pl.BlockSpec((ROW_TILE, d), lambda i, s: (i, 0)),
                      pl.BlockSpec((None, 8, d), lambda i, s: (jnp.where(i < n_ctx_blocks, 0, 1), 0, 0)),
                      pl.BlockSpec((1, d), lambda i, s: (0, 0)),
                      pl.BlockSpec((1, d), lambda i, s: (0, 0))],
            out_specs=pl.BlockSpec((ROW_TILE, d), lambda i, s: (i, 0)),
            scratch_shapes=[pltpu.VMEM((2, TOP_K, ROW_TILE, d), F32),
                            pltpu.SemaphoreType.DMA((2,))]),
        out_shape=jax.ShapeDtypeStruct((t, d), F32),
        compiler_params=_params("arbitrary"),
        name="combine",
    )(slots, ys, w_tok, x1, mods, ln_g, ln_b)


def _moe(h2, x1, mods, n_ctx_blocks, wr_t, rb_b, w_gate, w_up, w_down, layer, ln_g, ln_b):
    t = h2.shape[0]
    tm = MOE_TILE
    e_idx, w_tok, rank, cnt = _router(h2, wr_t, rb_b)
    counts = cnt[:, 0].astype(jnp.int32)
    tiles_per = (counts + tm - 1) // tm
    tile_end = jnp.cumsum(tiles_per)
    n_used = tile_end[-1]
    row_off = (tile_end - tiles_per) * tm
    experts = jnp.arange(N_EXPERTS, dtype=jnp.int32)
    slots = jnp.sum(jnp.where(e_idx[:, :, None] == experts, row_off, 0), axis=-1) + rank
    max_tiles = (TOP_K * t + N_EXPERTS * (tm - 1)) // tm + 1
    tile_ids = jnp.minimum(jnp.arange(max_tiles, dtype=jnp.int32), n_used - 1)
    tile_expert = jnp.sum((tile_end[None, :] <= tile_ids[:, None]).astype(jnp.int32), axis=1)
    tok = jnp.tile(jnp.arange(t, dtype=jnp.int32), TOP_K)
    src_tok = jnp.zeros((max_tiles * tm,), jnp.int32).at[slots.reshape(-1)].set(tok)
    ys = _experts(h2, tile_expert, n_used.reshape(1).astype(jnp.int32), src_tok, w_gate, w_up, w_down, layer)
    return _combine(ys, slots.reshape(-1), w_tok.T, x1, mods, n_ctx_blocks, ln_g, ln_b)


def _rope_tables(n):
    t = jnp.arange(n)
    row = (t // GRID_W).astype(F32)
    col = (t % GRID_W).astype(F32)
    n_freq = HEAD_DIM // 4
    inv_freq = ROPE_BASE ** (-jnp.arange(n_freq, dtype=F32) / n_freq)
    ang = jnp.concatenate([row[:, None] * inv_freq, col[:, None] * inv_freq], axis=-1)
    cos, sin = jnp.cos(ang), jnp.sin(ang)
    return jnp.concatenate([cos, cos], axis=-1), jnp.concatenate([-sin, sin], axis=-1)


def kernel(x, c, ctx, c_ctx, w_mod, b_mod, w_in, attn_sink, na_rpb, sgu_ln_g, sgu_ln_b, sgu_w, sgu_b,
           w_out, ln1_g, ln1_b, w_router, router_bias, w_gate, w_up, w_down, ln2_g, ln2_b):
    batch, n, d = x.shape
    lctx = ctx.shape[1]
    assert batch == 1 and d == D_MODEL and n % ROW_TILE == 0 and lctx % ROW_TILE == 0
    n_ctx_blocks = lctx // ROW_TILE

    mods = _modulation(c, c_ctx, w_mod, b_mod).reshape(DEPTH, 8, 6, d)
    mod_lat = jnp.pad(mods[:, 0], ((0, 0), (0, 2), (0, 0)))
    mod_ctx = jnp.pad(mods[:, 1], ((0, 0), (0, 2), (0, 0)))
    cos, sin = _rope_tables(n)
    wr_t = w_router.T
    rb_b = jnp.broadcast_to(router_bias.reshape(N_EXPERTS, 1), (N_EXPERTS, LANES))

    x_lat, lat_off = x[0], 0
    x_ctx = ctx[0]
    for l in range(DEPTH):
        last = l == DEPTH - 1
        w_in_bf = w_in[l].astype(BF16)
        w_out_bf = w_out[l].astype(BF16)
        sgu_w_bf = sgu_w[l].astype(BF16)
        sgu_b_b = jnp.broadcast_to(sgu_b[l][:, :, None], (C_GROUPS, CHUNK, LANES))
        ln_g_c, ln_b_c = sgu_ln_g[l].reshape(1, C_W), sgu_ln_b[l].reshape(1, C_W)
        bias_tab = _na_bias_table(na_rpb[l])
        g1, b1 = ln1_g[l].reshape(1, d), ln1_b[l].reshape(1, d)
        g2, b2 = ln2_g[l].reshape(1, d), ln2_b[l].reshape(1, d)

        p = _proj(x_lat, lat_off, n, mod_lat[l], cos, sin, w_in_bf, rope=True)
        pc = _proj(x_ctx, 0, lctx, mod_ctx[l], cos, sin, w_in_bf, rope=False)
        o_a = _attn_a(p, pc, attn_sink[l], latent=True)
        o_b = _attn_b(p, pc, bias_tab, latent=True)
        o_c = _sgu(p, ln_g_c, ln_b_c, sgu_w_bf, sgu_b_b)
        if last:
            x1, h2 = _outproj(o_a, o_b, o_c, w_out_bf, x_lat, lat_off, mod_lat[l], g1, b1, n, 0, None)
            x_lat = _moe(h2, x1, jnp.stack([mod_ctx[l], mod_lat[l]]), 0, wr_t, rb_b,
                         w_gate, w_up, w_down, l, g2, b2)
        else:
            oc_a = _attn_a(pc, pc, attn_sink[l], latent=False)
            oc_b = _attn_b(pc, pc, bias_tab, latent=False)
            oc_c = _sgu(pc, ln_g_c, ln_b_c, sgu_w_bf, sgu_b_b)
            total = lctx + n
            prev = _outproj(oc_a, oc_b, oc_c, w_out_bf, x_ctx, 0, mod_ctx[l], g1, b1, total, 0, None)
            x1, h2 = _outproj(o_a, o_b, o_c, w_out_bf, x_lat, lat_off, mod_lat[l], g1, b1, total,
                              n_ctx_blocks, prev)
            x_all = _moe(h2, x1, jnp.stack([mod_ctx[l], mod_lat[l]]), n_ctx_blocks, wr_t, rb_b,
                         w_gate, w_up, w_down, l, g2, b2)
            x_lat, lat_off, x_ctx = x_all, n_ctx_blocks, x_all
    return x_lat.reshape(batch, n, d)
```

```python
import functools

import numpy as np
import jax
import jax.numpy as jnp
from jax import lax
from jax.experimental import pallas as pl
from jax.experimental.pallas import tpu as pltpu

F32 = jnp.float32
BF16 = jnp.bfloat16

D_MODEL = 2048
DEPTH = 2
GRID_W = 64
HEAD_DIM = 128
A_HEADS = 6
A_KV_HEADS = 2
A_GROUP = A_HEADS // A_KV_HEADS
A_BLOCK = 128
B_HEADS = 6
NA_ROWS = 8
NA_COLS = 16
C_GROUPS = 4
C_W = C_GROUPS * HEAD_DIM
CHUNK = 128
N_EXPERTS = 32
N_EXPERT_GROUPS = 4
EXPERTS_PER_GROUP = N_EXPERTS // N_EXPERT_GROUPS
TOP_K = 2
EXPERT_FF = 512
ROPE_BASE = 10000.0
LN_EPS = 1e-5
NEG_INF = -1e30
DEEPNORM_ALPHA = (2 * DEPTH) ** 0.25
ATTN_SCALE = HEAD_DIM ** -0.5

A_Q_W = A_HEADS * HEAD_DIM
A_KV_W = A_KV_HEADS * HEAD_DIM
B_W = B_HEADS * HEAD_DIM
OFF_AK = A_Q_W
OFF_AV = OFF_AK + A_KV_W
OFF_BQ = OFF_AV + A_KV_W
OFF_BK = OFF_BQ + B_W
OFF_BV = OFF_BK + B_W
OFF_C = OFF_BV + B_W
IN_COLS = OFF_C + 2 * C_W

VMEM_LIMIT_BYTES = 56 * 1024 * 1024
LANES = 128

ROW_TILE = 256
PROJ_COL_TILE = 512
MOD_COL_TILE = 1024
MOE_TILE = 256

MOD_SH1, MOD_SC1, MOD_G1, MOD_SH2, MOD_SC2, MOD_G2 = range(6)


def _params(*sem):
    return pltpu.CompilerParams(dimension_semantics=sem, vmem_limit_bytes=VMEM_LIMIT_BYTES)


def _layer_norm(t, g, b):
    mu = jnp.mean(t, axis=-1, keepdims=True)
    d = t - mu
    var = jnp.mean(d * d, axis=-1, keepdims=True)
    return d * lax.rsqrt(var + LN_EPS) * g + b


def _sigmoid(v):
    return 1.0 / (1.0 + jnp.exp(-v))


def _dot_nt(a, b):
    return lax.dot_general(a, b, (((1,), (1,)), ((), ())), preferred_element_type=F32)


def _mod_kernel(c_ref, w_ref, b_ref, o_ref):
    w = w_ref[...]
    reps = w.shape[1] // LANES
    rows = []
    for r in range(2):
        cv = c_ref[r]
        s = cv * _sigmoid(cv)
        sb = jnp.concatenate([s] * reps, axis=1)
        rows.append(jnp.sum(w * sb, axis=0, keepdims=True) + b_ref[...])
    rows.append(jnp.zeros((6, w.shape[1]), F32))
    o_ref[...] = jnp.concatenate(rows, axis=0)


def _modulation(c, c_ctx, w_mod, b_mod):
    d = D_MODEL
    cb = jnp.stack([jnp.broadcast_to(c.reshape(d, 1), (d, LANES)),
                    jnp.broadcast_to(c_ctx.reshape(d, 1), (d, LANES))])
    n_out = 6 * d
    return pl.pallas_call(
        _mod_kernel,
        grid=(DEPTH, n_out // MOD_COL_TILE),
        in_specs=[pl.BlockSpec((2, d, LANES), lambda l, j: (0, 0, 0)),
                  pl.BlockSpec((None, d, MOD_COL_TILE), lambda l, j: (l, 0, j)),
                  pl.BlockSpec((None, 1, MOD_COL_TILE), lambda l, j: (l, 0, j))],
        out_specs=pl.BlockSpec((None, 8, MOD_COL_TILE), lambda l, j: (l, 0, j)),
        out_shape=jax.ShapeDtypeStruct((DEPTH, 8, n_out), F32),
        compiler_params=_params("arbitrary", "arbitrary"),
        name="modulation",
    )(cb, w_mod, b_mod.reshape(DEPTH, 1, n_out))


def _gelu_tanh(v):
    return 0.5 * v * (1.0 + jnp.tanh(np.sqrt(2.0 / np.pi).astype(np.float32) * (v + 0.044715 * (v * v * v))))


def _proj_kernel(x_ref, mod_ref, cos_ref, sin_ref, w_ref, o_ref, *, rope):
    x = x_ref[...]
    h = (x * (1.0 + mod_ref[MOD_SC1:MOD_SC1 + 1, :]) + mod_ref[MOD_SH1:MOD_SH1 + 1, :]).astype(BF16)
    tn = PROJ_COL_TILE
    for j in range(IN_COLS // tn):
        c0 = j * tn
        acc = jnp.dot(h, w_ref[:, c0:c0 + tn], preferred_element_type=F32)
        if c0 < OFF_AV:
            if rope:
                cos = cos_ref[...]
                sin = sin_ref[...]
                parts = []
                for hh in range(tn // HEAD_DIM):
                    a = acc[:, hh * HEAD_DIM:(hh + 1) * HEAD_DIM]
                    parts.append(a * cos + pltpu.roll(a, HEAD_DIM // 2, 1) * sin)
                acc = jnp.concatenate(parts, axis=1)
        elif c0 >= OFF_C:
            acc = _gelu_tanh(acc)
        o_ref[:, c0:c0 + tn] = acc.astype(BF16)


def _proj(x, x_off, rows, mod, cos, sin, w_bf, rope):
    d = D_MODEL
    return pl.pallas_call(
        functools.partial(_proj_kernel, rope=rope),
        grid=(rows // ROW_TILE,),
        in_specs=[pl.BlockSpec((ROW_TILE, d), lambda i: (i + x_off, 0)),
                  pl.BlockSpec((8, d), lambda i: (0, 0)),
                  pl.BlockSpec((ROW_TILE, HEAD_DIM), lambda i: (i, 0)),
                  pl.BlockSpec((ROW_TILE, HEAD_DIM), lambda i: (i, 0)),
                  pl.BlockSpec((d, IN_COLS), lambda i: (0, 0), pipeline_mode=pl.Buffered(1))],
        out_specs=pl.BlockSpec((ROW_TILE, IN_COLS), lambda i: (i, 0)),
        out_shape=jax.ShapeDtypeStruct((rows, IN_COLS), BF16),
        compiler_params=_params("arbitrary"),
        name="proj_rope" if rope else "proj_ctx",
    )(x, mod, cos, sin, w_bf)


def _softmax_pv(s_parts, v_parts, sink):
    m = s_parts[0].max(axis=-1, keepdims=True)
    for s in s_parts[1:]:
        m = jnp.maximum(m, s.max(axis=-1, keepdims=True))
    if sink is not None:
        m = jnp.maximum(m, sink)
    denom = None if sink is None else jnp.exp(sink - m)
    out = None
    for s, v in zip(s_parts, v_parts):
        e = jnp.exp(s - m)
        es = e.sum(axis=-1, keepdims=True)
        denom = es if denom is None else denom + es
        pv = jnp.dot(e.astype(BF16), v, preferred_element_type=F32)
        out = pv if out is None else out + pv
    return out / denom


def _attn_a_kernel(sink_ref, q_ref, kp_ref, kc_ref, kn_ref, vp_ref, vc_ref, vn_ref, kx_ref, vx_ref, o_ref,
                   mask_ref, *, latent):
    i = pl.program_id(0)
    nb = pl.num_programs(0)
    nq = A_GROUP * A_BLOCK

    if latent:
        @pl.when(i == 0)
        def _():
            qi = lax.broadcasted_iota(jnp.int32, (nq, 3 * A_BLOCK), 0) % A_BLOCK
            jj = lax.broadcasted_iota(jnp.int32, (nq, 3 * A_BLOCK), 1)
            ok = (jj >= qi) & (jj <= qi + 2 * A_BLOCK)
            mask_ref[...] = jnp.where(ok, 0.0, NEG_INF).astype(F32)

        col = lax.broadcasted_iota(jnp.int32, (nq, 3 * A_BLOCK), 1)
        off_band = ((col < A_BLOCK) & (i == 0)) | ((col >= 2 * A_BLOCK) & (i == nb - 1))

    for kh in range(A_KV_HEADS):
        hs = [kh * A_GROUP + g for g in range(A_GROUP)]
        q = jnp.concatenate([q_ref[:, h * HEAD_DIM:(h + 1) * HEAD_DIM] for h in hs], axis=0)
        sink = jnp.concatenate([jnp.full((A_BLOCK, 1), sink_ref[h], F32) for h in hs], axis=0)
        ks = slice(kh * HEAD_DIM, (kh + 1) * HEAD_DIM)
        s_parts = [_dot_nt(q, kx_ref[:, ks]) * ATTN_SCALE]
        v_parts = [vx_ref[:, ks]]
        if latent:
            kband = jnp.concatenate([kp_ref[:, ks], kc_ref[:, ks], kn_ref[:, ks]], axis=0)
            vband = jnp.concatenate([vp_ref[:, ks], vc_ref[:, ks], vn_ref[:, ks]], axis=0)
            s_loc = _dot_nt(q, kband) * ATTN_SCALE + mask_ref[...]
            s_parts.append(jnp.where(off_band, NEG_INF, s_loc))
            v_parts.append(vband)
        out = _softmax_pv(s_parts, v_parts, sink)
        for g, h in enumerate(hs):
            o_ref[:, h * HEAD_DIM:(h + 1) * HEAD_DIM] = out[g * A_BLOCK:(g + 1) * A_BLOCK].astype(BF16)


def _attn_a(p, pc, sink, latent):
    rows = p.shape[0]
    nb = rows // A_BLOCK
    kcol = OFF_AK // A_KV_W
    vcol = OFF_AV // A_KV_W

    def band(col, shift):
        return pl.BlockSpec((A_BLOCK, A_KV_W), lambda i, s: (jnp.clip(i + shift, 0, nb - 1), col))

    lctx = pc.shape[0]
    return pl.pallas_call(
        functools.partial(_attn_a_kernel, latent=latent),
        grid_spec=pltpu.PrefetchScalarGridSpec(
            num_scalar_prefetch=1,
            grid=(nb,),
            in_specs=[pl.BlockSpec((A_BLOCK, A_Q_W), lambda i, s: (i, 0)),
                      band(kcol, -1), band(kcol, 0), band(kcol, 1),
                      band(vcol, -1), band(vcol, 0), band(vcol, 1),
                      pl.BlockSpec((lctx, A_KV_W), lambda i, s: (0, kcol)),
                      pl.BlockSpec((lctx, A_KV_W), lambda i, s: (0, vcol))],
            out_specs=pl.BlockSpec((A_BLOCK, A_Q_W), lambda i, s: (i, 0)),
            scratch_shapes=[pltpu.VMEM((A_GROUP * A_BLOCK, 3 * A_BLOCK), F32)]),
        out_shape=jax.ShapeDtypeStruct((rows, A_Q_W), BF16),
        compiler_params=_params("arbitrary"),
        name="attn_a_latent" if latent else "attn_a_ctx",
    )(sink, p, p, p, p, p, p, p, pc, pc)


B_PAIR_W = 2 * HEAD_DIM
NA_GROUP_ROWS = ROW_TILE // GRID_W
NA_BIAS_TILES = 2 * NA_ROWS - 2


def _attn_b_kernel(q_ref, k_ref, v_ref, kx_ref, vx_ref, bias_ref, o_ref, *, latent, grid_rows):
    g = pl.program_id(1)
    kh = min(NA_ROWS, grid_rows)
    for rr in range(NA_GROUP_ROWS):
        rs = slice(rr * GRID_W, (rr + 1) * GRID_W)
        if latent:
            r = g * NA_GROUP_ROWS + rr
            r0 = jnp.clip(r - kh // 2, 0, grid_rows - kh)
            start = pl.multiple_of(r0 * GRID_W, GRID_W)
        for hh in range(2):
            hs = slice(hh * HEAD_DIM, (hh + 1) * HEAD_DIM)
            q = q_ref[rs, hs]
            s_parts = [_dot_nt(q, kx_ref[:, hs]) * ATTN_SCALE]
            v_parts = [vx_ref[:, hs]]
            if latent:
                kwin = k_ref[pl.ds(start, kh * GRID_W), hs]
                vwin = v_ref[pl.ds(start, kh * GRID_W), hs]
                bias = jnp.concatenate(
                    [bias_ref[hh, r0 - r + (NA_ROWS - 1) + 2 * j] for j in range(kh // 2)], axis=1)
                s_parts.append(_dot_nt(q, kwin) * ATTN_SCALE + bias)
                v_parts.append(vwin)
            out = _softmax_pv(s_parts, v_parts, None)
            o_ref[rs, hs] = out.astype(BF16)


def _na_bias_table(rpb):
    cols = np.arange(GRID_W)
    c0 = np.clip(cols - NA_COLS // 2, 0, GRID_W - NA_COLS)
    rel = cols[None, :] - cols[:, None] + NA_COLS - 1
    ok = (cols[None, :] >= c0[:, None]) & (cols[None, :] < c0[:, None] + NA_COLS)
    onehot = (rel[None] == np.arange(2 * NA_COLS - 1)[:, None, None]).astype(np.float32)
    t = jnp.einsum("hrd,dqk->hrqk", rpb, onehot, precision=lax.Precision.HIGHEST)
    t = jnp.where(ok[None, None], t, NEG_INF).astype(F32)
    return jnp.concatenate([t[:, :-1], t[:, 1:]], axis=-1)


def _attn_b(p, pc, bias_tab, latent):
    rows = p.shape[0]
    lctx = pc.shape[0]
    qcol = OFF_BQ // B_PAIR_W
    kcol = OFF_BK // B_PAIR_W
    vcol = OFF_BV // B_PAIR_W
    return pl.pallas_call(
        functools.partial(_attn_b_kernel, latent=latent, grid_rows=rows // GRID_W),
        grid=(B_HEADS // 2, rows // ROW_TILE),
        in_specs=[pl.BlockSpec((ROW_TILE, B_PAIR_W), lambda hp, g: (g, qcol + hp)),
                  pl.BlockSpec((rows, B_PAIR_W), lambda hp, g: (0, kcol + hp)),
                  pl.BlockSpec((rows, B_PAIR_W), lambda hp, g: (0, vcol + hp)),
                  pl.BlockSpec((lctx, B_PAIR_W), lambda hp, g: (0, kcol + hp)),
                  pl.BlockSpec((lctx, B_PAIR_W), lambda hp, g: (0, vcol + hp)),
                  pl.BlockSpec((2, NA_BIAS_TILES, GRID_W, 2 * GRID_W), lambda hp, g: (hp, 0, 0, 0))],
        out_specs=pl.BlockSpec((ROW_TILE, B_PAIR_W), lambda hp, g: (g, hp)),
        out_shape=jax.ShapeDtypeStruct((rows, B_W), BF16),
        compiler_params=_params("arbitrary", "arbitrary"),
        name="attn_b_latent" if latent else "attn_b_ctx",
    )(p, p, p, pc, pc, bias_tab)


def _sgu_kernel(u_ref, v_ref, g_ref, b_ref, w_ref, bs_ref, o_ref):
    for ch in range(ROW_TILE // CHUNK):
        rs = slice(ch * CHUNK, (ch + 1) * CHUNK)
        for grp in range(C_GROUPS):
            cs = slice(grp * HEAD_DIM, (grp + 1) * HEAD_DIM)
            vn = _layer_norm(v_ref[rs, cs].astype(F32), g_ref[:, cs], b_ref[:, cs])
            mixed = jnp.dot(w_ref[grp], vn.astype(BF16), preferred_element_type=F32) + bs_ref[grp]
            o_ref[rs, cs] = (u_ref[rs, cs].astype(F32) * mixed).astype(BF16)


def _sgu(p, ln_g, ln_b, w_bf, bs_b):
    rows = p.shape[0]
    ucol = OFF_C // C_W
    return pl.pallas_call(
        _sgu_kernel,
        grid=(rows // ROW_TILE,),
        in_specs=[pl.BlockSpec((ROW_TILE, C_W), lambda i: (i, ucol)),
                  pl.BlockSpec((ROW_TILE, C_W), lambda i: (i, ucol + 1)),
                  pl.BlockSpec((1, C_W), lambda i: (0, 0)),
                  pl.BlockSpec((1, C_W), lambda i: (0, 0)),
                  pl.BlockSpec((C_GROUPS, CHUNK, CHUNK), lambda i: (0, 0, 0)),
                  pl.BlockSpec((C_GROUPS, CHUNK, LANES), lambda i: (0, 0, 0))],
        out_specs=pl.BlockSpec((ROW_TILE, C_W), lambda i: (i, 0)),
        out_shape=jax.ShapeDtypeStruct((rows, C_W), BF16),
        compiler_params=_params("arbitrary"),
        name="sgu",
    )(p, p, ln_g, ln_b, w_bf, bs_b)


def _outproj_kernel(oa_ref, ob_ref, oc_ref, w_ref, x_ref, mod_ref, g_ref, b_ref, *rest):
    x1_ref, h2_ref = rest[-2], rest[-1]
    mix = jnp.dot(oa_ref[...], w_ref[0:A_Q_W, :], preferred_element_type=F32)
    mix += jnp.dot(ob_ref[...], w_ref[A_Q_W:A_Q_W + B_W, :], preferred_element_type=F32)
    mix += jnp.dot(oc_ref[...], w_ref[A_Q_W + B_W:, :], preferred_element_type=F32)
    t = DEEPNORM_ALPHA * x_ref[...] + mod_ref[MOD_G1:MOD_G1 + 1, :] * mix
    x1 = _layer_norm(t, g_ref[...], b_ref[...])
    x1_ref[...] = x1
    h2_ref[...] = x1 * (1.0 + mod_ref[MOD_SC2:MOD_SC2 + 1, :]) + mod_ref[MOD_SH2:MOD_SH2 + 1, :]


def _outproj(o_a, o_b, o_c, w_bf, x, x_off, mod, ln_g, ln_b, total_rows, out_off, prev):
    rows = o_a.shape[0]
    d = D_MODEL
    in_specs = [pl.BlockSpec((ROW_TILE, A_Q_W), lambda i: (i, 0)),
                pl.BlockSpec((ROW_TILE, B_W), lambda i: (i, 0)),
                pl.BlockSpec((ROW_TILE, C_W), lambda i: (i, 0)),
                pl.BlockSpec((d, d), lambda i: (0, 0), pipeline_mode=pl.Buffered(1)),
                pl.BlockSpec((ROW_TILE, d), lambda i: (i + x_off, 0)),
                pl.BlockSpec((8, d), lambda i: (0, 0)),
                pl.BlockSpec((1, d), lambda i: (0, 0)),
                pl.BlockSpec((1, d), lambda i: (0, 0))]
    args = [o_a, o_b, o_c, w_bf, x, mod, ln_g, ln_b]
    aliases = {}
    if prev is not None:
        in_specs += [pl.BlockSpec(memory_space=pl.ANY), pl.BlockSpec(memory_space=pl.ANY)]
        aliases = {len(args): 0, len(args) + 1: 1}
        args += list(prev)
    out_spec = pl.BlockSpec((ROW_TILE, d), lambda i: (i + out_off, 0))
    return pl.pallas_call(
        _outproj_kernel,
        grid=(rows // ROW_TILE,),
        in_specs=in_specs,
        out_specs=[out_spec, out_spec],
        out_shape=[jax.ShapeDtypeStruct((total_rows, d), F32)] * 2,
        input_output_aliases=aliases,
        compiler_params=_params("arbitrary"),
        name="outproj",
    )(*args)


def _top2_sublanes(vals, sub):
    m1 = vals.max(axis=0, keepdims=True)
    i1 = jnp.where(vals == m1, sub, vals.shape[0]).min(axis=0, keepdims=True)
    rest = jnp.where(sub == i1, -jnp.inf, vals)
    m2 = rest.max(axis=0, keepdims=True)
    i2 = jnp.where(rest == m2, sub, vals.shape[0]).min(axis=0, keepdims=True)
    return m1, i1, m2, i2


def _router_kernel(h_ref, wr_ref, rb_ref, e_ref, w_ref, rank_ref, cnt_ref, run_ref):
    i = pl.program_id(0)
    tm = h_ref.shape[0]
    epg = EXPERTS_PER_GROUP

    @pl.when(i == 0)
    def _():
        run_ref[...] = jnp.zeros_like(run_ref)

    logits = lax.dot_general(wr_ref[...], h_ref[...], (((1,), (1,)), ((), ())),
                             precision=lax.Precision.HIGHEST, preferred_element_type=F32)
    scores = _sigmoid(logits)
    biased = scores + jnp.concatenate([rb_ref[...]] * (tm // LANES), axis=1)
    sub = lax.broadcasted_iota(jnp.int32, (epg, tm), 0)

    best = None
    for g in range(N_EXPERT_GROUPS):
        m1, _, m2, _ = _top2_sublanes(biased[g * epg:(g + 1) * epg], sub)
        gs = m1 + m2
        if best is None:
            best, grp = gs, jnp.zeros((1, tm), jnp.int32)
            bsel, ssel = biased[0:epg], scores[0:epg]
        else:
            better = gs > best
            best = jnp.where(better, gs, best)
            grp = jnp.where(better, g, grp)
            bsel = jnp.where(better, biased[g * epg:(g + 1) * epg], bsel)
            ssel = jnp.where(better, scores[g * epg:(g + 1) * epg], ssel)
    _, i1, _, i2 = _top2_sublanes(bsel, sub)
    w1 = jnp.where(sub == i1, ssel, 0.0).sum(axis=0, keepdims=True)
    w2 = jnp.where(sub == i2, ssel, 0.0).sum(axis=0, keepdims=True)
    tot = w1 + w2
    e1 = grp * epg + i1
    e2 = grp * epg + i2

    eiota = lax.broadcasted_iota(jnp.int32, (N_EXPERTS, tm), 0)
    oh1 = (eiota == e1).astype(F32)
    oh2 = (eiota == e2).astype(F32)
    ohb = oh1 + oh2
    before = (lax.broadcasted_iota(jnp.int32, (tm, tm), 0) < lax.broadcasted_iota(jnp.int32, (tm, tm), 1))
    prefix = jnp.dot(ohb.astype(BF16), before.astype(BF16), preferred_element_type=F32)
    pos = run_ref[...] + prefix
    r1 = (oh1 * pos).sum(axis=0, keepdims=True)
    r2 = (oh2 * pos).sum(axis=0, keepdims=True)
    run_ref[...] = run_ref[...] + ohb.sum(axis=1, keepdims=True)

    e_ref[...] = jnp.concatenate([e1, e2], axis=0)
    w_ref[...] = jnp.concatenate([w1 / tot, w2 / tot], axis=0)
    rank_ref[...] = jnp.concatenate([r1, r2], axis=0).astype(jnp.int32)
    cnt_ref[...] = run_ref[:, 0:LANES]


def _router(h2, wr_t, rb_b):
    t = h2.shape[0]
    d = D_MODEL
    row2 = pl.BlockSpec((TOP_K, ROW_TILE), lambda i: (0, i))
    return pl.pallas_call(
        _router_kernel,
        grid=(t // ROW_TILE,),
        in_specs=[pl.BlockSpec((ROW_TILE, d), lambda i: (i, 0)),
                  pl.BlockSpec((N_EXPERTS, d), lambda i: (0, 0)),
                  pl.BlockSpec((N_EXPERTS, LANES), lambda i: (0, 0))],
        out_specs=[row2, row2, row2, pl.BlockSpec((N_EXPERTS, LANES), lambda i: (0, 0))],
        out_shape=[jax.ShapeDtypeStruct((TOP_K, t), jnp.int32),
                   jax.ShapeDtypeStruct((TOP_K, t), F32),
                   jax.ShapeDtypeStruct((TOP_K, t), jnp.int32),
                   jax.ShapeDtypeStruct((N_EXPERTS, LANES), F32)],
        scratch_shapes=[pltpu.VMEM((N_EXPERTS, ROW_TILE), F32)],
        compiler_params=_params("arbitrary"),
        name="router",
    )(h2, wr_t, rb_b)


def _row_copy(src_hbm, row, dst, sem):
    return pltpu.make_async_copy(src_hbm.at[pl.ds(row, 1), :], dst, sem)


def _experts_kernel(ts_ref, nu_ref, src_ref, h_hbm, wg_ref, wu_ref, wd_ref, ys_hbm,
                    xbuf, ybuf, gsem, ysem, wg_bf, wu_bf, wd_bf):
    e = pl.program_id(0)
    n_used = nu_ref[0]
    tm = MOE_TILE

    def gather(tile, slot):
        base = tile * tm
        for r in range(tm):
            _row_copy(h_hbm, src_ref[base + r], xbuf.at[slot, pl.ds(r, 1), :], gsem.at[slot]).start()

    def wait_gather(slot):
        pltpu.make_async_copy(h_hbm.at[pl.ds(0, tm), :], xbuf.at[slot], gsem.at[slot]).wait()

    def y_store(tile, slot):
        row0 = pl.multiple_of(tile * tm, tm)
        return pltpu.make_async_copy(ybuf.at[slot], ys_hbm.at[pl.ds(row0, tm), :], ysem.at[slot])

    @pl.when((e == 0) & (n_used > 0))
    def _():
        gather(0, 0)

    @pl.when(ts_ref[e + 1] > ts_ref[e])
    def _():
        wg_bf[...] = wg_ref[...].astype(BF16)
        wu_bf[...] = wu_ref[...].astype(BF16)
        wd_bf[...] = wd_ref[...].astype(BF16)

    def tile_body(g, carry):
        slot = g % 2

        @pl.when(g >= 2)
        def _():
            y_store(g - 2, slot).wait()

        wait_gather(slot)
        gather(jnp.minimum(g + 1, n_used - 1), 1 - slot)
        x = xbuf[slot].astype(BF16)
        gate = jnp.dot(x, wg_bf[...], preferred_element_type=F32)
        up = jnp.dot(x, wu_bf[...], preferred_element_type=F32)
        hid = (gate * _sigmoid(gate)) * up
        ybuf[slot] = jnp.dot(hid.astype(BF16), wd_bf[...], preferred_element_type=F32)
        y_store(g, slot).start()
        return carry

    lax.fori_loop(ts_ref[e], ts_ref[e + 1], tile_body, 0)

    @pl.when((e == pl.num_programs(0) - 1) & (n_used > 0))
    def _():
        wait_gather(n_used % 2)
        y_store(n_used - 1, (n_used - 1) % 2).wait()

        @pl.when(n_used >= 2)
        def _():
            y_store(n_used - 2, n_used % 2).wait()


def _experts(h2, tile_start, n_used, src_tok, w_gate, w_up, w_down, layer):
    d = D_MODEL
    tm = MOE_TILE
    rows = src_tok.shape[0]

    def wspec(shape):
        return pl.BlockSpec((None, None) + shape, lambda e, ts, nu, src: (layer, e, 0, 0))

    return pl.pallas_call(
        _experts_kernel,
        grid_spec=pltpu.PrefetchScalarGridSpec(
            num_scalar_prefetch=3,
            grid=(N_EXPERTS,),
            in_specs=[pl.BlockSpec(memory_space=pl.ANY),
                      wspec((d, EXPERT_FF)), wspec((d, EXPERT_FF)), wspec((EXPERT_FF, d))],
            out_specs=pl.BlockSpec(memory_space=pl.ANY),
            scratch_shapes=[pltpu.VMEM((2, tm, d), F32),
                            pltpu.VMEM((2, tm, d), F32),
                            pltpu.SemaphoreType.DMA((2,)),
                            pltpu.SemaphoreType.DMA((2,)),
                            pltpu.VMEM((d, EXPERT_FF), BF16),
                            pltpu.VMEM((d, EXPERT_FF), BF16),
                            pltpu.VMEM((EXPERT_FF, d), BF16)]),
        out_shape=jax.ShapeDtypeStruct((rows, d), F32),
        compiler_params=_params("arbitrary"),
        name="experts",
    )(tile_start, n_used, src_tok, h2, w_gate, w_up, w_down)


def _combine_kernel(slot_ref, ys_hbm, w_ref, x_ref, mod_ref, g_ref, b_ref, o_ref, ybuf, sem, *, n_tok):
    i = pl.program_id(0)
    nt = pl.num_programs(0)
    tm = ROW_TILE

    def issue(blk, buf):
        for r in range(tm):
            for k in range(TOP_K):
                row = slot_ref[k * n_tok + blk * tm + r]
                _row_copy(ys_hbm, row, ybuf.at[buf, k, pl.ds(r, 1), :], sem.at[buf]).start()

    def wait(buf):
        for k in range(TOP_K):
            pltpu.make_async_copy(ys_hbm.at[pl.ds(0, tm), :], ybuf.at[buf, k], sem.at[buf]).wait()

    @pl.when(i == 0)
    def _():
        issue(0, 0)

    buf = i % 2
    wait(buf)
    issue(jnp.minimum(i + 1, nt - 1), 1 - buf)
    y = w_ref[:, 0:1] * ybuf[buf, 0] + w_ref[:, 1:2] * ybuf[buf, 1]
    t = DEEPNORM_ALPHA * x_ref[...] + mod_ref[MOD_G2:MOD_G2 + 1, :] * y
    o_ref[...] = _layer_norm(t, g_ref[...], b_ref[...])

    @pl.when(i == nt - 1)
    def _():
        wait(1 - buf)


def _combine(ys, slots, w_tok, x1, mods, n_ctx_blocks, ln_g, ln_b):
    t = x1.shape[0]
    d = D_MODEL
    return pl.pallas_call(
        functools.partial(_combine_kernel, n_tok=t),
        grid_spec=pltpu.PrefetchScalarGridSpec(
            num_scalar_prefetch=1,
            grid=(t // ROW_TILE,),
            in_specs=[pl.BlockSpec(memory_space=pl.ANY),
                      pl.BlockSpec((ROW_TILE, TOP_K), lambda i, s: (i, 0)),
                      pl.BlockSpec((ROW_TILE, d), lambda i, s: (i, 0)),
                      pl.BlockSpec((None, 8, d), lambda i, s: (jnp.where(i < n_ctx_blocks, 0, 1), 0, 0)),
                      pl.BlockSpec((1, d), lambda i, s: (0, 0)),
                      pl.BlockSpec((1, d), lambda i, s: (0, 0))],
            out_specs=pl.BlockSpec((ROW_TILE, d), lambda i, s: (i, 0)),
            scratch_shapes=[pltpu.VMEM((2, TOP_K, ROW_TILE, d), F32),
                            pltpu.SemaphoreType.DMA((2,))]),
        out_shape=jax.ShapeDtypeStruct((t, d), F32),
        compiler_params=_params("arbitrary"),
        name="combine",
    )(slots, ys, w_tok, x1, mods, ln_g, ln_b)


def _moe(h2, x1, mods, n_ctx_blocks, wr_t, rb_b, w_gate, w_up, w_down, layer, ln_g, ln_b):
    t = h2.shape[0]
    tm = MOE_TILE
    e_idx, w_tok, rank, cnt = _router(h2, wr_t, rb_b)
    counts = cnt[:, 0].astype(jnp.int32)
    tiles_per = (counts + tm - 1) // tm
    tile_end = jnp.cumsum(tiles_per)
    n_used = tile_end[-1]
    row_off = (tile_end - tiles_per) * tm
    experts = jnp.arange(N_EXPERTS, dtype=jnp.int32)
    slots = jnp.sum(jnp.where(e_idx[:, :, None] == experts, row_off, 0), axis=-1) + rank
    max_tiles = (TOP_K * t + N_EXPERTS * (tm - 1)) // tm + 1
    tile_start = jnp.concatenate([tile_end - tiles_per, n_used.reshape(1)]).astype(jnp.int32)
    tok = jnp.tile(jnp.arange(t, dtype=jnp.int32), TOP_K)
    src_tok = jnp.zeros((max_tiles * tm,), jnp.int32).at[slots.reshape(-1)].set(tok)
    ys = _experts(h2, tile_start, n_used.reshape(1).astype(jnp.int32), src_tok, w_gate, w_up, w_down, layer)
    return _combine(ys, slots.reshape(-1), w_tok.T, x1, mods, n_ctx_blocks, ln_g, ln_b)


def _rope_tables(n):
    t = jnp.arange(n)
    row = (t // GRID_W).astype(F32)
    col = (t % GRID_W).astype(F32)
    n_freq = HEAD_DIM // 4
    inv_freq = ROPE_BASE ** (-jnp.arange(n_freq, dtype=F32) / n_freq)
    ang = jnp.concatenate([row[:, None] * inv_freq, col[:, None] * inv_freq], axis=-1)
    cos, sin = jnp.cos(ang), jnp.sin(ang)
    return jnp.concatenate([cos, cos], axis=-1), jnp.concatenate([-sin, sin], axis=-1)


def kernel(x, c, ctx, c_ctx, w_mod, b_mod, w_in, attn_sink, na_rpb, sgu_ln_g, sgu_ln_b, sgu_w, sgu_b,
           w_out, ln1_g, ln1_b, w_router, router_bias, w_gate, w_up, w_down, ln2_g, ln2_b):
    batch, n, d = x.shape
    lctx = ctx.shape[1]
    assert batch == 1 and d == D_MODEL and n % ROW_TILE == 0 and lctx % ROW_TILE == 0
    n_ctx_blocks = lctx // ROW_TILE

    mods = _modulation(c, c_ctx, w_mod, b_mod).reshape(DEPTH, 8, 6, d)
    mod_lat = jnp.pad(mods[:, 0], ((0, 0), (0, 2), (0, 0)))
    mod_ctx = jnp.pad(mods[:, 1], ((0, 0), (0, 2), (0, 0)))
    cos, sin = _rope_tables(n)
    wr_t = w_router.T
    rb_b = jnp.broadcast_to(router_bias.reshape(N_EXPERTS, 1), (N_EXPERTS, LANES))

    x_lat, lat_off = x[0], 0
    x_ctx = ctx[0]
    for l in range(DEPTH):
        last = l == DEPTH - 1
        w_in_bf = w_in[l].astype(BF16)
        w_out_bf = w_out[l].astype(BF16)
        sgu_w_bf = sgu_w[l].astype(BF16)
        sgu_b_b = jnp.broadcast_to(sgu_b[l][:, :, None], (C_GROUPS, CHUNK, LANES))
        ln_g_c, ln_b_c = sgu_ln_g[l].reshape(1, C_W), sgu_ln_b[l].reshape(1, C_W)
        bias_tab = _na_bias_table(na_rpb[l])
        g1, b1 = ln1_g[l].reshape(1, d), ln1_b[l].reshape(1, d)
        g2, b2 = ln2_g[l].reshape(1, d), ln2_b[l].reshape(1, d)

        p = _proj(x_lat, lat_off, n, mod_lat[l], cos, sin, w_in_bf, rope=True)
        pc = _proj(x_ctx, 0, lctx, mod_ctx[l], cos, sin, w_in_bf, rope=False)
        o_a = _attn_a(p, pc, attn_sink[l], latent=True)
        o_b = _attn_b(p, pc, bias_tab, latent=True)
        o_c = _sgu(p, ln_g_c, ln_b_c, sgu_w_bf, sgu_b_b)
        if last:
            x1, h2 = _outproj(o_a, o_b, o_c, w_out_bf, x_lat, lat_off, mod_lat[l], g1, b1, n, 0, None)
            x_lat = _moe(h2, x1, jnp.stack([mod_ctx[l], mod_lat[l]]), 0, wr_t, rb_b,
                         w_gate, w_up, w_down, l, g2, b2)
        else:
            oc_a = _attn_a(pc, pc, attn_sink[l], latent=False)
            oc_b = _attn_b(pc, pc, bias_tab, latent=False)
            oc_c = _sgu(pc, ln_g_c, ln_b_c, sgu_w_bf, sgu_b_b)
            total = lctx + n
            prev = _outproj(oc_a, oc_b, oc_c, w_out_bf, x_ctx, 0, mod_ctx[l], g1, b1, total, 0, None)
            x1, h2 = _outproj(o_a, o_b, o_c, w_out_bf, x_lat, lat_off, mod_lat[l], g1, b1, total,
                              n_ctx_blocks, prev)
            x_all = _moe(h2, x1, jnp.stack([mod_ctx[l], mod_lat[l]]), n_ctx_blocks, wr_t, rb_b,
                         w_gate, w_up, w_down, l, g2, b2)
            x_lat, lat_off, x_ctx = x_all, n_ctx_blocks, x_all
    return x_lat.reshape(batch, n, d)
```

```python
import functools

import numpy as np
import jax
import jax.numpy as jnp
from jax import lax
from jax.experimental import pallas as pl
from jax.experimental.pallas import tpu as pltpu

F32 = jnp.float32
BF16 = jnp.bfloat16

D_MODEL = 2048
DEPTH = 2
GRID_W = 64
HEAD_DIM = 128
A_HEADS = 6
A_KV_HEADS = 2
A_GROUP = A_HEADS // A_KV_HEADS
A_BLOCK = 128
B_HEADS = 6
NA_ROWS = 8
NA_COLS = 16
C_GROUPS = 4
C_W = C_GROUPS * HEAD_DIM
CHUNK = 128
N_EXPERTS = 32
N_EXPERT_GROUPS = 4
EXPERTS_PER_GROUP = N_EXPERTS // N_EXPERT_GROUPS
TOP_K = 2
EXPERT_FF = 512
ROPE_BASE = 10000.0
LN_EPS = 1e-5
NEG_INF = -1e30
DEEPNORM_ALPHA = (2 * DEPTH) ** 0.25
ATTN_SCALE = HEAD_DIM ** -0.5

A_Q_W = A_HEADS * HEAD_DIM
A_KV_W = A_KV_HEADS * HEAD_DIM
B_W = B_HEADS * HEAD_DIM
OFF_AK = A_Q_W
OFF_AV = OFF_AK + A_KV_W
OFF_BQ = OFF_AV + A_KV_W
OFF_BK = OFF_BQ + B_W
OFF_BV = OFF_BK + B_W
OFF_C = OFF_BV + B_W
IN_COLS = OFF_C + 2 * C_W

VMEM_LIMIT_BYTES = 56 * 1024 * 1024
LANES = 128

ROW_TILE = 256
PROJ_COL_TILE = 512
MOD_COL_TILE = 1024
MOE_TILE = 256

MOD_SH1, MOD_SC1, MOD_G1, MOD_SH2, MOD_SC2, MOD_G2 = range(6)


def _params(*sem):
    return pltpu.CompilerParams(dimension_semantics=sem, vmem_limit_bytes=VMEM_LIMIT_BYTES)


def _layer_norm(t, g, b):
    mu = jnp.mean(t, axis=-1, keepdims=True)
    d = t - mu
    var = jnp.mean(d * d, axis=-1, keepdims=True)
    return d * lax.rsqrt(var + LN_EPS) * g + b


def _sigmoid(v):
    return 1.0 / (1.0 + jnp.exp(-v))


def _dot_nt(a, b):
    return lax.dot_general(a, b, (((1,), (1,)), ((), ())), preferred_element_type=F32)


PACK_W = D_MODEL // 2
PACK_SUB = PACK_W // LANES


def _pack_rows(v):
    bits = pltpu.bitcast(v.astype(BF16).astype(F32), jnp.int32)
    return lax.shift_right_logical(bits[:, :PACK_W], 16) | bits[:, PACK_W:]


def _store_packed(ref, words, m):
    for s in range(PACK_SUB):
        ref[pl.ds(s, m, stride=PACK_SUB), :] = words[:, s * LANES:(s + 1) * LANES]


def _load_packed(ref, m):
    lo, hi = [], []
    for s in range(PACK_SUB):
        w = ref[pl.ds(s, m, stride=PACK_SUB), :]
        lo.append(pltpu.bitcast(w << 16, F32))
        hi.append(pltpu.bitcast(w & jnp.int32(-65536), F32))
    return jnp.concatenate(lo + hi, axis=1)


def _mod_kernel(c_ref, w_ref, b_ref, o_ref):
    w = w_ref[...]
    reps = w.shape[1] // LANES
    rows = []
    for r in range(2):
        cv = c_ref[r]
        s = cv * _sigmoid(cv)
        sb = jnp.concatenate([s] * reps, axis=1)
        rows.append(jnp.sum(w * sb, axis=0, keepdims=True) + b_ref[...])
    rows.append(jnp.zeros((6, w.shape[1]), F32))
    o_ref[...] = jnp.concatenate(rows, axis=0)


def _modulation(c, c_ctx, w_mod, b_mod):
    d = D_MODEL
    cb = jnp.stack([jnp.broadcast_to(c.reshape(d, 1), (d, LANES)),
                    jnp.broadcast_to(c_ctx.reshape(d, 1), (d, LANES))])
    n_out = 6 * d
    return pl.pallas_call(
        _mod_kernel,
        grid=(DEPTH, n_out // MOD_COL_TILE),
        in_specs=[pl.BlockSpec((2, d, LANES), lambda l, j: (0, 0, 0)),
                  pl.BlockSpec((None, d, MOD_COL_TILE), lambda l, j: (l, 0, j)),
                  pl.BlockSpec((None, 1, MOD_COL_TILE), lambda l, j: (l, 0, j))],
        out_specs=pl.BlockSpec((None, 8, MOD_COL_TILE), lambda l, j: (l, 0, j)),
        out_shape=jax.ShapeDtypeStruct((DEPTH, 8, n_out), F32),
        compiler_params=_params("arbitrary", "arbitrary"),
        name="modulation",
    )(cb, w_mod, b_mod.reshape(DEPTH, 1, n_out))


def _gelu_tanh(v):
    return 0.5 * v * (1.0 + jnp.tanh(np.sqrt(2.0 / np.pi).astype(np.float32) * (v + 0.044715 * (v * v * v))))


def _proj_kernel(x_ref, mod_ref, cos_ref, sin_ref, w_ref, o_ref, *, rope):
    x = x_ref[...]
    h = (x * (1.0 + mod_ref[MOD_SC1:MOD_SC1 + 1, :]) + mod_ref[MOD_SH1:MOD_SH1 + 1, :]).astype(BF16)
    tn = PROJ_COL_TILE
    for j in range(IN_COLS // tn):
        c0 = j * tn
        acc = jnp.dot(h, w_ref[:, c0:c0 + tn], preferred_element_type=F32)
        if c0 < OFF_AV:
            if rope:
                cos = cos_ref[...]
                sin = sin_ref[...]
                parts = []
                for hh in range(tn // HEAD_DIM):
                    a = acc[:, hh * HEAD_DIM:(hh + 1) * HEAD_DIM]
                    parts.append(a * cos + pltpu.roll(a, HEAD_DIM // 2, 1) * sin)
                acc = jnp.concatenate(parts, axis=1)
        elif c0 >= OFF_C:
            acc = _gelu_tanh(acc)
        o_ref[:, c0:c0 + tn] = acc.astype(BF16)


def _proj(x, x_off, rows, mod, cos, sin, w_bf, rope):
    d = D_MODEL
    return pl.pallas_call(
        functools.partial(_proj_kernel, rope=rope),
        grid=(rows // ROW_TILE,),
        in_specs=[pl.BlockSpec((ROW_TILE, d), lambda i: (i + x_off, 0)),
                  pl.BlockSpec((8, d), lambda i: (0, 0)),
                  pl.BlockSpec((ROW_TILE, HEAD_DIM), lambda i: (i, 0)),
                  pl.BlockSpec((ROW_TILE, HEAD_DIM), lambda i: (i, 0)),
                  pl.BlockSpec((d, IN_COLS), lambda i: (0, 0), pipeline_mode=pl.Buffered(1))],
        out_specs=pl.BlockSpec((ROW_TILE, IN_COLS), lambda i: (i, 0)),
        out_shape=jax.ShapeDtypeStruct((rows, IN_COLS), BF16),
        compiler_params=_params("arbitrary"),
        name="proj_rope" if rope else "proj_ctx",
    )(x, mod, cos, sin, w_bf)


def _softmax_pv(s_parts, v_parts, sink):
    m = s_parts[0].max(axis=-1, keepdims=True)
    for s in s_parts[1:]:
        m = jnp.maximum(m, s.max(axis=-1, keepdims=True))
    if sink is not None:
        m = jnp.maximum(m, sink)
    denom = None if sink is None else jnp.exp(sink - m)
    out = None
    for s, v in zip(s_parts, v_parts):
        e = jnp.exp(s - m)
        es = e.sum(axis=-1, keepdims=True)
        denom = es if denom is None else denom + es
        pv = jnp.dot(e.astype(BF16), v, preferred_element_type=F32)
        out = pv if out is None else out + pv
    return out / denom


def _attn_a_kernel(sink_ref, q_ref, kp_ref, kc_ref, kn_ref, vp_ref, vc_ref, vn_ref, kx_ref, vx_ref, o_ref,
                   mask_ref, *, latent):
    i = pl.program_id(0)
    nb = pl.num_programs(0)
    nq = A_GROUP * A_BLOCK

    if latent:
        @pl.when(i == 0)
        def _():
            qi = lax.broadcasted_iota(jnp.int32, (nq, 3 * A_BLOCK), 0) % A_BLOCK
            jj = lax.broadcasted_iota(jnp.int32, (nq, 3 * A_BLOCK), 1)
            ok = (jj >= qi) & (jj <= qi + 2 * A_BLOCK)
            mask_ref[...] = jnp.where(ok, 0.0, NEG_INF).astype(F32)

        col = lax.broadcasted_iota(jnp.int32, (nq, 3 * A_BLOCK), 1)
        off_band = ((col < A_BLOCK) & (i == 0)) | ((col >= 2 * A_BLOCK) & (i == nb - 1))

    for kh in range(A_KV_HEADS):
        hs = [kh * A_GROUP + g for g in range(A_GROUP)]
        q = jnp.concatenate([q_ref[:, h * HEAD_DIM:(h + 1) * HEAD_DIM] for h in hs], axis=0)
        sink = jnp.concatenate([jnp.full((A_BLOCK, 1), sink_ref[h], F32) for h in hs], axis=0)
        ks = slice(kh * HEAD_DIM, (kh + 1) * HEAD_DIM)
        s_parts = [_dot_nt(q, kx_ref[:, ks]) * ATTN_SCALE]
        v_parts = [vx_ref[:, ks]]
        if latent:
            kband = jnp.concatenate([kp_ref[:, ks], kc_ref[:, ks], kn_ref[:, ks]], axis=0)
            vband = jnp.concatenate([vp_ref[:, ks], vc_ref[:, ks], vn_ref[:, ks]], axis=0)
            s_loc = _dot_nt(q, kband) * ATTN_SCALE + mask_ref[...]
            s_parts.append(jnp.where(off_band, NEG_INF, s_loc))
            v_parts.append(vband)
        out = _softmax_pv(s_parts, v_parts, sink)
        for g, h in enumerate(hs):
            o_ref[:, h * HEAD_DIM:(h + 1) * HEAD_DIM] = out[g * A_BLOCK:(g + 1) * A_BLOCK].astype(BF16)


def _attn_a(p, pc, sink, latent):
    rows = p.shape[0]
    nb = rows // A_BLOCK
    kcol = OFF_AK // A_KV_W
    vcol = OFF_AV // A_KV_W

    def band(col, shift):
        return pl.BlockSpec((A_BLOCK, A_KV_W), lambda i, s: (jnp.clip(i + shift, 0, nb - 1), col))

    lctx = pc.shape[0]
    return pl.pallas_call(
        functools.partial(_attn_a_kernel, latent=latent),
        grid_spec=pltpu.PrefetchScalarGridSpec(
            num_scalar_prefetch=1,
            grid=(nb,),
            in_specs=[pl.BlockSpec((A_BLOCK, A_Q_W), lambda i, s: (i, 0)),
                      band(kcol, -1), band(kcol, 0), band(kcol, 1),
                      band(vcol, -1), band(vcol, 0), band(vcol, 1),
                      pl.BlockSpec((lctx, A_KV_W), lambda i, s: (0, kcol)),
                      pl.BlockSpec((lctx, A_KV_W), lambda i, s: (0, vcol))],
            out_specs=pl.BlockSpec((A_BLOCK, A_Q_W), lambda i, s: (i, 0)),
            scratch_shapes=[pltpu.VMEM((A_GROUP * A_BLOCK, 3 * A_BLOCK), F32)]),
        out_shape=jax.ShapeDtypeStruct((rows, A_Q_W), BF16),
        compiler_params=_params("arbitrary"),
        name="attn_a_latent" if latent else "attn_a_ctx",
    )(sink, p, p, p, p, p, p, p, pc, pc)


B_PAIR_W = 2 * HEAD_DIM
NA_GROUP_ROWS = ROW_TILE // GRID_W
NA_WIN_ROWS = NA_ROWS + NA_GROUP_ROWS
NA_PAIRS = NA_WIN_ROWS // 2
NA_BIAS_OFFS = 2 * NA_ROWS
NA_BOTH, NA_LEFT, NA_RIGHT = range(3)


def _attn_b_kernel(q_ref, k_ref, v_ref, kx_ref, vx_ref, bias_ref, o_ref, *, latent, grid_rows):
    g = pl.program_id(1)
    if latent:
        r_base = g * NA_GROUP_ROWS
        w0 = jnp.clip(r_base - NA_ROWS // 2, 0, grid_rows - NA_WIN_ROWS)
        start = pl.multiple_of(w0 * GRID_W, GRID_W)
    for hh in range(2):
        hs = slice(hh * HEAD_DIM, (hh + 1) * HEAD_DIM)
        q = q_ref[:, hs]
        s_parts = [_dot_nt(q, kx_ref[:, hs]) * ATTN_SCALE]
        v_parts = [vx_ref[:, hs]]
        if latent:
            kwin = k_ref[pl.ds(start, NA_WIN_ROWS * GRID_W), hs]
            vwin = v_ref[pl.ds(start, NA_WIN_ROWS * GRID_W), hs]
            bias_rows = []
            for rr in range(NA_GROUP_ROWS):
                r = r_base + rr
                r0 = jnp.clip(r - NA_ROWS // 2, 0, grid_rows - NA_ROWS)
                tiles = []
                for jp in range(NA_PAIRS):
                    ka = w0 + 2 * jp
                    in_a = (ka >= r0) & (ka < r0 + NA_ROWS)
                    in_b = (ka + 1 >= r0) & (ka + 1 < r0 + NA_ROWS)
                    variant = jnp.where(in_a, jnp.where(in_b, NA_BOTH, NA_LEFT), jnp.where(in_b, NA_RIGHT, NA_LEFT))
                    off = jnp.where(in_a | in_b, jnp.clip(ka - r + NA_ROWS, 0, NA_BIAS_OFFS - 1), 0)
                    tiles.append(bias_ref[hh, variant, off])
                bias_rows.append(jnp.concatenate(tiles, axis=1))
            bias = jnp.concatenate(bias_rows, axis=0)
            s_parts.append(_dot_nt(q, kwin) * ATTN_SCALE + bias)
            v_parts.append(vwin)
        out = _softmax_pv(s_parts, v_parts, None)
        o_ref[:, hs] = out.astype(BF16)


def _na_bias_table(rpb):
    cols = np.arange(GRID_W)
    c0 = np.clip(cols - NA_COLS // 2, 0, GRID_W - NA_COLS)
    rel = cols[None, :] - cols[:, None] + NA_COLS - 1
    ok = (cols[None, :] >= c0[:, None]) & (cols[None, :] < c0[:, None] + NA_COLS)
    onehot = (rel[None] == np.arange(2 * NA_COLS - 1)[:, None, None]).astype(np.float32)
    t = jnp.einsum("hrd,dqk->hrqk", rpb, onehot, precision=lax.Precision.HIGHEST)
    t = jnp.where(ok[None, None], t, NEG_INF).astype(F32)
    t = jnp.pad(t, ((0, 0), (1, 1), (0, 0), (0, 0)), constant_values=NEG_INF)
    first, second = t[:, :-1], t[:, 1:]
    masked = jnp.full_like(first, NEG_INF)
    return jnp.stack([jnp.concatenate([first, second], axis=-1),
                      jnp.concatenate([first, masked], axis=-1),
                      jnp.concatenate([masked, second], axis=-1)], axis=1)


def _attn_b(p, pc, bias_tab, latent):
    rows = p.shape[0]
    lctx = pc.shape[0]
    assert not latent or rows // GRID_W >= NA_WIN_ROWS
    qcol = OFF_BQ // B_PAIR_W
    kcol = OFF_BK // B_PAIR_W
    vcol = OFF_BV // B_PAIR_W
    return pl.pallas_call(
        functools.partial(_attn_b_kernel, latent=latent, grid_rows=rows // GRID_W),
        grid=(B_HEADS // 2, rows // ROW_TILE),
        in_specs=[pl.BlockSpec((ROW_TILE, B_PAIR_W), lambda hp, g: (g, qcol + hp)),
                  pl.BlockSpec((rows, B_PAIR_W), lambda hp, g: (0, kcol + hp)),
                  pl.BlockSpec((rows, B_PAIR_W), lambda hp, g: (0, vcol + hp)),
                  pl.BlockSpec((lctx, B_PAIR_W), lambda hp, g: (0, kcol + hp)),
                  pl.BlockSpec((lctx, B_PAIR_W), lambda hp, g: (0, vcol + hp)),
                  pl.BlockSpec((2, 3, NA_BIAS_OFFS, GRID_W, 2 * GRID_W), lambda hp, g: (hp, 0, 0, 0, 0))],
        out_specs=pl.BlockSpec((ROW_TILE, B_PAIR_W), lambda hp, g: (g, hp)),
        out_shape=jax.ShapeDtypeStruct((rows, B_W), BF16),
        compiler_params=_params("arbitrary", "arbitrary"),
        name="attn_b_latent" if latent else "attn_b_ctx",
    )(p, p, p, pc, pc, bias_tab)


def _sgu_kernel(u_ref, v_ref, g_ref, b_ref, w_ref, bs_ref, o_ref):
    for ch in range(ROW_TILE // CHUNK):
        rs = slice(ch * CHUNK, (ch + 1) * CHUNK)
        for grp in range(C_GROUPS):
            cs = slice(grp * HEAD_DIM, (grp + 1) * HEAD_DIM)
            vn = _layer_norm(v_ref[rs, cs].astype(F32), g_ref[:, cs], b_ref[:, cs])
            mixed = jnp.dot(w_ref[grp], vn.astype(BF16), preferred_element_type=F32) + bs_ref[grp]
            o_ref[rs, cs] = (u_ref[rs, cs].astype(F32) * mixed).astype(BF16)


def _sgu(p, ln_g, ln_b, w_bf, bs_b):
    rows = p.shape[0]
    ucol = OFF_C // C_W
    return pl.pallas_call(
        _sgu_kernel,
        grid=(rows // ROW_TILE,),
        in_specs=[pl.BlockSpec((ROW_TILE, C_W), lambda i: (i, ucol)),
                  pl.BlockSpec((ROW_TILE, C_W), lambda i: (i, ucol + 1)),
                  pl.BlockSpec((1, C_W), lambda i: (0, 0)),
                  pl.BlockSpec((1, C_W), lambda i: (0, 0)),
                  pl.BlockSpec((C_GROUPS, CHUNK, CHUNK), lambda i: (0, 0, 0)),
                  pl.BlockSpec((C_GROUPS, CHUNK, LANES), lambda i: (0, 0, 0))],
        out_specs=pl.BlockSpec((ROW_TILE, C_W), lambda i: (i, 0)),
        out_shape=jax.ShapeDtypeStruct((rows, C_W), BF16),
        compiler_params=_params("arbitrary"),
        name="sgu",
    )(p, p, ln_g, ln_b, w_bf, bs_b)


def _outproj_kernel(oa_ref, ob_ref, oc_ref, w_ref, x_ref, mod_ref, g_ref, b_ref, *rest):
    x1_ref, h2_ref = rest[-2], rest[-1]
    mix = jnp.dot(oa_ref[...], w_ref[0:A_Q_W, :], preferred_element_type=F32)
    mix += jnp.dot(ob_ref[...], w_ref[A_Q_W:A_Q_W + B_W, :], preferred_element_type=F32)
    mix += jnp.dot(oc_ref[...], w_ref[A_Q_W + B_W:, :], preferred_element_type=F32)
    t = DEEPNORM_ALPHA * x_ref[...] + mod_ref[MOD_G1:MOD_G1 + 1, :] * mix
    x1 = _layer_norm(t, g_ref[...], b_ref[...])
    x1_ref[...] = x1
    h2 = x1 * (1.0 + mod_ref[MOD_SC2:MOD_SC2 + 1, :]) + mod_ref[MOD_SH2:MOD_SH2 + 1, :]
    _store_packed(h2_ref, _pack_rows(h2), ROW_TILE)


def _outproj(o_a, o_b, o_c, w_bf, x, x_off, mod, ln_g, ln_b, total_rows, out_off, prev):
    rows = o_a.shape[0]
    d = D_MODEL
    in_specs = [pl.BlockSpec((ROW_TILE, A_Q_W), lambda i: (i, 0)),
                pl.BlockSpec((ROW_TILE, B_W), lambda i: (i, 0)),
                pl.BlockSpec((ROW_TILE, C_W), lambda i: (i, 0)),
                pl.BlockSpec((d, d), lambda i: (0, 0), pipeline_mode=pl.Buffered(1)),
                pl.BlockSpec((ROW_TILE, d), lambda i: (i + x_off, 0)),
                pl.BlockSpec((8, d), lambda i: (0, 0)),
                pl.BlockSpec((1, d), lambda i: (0, 0)),
                pl.BlockSpec((1, d), lambda i: (0, 0))]
    args = [o_a, o_b, o_c, w_bf, x, mod, ln_g, ln_b]
    aliases = {}
    if prev is not None:
        in_specs += [pl.BlockSpec(memory_space=pl.ANY), pl.BlockSpec(memory_space=pl.ANY)]
        aliases = {len(args): 0, len(args) + 1: 1}
        args += list(prev)
    return pl.pallas_call(
        _outproj_kernel,
        grid=(rows // ROW_TILE,),
        in_specs=in_specs,
        out_specs=[pl.BlockSpec((ROW_TILE, d), lambda i: (i + out_off, 0)),
                   pl.BlockSpec((ROW_TILE * PACK_SUB, LANES), lambda i: (i + out_off, 0))],
        out_shape=[jax.ShapeDtypeStruct((total_rows, d), F32),
                   jax.ShapeDtypeStruct((total_rows * PACK_SUB, LANES), jnp.int32)],
        input_output_aliases=aliases,
        compiler_params=_params("arbitrary"),
        name="outproj",
    )(*args)


def _top2_sublanes(vals, sub):
    m1 = vals.max(axis=0, keepdims=True)
    i1 = jnp.where(vals == m1, sub, vals.shape[0]).min(axis=0, keepdims=True)
    rest = jnp.where(sub == i1, -jnp.inf, vals)
    m2 = rest.max(axis=0, keepdims=True)
    i2 = jnp.where(rest == m2, sub, vals.shape[0]).min(axis=0, keepdims=True)
    return m1, i1, m2, i2


def _router_kernel(h_ref, wr_ref, rb_ref, e_ref, w_ref, rank_ref, cnt_ref, run_ref):
    i = pl.program_id(0)
    tm = ROW_TILE
    epg = EXPERTS_PER_GROUP

    @pl.when(i == 0)
    def _():
        run_ref[...] = jnp.zeros_like(run_ref)

    logits = lax.dot_general(wr_ref[...], _load_packed(h_ref, tm), (((1,), (1,)), ((), ())),
                             precision=lax.Precision.HIGHEST, preferred_element_type=F32)
    scores = _sigmoid(logits)
    biased = scores + jnp.concatenate([rb_ref[...]] * (tm // LANES), axis=1)
    sub = lax.broadcasted_iota(jnp.int32, (epg, tm), 0)

    best = None
    for g in range(N_EXPERT_GROUPS):
        m1, _, m2, _ = _top2_sublanes(biased[g * epg:(g + 1) * epg], sub)
        gs = m1 + m2
        if best is None:
            best, grp = gs, jnp.zeros((1, tm), jnp.int32)
            bsel, ssel = biased[0:epg], scores[0:epg]
        else:
            better = gs > best
            best = jnp.where(better, gs, best)
            grp = jnp.where(better, g, grp)
            bsel = jnp.where(better, biased[g * epg:(g + 1) * epg], bsel)
            ssel = jnp.where(better, scores[g * epg:(g + 1) * epg], ssel)
    _, i1, _, i2 = _top2_sublanes(bsel, sub)
    w1 = jnp.where(sub == i1, ssel, 0.0).sum(axis=0, keepdims=True)
    w2 = jnp.where(sub == i2, ssel, 0.0).sum(axis=0, keepdims=True)
    tot = w1 + w2
    e1 = grp * epg + i1
    e2 = grp * epg + i2

    eiota = lax.broadcasted_iota(jnp.int32, (N_EXPERTS, tm), 0)
    oh1 = (eiota == e1).astype(F32)
    oh2 = (eiota == e2).astype(F32)
    ohb = oh1 + oh2
    before = (lax.broadcasted_iota(jnp.int32, (tm, tm), 0) < lax.broadcasted_iota(jnp.int32, (tm, tm), 1))
    prefix = jnp.dot(ohb.astype(BF16), before.astype(BF16), preferred_element_type=F32)
    pos = run_ref[...] + prefix
    r1 = (oh1 * pos).sum(axis=0, keepdims=True)
    r2 = (oh2 * pos).sum(axis=0, keepdims=True)
    run_ref[...] = run_ref[...] + ohb.sum(axis=1, keepdims=True)

    e_ref[...] = jnp.concatenate([e1, e2], axis=0)
    w_ref[...] = jnp.concatenate([w1 / tot, w2 / tot], axis=0)
    rank_ref[...] = jnp.concatenate([r1, r2], axis=0).astype(jnp.int32)
    cnt_ref[...] = run_ref[:, 0:LANES]


def _router(h2p, wr_t, rb_b):
    t = h2p.shape[0] // PACK_SUB
    d = D_MODEL
    row2 = pl.BlockSpec((TOP_K, ROW_TILE), lambda i: (0, i))
    return pl.pallas_call(
        _router_kernel,
        grid=(t // ROW_TILE,),
        in_specs=[pl.BlockSpec((ROW_TILE * PACK_SUB, LANES), lambda i: (i, 0)),
                  pl.BlockSpec((N_EXPERTS, d), lambda i: (0, 0)),
                  pl.BlockSpec((N_EXPERTS, LANES), lambda i: (0, 0))],
        out_specs=[row2, row2, row2, pl.BlockSpec((N_EXPERTS, LANES), lambda i: (0, 0))],
        out_shape=[jax.ShapeDtypeStruct((TOP_K, t), jnp.int32),
                   jax.ShapeDtypeStruct((TOP_K, t), F32),
                   jax.ShapeDtypeStruct((TOP_K, t), jnp.int32),
                   jax.ShapeDtypeStruct((N_EXPERTS, LANES), F32)],
        scratch_shapes=[pltpu.VMEM((N_EXPERTS, ROW_TILE), F32)],
        compiler_params=_params("arbitrary"),
        name="router",
    )(h2p, wr_t, rb_b)


def _row_copy(src_hbm, first_sublane, dst, sem):
    return pltpu.make_async_copy(src_hbm.at[pl.ds(pl.multiple_of(first_sublane, PACK_SUB), PACK_SUB), :], dst, sem)


def _experts_kernel(ts_ref, nu_ref, src_ref, h_hbm, wg_ref, wu_ref, wd_ref, ys_hbm,
                    xbuf, ybuf, gsem, ysem, wg_bf, wu_bf, wd_bf):
    e = pl.program_id(0)
    n_used = nu_ref[0]
    tm = MOE_TILE

    tsub = tm * PACK_SUB

    def gather(tile, slot):
        base = tile * tm
        for r in range(tm):
            _row_copy(h_hbm, src_ref[base + r], xbuf.at[slot, pl.ds(r * PACK_SUB, PACK_SUB), :], gsem.at[slot]).start()

    def wait_gather(slot):
        pltpu.make_async_copy(h_hbm.at[pl.ds(0, tsub), :], xbuf.at[slot], gsem.at[slot]).wait()

    def y_store(tile, slot):
        row0 = pl.multiple_of(tile * tsub, tsub)
        return pltpu.make_async_copy(ybuf.at[slot], ys_hbm.at[pl.ds(row0, tsub), :], ysem.at[slot])

    @pl.when((e == 0) & (n_used > 0))
    def _():
        gather(0, 0)

    @pl.when(ts_ref[e + 1] > ts_ref[e])
    def _():
        wg_bf[...] = wg_ref[...].astype(BF16)
        wu_bf[...] = wu_ref[...].astype(BF16)
        wd_bf[...] = wd_ref[...].astype(BF16)

    def tile_body(g, carry):
        slot = g % 2

        @pl.when(g >= 2)
        def _():
            y_store(g - 2, slot).wait()

        wait_gather(slot)
        gather(jnp.minimum(g + 1, n_used - 1), 1 - slot)
        x = _load_packed(xbuf.at[slot], tm).astype(BF16)
        gate = jnp.dot(x, wg_bf[...], preferred_element_type=F32)
        up = jnp.dot(x, wu_bf[...], preferred_element_type=F32)
        hid = (gate * _sigmoid(gate)) * up
        y = jnp.dot(hid.astype(BF16), wd_bf[...], preferred_element_type=F32)
        _store_packed(ybuf.at[slot], _pack_rows(y), tm)
        y_store(g, slot).start()
        return carry

    lax.fori_loop(ts_ref[e], ts_ref[e + 1], tile_body, 0)

    @pl.when((e == pl.num_programs(0) - 1) & (n_used > 0))
    def _():
        wait_gather(n_used % 2)
        y_store(n_used - 1, (n_used - 1) % 2).wait()

        @pl.when(n_used >= 2)
        def _():
            y_store(n_used - 2, n_used % 2).wait()


def _experts(h2, tile_start, n_used, src_tok, w_gate, w_up, w_down, layer):
    d = D_MODEL
    tm = MOE_TILE
    rows = src_tok.shape[0]

    def wspec(shape):
        return pl.BlockSpec((None, None) + shape, lambda e, ts, nu, src: (layer, e, 0, 0))

    return pl.pallas_call(
        _experts_kernel,
        grid_spec=pltpu.PrefetchScalarGridSpec(
            num_scalar_prefetch=3,
            grid=(N_EXPERTS,),
            in_specs=[pl.BlockSpec(memory_space=pl.ANY),
                      wspec((d, EXPERT_FF)), wspec((d, EXPERT_FF)), wspec((EXPERT_FF, d))],
            out_specs=pl.BlockSpec(memory_space=pl.ANY),
            scratch_shapes=[pltpu.VMEM((2, tm * PACK_SUB, LANES), jnp.int32),
                            pltpu.VMEM((2, tm * PACK_SUB, LANES), jnp.int32),
                            pltpu.SemaphoreType.DMA((2,)),
                            pltpu.SemaphoreType.DMA((2,)),
                            pltpu.VMEM((d, EXPERT_FF), BF16),
                            pltpu.VMEM((d, EXPERT_FF), BF16),
                            pltpu.VMEM((EXPERT_FF, d), BF16)]),
        out_shape=jax.ShapeDtypeStruct((rows * PACK_SUB, LANES), jnp.int32),
        compiler_params=_params("arbitrary"),
        name="experts",
    )(tile_start, n_used, src_tok, h2, w_gate, w_up, w_down)


def _combine_kernel(slot_ref, ys_hbm, w_ref, x_ref, mod_ref, g_ref, b_ref, o_ref, ybuf, sem, *, n_tok):
    i = pl.program_id(0)
    nt = pl.num_programs(0)
    tm = ROW_TILE

    def issue(blk, buf):
        for r in range(tm):
            for k in range(TOP_K):
                row = slot_ref[k * n_tok + blk * tm + r]
                _row_copy(ys_hbm, row, ybuf.at[buf, k, pl.ds(r * PACK_SUB, PACK_SUB), :], sem.at[buf]).start()

    def wait(buf):
        for k in range(TOP_K):
            pltpu.make_async_copy(ys_hbm.at[pl.ds(0, tm * PACK_SUB), :], ybuf.at[buf, k], sem.at[buf]).wait()

    @pl.when(i == 0)
    def _():
        issue(0, 0)

    buf = i % 2
    wait(buf)
    issue(jnp.minimum(i + 1, nt - 1), 1 - buf)
    y = (w_ref[:, 0:1] * _load_packed(ybuf.at[buf, 0], tm)
         + w_ref[:, 1:2] * _load_packed(ybuf.at[buf, 1], tm))
    t = DEEPNORM_ALPHA * x_ref[...] + mod_ref[MOD_G2:MOD_G2 + 1, :] * y
    o_ref[...] = _layer_norm(t, g_ref[...], b_ref[...])

    @pl.when(i == nt - 1)
    def _():
        wait(1 - buf)


def _combine(ys, slots, w_tok, x1, mods, n_ctx_blocks, ln_g, ln_b):
    t = x1.shape[0]
    d = D_MODEL
    return pl.pallas_call(
        functools.partial(_combine_kernel, n_tok=t),
        grid_spec=pltpu.PrefetchScalarGridSpec(
            num_scalar_prefetch=1,
            grid=(t // ROW_TILE,),
            in_specs=[pl.BlockSpec(memory_space=pl.ANY),
                      pl.BlockSpec((ROW_TILE, TOP_K), lambda i, s: (i, 0)),
                      pl.BlockSpec((ROW_TILE, d), lambda i, s: (i, 0)),
                      pl.BlockSpec((None, 8, d), lambda i, s: (jnp.where(i < n_ctx_blocks, 0, 1), 0, 0)),
                      pl.BlockSpec((1, d), lambda i, s: (0, 0)),
                      pl.BlockSpec((1, d), lambda i, s: (0, 0))],
            out_specs=pl.BlockSpec((ROW_TILE, d), lambda i, s: (i, 0)),
            scratch_shapes=[pltpu.VMEM((2, TOP_K, ROW_TILE * PACK_SUB, LANES), jnp.int32),
                            pltpu.SemaphoreType.DMA((2,))]),
        out_shape=jax.ShapeDtypeStruct((t, d), F32),
        compiler_params=_params("arbitrary"),
        name="combine",
    )(slots, ys, w_tok, x1, mods, ln_g, ln_b)


def _moe(h2, x1, mods, n_ctx_blocks, wr_t, rb_b, w_gate, w_up, w_down, layer, ln_g, ln_b):
    t = x1.shape[0]
    tm = MOE_TILE
    e_idx, w_tok, rank, cnt = _router(h2, wr_t, rb_b)
    counts = cnt[:, 0].astype(jnp.int32)
    tiles_per = (counts + tm - 1) // tm
    tile_end = jnp.cumsum(tiles_per)
    n_used = tile_end[-1]
    row_off = (tile_end - tiles_per) * tm
    experts = jnp.arange(N_EXPERTS, dtype=jnp.int32)
    slots = jnp.sum(jnp.where(e_idx[:, :, None] == experts, row_off, 0), axis=-1) + rank
    max_tiles = (TOP_K * t + N_EXPERTS * (tm - 1)) // tm + 1
    tile_start = jnp.concatenate([tile_end - tiles_per, n_used.reshape(1)]).astype(jnp.int32)
    tok = jnp.tile(jnp.arange(t, dtype=jnp.int32), TOP_K)
    src_sub = jnp.zeros((max_tiles * tm,), jnp.int32).at[slots.reshape(-1)].set(tok * PACK_SUB)
    ys = _experts(h2, tile_start, n_used.reshape(1).astype(jnp.int32), src_sub, w_gate, w_up, w_down, layer)
    return _combine(ys, slots.reshape(-1) * PACK_SUB, w_tok.T, x1, mods, n_ctx_blocks, ln_g, ln_b)


def _rope_tables(n):
    t = jnp.arange(n)
    row = (t // GRID_W).astype(F32)
    col = (t % GRID_W).astype(F32)
    n_freq = HEAD_DIM // 4
    inv_freq = ROPE_BASE ** (-jnp.arange(n_freq, dtype=F32) / n_freq)
    ang = jnp.concatenate([row[:, None] * inv_freq, col[:, None] * inv_freq], axis=-1)
    cos, sin = jnp.cos(ang), jnp.sin(ang)
    return jnp.concatenate([cos, cos], axis=-1), jnp.concatenate([-sin, sin], axis=-1)


def kernel(x, c, ctx, c_ctx, w_mod, b_mod, w_in, attn_sink, na_rpb, sgu_ln_g, sgu_ln_b, sgu_w, sgu_b,
           w_out, ln1_g, ln1_b, w_router, router_bias, w_gate, w_up, w_down, ln2_g, ln2_b):
    batch, n, d = x.shape
    lctx = ctx.shape[1]
    assert batch == 1 and d == D_MODEL and n % ROW_TILE == 0 and lctx % ROW_TILE == 0
    n_ctx_blocks = lctx // ROW_TILE

    mods = _modulation(c, c_ctx, w_mod, b_mod).reshape(DEPTH, 8, 6, d)
    mod_lat = jnp.pad(mods[:, 0], ((0, 0), (0, 2), (0, 0)))
    mod_ctx = jnp.pad(mods[:, 1], ((0, 0), (0, 2), (0, 0)))
    cos, sin = _rope_tables(n)
    wr_t = w_router.T
    rb_b = jnp.broadcast_to(router_bias.reshape(N_EXPERTS, 1), (N_EXPERTS, LANES))

    x_lat, lat_off = x[0], 0
    x_ctx = ctx[0]
    for l in range(DEPTH):
        last = l == DEPTH - 1
        w_in_bf = w_in[l].astype(BF16)
        w_out_bf = w_out[l].astype(BF16)
        sgu_w_bf = sgu_w[l].astype(BF16)
        sgu_b_b = jnp.broadcast_to(sgu_b[l][:, :, None], (C_GROUPS, CHUNK, LANES))
        ln_g_c, ln_b_c = sgu_ln_g[l].reshape(1, C_W), sgu_ln_b[l].reshape(1, C_W)
        bias_tab = _na_bias_table(na_rpb[l])
        g1, b1 = ln1_g[l].reshape(1, d), ln1_b[l].reshape(1, d)
        g2, b2 = ln2_g[l].reshape(1, d), ln2_b[l].reshape(1, d)

        p = _proj(x_lat, lat_off, n, mod_lat[l], cos, sin, w_in_bf, rope=True)
        pc = _proj(x_ctx, 0, lctx, mod_ctx[l], cos, sin, w_in_bf, rope=False)
        o_a = _attn_a(p, pc, attn_sink[l], latent=True)
        o_b = _attn_b(p, pc, bias_tab, latent=True)
        o_c = _sgu(p, ln_g_c, ln_b_c, sgu_w_bf, sgu_b_b)
        if last:
            x1, h2 = _outproj(o_a, o_b, o_c, w_out_bf, x_lat, lat_off, mod_lat[l], g1, b1, n, 0, None)
            x_lat = _moe(h2, x1, jnp.stack([mod_ctx[l], mod_lat[l]]), 0, wr_t, rb_b,
                         w_gate, w_up, w_down, l, g2, b2)
        else:
            oc_a = _attn_a(pc, pc, attn_sink[l], latent=False)
            oc_b = _attn_b(pc, pc, bias_tab, latent=False)
            oc_c = _sgu(pc, ln_g_c, ln_b_c, sgu_w_bf, sgu_b_b)
            total = lctx + n
            prev = _outproj(oc_a, oc_b, oc_c, w_out_bf, x_ctx, 0, mod_ctx[l], g1, b1, total, 0, None)
            x1, h2 = _outproj(o_a, o_b, o_c, w_out_bf, x_lat, lat_off, mod_lat[l], g1, b1, total,
                              n_ctx_blocks, prev)
            x_all = _moe(h2, x1, jnp.stack([mod_ctx[l], mod_lat[l]]), n_ctx_blocks, wr_t, rb_b,
                         w_gate, w_up, w_down, l, g2, b2)
            x_lat, lat_off, x_ctx = x_all, n_ctx_blocks, x_all
    return x_lat.reshape(batch, n, d)
```

```python
import functools

import numpy as np
import jax
import jax.numpy as jnp
from jax import lax
from jax.experimental import pallas as pl
from jax.experimental.pallas import tpu as pltpu

F32 = jnp.float32
BF16 = jnp.bfloat16

D_MODEL = 2048
DEPTH = 2
GRID_W = 64
HEAD_DIM = 128
A_HEADS = 6
A_KV_HEADS = 2
A_GROUP = A_HEADS // A_KV_HEADS
A_BLOCK = 128
B_HEADS = 6
NA_ROWS = 8
NA_COLS = 16
C_GROUPS = 4
C_W = C_GROUPS * HEAD_DIM
CHUNK = 128
N_EXPERTS = 32
N_EXPERT_GROUPS = 4
EXPERTS_PER_GROUP = N_EXPERTS // N_EXPERT_GROUPS
TOP_K = 2
EXPERT_FF = 512
ROPE_BASE = 10000.0
LN_EPS = 1e-5
NEG_INF = -1e30
DEEPNORM_ALPHA = (2 * DEPTH) ** 0.25
ATTN_SCALE = HEAD_DIM ** -0.5

A_Q_W = A_HEADS * HEAD_DIM
A_KV_W = A_KV_HEADS * HEAD_DIM
B_W = B_HEADS * HEAD_DIM
OFF_AK = A_Q_W
OFF_AV = OFF_AK + A_KV_W
OFF_BQ = OFF_AV + A_KV_W
OFF_BK = OFF_BQ + B_W
OFF_BV = OFF_BK + B_W
OFF_C = OFF_BV + B_W
IN_COLS = OFF_C + 2 * C_W

VMEM_LIMIT_BYTES = 56 * 1024 * 1024
LANES = 128

ROW_TILE = 256
PROJ_COL_TILE = 512
MOD_COL_TILE = 1024
MOE_TILE = 256

MOD_SH1, MOD_SC1, MOD_G1, MOD_SH2, MOD_SC2, MOD_G2 = range(6)


def _params(*sem):
    return pltpu.CompilerParams(dimension_semantics=sem, vmem_limit_bytes=VMEM_LIMIT_BYTES)


def _layer_norm(t, g, b):
    mu = jnp.mean(t, axis=-1, keepdims=True)
    d = t - mu
    var = jnp.mean(d * d, axis=-1, keepdims=True)
    return d * lax.rsqrt(var + LN_EPS) * g + b


def _sigmoid(v):
    return 1.0 / (1.0 + jnp.exp(-v))


def _dot_nt(a, b):
    return lax.dot_general(a, b, (((1,), (1,)), ((), ())), preferred_element_type=F32)


PACK_W = D_MODEL // 2
PACK_SUB = PACK_W // LANES


def _pack_rows(v):
    bits = pltpu.bitcast(v.astype(BF16).astype(F32), jnp.int32)
    return lax.shift_right_logical(bits[:, :PACK_W], 16) | bits[:, PACK_W:]


def _store_packed(ref, words, m):
    for s in range(PACK_SUB):
        ref[pl.ds(s, m, stride=PACK_SUB), :] = words[:, s * LANES:(s + 1) * LANES]


def _load_packed(ref, m):
    lo, hi = [], []
    for s in range(PACK_SUB):
        w = ref[pl.ds(s, m, stride=PACK_SUB), :]
        lo.append(pltpu.bitcast(w << 16, F32))
        hi.append(pltpu.bitcast(w & jnp.int32(-65536), F32))
    return jnp.concatenate(lo + hi, axis=1)


def _mod_kernel(c_ref, w_ref, b_ref, o_ref):
    w = w_ref[...]
    reps = w.shape[1] // LANES
    rows = []
    for r in range(2):
        cv = c_ref[r]
        s = cv * _sigmoid(cv)
        sb = jnp.concatenate([s] * reps, axis=1)
        rows.append(jnp.sum(w * sb, axis=0, keepdims=True) + b_ref[...])
    rows.append(jnp.zeros((6, w.shape[1]), F32))
    o_ref[...] = jnp.concatenate(rows, axis=0)


def _modulation(c, c_ctx, w_mod, b_mod):
    d = D_MODEL
    cb = jnp.stack([jnp.broadcast_to(c.reshape(d, 1), (d, LANES)),
                    jnp.broadcast_to(c_ctx.reshape(d, 1), (d, LANES))])
    n_out = 6 * d
    return pl.pallas_call(
        _mod_kernel,
        grid=(DEPTH, n_out // MOD_COL_TILE),
        in_specs=[pl.BlockSpec((2, d, LANES), lambda l, j: (0, 0, 0)),
                  pl.BlockSpec((None, d, MOD_COL_TILE), lambda l, j: (l, 0, j)),
                  pl.BlockSpec((None, 1, MOD_COL_TILE), lambda l, j: (l, 0, j))],
        out_specs=pl.BlockSpec((None, 8, MOD_COL_TILE), lambda l, j: (l, 0, j)),
        out_shape=jax.ShapeDtypeStruct((DEPTH, 8, n_out), F32),
        compiler_params=_params("arbitrary", "arbitrary"),
        name="modulation",
    )(cb, w_mod, b_mod.reshape(DEPTH, 1, n_out))


def _gelu_tanh(v):
    return 0.5 * v * (1.0 + jnp.tanh(np.sqrt(2.0 / np.pi).astype(np.float32) * (v + 0.044715 * (v * v * v))))


def _proj_kernel(x_ref, mod_ref, cos_ref, sin_ref, w_ref, o_ref, *, rope):
    x = x_ref[...]
    h = (x * (1.0 + mod_ref[MOD_SC1:MOD_SC1 + 1, :]) + mod_ref[MOD_SH1:MOD_SH1 + 1, :]).astype(BF16)
    tn = PROJ_COL_TILE
    for j in range(IN_COLS // tn):
        c0 = j * tn
        acc = jnp.dot(h, w_ref[:, c0:c0 + tn], preferred_element_type=F32)
        if c0 < OFF_AV:
            if rope:
                cos = cos_ref[...]
                sin = sin_ref[...]
                parts = []
                for hh in range(tn // HEAD_DIM):
                    a = acc[:, hh * HEAD_DIM:(hh + 1) * HEAD_DIM]
                    parts.append(a * cos + pltpu.roll(a, HEAD_DIM // 2, 1) * sin)
                acc = jnp.concatenate(parts, axis=1)
        elif c0 >= OFF_C:
            acc = _gelu_tanh(acc)
        o_ref[:, c0:c0 + tn] = acc.astype(BF16)


def _proj(x, x_off, rows, mod, cos, sin, w_bf, rope):
    d = D_MODEL
    return pl.pallas_call(
        functools.partial(_proj_kernel, rope=rope),
        grid=(rows // ROW_TILE,),
        in_specs=[pl.BlockSpec((ROW_TILE, d), lambda i: (i + x_off, 0)),
                  pl.BlockSpec((8, d), lambda i: (0, 0)),
                  pl.BlockSpec((ROW_TILE, HEAD_DIM), lambda i: (i, 0)),
                  pl.BlockSpec((ROW_TILE, HEAD_DIM), lambda i: (i, 0)),
                  pl.BlockSpec((d, IN_COLS), lambda i: (0, 0), pipeline_mode=pl.Buffered(1))],
        out_specs=pl.BlockSpec((ROW_TILE, IN_COLS), lambda i: (i, 0)),
        out_shape=jax.ShapeDtypeStruct((rows, IN_COLS), BF16),
        compiler_params=_params("arbitrary"),
        name="proj_rope" if rope else "proj_ctx",
    )(x, mod, cos, sin, w_bf)


def _softmax_pv(s_parts, v_parts, sink):
    m = s_parts[0].max(axis=-1, keepdims=True)
    for s in s_parts[1:]:
        m = jnp.maximum(m, s.max(axis=-1, keepdims=True))
    if sink is not None:
        m = jnp.maximum(m, sink)
    denom = None if sink is None else jnp.exp(sink - m)
    out = None
    for s, v in zip(s_parts, v_parts):
        e = jnp.exp(s - m)
        es = e.sum(axis=-1, keepdims=True)
        denom = es if denom is None else denom + es
        pv = jnp.dot(e.astype(BF16), v, preferred_element_type=F32)
        out = pv if out is None else out + pv
    return out / denom


def _attn_a_kernel(sink_ref, q_ref, kp_ref, kc_ref, kn_ref, vp_ref, vc_ref, vn_ref, kx_ref, vx_ref, o_ref,
                   mask_ref, *, latent):
    i = pl.program_id(0)
    nb = pl.num_programs(0)
    nq = A_GROUP * A_BLOCK

    if latent:
        @pl.when(i == 0)
        def _():
            qi = lax.broadcasted_iota(jnp.int32, (nq, 3 * A_BLOCK), 0) % A_BLOCK
            jj = lax.broadcasted_iota(jnp.int32, (nq, 3 * A_BLOCK), 1)
            ok = (jj >= qi) & (jj <= qi + 2 * A_BLOCK)
            mask_ref[...] = jnp.where(ok, 0.0, NEG_INF).astype(F32)

        col = lax.broadcasted_iota(jnp.int32, (nq, 3 * A_BLOCK), 1)
        off_band = ((col < A_BLOCK) & (i == 0)) | ((col >= 2 * A_BLOCK) & (i == nb - 1))

    for kh in range(A_KV_HEADS):
        hs = [kh * A_GROUP + g for g in range(A_GROUP)]
        q = jnp.concatenate([q_ref[:, h * HEAD_DIM:(h + 1) * HEAD_DIM] for h in hs], axis=0)
        sink = jnp.concatenate([jnp.full((A_BLOCK, 1), sink_ref[h], F32) for h in hs], axis=0)
        ks = slice(kh * HEAD_DIM, (kh + 1) * HEAD_DIM)
        s_parts = [_dot_nt(q, kx_ref[:, ks]) * ATTN_SCALE]
        v_parts = [vx_ref[:, ks]]
        if latent:
            kband = jnp.concatenate([kp_ref[:, ks], kc_ref[:, ks], kn_ref[:, ks]], axis=0)
            vband = jnp.concatenate([vp_ref[:, ks], vc_ref[:, ks], vn_ref[:, ks]], axis=0)
            s_loc = _dot_nt(q, kband) * ATTN_SCALE + mask_ref[...]
            s_parts.append(jnp.where(off_band, NEG_INF, s_loc))
            v_parts.append(vband)
        out = _softmax_pv(s_parts, v_parts, sink)
        for g, h in enumerate(hs):
            o_ref[:, h * HEAD_DIM:(h + 1) * HEAD_DIM] = out[g * A_BLOCK:(g + 1) * A_BLOCK].astype(BF16)


def _attn_a(p, pc, sink, latent):
    rows = p.shape[0]
    nb = rows // A_BLOCK
    kcol = OFF_AK // A_KV_W
    vcol = OFF_AV // A_KV_W

    def band(col, shift):
        return pl.BlockSpec((A_BLOCK, A_KV_W), lambda i, s: (jnp.clip(i + shift, 0, nb - 1), col))

    lctx = pc.shape[0]
    return pl.pallas_call(
        functools.partial(_attn_a_kernel, latent=latent),
        grid_spec=pltpu.PrefetchScalarGridSpec(
            num_scalar_prefetch=1,
            grid=(nb,),
            in_specs=[pl.BlockSpec((A_BLOCK, A_Q_W), lambda i, s: (i, 0)),
                      band(kcol, -1), band(kcol, 0), band(kcol, 1),
                      band(vcol, -1), band(vcol, 0), band(vcol, 1),
                      pl.BlockSpec((lctx, A_KV_W), lambda i, s: (0, kcol)),
                      pl.BlockSpec((lctx, A_KV_W), lambda i, s: (0, vcol))],
            out_specs=pl.BlockSpec((A_BLOCK, A_Q_W), lambda i, s: (i, 0)),
            scratch_shapes=[pltpu.VMEM((A_GROUP * A_BLOCK, 3 * A_BLOCK), F32)]),
        out_shape=jax.ShapeDtypeStruct((rows, A_Q_W), BF16),
        compiler_params=_params("arbitrary"),
        name="attn_a_latent" if latent else "attn_a_ctx",
    )(sink, p, p, p, p, p, p, p, pc, pc)


B_PAIR_W = 2 * HEAD_DIM
NA_GROUP_ROWS = ROW_TILE // GRID_W
NA_WIN_ROWS = NA_ROWS + NA_GROUP_ROWS
NA_PAIRS = NA_WIN_ROWS // 2
NA_BIAS_OFFS = 2 * NA_ROWS
NA_BOTH, NA_LEFT, NA_RIGHT = range(3)


def _attn_b_kernel(q_ref, k_ref, v_ref, kx_ref, vx_ref, bias_ref, o_ref, *, latent, grid_rows):
    g = pl.program_id(1)
    if latent:
        r_base = g * NA_GROUP_ROWS
        w0 = jnp.clip(r_base - NA_ROWS // 2, 0, grid_rows - NA_WIN_ROWS)
        start = pl.multiple_of(w0 * GRID_W, GRID_W)
    for hh in range(2):
        hs = slice(hh * HEAD_DIM, (hh + 1) * HEAD_DIM)
        q = q_ref[:, hs]
        s_parts = [_dot_nt(q, kx_ref[:, hs]) * ATTN_SCALE]
        v_parts = [vx_ref[:, hs]]
        if latent:
            kwin = k_ref[pl.ds(start, NA_WIN_ROWS * GRID_W), hs]
            vwin = v_ref[pl.ds(start, NA_WIN_ROWS * GRID_W), hs]
            bias_rows = []
            for rr in range(NA_GROUP_ROWS):
                r = r_base + rr
                r0 = jnp.clip(r - NA_ROWS // 2, 0, grid_rows - NA_ROWS)
                tiles = []
                for jp in range(NA_PAIRS):
                    ka = w0 + 2 * jp
                    in_a = (ka >= r0) & (ka < r0 + NA_ROWS)
                    in_b = (ka + 1 >= r0) & (ka + 1 < r0 + NA_ROWS)
                    variant = jnp.where(in_a, jnp.where(in_b, NA_BOTH, NA_LEFT), jnp.where(in_b, NA_RIGHT, NA_LEFT))
                    off = jnp.where(in_a | in_b, jnp.clip(ka - r + NA_ROWS, 0, NA_BIAS_OFFS - 1), 0)
                    tiles.append(bias_ref[hh, variant, off])
                bias_rows.append(jnp.concatenate(tiles, axis=1))
            bias = jnp.concatenate(bias_rows, axis=0)
            s_parts.append(_dot_nt(q, kwin) * ATTN_SCALE + bias)
            v_parts.append(vwin)
        out = _softmax_pv(s_parts, v_parts, None)
        o_ref[:, hs] = out.astype(BF16)


def _na_bias_table(rpb):
    cols = np.arange(GRID_W)
    c0 = np.clip(cols - NA_COLS // 2, 0, GRID_W - NA_COLS)
    rel = cols[None, :] - cols[:, None] + NA_COLS - 1
    ok = (cols[None, :] >= c0[:, None]) & (cols[None, :] < c0[:, None] + NA_COLS)
    onehot = (rel[None] == np.arange(2 * NA_COLS - 1)[:, None, None]).astype(np.float32)
    t = jnp.einsum("hrd,dqk->hrqk", rpb, onehot, precision=lax.Precision.HIGHEST)
    t = jnp.where(ok[None, None], t, NEG_INF).astype(F32)
    t = jnp.pad(t, ((0, 0), (1, 1), (0, 0), (0, 0)), constant_values=NEG_INF)
    first, second = t[:, :-1], t[:, 1:]
    masked = jnp.full_like(first, NEG_INF)
    return jnp.stack([jnp.concatenate([first, second], axis=-1),
                      jnp.concatenate([first, masked], axis=-1),
                      jnp.concatenate([masked, second], axis=-1)], axis=1)


def _attn_b(p, pc, bias_tab, latent):
    rows = p.shape[0]
    lctx = pc.shape[0]
    assert not latent or rows // GRID_W >= NA_WIN_ROWS
    qcol = OFF_BQ // B_PAIR_W
    kcol = OFF_BK // B_PAIR_W
    vcol = OFF_BV // B_PAIR_W
    return pl.pallas_call(
        functools.partial(_attn_b_kernel, latent=latent, grid_rows=rows // GRID_W),
        grid=(B_HEADS // 2, rows // ROW_TILE),
        in_specs=[pl.BlockSpec((ROW_TILE, B_PAIR_W), lambda hp, g: (g, qcol + hp)),
                  pl.BlockSpec((rows, B_PAIR_W), lambda hp, g: (0, kcol + hp)),
                  pl.BlockSpec((rows, B_PAIR_W), lambda hp, g: (0, vcol + hp)),
                  pl.BlockSpec((lctx, B_PAIR_W), lambda hp, g: (0, kcol + hp)),
                  pl.BlockSpec((lctx, B_PAIR_W), lambda hp, g: (0, vcol + hp)),
                  pl.BlockSpec((2, 3, NA_BIAS_OFFS, GRID_W, 2 * GRID_W), lambda hp, g: (hp, 0, 0, 0, 0))],
        out_specs=pl.BlockSpec((ROW_TILE, B_PAIR_W), lambda hp, g: (g, hp)),
        out_shape=jax.ShapeDtypeStruct((rows, B_W), BF16),
        compiler_params=_params("arbitrary", "arbitrary"),
        name="attn_b_latent" if latent else "attn_b_ctx",
    )(p, p, p, pc, pc, bias_tab)


def _sgu_kernel(u_ref, v_ref, g_ref, b_ref, w_ref, bs_ref, o_ref):
    for ch in range(ROW_TILE // CHUNK):
        rs = slice(ch * CHUNK, (ch + 1) * CHUNK)
        for grp in range(C_GROUPS):
            cs = slice(grp * HEAD_DIM, (grp + 1) * HEAD_DIM)
            vn = _layer_norm(v_ref[rs, cs].astype(F32), g_ref[:, cs], b_ref[:, cs])
            mixed = jnp.dot(w_ref[grp], vn.astype(BF16), preferred_element_type=F32) + bs_ref[grp]
            o_ref[rs, cs] = (u_ref[rs, cs].astype(F32) * mixed).astype(BF16)


def _sgu(p, ln_g, ln_b, w_bf, bs_b):
    rows = p.shape[0]
    ucol = OFF_C // C_W
    return pl.pallas_call(
        _sgu_kernel,
        grid=(rows // ROW_TILE,),
        in_specs=[pl.BlockSpec((ROW_TILE, C_W), lambda i: (i, ucol)),
                  pl.BlockSpec((ROW_TILE, C_W), lambda i: (i, ucol + 1)),
                  pl.BlockSpec((1, C_W), lambda i: (0, 0)),
                  pl.BlockSpec((1, C_W), lambda i: (0, 0)),
                  pl.BlockSpec((C_GROUPS, CHUNK, CHUNK), lambda i: (0, 0, 0)),
                  pl.BlockSpec((C_GROUPS, CHUNK, LANES), lambda i: (0, 0, 0))],
        out_specs=pl.BlockSpec((ROW_TILE, C_W), lambda i: (i, 0)),
        out_shape=jax.ShapeDtypeStruct((rows, C_W), BF16),
        compiler_params=_params("arbitrary"),
        name="sgu",
    )(p, p, ln_g, ln_b, w_bf, bs_b)


def _outproj_kernel(oa_ref, ob_ref, oc_ref, w_ref, x_ref, mod_ref, g_ref, b_ref, *rest):
    x1_ref, h2_ref = rest[-2], rest[-1]
    mix = jnp.dot(oa_ref[...], w_ref[0:A_Q_W, :], preferred_element_type=F32)
    mix += jnp.dot(ob_ref[...], w_ref[A_Q_W:A_Q_W + B_W, :], preferred_element_type=F32)
    mix += jnp.dot(oc_ref[...], w_ref[A_Q_W + B_W:, :], preferred_element_type=F32)
    t = DEEPNORM_ALPHA * x_ref[...] + mod_ref[MOD_G1:MOD_G1 + 1, :] * mix
    x1 = _layer_norm(t, g_ref[...], b_ref[...])
    x1_ref[...] = x1
    h2 = x1 * (1.0 + mod_ref[MOD_SC2:MOD_SC2 + 1, :]) + mod_ref[MOD_SH2:MOD_SH2 + 1, :]
    _store_packed(h2_ref, _pack_rows(h2), ROW_TILE)


def _outproj(o_a, o_b, o_c, w_bf, x, x_off, mod, ln_g, ln_b, total_rows, out_off, prev):
    rows = o_a.shape[0]
    d = D_MODEL
    in_specs = [pl.BlockSpec((ROW_TILE, A_Q_W), lambda i: (i, 0)),
                pl.BlockSpec((ROW_TILE, B_W), lambda i: (i, 0)),
                pl.BlockSpec((ROW_TILE, C_W), lambda i: (i, 0)),
                pl.BlockSpec((d, d), lambda i: (0, 0), pipeline_mode=pl.Buffered(1)),
                pl.BlockSpec((ROW_TILE, d), lambda i: (i + x_off, 0)),
                pl.BlockSpec((8, d), lambda i: (0, 0)),
                pl.BlockSpec((1, d), lambda i: (0, 0)),
                pl.BlockSpec((1, d), lambda i: (0, 0))]
    args = [o_a, o_b, o_c, w_bf, x, mod, ln_g, ln_b]
    aliases = {}
    if prev is not None:
        in_specs += [pl.BlockSpec(memory_space=pl.ANY), pl.BlockSpec(memory_space=pl.ANY)]
        aliases = {len(args): 0, len(args) + 1: 1}
        args += list(prev)
    return pl.pallas_call(
        _outproj_kernel,
        grid=(rows // ROW_TILE,),
        in_specs=in_specs,
        out_specs=[pl.BlockSpec((ROW_TILE, d), lambda i: (i + out_off, 0)),
                   pl.BlockSpec((ROW_TILE * PACK_SUB, LANES), lambda i: (i + out_off, 0))],
        out_shape=[jax.ShapeDtypeStruct((total_rows, d), F32),
                   jax.ShapeDtypeStruct((total_rows * PACK_SUB, LANES), jnp.int32)],
        input_output_aliases=aliases,
        compiler_params=_params("arbitrary"),
        name="outproj",
    )(*args)


def _top2_sublanes(vals, sub):
    m1 = vals.max(axis=0, keepdims=True)
    i1 = jnp.where(vals == m1, sub, vals.shape[0]).min(axis=0, keepdims=True)
    rest = jnp.where(sub == i1, -jnp.inf, vals)
    m2 = rest.max(axis=0, keepdims=True)
    i2 = jnp.where(rest == m2, sub, vals.shape[0]).min(axis=0, keepdims=True)
    return m1, i1, m2, i2


def _router_kernel(h_ref, wr_ref, rb_ref, e_ref, w_ref, rank_ref, cnt_ref, run_ref):
    i = pl.program_id(0)
    tm = ROW_TILE
    epg = EXPERTS_PER_GROUP

    @pl.when(i == 0)
    def _():
        run_ref[...] = jnp.zeros_like(run_ref)

    parts = _dot_nt(wr_ref[...], _load_packed(h_ref, tm).astype(BF16))
    logits = parts[0:N_EXPERTS] + parts[N_EXPERTS:2 * N_EXPERTS] + parts[2 * N_EXPERTS:]
    scores = _sigmoid(logits)
    biased = scores + jnp.concatenate([rb_ref[...]] * (tm // LANES), axis=1)
    sub = lax.broadcasted_iota(jnp.int32, (epg, tm), 0)

    best = None
    for g in range(N_EXPERT_GROUPS):
        m1, _, m2, _ = _top2_sublanes(biased[g * epg:(g + 1) * epg], sub)
        gs = m1 + m2
        if best is None:
            best, grp = gs, jnp.zeros((1, tm), jnp.int32)
            bsel, ssel = biased[0:epg], scores[0:epg]
        else:
            better = gs > best
            best = jnp.where(better, gs, best)
            grp = jnp.where(better, g, grp)
            bsel = jnp.where(better, biased[g * epg:(g + 1) * epg], bsel)
            ssel = jnp.where(better, scores[g * epg:(g + 1) * epg], ssel)
    _, i1, _, i2 = _top2_sublanes(bsel, sub)
    w1 = jnp.where(sub == i1, ssel, 0.0).sum(axis=0, keepdims=True)
    w2 = jnp.where(sub == i2, ssel, 0.0).sum(axis=0, keepdims=True)
    tot = w1 + w2
    e1 = grp * epg + i1
    e2 = grp * epg + i2

    eiota = lax.broadcasted_iota(jnp.int32, (N_EXPERTS, tm), 0)
    oh1 = (eiota == e1).astype(F32)
    oh2 = (eiota == e2).astype(F32)
    ohb = oh1 + oh2
    before = (lax.broadcasted_iota(jnp.int32, (tm, tm), 0) < lax.broadcasted_iota(jnp.int32, (tm, tm), 1))
    prefix = jnp.dot(ohb.astype(BF16), before.astype(BF16), preferred_element_type=F32)
    pos = run_ref[...] + prefix
    r1 = (oh1 * pos).sum(axis=0, keepdims=True)
    r2 = (oh2 * pos).sum(axis=0, keepdims=True)
    run_ref[...] = run_ref[...] + ohb.sum(axis=1, keepdims=True)

    e_ref[...] = jnp.concatenate([e1, e2], axis=0)
    w_ref[...] = jnp.concatenate([w1 / tot, w2 / tot], axis=0)
    rank_ref[...] = jnp.concatenate([r1, r2], axis=0).astype(jnp.int32)
    cnt_ref[...] = run_ref[:, 0:LANES]


def _router(h2p, wr_t, rb_b):
    t = h2p.shape[0] // PACK_SUB
    d = D_MODEL
    row2 = pl.BlockSpec((TOP_K, ROW_TILE), lambda i: (0, i))
    return pl.pallas_call(
        _router_kernel,
        grid=(t // ROW_TILE,),
        in_specs=[pl.BlockSpec((ROW_TILE * PACK_SUB, LANES), lambda i: (i, 0)),
                  pl.BlockSpec((3 * N_EXPERTS, d), lambda i: (0, 0)),
                  pl.BlockSpec((N_EXPERTS, LANES), lambda i: (0, 0))],
        out_specs=[row2, row2, row2, pl.BlockSpec((N_EXPERTS, LANES), lambda i: (0, 0))],
        out_shape=[jax.ShapeDtypeStruct((TOP_K, t), jnp.int32),
                   jax.ShapeDtypeStruct((TOP_K, t), F32),
                   jax.ShapeDtypeStruct((TOP_K, t), jnp.int32),
                   jax.ShapeDtypeStruct((N_EXPERTS, LANES), F32)],
        scratch_shapes=[pltpu.VMEM((N_EXPERTS, ROW_TILE), F32)],
        compiler_params=_params("arbitrary"),
        name="router",
    )(h2p, wr_t, rb_b)


def _row_copy(src_hbm, first_sublane, dst, sem):
    return pltpu.make_async_copy(src_hbm.at[pl.ds(pl.multiple_of(first_sublane, PACK_SUB), PACK_SUB), :], dst, sem)


def _experts_kernel(ts_ref, nu_ref, src_ref, h_hbm, wg_ref, wu_ref, wd_ref, ys_hbm,
                    xbuf, ybuf, gsem, ysem, wg_bf, wu_bf, wd_bf):
    e = pl.program_id(0)
    n_used = nu_ref[0]
    tm = MOE_TILE

    tsub = tm * PACK_SUB

    def gather(tile, slot):
        base = tile * tm
        for r in range(tm):
            _row_copy(h_hbm, src_ref[base + r], xbuf.at[slot, pl.ds(r * PACK_SUB, PACK_SUB), :], gsem.at[slot]).start()

    def wait_gather(slot):
        pltpu.make_async_copy(h_hbm.at[pl.ds(0, tsub), :], xbuf.at[slot], gsem.at[slot]).wait()

    def y_store(tile, slot):
        row0 = pl.multiple_of(tile * tsub, tsub)
        return pltpu.make_async_copy(ybuf.at[slot], ys_hbm.at[pl.ds(row0, tsub), :], ysem.at[slot])

    @pl.when((e == 0) & (n_used > 0))
    def _():
        gather(0, 0)

    @pl.when(ts_ref[e + 1] > ts_ref[e])
    def _():
        wg_bf[...] = wg_ref[...].astype(BF16)
        wu_bf[...] = wu_ref[...].astype(BF16)
        wd_bf[...] = wd_ref[...].astype(BF16)

    def tile_body(g, carry):
        slot = g % 2

        @pl.when(g >= 2)
        def _():
            y_store(g - 2, slot).wait()

        wait_gather(slot)
        x = _load_packed(xbuf.at[slot], tm).astype(BF16)
        gather(jnp.minimum(g + 1, n_used - 1), 1 - slot)
        gate = jnp.dot(x, wg_bf[...], preferred_element_type=F32)
        up = jnp.dot(x, wu_bf[...], preferred_element_type=F32)
        hid = (gate * _sigmoid(gate)) * up
        y = jnp.dot(hid.astype(BF16), wd_bf[...], preferred_element_type=F32)
        _store_packed(ybuf.at[slot], _pack_rows(y), tm)
        y_store(g, slot).start()
        return carry

    lax.fori_loop(ts_ref[e], ts_ref[e + 1], tile_body, 0)

    @pl.when((e == pl.num_programs(0) - 1) & (n_used > 0))
    def _():
        wait_gather(n_used % 2)
        y_store(n_used - 1, (n_used - 1) % 2).wait()

        @pl.when(n_used >= 2)
        def _():
            y_store(n_used - 2, n_used % 2).wait()


def _experts(h2, tile_start, n_used, src_tok, w_gate, w_up, w_down, layer):
    d = D_MODEL
    tm = MOE_TILE
    rows = src_tok.shape[0]

    def wspec(shape):
        return pl.BlockSpec((None, None) + shape, lambda e, ts, nu, src: (layer, e, 0, 0))

    return pl.pallas_call(
        _experts_kernel,
        grid_spec=pltpu.PrefetchScalarGridSpec(
            num_scalar_prefetch=3,
            grid=(N_EXPERTS,),
            in_specs=[pl.BlockSpec(memory_space=pl.ANY),
                      wspec((d, EXPERT_FF)), wspec((d, EXPERT_FF)), wspec((EXPERT_FF, d))],
            out_specs=pl.BlockSpec(memory_space=pl.ANY),
            scratch_shapes=[pltpu.VMEM((2, tm * PACK_SUB, LANES), jnp.int32),
                            pltpu.VMEM((2, tm * PACK_SUB, LANES), jnp.int32),
                            pltpu.SemaphoreType.DMA((2,)),
                            pltpu.SemaphoreType.DMA((2,)),
                            pltpu.VMEM((d, EXPERT_FF), BF16),
                            pltpu.VMEM((d, EXPERT_FF), BF16),
                            pltpu.VMEM((EXPERT_FF, d), BF16)]),
        out_shape=jax.ShapeDtypeStruct((rows * PACK_SUB, LANES), jnp.int32),
        compiler_params=_params("arbitrary"),
        name="experts",
    )(tile_start, n_used, src_tok, h2, w_gate, w_up, w_down)


def _combine_kernel(slot_ref, ys_hbm, w_ref, x_ref, mod_ref, g_ref, b_ref, o_ref, ybuf, sem, *, n_tok):
    i = pl.program_id(0)
    nt = pl.num_programs(0)
    tm = ROW_TILE

    def issue(blk, buf):
        for r in range(tm):
            for k in range(TOP_K):
                row = slot_ref[k * n_tok + blk * tm + r]
                _row_copy(ys_hbm, row, ybuf.at[buf, k, pl.ds(r * PACK_SUB, PACK_SUB), :], sem.at[buf]).start()

    def wait(buf):
        for k in range(TOP_K):
            pltpu.make_async_copy(ys_hbm.at[pl.ds(0, tm * PACK_SUB), :], ybuf.at[buf, k], sem.at[buf]).wait()

    @pl.when(i == 0)
    def _():
        issue(0, 0)

    buf = i % 2
    wait(buf)
    y = (w_ref[:, 0:1] * _load_packed(ybuf.at[buf, 0], tm)
         + w_ref[:, 1:2] * _load_packed(ybuf.at[buf, 1], tm))
    issue(jnp.minimum(i + 1, nt - 1), 1 - buf)
    t = DEEPNORM_ALPHA * x_ref[...] + mod_ref[MOD_G2:MOD_G2 + 1, :] * y
    o_ref[...] = _layer_norm(t, g_ref[...], b_ref[...])

    @pl.when(i == nt - 1)
    def _():
        wait(1 - buf)


def _combine(ys, slots, w_tok, x1, mods, n_ctx_blocks, ln_g, ln_b):
    t = x1.shape[0]
    d = D_MODEL
    return pl.pallas_call(
        functools.partial(_combine_kernel, n_tok=t),
        grid_spec=pltpu.PrefetchScalarGridSpec(
            num_scalar_prefetch=1,
            grid=(t // ROW_TILE,),
            in_specs=[pl.BlockSpec(memory_space=pl.ANY),
                      pl.BlockSpec((ROW_TILE, TOP_K), lambda i, s: (i, 0)),
                      pl.BlockSpec((ROW_TILE, d), lambda i, s: (i, 0)),
                      pl.BlockSpec((None, 8, d), lambda i, s: (jnp.where(i < n_ctx_blocks, 0, 1), 0, 0)),
                      pl.BlockSpec((1, d), lambda i, s: (0, 0)),
                      pl.BlockSpec((1, d), lambda i, s: (0, 0))],
            out_specs=pl.BlockSpec((ROW_TILE, d), lambda i, s: (i, 0)),
            scratch_shapes=[pltpu.VMEM((2, TOP_K, ROW_TILE * PACK_SUB, LANES), jnp.int32),
                            pltpu.SemaphoreType.DMA((2,))]),
        out_shape=jax.ShapeDtypeStruct((t, d), F32),
        compiler_params=_params("arbitrary"),
        name="combine",
    )(slots, ys, w_tok, x1, mods, ln_g, ln_b)


def _moe(h2, x1, mods, n_ctx_blocks, wr_t, rb_b, w_gate, w_up, w_down, layer, ln_g, ln_b):
    t = x1.shape[0]
    tm = MOE_TILE
    e_idx, w_tok, rank, cnt = _router(h2, wr_t, rb_b)
    counts = cnt[:, 0].astype(jnp.int32)
    tiles_per = (counts + tm - 1) // tm
    tile_end = jnp.cumsum(tiles_per)
    n_used = tile_end[-1]
    row_off = (tile_end - tiles_per) * tm
    experts = jnp.arange(N_EXPERTS, dtype=jnp.int32)
    slots = jnp.sum(jnp.where(e_idx[:, :, None] == experts, row_off, 0), axis=-1) + rank
    max_tiles = (TOP_K * t + N_EXPERTS * (tm - 1)) // tm + 1
    tile_start = jnp.concatenate([tile_end - tiles_per, n_used.reshape(1)]).astype(jnp.int32)
    tok = jnp.tile(jnp.arange(t, dtype=jnp.int32), TOP_K)
    src_sub = jnp.zeros((max_tiles * tm,), jnp.int32).at[slots.reshape(-1)].set(tok * PACK_SUB)
    ys = _experts(h2, tile_start, n_used.reshape(1).astype(jnp.int32), src_sub, w_gate, w_up, w_down, layer)
    return _combine(ys, slots.reshape(-1) * PACK_SUB, w_tok.T, x1, mods, n_ctx_blocks, ln_g, ln_b)


def _rope_tables(n):
    t = jnp.arange(n)
    row = (t // GRID_W).astype(F32)
    col = (t % GRID_W).astype(F32)
    n_freq = HEAD_DIM // 4
    inv_freq = ROPE_BASE ** (-jnp.arange(n_freq, dtype=F32) / n_freq)
    ang = jnp.concatenate([row[:, None] * inv_freq, col[:, None] * inv_freq], axis=-1)
    cos, sin = jnp.cos(ang), jnp.sin(ang)
    return jnp.concatenate([cos, cos], axis=-1), jnp.concatenate([-sin, sin], axis=-1)


def kernel(x, c, ctx, c_ctx, w_mod, b_mod, w_in, attn_sink, na_rpb, sgu_ln_g, sgu_ln_b, sgu_w, sgu_b,
           w_out, ln1_g, ln1_b, w_router, router_bias, w_gate, w_up, w_down, ln2_g, ln2_b):
    batch, n, d = x.shape
    lctx = ctx.shape[1]
    assert batch == 1 and d == D_MODEL and n % ROW_TILE == 0 and lctx % ROW_TILE == 0
    n_ctx_blocks = lctx // ROW_TILE

    mods = _modulation(c, c_ctx, w_mod, b_mod).reshape(DEPTH, 8, 6, d)
    mod_lat = jnp.pad(mods[:, 0], ((0, 0), (0, 2), (0, 0)))
    mod_ctx = jnp.pad(mods[:, 1], ((0, 0), (0, 2), (0, 0)))
    cos, sin = _rope_tables(n)
    wr_f = w_router.T
    wr_hi = lax.reduce_precision(wr_f, exponent_bits=8, mantissa_bits=7)
    wr_mid = lax.reduce_precision(wr_f - wr_hi, exponent_bits=8, mantissa_bits=7)
    wr_lo = wr_f - wr_hi - wr_mid
    wr_t = jnp.concatenate([wr_hi, wr_mid, wr_lo], axis=0).astype(BF16)
    rb_b = jnp.broadcast_to(router_bias.reshape(N_EXPERTS, 1), (N_EXPERTS, LANES))

    x_lat, lat_off = x[0], 0
    x_ctx = ctx[0]
    for l in range(DEPTH):
        last = l == DEPTH - 1
        w_in_bf = w_in[l].astype(BF16)
        w_out_bf = w_out[l].astype(BF16)
        sgu_w_bf = sgu_w[l].astype(BF16)
        sgu_b_b = jnp.broadcast_to(sgu_b[l][:, :, None], (C_GROUPS, CHUNK, LANES))
        ln_g_c, ln_b_c = sgu_ln_g[l].reshape(1, C_W), sgu_ln_b[l].reshape(1, C_W)
        bias_tab = _na_bias_table(na_rpb[l])
        g1, b1 = ln1_g[l].reshape(1, d), ln1_b[l].reshape(1, d)
        g2, b2 = ln2_g[l].reshape(1, d), ln2_b[l].reshape(1, d)

        p = _proj(x_lat, lat_off, n, mod_lat[l], cos, sin, w_in_bf, rope=True)
        pc = _proj(x_ctx, 0, lctx, mod_ctx[l], cos, sin, w_in_bf, rope=False)
        o_a = _attn_a(p, pc, attn_sink[l], latent=True)
        o_b = _attn_b(p, pc, bias_tab, latent=True)
        o_c = _sgu(p, ln_g_c, ln_b_c, sgu_w_bf, sgu_b_b)
        if last:
            x1, h2 = _outproj(o_a, o_b, o_c, w_out_bf, x_lat, lat_off, mod_lat[l], g1, b1, n, 0, None)
            x_lat = _moe(h2, x1, jnp.stack([mod_ctx[l], mod_lat[l]]), 0, wr_t, rb_b,
                         w_gate, w_up, w_down, l, g2, b2)
        else:
            oc_a = _attn_a(pc, pc, attn_sink[l], latent=False)
            oc_b = _attn_b(pc, pc, bias_tab, latent=False)
            oc_c = _sgu(pc, ln_g_c, ln_b_c, sgu_w_bf, sgu_b_b)
            total = lctx + n
            prev = _outproj(oc_a, oc_b, oc_c, w_out_bf, x_ctx, 0, mod_ctx[l], g1, b1, total, 0, None)
            x1, h2 = _outproj(o_a, o_b, o_c, w_out_bf, x_lat, lat_off, mod_lat[l], g1, b1, total,
                              n_ctx_blocks, prev)
            x_all = _moe(h2, x1, jnp.stack([mod_ctx[l], mod_lat[l]]), n_ctx_blocks, wr_t, rb_b,
                         w_gate, w_up, w_down, l, g2, b2)
            x_lat, lat_off, x_ctx = x_all, n_ctx_blocks, x_all
    return x_lat.reshape(batch, n, d)
```

```python
import functools

import numpy as np
import jax
import jax.numpy as jnp
from jax import lax
from jax.experimental import pallas as pl
from jax.experimental.pallas import tpu as pltpu

F32 = jnp.float32
BF16 = jnp.bfloat16

D_MODEL = 2048
DEPTH = 2
GRID_W = 64
HEAD_DIM = 128
A_HEADS = 6
A_KV_HEADS = 2
A_GROUP = A_HEADS // A_KV_HEADS
A_BLOCK = 128
B_HEADS = 6
NA_ROWS = 8
NA_COLS = 16
C_GROUPS = 4
C_W = C_GROUPS * HEAD_DIM
CHUNK = 128
N_EXPERTS = 32
N_EXPERT_GROUPS = 4
EXPERTS_PER_GROUP = N_EXPERTS // N_EXPERT_GROUPS
TOP_K = 2
EXPERT_FF = 512
ROPE_BASE = 10000.0
LN_EPS = 1e-5
NEG_INF = -1e30
DEEPNORM_ALPHA = (2 * DEPTH) ** 0.25
ATTN_SCALE = HEAD_DIM ** -0.5

A_Q_W = A_HEADS * HEAD_DIM
A_KV_W = A_KV_HEADS * HEAD_DIM
B_W = B_HEADS * HEAD_DIM
OFF_AK = A_Q_W
OFF_AV = OFF_AK + A_KV_W
OFF_BQ = OFF_AV + A_KV_W
OFF_BK = OFF_BQ + B_W
OFF_BV = OFF_BK + B_W
OFF_C = OFF_BV + B_W
IN_COLS = OFF_C + 2 * C_W

VMEM_LIMIT_BYTES = 56 * 1024 * 1024
LANES = 128

ROW_TILE = 256
PROJ_COL_TILE = 512
MOD_COL_TILE = 1024
MOE_TILE = 256

MOD_SH1, MOD_SC1, MOD_G1, MOD_SH2, MOD_SC2, MOD_G2 = range(6)


def _params(*sem):
    return pltpu.CompilerParams(dimension_semantics=sem, vmem_limit_bytes=VMEM_LIMIT_BYTES)


def _layer_norm(t, g, b):
    mu = jnp.mean(t, axis=-1, keepdims=True)
    d = t - mu
    var = jnp.mean(d * d, axis=-1, keepdims=True)
    return d * lax.rsqrt(var + LN_EPS) * g + b


def _sigmoid(v):
    return 1.0 / (1.0 + jnp.exp(-v))


def _dot_nt(a, b):
    return lax.dot_general(a, b, (((1,), (1,)), ((), ())), preferred_element_type=F32)


PACK_W = D_MODEL // 2
PACK_SUB = PACK_W // LANES


def _pack_rows(v):
    bits = pltpu.bitcast(v.astype(BF16).astype(F32), jnp.int32)
    return lax.shift_right_logical(bits[:, :PACK_W], 16) | bits[:, PACK_W:]


def _store_packed(ref, words, m):
    for s in range(PACK_SUB):
        ref[pl.ds(s, m, stride=PACK_SUB), :] = words[:, s * LANES:(s + 1) * LANES]


def _load_packed(ref, m):
    lo, hi = [], []
    for s in range(PACK_SUB):
        w = ref[pl.ds(s, m, stride=PACK_SUB), :]
        lo.append(pltpu.bitcast(w << 16, F32))
        hi.append(pltpu.bitcast(w & jnp.int32(-65536), F32))
    return jnp.concatenate(lo + hi, axis=1)


def _mod_kernel(c_ref, w_ref, b_ref, o_ref):
    w = w_ref[...]
    reps = w.shape[1] // LANES
    rows = []
    for r in range(2):
        cv = c_ref[r]
        s = cv * _sigmoid(cv)
        sb = jnp.concatenate([s] * reps, axis=1)
        rows.append(jnp.sum(w * sb, axis=0, keepdims=True) + b_ref[...])
    rows.append(jnp.zeros((6, w.shape[1]), F32))
    o_ref[...] = jnp.concatenate(rows, axis=0)


def _modulation(c, c_ctx, w_mod, b_mod):
    d = D_MODEL
    cb = jnp.stack([jnp.broadcast_to(c.reshape(d, 1), (d, LANES)),
                    jnp.broadcast_to(c_ctx.reshape(d, 1), (d, LANES))])
    n_out = 6 * d
    return pl.pallas_call(
        _mod_kernel,
        grid=(DEPTH, n_out // MOD_COL_TILE),
        in_specs=[pl.BlockSpec((2, d, LANES), lambda l, j: (0, 0, 0)),
                  pl.BlockSpec((None, d, MOD_COL_TILE), lambda l, j: (l, 0, j)),
                  pl.BlockSpec((None, 1, MOD_COL_TILE), lambda l, j: (l, 0, j))],
        out_specs=pl.BlockSpec((None, 8, MOD_COL_TILE), lambda l, j: (l, 0, j)),
        out_shape=jax.ShapeDtypeStruct((DEPTH, 8, n_out), F32),
        compiler_params=_params("arbitrary", "arbitrary"),
        name="modulation",
    )(cb, w_mod, b_mod.reshape(DEPTH, 1, n_out))


def _gelu_tanh(v):
    return 0.5 * v * (1.0 + jnp.tanh(np.sqrt(2.0 / np.pi).astype(np.float32) * (v + 0.044715 * (v * v * v))))


def _proj_kernel(x_ref, mod_ref, cos_ref, sin_ref, w_ref, o_ref, *, rope):
    x = x_ref[...]
    h = (x * (1.0 + mod_ref[MOD_SC1:MOD_SC1 + 1, :]) + mod_ref[MOD_SH1:MOD_SH1 + 1, :]).astype(BF16)
    tn = PROJ_COL_TILE
    for j in range(IN_COLS // tn):
        c0 = j * tn
        acc = jnp.dot(h, w_ref[:, c0:c0 + tn], preferred_element_type=F32)
        if c0 < OFF_AV:
            if rope:
                cos = cos_ref[...]
                sin = sin_ref[...]
                parts = []
                for hh in range(tn // HEAD_DIM):
                    a = acc[:, hh * HEAD_DIM:(hh + 1) * HEAD_DIM]
                    parts.append(a * cos + pltpu.roll(a, HEAD_DIM // 2, 1) * sin)
                acc = jnp.concatenate(parts, axis=1)
        elif c0 >= OFF_C:
            acc = _gelu_tanh(acc)
        o_ref[:, c0:c0 + tn] = acc.astype(BF16)


def _proj(x, x_off, rows, mod, cos, sin, w_bf, rope):
    d = D_MODEL
    return pl.pallas_call(
        functools.partial(_proj_kernel, rope=rope),
        grid=(rows // ROW_TILE,),
        in_specs=[pl.BlockSpec((ROW_TILE, d), lambda i: (i + x_off, 0)),
                  pl.BlockSpec((8, d), lambda i: (0, 0)),
                  pl.BlockSpec((ROW_TILE, HEAD_DIM), lambda i: (i, 0)),
                  pl.BlockSpec((ROW_TILE, HEAD_DIM), lambda i: (i, 0)),
                  pl.BlockSpec((d, IN_COLS), lambda i: (0, 0), pipeline_mode=pl.Buffered(1))],
        out_specs=pl.BlockSpec((ROW_TILE, IN_COLS), lambda i: (i, 0)),
        out_shape=jax.ShapeDtypeStruct((rows, IN_COLS), BF16),
        compiler_params=_params("arbitrary"),
        name="proj_rope" if rope else "proj_ctx",
    )(x, mod, cos, sin, w_bf)


def _softmax_pv(s_parts, v_parts, sink):
    m = s_parts[0].max(axis=-1, keepdims=True)
    for s in s_parts[1:]:
        m = jnp.maximum(m, s.max(axis=-1, keepdims=True))
    if sink is not None:
        m = jnp.maximum(m, sink)
    denom = None if sink is None else jnp.exp(sink - m)
    out = None
    for s, v in zip(s_parts, v_parts):
        e = jnp.exp(s - m)
        es = e.sum(axis=-1, keepdims=True)
        denom = es if denom is None else denom + es
        pv = jnp.dot(e.astype(BF16), v, preferred_element_type=F32)
        out = pv if out is None else out + pv
    return out / denom


def _attn_a_kernel(sink_ref, q_ref, kp_ref, kc_ref, kn_ref, vp_ref, vc_ref, vn_ref, kx_ref, vx_ref, o_ref,
                   mask_ref, *, latent):
    i = pl.program_id(0)
    nb = pl.num_programs(0)
    nq = A_GROUP * A_BLOCK

    if latent:
        @pl.when(i == 0)
        def _():
            qi = lax.broadcasted_iota(jnp.int32, (nq, 3 * A_BLOCK), 0) % A_BLOCK
            jj = lax.broadcasted_iota(jnp.int32, (nq, 3 * A_BLOCK), 1)
            ok = (jj >= qi) & (jj <= qi + 2 * A_BLOCK)
            mask_ref[...] = jnp.where(ok, 0.0, NEG_INF).astype(F32)

        col = lax.broadcasted_iota(jnp.int32, (nq, 3 * A_BLOCK), 1)
        off_band = ((col < A_BLOCK) & (i == 0)) | ((col >= 2 * A_BLOCK) & (i == nb - 1))

    for kh in range(A_KV_HEADS):
        hs = [kh * A_GROUP + g for g in range(A_GROUP)]
        q = jnp.concatenate([q_ref[:, h * HEAD_DIM:(h + 1) * HEAD_DIM] for h in hs], axis=0)
        sink = jnp.concatenate([jnp.full((A_BLOCK, 1), sink_ref[h], F32) for h in hs], axis=0)
        ks = slice(kh * HEAD_DIM, (kh + 1) * HEAD_DIM)
        s_parts = [_dot_nt(q, kx_ref[:, ks]) * ATTN_SCALE]
        v_parts = [vx_ref[:, ks]]
        if latent:
            kband = jnp.concatenate([kp_ref[:, ks], kc_ref[:, ks], kn_ref[:, ks]], axis=0)
            vband = jnp.concatenate([vp_ref[:, ks], vc_ref[:, ks], vn_ref[:, ks]], axis=0)
            s_loc = _dot_nt(q, kband) * ATTN_SCALE + mask_ref[...]
            s_parts.append(jnp.where(off_band, NEG_INF, s_loc))
            v_parts.append(vband)
        out = _softmax_pv(s_parts, v_parts, sink)
        for g, h in enumerate(hs):
            o_ref[:, h * HEAD_DIM:(h + 1) * HEAD_DIM] = out[g * A_BLOCK:(g + 1) * A_BLOCK].astype(BF16)


def _attn_a(p, pc, sink, latent):
    rows = p.shape[0]
    nb = rows // A_BLOCK
    kcol = OFF_AK // A_KV_W
    vcol = OFF_AV // A_KV_W

    def band(col, shift):
        return pl.BlockSpec((A_BLOCK, A_KV_W), lambda i, s: (jnp.clip(i + shift, 0, nb - 1), col))

    lctx = pc.shape[0]
    return pl.pallas_call(
        functools.partial(_attn_a_kernel, latent=latent),
        grid_spec=pltpu.PrefetchScalarGridSpec(
            num_scalar_prefetch=1,
            grid=(nb,),
            in_specs=[pl.BlockSpec((A_BLOCK, A_Q_W), lambda i, s: (i, 0)),
                      band(kcol, -1), band(kcol, 0), band(kcol, 1),
                      band(vcol, -1), band(vcol, 0), band(vcol, 1),
                      pl.BlockSpec((lctx, A_KV_W), lambda i, s: (0, kcol)),
                      pl.BlockSpec((lctx, A_KV_W), lambda i, s: (0, vcol))],
            out_specs=pl.BlockSpec((A_BLOCK, A_Q_W), lambda i, s: (i, 0)),
            scratch_shapes=[pltpu.VMEM((A_GROUP * A_BLOCK, 3 * A_BLOCK), F32)]),
        out_shape=jax.ShapeDtypeStruct((rows, A_Q_W), BF16),
        compiler_params=_params("arbitrary"),
        name="attn_a_latent" if latent else "attn_a_ctx",
    )(sink, p, p, p, p, p, p, p, pc, pc)


B_PAIR_W = 2 * HEAD_DIM
NA_GROUP_ROWS = ROW_TILE // GRID_W
NA_WIN_ROWS = NA_ROWS + NA_GROUP_ROWS
NA_PAIRS = NA_WIN_ROWS // 2
NA_BIAS_OFFS = 2 * NA_ROWS
NA_BOTH, NA_LEFT, NA_RIGHT = range(3)


def _attn_b_kernel(q_ref, k_ref, v_ref, kx_ref, vx_ref, bias_ref, o_ref, *, latent, grid_rows):
    g = pl.program_id(1)
    if latent:
        r_base = g * NA_GROUP_ROWS
        w0 = jnp.clip(r_base - NA_ROWS // 2, 0, grid_rows - NA_WIN_ROWS)
        start = pl.multiple_of(w0 * GRID_W, GRID_W)
    for hh in range(2):
        hs = slice(hh * HEAD_DIM, (hh + 1) * HEAD_DIM)
        q = q_ref[:, hs]
        s_parts = [_dot_nt(q, kx_ref[:, hs]) * ATTN_SCALE]
        v_parts = [vx_ref[:, hs]]
        if latent:
            kwin = k_ref[pl.ds(start, NA_WIN_ROWS * GRID_W), hs]
            vwin = v_ref[pl.ds(start, NA_WIN_ROWS * GRID_W), hs]
            bias_rows = []
            for rr in range(NA_GROUP_ROWS):
                r = r_base + rr
                r0 = jnp.clip(r - NA_ROWS // 2, 0, grid_rows - NA_ROWS)
                tiles = []
                for jp in range(NA_PAIRS):
                    ka = w0 + 2 * jp
                    in_a = (ka >= r0) & (ka < r0 + NA_ROWS)
                    in_b = (ka + 1 >= r0) & (ka + 1 < r0 + NA_ROWS)
                    variant = jnp.where(in_a, jnp.where(in_b, NA_BOTH, NA_LEFT), jnp.where(in_b, NA_RIGHT, NA_LEFT))
                    off = jnp.where(in_a | in_b, jnp.clip(ka - r + NA_ROWS, 0, NA_BIAS_OFFS - 1), 0)
                    tiles.append(bias_ref[hh, variant, off])
                bias_rows.append(jnp.concatenate(tiles, axis=1))
            bias = jnp.concatenate(bias_rows, axis=0)
            s_parts.append(_dot_nt(q, kwin) * ATTN_SCALE + bias)
            v_parts.append(vwin)
        out = _softmax_pv(s_parts, v_parts, None)
        o_ref[:, hs] = out.astype(BF16)


def _na_bias_table(rpb):
    cols = np.arange(GRID_W)
    c0 = np.clip(cols - NA_COLS // 2, 0, GRID_W - NA_COLS)
    rel = cols[None, :] - cols[:, None] + NA_COLS - 1
    ok = (cols[None, :] >= c0[:, None]) & (cols[None, :] < c0[:, None] + NA_COLS)
    onehot = (rel[None] == np.arange(2 * NA_COLS - 1)[:, None, None]).astype(np.float32)
    t = jnp.einsum("hrd,dqk->hrqk", rpb, onehot, precision=lax.Precision.HIGHEST)
    t = jnp.where(ok[None, None], t, NEG_INF).astype(F32)
    t = jnp.pad(t, ((0, 0), (1, 1), (0, 0), (0, 0)), constant_values=NEG_INF)
    first, second = t[:, :-1], t[:, 1:]
    masked = jnp.full_like(first, NEG_INF)
    return jnp.stack([jnp.concatenate([first, second], axis=-1),
                      jnp.concatenate([first, masked], axis=-1),
                      jnp.concatenate([masked, second], axis=-1)], axis=1)


def _attn_b(p, pc, bias_tab, latent):
    rows = p.shape[0]
    lctx = pc.shape[0]
    assert not latent or rows // GRID_W >= NA_WIN_ROWS
    qcol = OFF_BQ // B_PAIR_W
    kcol = OFF_BK // B_PAIR_W
    vcol = OFF_BV // B_PAIR_W
    return pl.pallas_call(
        functools.partial(_attn_b_kernel, latent=latent, grid_rows=rows // GRID_W),
        grid=(B_HEADS // 2, rows // ROW_TILE),
        in_specs=[pl.BlockSpec((ROW_TILE, B_PAIR_W), lambda hp, g: (g, qcol + hp)),
                  pl.BlockSpec((rows, B_PAIR_W), lambda hp, g: (0, kcol + hp)),
                  pl.BlockSpec((rows, B_PAIR_W), lambda hp, g: (0, vcol + hp)),
                  pl.BlockSpec((lctx, B_PAIR_W), lambda hp, g: (0, kcol + hp)),
                  pl.BlockSpec((lctx, B_PAIR_W), lambda hp, g: (0, vcol + hp)),
                  pl.BlockSpec((2, 3, NA_BIAS_OFFS, GRID_W, 2 * GRID_W), lambda hp, g: (hp, 0, 0, 0, 0))],
        out_specs=pl.BlockSpec((ROW_TILE, B_PAIR_W), lambda hp, g: (g, hp)),
        out_shape=jax.ShapeDtypeStruct((rows, B_W), BF16),
        compiler_params=_params("arbitrary", "arbitrary"),
        name="attn_b_latent" if latent else "attn_b_ctx",
    )(p, p, p, pc, pc, bias_tab)


def _sgu_kernel(u_ref, v_ref, g_ref, b_ref, w_ref, bs_ref, o_ref):
    for ch in range(ROW_TILE // CHUNK):
        rs = slice(ch * CHUNK, (ch + 1) * CHUNK)
        for grp in range(C_GROUPS):
            cs = slice(grp * HEAD_DIM, (grp + 1) * HEAD_DIM)
            vn = _layer_norm(v_ref[rs, cs].astype(F32), g_ref[:, cs], b_ref[:, cs])
            mixed = jnp.dot(w_ref[grp], vn.astype(BF16), preferred_element_type=F32) + bs_ref[grp]
            o_ref[rs, cs] = (u_ref[rs, cs].astype(F32) * mixed).astype(BF16)


def _sgu(p, ln_g, ln_b, w_bf, bs_b):
    rows = p.shape[0]
    ucol = OFF_C // C_W
    return pl.pallas_call(
        _sgu_kernel,
        grid=(rows // ROW_TILE,),
        in_specs=[pl.BlockSpec((ROW_TILE, C_W), lambda i: (i, ucol)),
                  pl.BlockSpec((ROW_TILE, C_W), lambda i: (i, ucol + 1)),
                  pl.BlockSpec((1, C_W), lambda i: (0, 0)),
                  pl.BlockSpec((1, C_W), lambda i: (0, 0)),
                  pl.BlockSpec((C_GROUPS, CHUNK, CHUNK), lambda i: (0, 0, 0)),
                  pl.BlockSpec((C_GROUPS, CHUNK, LANES), lambda i: (0, 0, 0))],
        out_specs=pl.BlockSpec((ROW_TILE, C_W), lambda i: (i, 0)),
        out_shape=jax.ShapeDtypeStruct((rows, C_W), BF16),
        compiler_params=_params("arbitrary"),
        name="sgu",
    )(p, p, ln_g, ln_b, w_bf, bs_b)


def _outproj_kernel(oa_ref, ob_ref, oc_ref, w_ref, x_ref, mod_ref, g_ref, b_ref, *rest):
    x1_ref, h2_ref = rest[-2], rest[-1]
    mix = jnp.dot(oa_ref[...], w_ref[0:A_Q_W, :], preferred_element_type=F32)
    mix += jnp.dot(ob_ref[...], w_ref[A_Q_W:A_Q_W + B_W, :], preferred_element_type=F32)
    mix += jnp.dot(oc_ref[...], w_ref[A_Q_W + B_W:, :], preferred_element_type=F32)
    t = DEEPNORM_ALPHA * x_ref[...] + mod_ref[MOD_G1:MOD_G1 + 1, :] * mix
    x1 = _layer_norm(t, g_ref[...], b_ref[...])
    x1_ref[...] = x1
    h2 = x1 * (1.0 + mod_ref[MOD_SC2:MOD_SC2 + 1, :]) + mod_ref[MOD_SH2:MOD_SH2 + 1, :]
    _store_packed(h2_ref, _pack_rows(h2), ROW_TILE)


def _outproj(o_a, o_b, o_c, w_bf, x, x_off, mod, ln_g, ln_b, total_rows, out_off, prev):
    rows = o_a.shape[0]
    d = D_MODEL
    in_specs = [pl.BlockSpec((ROW_TILE, A_Q_W), lambda i: (i, 0)),
                pl.BlockSpec((ROW_TILE, B_W), lambda i: (i, 0)),
                pl.BlockSpec((ROW_TILE, C_W), lambda i: (i, 0)),
                pl.BlockSpec((d, d), lambda i: (0, 0), pipeline_mode=pl.Buffered(1)),
                pl.BlockSpec((ROW_TILE, d), lambda i: (i + x_off, 0)),
                pl.BlockSpec((8, d), lambda i: (0, 0)),
                pl.BlockSpec((1, d), lambda i: (0, 0)),
                pl.BlockSpec((1, d), lambda i: (0, 0))]
    args = [o_a, o_b, o_c, w_bf, x, mod, ln_g, ln_b]
    aliases = {}
    if prev is not None:
        in_specs += [pl.BlockSpec(memory_space=pl.ANY), pl.BlockSpec(memory_space=pl.ANY)]
        aliases = {len(args): 0, len(args) + 1: 1}
        args += list(prev)
    return pl.pallas_call(
        _outproj_kernel,
        grid=(rows // ROW_TILE,),
        in_specs=in_specs,
        out_specs=[pl.BlockSpec((ROW_TILE, d), lambda i: (i + out_off, 0)),
                   pl.BlockSpec((ROW_TILE * PACK_SUB, LANES), lambda i: (i + out_off, 0))],
        out_shape=[jax.ShapeDtypeStruct((total_rows, d), F32),
                   jax.ShapeDtypeStruct((total_rows * PACK_SUB, LANES), jnp.int32)],
        input_output_aliases=aliases,
        compiler_params=_params("arbitrary"),
        name="outproj",
    )(*args)


def _top2_sublanes(vals, sub):
    m1 = vals.max(axis=0, keepdims=True)
    i1 = jnp.where(vals == m1, sub, vals.shape[0]).min(axis=0, keepdims=True)
    rest = jnp.where(sub == i1, -jnp.inf, vals)
    m2 = rest.max(axis=0, keepdims=True)
    i2 = jnp.where(rest == m2, sub, vals.shape[0]).min(axis=0, keepdims=True)
    return m1, i1, m2, i2


def _router_kernel(h_ref, wr_ref, rb_ref, e_ref, w_ref, rank_ref, cnt_ref, run_ref):
    i = pl.program_id(0)
    tm = ROW_TILE
    epg = EXPERTS_PER_GROUP

    @pl.when(i == 0)
    def _():
        run_ref[...] = jnp.zeros_like(run_ref)

    parts = _dot_nt(wr_ref[...], _load_packed(h_ref, tm).astype(BF16))
    logits = parts[0:N_EXPERTS] + parts[N_EXPERTS:2 * N_EXPERTS] + parts[2 * N_EXPERTS:]
    scores = _sigmoid(logits)
    biased = scores + jnp.concatenate([rb_ref[...]] * (tm // LANES), axis=1)
    sub = lax.broadcasted_iota(jnp.int32, (epg, tm), 0)

    best = None
    for g in range(N_EXPERT_GROUPS):
        m1, _, m2, _ = _top2_sublanes(biased[g * epg:(g + 1) * epg], sub)
        gs = m1 + m2
        if best is None:
            best, grp = gs, jnp.zeros((1, tm), jnp.int32)
            bsel, ssel = biased[0:epg], scores[0:epg]
        else:
            better = gs > best
            best = jnp.where(better, gs, best)
            grp = jnp.where(better, g, grp)
            bsel = jnp.where(better, biased[g * epg:(g + 1) * epg], bsel)
            ssel = jnp.where(better, scores[g * epg:(g + 1) * epg], ssel)
    _, i1, _, i2 = _top2_sublanes(bsel, sub)
    w1 = jnp.where(sub == i1, ssel, 0.0).sum(axis=0, keepdims=True)
    w2 = jnp.where(sub == i2, ssel, 0.0).sum(axis=0, keepdims=True)
    tot = w1 + w2
    e1 = grp * epg + i1
    e2 = grp * epg + i2

    eiota = lax.broadcasted_iota(jnp.int32, (N_EXPERTS, tm), 0)
    oh1 = (eiota == e1).astype(F32)
    oh2 = (eiota == e2).astype(F32)
    ohb = oh1 + oh2
    before = (lax.broadcasted_iota(jnp.int32, (tm, tm), 0) < lax.broadcasted_iota(jnp.int32, (tm, tm), 1))
    prefix = jnp.dot(ohb.astype(BF16), before.astype(BF16), preferred_element_type=F32)
    pos = run_ref[...] + prefix
    r1 = (oh1 * pos).sum(axis=0, keepdims=True)
    r2 = (oh2 * pos).sum(axis=0, keepdims=True)
    run_ref[...] = run_ref[...] + ohb.sum(axis=1, keepdims=True)

    e_ref[...] = jnp.concatenate([e1, e2], axis=0)
    w_ref[...] = jnp.concatenate([w1 / tot, w2 / tot], axis=0)
    rank_ref[...] = jnp.concatenate([r1, r2], axis=0).astype(jnp.int32)
    cnt_ref[...] = run_ref[:, 0:LANES]


def _router(h2p, wr_t, rb_b):
    t = h2p.shape[0] // PACK_SUB
    d = D_MODEL
    row2 = pl.BlockSpec((TOP_K, ROW_TILE), lambda i: (0, i))
    return pl.pallas_call(
        _router_kernel,
        grid=(t // ROW_TILE,),
        in_specs=[pl.BlockSpec((ROW_TILE * PACK_SUB, LANES), lambda i: (i, 0)),
                  pl.BlockSpec((3 * N_EXPERTS, d), lambda i: (0, 0)),
                  pl.BlockSpec((N_EXPERTS, LANES), lambda i: (0, 0))],
        out_specs=[row2, row2, row2, pl.BlockSpec((N_EXPERTS, LANES), lambda i: (0, 0))],
        out_shape=[jax.ShapeDtypeStruct((TOP_K, t), jnp.int32),
                   jax.ShapeDtypeStruct((TOP_K, t), F32),
                   jax.ShapeDtypeStruct((TOP_K, t), jnp.int32),
                   jax.ShapeDtypeStruct((N_EXPERTS, LANES), F32)],
        scratch_shapes=[pltpu.VMEM((N_EXPERTS, ROW_TILE), F32)],
        compiler_params=_params("arbitrary"),
        name="router",
    )(h2p, wr_t, rb_b)


def _row_copy(src_hbm, first_sublane, dst, sem):
    return pltpu.make_async_copy(src_hbm.at[pl.ds(pl.multiple_of(first_sublane, PACK_SUB), PACK_SUB), :], dst, sem)


def _experts_kernel(ts_ref, nu_ref, src_ref, h_hbm, wg_hbm, wu_hbm, wd_hbm, ys_hbm,
                    xbuf, ybuf, gsem, ysem, wg_f, wu_f, wd_f, wsem, wg_bf, wu_bf, wd_bf, *, layer):
    e = pl.program_id(0)
    ne = pl.num_programs(0)
    n_used = nu_ref[0]
    tm = MOE_TILE
    tsub = tm * PACK_SUB

    def has_rows(ex):
        return ts_ref[ex + 1] > ts_ref[ex]

    def weight_copies(ex, wslot):
        return [pltpu.make_async_copy(src.at[layer, ex], dst.at[wslot], wsem.at[wslot])
                for src, dst in ((wg_hbm, wg_f), (wu_hbm, wu_f), (wd_hbm, wd_f))]

    def start_weights(ex, wslot):
        for cp in weight_copies(ex, wslot):
            cp.start(priority=1)

    def gather(tile, slot):
        base = tile * tm
        for r in range(tm):
            _row_copy(h_hbm, src_ref[base + r], xbuf.at[slot, pl.ds(r * PACK_SUB, PACK_SUB), :], gsem.at[slot]).start()

    def wait_gather(slot):
        pltpu.make_async_copy(h_hbm.at[pl.ds(0, tsub), :], xbuf.at[slot], gsem.at[slot]).wait()

    def y_store(tile, slot):
        row0 = pl.multiple_of(tile * tsub, tsub)
        return pltpu.make_async_copy(ybuf.at[slot], ys_hbm.at[pl.ds(row0, tsub), :], ysem.at[slot])

    wslot = e % 2

    @pl.when((e == 0) & has_rows(0))
    def _():
        start_weights(0, 0)

    @pl.when((e == 0) & (n_used > 0))
    def _():
        gather(0, 0)

    nxt = jnp.minimum(e + 1, ne - 1)

    @pl.when((e + 1 < ne) & has_rows(nxt))
    def _():
        start_weights(nxt, 1 - wslot)

    @pl.when(has_rows(e))
    def _():
        for cp in weight_copies(e, wslot):
            cp.wait()
        wg_bf[...] = wg_f[wslot].astype(BF16)
        wu_bf[...] = wu_f[wslot].astype(BF16)
        wd_bf[...] = wd_f[wslot].astype(BF16)

    def tile_body(g, carry):
        slot = g % 2

        @pl.when(g >= 2)
        def _():
            y_store(g - 2, slot).wait()

        wait_gather(slot)
        x = _load_packed(xbuf.at[slot], tm).astype(BF16)
        gather(jnp.minimum(g + 1, n_used - 1), 1 - slot)
        gate = jnp.dot(x, wg_bf[...], preferred_element_type=F32)
        up = jnp.dot(x, wu_bf[...], preferred_element_type=F32)
        hid = (gate * _sigmoid(gate)) * up
        y = jnp.dot(hid.astype(BF16), wd_bf[...], preferred_element_type=F32)
        _store_packed(ybuf.at[slot], _pack_rows(y), tm)
        y_store(g, slot).start(priority=1)
        return carry

    lax.fori_loop(ts_ref[e], ts_ref[e + 1], tile_body, 0)

    @pl.when((e == pl.num_programs(0) - 1) & (n_used > 0))
    def _():
        wait_gather(n_used % 2)
        y_store(n_used - 1, (n_used - 1) % 2).wait()

        @pl.when(n_used >= 2)
        def _():
            y_store(n_used - 2, n_used % 2).wait()


def _experts(h2, tile_start, n_used, src_tok, w_gate, w_up, w_down, layer):
    d = D_MODEL
    tm = MOE_TILE
    rows = src_tok.shape[0]

    hbm = pl.BlockSpec(memory_space=pl.ANY)
    return pl.pallas_call(
        functools.partial(_experts_kernel, layer=layer),
        grid_spec=pltpu.PrefetchScalarGridSpec(
            num_scalar_prefetch=3,
            grid=(N_EXPERTS,),
            in_specs=[hbm, hbm, hbm, hbm],
            out_specs=hbm,
            scratch_shapes=[pltpu.VMEM((2, tm * PACK_SUB, LANES), jnp.int32),
                            pltpu.VMEM((2, tm * PACK_SUB, LANES), jnp.int32),
                            pltpu.SemaphoreType.DMA((2,)),
                            pltpu.SemaphoreType.DMA((2,)),
                            pltpu.VMEM((2, d, EXPERT_FF), F32),
                            pltpu.VMEM((2, d, EXPERT_FF), F32),
                            pltpu.VMEM((2, EXPERT_FF, d), F32),
                            pltpu.SemaphoreType.DMA((2,)),
                            pltpu.VMEM((d, EXPERT_FF), BF16),
                            pltpu.VMEM((d, EXPERT_FF), BF16),
                            pltpu.VMEM((EXPERT_FF, d), BF16)]),
        out_shape=jax.ShapeDtypeStruct((rows * PACK_SUB, LANES), jnp.int32),
        compiler_params=_params("arbitrary"),
        name="experts",
    )(tile_start, n_used, src_tok, h2, w_gate, w_up, w_down)


def _combine_kernel(slot_ref, ys_hbm, w_ref, x_ref, mod_ref, g_ref, b_ref, o_ref, ybuf, sem, *, n_tok):
    i = pl.program_id(0)
    nt = pl.num_programs(0)
    tm = ROW_TILE

    def issue(blk, buf):
        for r in range(tm):
            for k in range(TOP_K):
                row = slot_ref[k * n_tok + blk * tm + r]
                _row_copy(ys_hbm, row, ybuf.at[buf, k, pl.ds(r * PACK_SUB, PACK_SUB), :],
                          sem.at[buf]).start(priority=k)

    def wait(buf):
        for k in range(TOP_K):
            pltpu.make_async_copy(ys_hbm.at[pl.ds(0, tm * PACK_SUB), :], ybuf.at[buf, k], sem.at[buf]).wait()

    @pl.when(i == 0)
    def _():
        issue(0, 0)

    buf = i % 2
    wait(buf)
    issue(jnp.minimum(i + 1, nt - 1), 1 - buf)
    y = (w_ref[:, 0:1] * _load_packed(ybuf.at[buf, 0], tm)
         + w_ref[:, 1:2] * _load_packed(ybuf.at[buf, 1], tm))
    t = DEEPNORM_ALPHA * x_ref[...] + mod_ref[MOD_G2:MOD_G2 + 1, :] * y
    o_ref[...] = _layer_norm(t, g_ref[...], b_ref[...])

    @pl.when(i == nt - 1)
    def _():
        wait(1 - buf)


def _combine(ys, slots, w_tok, x1, mods, n_ctx_blocks, ln_g, ln_b):
    t = x1.shape[0]
    d = D_MODEL
    return pl.pallas_call(
        functools.partial(_combine_kernel, n_tok=t),
        grid_spec=pltpu.PrefetchScalarGridSpec(
            num_scalar_prefetch=1,
            grid=(t // ROW_TILE,),
            in_specs=[pl.BlockSpec(memory_space=pl.ANY),
                      pl.BlockSpec((ROW_TILE, TOP_K), lambda i, s: (i, 0)),
                      pl.BlockSpec((ROW_TILE, d), lambda i, s: (i, 0)),
                      pl.BlockSpec((None, 8, d), lambda i, s: (jnp.where(i < n_ctx_blocks, 0, 1), 0, 0)),
                      pl.BlockSpec((1, d), lambda i, s: (0, 0)),
                      pl.BlockSpec((1, d), lambda i, s: (0, 0))],
            out_specs=pl.BlockSpec((ROW_TILE, d), lambda i, s: (i, 0)),
            scratch_shapes=[pltpu.VMEM((2, TOP_K, ROW_TILE * PACK_SUB, LANES), jnp.int32),
                            pltpu.SemaphoreType.DMA((2,))]),
        out_shape=jax.ShapeDtypeStruct((t, d), F32),
        compiler_params=_params("arbitrary"),
        name="combine",
    )(slots, ys, w_tok, x1, mods, ln_g, ln_b)


def _moe(h2, x1, mods, n_ctx_blocks, wr_t, rb_b, w_gate, w_up, w_down, layer, ln_g, ln_b):
    t = x1.shape[0]
    tm = MOE_TILE
    e_idx, w_tok, rank, cnt = _router(h2, wr_t, rb_b)
    counts = cnt[:, 0].astype(jnp.int32)
    tiles_per = (counts + tm - 1) // tm
    tile_end = jnp.cumsum(tiles_per)
    n_used = tile_end[-1]
    row_off = (tile_end - tiles_per) * tm
    experts = jnp.arange(N_EXPERTS, dtype=jnp.int32)
    slots = jnp.sum(jnp.where(e_idx[:, :, None] == experts, row_off, 0), axis=-1) + rank
    max_tiles = (TOP_K * t + N_EXPERTS * (tm - 1)) // tm + 1
    tile_start = jnp.concatenate([tile_end - tiles_per, n_used.reshape(1)]).astype(jnp.int32)
    tok = jnp.tile(jnp.arange(t, dtype=jnp.int32), TOP_K)
    src_sub = jnp.zeros((max_tiles * tm,), jnp.int32).at[slots.reshape(-1)].set(tok * PACK_SUB)
    ys = _experts(h2, tile_start, n_used.reshape(1).astype(jnp.int32), src_sub, w_gate, w_up, w_down, layer)
    return _combine(ys, slots.reshape(-1) * PACK_SUB, w_tok.T, x1, mods, n_ctx_blocks, ln_g, ln_b)


def _rope_tables(n):
    t = jnp.arange(n)
    row = (t // GRID_W).astype(F32)
    col = (t % GRID_W).astype(F32)
    n_freq = HEAD_DIM // 4
    inv_freq = ROPE_BASE ** (-jnp.arange(n_freq, dtype=F32) / n_freq)
    ang = jnp.concatenate([row[:, None] * inv_freq, col[:, None] * inv_freq], axis=-1)
    cos, sin = jnp.cos(ang), jnp.sin(ang)
    return jnp.concatenate([cos, cos], axis=-1), jnp.concatenate([-sin, sin], axis=-1)


def kernel(x, c, ctx, c_ctx, w_mod, b_mod, w_in, attn_sink, na_rpb, sgu_ln_g, sgu_ln_b, sgu_w, sgu_b,
           w_out, ln1_g, ln1_b, w_router, router_bias, w_gate, w_up, w_down, ln2_g, ln2_b):
    batch, n, d = x.shape
    lctx = ctx.shape[1]
    assert batch == 1 and d == D_MODEL and n % ROW_TILE == 0 and lctx % ROW_TILE == 0
    n_ctx_blocks = lctx // ROW_TILE

    mods = _modulation(c, c_ctx, w_mod, b_mod).reshape(DEPTH, 8, 6, d)
    mod_lat = jnp.pad(mods[:, 0], ((0, 0), (0, 2), (0, 0)))
    mod_ctx = jnp.pad(mods[:, 1], ((0, 0), (0, 2), (0, 0)))
    cos, sin = _rope_tables(n)
    wr_f = w_router.T
    wr_hi = lax.reduce_precision(wr_f, exponent_bits=8, mantissa_bits=7)
    wr_mid = lax.reduce_precision(wr_f - wr_hi, exponent_bits=8, mantissa_bits=7)
    wr_lo = wr_f - wr_hi - wr_mid
    wr_t = jnp.concatenate([wr_hi, wr_mid, wr_lo], axis=0).astype(BF16)
    rb_b = jnp.broadcast_to(router_bias.reshape(N_EXPERTS, 1), (N_EXPERTS, LANES))

    x_lat, lat_off = x[0], 0
    x_ctx = ctx[0]
    for l in range(DEPTH):
        last = l == DEPTH - 1
        w_in_bf = w_in[l].astype(BF16)
        w_out_bf = w_out[l].astype(BF16)
        sgu_w_bf = sgu_w[l].astype(BF16)
        sgu_b_b = jnp.broadcast_to(sgu_b[l][:, :, None], (C_GROUPS, CHUNK, LANES))
        ln_g_c, ln_b_c = sgu_ln_g[l].reshape(1, C_W), sgu_ln_b[l].reshape(1, C_W)
        bias_tab = _na_bias_table(na_rpb[l])
        g1, b1 = ln1_g[l].reshape(1, d), ln1_b[l].reshape(1, d)
        g2, b2 = ln2_g[l].reshape(1, d), ln2_b[l].reshape(1, d)

        p = _proj(x_lat, lat_off, n, mod_lat[l], cos, sin, w_in_bf, rope=True)
        pc = _proj(x_ctx, 0, lctx, mod_ctx[l], cos, sin, w_in_bf, rope=False)
        o_a = _attn_a(p, pc, attn_sink[l], latent=True)
        o_b = _attn_b(p, pc, bias_tab, latent=True)
        o_c = _sgu(p, ln_g_c, ln_b_c, sgu_w_bf, sgu_b_b)
        if last:
            x1, h2 = _outproj(o_a, o_b, o_c, w_out_bf, x_lat, lat_off, mod_lat[l], g1, b1, n, 0, None)
            x_lat = _moe(h2, x1, jnp.stack([mod_ctx[l], mod_lat[l]]), 0, wr_t, rb_b,
                         w_gate, w_up, w_down, l, g2, b2)
        else:
            oc_a = _attn_a(pc, pc, attn_sink[l], latent=False)
            oc_b = _attn_b(pc, pc, bias_tab, latent=False)
            oc_c = _sgu(pc, ln_g_c, ln_b_c, sgu_w_bf, sgu_b_b)
            total = lctx + n
            prev = _outproj(oc_a, oc_b, oc_c, w_out_bf, x_ctx, 0, mod_ctx[l], g1, b1, total, 0, None)
            x1, h2 = _outproj(o_a, o_b, o_c, w_out_bf, x_lat, lat_off, mod_lat[l], g1, b1, total,
                              n_ctx_blocks, prev)
            x_all = _moe(h2, x1, jnp.stack([mod_ctx[l], mod_lat[l]]), n_ctx_blocks, wr_t, rb_b,
                         w_gate, w_up, w_down, l, g2, b2)
            x_lat, lat_off, x_ctx = x_all, n_ctx_blocks, x_all
    return x_lat.reshape(batch, n, d)
```

```python
import functools

import numpy as np
import jax
import jax.numpy as jnp
from jax import lax
from jax.experimental import pallas as pl
from jax.experimental.pallas import tpu as pltpu

F32 = jnp.float32
BF16 = jnp.bfloat16

D_MODEL = 2048
DEPTH = 2
GRID_W = 64
HEAD_DIM = 128
A_HEADS = 6
A_KV_HEADS = 2
A_GROUP = A_HEADS // A_KV_HEADS
A_BLOCK = 128
B_HEADS = 6
NA_ROWS = 8
NA_COLS = 16
C_GROUPS = 4
C_W = C_GROUPS * HEAD_DIM
CHUNK = 128
N_EXPERTS = 32
N_EXPERT_GROUPS = 4
EXPERTS_PER_GROUP = N_EXPERTS // N_EXPERT_GROUPS
TOP_K = 2
EXPERT_FF = 512
ROPE_BASE = 10000.0
LN_EPS = 1e-5
NEG_INF = -1e30
DEEPNORM_ALPHA = (2 * DEPTH) ** 0.25
ATTN_SCALE = HEAD_DIM ** -0.5

A_Q_W = A_HEADS * HEAD_DIM
A_KV_W = A_KV_HEADS * HEAD_DIM
B_W = B_HEADS * HEAD_DIM
OFF_AK = A_Q_W
OFF_AV = OFF_AK + A_KV_W
OFF_BQ = OFF_AV + A_KV_W
OFF_BK = OFF_BQ + B_W
OFF_BV = OFF_BK + B_W
OFF_C = OFF_BV + B_W
IN_COLS = OFF_C + 2 * C_W

VMEM_LIMIT_BYTES = 56 * 1024 * 1024
LANES = 128

ROW_TILE = 256
PROJ_COL_TILE = 512
MOD_COL_TILE = 1024
MOE_TILE = 256

MOD_SH1, MOD_SC1, MOD_G1, MOD_SH2, MOD_SC2, MOD_G2 = range(6)


def _params(*sem):
    return pltpu.CompilerParams(dimension_semantics=sem, vmem_limit_bytes=VMEM_LIMIT_BYTES)


def _layer_norm(t, g, b):
    mu = jnp.mean(t, axis=-1, keepdims=True)
    d = t - mu
    var = jnp.mean(d * d, axis=-1, keepdims=True)
    return d * lax.rsqrt(var + LN_EPS) * g + b


def _sigmoid(v):
    return 1.0 / (1.0 + jnp.exp(-v))


def _dot_nt(a, b):
    return lax.dot_general(a, b, (((1,), (1,)), ((), ())), preferred_element_type=F32)


PACK_W = D_MODEL // 2
PACK_SUB = PACK_W // LANES


def _pack_rows(v):
    bits = pltpu.bitcast(v.astype(BF16).astype(F32), jnp.int32)
    return lax.shift_right_logical(bits[:, :PACK_W], 16) | bits[:, PACK_W:]


def _store_packed(ref, words, m):
    for s in range(PACK_SUB):
        ref[pl.ds(s, m, stride=PACK_SUB), :] = words[:, s * LANES:(s + 1) * LANES]


def _load_packed(ref, m):
    lo, hi = [], []
    for s in range(PACK_SUB):
        w = ref[pl.ds(s, m, stride=PACK_SUB), :]
        lo.append(pltpu.bitcast(w << 16, F32))
        hi.append(pltpu.bitcast(w & jnp.int32(-65536), F32))
    return jnp.concatenate(lo + hi, axis=1)


def _mod_kernel(c_ref, w_ref, b_ref, o_ref):
    w = w_ref[...]
    reps = w.shape[1] // LANES
    rows = []
    for r in range(2):
        cv = c_ref[r]
        s = cv * _sigmoid(cv)
        sb = jnp.concatenate([s] * reps, axis=1)
        rows.append(jnp.sum(w * sb, axis=0, keepdims=True) + b_ref[...])
    rows.append(jnp.zeros((6, w.shape[1]), F32))
    o_ref[...] = jnp.concatenate(rows, axis=0)


def _modulation(c, c_ctx, w_mod, b_mod):
    d = D_MODEL
    cb = jnp.stack([jnp.broadcast_to(c.reshape(d, 1), (d, LANES)),
                    jnp.broadcast_to(c_ctx.reshape(d, 1), (d, LANES))])
    n_out = 6 * d
    return pl.pallas_call(
        _mod_kernel,
        grid=(DEPTH, n_out // MOD_COL_TILE),
        in_specs=[pl.BlockSpec((2, d, LANES), lambda l, j: (0, 0, 0)),
                  pl.BlockSpec((None, d, MOD_COL_TILE), lambda l, j: (l, 0, j)),
                  pl.BlockSpec((None, 1, MOD_COL_TILE), lambda l, j: (l, 0, j))],
        out_specs=pl.BlockSpec((None, 8, MOD_COL_TILE), lambda l, j: (l, 0, j)),
        out_shape=jax.ShapeDtypeStruct((DEPTH, 8, n_out), F32),
        compiler_params=_params("arbitrary", "arbitrary"),
        name="modulation",
    )(cb, w_mod, b_mod.reshape(DEPTH, 1, n_out))


def _gelu_tanh(v):
    return 0.5 * v * (1.0 + jnp.tanh(np.sqrt(2.0 / np.pi).astype(np.float32) * (v + 0.044715 * (v * v * v))))


def _proj_kernel(x_ref, mod_ref, cos_ref, sin_ref, w_ref, o_ref, *, rope):
    x = x_ref[...]
    h = (x * (1.0 + mod_ref[MOD_SC1:MOD_SC1 + 1, :]) + mod_ref[MOD_SH1:MOD_SH1 + 1, :]).astype(BF16)
    tn = PROJ_COL_TILE
    for j in range(IN_COLS // tn):
        c0 = j * tn
        acc = jnp.dot(h, w_ref[:, c0:c0 + tn], preferred_element_type=F32)
        if c0 < OFF_AV:
            if rope:
                cos = cos_ref[...]
                sin = sin_ref[...]
                parts = []
                for hh in range(tn // HEAD_DIM):
                    a = acc[:, hh * HEAD_DIM:(hh + 1) * HEAD_DIM]
                    parts.append(a * cos + pltpu.roll(a, HEAD_DIM // 2, 1) * sin)
                acc = jnp.concatenate(parts, axis=1)
        elif c0 >= OFF_C:
            acc = _gelu_tanh(acc)
        o_ref[:, c0:c0 + tn] = acc.astype(BF16)


def _proj(x, x_off, rows, mod, cos, sin, w_bf, rope):
    d = D_MODEL
    return pl.pallas_call(
        functools.partial(_proj_kernel, rope=rope),
        grid=(rows // ROW_TILE,),
        in_specs=[pl.BlockSpec((ROW_TILE, d), lambda i: (i + x_off, 0)),
                  pl.BlockSpec((8, d), lambda i: (0, 0)),
                  pl.BlockSpec((ROW_TILE, HEAD_DIM), lambda i: (i, 0)),
                  pl.BlockSpec((ROW_TILE, HEAD_DIM), lambda i: (i, 0)),
                  pl.BlockSpec((d, IN_COLS), lambda i: (0, 0), pipeline_mode=pl.Buffered(1))],
        out_specs=pl.BlockSpec((ROW_TILE, IN_COLS), lambda i: (i, 0)),
        out_shape=jax.ShapeDtypeStruct((rows, IN_COLS), BF16),
        compiler_params=_params("arbitrary"),
        name="proj_rope" if rope else "proj_ctx",
    )(x, mod, cos, sin, w_bf)


def _softmax_pv(s_parts, v_parts, sink):
    m = s_parts[0].max(axis=-1, keepdims=True)
    for s in s_parts[1:]:
        m = jnp.maximum(m, s.max(axis=-1, keepdims=True))
    if sink is not None:
        m = jnp.maximum(m, sink)
    denom = None if sink is None else jnp.exp(sink - m)
    out = None
    for s, v in zip(s_parts, v_parts):
        e = jnp.exp(s - m)
        es = e.sum(axis=-1, keepdims=True)
        denom = es if denom is None else denom + es
        pv = jnp.dot(e.astype(BF16), v, preferred_element_type=F32)
        out = pv if out is None else out + pv
    return out / denom


def _attn_a_kernel(sink_ref, q_ref, kp_ref, kc_ref, kn_ref, vp_ref, vc_ref, vn_ref, kx_ref, vx_ref, o_ref,
                   mask_ref, *, latent):
    i = pl.program_id(0)
    nb = pl.num_programs(0)
    nq = A_GROUP * A_BLOCK

    if latent:
        @pl.when(i == 0)
        def _():
            qi = lax.broadcasted_iota(jnp.int32, (nq, 3 * A_BLOCK), 0) % A_BLOCK
            jj = lax.broadcasted_iota(jnp.int32, (nq, 3 * A_BLOCK), 1)
            ok = (jj >= qi) & (jj <= qi + 2 * A_BLOCK)
            mask_ref[...] = jnp.where(ok, 0.0, NEG_INF).astype(F32)

        col = lax.broadcasted_iota(jnp.int32, (nq, 3 * A_BLOCK), 1)
        off_band = ((col < A_BLOCK) & (i == 0)) | ((col >= 2 * A_BLOCK) & (i == nb - 1))

    for kh in range(A_KV_HEADS):
        hs = [kh * A_GROUP + g for g in range(A_GROUP)]
        q = jnp.concatenate([q_ref[:, h * HEAD_DIM:(h + 1) * HEAD_DIM] for h in hs], axis=0)
        sink = jnp.concatenate([jnp.full((A_BLOCK, 1), sink_ref[h], F32) for h in hs], axis=0)
        ks = slice(kh * HEAD_DIM, (kh + 1) * HEAD_DIM)
        s_parts = [_dot_nt(q, kx_ref[:, ks]) * ATTN_SCALE]
        v_parts = [vx_ref[:, ks]]
        if latent:
            kband = jnp.concatenate([kp_ref[:, ks], kc_ref[:, ks], kn_ref[:, ks]], axis=0)
            vband = jnp.concatenate([vp_ref[:, ks], vc_ref[:, ks], vn_ref[:, ks]], axis=0)
            s_loc = _dot_nt(q, kband) * ATTN_SCALE + mask_ref[...]
            s_parts.append(jnp.where(off_band, NEG_INF, s_loc))
            v_parts.append(vband)
        out = _softmax_pv(s_parts, v_parts, sink)
        for g, h in enumerate(hs):
            o_ref[:, h * HEAD_DIM:(h + 1) * HEAD_DIM] = out[g * A_BLOCK:(g + 1) * A_BLOCK].astype(BF16)


def _attn_a(p, pc, sink, latent):
    rows = p.shape[0]
    nb = rows // A_BLOCK
    kcol = OFF_AK // A_KV_W
    vcol = OFF_AV // A_KV_W

    def band(col, shift):
        return pl.BlockSpec((A_BLOCK, A_KV_W), lambda i, s: (jnp.clip(i + shift, 0, nb - 1), col))

    lctx = pc.shape[0]
    return pl.pallas_call(
        functools.partial(_attn_a_kernel, latent=latent),
        grid_spec=pltpu.PrefetchScalarGridSpec(
            num_scalar_prefetch=1,
            grid=(nb,),
            in_specs=[pl.BlockSpec((A_BLOCK, A_Q_W), lambda i, s: (i, 0)),
                      band(kcol, -1), band(kcol, 0), band(kcol, 1),
                      band(vcol, -1), band(vcol, 0), band(vcol, 1),
                      pl.BlockSpec((lctx, A_KV_W), lambda i, s: (0, kcol)),
                      pl.BlockSpec((lctx, A_KV_W), lambda i, s: (0, vcol))],
            out_specs=pl.BlockSpec((A_BLOCK, A_Q_W), lambda i, s: (i, 0)),
            scratch_shapes=[pltpu.VMEM((A_GROUP * A_BLOCK, 3 * A_BLOCK), F32)]),
        out_shape=jax.ShapeDtypeStruct((rows, A_Q_W), BF16),
        compiler_params=_params("arbitrary"),
        name="attn_a_latent" if latent else "attn_a_ctx",
    )(sink, p, p, p, p, p, p, p, pc, pc)


B_PAIR_W = 2 * HEAD_DIM
NA_GROUP_ROWS = ROW_TILE // GRID_W
NA_WIN_ROWS = NA_ROWS + NA_GROUP_ROWS
NA_PAIRS = NA_WIN_ROWS // 2
NA_BIAS_OFFS = 2 * NA_ROWS
NA_BOTH, NA_LEFT, NA_RIGHT = range(3)


def _attn_b_kernel(q_ref, k_ref, v_ref, kx_ref, vx_ref, bias_ref, o_ref, *, latent, grid_rows):
    g = pl.program_id(1)
    if latent:
        r_base = g * NA_GROUP_ROWS
        w0 = jnp.clip(r_base - NA_ROWS // 2, 0, grid_rows - NA_WIN_ROWS)
        start = pl.multiple_of(w0 * GRID_W, GRID_W)
    for hh in range(2):
        hs = slice(hh * HEAD_DIM, (hh + 1) * HEAD_DIM)
        q = q_ref[:, hs]
        s_parts = [_dot_nt(q, kx_ref[:, hs]) * ATTN_SCALE]
        v_parts = [vx_ref[:, hs]]
        if latent:
            kwin = k_ref[pl.ds(start, NA_WIN_ROWS * GRID_W), hs]
            vwin = v_ref[pl.ds(start, NA_WIN_ROWS * GRID_W), hs]
            bias_rows = []
            for rr in range(NA_GROUP_ROWS):
                r = r_base + rr
                r0 = jnp.clip(r - NA_ROWS // 2, 0, grid_rows - NA_ROWS)
                tiles = []
                for jp in range(NA_PAIRS):
                    ka = w0 + 2 * jp
                    in_a = (ka >= r0) & (ka < r0 + NA_ROWS)
                    in_b = (ka + 1 >= r0) & (ka + 1 < r0 + NA_ROWS)
                    variant = jnp.where(in_a, jnp.where(in_b, NA_BOTH, NA_LEFT), jnp.where(in_b, NA_RIGHT, NA_LEFT))
                    off = jnp.where(in_a | in_b, jnp.clip(ka - r + NA_ROWS, 0, NA_BIAS_OFFS - 1), 0)
                    tiles.append(bias_ref[hh, variant, off])
                bias_rows.append(jnp.concatenate(tiles, axis=1))
            bias = jnp.concatenate(bias_rows, axis=0)
            s_parts.append(_dot_nt(q, kwin) * ATTN_SCALE + bias)
            v_parts.append(vwin)
        out = _softmax_pv(s_parts, v_parts, None)
        o_ref[:, hs] = out.astype(BF16)


def _na_bias_table(rpb):
    cols = np.arange(GRID_W)
    c0 = np.clip(cols - NA_COLS // 2, 0, GRID_W - NA_COLS)
    rel = cols[None, :] - cols[:, None] + NA_COLS - 1
    ok = (cols[None, :] >= c0[:, None]) & (cols[None, :] < c0[:, None] + NA_COLS)
    onehot = (rel[None] == np.arange(2 * NA_COLS - 1)[:, None, None]).astype(np.float32)
    t = jnp.einsum("hrd,dqk->hrqk", rpb, onehot, precision=lax.Precision.HIGHEST)
    t = jnp.where(ok[None, None], t, NEG_INF).astype(F32)
    t = jnp.pad(t, ((0, 0), (1, 1), (0, 0), (0, 0)), constant_values=NEG_INF)
    first, second = t[:, :-1], t[:, 1:]
    masked = jnp.full_like(first, NEG_INF)
    return jnp.stack([jnp.concatenate([first, second], axis=-1),
                      jnp.concatenate([first, masked], axis=-1),
                      jnp.concatenate([masked, second], axis=-1)], axis=1)


def _attn_b(p, pc, bias_tab, latent):
    rows = p.shape[0]
    lctx = pc.shape[0]
    assert not latent or rows // GRID_W >= NA_WIN_ROWS
    qcol = OFF_BQ // B_PAIR_W
    kcol = OFF_BK // B_PAIR_W
    vcol = OFF_BV // B_PAIR_W
    return pl.pallas_call(
        functools.partial(_attn_b_kernel, latent=latent, grid_rows=rows // GRID_W),
        grid=(B_HEADS // 2, rows // ROW_TILE),
        in_specs=[pl.BlockSpec((ROW_TILE, B_PAIR_W), lambda hp, g: (g, qcol + hp)),
                  pl.BlockSpec((rows, B_PAIR_W), lambda hp, g: (0, kcol + hp)),
                  pl.BlockSpec((rows, B_PAIR_W), lambda hp, g: (0, vcol + hp)),
                  pl.BlockSpec((lctx, B_PAIR_W), lambda hp, g: (0, kcol + hp)),
                  pl.BlockSpec((lctx, B_PAIR_W), lambda hp, g: (0, vcol + hp)),
                  pl.BlockSpec((2, 3, NA_BIAS_OFFS, GRID_W, 2 * GRID_W), lambda hp, g: (hp, 0, 0, 0, 0))],
        out_specs=pl.BlockSpec((ROW_TILE, B_PAIR_W), lambda hp, g: (g, hp)),
        out_shape=jax.ShapeDtypeStruct((rows, B_W), BF16),
        compiler_params=_params("arbitrary", "arbitrary"),
        name="attn_b_latent" if latent else "attn_b_ctx",
    )(p, p, p, pc, pc, bias_tab)


def _sgu_kernel(u_ref, v_ref, g_ref, b_ref, w_ref, bs_ref, o_ref):
    for ch in range(ROW_TILE // CHUNK):
        rs = slice(ch * CHUNK, (ch + 1) * CHUNK)
        for grp in range(C_GROUPS):
            cs = slice(grp * HEAD_DIM, (grp + 1) * HEAD_DIM)
            vn = _layer_norm(v_ref[rs, cs].astype(F32), g_ref[:, cs], b_ref[:, cs])
            mixed = jnp.dot(w_ref[grp], vn.astype(BF16), preferred_element_type=F32) + bs_ref[grp]
            o_ref[rs, cs] = (u_ref[rs, cs].astype(F32) * mixed).astype(BF16)


def _sgu(p, ln_g, ln_b, w_bf, bs_b):
    rows = p.shape[0]
    ucol = OFF_C // C_W
    return pl.pallas_call(
        _sgu_kernel,
        grid=(rows // ROW_TILE,),
        in_specs=[pl.BlockSpec((ROW_TILE, C_W), lambda i: (i, ucol)),
                  pl.BlockSpec((ROW_TILE, C_W), lambda i: (i, ucol + 1)),
                  pl.BlockSpec((1, C_W), lambda i: (0, 0)),
                  pl.BlockSpec((1, C_W), lambda i: (0, 0)),
                  pl.BlockSpec((C_GROUPS, CHUNK, CHUNK), lambda i: (0, 0, 0)),
                  pl.BlockSpec((C_GROUPS, CHUNK, LANES), lambda i: (0, 0, 0))],
        out_specs=pl.BlockSpec((ROW_TILE, C_W), lambda i: (i, 0)),
        out_shape=jax.ShapeDtypeStruct((rows, C_W), BF16),
        compiler_params=_params("arbitrary"),
        name="sgu",
    )(p, p, ln_g, ln_b, w_bf, bs_b)


def _outproj_kernel(oa_ref, ob_ref, oc_ref, w_ref, x_ref, mod_ref, g_ref, b_ref, *rest):
    x1_ref, h2_ref = rest[-2], rest[-1]
    mix = jnp.dot(oa_ref[...], w_ref[0:A_Q_W, :], preferred_element_type=F32)
    mix += jnp.dot(ob_ref[...], w_ref[A_Q_W:A_Q_W + B_W, :], preferred_element_type=F32)
    mix += jnp.dot(oc_ref[...], w_ref[A_Q_W + B_W:, :], preferred_element_type=F32)
    t = DEEPNORM_ALPHA * x_ref[...] + mod_ref[MOD_G1:MOD_G1 + 1, :] * mix
    x1 = _layer_norm(t, g_ref[...], b_ref[...])
    x1_ref[...] = x1
    h2 = x1 * (1.0 + mod_ref[MOD_SC2:MOD_SC2 + 1, :]) + mod_ref[MOD_SH2:MOD_SH2 + 1, :]
    _store_packed(h2_ref, _pack_rows(h2), ROW_TILE)


def _outproj(o_a, o_b, o_c, w_bf, x, x_off, mod, ln_g, ln_b, total_rows, out_off, prev):
    rows = o_a.shape[0]
    d = D_MODEL
    in_specs = [pl.BlockSpec((ROW_TILE, A_Q_W), lambda i: (i, 0)),
                pl.BlockSpec((ROW_TILE, B_W), lambda i: (i, 0)),
                pl.BlockSpec((ROW_TILE, C_W), lambda i: (i, 0)),
                pl.BlockSpec((d, d), lambda i: (0, 0), pipeline_mode=pl.Buffered(1)),
                pl.BlockSpec((ROW_TILE, d), lambda i: (i + x_off, 0)),
                pl.BlockSpec((8, d), lambda i: (0, 0)),
                pl.BlockSpec((1, d), lambda i: (0, 0)),
                pl.BlockSpec((1, d), lambda i: (0, 0))]
    args = [o_a, o_b, o_c, w_bf, x, mod, ln_g, ln_b]
    aliases = {}
    if prev is not None:
        in_specs += [pl.BlockSpec(memory_space=pl.ANY), pl.BlockSpec(memory_space=pl.ANY)]
        aliases = {len(args): 0, len(args) + 1: 1}
        args += list(prev)
    return pl.pallas_call(
        _outproj_kernel,
        grid=(rows // ROW_TILE,),
        in_specs=in_specs,
        out_specs=[pl.BlockSpec((ROW_TILE, d), lambda i: (i + out_off, 0)),
                   pl.BlockSpec((ROW_TILE * PACK_SUB, LANES), lambda i: (i + out_off, 0))],
        out_shape=[jax.ShapeDtypeStruct((total_rows, d), F32),
                   jax.ShapeDtypeStruct((total_rows * PACK_SUB, LANES), jnp.int32)],
        input_output_aliases=aliases,
        compiler_params=_params("arbitrary"),
        name="outproj",
    )(*args)


def _top2_sublanes(vals, sub):
    m1 = vals.max(axis=0, keepdims=True)
    i1 = jnp.where(vals == m1, sub, vals.shape[0]).min(axis=0, keepdims=True)
    rest = jnp.where(sub == i1, -jnp.inf, vals)
    m2 = rest.max(axis=0, keepdims=True)
    i2 = jnp.where(rest == m2, sub, vals.shape[0]).min(axis=0, keepdims=True)
    return m1, i1, m2, i2


def _router_kernel(h_ref, wr_ref, rb_ref, e_ref, w_ref, rank_ref, cnt_ref, run_ref):
    i = pl.program_id(0)
    tm = ROW_TILE
    epg = EXPERTS_PER_GROUP

    @pl.when(i == 0)
    def _():
        run_ref[...] = jnp.zeros_like(run_ref)

    parts = _dot_nt(wr_ref[...], _load_packed(h_ref, tm).astype(BF16))
    logits = parts[0:N_EXPERTS] + parts[N_EXPERTS:2 * N_EXPERTS] + parts[2 * N_EXPERTS:]
    scores = _sigmoid(logits)
    biased = scores + jnp.concatenate([rb_ref[...]] * (tm // LANES), axis=1)
    sub = lax.broadcasted_iota(jnp.int32, (epg, tm), 0)

    best = None
    for g in range(N_EXPERT_GROUPS):
        m1, _, m2, _ = _top2_sublanes(biased[g * epg:(g + 1) * epg], sub)
        gs = m1 + m2
        if best is None:
            best, grp = gs, jnp.zeros((1, tm), jnp.int32)
            bsel, ssel = biased[0:epg], scores[0:epg]
        else:
            better = gs > best
            best = jnp.where(better, gs, best)
            grp = jnp.where(better, g, grp)
            bsel = jnp.where(better, biased[g * epg:(g + 1) * epg], bsel)
            ssel = jnp.where(better, scores[g * epg:(g + 1) * epg], ssel)
    _, i1, _, i2 = _top2_sublanes(bsel, sub)
    w1 = jnp.where(sub == i1, ssel, 0.0).sum(axis=0, keepdims=True)
    w2 = jnp.where(sub == i2, ssel, 0.0).sum(axis=0, keepdims=True)
    tot = w1 + w2
    e1 = grp * epg + i1
    e2 = grp * epg + i2

    eiota = lax.broadcasted_iota(jnp.int32, (N_EXPERTS, tm), 0)
    oh1 = (eiota == e1).astype(F32)
    oh2 = (eiota == e2).astype(F32)
    ohb = oh1 + oh2
    before = (lax.broadcasted_iota(jnp.int32, (tm, tm), 0) < lax.broadcasted_iota(jnp.int32, (tm, tm), 1))
    prefix = jnp.dot(ohb.astype(BF16), before.astype(BF16), preferred_element_type=F32)
    pos = run_ref[...] + prefix
    r1 = (oh1 * pos).sum(axis=0, keepdims=True)
    r2 = (oh2 * pos).sum(axis=0, keepdims=True)
    run_ref[...] = run_ref[...] + ohb.sum(axis=1, keepdims=True)

    e_ref[...] = jnp.concatenate([e1, e2], axis=0)
    w_ref[...] = jnp.concatenate([w1 / tot, w2 / tot], axis=0)
    rank_ref[...] = jnp.concatenate([r1, r2], axis=0).astype(jnp.int32)
    cnt_ref[...] = run_ref[:, 0:LANES]


def _router(h2p, wr_t, rb_b):
    t = h2p.shape[0] // PACK_SUB
    d = D_MODEL
    row2 = pl.BlockSpec((TOP_K, ROW_TILE), lambda i: (0, i))
    return pl.pallas_call(
        _router_kernel,
        grid=(t // ROW_TILE,),
        in_specs=[pl.BlockSpec((ROW_TILE * PACK_SUB, LANES), lambda i: (i, 0)),
                  pl.BlockSpec((3 * N_EXPERTS, d), lambda i: (0, 0)),
                  pl.BlockSpec((N_EXPERTS, LANES), lambda i: (0, 0))],
        out_specs=[row2, row2, row2, pl.BlockSpec((N_EXPERTS, LANES), lambda i: (0, 0))],
        out_shape=[jax.ShapeDtypeStruct((TOP_K, t), jnp.int32),
                   jax.ShapeDtypeStruct((TOP_K, t), F32),
                   jax.ShapeDtypeStruct((TOP_K, t), jnp.int32),
                   jax.ShapeDtypeStruct((N_EXPERTS, LANES), F32)],
        scratch_shapes=[pltpu.VMEM((N_EXPERTS, ROW_TILE), F32)],
        compiler_params=_params("arbitrary"),
        name="router",
    )(h2p, wr_t, rb_b)


def _row_copy(src_hbm, first_sublane, dst, sem):
    return pltpu.make_async_copy(src_hbm.at[pl.ds(pl.multiple_of(first_sublane, PACK_SUB), PACK_SUB), :], dst, sem)


def _dispatch_kernel(slot_ref, h_hbm, xs_hbm, sem, *, n_tok):
    i = pl.program_id(0)
    nt = pl.num_programs(0)
    tm = ROW_TILE

    def wait(par):
        for _ in range(TOP_K):
            pltpu.make_async_copy(h_hbm.at[pl.ds(0, tm * PACK_SUB), :], xs_hbm.at[pl.ds(0, tm * PACK_SUB), :],
                                  sem.at[par]).wait()

    par = i % 2
    for r in range(tm):
        tok = i * tm + r
        src = h_hbm.at[pl.ds(pl.multiple_of(tok * PACK_SUB, PACK_SUB), PACK_SUB), :]
        for k in range(TOP_K):
            dst_row = pl.multiple_of(slot_ref[k * n_tok + tok], PACK_SUB)
            pltpu.make_async_copy(src, xs_hbm.at[pl.ds(dst_row, PACK_SUB), :], sem.at[par]).start(priority=k)

    @pl.when(i > 0)
    def _():
        wait(1 - par)

    @pl.when(i == nt - 1)
    def _():
        wait(par)


def _dispatch(h2p, slots_sub, rows):
    t = h2p.shape[0] // PACK_SUB
    return pl.pallas_call(
        functools.partial(_dispatch_kernel, n_tok=t),
        grid_spec=pltpu.PrefetchScalarGridSpec(
            num_scalar_prefetch=1,
            grid=(t // ROW_TILE,),
            in_specs=[pl.BlockSpec(memory_space=pl.ANY)],
            out_specs=pl.BlockSpec(memory_space=pl.ANY),
            scratch_shapes=[pltpu.SemaphoreType.DMA((2,))]),
        out_shape=jax.ShapeDtypeStruct((rows * PACK_SUB, LANES), jnp.int32),
        compiler_params=_params("arbitrary"),
        name="dispatch",
    )(slots_sub, h2p)


def _experts_kernel(ts_ref, nu_ref, h_hbm, wg_hbm, wu_hbm, wd_hbm, ys_hbm,
                    xbuf, ybuf, gsem, ysem, wg_f, wu_f, wd_f, wsem, wg_bf, wu_bf, wd_bf, *, layer):
    e = pl.program_id(0)
    ne = pl.num_programs(0)
    n_used = nu_ref[0]
    tm = MOE_TILE
    tsub = tm * PACK_SUB

    def has_rows(ex):
        return ts_ref[ex + 1] > ts_ref[ex]

    def weight_copies(ex, wslot):
        return [pltpu.make_async_copy(src.at[layer, ex], dst.at[wslot], wsem.at[wslot])
                for src, dst in ((wg_hbm, wg_f), (wu_hbm, wu_f), (wd_hbm, wd_f))]

    def start_weights(ex, wslot):
        for cp in weight_copies(ex, wslot):
            cp.start(priority=1)

    def x_load(tile, slot):
        row0 = pl.multiple_of(tile * tsub, tsub)
        return pltpu.make_async_copy(h_hbm.at[pl.ds(row0, tsub), :], xbuf.at[slot], gsem.at[slot])

    def gather(tile, slot):
        x_load(tile, slot).start()

    def wait_gather(slot):
        x_load(0, slot).wait()

    def y_store(tile, slot):
        row0 = pl.multiple_of(tile * tsub, tsub)
        return pltpu.make_async_copy(ybuf.at[slot], ys_hbm.at[pl.ds(row0, tsub), :], ysem.at[slot])

    wslot = e % 2

    @pl.when((e == 0) & has_rows(0))
    def _():
        start_weights(0, 0)

    @pl.when((e == 0) & (n_used > 0))
    def _():
        gather(0, 0)

    nxt = jnp.minimum(e + 1, ne - 1)

    @pl.when((e + 1 < ne) & has_rows(nxt))
    def _():
        start_weights(nxt, 1 - wslot)

    @pl.when(has_rows(e))
    def _():
        for cp in weight_copies(e, wslot):
            cp.wait()
        wg_bf[...] = wg_f[wslot].astype(BF16)
        wu_bf[...] = wu_f[wslot].astype(BF16)
        wd_bf[...] = wd_f[wslot].astype(BF16)

    def tile_body(g, carry):
        slot = g % 2

        @pl.when(g >= 2)
        def _():
            y_store(g - 2, slot).wait()

        wait_gather(slot)
        x = _load_packed(xbuf.at[slot], tm).astype(BF16)
        gather(jnp.minimum(g + 1, n_used - 1), 1 - slot)
        gate = jnp.dot(x, wg_bf[...], preferred_element_type=F32)
        up = jnp.dot(x, wu_bf[...], preferred_element_type=F32)
        hid = (gate * _sigmoid(gate)) * up
        y = jnp.dot(hid.astype(BF16), wd_bf[...], preferred_element_type=F32)
        _store_packed(ybuf.at[slot], _pack_rows(y), tm)
        y_store(g, slot).start(priority=1)
        return carry

    lax.fori_loop(ts_ref[e], ts_ref[e + 1], tile_body, 0)

    @pl.when((e == pl.num_programs(0) - 1) & (n_used > 0))
    def _():
        wait_gather(n_used % 2)
        y_store(n_used - 1, (n_used - 1) % 2).wait()

        @pl.when(n_used >= 2)
        def _():
            y_store(n_used - 2, n_used % 2).wait()


def _experts(xs, tile_start, n_used, w_gate, w_up, w_down, layer):
    d = D_MODEL
    tm = MOE_TILE
    rows = xs.shape[0] // PACK_SUB

    hbm = pl.BlockSpec(memory_space=pl.ANY)
    return pl.pallas_call(
        functools.partial(_experts_kernel, layer=layer),
        grid_spec=pltpu.PrefetchScalarGridSpec(
            num_scalar_prefetch=2,
            grid=(N_EXPERTS,),
            in_specs=[hbm, hbm, hbm, hbm],
            out_specs=hbm,
            scratch_shapes=[pltpu.VMEM((2, tm * PACK_SUB, LANES), jnp.int32),
                            pltpu.VMEM((2, tm * PACK_SUB, LANES), jnp.int32),
                            pltpu.SemaphoreType.DMA((2,)),
                            pltpu.SemaphoreType.DMA((2,)),
                            pltpu.VMEM((2, d, EXPERT_FF), F32),
                            pltpu.VMEM((2, d, EXPERT_FF), F32),
                            pltpu.VMEM((2, EXPERT_FF, d), F32),
                            pltpu.SemaphoreType.DMA((2,)),
                            pltpu.VMEM((d, EXPERT_FF), BF16),
                            pltpu.VMEM((d, EXPERT_FF), BF16),
                            pltpu.VMEM((EXPERT_FF, d), BF16)]),
        out_shape=jax.ShapeDtypeStruct((rows * PACK_SUB, LANES), jnp.int32),
        compiler_params=_params("arbitrary"),
        name="experts",
    )(tile_start, n_used, xs, w_gate, w_up, w_down)


def _combine_kernel(slot_ref, ys_hbm, w_ref, x_ref, mod_ref, g_ref, b_ref, o_ref, ybuf, sem, *, n_tok):
    i = pl.program_id(0)
    nt = pl.num_programs(0)
    tm = ROW_TILE

    def issue(blk, buf):
        for r in range(tm):
            for k in range(TOP_K):
                row = slot_ref[k * n_tok + blk * tm + r]
                _row_copy(ys_hbm, row, ybuf.at[buf, k, pl.ds(r * PACK_SUB, PACK_SUB), :],
                          sem.at[buf]).start(priority=k)

    def wait(buf):
        for k in range(TOP_K):
            pltpu.make_async_copy(ys_hbm.at[pl.ds(0, tm * PACK_SUB), :], ybuf.at[buf, k], sem.at[buf]).wait()

    @pl.when(i == 0)
    def _():
        issue(0, 0)

    buf = i % 2
    wait(buf)
    issue(jnp.minimum(i + 1, nt - 1), 1 - buf)
    y = (w_ref[:, 0:1] * _load_packed(ybuf.at[buf, 0], tm)
         + w_ref[:, 1:2] * _load_packed(ybuf.at[buf, 1], tm))
    t = DEEPNORM_ALPHA * x_ref[...] + mod_ref[MOD_G2:MOD_G2 + 1, :] * y
    o_ref[...] = _layer_norm(t, g_ref[...], b_ref[...])

    @pl.when(i == nt - 1)
    def _():
        wait(1 - buf)


def _combine(ys, slots, w_tok, x1, mods, n_ctx_blocks, ln_g, ln_b):
    t = x1.shape[0]
    d = D_MODEL
    return pl.pallas_call(
        functools.partial(_combine_kernel, n_tok=t),
        grid_spec=pltpu.PrefetchScalarGridSpec(
            num_scalar_prefetch=1,
            grid=(t // ROW_TILE,),
            in_specs=[pl.BlockSpec(memory_space=pl.ANY),
                      pl.BlockSpec((ROW_TILE, TOP_K), lambda i, s: (i, 0)),
                      pl.BlockSpec((ROW_TILE, d), lambda i, s: (i, 0)),
                      pl.BlockSpec((None, 8, d), lambda i, s: (jnp.where(i < n_ctx_blocks, 0, 1), 0, 0)),
                      pl.BlockSpec((1, d), lambda i, s: (0, 0)),
                      pl.BlockSpec((1, d), lambda i, s: (0, 0))],
            out_specs=pl.BlockSpec((ROW_TILE, d), lambda i, s: (i, 0)),
            scratch_shapes=[pltpu.VMEM((2, TOP_K, ROW_TILE * PACK_SUB, LANES), jnp.int32),
                            pltpu.SemaphoreType.DMA((2,))]),
        out_shape=jax.ShapeDtypeStruct((t, d), F32),
        compiler_params=_params("arbitrary"),
        name="combine",
    )(slots, ys, w_tok, x1, mods, ln_g, ln_b)


def _moe(h2, x1, mods, n_ctx_blocks, wr_t, rb_b, w_gate, w_up, w_down, layer, ln_g, ln_b):
    t = x1.shape[0]
    tm = MOE_TILE
    e_idx, w_tok, rank, cnt = _router(h2, wr_t, rb_b)
    counts = cnt[:, 0].astype(jnp.int32)
    tiles_per = (counts + tm - 1) // tm
    tile_end = jnp.cumsum(tiles_per)
    n_used = tile_end[-1]
    row_off = (tile_end - tiles_per) * tm
    experts = jnp.arange(N_EXPERTS, dtype=jnp.int32)
    slots = jnp.sum(jnp.where(e_idx[:, :, None] == experts, row_off, 0), axis=-1) + rank
    max_tiles = (TOP_K * t + N_EXPERTS * (tm - 1)) // tm + 1
    tile_start = jnp.concatenate([tile_end - tiles_per, n_used.reshape(1)]).astype(jnp.int32)
    slots_sub = slots.reshape(-1) * PACK_SUB
    xs = _dispatch(h2, slots_sub, max_tiles * tm)
    ys = _experts(xs, tile_start, n_used.reshape(1).astype(jnp.int32), w_gate, w_up, w_down, layer)
    return _combine(ys, slots_sub, w_tok.T, x1, mods, n_ctx_blocks, ln_g, ln_b)


def _rope_tables(n):
    t = jnp.arange(n)
    row = (t // GRID_W).astype(F32)
    col = (t % GRID_W).astype(F32)
    n_freq = HEAD_DIM // 4
    inv_freq = ROPE_BASE ** (-jnp.arange(n_freq, dtype=F32) / n_freq)
    ang = jnp.concatenate([row[:, None] * inv_freq, col[:, None] * inv_freq], axis=-1)
    cos, sin = jnp.cos(ang), jnp.sin(ang)
    return jnp.concatenate([cos, cos], axis=-1), jnp.concatenate([-sin, sin], axis=-1)


def kernel(x, c, ctx, c_ctx, w_mod, b_mod, w_in, attn_sink, na_rpb, sgu_ln_g, sgu_ln_b, sgu_w, sgu_b,
           w_out, ln1_g, ln1_b, w_router, router_bias, w_gate, w_up, w_down, ln2_g, ln2_b):
    batch, n, d = x.shape
    lctx = ctx.shape[1]
    assert batch == 1 and d == D_MODEL and n % ROW_TILE == 0 and lctx % ROW_TILE == 0
    n_ctx_blocks = lctx // ROW_TILE

    mods = _modulation(c, c_ctx, w_mod, b_mod).reshape(DEPTH, 8, 6, d)
    mod_lat = jnp.pad(mods[:, 0], ((0, 0), (0, 2), (0, 0)))
    mod_ctx = jnp.pad(mods[:, 1], ((0, 0), (0, 2), (0, 0)))
    cos, sin = _rope_tables(n)
    wr_f = w_router.T
    wr_hi = lax.reduce_precision(wr_f, exponent_bits=8, mantissa_bits=7)
    wr_mid = lax.reduce_precision(wr_f - wr_hi, exponent_bits=8, mantissa_bits=7)
    wr_lo = wr_f - wr_hi - wr_mid
    wr_t = jnp.concatenate([wr_hi, wr_mid, wr_lo], axis=0).astype(BF16)
    rb_b = jnp.broadcast_to(router_bias.reshape(N_EXPERTS, 1), (N_EXPERTS, LANES))

    x_lat, lat_off = x[0], 0
    x_ctx = ctx[0]
    for l in range(DEPTH):
        last = l == DEPTH - 1
        w_in_bf = w_in[l].astype(BF16)
        w_out_bf = w_out[l].astype(BF16)
        sgu_w_bf = sgu_w[l].astype(BF16)
        sgu_b_b = jnp.broadcast_to(sgu_b[l][:, :, None], (C_GROUPS, CHUNK, LANES))
        ln_g_c, ln_b_c = sgu_ln_g[l].reshape(1, C_W), sgu_ln_b[l].reshape(1, C_W)
        bias_tab = _na_bias_table(na_rpb[l])
        g1, b1 = ln1_g[l].reshape(1, d), ln1_b[l].reshape(1, d)
        g2, b2 = ln2_g[l].reshape(1, d), ln2_b[l].reshape(1, d)

        p = _proj(x_lat, lat_off, n, mod_lat[l], cos, sin, w_in_bf, rope=True)
        pc = _proj(x_ctx, 0, lctx, mod_ctx[l], cos, sin, w_in_bf, rope=False)
        o_a = _attn_a(p, pc, attn_sink[l], latent=True)
        o_b = _attn_b(p, pc, bias_tab, latent=True)
        o_c = _sgu(p, ln_g_c, ln_b_c, sgu_w_bf, sgu_b_b)
        if last:
            x1, h2 = _outproj(o_a, o_b, o_c, w_out_bf, x_lat, lat_off, mod_lat[l], g1, b1, n, 0, None)
            x_lat = _moe(h2, x1, jnp.stack([mod_ctx[l], mod_lat[l]]), 0, wr_t, rb_b,
                         w_gate, w_up, w_down, l, g2, b2)
        else:
            oc_a = _attn_a(pc, pc, attn_sink[l], latent=False)
            oc_b = _attn_b(pc, pc, bias_tab, latent=False)
            oc_c = _sgu(pc, ln_g_c, ln_b_c, sgu_w_bf, sgu_b_b)
            total = lctx + n
            prev = _outproj(oc_a, oc_b, oc_c, w_out_bf, x_ctx, 0, mod_ctx[l], g1, b1, total, 0, None)
            x1, h2 = _outproj(o_a, o_b, o_c, w_out_bf, x_lat, lat_off, mod_lat[l], g1, b1, total,
                              n_ctx_blocks, prev)
            x_all = _moe(h2, x1, jnp.stack([mod_ctx[l], mod_lat[l]]), n_ctx_blocks, wr_t, rb_b,
                         w_gate, w_up, w_down, l, g2, b2)
            x_lat, lat_off, x_ctx = x_all, n_ctx_blocks, x_all
    return x_lat.reshape(batch, n, d)
```

```python
import functools

import numpy as np
import jax
import jax.numpy as jnp
from jax import lax
from jax.experimental import pallas as pl
from jax.experimental.pallas import tpu as pltpu

F32 = jnp.float32
BF16 = jnp.bfloat16

D_MODEL = 2048
DEPTH = 2
GRID_W = 64
HEAD_DIM = 128
A_HEADS = 6
A_KV_HEADS = 2
A_GROUP = A_HEADS // A_KV_HEADS
A_BLOCK = 128
B_HEADS = 6
NA_ROWS = 8
NA_COLS = 16
C_GROUPS = 4
C_W = C_GROUPS * HEAD_DIM
CHUNK = 128
N_EXPERTS = 32
N_EXPERT_GROUPS = 4
EXPERTS_PER_GROUP = N_EXPERTS // N_EXPERT_GROUPS
TOP_K = 2
EXPERT_FF = 512
ROPE_BASE = 10000.0
LN_EPS = 1e-5
NEG_INF = -1e30
DEEPNORM_ALPHA = (2 * DEPTH) ** 0.25
ATTN_SCALE = HEAD_DIM ** -0.5

A_Q_W = A_HEADS * HEAD_DIM
A_KV_W = A_KV_HEADS * HEAD_DIM
B_W = B_HEADS * HEAD_DIM
OFF_AK = A_Q_W
OFF_AV = OFF_AK + A_KV_W
OFF_BQ = OFF_AV + A_KV_W
OFF_BK = OFF_BQ + B_W
OFF_BV = OFF_BK + B_W
OFF_C = OFF_BV + B_W
IN_COLS = OFF_C + 2 * C_W

VMEM_LIMIT_BYTES = 56 * 1024 * 1024
LANES = 128

ROW_TILE = 256
PROJ_COL_TILE = 512
MOD_COL_TILE = 1024
MOE_TILE = 256

MOD_SH1, MOD_SC1, MOD_G1, MOD_SH2, MOD_SC2, MOD_G2 = range(6)


def _params(*sem):
    return pltpu.CompilerParams(dimension_semantics=sem, vmem_limit_bytes=VMEM_LIMIT_BYTES)


def _layer_norm(t, g, b):
    mu = jnp.mean(t, axis=-1, keepdims=True)
    d = t - mu
    var = jnp.mean(d * d, axis=-1, keepdims=True)
    return d * lax.rsqrt(var + LN_EPS) * g + b


def _sigmoid(v):
    return 1.0 / (1.0 + jnp.exp(-v))


def _dot_nt(a, b):
    return lax.dot_general(a, b, (((1,), (1,)), ((), ())), preferred_element_type=F32)


PACK_W = D_MODEL // 2
PACK_SUB = PACK_W // LANES


def _pack_rows(v):
    bits = pltpu.bitcast(v.astype(BF16).astype(F32), jnp.int32)
    return lax.shift_right_logical(bits[:, :PACK_W], 16) | bits[:, PACK_W:]


def _store_packed(ref, words, m):
    for s in range(PACK_SUB):
        ref[pl.ds(s, m, stride=PACK_SUB), :] = words[:, s * LANES:(s + 1) * LANES]


def _load_packed(ref, m):
    lo, hi = [], []
    for s in range(PACK_SUB):
        w = ref[pl.ds(s, m, stride=PACK_SUB), :]
        lo.append(pltpu.bitcast(w << 16, F32))
        hi.append(pltpu.bitcast(w & jnp.int32(-65536), F32))
    return jnp.concatenate(lo + hi, axis=1)


def _mod_kernel(c_ref, w_ref, b_ref, o_ref):
    w = w_ref[...]
    reps = w.shape[1] // LANES
    rows = []
    for r in range(2):
        cv = c_ref[r]
        s = cv * _sigmoid(cv)
        sb = jnp.concatenate([s] * reps, axis=1)
        rows.append(jnp.sum(w * sb, axis=0, keepdims=True) + b_ref[...])
    rows.append(jnp.zeros((6, w.shape[1]), F32))
    o_ref[...] = jnp.concatenate(rows, axis=0)


def _modulation(c, c_ctx, w_mod, b_mod):
    d = D_MODEL
    cb = jnp.stack([jnp.broadcast_to(c.reshape(d, 1), (d, LANES)),
                    jnp.broadcast_to(c_ctx.reshape(d, 1), (d, LANES))])
    n_out = 6 * d
    return pl.pallas_call(
        _mod_kernel,
        grid=(DEPTH, n_out // MOD_COL_TILE),
        in_specs=[pl.BlockSpec((2, d, LANES), lambda l, j: (0, 0, 0)),
                  pl.BlockSpec((None, d, MOD_COL_TILE), lambda l, j: (l, 0, j)),
                  pl.BlockSpec((None, 1, MOD_COL_TILE), lambda l, j: (l, 0, j))],
        out_specs=pl.BlockSpec((None, 8, MOD_COL_TILE), lambda l, j: (l, 0, j)),
        out_shape=jax.ShapeDtypeStruct((DEPTH, 8, n_out), F32),
        compiler_params=_params("arbitrary", "arbitrary"),
        name="modulation",
    )(cb, w_mod, b_mod.reshape(DEPTH, 1, n_out))


def _gelu_tanh(v):
    return 0.5 * v * (1.0 + jnp.tanh(np.sqrt(2.0 / np.pi).astype(np.float32) * (v + 0.044715 * (v * v * v))))


def _proj_kernel(x_ref, mod_ref, cos_ref, sin_ref, w_ref, o_ref, *, rope):
    x = x_ref[...]
    h = (x * (1.0 + mod_ref[MOD_SC1:MOD_SC1 + 1, :]) + mod_ref[MOD_SH1:MOD_SH1 + 1, :]).astype(BF16)
    tn = PROJ_COL_TILE
    for j in range(IN_COLS // tn):
        c0 = j * tn
        acc = jnp.dot(h, w_ref[:, c0:c0 + tn], preferred_element_type=F32)
        if c0 < OFF_AV:
            if rope:
                cos = cos_ref[...]
                sin = sin_ref[...]
                parts = []
                for hh in range(tn // HEAD_DIM):
                    a = acc[:, hh * HEAD_DIM:(hh + 1) * HEAD_DIM]
                    parts.append(a * cos + pltpu.roll(a, HEAD_DIM // 2, 1) * sin)
                acc = jnp.concatenate(parts, axis=1)
        elif c0 >= OFF_C:
            acc = _gelu_tanh(acc)
        o_ref[:, c0:c0 + tn] = acc.astype(BF16)


def _proj(x, x_off, rows, mod, cos, sin, w_bf, rope):
    d = D_MODEL
    return pl.pallas_call(
        functools.partial(_proj_kernel, rope=rope),
        grid=(rows // ROW_TILE,),
        in_specs=[pl.BlockSpec((ROW_TILE, d), lambda i: (i + x_off, 0)),
                  pl.BlockSpec((8, d), lambda i: (0, 0)),
                  pl.BlockSpec((ROW_TILE, HEAD_DIM), lambda i: (i, 0)),
                  pl.BlockSpec((ROW_TILE, HEAD_DIM), lambda i: (i, 0)),
                  pl.BlockSpec((d, IN_COLS), lambda i: (0, 0), pipeline_mode=pl.Buffered(1))],
        out_specs=pl.BlockSpec((ROW_TILE, IN_COLS), lambda i: (i, 0)),
        out_shape=jax.ShapeDtypeStruct((rows, IN_COLS), BF16),
        compiler_params=_params("arbitrary"),
        name="proj_rope" if rope else "proj_ctx",
    )(x, mod, cos, sin, w_bf)


def _softmax_pv(s_parts, v_parts, sink):
    m = s_parts[0].max(axis=-1, keepdims=True)
    for s in s_parts[1:]:
        m = jnp.maximum(m, s.max(axis=-1, keepdims=True))
    if sink is not None:
        m = jnp.maximum(m, sink)
    denom = None if sink is None else jnp.exp(sink - m)
    out = None
    for s, v in zip(s_parts, v_parts):
        e = jnp.exp(s - m)
        es = e.sum(axis=-1, keepdims=True)
        denom = es if denom is None else denom + es
        pv = jnp.dot(e.astype(BF16), v, preferred_element_type=F32)
        out = pv if out is None else out + pv
    return out / denom


def _attn_a_kernel(sink_ref, q_ref, kp_ref, kc_ref, kn_ref, vp_ref, vc_ref, vn_ref, kx_ref, vx_ref, o_ref,
                   mask_ref, *, latent):
    i = pl.program_id(0)
    nb = pl.num_programs(0)
    nq = A_GROUP * A_BLOCK

    if latent:
        @pl.when(i == 0)
        def _():
            qi = lax.broadcasted_iota(jnp.int32, (nq, 3 * A_BLOCK), 0) % A_BLOCK
            jj = lax.broadcasted_iota(jnp.int32, (nq, 3 * A_BLOCK), 1)
            ok = (jj >= qi) & (jj <= qi + 2 * A_BLOCK)
            mask_ref[...] = jnp.where(ok, 0.0, NEG_INF).astype(F32)

        col = lax.broadcasted_iota(jnp.int32, (nq, 3 * A_BLOCK), 1)
        off_band = ((col < A_BLOCK) & (i == 0)) | ((col >= 2 * A_BLOCK) & (i == nb - 1))

    for kh in range(A_KV_HEADS):
        hs = [kh * A_GROUP + g for g in range(A_GROUP)]
        q = jnp.concatenate([q_ref[:, h * HEAD_DIM:(h + 1) * HEAD_DIM] for h in hs], axis=0)
        sink = jnp.concatenate([jnp.full((A_BLOCK, 1), sink_ref[h], F32) for h in hs], axis=0)
        ks = slice(kh * HEAD_DIM, (kh + 1) * HEAD_DIM)
        s_parts = [_dot_nt(q, kx_ref[:, ks]) * ATTN_SCALE]
        v_parts = [vx_ref[:, ks]]
        if latent:
            kband = jnp.concatenate([kp_ref[:, ks], kc_ref[:, ks], kn_ref[:, ks]], axis=0)
            vband = jnp.concatenate([vp_ref[:, ks], vc_ref[:, ks], vn_ref[:, ks]], axis=0)
            s_loc = _dot_nt(q, kband) * ATTN_SCALE + mask_ref[...]
            s_parts.append(jnp.where(off_band, NEG_INF, s_loc))
            v_parts.append(vband)
        out = _softmax_pv(s_parts, v_parts, sink)
        for g, h in enumerate(hs):
            o_ref[:, h * HEAD_DIM:(h + 1) * HEAD_DIM] = out[g * A_BLOCK:(g + 1) * A_BLOCK].astype(BF16)


def _attn_a(p, pc, sink, latent):
    rows = p.shape[0]
    nb = rows // A_BLOCK
    kcol = OFF_AK // A_KV_W
    vcol = OFF_AV // A_KV_W

    def band(col, shift):
        return pl.BlockSpec((A_BLOCK, A_KV_W), lambda i, s: (jnp.clip(i + shift, 0, nb - 1), col))

    lctx = pc.shape[0]
    return pl.pallas_call(
        functools.partial(_attn_a_kernel, latent=latent),
        grid_spec=pltpu.PrefetchScalarGridSpec(
            num_scalar_prefetch=1,
            grid=(nb,),
            in_specs=[pl.BlockSpec((A_BLOCK, A_Q_W), lambda i, s: (i, 0)),
                      band(kcol, -1), band(kcol, 0), band(kcol, 1),
                      band(vcol, -1), band(vcol, 0), band(vcol, 1),
                      pl.BlockSpec((lctx, A_KV_W), lambda i, s: (0, kcol)),
                      pl.BlockSpec((lctx, A_KV_W), lambda i, s: (0, vcol))],
            out_specs=pl.BlockSpec((A_BLOCK, A_Q_W), lambda i, s: (i, 0)),
            scratch_shapes=[pltpu.VMEM((A_GROUP * A_BLOCK, 3 * A_BLOCK), F32)]),
        out_shape=jax.ShapeDtypeStruct((rows, A_Q_W), BF16),
        compiler_params=_params("arbitrary"),
        name="attn_a_latent" if latent else "attn_a_ctx",
    )(sink, p, p, p, p, p, p, p, pc, pc)


B_PAIR_W = 2 * HEAD_DIM
NA_GROUP_ROWS = ROW_TILE // GRID_W
NA_WIN_ROWS = NA_ROWS + NA_GROUP_ROWS
NA_PAIRS = NA_WIN_ROWS // 2
NA_BIAS_OFFS = 2 * NA_ROWS
NA_BOTH, NA_LEFT, NA_RIGHT = range(3)


def _attn_b_kernel(q_ref, k_ref, v_ref, kx_ref, vx_ref, bias_ref, o_ref, *, latent, grid_rows):
    g = pl.program_id(1)
    if latent:
        r_base = g * NA_GROUP_ROWS
        w0 = jnp.clip(r_base - NA_ROWS // 2, 0, grid_rows - NA_WIN_ROWS)
        start = pl.multiple_of(w0 * GRID_W, GRID_W)
    for hh in range(2):
        hs = slice(hh * HEAD_DIM, (hh + 1) * HEAD_DIM)
        q = q_ref[:, hs]
        s_parts = [_dot_nt(q, kx_ref[:, hs]) * ATTN_SCALE]
        v_parts = [vx_ref[:, hs]]
        if latent:
            kwin = k_ref[pl.ds(start, NA_WIN_ROWS * GRID_W), hs]
            vwin = v_ref[pl.ds(start, NA_WIN_ROWS * GRID_W), hs]
            bias_rows = []
            for rr in range(NA_GROUP_ROWS):
                r = r_base + rr
                r0 = jnp.clip(r - NA_ROWS // 2, 0, grid_rows - NA_ROWS)
                tiles = []
                for jp in range(NA_PAIRS):
                    ka = w0 + 2 * jp
                    in_a = (ka >= r0) & (ka < r0 + NA_ROWS)
                    in_b = (ka + 1 >= r0) & (ka + 1 < r0 + NA_ROWS)
                    variant = jnp.where(in_a, jnp.where(in_b, NA_BOTH, NA_LEFT), jnp.where(in_b, NA_RIGHT, NA_LEFT))
                    off = jnp.where(in_a | in_b, jnp.clip(ka - r + NA_ROWS, 0, NA_BIAS_OFFS - 1), 0)
                    tiles.append(bias_ref[hh, variant, off])
                bias_rows.append(jnp.concatenate(tiles, axis=1))
            bias = jnp.concatenate(bias_rows, axis=0)
            s_parts.append(_dot_nt(q, kwin) * ATTN_SCALE + bias)
            v_parts.append(vwin)
        out = _softmax_pv(s_parts, v_parts, None)
        o_ref[:, hs] = out.astype(BF16)


def _na_bias_table(rpb):
    cols = np.arange(GRID_W)
    c0 = np.clip(cols - NA_COLS // 2, 0, GRID_W - NA_COLS)
    rel = cols[None, :] - cols[:, None] + NA_COLS - 1
    ok = (cols[None, :] >= c0[:, None]) & (cols[None, :] < c0[:, None] + NA_COLS)
    onehot = (rel[None] == np.arange(2 * NA_COLS - 1)[:, None, None]).astype(np.float32)
    t = jnp.einsum("hrd,dqk->hrqk", rpb, onehot, precision=lax.Precision.HIGHEST)
    t = jnp.where(ok[None, None], t, NEG_INF).astype(F32)
    t = jnp.pad(t, ((0, 0), (1, 1), (0, 0), (0, 0)), constant_values=NEG_INF)
    first, second = t[:, :-1], t[:, 1:]
    masked = jnp.full_like(first, NEG_INF)
    return jnp.stack([jnp.concatenate([first, second], axis=-1),
                      jnp.concatenate([first, masked], axis=-1),
                      jnp.concatenate([masked, second], axis=-1)], axis=1)


def _attn_b(p, pc, bias_tab, latent):
    rows = p.shape[0]
    lctx = pc.shape[0]
    assert not latent or rows // GRID_W >= NA_WIN_ROWS
    qcol = OFF_BQ // B_PAIR_W
    kcol = OFF_BK // B_PAIR_W
    vcol = OFF_BV // B_PAIR_W
    return pl.pallas_call(
        functools.partial(_attn_b_kernel, latent=latent, grid_rows=rows // GRID_W),
        grid=(B_HEADS // 2, rows // ROW_TILE),
        in_specs=[pl.BlockSpec((ROW_TILE, B_PAIR_W), lambda hp, g: (g, qcol + hp)),
                  pl.BlockSpec((rows, B_PAIR_W), lambda hp, g: (0, kcol + hp)),
                  pl.BlockSpec((rows, B_PAIR_W), lambda hp, g: (0, vcol + hp)),
                  pl.BlockSpec((lctx, B_PAIR_W), lambda hp, g: (0, kcol + hp)),
                  pl.BlockSpec((lctx, B_PAIR_W), lambda hp, g: (0, vcol + hp)),
                  pl.BlockSpec((2, 3, NA_BIAS_OFFS, GRID_W, 2 * GRID_W), lambda hp, g: (hp, 0, 0, 0, 0))],
        out_specs=pl.BlockSpec((ROW_TILE, B_PAIR_W), lambda hp, g: (g, hp)),
        out_shape=jax.ShapeDtypeStruct((rows, B_W), BF16),
        compiler_params=_params("arbitrary", "arbitrary"),
        name="attn_b_latent" if latent else "attn_b_ctx",
    )(p, p, p, pc, pc, bias_tab)


def _sgu_kernel(u_ref, v_ref, g_ref, b_ref, w_ref, bs_ref, o_ref):
    for ch in range(ROW_TILE // CHUNK):
        rs = slice(ch * CHUNK, (ch + 1) * CHUNK)
        for grp in range(C_GROUPS):
            cs = slice(grp * HEAD_DIM, (grp + 1) * HEAD_DIM)
            vn = _layer_norm(v_ref[rs, cs].astype(F32), g_ref[:, cs], b_ref[:, cs])
            mixed = jnp.dot(w_ref[grp], vn.astype(BF16), preferred_element_type=F32) + bs_ref[grp]
            o_ref[rs, cs] = (u_ref[rs, cs].astype(F32) * mixed).astype(BF16)


def _sgu(p, ln_g, ln_b, w_bf, bs_b):
    rows = p.shape[0]
    ucol = OFF_C // C_W
    return pl.pallas_call(
        _sgu_kernel,
        grid=(rows // ROW_TILE,),
        in_specs=[pl.BlockSpec((ROW_TILE, C_W), lambda i: (i, ucol)),
                  pl.BlockSpec((ROW_TILE, C_W), lambda i: (i, ucol + 1)),
                  pl.BlockSpec((1, C_W), lambda i: (0, 0)),
                  pl.BlockSpec((1, C_W), lambda i: (0, 0)),
                  pl.BlockSpec((C_GROUPS, CHUNK, CHUNK), lambda i: (0, 0, 0)),
                  pl.BlockSpec((C_GROUPS, CHUNK, LANES), lambda i: (0, 0, 0))],
        out_specs=pl.BlockSpec((ROW_TILE, C_W), lambda i: (i, 0)),
        out_shape=jax.ShapeDtypeStruct((rows, C_W), BF16),
        compiler_params=_params("arbitrary"),
        name="sgu",
    )(p, p, ln_g, ln_b, w_bf, bs_b)


def _outproj_kernel(oa_ref, ob_ref, oc_ref, w_ref, x_ref, mod_ref, g_ref, b_ref, *rest):
    x1_ref, h2_ref = rest[-2], rest[-1]
    mix = jnp.dot(oa_ref[...], w_ref[0:A_Q_W, :], preferred_element_type=F32)
    mix += jnp.dot(ob_ref[...], w_ref[A_Q_W:A_Q_W + B_W, :], preferred_element_type=F32)
    mix += jnp.dot(oc_ref[...], w_ref[A_Q_W + B_W:, :], preferred_element_type=F32)
    t = DEEPNORM_ALPHA * x_ref[...] + mod_ref[MOD_G1:MOD_G1 + 1, :] * mix
    x1 = _layer_norm(t, g_ref[...], b_ref[...])
    x1_ref[...] = x1
    h2 = x1 * (1.0 + mod_ref[MOD_SC2:MOD_SC2 + 1, :]) + mod_ref[MOD_SH2:MOD_SH2 + 1, :]
    _store_packed(h2_ref, _pack_rows(h2), ROW_TILE)


def _outproj(o_a, o_b, o_c, w_bf, x, x_off, mod, ln_g, ln_b, total_rows, out_off, prev):
    rows = o_a.shape[0]
    d = D_MODEL
    in_specs = [pl.BlockSpec((ROW_TILE, A_Q_W), lambda i: (i, 0)),
                pl.BlockSpec((ROW_TILE, B_W), lambda i: (i, 0)),
                pl.BlockSpec((ROW_TILE, C_W), lambda i: (i, 0)),
                pl.BlockSpec((d, d), lambda i: (0, 0), pipeline_mode=pl.Buffered(1)),
                pl.BlockSpec((ROW_TILE, d), lambda i: (i + x_off, 0)),
                pl.BlockSpec((8, d), lambda i: (0, 0)),
                pl.BlockSpec((1, d), lambda i: (0, 0)),
                pl.BlockSpec((1, d), lambda i: (0, 0))]
    args = [o_a, o_b, o_c, w_bf, x, mod, ln_g, ln_b]
    aliases = {}
    if prev is not None:
        in_specs += [pl.BlockSpec(memory_space=pl.ANY), pl.BlockSpec(memory_space=pl.ANY)]
        aliases = {len(args): 0, len(args) + 1: 1}
        args += list(prev)
    return pl.pallas_call(
        _outproj_kernel,
        grid=(rows // ROW_TILE,),
        in_specs=in_specs,
        out_specs=[pl.BlockSpec((ROW_TILE, d), lambda i: (i + out_off, 0)),
                   pl.BlockSpec((ROW_TILE * PACK_SUB, LANES), lambda i: (i + out_off, 0))],
        out_shape=[jax.ShapeDtypeStruct((total_rows, d), F32),
                   jax.ShapeDtypeStruct((total_rows * PACK_SUB, LANES), jnp.int32)],
        input_output_aliases=aliases,
        compiler_params=_params("arbitrary"),
        name="outproj",
    )(*args)


def _top2_sublanes(vals, sub):
    m1 = vals.max(axis=0, keepdims=True)
    i1 = jnp.where(vals == m1, sub, vals.shape[0]).min(axis=0, keepdims=True)
    rest = jnp.where(sub == i1, -jnp.inf, vals)
    m2 = rest.max(axis=0, keepdims=True)
    i2 = jnp.where(rest == m2, sub, vals.shape[0]).min(axis=0, keepdims=True)
    return m1, i1, m2, i2


def _router_kernel(h_ref, wr_ref, rb_ref, e_ref, w_ref, rank_ref, cnt_ref, run_ref):
    i = pl.program_id(0)
    tm = ROW_TILE
    epg = EXPERTS_PER_GROUP

    @pl.when(i == 0)
    def _():
        run_ref[...] = jnp.zeros_like(run_ref)

    parts = _dot_nt(wr_ref[...], _load_packed(h_ref, tm).astype(BF16))
    logits = parts[0:N_EXPERTS] + parts[N_EXPERTS:2 * N_EXPERTS] + parts[2 * N_EXPERTS:]
    scores = _sigmoid(logits)
    biased = scores + jnp.concatenate([rb_ref[...]] * (tm // LANES), axis=1)
    sub = lax.broadcasted_iota(jnp.int32, (epg, tm), 0)

    best = None
    for g in range(N_EXPERT_GROUPS):
        m1, _, m2, _ = _top2_sublanes(biased[g * epg:(g + 1) * epg], sub)
        gs = m1 + m2
        if best is None:
            best, grp = gs, jnp.zeros((1, tm), jnp.int32)
            bsel, ssel = biased[0:epg], scores[0:epg]
        else:
            better = gs > best
            best = jnp.where(better, gs, best)
            grp = jnp.where(better, g, grp)
            bsel = jnp.where(better, biased[g * epg:(g + 1) * epg], bsel)
            ssel = jnp.where(better, scores[g * epg:(g + 1) * epg], ssel)
    _, i1, _, i2 = _top2_sublanes(bsel, sub)
    w1 = jnp.where(sub == i1, ssel, 0.0).sum(axis=0, keepdims=True)
    w2 = jnp.where(sub == i2, ssel, 0.0).sum(axis=0, keepdims=True)
    tot = w1 + w2
    e1 = grp * epg + i1
    e2 = grp * epg + i2

    eiota = lax.broadcasted_iota(jnp.int32, (N_EXPERTS, tm), 0)
    oh1 = (eiota == e1).astype(F32)
    oh2 = (eiota == e2).astype(F32)
    ohb = oh1 + oh2
    before = (lax.broadcasted_iota(jnp.int32, (tm, tm), 0) < lax.broadcasted_iota(jnp.int32, (tm, tm), 1))
    prefix = jnp.dot(ohb.astype(BF16), before.astype(BF16), preferred_element_type=F32)
    pos = run_ref[...] + prefix
    r1 = (oh1 * pos).sum(axis=0, keepdims=True)
    r2 = (oh2 * pos).sum(axis=0, keepdims=True)
    run_ref[...] = run_ref[...] + ohb.sum(axis=1, keepdims=True)

    e_ref[...] = jnp.concatenate([e1, e2], axis=0)
    w_ref[...] = jnp.concatenate([w1 / tot, w2 / tot], axis=0)
    rank_ref[...] = jnp.concatenate([r1, r2], axis=0).astype(jnp.int32)
    cnt_ref[...] = run_ref[:, 0:LANES]


def _router(h2p, wr_t, rb_b):
    t = h2p.shape[0] // PACK_SUB
    d = D_MODEL
    row2 = pl.BlockSpec((TOP_K, ROW_TILE), lambda i: (0, i))
    return pl.pallas_call(
        _router_kernel,
        grid=(t // ROW_TILE,),
        in_specs=[pl.BlockSpec((ROW_TILE * PACK_SUB, LANES), lambda i: (i, 0)),
                  pl.BlockSpec((3 * N_EXPERTS, d), lambda i: (0, 0)),
                  pl.BlockSpec((N_EXPERTS, LANES), lambda i: (0, 0))],
        out_specs=[row2, row2, row2, pl.BlockSpec((N_EXPERTS, LANES), lambda i: (0, 0))],
        out_shape=[jax.ShapeDtypeStruct((TOP_K, t), jnp.int32),
                   jax.ShapeDtypeStruct((TOP_K, t), F32),
                   jax.ShapeDtypeStruct((TOP_K, t), jnp.int32),
                   jax.ShapeDtypeStruct((N_EXPERTS, LANES), F32)],
        scratch_shapes=[pltpu.VMEM((N_EXPERTS, ROW_TILE), F32)],
        compiler_params=_params("arbitrary"),
        name="router",
    )(h2p, wr_t, rb_b)


def _row_copy(src_hbm, first_sublane, dst, sem):
    return pltpu.make_async_copy(src_hbm.at[pl.ds(pl.multiple_of(first_sublane, PACK_SUB), PACK_SUB), :], dst, sem)


DISPATCH_SLOTS = 3


def _dispatch_kernel(slot_ref, h_hbm, xs_hbm, hbuf, lsem, ssem, *, n_tok):
    i = pl.program_id(0)
    nt = pl.num_programs(0)
    tm = ROW_TILE
    tsub = tm * PACK_SUB

    def load(blk):
        row0 = pl.multiple_of(blk * tsub, tsub)
        return pltpu.make_async_copy(h_hbm.at[pl.ds(row0, tsub), :], hbuf.at[blk % DISPATCH_SLOTS],
                                     lsem.at[blk % DISPATCH_SLOTS])

    def wait_rows(par):
        for _ in range(TOP_K):
            pltpu.make_async_copy(hbuf.at[0], xs_hbm.at[pl.ds(0, tsub), :], ssem.at[par]).wait()

    @pl.when(i == 0)
    def _():
        load(0).start()

    @pl.when(i + 1 < nt)
    def _():
        load(i + 1).start()

    load(i).wait()
    par = i % 2
    cur = hbuf.at[i % DISPATCH_SLOTS]
    for r in range(tm):
        src = cur.at[pl.ds(r * PACK_SUB, PACK_SUB), :]
        for k in range(TOP_K):
            dst_row = pl.multiple_of(slot_ref[k * n_tok + i * tm + r], PACK_SUB)
            pltpu.make_async_copy(src, xs_hbm.at[pl.ds(dst_row, PACK_SUB), :], ssem.at[par]).start(priority=k)

    @pl.when(i > 0)
    def _():
        wait_rows(1 - par)

    @pl.when(i == nt - 1)
    def _():
        wait_rows(par)


def _dispatch(h2p, slots_sub, rows):
    t = h2p.shape[0] // PACK_SUB
    return pl.pallas_call(
        functools.partial(_dispatch_kernel, n_tok=t),
        grid_spec=pltpu.PrefetchScalarGridSpec(
            num_scalar_prefetch=1,
            grid=(t // ROW_TILE,),
            in_specs=[pl.BlockSpec(memory_space=pl.ANY)],
            out_specs=pl.BlockSpec(memory_space=pl.ANY),
            scratch_shapes=[pltpu.VMEM((DISPATCH_SLOTS, ROW_TILE * PACK_SUB, LANES), jnp.int32),
                            pltpu.SemaphoreType.DMA((DISPATCH_SLOTS,)),
                            pltpu.SemaphoreType.DMA((2,))]),
        out_shape=jax.ShapeDtypeStruct((rows * PACK_SUB, LANES), jnp.int32),
        compiler_params=_params("arbitrary"),
        name="dispatch",
    )(slots_sub, h2p)


def _experts_kernel(ts_ref, nu_ref, h_hbm, wg_hbm, wu_hbm, wd_hbm, ys_hbm,
                    xbuf, ybuf, gsem, ysem, wg_f, wu_f, wd_f, wsem, wg_bf, wu_bf, wd_bf, *, layer):
    e = pl.program_id(0)
    ne = pl.num_programs(0)
    n_used = nu_ref[0]
    tm = MOE_TILE
    tsub = tm * PACK_SUB

    def has_rows(ex):
        return ts_ref[ex + 1] > ts_ref[ex]

    def weight_copies(ex, wslot):
        return [pltpu.make_async_copy(src.at[layer, ex], dst.at[wslot], wsem.at[wslot])
                for src, dst in ((wg_hbm, wg_f), (wu_hbm, wu_f), (wd_hbm, wd_f))]

    def start_weights(ex, wslot):
        for cp in weight_copies(ex, wslot):
            cp.start(priority=1)

    def x_load(tile, slot):
        row0 = pl.multiple_of(tile * tsub, tsub)
        return pltpu.make_async_copy(h_hbm.at[pl.ds(row0, tsub), :], xbuf.at[slot], gsem.at[slot])

    def gather(tile, slot):
        x_load(tile, slot).start()

    def wait_gather(slot):
        x_load(0, slot).wait()

    def y_store(tile, slot):
        row0 = pl.multiple_of(tile * tsub, tsub)
        return pltpu.make_async_copy(ybuf.at[slot], ys_hbm.at[pl.ds(row0, tsub), :], ysem.at[slot])

    wslot = e % 2

    @pl.when((e == 0) & has_rows(0))
    def _():
        start_weights(0, 0)

    @pl.when((e == 0) & (n_used > 0))
    def _():
        gather(0, 0)

    nxt = jnp.minimum(e + 1, ne - 1)

    @pl.when((e + 1 < ne) & has_rows(nxt))
    def _():
        start_weights(nxt, 1 - wslot)

    @pl.when(has_rows(e))
    def _():
        for cp in weight_copies(e, wslot):
            cp.wait()
        wg_bf[...] = wg_f[wslot].astype(BF16)
        wu_bf[...] = wu_f[wslot].astype(BF16)
        wd_bf[...] = wd_f[wslot].astype(BF16)

    def tile_body(g, carry):
        slot = g % 2

        @pl.when(g >= 2)
        def _():
            y_store(g - 2, slot).wait()

        wait_gather(slot)
        x = _load_packed(xbuf.at[slot], tm).astype(BF16)
        gather(jnp.minimum(g + 1, n_used - 1), 1 - slot)
        gate = jnp.dot(x, wg_bf[...], preferred_element_type=F32)
        up = jnp.dot(x, wu_bf[...], preferred_element_type=F32)
        hid = (gate * _sigmoid(gate)) * up
        y = jnp.dot(hid.astype(BF16), wd_bf[...], preferred_element_type=F32)
        _store_packed(ybuf.at[slot], _pack_rows(y), tm)
        y_store(g, slot).start(priority=1)
        return carry

    lax.fori_loop(ts_ref[e], ts_ref[e + 1], tile_body, 0)

    @pl.when((e == pl.num_programs(0) - 1) & (n_used > 0))
    def _():
        wait_gather(n_used % 2)
        y_store(n_used - 1, (n_used - 1) % 2).wait()

        @pl.when(n_used >= 2)
        def _():
            y_store(n_used - 2, n_used % 2).wait()


def _experts(xs, tile_start, n_used, w_gate, w_up, w_down, layer):
    d = D_MODEL
    tm = MOE_TILE
    rows = xs.shape[0] // PACK_SUB

    hbm = pl.BlockSpec(memory_space=pl.ANY)
    return pl.pallas_call(
        functools.partial(_experts_kernel, layer=layer),
        grid_spec=pltpu.PrefetchScalarGridSpec(
            num_scalar_prefetch=2,
            grid=(N_EXPERTS,),
            in_specs=[hbm, hbm, hbm, hbm],
            out_specs=hbm,
            scratch_shapes=[pltpu.VMEM((2, tm * PACK_SUB, LANES), jnp.int32),
                            pltpu.VMEM((2, tm * PACK_SUB, LANES), jnp.int32),
                            pltpu.SemaphoreType.DMA((2,)),
                            pltpu.SemaphoreType.DMA((2,)),
                            pltpu.VMEM((2, d, EXPERT_FF), F32),
                            pltpu.VMEM((2, d, EXPERT_FF), F32),
                            pltpu.VMEM((2, EXPERT_FF, d), F32),
                            pltpu.SemaphoreType.DMA((2,)),
                            pltpu.VMEM((d, EXPERT_FF), BF16),
                            pltpu.VMEM((d, EXPERT_FF), BF16),
                            pltpu.VMEM((EXPERT_FF, d), BF16)]),
        out_shape=jax.ShapeDtypeStruct((rows * PACK_SUB, LANES), jnp.int32),
        compiler_params=_params("arbitrary"),
        name="experts",
    )(tile_start, n_used, xs, w_gate, w_up, w_down)


def _combine_kernel(slot_ref, ys_hbm, w_ref, x_ref, mod_ref, g_ref, b_ref, o_ref, ybuf, sem, *, n_tok):
    i = pl.program_id(0)
    nt = pl.num_programs(0)
    tm = ROW_TILE

    def issue(blk, buf):
        for r in range(tm):
            for k in range(TOP_K):
                row = slot_ref[k * n_tok + blk * tm + r]
                _row_copy(ys_hbm, row, ybuf.at[buf, k, pl.ds(r * PACK_SUB, PACK_SUB), :],
                          sem.at[buf]).start(priority=k)

    def wait(buf):
        for k in range(TOP_K):
            pltpu.make_async_copy(ys_hbm.at[pl.ds(0, tm * PACK_SUB), :], ybuf.at[buf, k], sem.at[buf]).wait()

    @pl.when(i == 0)
    def _():
        issue(0, 0)

    buf = i % 2
    wait(buf)
    issue(jnp.minimum(i + 1, nt - 1), 1 - buf)
    y = (w_ref[:, 0:1] * _load_packed(ybuf.at[buf, 0], tm)
         + w_ref[:, 1:2] * _load_packed(ybuf.at[buf, 1], tm))
    t = DEEPNORM_ALPHA * x_ref[...] + mod_ref[MOD_G2:MOD_G2 + 1, :] * y
    o_ref[...] = _layer_norm(t, g_ref[...], b_ref[...])

    @pl.when(i == nt - 1)
    def _():
        wait(1 - buf)


def _combine(ys, slots, w_tok, x1, mods, n_ctx_blocks, ln_g, ln_b):
    t = x1.shape[0]
    d = D_MODEL
    return pl.pallas_call(
        functools.partial(_combine_kernel, n_tok=t),
        grid_spec=pltpu.PrefetchScalarGridSpec(
            num_scalar_prefetch=1,
            grid=(t // ROW_TILE,),
            in_specs=[pl.BlockSpec(memory_space=pl.ANY),
                      pl.BlockSpec((ROW_TILE, TOP_K), lambda i, s: (i, 0)),
                      pl.BlockSpec((ROW_TILE, d), lambda i, s: (i, 0)),
                      pl.BlockSpec((None, 8, d), lambda i, s: (jnp.where(i < n_ctx_blocks, 0, 1), 0, 0)),
                      pl.BlockSpec((1, d), lambda i, s: (0, 0)),
                      pl.BlockSpec((1, d), lambda i, s: (0, 0))],
            out_specs=pl.BlockSpec((ROW_TILE, d), lambda i, s: (i, 0)),
            scratch_shapes=[pltpu.VMEM((2, TOP_K, ROW_TILE * PACK_SUB, LANES), jnp.int32),
                            pltpu.SemaphoreType.DMA((2,))]),
        out_shape=jax.ShapeDtypeStruct((t, d), F32),
        compiler_params=_params("arbitrary"),
        name="combine",
    )(slots, ys, w_tok, x1, mods, ln_g, ln_b)


def _moe(h2, x1, mods, n_ctx_blocks, wr_t, rb_b, w_gate, w_up, w_down, layer, ln_g, ln_b):
    t = x1.shape[0]
    tm = MOE_TILE
    e_idx, w_tok, rank, cnt = _router(h2, wr_t, rb_b)
    counts = cnt[:, 0].astype(jnp.int32)
    tiles_per = (counts + tm - 1) // tm
    tile_end = jnp.cumsum(tiles_per)
    n_used = tile_end[-1]
    row_off = (tile_end - tiles_per) * tm
    experts = jnp.arange(N_EXPERTS, dtype=jnp.int32)
    slots = jnp.sum(jnp.where(e_idx[:, :, None] == experts, row_off, 0), axis=-1) + rank
    max_tiles = (TOP_K * t + N_EXPERTS * (tm - 1)) // tm + 1
    tile_start = jnp.concatenate([tile_end - tiles_per, n_used.reshape(1)]).astype(jnp.int32)
    slots_sub = slots.reshape(-1) * PACK_SUB
    xs = _dispatch(h2, slots_sub, max_tiles * tm)
    ys = _experts(xs, tile_start, n_used.reshape(1).astype(jnp.int32), w_gate, w_up, w_down, layer)
    return _combine(ys, slots_sub, w_tok.T, x1, mods, n_ctx_blocks, ln_g, ln_b)


def _rope_tables(n):
    t = jnp.arange(n)
    row = (t // GRID_W).astype(F32)
    col = (t % GRID_W).astype(F32)
    n_freq = HEAD_DIM // 4
    inv_freq = ROPE_BASE ** (-jnp.arange(n_freq, dtype=F32) / n_freq)
    ang = jnp.concatenate([row[:, None] * inv_freq, col[:, None] * inv_freq], axis=-1)
    cos, sin = jnp.cos(ang), jnp.sin(ang)
    return jnp.concatenate([cos, cos], axis=-1), jnp.concatenate([-sin, sin], axis=-1)


def kernel(x, c, ctx, c_ctx, w_mod, b_mod, w_in, attn_sink, na_rpb, sgu_ln_g, sgu_ln_b, sgu_w, sgu_b,
           w_out, ln1_g, ln1_b, w_router, router_bias, w_gate, w_up, w_down, ln2_g, ln2_b):
    batch, n, d = x.shape
    lctx = ctx.shape[1]
    assert batch == 1 and d == D_MODEL and n % ROW_TILE == 0 and lctx % ROW_TILE == 0
    n_ctx_blocks = lctx // ROW_TILE

    mods = _modulation(c, c_ctx, w_mod, b_mod).reshape(DEPTH, 8, 6, d)
    mod_lat = jnp.pad(mods[:, 0], ((0, 0), (0, 2), (0, 0)))
    mod_ctx = jnp.pad(mods[:, 1], ((0, 0), (0, 2), (0, 0)))
    cos, sin = _rope_tables(n)
    wr_f = w_router.T
    wr_hi = lax.reduce_precision(wr_f, exponent_bits=8, mantissa_bits=7)
    wr_mid = lax.reduce_precision(wr_f - wr_hi, exponent_bits=8, mantissa_bits=7)
    wr_lo = wr_f - wr_hi - wr_mid
    wr_t = jnp.concatenate([wr_hi, wr_mid, wr_lo], axis=0).astype(BF16)
    rb_b = jnp.broadcast_to(router_bias.reshape(N_EXPERTS, 1), (N_EXPERTS, LANES))

    x_lat, lat_off = x[0], 0
    x_ctx = ctx[0]
    for l in range(DEPTH):
        last = l == DEPTH - 1
        w_in_bf = w_in[l].astype(BF16)
        w_out_bf = w_out[l].astype(BF16)
        sgu_w_bf = sgu_w[l].astype(BF16)
        sgu_b_b = jnp.broadcast_to(sgu_b[l][:, :, None], (C_GROUPS, CHUNK, LANES))
        ln_g_c, ln_b_c = sgu_ln_g[l].reshape(1, C_W), sgu_ln_b[l].reshape(1, C_W)
        bias_tab = _na_bias_table(na_rpb[l])
        g1, b1 = ln1_g[l].reshape(1, d), ln1_b[l].reshape(1, d)
        g2, b2 = ln2_g[l].reshape(1, d), ln2_b[l].reshape(1, d)

        p = _proj(x_lat, lat_off, n, mod_lat[l], cos, sin, w_in_bf, rope=True)
        pc = _proj(x_ctx, 0, lctx, mod_ctx[l], cos, sin, w_in_bf, rope=False)
        o_a = _attn_a(p, pc, attn_sink[l], latent=True)
        o_b = _attn_b(p, pc, bias_tab, latent=True)
        o_c = _sgu(p, ln_g_c, ln_b_c, sgu_w_bf, sgu_b_b)
        if last:
            x1, h2 = _outproj(o_a, o_b, o_c, w_out_bf, x_lat, lat_off, mod_lat[l], g1, b1, n, 0, None)
            x_lat = _moe(h2, x1, jnp.stack([mod_ctx[l], mod_lat[l]]), 0, wr_t, rb_b,
                         w_gate, w_up, w_down, l, g2, b2)
        else:
            oc_a = _attn_a(pc, pc, attn_sink[l], latent=False)
            oc_b = _attn_b(pc, pc, bias_tab, latent=False)
            oc_c = _sgu(pc, ln_g_c, ln_b_c, sgu_w_bf, sgu_b_b)
            total = lctx + n
            prev = _outproj(oc_a, oc_b, oc_c, w_out_bf, x_ctx, 0, mod_ctx[l], g1, b1, total, 0, None)
            x1, h2 = _outproj(o_a, o_b, o_c, w_out_bf, x_lat, lat_off, mod_lat[l], g1, b1, total,
                              n_ctx_blocks, prev)
            x_all = _moe(h2, x1, jnp.stack([mod_ctx[l], mod_lat[l]]), n_ctx_blocks, wr_t, rb_b,
                         w_gate, w_up, w_down, l, g2, b2)
            x_lat, lat_off, x_ctx = x_all, n_ctx_blocks, x_all
    return x_lat.reshape(batch, n, d)
```

```python
import functools

import numpy as np
import jax
import jax.numpy as jnp
from jax import lax
from jax.experimental import pallas as pl
from jax.experimental.pallas import tpu as pltpu

F32 = jnp.float32
BF16 = jnp.bfloat16

D_MODEL = 2048
DEPTH = 2
GRID_W = 64
HEAD_DIM = 128
A_HEADS = 6
A_KV_HEADS = 2
A_GROUP = A_HEADS // A_KV_HEADS
A_BLOCK = 128
B_HEADS = 6
NA_ROWS = 8
NA_COLS = 16
C_GROUPS = 4
C_W = C_GROUPS * HEAD_DIM
CHUNK = 128
N_EXPERTS = 32
N_EXPERT_GROUPS = 4
EXPERTS_PER_GROUP = N_EXPERTS // N_EXPERT_GROUPS
TOP_K = 2
EXPERT_FF = 512
ROPE_BASE = 10000.0
LN_EPS = 1e-5
NEG_INF = -1e30
DEEPNORM_ALPHA = (2 * DEPTH) ** 0.25
ATTN_SCALE = HEAD_DIM ** -0.5

A_Q_W = A_HEADS * HEAD_DIM
A_KV_W = A_KV_HEADS * HEAD_DIM
B_W = B_HEADS * HEAD_DIM
OFF_AK = A_Q_W
OFF_AV = OFF_AK + A_KV_W
OFF_BQ = OFF_AV + A_KV_W
OFF_BK = OFF_BQ + B_W
OFF_BV = OFF_BK + B_W
OFF_C = OFF_BV + B_W
IN_COLS = OFF_C + 2 * C_W

VMEM_LIMIT_BYTES = 56 * 1024 * 1024
LANES = 128

ROW_TILE = 256
PROJ_COL_TILE = 512
MOD_COL_TILE = 1024
MOE_TILE = 256

MOD_SH1, MOD_SC1, MOD_G1, MOD_SH2, MOD_SC2, MOD_G2 = range(6)


def _params(*sem):
    return pltpu.CompilerParams(dimension_semantics=sem, vmem_limit_bytes=VMEM_LIMIT_BYTES)


def _layer_norm(t, g, b):
    mu = jnp.mean(t, axis=-1, keepdims=True)
    d = t - mu
    var = jnp.mean(d * d, axis=-1, keepdims=True)
    return d * lax.rsqrt(var + LN_EPS) * g + b


def _sigmoid(v):
    return 1.0 / (1.0 + jnp.exp(-v))


def _dot_nt(a, b):
    return lax.dot_general(a, b, (((1,), (1,)), ((), ())), preferred_element_type=F32)


def _mod_kernel(c_ref, w_ref, b_ref, o_ref):
    w = w_ref[...]
    reps = w.shape[1] // LANES
    rows = []
    for r in range(2):
        cv = c_ref[r]
        s = cv * _sigmoid(cv)
        sb = jnp.concatenate([s] * reps, axis=1)
        rows.append(jnp.sum(w * sb, axis=0, keepdims=True) + b_ref[...])
    rows.append(jnp.zeros((6, w.shape[1]), F32))
    o_ref[...] = jnp.concatenate(rows, axis=0)


def _modulation(c, c_ctx, w_mod, b_mod):
    d = D_MODEL
    cb = jnp.stack([jnp.broadcast_to(c.reshape(d, 1), (d, LANES)),
                    jnp.broadcast_to(c_ctx.reshape(d, 1), (d, LANES))])
    n_out = 6 * d
    return pl.pallas_call(
        _mod_kernel,
        grid=(DEPTH, n_out // MOD_COL_TILE),
        in_specs=[pl.BlockSpec((2, d, LANES), lambda l, j: (0, 0, 0)),
                  pl.BlockSpec((None, d, MOD_COL_TILE), lambda l, j: (l, 0, j)),
                  pl.BlockSpec((None, 1, MOD_COL_TILE), lambda l, j: (l, 0, j))],
        out_specs=pl.BlockSpec((None, 8, MOD_COL_TILE), lambda l, j: (l, 0, j)),
        out_shape=jax.ShapeDtypeStruct((DEPTH, 8, n_out), F32),
        compiler_params=_params("arbitrary", "arbitrary"),
        name="modulation",
    )(cb, w_mod, b_mod.reshape(DEPTH, 1, n_out))


def _gelu_tanh(v):
    return 0.5 * v * (1.0 + jnp.tanh(np.sqrt(2.0 / np.pi).astype(np.float32) * (v + 0.044715 * (v * v * v))))


def _proj_kernel(x_ref, mod_ref, cos_ref, sin_ref, w_ref, o_ref, *, rope):
    x = x_ref[...]
    h = (x * (1.0 + mod_ref[MOD_SC1:MOD_SC1 + 1, :]) + mod_ref[MOD_SH1:MOD_SH1 + 1, :]).astype(BF16)
    tn = PROJ_COL_TILE
    for j in range(IN_COLS // tn):
        c0 = j * tn
        acc = jnp.dot(h, w_ref[:, c0:c0 + tn], preferred_element_type=F32)
        if c0 < OFF_AV:
            if rope:
                cos = cos_ref[...]
                sin = sin_ref[...]
                parts = []
                for hh in range(tn // HEAD_DIM):
                    a = acc[:, hh * HEAD_DIM:(hh + 1) * HEAD_DIM]
                    parts.append(a * cos + pltpu.roll(a, HEAD_DIM // 2, 1) * sin)
                acc = jnp.concatenate(parts, axis=1)
        elif c0 >= OFF_C:
            acc = _gelu_tanh(acc)
        o_ref[:, c0:c0 + tn] = acc.astype(BF16)


def _proj(x, x_off, rows, mod, cos, sin, w_bf, rope):
    d = D_MODEL
    return pl.pallas_call(
        functools.partial(_proj_kernel, rope=rope),
        grid=(rows // ROW_TILE,),
        in_specs=[pl.BlockSpec((ROW_TILE, d), lambda i: (i + x_off, 0)),
                  pl.BlockSpec((8, d), lambda i: (0, 0)),
                  pl.BlockSpec((ROW_TILE, HEAD_DIM), lambda i: (i, 0)),
                  pl.BlockSpec((ROW_TILE, HEAD_DIM), lambda i: (i, 0)),
                  pl.BlockSpec((d, IN_COLS), lambda i: (0, 0), pipeline_mode=pl.Buffered(1))],
        out_specs=pl.BlockSpec((ROW_TILE, IN_COLS), lambda i: (i, 0)),
        out_shape=jax.ShapeDtypeStruct((rows, IN_COLS), BF16),
        compiler_params=_params("arbitrary"),
        name="proj_rope" if rope else "proj_ctx",
    )(x, mod, cos, sin, w_bf)


def _softmax_pv(s_parts, v_parts, sink):
    m = s_parts[0].max(axis=-1, keepdims=True)
    for s in s_parts[1:]:
        m = jnp.maximum(m, s.max(axis=-1, keepdims=True))
    if sink is not None:
        m = jnp.maximum(m, sink)
    denom = None if sink is None else jnp.exp(sink - m)
    out = None
    for s, v in zip(s_parts, v_parts):
        e = jnp.exp(s - m)
        es = e.sum(axis=-1, keepdims=True)
        denom = es if denom is None else denom + es
        pv = jnp.dot(e.astype(BF16), v, preferred_element_type=F32)
        out = pv if out is None else out + pv
    return out / denom


def _attn_a_kernel(sink_ref, q_ref, kp_ref, kc_ref, kn_ref, vp_ref, vc_ref, vn_ref, kx_ref, vx_ref, o_ref,
                   mask_ref, *, latent):
    i = pl.program_id(0)
    nb = pl.num_programs(0)
    nq = A_GROUP * A_BLOCK

    if latent:
        @pl.when(i == 0)
        def _():
            qi = lax.broadcasted_iota(jnp.int32, (nq, 3 * A_BLOCK), 0) % A_BLOCK
            jj = lax.broadcasted_iota(jnp.int32, (nq, 3 * A_BLOCK), 1)
            ok = (jj >= qi) & (jj <= qi + 2 * A_BLOCK)
            mask_ref[...] = jnp.where(ok, 0.0, NEG_INF).astype(F32)

        col = lax.broadcasted_iota(jnp.int32, (nq, 3 * A_BLOCK), 1)
        off_band = ((col < A_BLOCK) & (i == 0)) | ((col >= 2 * A_BLOCK) & (i == nb - 1))

    for kh in range(A_KV_HEADS):
        hs = [kh * A_GROUP + g for g in range(A_GROUP)]
        q = jnp.concatenate([q_ref[:, h * HEAD_DIM:(h + 1) * HEAD_DIM] for h in hs], axis=0)
        sink = jnp.concatenate([jnp.full((A_BLOCK, 1), sink_ref[h], F32) for h in hs], axis=0)
        ks = slice(kh * HEAD_DIM, (kh + 1) * HEAD_DIM)
        s_parts = [_dot_nt(q, kx_ref[:, ks]) * ATTN_SCALE]
        v_parts = [vx_ref[:, ks]]
        if latent:
            kband = jnp.concatenate([kp_ref[:, ks], kc_ref[:, ks], kn_ref[:, ks]], axis=0)
            vband = jnp.concatenate([vp_ref[:, ks], vc_ref[:, ks], vn_ref[:, ks]], axis=0)
            s_loc = _dot_nt(q, kband) * ATTN_SCALE + mask_ref[...]
            s_parts.append(jnp.where(off_band, NEG_INF, s_loc))
            v_parts.append(vband)
        out = _softmax_pv(s_parts, v_parts, sink)
        for g, h in enumerate(hs):
            o_ref[:, h * HEAD_DIM:(h + 1) * HEAD_DIM] = out[g * A_BLOCK:(g + 1) * A_BLOCK].astype(BF16)


def _attn_a(p, pc, sink, latent):
    rows = p.shape[0]
    nb = rows // A_BLOCK
    kcol = OFF_AK // A_KV_W
    vcol = OFF_AV // A_KV_W

    def band(col, shift):
        return pl.BlockSpec((A_BLOCK, A_KV_W), lambda i, s: (jnp.clip(i + shift, 0, nb - 1), col))

    lctx = pc.shape[0]
    return pl.pallas_call(
        functools.partial(_attn_a_kernel, latent=latent),
        grid_spec=pltpu.PrefetchScalarGridSpec(
            num_scalar_prefetch=1,
            grid=(nb,),
            in_specs=[pl.BlockSpec((A_BLOCK, A_Q_W), lambda i, s: (i, 0)),
                      band(kcol, -1), band(kcol, 0), band(kcol, 1),
                      band(vcol, -1), band(vcol, 0), band(vcol, 1),
                      pl.BlockSpec((lctx, A_KV_W), lambda i, s: (0, kcol)),
                      pl.BlockSpec((lctx, A_KV_W), lambda i, s: (0, vcol))],
            out_specs=pl.BlockSpec((A_BLOCK, A_Q_W), lambda i, s: (i, 0)),
            scratch_shapes=[pltpu.VMEM((A_GROUP * A_BLOCK, 3 * A_BLOCK), F32)]),
        out_shape=jax.ShapeDtypeStruct((rows, A_Q_W), BF16),
        compiler_params=_params("arbitrary"),
        name="attn_a_latent" if latent else "attn_a_ctx",
    )(sink, p, p, p, p, p, p, p, pc, pc)


B_PAIR_W = 2 * HEAD_DIM
NA_GROUP_ROWS = ROW_TILE // GRID_W
NA_WIN_ROWS = NA_ROWS + NA_GROUP_ROWS
NA_PAIRS = NA_WIN_ROWS // 2
NA_BIAS_OFFS = 2 * NA_ROWS
NA_BOTH, NA_LEFT, NA_RIGHT = range(3)


def _attn_b_kernel(q_ref, k_ref, v_ref, kx_ref, vx_ref, bias_ref, o_ref, *, latent, grid_rows):
    g = pl.program_id(1)
    if latent:
        r_base = g * NA_GROUP_ROWS
        w0 = jnp.clip(r_base - NA_ROWS // 2, 0, grid_rows - NA_WIN_ROWS)
        start = pl.multiple_of(w0 * GRID_W, GRID_W)
    for hh in range(2):
        hs = slice(hh * HEAD_DIM, (hh + 1) * HEAD_DIM)
        q = q_ref[:, hs]
        s_parts = [_dot_nt(q, kx_ref[:, hs]) * ATTN_SCALE]
        v_parts = [vx_ref[:, hs]]
        if latent:
            kwin = k_ref[pl.ds(start, NA_WIN_ROWS * GRID_W), hs]
            vwin = v_ref[pl.ds(start, NA_WIN_ROWS * GRID_W), hs]
            bias_rows = []
            for rr in range(NA_GROUP_ROWS):
                r = r_base + rr
                r0 = jnp.clip(r - NA_ROWS // 2, 0, grid_rows - NA_ROWS)
                tiles = []
                for jp in range(NA_PAIRS):
                    ka = w0 + 2 * jp
                    in_a = (ka >= r0) & (ka < r0 + NA_ROWS)
                    in_b = (ka + 1 >= r0) & (ka + 1 < r0 + NA_ROWS)
                    variant = jnp.where(in_a, jnp.where(in_b, NA_BOTH, NA_LEFT), jnp.where(in_b, NA_RIGHT, NA_LEFT))
                    off = jnp.where(in_a | in_b, jnp.clip(ka - r + NA_ROWS, 0, NA_BIAS_OFFS - 1), 0)
                    tiles.append(bias_ref[hh, variant, off])
                bias_rows.append(jnp.concatenate(tiles, axis=1))
            bias = jnp.concatenate(bias_rows, axis=0)
            s_parts.append(_dot_nt(q, kwin) * ATTN_SCALE + bias)
            v_parts.append(vwin)
        out = _softmax_pv(s_parts, v_parts, None)
        o_ref[:, hs] = out.astype(BF16)


def _na_bias_table(rpb):
    cols = np.arange(GRID_W)
    c0 = np.clip(cols - NA_COLS // 2, 0, GRID_W - NA_COLS)
    rel = cols[None, :] - cols[:, None] + NA_COLS - 1
    ok = (cols[None, :] >= c0[:, None]) & (cols[None, :] < c0[:, None] + NA_COLS)
    onehot = (rel[None] == np.arange(2 * NA_COLS - 1)[:, None, None]).astype(np.float32)
    t = jnp.einsum("hrd,dqk->hrqk", rpb, onehot, precision=lax.Precision.HIGHEST)
    t = jnp.where(ok[None, None], t, NEG_INF).astype(F32)
    t = jnp.pad(t, ((0, 0), (1, 1), (0, 0), (0, 0)), constant_values=NEG_INF)
    first, second = t[:, :-1], t[:, 1:]
    masked = jnp.full_like(first, NEG_INF)
    return jnp.stack([jnp.concatenate([first, second], axis=-1),
                      jnp.concatenate([first, masked], axis=-1),
                      jnp.concatenate([masked, second], axis=-1)], axis=1)


def _attn_b(p, pc, bias_tab, latent):
    rows = p.shape[0]
    lctx = pc.shape[0]
    assert not latent or rows // GRID_W >= NA_WIN_ROWS
    qcol = OFF_BQ // B_PAIR_W
    kcol = OFF_BK // B_PAIR_W
    vcol = OFF_BV // B_PAIR_W
    return pl.pallas_call(
        functools.partial(_attn_b_kernel, latent=latent, grid_rows=rows // GRID_W),
        grid=(B_HEADS // 2, rows // ROW_TILE),
        in_specs=[pl.BlockSpec((ROW_TILE, B_PAIR_W), lambda hp, g: (g, qcol + hp)),
                  pl.BlockSpec((rows, B_PAIR_W), lambda hp, g: (0, kcol + hp)),
                  pl.BlockSpec((rows, B_PAIR_W), lambda hp, g: (0, vcol + hp)),
                  pl.BlockSpec((lctx, B_PAIR_W), lambda hp, g: (0, kcol + hp)),
                  pl.BlockSpec((lctx, B_PAIR_W), lambda hp, g: (0, vcol + hp)),
                  pl.BlockSpec((2, 3, NA_BIAS_OFFS, GRID_W, 2 * GRID_W), lambda hp, g: (hp, 0, 0, 0, 0))],
        out_specs=pl.BlockSpec((ROW_TILE, B_PAIR_W), lambda hp, g: (g, hp)),
        out_shape=jax.ShapeDtypeStruct((rows, B_W), BF16),
        compiler_params=_params("arbitrary", "arbitrary"),
        name="attn_b_latent" if latent else "attn_b_ctx",
    )(p, p, p, pc, pc, bias_tab)


def _sgu_kernel(u_ref, v_ref, g_ref, b_ref, w_ref, bs_ref, o_ref):
    for ch in range(ROW_TILE // CHUNK):
        rs = slice(ch * CHUNK, (ch + 1) * CHUNK)
        for grp in range(C_GROUPS):
            cs = slice(grp * HEAD_DIM, (grp + 1) * HEAD_DIM)
            vn = _layer_norm(v_ref[rs, cs].astype(F32), g_ref[:, cs], b_ref[:, cs])
            mixed = jnp.dot(w_ref[grp], vn.astype(BF16), preferred_element_type=F32) + bs_ref[grp]
            o_ref[rs, cs] = (u_ref[rs, cs].astype(F32) * mixed).astype(BF16)


def _sgu(p, ln_g, ln_b, w_bf, bs_b):
    rows = p.shape[0]
    ucol = OFF_C // C_W
    return pl.pallas_call(
        _sgu_kernel,
        grid=(rows // ROW_TILE,),
        in_specs=[pl.BlockSpec((ROW_TILE, C_W), lambda i: (i, ucol)),
                  pl.BlockSpec((ROW_TILE, C_W), lambda i: (i, ucol + 1)),
                  pl.BlockSpec((1, C_W), lambda i: (0, 0)),
                  pl.BlockSpec((1, C_W), lambda i: (0, 0)),
                  pl.BlockSpec((C_GROUPS, CHUNK, CHUNK), lambda i: (0, 0, 0)),
                  pl.BlockSpec((C_GROUPS, CHUNK, LANES), lambda i: (0, 0, 0))],
        out_specs=pl.BlockSpec((ROW_TILE, C_W), lambda i: (i, 0)),
        out_shape=jax.ShapeDtypeStruct((rows, C_W), BF16),
        compiler_params=_params("arbitrary"),
        name="sgu",
    )(p, p, ln_g, ln_b, w_bf, bs_b)


def _outproj_kernel(oa_ref, ob_ref, oc_ref, w_ref, x_ref, mod_ref, g_ref, b_ref, *rest):
    x1_ref, h2_ref = rest[-2], rest[-1]
    mix = jnp.dot(oa_ref[...], w_ref[0:A_Q_W, :], preferred_element_type=F32)
    mix += jnp.dot(ob_ref[...], w_ref[A_Q_W:A_Q_W + B_W, :], preferred_element_type=F32)
    mix += jnp.dot(oc_ref[...], w_ref[A_Q_W + B_W:, :], preferred_element_type=F32)
    t = DEEPNORM_ALPHA * x_ref[...] + mod_ref[MOD_G1:MOD_G1 + 1, :] * mix
    x1 = _layer_norm(t, g_ref[...], b_ref[...])
    x1_ref[...] = x1
    h2_ref[...] = x1 * (1.0 + mod_ref[MOD_SC2:MOD_SC2 + 1, :]) + mod_ref[MOD_SH2:MOD_SH2 + 1, :]


def _outproj(o_a, o_b, o_c, w_bf, x, x_off, mod, ln_g, ln_b, total_rows, out_off, prev):
    rows = o_a.shape[0]
    d = D_MODEL
    in_specs = [pl.BlockSpec((ROW_TILE, A_Q_W), lambda i: (i, 0)),
                pl.BlockSpec((ROW_TILE, B_W), lambda i: (i, 0)),
                pl.BlockSpec((ROW_TILE, C_W), lambda i: (i, 0)),
                pl.BlockSpec((d, d), lambda i: (0, 0), pipeline_mode=pl.Buffered(1)),
                pl.BlockSpec((ROW_TILE, d), lambda i: (i + x_off, 0)),
                pl.BlockSpec((8, d), lambda i: (0, 0)),
                pl.BlockSpec((1, d), lambda i: (0, 0)),
                pl.BlockSpec((1, d), lambda i: (0, 0))]
    args = [o_a, o_b, o_c, w_bf, x, mod, ln_g, ln_b]
    aliases = {}
    if prev is not None:
        in_specs += [pl.BlockSpec(memory_space=pl.ANY), pl.BlockSpec(memory_space=pl.ANY)]
        aliases = {len(args): 0, len(args) + 1: 1}
        args += list(prev)
    return pl.pallas_call(
        _outproj_kernel,
        grid=(rows // ROW_TILE,),
        in_specs=in_specs,
        out_specs=[pl.BlockSpec((ROW_TILE, d), lambda i: (i + out_off, 0))] * 2,
        out_shape=[jax.ShapeDtypeStruct((total_rows, d), F32)] * 2,
        input_output_aliases=aliases,
        compiler_params=_params("arbitrary"),
        name="outproj",
    )(*args)


def _top2_sublanes(vals, sub):
    m1 = vals.max(axis=0, keepdims=True)
    i1 = jnp.where(vals == m1, sub, vals.shape[0]).min(axis=0, keepdims=True)
    rest = jnp.where(sub == i1, -jnp.inf, vals)
    m2 = rest.max(axis=0, keepdims=True)
    i2 = jnp.where(rest == m2, sub, vals.shape[0]).min(axis=0, keepdims=True)
    return m1, i1, m2, i2


def _router_kernel(h_ref, wr_ref, rb_ref, e_ref, w_ref, rank_ref, cnt_ref, run_ref):
    i = pl.program_id(0)
    tm = ROW_TILE
    epg = EXPERTS_PER_GROUP

    @pl.when(i == 0)
    def _():
        run_ref[...] = jnp.zeros_like(run_ref)

    h = h_ref[...]
    h_hi = h.astype(BF16)
    h_mid = (h - h_hi.astype(F32)).astype(BF16)
    parts = _dot_nt(wr_ref[...], h_hi)
    parts_mid = _dot_nt(wr_ref[0:2 * N_EXPERTS, :], h_mid)
    logits = (parts[0:N_EXPERTS] + parts[N_EXPERTS:2 * N_EXPERTS] + parts[2 * N_EXPERTS:]
              + parts_mid[0:N_EXPERTS] + parts_mid[N_EXPERTS:])
    scores = _sigmoid(logits)
    biased = scores + jnp.concatenate([rb_ref[...]] * (tm // LANES), axis=1)
    sub = lax.broadcasted_iota(jnp.int32, (epg, tm), 0)

    best = None
    for g in range(N_EXPERT_GROUPS):
        m1, _, m2, _ = _top2_sublanes(biased[g * epg:(g + 1) * epg], sub)
        gs = m1 + m2
        if best is None:
            best, grp = gs, jnp.zeros((1, tm), jnp.int32)
            bsel, ssel = biased[0:epg], scores[0:epg]
        else:
            better = gs > best
            best = jnp.where(better, gs, best)
            grp = jnp.where(better, g, grp)
            bsel = jnp.where(better, biased[g * epg:(g + 1) * epg], bsel)
            ssel = jnp.where(better, scores[g * epg:(g + 1) * epg], ssel)
    _, i1, _, i2 = _top2_sublanes(bsel, sub)
    w1 = jnp.where(sub == i1, ssel, 0.0).sum(axis=0, keepdims=True)
    w2 = jnp.where(sub == i2, ssel, 0.0).sum(axis=0, keepdims=True)
    tot = w1 + w2
    e1 = grp * epg + i1
    e2 = grp * epg + i2

    eiota = lax.broadcasted_iota(jnp.int32, (N_EXPERTS, tm), 0)
    oh1 = (eiota == e1).astype(F32)
    oh2 = (eiota == e2).astype(F32)
    ohb = oh1 + oh2
    before = (lax.broadcasted_iota(jnp.int32, (tm, tm), 0) < lax.broadcasted_iota(jnp.int32, (tm, tm), 1))
    prefix = jnp.dot(ohb.astype(BF16), before.astype(BF16), preferred_element_type=F32)
    pos = run_ref[...] + prefix
    r1 = (oh1 * pos).sum(axis=0, keepdims=True)
    r2 = (oh2 * pos).sum(axis=0, keepdims=True)
    run_ref[...] = run_ref[...] + ohb.sum(axis=1, keepdims=True)

    e_ref[...] = jnp.concatenate([e1, e2], axis=0)
    w_ref[...] = jnp.concatenate([w1 / tot, w2 / tot], axis=0)
    rank_ref[...] = jnp.concatenate([r1, r2], axis=0).astype(jnp.int32)
    cnt_ref[...] = run_ref[:, 0:LANES]


def _router(h2, wr_t, rb_b):
    t = h2.shape[0]
    d = D_MODEL
    row2 = pl.BlockSpec((TOP_K, ROW_TILE), lambda i: (0, i))
    return pl.pallas_call(
        _router_kernel,
        grid=(t // ROW_TILE,),
        in_specs=[pl.BlockSpec((ROW_TILE, d), lambda i: (i, 0)),
                  pl.BlockSpec((3 * N_EXPERTS, d), lambda i: (0, 0)),
                  pl.BlockSpec((N_EXPERTS, LANES), lambda i: (0, 0))],
        out_specs=[row2, row2, row2, pl.BlockSpec((N_EXPERTS, LANES), lambda i: (0, 0))],
        out_shape=[jax.ShapeDtypeStruct((TOP_K, t), jnp.int32),
                   jax.ShapeDtypeStruct((TOP_K, t), F32),
                   jax.ShapeDtypeStruct((TOP_K, t), jnp.int32),
                   jax.ShapeDtypeStruct((N_EXPERTS, LANES), F32)],
        scratch_shapes=[pltpu.VMEM((N_EXPERTS, ROW_TILE), F32)],
        compiler_params=_params("arbitrary"),
        name="router",
    )(h2, wr_t, rb_b)


def _row_copy(src, row, dst, sem):
    return pltpu.make_async_copy(src.at[pl.ds(row, 1), :], dst, sem)


DISPATCH_SLOTS = 3


def _dispatch_kernel(slot_ref, h_hbm, xs_hbm, hbuf, lsem, ssem, *, n_tok):
    i = pl.program_id(0)
    nt = pl.num_programs(0)
    tm = ROW_TILE

    def load(blk):
        row0 = pl.multiple_of(blk * tm, tm)
        return pltpu.make_async_copy(h_hbm.at[pl.ds(row0, tm), :], hbuf.at[blk % DISPATCH_SLOTS],
                                     lsem.at[blk % DISPATCH_SLOTS])

    def wait_rows(par):
        for _ in range(TOP_K):
            pltpu.make_async_copy(hbuf.at[0], xs_hbm.at[pl.ds(0, tm), :], ssem.at[par]).wait()

    @pl.when(i == 0)
    def _():
        load(0).start()

    @pl.when(i + 1 < nt)
    def _():
        load(i + 1).start()

    load(i).wait()
    par = i % 2
    cur = hbuf.at[i % DISPATCH_SLOTS]
    for r in range(tm):
        for k in range(TOP_K):
            dst = xs_hbm.at[pl.ds(slot_ref[k * n_tok + i * tm + r], 1), :]
            pltpu.make_async_copy(cur.at[pl.ds(r, 1), :], dst, ssem.at[par]).start(priority=k)

    @pl.when(i > 0)
    def _():
        wait_rows(1 - par)

    @pl.when(i == nt - 1)
    def _():
        wait_rows(par)


def _dispatch(h2, slots, rows):
    t, d = h2.shape
    return pl.pallas_call(
        functools.partial(_dispatch_kernel, n_tok=t),
        grid_spec=pltpu.PrefetchScalarGridSpec(
            num_scalar_prefetch=1,
            grid=(t // ROW_TILE,),
            in_specs=[pl.BlockSpec(memory_space=pl.ANY)],
            out_specs=pl.BlockSpec(memory_space=pl.ANY),
            scratch_shapes=[pltpu.VMEM((DISPATCH_SLOTS, ROW_TILE, d), F32),
                            pltpu.SemaphoreType.DMA((DISPATCH_SLOTS,)),
                            pltpu.SemaphoreType.DMA((2,))]),
        out_shape=jax.ShapeDtypeStruct((rows, d), F32),
        compiler_params=_params("arbitrary"),
        name="dispatch",
    )(slots, h2)


def _experts_kernel(ts_ref, nu_ref, h_hbm, wg_hbm, wu_hbm, wd_hbm, ys_hbm,
                    xbuf, ybuf, gsem, ysem, wg_f, wu_f, wd_f, wsem, wg_bf, wu_bf, wd_bf, *, layer):
    e = pl.program_id(0)
    ne = pl.num_programs(0)
    n_used = nu_ref[0]
    tm = MOE_TILE
    tsub = tm

    def has_rows(ex):
        return ts_ref[ex + 1] > ts_ref[ex]

    def weight_copies(ex, wslot):
        return [pltpu.make_async_copy(src.at[layer, ex], dst.at[wslot], wsem.at[wslot])
                for src, dst in ((wg_hbm, wg_f), (wu_hbm, wu_f), (wd_hbm, wd_f))]

    def start_weights(ex, wslot):
        for cp in weight_copies(ex, wslot):
            cp.start(priority=1)

    def x_load(tile, slot):
        row0 = pl.multiple_of(tile * tsub, tsub)
        return pltpu.make_async_copy(h_hbm.at[pl.ds(row0, tsub), :], xbuf.at[slot], gsem.at[slot])

    def gather(tile, slot):
        x_load(tile, slot).start()

    def wait_gather(slot):
        x_load(0, slot).wait()

    def y_store(tile, slot):
        row0 = pl.multiple_of(tile * tsub, tsub)
        return pltpu.make_async_copy(ybuf.at[slot], ys_hbm.at[pl.ds(row0, tsub), :], ysem.at[slot])

    wslot = e % 2

    @pl.when((e == 0) & has_rows(0))
    def _():
        start_weights(0, 0)

    @pl.when((e == 0) & (n_used > 0))
    def _():
        gather(0, 0)

    nxt = jnp.minimum(e + 1, ne - 1)

    @pl.when((e + 1 < ne) & has_rows(nxt))
    def _():
        start_weights(nxt, 1 - wslot)

    @pl.when(has_rows(e))
    def _():
        for cp in weight_copies(e, wslot):
            cp.wait()
        wg_bf[...] = wg_f[wslot].astype(BF16)
        wu_bf[...] = wu_f[wslot].astype(BF16)
        wd_bf[...] = wd_f[wslot].astype(BF16)

    def tile_body(g, carry):
        slot = g % 2

        @pl.when(g >= 2)
        def _():
            y_store(g - 2, slot).wait()

        wait_gather(slot)
        x = xbuf[slot].astype(BF16)
        gather(jnp.minimum(g + 1, n_used - 1), 1 - slot)
        gate = jnp.dot(x, wg_bf[...], preferred_element_type=F32)
        up = jnp.dot(x, wu_bf[...], preferred_element_type=F32)
        hid = (gate * _sigmoid(gate)) * up
        ybuf[slot] = jnp.dot(hid.astype(BF16), wd_bf[...], preferred_element_type=F32)
        y_store(g, slot).start(priority=1)
        return carry

    lax.fori_loop(ts_ref[e], ts_ref[e + 1], tile_body, 0)

    @pl.when((e == pl.num_programs(0) - 1) & (n_used > 0))
    def _():
        wait_gather(n_used % 2)
        y_store(n_used - 1, (n_used - 1) % 2).wait()

        @pl.when(n_used >= 2)
        def _():
            y_store(n_used - 2, n_used % 2).wait()


def _experts(xs, tile_start, n_used, w_gate, w_up, w_down, layer):
    d = D_MODEL
    tm = MOE_TILE
    rows = xs.shape[0]

    hbm = pl.BlockSpec(memory_space=pl.ANY)
    return pl.pallas_call(
        functools.partial(_experts_kernel, layer=layer),
        grid_spec=pltpu.PrefetchScalarGridSpec(
            num_scalar_prefetch=2,
            grid=(N_EXPERTS,),
            in_specs=[hbm, hbm, hbm, hbm],
            out_specs=hbm,
            scratch_shapes=[pltpu.VMEM((2, tm, d), F32),
                            pltpu.VMEM((2, tm, d), F32),
                            pltpu.SemaphoreType.DMA((2,)),
                            pltpu.SemaphoreType.DMA((2,)),
                            pltpu.VMEM((2, d, EXPERT_FF), F32),
                            pltpu.VMEM((2, d, EXPERT_FF), F32),
                            pltpu.VMEM((2, EXPERT_FF, d), F32),
                            pltpu.SemaphoreType.DMA((2,)),
                            pltpu.VMEM((d, EXPERT_FF), BF16),
                            pltpu.VMEM((d, EXPERT_FF), BF16),
                            pltpu.VMEM((EXPERT_FF, d), BF16)]),
        out_shape=jax.ShapeDtypeStruct((rows, d), F32),
        compiler_params=_params("arbitrary"),
        name="experts",
    )(tile_start, n_used, xs, w_gate, w_up, w_down)


def _combine_kernel(slot_ref, ys_hbm, w_ref, x_ref, mod_ref, g_ref, b_ref, o_ref, ybuf, sem, *, n_tok):
    i = pl.program_id(0)
    nt = pl.num_programs(0)
    tm = ROW_TILE

    def issue(blk, buf):
        for r in range(tm):
            for k in range(TOP_K):
                row = slot_ref[k * n_tok + blk * tm + r]
                _row_copy(ys_hbm, row, ybuf.at[buf, k, pl.ds(r, 1), :], sem.at[buf]).start(priority=k)

    def wait(buf):
        for k in range(TOP_K):
            pltpu.make_async_copy(ys_hbm.at[pl.ds(0, tm), :], ybuf.at[buf, k], sem.at[buf]).wait()

    @pl.when(i == 0)
    def _():
        issue(0, 0)

    buf = i % 2
    wait(buf)
    issue(jnp.minimum(i + 1, nt - 1), 1 - buf)
    y = w_ref[:, 0:1] * ybuf[buf, 0] + w_ref[:, 1:2] * ybuf[buf, 1]
    t = DEEPNORM_ALPHA * x_ref[...] + mod_ref[MOD_G2:MOD_G2 + 1, :] * y
    o_ref[...] = _layer_norm(t, g_ref[...], b_ref[...])

    @pl.when(i == nt - 1)
    def _():
        wait(1 - buf)


def _combine(ys, slots, w_tok, x1, mods, n_ctx_blocks, ln_g, ln_b):
    t = x1.shape[0]
    d = D_MODEL
    return pl.pallas_call(
        functools.partial(_combine_kernel, n_tok=t),
        grid_spec=pltpu.PrefetchScalarGridSpec(
            num_scalar_prefetch=1,
            grid=(t // ROW_TILE,),
            in_specs=[pl.BlockSpec(memory_space=pl.ANY),
                      pl.BlockSpec((ROW_TILE, TOP_K), lambda i, s: (i, 0)),
                      pl.BlockSpec((ROW_TILE, d), lambda i, s: (i, 0)),
                      pl.BlockSpec((None, 8, d), lambda i, s: (jnp.where(i < n_ctx_blocks, 0, 1), 0, 0)),
                      pl.BlockSpec((1, d), lambda i, s: (0, 0)),
                      pl.BlockSpec((1, d), lambda i, s: (0, 0))],
            out_specs=pl.BlockSpec((ROW_TILE, d), lambda i, s: (i, 0)),
            scratch_shapes=[pltpu.VMEM((2, TOP_K, ROW_TILE, d), F32),
                            pltpu.SemaphoreType.DMA((2,))]),
        out_shape=jax.ShapeDtypeStruct((t, d), F32),
        compiler_params=_params("arbitrary"),
        name="combine",
    )(slots, ys, w_tok, x1, mods, ln_g, ln_b)


def _moe(h2, x1, mods, n_ctx_blocks, wr_t, rb_b, w_gate, w_up, w_down, layer, ln_g, ln_b):
    t = x1.shape[0]
    tm = MOE_TILE
    e_idx, w_tok, rank, cnt = _router(h2, wr_t, rb_b)
    counts = cnt[:, 0].astype(jnp.int32)
    tiles_per = (counts + tm - 1) // tm
    tile_end = jnp.cumsum(tiles_per)
    n_used = tile_end[-1]
    row_off = (tile_end - tiles_per) * tm
    experts = jnp.arange(N_EXPERTS, dtype=jnp.int32)
    slots = jnp.sum(jnp.where(e_idx[:, :, None] == experts, row_off, 0), axis=-1) + rank
    max_tiles = (TOP_K * t + N_EXPERTS * (tm - 1)) // tm + 1
    tile_start = jnp.concatenate([tile_end - tiles_per, n_used.reshape(1)]).astype(jnp.int32)
    slots = slots.reshape(-1)
    xs = _dispatch(h2, slots, max_tiles * tm)
    ys = _experts(xs, tile_start, n_used.reshape(1).astype(jnp.int32), w_gate, w_up, w_down, layer)
    return _combine(ys, slots, w_tok.T, x1, mods, n_ctx_blocks, ln_g, ln_b)


def _rope_tables(n):
    t = jnp.arange(n)
    row = (t // GRID_W).astype(F32)
    col = (t % GRID_W).astype(F32)
    n_freq = HEAD_DIM // 4
    inv_freq = ROPE_BASE ** (-jnp.arange(n_freq, dtype=F32) / n_freq)
    ang = jnp.concatenate([row[:, None] * inv_freq, col[:, None] * inv_freq], axis=-1)
    cos, sin = jnp.cos(ang), jnp.sin(ang)
    return jnp.concatenate([cos, cos], axis=-1), jnp.concatenate([-sin, sin], axis=-1)


def kernel(x, c, ctx, c_ctx, w_mod, b_mod, w_in, attn_sink, na_rpb, sgu_ln_g, sgu_ln_b, sgu_w, sgu_b,
           w_out, ln1_g, ln1_b, w_router, router_bias, w_gate, w_up, w_down, ln2_g, ln2_b):
    batch, n, d = x.shape
    lctx = ctx.shape[1]
    assert batch == 1 and d == D_MODEL and n % ROW_TILE == 0 and lctx % ROW_TILE == 0
    n_ctx_blocks = lctx // ROW_TILE

    mods = _modulation(c, c_ctx, w_mod, b_mod).reshape(DEPTH, 8, 6, d)
    mod_lat = jnp.pad(mods[:, 0], ((0, 0), (0, 2), (0, 0)))
    mod_ctx = jnp.pad(mods[:, 1], ((0, 0), (0, 2), (0, 0)))
    cos, sin = _rope_tables(n)
    wr_f = w_router.T
    def bf16_part(v):
        return lax.bitcast_convert_type(lax.bitcast_convert_type(v, jnp.int32) & jnp.int32(-65536), F32)

    wr_hi = bf16_part(wr_f)
    wr_mid = bf16_part(wr_f - wr_hi)
    wr_lo = wr_f - wr_hi - wr_mid
    wr_t = jnp.concatenate([wr_hi, wr_mid, wr_lo], axis=0).astype(BF16)
    rb_b = jnp.broadcast_to(router_bias.reshape(N_EXPERTS, 1), (N_EXPERTS, LANES))

    x_lat, lat_off = x[0], 0
    x_ctx = ctx[0]
    for l in range(DEPTH):
        last = l == DEPTH - 1
        w_in_bf = w_in[l].astype(BF16)
        w_out_bf = w_out[l].astype(BF16)
        sgu_w_bf = sgu_w[l].astype(BF16)
        sgu_b_b = jnp.broadcast_to(sgu_b[l][:, :, None], (C_GROUPS, CHUNK, LANES))
        ln_g_c, ln_b_c = sgu_ln_g[l].reshape(1, C_W), sgu_ln_b[l].reshape(1, C_W)
        bias_tab = _na_bias_table(na_rpb[l])
        g1, b1 = ln1_g[l].reshape(1, d), ln1_b[l].reshape(1, d)
        g2, b2 = ln2_g[l].reshape(1, d), ln2_b[l].reshape(1, d)

        p = _proj(x_lat, lat_off, n, mod_lat[l], cos, sin, w_in_bf, rope=True)
        pc = _proj(x_ctx, 0, lctx, mod_ctx[l], cos, sin, w_in_bf, rope=False)
        o_a = _attn_a(p, pc, attn_sink[l], latent=True)
        o_b = _attn_b(p, pc, bias_tab, latent=True)
        o_c = _sgu(p, ln_g_c, ln_b_c, sgu_w_bf, sgu_b_b)
        if last:
            x1, h2 = _outproj(o_a, o_b, o_c, w_out_bf, x_lat, lat_off, mod_lat[l], g1, b1, n, 0, None)
            x_lat = _moe(h2, x1, jnp.stack([mod_ctx[l], mod_lat[l]]), 0, wr_t, rb_b,
                         w_gate, w_up, w_down, l, g2, b2)
        else:
            oc_a = _attn_a(pc, pc, attn_sink[l], latent=False)
            oc_b = _attn_b(pc, pc, bias_tab, latent=False)
            oc_c = _sgu(pc, ln_g_c, ln_b_c, sgu_w_bf, sgu_b_b)
            total = lctx + n
            prev = _outproj(oc_a, oc_b, oc_c, w_out_bf, x_ctx, 0, mod_ctx[l], g1, b1, total, 0, None)
            x1, h2 = _outproj(o_a, o_b, o_c, w_out_bf, x_lat, lat_off, mod_lat[l], g1, b1, total,
                              n_ctx_blocks, prev)
            x_all = _moe(h2, x1, jnp.stack([mod_ctx[l], mod_lat[l]]), n_ctx_blocks, wr_t, rb_b,
                         w_gate, w_up, w_down, l, g2, b2)
            x_lat, lat_off, x_ctx = x_all, n_ctx_blocks, x_all
    return x_lat.reshape(batch, n, d)
```

```python
import functools

import numpy as np
import jax
import jax.numpy as jnp
from jax import lax
from jax.experimental import pallas as pl
from jax.experimental.pallas import tpu as pltpu

F32 = jnp.float32
BF16 = jnp.bfloat16

D_MODEL = 2048
DEPTH = 2
GRID_W = 64
HEAD_DIM = 128
A_HEADS = 6
A_KV_HEADS = 2
A_GROUP = A_HEADS // A_KV_HEADS
A_BLOCK = 128
B_HEADS = 6
NA_ROWS = 8
NA_COLS = 16
C_GROUPS = 4
C_W = C_GROUPS * HEAD_DIM
CHUNK = 128
N_EXPERTS = 32
N_EXPERT_GROUPS = 4
EXPERTS_PER_GROUP = N_EXPERTS // N_EXPERT_GROUPS
TOP_K = 2
EXPERT_FF = 512
ROPE_BASE = 10000.0
LN_EPS = 1e-5
NEG_INF = -1e30
DEEPNORM_ALPHA = (2 * DEPTH) ** 0.25
ATTN_SCALE = HEAD_DIM ** -0.5

A_Q_W = A_HEADS * HEAD_DIM
A_KV_W = A_KV_HEADS * HEAD_DIM
B_W = B_HEADS * HEAD_DIM
OFF_AK = A_Q_W
OFF_AV = OFF_AK + A_KV_W
OFF_BQ = OFF_AV + A_KV_W
OFF_BK = OFF_BQ + B_W
OFF_BV = OFF_BK + B_W
OFF_C = OFF_BV + B_W
IN_COLS = OFF_C + 2 * C_W

VMEM_LIMIT_BYTES = 56 * 1024 * 1024
LANES = 128

ROW_TILE = 256
PROJ_COL_TILE = 512
MOD_COL_TILE = 1024
MOE_TILE = 256

MOD_SH1, MOD_SC1, MOD_G1, MOD_SH2, MOD_SC2, MOD_G2 = range(6)


def _params(*sem):
    return pltpu.CompilerParams(dimension_semantics=sem, vmem_limit_bytes=VMEM_LIMIT_BYTES)


def _layer_norm(t, g, b):
    mu = jnp.mean(t, axis=-1, keepdims=True)
    d = t - mu
    var = jnp.mean(d * d, axis=-1, keepdims=True)
    return d * lax.rsqrt(var + LN_EPS) * g + b


def _sigmoid(v):
    return 1.0 / (1.0 + jnp.exp(-v))


def _dot_nt(a, b):
    return lax.dot_general(a, b, (((1,), (1,)), ((), ())), preferred_element_type=F32)


def _mod_kernel(c_ref, w_ref, b_ref, o_ref):
    w = w_ref[...]
    reps = w.shape[1] // LANES
    rows = []
    for r in range(2):
        cv = c_ref[r]
        s = cv * _sigmoid(cv)
        sb = jnp.concatenate([s] * reps, axis=1)
        rows.append(jnp.sum(w * sb, axis=0, keepdims=True) + b_ref[...])
    rows.append(jnp.zeros((6, w.shape[1]), F32))
    o_ref[...] = jnp.concatenate(rows, axis=0)


def _modulation(c, c_ctx, w_mod, b_mod):
    d = D_MODEL
    cb = jnp.stack([jnp.broadcast_to(c.reshape(d, 1), (d, LANES)),
                    jnp.broadcast_to(c_ctx.reshape(d, 1), (d, LANES))])
    n_out = 6 * d
    return pl.pallas_call(
        _mod_kernel,
        grid=(DEPTH, n_out // MOD_COL_TILE),
        in_specs=[pl.BlockSpec((2, d, LANES), lambda l, j: (0, 0, 0)),
                  pl.BlockSpec((None, d, MOD_COL_TILE), lambda l, j: (l, 0, j)),
                  pl.BlockSpec((None, 1, MOD_COL_TILE), lambda l, j: (l, 0, j))],
        out_specs=pl.BlockSpec((None, 8, MOD_COL_TILE), lambda l, j: (l, 0, j)),
        out_shape=jax.ShapeDtypeStruct((DEPTH, 8, n_out), F32),
        compiler_params=_params("arbitrary", "arbitrary"),
        name="modulation",
    )(cb, w_mod, b_mod.reshape(DEPTH, 1, n_out))


def _gelu_tanh(v):
    return 0.5 * v * (1.0 + jnp.tanh(np.sqrt(2.0 / np.pi).astype(np.float32) * (v + 0.044715 * (v * v * v))))


def _proj_kernel(x_ref, mod_ref, cos_ref, sin_ref, w_ref, o_ref, *, rope):
    x = x_ref[...]
    h = (x * (1.0 + mod_ref[MOD_SC1:MOD_SC1 + 1, :]) + mod_ref[MOD_SH1:MOD_SH1 + 1, :]).astype(BF16)
    tn = PROJ_COL_TILE
    for j in range(IN_COLS // tn):
        c0 = j * tn
        acc = jnp.dot(h, w_ref[:, c0:c0 + tn], preferred_element_type=F32)
        if c0 < OFF_AV:
            if rope:
                cos = cos_ref[...]
                sin = sin_ref[...]
                parts = []
                for hh in range(tn // HEAD_DIM):
                    a = acc[:, hh * HEAD_DIM:(hh + 1) * HEAD_DIM]
                    parts.append(a * cos + pltpu.roll(a, HEAD_DIM // 2, 1) * sin)
                acc = jnp.concatenate(parts, axis=1)
        elif c0 >= OFF_C:
            acc = _gelu_tanh(acc)
        o_ref[:, c0:c0 + tn] = acc.astype(BF16)


def _proj(x, x_off, rows, mod, cos, sin, w_bf, rope):
    d = D_MODEL
    return pl.pallas_call(
        functools.partial(_proj_kernel, rope=rope),
        grid=(rows // ROW_TILE,),
        in_specs=[pl.BlockSpec((ROW_TILE, d), lambda i: (i + x_off, 0)),
                  pl.BlockSpec((8, d), lambda i: (0, 0)),
                  pl.BlockSpec((ROW_TILE, HEAD_DIM), lambda i: (i, 0)),
                  pl.BlockSpec((ROW_TILE, HEAD_DIM), lambda i: (i, 0)),
                  pl.BlockSpec((d, IN_COLS), lambda i: (0, 0), pipeline_mode=pl.Buffered(1))],
        out_specs=pl.BlockSpec((ROW_TILE, IN_COLS), lambda i: (i, 0)),
        out_shape=jax.ShapeDtypeStruct((rows, IN_COLS), BF16),
        compiler_params=_params("arbitrary"),
        name="proj_rope" if rope else "proj_ctx",
    )(x, mod, cos, sin, w_bf)


LOG2E = 1.4426950408889634
LOGIT_SCALE = ATTN_SCALE * LOG2E


def _softmax_pv(s_parts, v_parts, sink):
    m = s_parts[0].max(axis=-1, keepdims=True)
    for s in s_parts[1:]:
        m = jnp.maximum(m, s.max(axis=-1, keepdims=True))
    if sink is not None:
        m = jnp.maximum(m, sink)
    acc = None
    for s, v in zip(s_parts, v_parts):
        e = jnp.exp2(s - m).astype(BF16)
        v1 = jnp.concatenate([v, jnp.ones((v.shape[0], LANES), BF16)], axis=1)
        pv = jnp.dot(e, v1, preferred_element_type=F32)
        acc = pv if acc is None else acc + pv
    denom = acc[:, HEAD_DIM:HEAD_DIM + 1]
    if sink is not None:
        denom = denom + jnp.exp2(sink - m)
    return acc[:, :HEAD_DIM] / denom


def _attn_a_kernel(sink_ref, q_ref, kp_ref, kc_ref, kn_ref, vp_ref, vc_ref, vn_ref, kx_ref, vx_ref, o_ref,
                   mask_ref, *, latent):
    i = pl.program_id(0)
    nb = pl.num_programs(0)
    nq = A_GROUP * A_BLOCK

    if latent:
        @pl.when(i == 0)
        def _():
            qi = lax.broadcasted_iota(jnp.int32, (nq, 3 * A_BLOCK), 0) % A_BLOCK
            jj = lax.broadcasted_iota(jnp.int32, (nq, 3 * A_BLOCK), 1)
            ok = (jj >= qi) & (jj <= qi + 2 * A_BLOCK)
            mask_ref[0] = jnp.where(ok & (jj >= A_BLOCK), 0.0, NEG_INF).astype(F32)
            mask_ref[1] = jnp.where(ok, 0.0, NEG_INF).astype(F32)
            mask_ref[2] = jnp.where(ok & (jj < 2 * A_BLOCK), 0.0, NEG_INF).astype(F32)

        which = jnp.where(i == 0, 0, jnp.where(i == nb - 1, 2, 1))

    for kh in range(A_KV_HEADS):
        hs = [kh * A_GROUP + g for g in range(A_GROUP)]
        q = jnp.concatenate([q_ref[:, h * HEAD_DIM:(h + 1) * HEAD_DIM] for h in hs], axis=0)
        sink = jnp.concatenate([jnp.full((A_BLOCK, 1), sink_ref[h] * LOG2E, F32) for h in hs], axis=0)
        ks = slice(kh * HEAD_DIM, (kh + 1) * HEAD_DIM)
        s_parts = [_dot_nt(q, kx_ref[:, ks]) * LOGIT_SCALE]
        v_parts = [vx_ref[:, ks]]
        if latent:
            kband = jnp.concatenate([kp_ref[:, ks], kc_ref[:, ks], kn_ref[:, ks]], axis=0)
            vband = jnp.concatenate([vp_ref[:, ks], vc_ref[:, ks], vn_ref[:, ks]], axis=0)
            s_parts.append(_dot_nt(q, kband) * LOGIT_SCALE + mask_ref[which])
            v_parts.append(vband)
        out = _softmax_pv(s_parts, v_parts, sink)
        for g, h in enumerate(hs):
            o_ref[:, h * HEAD_DIM:(h + 1) * HEAD_DIM] = out[g * A_BLOCK:(g + 1) * A_BLOCK].astype(BF16)


def _attn_a(p, pc, sink, latent):
    rows = p.shape[0]
    nb = rows // A_BLOCK
    assert nb >= 2
    kcol = OFF_AK // A_KV_W
    vcol = OFF_AV // A_KV_W

    def band(col, shift):
        return pl.BlockSpec((A_BLOCK, A_KV_W), lambda i, s: (jnp.clip(i + shift, 0, nb - 1), col))

    lctx = pc.shape[0]
    return pl.pallas_call(
        functools.partial(_attn_a_kernel, latent=latent),
        grid_spec=pltpu.PrefetchScalarGridSpec(
            num_scalar_prefetch=1,
            grid=(nb,),
            in_specs=[pl.BlockSpec((A_BLOCK, A_Q_W), lambda i, s: (i, 0)),
                      band(kcol, -1), band(kcol, 0), band(kcol, 1),
                      band(vcol, -1), band(vcol, 0), band(vcol, 1),
                      pl.BlockSpec((lctx, A_KV_W), lambda i, s: (0, kcol)),
                      pl.BlockSpec((lctx, A_KV_W), lambda i, s: (0, vcol))],
            out_specs=pl.BlockSpec((A_BLOCK, A_Q_W), lambda i, s: (i, 0)),
            scratch_shapes=[pltpu.VMEM((3, A_GROUP * A_BLOCK, 3 * A_BLOCK), F32)]),
        out_shape=jax.ShapeDtypeStruct((rows, A_Q_W), BF16),
        compiler_params=_params("arbitrary"),
        name="attn_a_latent" if latent else "attn_a_ctx",
    )(sink, p, p, p, p, p, p, p, pc, pc)


B_PAIR_W = 2 * HEAD_DIM
NA_GROUP_ROWS = ROW_TILE // GRID_W
NA_WIN_ROWS = NA_ROWS + NA_GROUP_ROWS
NA_PAIRS = NA_WIN_ROWS // 2
NA_BIAS_OFFS = 2 * NA_ROWS
NA_BOTH, NA_LEFT, NA_RIGHT = range(3)


def _attn_b_kernel(q_ref, k_ref, v_ref, kx_ref, vx_ref, bias_ref, o_ref, *, latent, grid_rows):
    g = pl.program_id(1)
    if latent:
        r_base = g * NA_GROUP_ROWS
        w0 = jnp.clip(r_base - NA_ROWS // 2, 0, grid_rows - NA_WIN_ROWS)
        start = pl.multiple_of(w0 * GRID_W, GRID_W)
    for hh in range(2):
        hs = slice(hh * HEAD_DIM, (hh + 1) * HEAD_DIM)
        q = q_ref[:, hs]
        s_parts = [_dot_nt(q, kx_ref[:, hs]) * LOGIT_SCALE]
        v_parts = [vx_ref[:, hs]]
        if latent:
            kwin = k_ref[pl.ds(start, NA_WIN_ROWS * GRID_W), hs]
            vwin = v_ref[pl.ds(start, NA_WIN_ROWS * GRID_W), hs]
            bias_rows = []
            for rr in range(NA_GROUP_ROWS):
                r = r_base + rr
                r0 = jnp.clip(r - NA_ROWS // 2, 0, grid_rows - NA_ROWS)
                tiles = []
                for jp in range(NA_PAIRS):
                    ka = w0 + 2 * jp
                    in_a = (ka >= r0) & (ka < r0 + NA_ROWS)
                    in_b = (ka + 1 >= r0) & (ka + 1 < r0 + NA_ROWS)
                    variant = jnp.where(in_a, jnp.where(in_b, NA_BOTH, NA_LEFT), jnp.where(in_b, NA_RIGHT, NA_LEFT))
                    off = jnp.where(in_a | in_b, jnp.clip(ka - r + NA_ROWS, 0, NA_BIAS_OFFS - 1), 0)
                    tiles.append(bias_ref[hh, variant, off])
                bias_rows.append(jnp.concatenate(tiles, axis=1))
            bias = jnp.concatenate(bias_rows, axis=0)
            s_parts.append(_dot_nt(q, kwin) * LOGIT_SCALE + bias)
            v_parts.append(vwin)
        out = _softmax_pv(s_parts, v_parts, None)
        o_ref[:, hs] = out.astype(BF16)


def _na_bias_table(rpb):
    cols = np.arange(GRID_W)
    c0 = np.clip(cols - NA_COLS // 2, 0, GRID_W - NA_COLS)
    rel = cols[None, :] - cols[:, None] + NA_COLS - 1
    ok = (cols[None, :] >= c0[:, None]) & (cols[None, :] < c0[:, None] + NA_COLS)
    onehot = (rel[None] == np.arange(2 * NA_COLS - 1)[:, None, None]).astype(np.float32)
    t = jnp.einsum("hrd,dqk->hrqk", rpb, onehot, precision=lax.Precision.HIGHEST)
    t = jnp.where(ok[None, None], t, NEG_INF).astype(F32)
    t = t * LOG2E
    t = jnp.pad(t, ((0, 0), (1, 1), (0, 0), (0, 0)), constant_values=NEG_INF)
    first, second = t[:, :-1], t[:, 1:]
    masked = jnp.full_like(first, NEG_INF)
    return jnp.stack([jnp.concatenate([first, second], axis=-1),
                      jnp.concatenate([first, masked], axis=-1),
                      jnp.concatenate([masked, second], axis=-1)], axis=1)


def _attn_b(p, pc, bias_tab, latent):
    rows = p.shape[0]
    lctx = pc.shape[0]
    assert not latent or rows // GRID_W >= NA_WIN_ROWS
    qcol = OFF_BQ // B_PAIR_W
    kcol = OFF_BK // B_PAIR_W
    vcol = OFF_BV // B_PAIR_W
    return pl.pallas_call(
        functools.partial(_attn_b_kernel, latent=latent, grid_rows=rows // GRID_W),
        grid=(B_HEADS // 2, rows // ROW_TILE),
        in_specs=[pl.BlockSpec((ROW_TILE, B_PAIR_W), lambda hp, g: (g, qcol + hp)),
                  pl.BlockSpec((rows, B_PAIR_W), lambda hp, g: (0, kcol + hp)),
                  pl.BlockSpec((rows, B_PAIR_W), lambda hp, g: (0, vcol + hp)),
                  pl.BlockSpec((lctx, B_PAIR_W), lambda hp, g: (0, kcol + hp)),
                  pl.BlockSpec((lctx, B_PAIR_W), lambda hp, g: (0, vcol + hp)),
                  pl.BlockSpec((2, 3, NA_BIAS_OFFS, GRID_W, 2 * GRID_W), lambda hp, g: (hp, 0, 0, 0, 0))],
        out_specs=pl.BlockSpec((ROW_TILE, B_PAIR_W), lambda hp, g: (g, hp)),
        out_shape=jax.ShapeDtypeStruct((rows, B_W), BF16),
        compiler_params=_params("arbitrary", "arbitrary"),
        name="attn_b_latent" if latent else "attn_b_ctx",
    )(p, p, p, pc, pc, bias_tab)


def _sgu_kernel(u_ref, v_ref, g_ref, b_ref, w_ref, bs_ref, o_ref):
    for ch in range(ROW_TILE // CHUNK):
        rs = slice(ch * CHUNK, (ch + 1) * CHUNK)
        for grp in range(C_GROUPS):
            cs = slice(grp * HEAD_DIM, (grp + 1) * HEAD_DIM)
            vn = _layer_norm(v_ref[rs, cs].astype(F32), g_ref[:, cs], b_ref[:, cs])
            mixed = jnp.dot(w_ref[grp], vn.astype(BF16), preferred_element_type=F32) + bs_ref[grp]
            o_ref[rs, cs] = (u_ref[rs, cs].astype(F32) * mixed).astype(BF16)


def _sgu(p, ln_g, ln_b, w_bf, bs_b):
    rows = p.shape[0]
    ucol = OFF_C // C_W
    return pl.pallas_call(
        _sgu_kernel,
        grid=(rows // ROW_TILE,),
        in_specs=[pl.BlockSpec((ROW_TILE, C_W), lambda i: (i, ucol)),
                  pl.BlockSpec((ROW_TILE, C_W), lambda i: (i, ucol + 1)),
                  pl.BlockSpec((1, C_W), lambda i: (0, 0)),
                  pl.BlockSpec((1, C_W), lambda i: (0, 0)),
                  pl.BlockSpec((C_GROUPS, CHUNK, CHUNK), lambda i: (0, 0, 0)),
                  pl.BlockSpec((C_GROUPS, CHUNK, LANES), lambda i: (0, 0, 0))],
        out_specs=pl.BlockSpec((ROW_TILE, C_W), lambda i: (i, 0)),
        out_shape=jax.ShapeDtypeStruct((rows, C_W), BF16),
        compiler_params=_params("arbitrary"),
        name="sgu",
    )(p, p, ln_g, ln_b, w_bf, bs_b)


def _outproj_kernel(oa_ref, ob_ref, oc_ref, w_ref, x_ref, mod_ref, g_ref, b_ref, *rest):
    x1_ref, h2_ref = rest[-2], rest[-1]
    mix = jnp.dot(oa_ref[...], w_ref[0:A_Q_W, :], preferred_element_type=F32)
    mix += jnp.dot(ob_ref[...], w_ref[A_Q_W:A_Q_W + B_W, :], preferred_element_type=F32)
    mix += jnp.dot(oc_ref[...], w_ref[A_Q_W + B_W:, :], preferred_element_type=F32)
    t = DEEPNORM_ALPHA * x_ref[...] + mod_ref[MOD_G1:MOD_G1 + 1, :] * mix
    x1 = _layer_norm(t, g_ref[...], b_ref[...])
    x1_ref[...] = x1
    h2_ref[...] = x1 * (1.0 + mod_ref[MOD_SC2:MOD_SC2 + 1, :]) + mod_ref[MOD_SH2:MOD_SH2 + 1, :]


def _outproj(o_a, o_b, o_c, w_bf, x, x_off, mod, ln_g, ln_b, total_rows, out_off, prev):
    rows = o_a.shape[0]
    d = D_MODEL
    in_specs = [pl.BlockSpec((ROW_TILE, A_Q_W), lambda i: (i, 0)),
                pl.BlockSpec((ROW_TILE, B_W), lambda i: (i, 0)),
                pl.BlockSpec((ROW_TILE, C_W), lambda i: (i, 0)),
                pl.BlockSpec((d, d), lambda i: (0, 0), pipeline_mode=pl.Buffered(1)),
                pl.BlockSpec((ROW_TILE, d), lambda i: (i + x_off, 0)),
                pl.BlockSpec((8, d), lambda i: (0, 0)),
                pl.BlockSpec((1, d), lambda i: (0, 0)),
                pl.BlockSpec((1, d), lambda i: (0, 0))]
    args = [o_a, o_b, o_c, w_bf, x, mod, ln_g, ln_b]
    aliases = {}
    if prev is not None:
        in_specs += [pl.BlockSpec(memory_space=pl.ANY), pl.BlockSpec(memory_space=pl.ANY)]
        aliases = {len(args): 0, len(args) + 1: 1}
        args += list(prev)
    return pl.pallas_call(
        _outproj_kernel,
        grid=(rows // ROW_TILE,),
        in_specs=in_specs,
        out_specs=[pl.BlockSpec((ROW_TILE, d), lambda i: (i + out_off, 0))] * 2,
        out_shape=[jax.ShapeDtypeStruct((total_rows, d), F32)] * 2,
        input_output_aliases=aliases,
        compiler_params=_params("arbitrary"),
        name="outproj",
    )(*args)


def _top2_sublanes(vals, sub):
    m1 = vals.max(axis=0, keepdims=True)
    i1 = jnp.where(vals == m1, sub, vals.shape[0]).min(axis=0, keepdims=True)
    rest = jnp.where(sub == i1, -jnp.inf, vals)
    m2 = rest.max(axis=0, keepdims=True)
    i2 = jnp.where(rest == m2, sub, vals.shape[0]).min(axis=0, keepdims=True)
    return m1, i1, m2, i2


def _router_kernel(h_ref, wr_ref, rb_ref, e_ref, w_ref, rank_ref, cnt_ref, run_ref):
    i = pl.program_id(0)
    tm = ROW_TILE
    epg = EXPERTS_PER_GROUP

    @pl.when(i == 0)
    def _():
        run_ref[...] = jnp.zeros_like(run_ref)

    h = h_ref[...]
    h_hi = h.astype(BF16)
    h_mid = (h - h_hi.astype(F32)).astype(BF16)
    parts = _dot_nt(wr_ref[...], h_hi)
    parts_mid = _dot_nt(wr_ref[0:2 * N_EXPERTS, :], h_mid)
    logits = (parts[0:N_EXPERTS] + parts[N_EXPERTS:2 * N_EXPERTS] + parts[2 * N_EXPERTS:]
              + parts_mid[0:N_EXPERTS] + parts_mid[N_EXPERTS:])
    scores = _sigmoid(logits)
    biased = scores + jnp.concatenate([rb_ref[...]] * (tm // LANES), axis=1)
    sub = lax.broadcasted_iota(jnp.int32, (epg, tm), 0)

    best = None
    for g in range(N_EXPERT_GROUPS):
        m1, _, m2, _ = _top2_sublanes(biased[g * epg:(g + 1) * epg], sub)
        gs = m1 + m2
        if best is None:
            best, grp = gs, jnp.zeros((1, tm), jnp.int32)
            bsel, ssel = biased[0:epg], scores[0:epg]
        else:
            better = gs > best
            best = jnp.where(better, gs, best)
            grp = jnp.where(better, g, grp)
            bsel = jnp.where(better, biased[g * epg:(g + 1) * epg], bsel)
            ssel = jnp.where(better, scores[g * epg:(g + 1) * epg], ssel)
    _, i1, _, i2 = _top2_sublanes(bsel, sub)
    w1 = jnp.where(sub == i1, ssel, 0.0).sum(axis=0, keepdims=True)
    w2 = jnp.where(sub == i2, ssel, 0.0).sum(axis=0, keepdims=True)
    tot = w1 + w2
    e1 = grp * epg + i1
    e2 = grp * epg + i2

    eiota = lax.broadcasted_iota(jnp.int32, (N_EXPERTS, tm), 0)
    oh1 = (eiota == e1).astype(F32)
    oh2 = (eiota == e2).astype(F32)
    ohb = oh1 + oh2
    before = (lax.broadcasted_iota(jnp.int32, (tm, tm), 0) < lax.broadcasted_iota(jnp.int32, (tm, tm), 1))
    prefix = jnp.dot(ohb.astype(BF16), before.astype(BF16), preferred_element_type=F32)
    pos = run_ref[...] + prefix
    r1 = (oh1 * pos).sum(axis=0, keepdims=True)
    r2 = (oh2 * pos).sum(axis=0, keepdims=True)
    run_ref[...] = run_ref[...] + ohb.sum(axis=1, keepdims=True)

    e_ref[...] = jnp.concatenate([e1, e2], axis=0)
    w_ref[...] = jnp.concatenate([w1 / tot, w2 / tot], axis=0)
    rank_ref[...] = jnp.concatenate([r1, r2], axis=0).astype(jnp.int32)
    cnt_ref[...] = run_ref[:, 0:LANES]


def _router(h2, wr_t, rb_b):
    t = h2.shape[0]
    d = D_MODEL
    row2 = pl.BlockSpec((TOP_K, ROW_TILE), lambda i: (0, i))
    return pl.pallas_call(
        _router_kernel,
        grid=(t // ROW_TILE,),
        in_specs=[pl.BlockSpec((ROW_TILE, d), lambda i: (i, 0)),
                  pl.BlockSpec((3 * N_EXPERTS, d), lambda i: (0, 0)),
                  pl.BlockSpec((N_EXPERTS, LANES), lambda i: (0, 0))],
        out_specs=[row2, row2, row2, pl.BlockSpec((N_EXPERTS, LANES), lambda i: (0, 0))],
        out_shape=[jax.ShapeDtypeStruct((TOP_K, t), jnp.int32),
                   jax.ShapeDtypeStruct((TOP_K, t), F32),
                   jax.ShapeDtypeStruct((TOP_K, t), jnp.int32),
                   jax.ShapeDtypeStruct((N_EXPERTS, LANES), F32)],
        scratch_shapes=[pltpu.VMEM((N_EXPERTS, ROW_TILE), F32)],
        compiler_params=_params("arbitrary"),
        name="router",
    )(h2, wr_t, rb_b)


def _row_copy(src, row, dst, sem):
    return pltpu.make_async_copy(src.at[pl.ds(row, 1), :], dst, sem)


DISPATCH_SLOTS = 3


def _dispatch_kernel(slot_ref, h_hbm, xs_hbm, hbuf, lsem, ssem, *, n_tok):
    i = pl.program_id(0)
    nt = pl.num_programs(0)
    tm = ROW_TILE

    def load(blk):
        row0 = pl.multiple_of(blk * tm, tm)
        return pltpu.make_async_copy(h_hbm.at[pl.ds(row0, tm), :], hbuf.at[blk % DISPATCH_SLOTS],
                                     lsem.at[blk % DISPATCH_SLOTS])

    def wait_rows(par):
        for _ in range(TOP_K):
            pltpu.make_async_copy(hbuf.at[0], xs_hbm.at[pl.ds(0, tm), :], ssem.at[par]).wait()

    @pl.when(i == 0)
    def _():
        load(0).start()

    @pl.when(i + 1 < nt)
    def _():
        load(i + 1).start()

    load(i).wait()
    par = i % 2
    cur = hbuf.at[i % DISPATCH_SLOTS]
    for r in range(tm):
        for k in range(TOP_K):
            dst = xs_hbm.at[pl.ds(slot_ref[k * n_tok + i * tm + r], 1), :]
            pltpu.make_async_copy(cur.at[pl.ds(r, 1), :], dst, ssem.at[par]).start(priority=k)

    @pl.when(i > 0)
    def _():
        wait_rows(1 - par)

    @pl.when(i == nt - 1)
    def _():
        wait_rows(par)


def _dispatch(h2, slots, rows):
    t, d = h2.shape
    return pl.pallas_call(
        functools.partial(_dispatch_kernel, n_tok=t),
        grid_spec=pltpu.PrefetchScalarGridSpec(
            num_scalar_prefetch=1,
            grid=(t // ROW_TILE,),
            in_specs=[pl.BlockSpec(memory_space=pl.ANY)],
            out_specs=pl.BlockSpec(memory_space=pl.ANY),
            scratch_shapes=[pltpu.VMEM((DISPATCH_SLOTS, ROW_TILE, d), F32),
                            pltpu.SemaphoreType.DMA((DISPATCH_SLOTS,)),
                            pltpu.SemaphoreType.DMA((2,))]),
        out_shape=jax.ShapeDtypeStruct((rows, d), F32),
        compiler_params=_params("arbitrary"),
        name="dispatch",
    )(slots, h2)


def _experts_kernel(ts_ref, nu_ref, h_hbm, wg_hbm, wu_hbm, wd_hbm, ys_hbm,
                    xbuf, ybuf, gsem, ysem, wg_f, wu_f, wd_f, wsem, wg_bf, wu_bf, wd_bf, *, layer):
    e = pl.program_id(0)
    ne = pl.num_programs(0)
    n_used = nu_ref[0]
    tm = MOE_TILE
    tsub = tm

    def has_rows(ex):
        return ts_ref[ex + 1] > ts_ref[ex]

    def weight_copies(ex, wslot):
        return [pltpu.make_async_copy(src.at[layer, ex], dst.at[wslot], wsem.at[wslot])
                for src, dst in ((wg_hbm, wg_f), (wu_hbm, wu_f), (wd_hbm, wd_f))]

    def start_weights(ex, wslot):
        for cp in weight_copies(ex, wslot):
            cp.start(priority=1)

    def x_load(tile, slot):
        row0 = pl.multiple_of(tile * tsub, tsub)
        return pltpu.make_async_copy(h_hbm.at[pl.ds(row0, tsub), :], xbuf.at[slot], gsem.at[slot])

    def gather(tile, slot):
        x_load(tile, slot).start()

    def wait_gather(slot):
        x_load(0, slot).wait()

    def y_store(tile, slot):
        row0 = pl.multiple_of(tile * tsub, tsub)
        return pltpu.make_async_copy(ybuf.at[slot], ys_hbm.at[pl.ds(row0, tsub), :], ysem.at[slot])

    wslot = e % 2

    @pl.when((e == 0) & has_rows(0))
    def _():
        start_weights(0, 0)

    @pl.when((e == 0) & (n_used > 0))
    def _():
        gather(0, 0)

    nxt = jnp.minimum(e + 1, ne - 1)

    @pl.when((e + 1 < ne) & has_rows(nxt))
    def _():
        start_weights(nxt, 1 - wslot)

    @pl.when(has_rows(e))
    def _():
        for cp in weight_copies(e, wslot):
            cp.wait()
        wg_bf[...] = wg_f[wslot].astype(BF16)
        wu_bf[...] = wu_f[wslot].astype(BF16)
        wd_bf[...] = wd_f[wslot].astype(BF16)

    def tile_body(g, carry):
        slot = g % 2

        @pl.when(g >= 2)
        def _():
            y_store(g - 2, slot).wait()

        wait_gather(slot)
        x = xbuf[slot].astype(BF16)
        gather(jnp.minimum(g + 1, n_used - 1), 1 - slot)
        gate = jnp.dot(x, wg_bf[...], preferred_element_type=F32)
        up = jnp.dot(x, wu_bf[...], preferred_element_type=F32)
        hid = (gate * _sigmoid(gate)) * up
        ybuf[slot] = jnp.dot(hid.astype(BF16), wd_bf[...], preferred_element_type=F32)
        y_store(g, slot).start(priority=1)
        return carry

    lax.fori_loop(ts_ref[e], ts_ref[e + 1], tile_body, 0)

    @pl.when((e == pl.num_programs(0) - 1) & (n_used > 0))
    def _():
        wait_gather(n_used % 2)
        y_store(n_used - 1, (n_used - 1) % 2).wait()

        @pl.when(n_used >= 2)
        def _():
            y_store(n_used - 2, n_used % 2).wait()


def _experts(xs, tile_start, n_used, w_gate, w_up, w_down, layer):
    d = D_MODEL
    tm = MOE_TILE
    rows = xs.shape[0]

    hbm = pl.BlockSpec(memory_space=pl.ANY)
    return pl.pallas_call(
        functools.partial(_experts_kernel, layer=layer),
        grid_spec=pltpu.PrefetchScalarGridSpec(
            num_scalar_prefetch=2,
            grid=(N_EXPERTS,),
            in_specs=[hbm, hbm, hbm, hbm],
            out_specs=hbm,
            scratch_shapes=[pltpu.VMEM((2, tm, d), F32),
                            pltpu.VMEM((2, tm, d), F32),
                            pltpu.SemaphoreType.DMA((2,)),
                            pltpu.SemaphoreType.DMA((2,)),
                            pltpu.VMEM((2, d, EXPERT_FF), F32),
                            pltpu.VMEM((2, d, EXPERT_FF), F32),
                            pltpu.VMEM((2, EXPERT_FF, d), F32),
                            pltpu.SemaphoreType.DMA((2,)),
                            pltpu.VMEM((d, EXPERT_FF), BF16),
                            pltpu.VMEM((d, EXPERT_FF), BF16),
                            pltpu.VMEM((EXPERT_FF, d), BF16)]),
        out_shape=jax.ShapeDtypeStruct((rows, d), F32),
        compiler_params=_params("arbitrary"),
        name="experts",
    )(tile_start, n_used, xs, w_gate, w_up, w_down)


def _combine_kernel(slot_ref, ys_hbm, w_ref, x_ref, mod_ref, g_ref, b_ref, o_ref, ybuf, sem, *, n_tok):
    i = pl.program_id(0)
    nt = pl.num_programs(0)
    tm = ROW_TILE

    def issue(blk, buf):
        for r in range(tm):
            for k in range(TOP_K):
                row = slot_ref[k * n_tok + blk * tm + r]
                _row_copy(ys_hbm, row, ybuf.at[buf, k, pl.ds(r, 1), :], sem.at[buf]).start(priority=k)

    def wait(buf):
        for k in range(TOP_K):
            pltpu.make_async_copy(ys_hbm.at[pl.ds(0, tm), :], ybuf.at[buf, k], sem.at[buf]).wait()

    @pl.when(i == 0)
    def _():
        issue(0, 0)

    buf = i % 2
    wait(buf)
    issue(jnp.minimum(i + 1, nt - 1), 1 - buf)
    y = w_ref[:, 0:1] * ybuf[buf, 0] + w_ref[:, 1:2] * ybuf[buf, 1]
    t = DEEPNORM_ALPHA * x_ref[...] + mod_ref[MOD_G2:MOD_G2 + 1, :] * y
    o_ref[...] = _layer_norm(t, g_ref[...], b_ref[...])

    @pl.when(i == nt - 1)
    def _():
        wait(1 - buf)


def _combine(ys, slots, w_tok, x1, mods, n_ctx_blocks, ln_g, ln_b):
    t = x1.shape[0]
    d = D_MODEL
    return pl.pallas_call(
        functools.partial(_combine_kernel, n_tok=t),
        grid_spec=pltpu.PrefetchScalarGridSpec(
            num_scalar_prefetch=1,
            grid=(t // ROW_TILE,),
            in_specs=[pl.BlockSpec(memory_space=pl.ANY),
                      pl.BlockSpec((ROW_TILE, TOP_K), lambda i, s: (i, 0)),
                      pl.BlockSpec((ROW_TILE, d), lambda i, s: (i, 0)),
                      pl.BlockSpec((None, 8, d), lambda i, s: (jnp.where(i < n_ctx_blocks, 0, 1), 0, 0)),
                      pl.BlockSpec((1, d), lambda i, s: (0, 0)),
                      pl.BlockSpec((1, d), lambda i, s: (0, 0))],
            out_specs=pl.BlockSpec((ROW_TILE, d), lambda i, s: (i, 0)),
            scratch_shapes=[pltpu.VMEM((2, TOP_K, ROW_TILE, d), F32),
                            pltpu.SemaphoreType.DMA((2,))]),
        out_shape=jax.ShapeDtypeStruct((t, d), F32),
        compiler_params=_params("arbitrary"),
        name="combine",
    )(slots, ys, w_tok, x1, mods, ln_g, ln_b)


def _moe(h2, x1, mods, n_ctx_blocks, wr_t, rb_b, w_gate, w_up, w_down, layer, ln_g, ln_b):
    t = x1.shape[0]
    tm = MOE_TILE
    e_idx, w_tok, rank, cnt = _router(h2, wr_t, rb_b)
    counts = cnt[:, 0].astype(jnp.int32)
    tiles_per = (counts + tm - 1) // tm
    tile_end = jnp.cumsum(tiles_per)
    n_used = tile_end[-1]
    row_off = (tile_end - tiles_per) * tm
    experts = jnp.arange(N_EXPERTS, dtype=jnp.int32)
    slots = jnp.sum(jnp.where(e_idx[:, :, None] == experts, row_off, 0), axis=-1) + rank
    max_tiles = (TOP_K * t + N_EXPERTS * (tm - 1)) // tm + 1
    tile_start = jnp.concatenate([tile_end - tiles_per, n_used.reshape(1)]).astype(jnp.int32)
    slots = slots.reshape(-1)
    xs = _dispatch(h2, slots, max_tiles * tm)
    ys = _experts(xs, tile_start, n_used.reshape(1).astype(jnp.int32), w_gate, w_up, w_down, layer)
    return _combine(ys, slots, w_tok.T, x1, mods, n_ctx_blocks, ln_g, ln_b)


def _rope_tables(n):
    t = jnp.arange(n)
    row = (t // GRID_W).astype(F32)
    col = (t % GRID_W).astype(F32)
    n_freq = HEAD_DIM // 4
    inv_freq = ROPE_BASE ** (-jnp.arange(n_freq, dtype=F32) / n_freq)
    ang = jnp.concatenate([row[:, None] * inv_freq, col[:, None] * inv_freq], axis=-1)
    cos, sin = jnp.cos(ang), jnp.sin(ang)
    return jnp.concatenate([cos, cos], axis=-1), jnp.concatenate([-sin, sin], axis=-1)


def kernel(x, c, ctx, c_ctx, w_mod, b_mod, w_in, attn_sink, na_rpb, sgu_ln_g, sgu_ln_b, sgu_w, sgu_b,
           w_out, ln1_g, ln1_b, w_router, router_bias, w_gate, w_up, w_down, ln2_g, ln2_b):
    batch, n, d = x.shape
    lctx = ctx.shape[1]
    assert batch == 1 and d == D_MODEL and n % ROW_TILE == 0 and lctx % ROW_TILE == 0
    n_ctx_blocks = lctx // ROW_TILE

    mods = _modulation(c, c_ctx, w_mod, b_mod).reshape(DEPTH, 8, 6, d)
    mod_lat = jnp.pad(mods[:, 0], ((0, 0), (0, 2), (0, 0)))
    mod_ctx = jnp.pad(mods[:, 1], ((0, 0), (0, 2), (0, 0)))
    cos, sin = _rope_tables(n)
    wr_f = w_router.T
    def bf16_part(v):
        return lax.bitcast_convert_type(lax.bitcast_convert_type(v, jnp.int32) & jnp.int32(-65536), F32)

    wr_hi = bf16_part(wr_f)
    wr_mid = bf16_part(wr_f - wr_hi)
    wr_lo = wr_f - wr_hi - wr_mid
    wr_t = jnp.concatenate([wr_hi, wr_mid, wr_lo], axis=0).astype(BF16)
    rb_b = jnp.broadcast_to(router_bias.reshape(N_EXPERTS, 1), (N_EXPERTS, LANES))

    x_lat, lat_off = x[0], 0
    x_ctx = ctx[0]
    for l in range(DEPTH):
        last = l == DEPTH - 1
        w_in_bf = w_in[l].astype(BF16)
        w_out_bf = w_out[l].astype(BF16)
        sgu_w_bf = sgu_w[l].astype(BF16)
        sgu_b_b = jnp.broadcast_to(sgu_b[l][:, :, None], (C_GROUPS, CHUNK, LANES))
        ln_g_c, ln_b_c = sgu_ln_g[l].reshape(1, C_W), sgu_ln_b[l].reshape(1, C_W)
        bias_tab = _na_bias_table(na_rpb[l])
        g1, b1 = ln1_g[l].reshape(1, d), ln1_b[l].reshape(1, d)
        g2, b2 = ln2_g[l].reshape(1, d), ln2_b[l].reshape(1, d)

        p = _proj(x_lat, lat_off, n, mod_lat[l], cos, sin, w_in_bf, rope=True)
        pc = _proj(x_ctx, 0, lctx, mod_ctx[l], cos, sin, w_in_bf, rope=False)
        o_a = _attn_a(p, pc, attn_sink[l], latent=True)
        o_b = _attn_b(p, pc, bias_tab, latent=True)
        o_c = _sgu(p, ln_g_c, ln_b_c, sgu_w_bf, sgu_b_b)
        if last:
            x1, h2 = _outproj(o_a, o_b, o_c, w_out_bf, x_lat, lat_off, mod_lat[l], g1, b1, n, 0, None)
            x_lat = _moe(h2, x1, jnp.stack([mod_ctx[l], mod_lat[l]]), 0, wr_t, rb_b,
                         w_gate, w_up, w_down, l, g2, b2)
        else:
            oc_a = _attn_a(pc, pc, attn_sink[l], latent=False)
            oc_b = _attn_b(pc, pc, bias_tab, latent=False)
            oc_c = _sgu(pc, ln_g_c, ln_b_c, sgu_w_bf, sgu_b_b)
            total = lctx + n
            prev = _outproj(oc_a, oc_b, oc_c, w_out_bf, x_ctx, 0, mod_ctx[l], g1, b1, total, 0, None)
            x1, h2 = _outproj(o_a, o_b, o_c, w_out_bf, x_lat, lat_off, mod_lat[l], g1, b1, total,
                              n_ctx_blocks, prev)
            x_all = _moe(h2, x1, jnp.stack([mod_ctx[l], mod_lat[l]]), n_ctx_blocks, wr_t, rb_b,
                         w_gate, w_up, w_down, l, g2, b2)
            x_lat, lat_off, x_ctx = x_all, n_ctx_blocks, x_all
    return x_lat.reshape(batch, n, d)
```

```python
import functools

import numpy as np
import jax
import jax.numpy as jnp
from jax import lax
from jax.experimental import pallas as pl
from jax.experimental.pallas import tpu as pltpu

F32 = jnp.float32
BF16 = jnp.bfloat16

D_MODEL = 2048
DEPTH = 2
GRID_W = 64
HEAD_DIM = 128
A_HEADS = 6
A_KV_HEADS = 2
A_GROUP = A_HEADS // A_KV_HEADS
A_BLOCK = 128
B_HEADS = 6
NA_ROWS = 8
NA_COLS = 16
C_GROUPS = 4
C_W = C_GROUPS * HEAD_DIM
CHUNK = 128
N_EXPERTS = 32
N_EXPERT_GROUPS = 4
EXPERTS_PER_GROUP = N_EXPERTS // N_EXPERT_GROUPS
TOP_K = 2
EXPERT_FF = 512
ROPE_BASE = 10000.0
LN_EPS = 1e-5
NEG_INF = -1e30
DEEPNORM_ALPHA = (2 * DEPTH) ** 0.25
ATTN_SCALE = HEAD_DIM ** -0.5

A_Q_W = A_HEADS * HEAD_DIM
A_KV_W = A_KV_HEADS * HEAD_DIM
B_W = B_HEADS * HEAD_DIM
OFF_AK = A_Q_W
OFF_AV = OFF_AK + A_KV_W
OFF_BQ = OFF_AV + A_KV_W
OFF_BK = OFF_BQ + B_W
OFF_BV = OFF_BK + B_W
OFF_C = OFF_BV + B_W
IN_COLS = OFF_C + 2 * C_W

VMEM_LIMIT_BYTES = 56 * 1024 * 1024
LANES = 128

ROW_TILE = 256
MATMUL_ROW_TILE = 512
PROJ_COL_TILE = 512
MOD_COL_TILE = 1024
MOE_TILE = 256

MOD_SH1, MOD_SC1, MOD_G1, MOD_SH2, MOD_SC2, MOD_G2 = range(6)


def _params(*sem):
    return pltpu.CompilerParams(dimension_semantics=sem, vmem_limit_bytes=VMEM_LIMIT_BYTES)


def _layer_norm(t, g, b):
    mu = jnp.mean(t, axis=-1, keepdims=True)
    d = t - mu
    var = jnp.mean(d * d, axis=-1, keepdims=True)
    return d * lax.rsqrt(var + LN_EPS) * g + b


def _sigmoid(v):
    return 1.0 / (1.0 + jnp.exp(-v))


def _dot_nt(a, b):
    return lax.dot_general(a, b, (((1,), (1,)), ((), ())), preferred_element_type=F32)


def _mod_kernel(c_ref, w_ref, b_ref, o_ref):
    w = w_ref[...]
    reps = w.shape[1] // LANES
    rows = []
    for r in range(2):
        cv = c_ref[r]
        s = cv * _sigmoid(cv)
        sb = jnp.concatenate([s] * reps, axis=1)
        rows.append(jnp.sum(w * sb, axis=0, keepdims=True) + b_ref[...])
    rows.append(jnp.zeros((6, w.shape[1]), F32))
    o_ref[...] = jnp.concatenate(rows, axis=0)


def _modulation(c, c_ctx, w_mod, b_mod):
    d = D_MODEL
    cb = jnp.stack([jnp.broadcast_to(c.reshape(d, 1), (d, LANES)),
                    jnp.broadcast_to(c_ctx.reshape(d, 1), (d, LANES))])
    n_out = 6 * d
    return pl.pallas_call(
        _mod_kernel,
        grid=(DEPTH, n_out // MOD_COL_TILE),
        in_specs=[pl.BlockSpec((2, d, LANES), lambda l, j: (0, 0, 0)),
                  pl.BlockSpec((None, d, MOD_COL_TILE), lambda l, j: (l, 0, j)),
                  pl.BlockSpec((None, 1, MOD_COL_TILE), lambda l, j: (l, 0, j))],
        out_specs=pl.BlockSpec((None, 8, MOD_COL_TILE), lambda l, j: (l, 0, j)),
        out_shape=jax.ShapeDtypeStruct((DEPTH, 8, n_out), F32),
        compiler_params=_params("arbitrary", "arbitrary"),
        name="modulation",
    )(cb, w_mod, b_mod.reshape(DEPTH, 1, n_out))


def _gelu_tanh(v):
    return 0.5 * v * (1.0 + jnp.tanh(np.sqrt(2.0 / np.pi).astype(np.float32) * (v + 0.044715 * (v * v * v))))


def _proj_kernel(x_ref, mod_ref, cos_ref, sin_ref, w_ref, o_ref, *, rope):
    x = x_ref[...]
    h = (x * (1.0 + mod_ref[MOD_SC1:MOD_SC1 + 1, :]) + mod_ref[MOD_SH1:MOD_SH1 + 1, :]).astype(BF16)
    tn = PROJ_COL_TILE
    for j in range(IN_COLS // tn):
        c0 = j * tn
        acc = jnp.dot(h, w_ref[:, c0:c0 + tn], preferred_element_type=F32)
        if c0 < OFF_AV:
            if rope:
                cos = cos_ref[...]
                sin = sin_ref[...]
                parts = []
                for hh in range(tn // HEAD_DIM):
                    a = acc[:, hh * HEAD_DIM:(hh + 1) * HEAD_DIM]
                    parts.append(a * cos + pltpu.roll(a, HEAD_DIM // 2, 1) * sin)
                acc = jnp.concatenate(parts, axis=1)
        elif c0 >= OFF_C:
            acc = _gelu_tanh(acc)
        o_ref[:, c0:c0 + tn] = acc.astype(BF16)


def _matmul_row_tile(*row_counts):
    return MATMUL_ROW_TILE if all(r % MATMUL_ROW_TILE == 0 for r in row_counts) else ROW_TILE


def _proj(x, x_row0, rows, mod, cos, sin, w_bf, rope):
    d = D_MODEL
    tm = _matmul_row_tile(rows, x_row0)
    x_off = x_row0 // tm
    return pl.pallas_call(
        functools.partial(_proj_kernel, rope=rope),
        grid=(rows // tm,),
        in_specs=[pl.BlockSpec((tm, d), lambda i: (i + x_off, 0)),
                  pl.BlockSpec((8, d), lambda i: (0, 0)),
                  pl.BlockSpec((tm, HEAD_DIM), lambda i: (i, 0)),
                  pl.BlockSpec((tm, HEAD_DIM), lambda i: (i, 0)),
                  pl.BlockSpec((d, IN_COLS), lambda i: (0, 0), pipeline_mode=pl.Buffered(1))],
        out_specs=pl.BlockSpec((tm, IN_COLS), lambda i: (i, 0)),
        out_shape=jax.ShapeDtypeStruct((rows, IN_COLS), BF16),
        compiler_params=_params("arbitrary"),
        name="proj_rope" if rope else "proj_ctx",
    )(x, mod, cos, sin, w_bf)


LOG2E = 1.4426950408889634
LOGIT_SCALE = ATTN_SCALE * LOG2E


def _softmax_pv(s_parts, v_parts, sink):
    m = s_parts[0].max(axis=-1, keepdims=True)
    for s in s_parts[1:]:
        m = jnp.maximum(m, s.max(axis=-1, keepdims=True))
    if sink is not None:
        m = jnp.maximum(m, sink)
    acc = None
    for s, v in zip(s_parts, v_parts):
        e = jnp.exp2(s - m).astype(BF16)
        v1 = jnp.concatenate([v, jnp.ones((v.shape[0], LANES), BF16)], axis=1)
        pv = jnp.dot(e, v1, preferred_element_type=F32)
        acc = pv if acc is None else acc + pv
    denom = acc[:, HEAD_DIM:HEAD_DIM + 1]
    if sink is not None:
        denom = denom + jnp.exp2(sink - m)
    return acc[:, :HEAD_DIM] / denom


def _attn_a_kernel(sink_ref, q_ref, kp_ref, kc_ref, kn_ref, vp_ref, vc_ref, vn_ref, kx_ref, vx_ref, o_ref,
                   mask_ref, *, latent):
    i = pl.program_id(0)
    nb = pl.num_programs(0)
    nq = A_GROUP * A_BLOCK

    if latent:
        @pl.when(i == 0)
        def _():
            qi = lax.broadcasted_iota(jnp.int32, (nq, 3 * A_BLOCK), 0) % A_BLOCK
            jj = lax.broadcasted_iota(jnp.int32, (nq, 3 * A_BLOCK), 1)
            ok = (jj >= qi) & (jj <= qi + 2 * A_BLOCK)
            mask_ref[0] = jnp.where(ok & (jj >= A_BLOCK), 0.0, NEG_INF).astype(F32)
            mask_ref[1] = jnp.where(ok, 0.0, NEG_INF).astype(F32)
            mask_ref[2] = jnp.where(ok & (jj < 2 * A_BLOCK), 0.0, NEG_INF).astype(F32)

        which = jnp.where(i == 0, 0, jnp.where(i == nb - 1, 2, 1))

    for kh in range(A_KV_HEADS):
        hs = [kh * A_GROUP + g for g in range(A_GROUP)]
        q = jnp.concatenate([q_ref[:, h * HEAD_DIM:(h + 1) * HEAD_DIM] for h in hs], axis=0)
        sink = jnp.concatenate([jnp.full((A_BLOCK, 1), sink_ref[h] * LOG2E, F32) for h in hs], axis=0)
        ks = slice(kh * HEAD_DIM, (kh + 1) * HEAD_DIM)
        s_parts = [_dot_nt(q, kx_ref[:, ks]) * LOGIT_SCALE]
        v_parts = [vx_ref[:, ks]]
        if latent:
            kband = jnp.concatenate([kp_ref[:, ks], kc_ref[:, ks], kn_ref[:, ks]], axis=0)
            vband = jnp.concatenate([vp_ref[:, ks], vc_ref[:, ks], vn_ref[:, ks]], axis=0)
            s_parts.append(_dot_nt(q, kband) * LOGIT_SCALE + mask_ref[which])
            v_parts.append(vband)
        out = _softmax_pv(s_parts, v_parts, sink)
        for g, h in enumerate(hs):
            o_ref[:, h * HEAD_DIM:(h + 1) * HEAD_DIM] = out[g * A_BLOCK:(g + 1) * A_BLOCK].astype(BF16)


def _attn_a(p, pc, sink, latent):
    rows = p.shape[0]
    nb = rows // A_BLOCK
    assert nb >= 2
    kcol = OFF_AK // A_KV_W
    vcol = OFF_AV // A_KV_W

    def band(col, shift):
        return pl.BlockSpec((A_BLOCK, A_KV_W), lambda i, s: (jnp.clip(i + shift, 0, nb - 1), col))

    lctx = pc.shape[0]
    return pl.pallas_call(
        functools.partial(_attn_a_kernel, latent=latent),
        grid_spec=pltpu.PrefetchScalarGridSpec(
            num_scalar_prefetch=1,
            grid=(nb,),
            in_specs=[pl.BlockSpec((A_BLOCK, A_Q_W), lambda i, s: (i, 0)),
                      band(kcol, -1), band(kcol, 0), band(kcol, 1),
                      band(vcol, -1), band(vcol, 0), band(vcol, 1),
                      pl.BlockSpec((lctx, A_KV_W), lambda i, s: (0, kcol)),
                      pl.BlockSpec((lctx, A_KV_W), lambda i, s: (0, vcol))],
            out_specs=pl.BlockSpec((A_BLOCK, A_Q_W), lambda i, s: (i, 0)),
            scratch_shapes=[pltpu.VMEM((3, A_GROUP * A_BLOCK, 3 * A_BLOCK), F32)]),
        out_shape=jax.ShapeDtypeStruct((rows, A_Q_W), BF16),
        compiler_params=_params("arbitrary"),
        name="attn_a_latent" if latent else "attn_a_ctx",
    )(sink, p, p, p, p, p, p, p, pc, pc)


B_PAIR_W = 2 * HEAD_DIM
NA_TILE = 256
NA_GROUP_ROWS = NA_TILE // GRID_W
NA_WIN_ROWS = NA_ROWS + NA_GROUP_ROWS
NA_PAIRS = NA_WIN_ROWS // 2
NA_BIAS_OFFS = 2 * NA_ROWS
NA_BOTH, NA_LEFT, NA_RIGHT = range(3)


def _attn_b_kernel(q_ref, k_ref, v_ref, kx_ref, vx_ref, bias_ref, o_ref, *, latent, grid_rows):
    g = pl.program_id(1)
    if latent:
        r_base = g * NA_GROUP_ROWS
        w0 = jnp.clip(r_base - NA_ROWS // 2, 0, grid_rows - NA_WIN_ROWS)
        start = pl.multiple_of(w0 * GRID_W, GRID_W)
    for hh in range(2):
        hs = slice(hh * HEAD_DIM, (hh + 1) * HEAD_DIM)
        q = q_ref[:, hs]
        s_parts = [_dot_nt(q, kx_ref[:, hs]) * LOGIT_SCALE]
        v_parts = [vx_ref[:, hs]]
        if latent:
            kwin = k_ref[pl.ds(start, NA_WIN_ROWS * GRID_W), hs]
            vwin = v_ref[pl.ds(start, NA_WIN_ROWS * GRID_W), hs]
            bias_rows = []
            for rr in range(NA_GROUP_ROWS):
                r = r_base + rr
                r0 = jnp.clip(r - NA_ROWS // 2, 0, grid_rows - NA_ROWS)
                tiles = []
                for jp in range(NA_PAIRS):
                    ka = w0 + 2 * jp
                    in_a = (ka >= r0) & (ka < r0 + NA_ROWS)
                    in_b = (ka + 1 >= r0) & (ka + 1 < r0 + NA_ROWS)
                    variant = jnp.where(in_a, jnp.where(in_b, NA_BOTH, NA_LEFT), jnp.where(in_b, NA_RIGHT, NA_LEFT))
                    off = jnp.where(in_a | in_b, jnp.clip(ka - r + NA_ROWS, 0, NA_BIAS_OFFS - 1), 0)
                    tiles.append(bias_ref[hh, variant, off])
                bias_rows.append(jnp.concatenate(tiles, axis=1))
            bias = jnp.concatenate(bias_rows, axis=0)
            s_parts.append(_dot_nt(q, kwin) * LOGIT_SCALE + bias)
            v_parts.append(vwin)
        out = _softmax_pv(s_parts, v_parts, None)
        o_ref[:, hs] = out.astype(BF16)


def _na_bias_table(rpb):
    cols = np.arange(GRID_W)
    c0 = np.clip(cols - NA_COLS // 2, 0, GRID_W - NA_COLS)
    rel = cols[None, :] - cols[:, None] + NA_COLS - 1
    ok = (cols[None, :] >= c0[:, None]) & (cols[None, :] < c0[:, None] + NA_COLS)
    onehot = (rel[None] == np.arange(2 * NA_COLS - 1)[:, None, None]).astype(np.float32)
    t = jnp.einsum("hrd,dqk->hrqk", rpb, onehot, precision=lax.Precision.HIGHEST)
    t = jnp.where(ok[None, None], t, NEG_INF).astype(F32)
    t = t * LOG2E
    t = jnp.pad(t, ((0, 0), (1, 1), (0, 0), (0, 0)), constant_values=NEG_INF)
    first, second = t[:, :-1], t[:, 1:]
    masked = jnp.full_like(first, NEG_INF)
    return jnp.stack([jnp.concatenate([first, second], axis=-1),
                      jnp.concatenate([first, masked], axis=-1),
                      jnp.concatenate([masked, second], axis=-1)], axis=1)


def _attn_b(p, pc, bias_tab, latent):
    rows = p.shape[0]
    lctx = pc.shape[0]
    assert not latent or (rows // GRID_W >= NA_WIN_ROWS and rows % NA_TILE == 0)
    tile = NA_TILE if latent else ROW_TILE
    qcol = OFF_BQ // B_PAIR_W
    kcol = OFF_BK // B_PAIR_W
    vcol = OFF_BV // B_PAIR_W
    return pl.pallas_call(
        functools.partial(_attn_b_kernel, latent=latent, grid_rows=rows // GRID_W),
        grid=(B_HEADS // 2, rows // tile),
        in_specs=[pl.BlockSpec((tile, B_PAIR_W), lambda hp, g: (g, qcol + hp)),
                  pl.BlockSpec((rows, B_PAIR_W), lambda hp, g: (0, kcol + hp)),
                  pl.BlockSpec((rows, B_PAIR_W), lambda hp, g: (0, vcol + hp)),
                  pl.BlockSpec((lctx, B_PAIR_W), lambda hp, g: (0, kcol + hp)),
                  pl.BlockSpec((lctx, B_PAIR_W), lambda hp, g: (0, vcol + hp)),
                  pl.BlockSpec((2, 3, NA_BIAS_OFFS, GRID_W, 2 * GRID_W), lambda hp, g: (hp, 0, 0, 0, 0))],
        out_specs=pl.BlockSpec((tile, B_PAIR_W), lambda hp, g: (g, hp)),
        out_shape=jax.ShapeDtypeStruct((rows, B_W), BF16),
        compiler_params=_params("arbitrary", "arbitrary"),
        name="attn_b_latent" if latent else "attn_b_ctx",
    )(p, p, p, pc, pc, bias_tab)


def _sgu_kernel(u_ref, v_ref, g_ref, b_ref, w_ref, bs_ref, o_ref):
    for ch in range(ROW_TILE // CHUNK):
        rs = slice(ch * CHUNK, (ch + 1) * CHUNK)
        for grp in range(C_GROUPS):
            cs = slice(grp * HEAD_DIM, (grp + 1) * HEAD_DIM)
            vn = _layer_norm(v_ref[rs, cs].astype(F32), g_ref[:, cs], b_ref[:, cs])
            mixed = jnp.dot(w_ref[grp], vn.astype(BF16), preferred_element_type=F32) + bs_ref[grp]
            o_ref[rs, cs] = (u_ref[rs, cs].astype(F32) * mixed).astype(BF16)


def _sgu(p, ln_g, ln_b, w_bf, bs_b):
    rows = p.shape[0]
    ucol = OFF_C // C_W
    return pl.pallas_call(
        _sgu_kernel,
        grid=(rows // ROW_TILE,),
        in_specs=[pl.BlockSpec((ROW_TILE, C_W), lambda i: (i, ucol)),
                  pl.BlockSpec((ROW_TILE, C_W), lambda i: (i, ucol + 1)),
                  pl.BlockSpec((1, C_W), lambda i: (0, 0)),
                  pl.BlockSpec((1, C_W), lambda i: (0, 0)),
                  pl.BlockSpec((C_GROUPS, CHUNK, CHUNK), lambda i: (0, 0, 0)),
                  pl.BlockSpec((C_GROUPS, CHUNK, LANES), lambda i: (0, 0, 0))],
        out_specs=pl.BlockSpec((ROW_TILE, C_W), lambda i: (i, 0)),
        out_shape=jax.ShapeDtypeStruct((rows, C_W), BF16),
        compiler_params=_params("arbitrary"),
        name="sgu",
    )(p, p, ln_g, ln_b, w_bf, bs_b)


def _outproj_kernel(oa_ref, ob_ref, oc_ref, w_ref, x_ref, mod_ref, g_ref, b_ref, *rest):
    x1_ref, h2_ref = rest[-2], rest[-1]
    mix = jnp.dot(oa_ref[...], w_ref[0:A_Q_W, :], preferred_element_type=F32)
    mix += jnp.dot(ob_ref[...], w_ref[A_Q_W:A_Q_W + B_W, :], preferred_element_type=F32)
    mix += jnp.dot(oc_ref[...], w_ref[A_Q_W + B_W:, :], preferred_element_type=F32)
    t = DEEPNORM_ALPHA * x_ref[...] + mod_ref[MOD_G1:MOD_G1 + 1, :] * mix
    x1 = _layer_norm(t, g_ref[...], b_ref[...])
    x1_ref[...] = x1
    h2_ref[...] = x1 * (1.0 + mod_ref[MOD_SC2:MOD_SC2 + 1, :]) + mod_ref[MOD_SH2:MOD_SH2 + 1, :]


def _outproj(o_a, o_b, o_c, w_bf, x, x_row0, mod, ln_g, ln_b, total_rows, out_row0, prev):
    rows = o_a.shape[0]
    d = D_MODEL
    tm = _matmul_row_tile(rows, x_row0, out_row0)
    x_off, out_off = x_row0 // tm, out_row0 // tm
    in_specs = [pl.BlockSpec((tm, A_Q_W), lambda i: (i, 0)),
                pl.BlockSpec((tm, B_W), lambda i: (i, 0)),
                pl.BlockSpec((tm, C_W), lambda i: (i, 0)),
                pl.BlockSpec((d, d), lambda i: (0, 0), pipeline_mode=pl.Buffered(1)),
                pl.BlockSpec((tm, d), lambda i: (i + x_off, 0)),
                pl.BlockSpec((8, d), lambda i: (0, 0)),
                pl.BlockSpec((1, d), lambda i: (0, 0)),
                pl.BlockSpec((1, d), lambda i: (0, 0))]
    args = [o_a, o_b, o_c, w_bf, x, mod, ln_g, ln_b]
    aliases = {}
    if prev is not None:
        in_specs += [pl.BlockSpec(memory_space=pl.ANY), pl.BlockSpec(memory_space=pl.ANY)]
        aliases = {len(args): 0, len(args) + 1: 1}
        args += list(prev)
    return pl.pallas_call(
        _outproj_kernel,
        grid=(rows // tm,),
        in_specs=in_specs,
        out_specs=[pl.BlockSpec((tm, d), lambda i: (i + out_off, 0))] * 2,
        out_shape=[jax.ShapeDtypeStruct((total_rows, d), F32)] * 2,
        input_output_aliases=aliases,
        compiler_params=_params("arbitrary"),
        name="outproj",
    )(*args)


def _top2_sublanes(vals, sub):
    m1 = vals.max(axis=0, keepdims=True)
    i1 = jnp.where(vals == m1, sub, vals.shape[0]).min(axis=0, keepdims=True)
    rest = jnp.where(sub == i1, -jnp.inf, vals)
    m2 = rest.max(axis=0, keepdims=True)
    i2 = jnp.where(rest == m2, sub, vals.shape[0]).min(axis=0, keepdims=True)
    return m1, i1, m2, i2


def _router_kernel(h_ref, wr_ref, rb_ref, e_ref, w_ref, rank_ref, cnt_ref, run_ref):
    i = pl.program_id(0)
    tm = ROW_TILE
    epg = EXPERTS_PER_GROUP

    @pl.when(i == 0)
    def _():
        run_ref[...] = jnp.zeros_like(run_ref)

    h = h_ref[...]
    h_hi = h.astype(BF16)
    h_mid = (h - h_hi.astype(F32)).astype(BF16)
    parts = _dot_nt(wr_ref[...], h_hi)
    parts_mid = _dot_nt(wr_ref[0:2 * N_EXPERTS, :], h_mid)
    logits = (parts[0:N_EXPERTS] + parts[N_EXPERTS:2 * N_EXPERTS] + parts[2 * N_EXPERTS:]
              + parts_mid[0:N_EXPERTS] + parts_mid[N_EXPERTS:])
    scores = _sigmoid(logits)
    biased = scores + jnp.concatenate([rb_ref[...]] * (tm // LANES), axis=1)
    sub = lax.broadcasted_iota(jnp.int32, (epg, tm), 0)

    best = None
    for g in range(N_EXPERT_GROUPS):
        m1, _, m2, _ = _top2_sublanes(biased[g * epg:(g + 1) * epg], sub)
        gs = m1 + m2
        if best is None:
            best, grp = gs, jnp.zeros((1, tm), jnp.int32)
            bsel, ssel = biased[0:epg], scores[0:epg]
        else:
            better = gs > best
            best = jnp.where(better, gs, best)
            grp = jnp.where(better, g, grp)
            bsel = jnp.where(better, biased[g * epg:(g + 1) * epg], bsel)
            ssel = jnp.where(better, scores[g * epg:(g + 1) * epg], ssel)
    _, i1, _, i2 = _top2_sublanes(bsel, sub)
    w1 = jnp.where(sub == i1, ssel, 0.0).sum(axis=0, keepdims=True)
    w2 = jnp.where(sub == i2, ssel, 0.0).sum(axis=0, keepdims=True)
    tot = w1 + w2
    e1 = grp * epg + i1
    e2 = grp * epg + i2

    eiota = lax.broadcasted_iota(jnp.int32, (N_EXPERTS, tm), 0)
    oh1 = (eiota == e1).astype(F32)
    oh2 = (eiota == e2).astype(F32)
    ohb = oh1 + oh2
    before = (lax.broadcasted_iota(jnp.int32, (tm, tm), 0) < lax.broadcasted_iota(jnp.int32, (tm, tm), 1))
    prefix = jnp.dot(ohb.astype(BF16), before.astype(BF16), preferred_element_type=F32)
    pos = run_ref[...] + prefix
    r1 = (oh1 * pos).sum(axis=0, keepdims=True)
    r2 = (oh2 * pos).sum(axis=0, keepdims=True)
    run_ref[...] = run_ref[...] + ohb.sum(axis=1, keepdims=True)

    e_ref[...] = jnp.concatenate([e1, e2], axis=0)
    w_ref[...] = jnp.concatenate([w1 / tot, w2 / tot], axis=0)
    rank_ref[...] = jnp.concatenate([r1, r2], axis=0).astype(jnp.int32)
    cnt_ref[...] = run_ref[:, 0:LANES]


def _router(h2, wr_t, rb_b):
    t = h2.shape[0]
    d = D_MODEL
    row2 = pl.BlockSpec((TOP_K, ROW_TILE), lambda i: (0, i))
    return pl.pallas_call(
        _router_kernel,
        grid=(t // ROW_TILE,),
        in_specs=[pl.BlockSpec((ROW_TILE, d), lambda i: (i, 0)),
                  pl.BlockSpec((3 * N_EXPERTS, d), lambda i: (0, 0)),
                  pl.BlockSpec((N_EXPERTS, LANES), lambda i: (0, 0))],
        out_specs=[row2, row2, row2, pl.BlockSpec((N_EXPERTS, LANES), lambda i: (0, 0))],
        out_shape=[jax.ShapeDtypeStruct((TOP_K, t), jnp.int32),
                   jax.ShapeDtypeStruct((TOP_K, t), F32),
                   jax.ShapeDtypeStruct((TOP_K, t), jnp.int32),
                   jax.ShapeDtypeStruct((N_EXPERTS, LANES), F32)],
        scratch_shapes=[pltpu.VMEM((N_EXPERTS, ROW_TILE), F32)],
        compiler_params=_params("arbitrary"),
        name="router",
    )(h2, wr_t, rb_b)


def _row_copy(src, row, dst, sem):
    return pltpu.make_async_copy(src.at[pl.ds(row, 1), :], dst, sem)


DISPATCH_SLOTS = 3


def _dispatch_kernel(slot_ref, h_hbm, xs_hbm, hbuf, lsem, ssem, *, n_tok):
    i = pl.program_id(0)
    nt = pl.num_programs(0)
    tm = ROW_TILE

    def load(blk):
        row0 = pl.multiple_of(blk * tm, tm)
        return pltpu.make_async_copy(h_hbm.at[pl.ds(row0, tm), :], hbuf.at[blk % DISPATCH_SLOTS],
                                     lsem.at[blk % DISPATCH_SLOTS])

    def wait_rows(par):
        for _ in range(TOP_K):
            pltpu.make_async_copy(hbuf.at[0], xs_hbm.at[pl.ds(0, tm), :], ssem.at[par]).wait()

    @pl.when(i == 0)
    def _():
        load(0).start()

    @pl.when(i + 1 < nt)
    def _():
        load(i + 1).start()

    load(i).wait()
    par = i % 2
    cur = hbuf.at[i % DISPATCH_SLOTS]
    for r in range(tm):
        for k in range(TOP_K):
            dst = xs_hbm.at[pl.ds(slot_ref[k * n_tok + i * tm + r], 1), :]
            pltpu.make_async_copy(cur.at[pl.ds(r, 1), :], dst, ssem.at[par]).start(priority=k)

    @pl.when(i > 0)
    def _():
        wait_rows(1 - par)

    @pl.when(i == nt - 1)
    def _():
        wait_rows(par)


def _dispatch(h2, slots, rows):
    t, d = h2.shape
    return pl.pallas_call(
        functools.partial(_dispatch_kernel, n_tok=t),
        grid_spec=pltpu.PrefetchScalarGridSpec(
            num_scalar_prefetch=1,
            grid=(t // ROW_TILE,),
            in_specs=[pl.BlockSpec(memory_space=pl.ANY)],
            out_specs=pl.BlockSpec(memory_space=pl.ANY),
            scratch_shapes=[pltpu.VMEM((DISPATCH_SLOTS, ROW_TILE, d), F32),
                            pltpu.SemaphoreType.DMA((DISPATCH_SLOTS,)),
                            pltpu.SemaphoreType.DMA((2,))]),
        out_shape=jax.ShapeDtypeStruct((rows, d), F32),
        compiler_params=_params("arbitrary"),
        name="dispatch",
    )(slots, h2)


def _experts_kernel(ts_ref, nu_ref, h_hbm, wg_hbm, wu_hbm, wd_hbm, ys_hbm,
                    xbuf, ybuf, gsem, ysem, wg_f, wu_f, wd_f, wsem, wg_bf, wu_bf, wd_bf, *, layer):
    e = pl.program_id(0)
    ne = pl.num_programs(0)
    n_used = nu_ref[0]
    tm = MOE_TILE
    tsub = tm

    def has_rows(ex):
        return ts_ref[ex + 1] > ts_ref[ex]

    def weight_copies(ex, wslot):
        return [pltpu.make_async_copy(src.at[layer, ex], dst.at[wslot], wsem.at[wslot])
                for src, dst in ((wg_hbm, wg_f), (wu_hbm, wu_f), (wd_hbm, wd_f))]

    def start_weights(ex, wslot):
        for cp in weight_copies(ex, wslot):
            cp.start(priority=1)

    def x_load(tile, slot):
        row0 = pl.multiple_of(tile * tsub, tsub)
        return pltpu.make_async_copy(h_hbm.at[pl.ds(row0, tsub), :], xbuf.at[slot], gsem.at[slot])

    def gather(tile, slot):
        x_load(tile, slot).start()

    def wait_gather(slot):
        x_load(0, slot).wait()

    def y_store(tile, slot):
        row0 = pl.multiple_of(tile * tsub, tsub)
        return pltpu.make_async_copy(ybuf.at[slot], ys_hbm.at[pl.ds(row0, tsub), :], ysem.at[slot])

    wslot = e % 2

    @pl.when((e == 0) & has_rows(0))
    def _():
        start_weights(0, 0)

    @pl.when((e == 0) & (n_used > 0))
    def _():
        gather(0, 0)

    nxt = jnp.minimum(e + 1, ne - 1)

    @pl.when((e + 1 < ne) & has_rows(nxt))
    def _():
        start_weights(nxt, 1 - wslot)

    @pl.when(has_rows(e))
    def _():
        for cp in weight_copies(e, wslot):
            cp.wait()
        wg_bf[...] = wg_f[wslot].astype(BF16)
        wu_bf[...] = wu_f[wslot].astype(BF16)
        wd_bf[...] = wd_f[wslot].astype(BF16)

    def tile_body(g, carry):
        slot = g % 2

        @pl.when(g >= 2)
        def _():
            y_store(g - 2, slot).wait()

        wait_gather(slot)
        x = xbuf[slot].astype(BF16)
        gather(jnp.minimum(g + 1, n_used - 1), 1 - slot)
        gate = jnp.dot(x, wg_bf[...], preferred_element_type=F32)
        up = jnp.dot(x, wu_bf[...], preferred_element_type=F32)
        hid = (gate * _sigmoid(gate)) * up
        ybuf[slot] = jnp.dot(hid.astype(BF16), wd_bf[...], preferred_element_type=F32)
        y_store(g, slot).start(priority=1)
        return carry

    lax.fori_loop(ts_ref[e], ts_ref[e + 1], tile_body, 0)

    @pl.when((e == pl.num_programs(0) - 1) & (n_used > 0))
    def _():
        wait_gather(n_used % 2)
        y_store(n_used - 1, (n_used - 1) % 2).wait()

        @pl.when(n_used >= 2)
        def _():
            y_store(n_used - 2, n_used % 2).wait()


def _experts(xs, tile_start, n_used, w_gate, w_up, w_down, layer):
    d = D_MODEL
    tm = MOE_TILE
    rows = xs.shape[0]

    hbm = pl.BlockSpec(memory_space=pl.ANY)
    return pl.pallas_call(
        functools.partial(_experts_kernel, layer=layer),
        grid_spec=pltpu.PrefetchScalarGridSpec(
            num_scalar_prefetch=2,
            grid=(N_EXPERTS,),
            in_specs=[hbm, hbm, hbm, hbm],
            out_specs=hbm,
            scratch_shapes=[pltpu.VMEM((2, tm, d), F32),
                            pltpu.VMEM((2, tm, d), F32),
                            pltpu.SemaphoreType.DMA((2,)),
                            pltpu.SemaphoreType.DMA((2,)),
                            pltpu.VMEM((2, d, EXPERT_FF), F32),
                            pltpu.VMEM((2, d, EXPERT_FF), F32),
                            pltpu.VMEM((2, EXPERT_FF, d), F32),
                            pltpu.SemaphoreType.DMA((2,)),
                            pltpu.VMEM((d, EXPERT_FF), BF16),
                            pltpu.VMEM((d, EXPERT_FF), BF16),
                            pltpu.VMEM((EXPERT_FF, d), BF16)]),
        out_shape=jax.ShapeDtypeStruct((rows, d), F32),
        compiler_params=_params("arbitrary"),
        name="experts",
    )(tile_start, n_used, xs, w_gate, w_up, w_down)


def _combine_kernel(slot_ref, ys_hbm, w_ref, x_ref, mod_ref, g_ref, b_ref, o_ref, ybuf, sem, *, n_tok):
    i = pl.program_id(0)
    nt = pl.num_programs(0)
    tm = ROW_TILE

    def issue(blk, buf):
        for r in range(tm):
            for k in range(TOP_K):
                row = slot_ref[k * n_tok + blk * tm + r]
                _row_copy(ys_hbm, row, ybuf.at[buf, k, pl.ds(r, 1), :], sem.at[buf]).start(priority=k)

    def wait(buf):
        for k in range(TOP_K):
            pltpu.make_async_copy(ys_hbm.at[pl.ds(0, tm), :], ybuf.at[buf, k], sem.at[buf]).wait()

    @pl.when(i == 0)
    def _():
        issue(0, 0)

    buf = i % 2
    wait(buf)
    issue(jnp.minimum(i + 1, nt - 1), 1 - buf)
    y = w_ref[:, 0:1] * ybuf[buf, 0] + w_ref[:, 1:2] * ybuf[buf, 1]
    t = DEEPNORM_ALPHA * x_ref[...] + mod_ref[MOD_G2:MOD_G2 + 1, :] * y
    o_ref[...] = _layer_norm(t, g_ref[...], b_ref[...])

    @pl.when(i == nt - 1)
    def _():
        wait(1 - buf)


def _combine(ys, slots, w_tok, x1, mods, n_lat_blocks, ln_g, ln_b):
    t = x1.shape[0]
    d = D_MODEL
    return pl.pallas_call(
        functools.partial(_combine_kernel, n_tok=t),
        grid_spec=pltpu.PrefetchScalarGridSpec(
            num_scalar_prefetch=1,
            grid=(t // ROW_TILE,),
            in_specs=[pl.BlockSpec(memory_space=pl.ANY),
                      pl.BlockSpec((ROW_TILE, TOP_K), lambda i, s: (i, 0)),
                      pl.BlockSpec((ROW_TILE, d), lambda i, s: (i, 0)),
                      pl.BlockSpec((None, 8, d), lambda i, s: (jnp.where(i < n_lat_blocks, 1, 0), 0, 0)),
                      pl.BlockSpec((1, d), lambda i, s: (0, 0)),
                      pl.BlockSpec((1, d), lambda i, s: (0, 0))],
            out_specs=pl.BlockSpec((ROW_TILE, d), lambda i, s: (i, 0)),
            scratch_shapes=[pltpu.VMEM((2, TOP_K, ROW_TILE, d), F32),
                            pltpu.SemaphoreType.DMA((2,))]),
        out_shape=jax.ShapeDtypeStruct((t, d), F32),
        compiler_params=_params("arbitrary"),
        name="combine",
    )(slots, ys, w_tok, x1, mods, ln_g, ln_b)


def _moe(h2, x1, mods, n_lat_blocks, wr_t, rb_b, w_gate, w_up, w_down, layer, ln_g, ln_b):
    t = x1.shape[0]
    tm = MOE_TILE
    e_idx, w_tok, rank, cnt = _router(h2, wr_t, rb_b)
    counts = cnt[:, 0].astype(jnp.int32)
    tiles_per = (counts + tm - 1) // tm
    tile_end = jnp.cumsum(tiles_per)
    n_used = tile_end[-1]
    row_off = (tile_end - tiles_per) * tm
    experts = jnp.arange(N_EXPERTS, dtype=jnp.int32)
    slots = jnp.sum(jnp.where(e_idx[:, :, None] == experts, row_off, 0), axis=-1) + rank
    max_tiles = (TOP_K * t + N_EXPERTS * (tm - 1)) // tm + 1
    tile_start = jnp.concatenate([tile_end - tiles_per, n_used.reshape(1)]).astype(jnp.int32)
    slots = slots.reshape(-1)
    xs = _dispatch(h2, slots, max_tiles * tm)
    ys = _experts(xs, tile_start, n_used.reshape(1).astype(jnp.int32), w_gate, w_up, w_down, layer)
    return _combine(ys, slots, w_tok.T, x1, mods, n_lat_blocks, ln_g, ln_b)


def _rope_tables(n):
    t = jnp.arange(n)
    row = (t // GRID_W).astype(F32)
    col = (t % GRID_W).astype(F32)
    n_freq = HEAD_DIM // 4
    inv_freq = ROPE_BASE ** (-jnp.arange(n_freq, dtype=F32) / n_freq)
    ang = jnp.concatenate([row[:, None] * inv_freq, col[:, None] * inv_freq], axis=-1)
    cos, sin = jnp.cos(ang), jnp.sin(ang)
    return jnp.concatenate([cos, cos], axis=-1), jnp.concatenate([-sin, sin], axis=-1)


def kernel(x, c, ctx, c_ctx, w_mod, b_mod, w_in, attn_sink, na_rpb, sgu_ln_g, sgu_ln_b, sgu_w, sgu_b,
           w_out, ln1_g, ln1_b, w_router, router_bias, w_gate, w_up, w_down, ln2_g, ln2_b):
    batch, n, d = x.shape
    lctx = ctx.shape[1]
    assert batch == 1 and d == D_MODEL and n % ROW_TILE == 0 and lctx % ROW_TILE == 0
    n_lat_blocks = n // ROW_TILE

    mods = _modulation(c, c_ctx, w_mod, b_mod).reshape(DEPTH, 8, 6, d)
    mod_lat = jnp.pad(mods[:, 0], ((0, 0), (0, 2), (0, 0)))
    mod_ctx = jnp.pad(mods[:, 1], ((0, 0), (0, 2), (0, 0)))
    cos, sin = _rope_tables(n)
    wr_f = w_router.T
    def bf16_part(v):
        return lax.bitcast_convert_type(lax.bitcast_convert_type(v, jnp.int32) & jnp.int32(-65536), F32)

    wr_hi = bf16_part(wr_f)
    wr_mid = bf16_part(wr_f - wr_hi)
    wr_lo = wr_f - wr_hi - wr_mid
    wr_t = jnp.concatenate([wr_hi, wr_mid, wr_lo], axis=0).astype(BF16)
    rb_b = jnp.broadcast_to(router_bias.reshape(N_EXPERTS, 1), (N_EXPERTS, LANES))

    x_lat = x[0]
    x_ctx, ctx_row0 = ctx[0], 0
    for l in range(DEPTH):
        last = l == DEPTH - 1
        w_in_bf = w_in[l].astype(BF16)
        w_out_bf = w_out[l].astype(BF16)
        sgu_w_bf = sgu_w[l].astype(BF16)
        sgu_b_b = jnp.broadcast_to(sgu_b[l][:, :, None], (C_GROUPS, CHUNK, LANES))
        ln_g_c, ln_b_c = sgu_ln_g[l].reshape(1, C_W), sgu_ln_b[l].reshape(1, C_W)
        bias_tab = _na_bias_table(na_rpb[l])
        g1, b1 = ln1_g[l].reshape(1, d), ln1_b[l].reshape(1, d)
        g2, b2 = ln2_g[l].reshape(1, d), ln2_b[l].reshape(1, d)

        p = _proj(x_lat, 0, n, mod_lat[l], cos, sin, w_in_bf, rope=True)
        pc = _proj(x_ctx, ctx_row0, lctx, mod_ctx[l], cos, sin, w_in_bf, rope=False)
        o_a = _attn_a(p, pc, attn_sink[l], latent=True)
        o_b = _attn_b(p, pc, bias_tab, latent=True)
        o_c = _sgu(p, ln_g_c, ln_b_c, sgu_w_bf, sgu_b_b)
        mod_both = jnp.stack([mod_ctx[l], mod_lat[l]])
        if last:
            x1, h2 = _outproj(o_a, o_b, o_c, w_out_bf, x_lat, 0, mod_lat[l], g1, b1, n, 0, None)
            x_lat = _moe(h2, x1, mod_both, n_lat_blocks, wr_t, rb_b, w_gate, w_up, w_down, l, g2, b2)
        else:
            oc_a = _attn_a(pc, pc, attn_sink[l], latent=False)
            oc_b = _attn_b(pc, pc, bias_tab, latent=False)
            oc_c = _sgu(pc, ln_g_c, ln_b_c, sgu_w_bf, sgu_b_b)
            total = n + lctx
            prev = _outproj(o_a, o_b, o_c, w_out_bf, x_lat, 0, mod_lat[l], g1, b1, total, 0, None)
            x1, h2 = _outproj(oc_a, oc_b, oc_c, w_out_bf, x_ctx, ctx_row0, mod_ctx[l], g1, b1, total, n, prev)
            x_all = _moe(h2, x1, mod_both, n_lat_blocks, wr_t, rb_b, w_gate, w_up, w_down, l, g2, b2)
            x_lat, x_ctx, ctx_row0 = x_all, x_all, n
    return x_lat.reshape(batch, n, d)
```

```python
import functools

import numpy as np
import jax
import jax.numpy as jnp
from jax import lax
from jax.experimental import pallas as pl
from jax.experimental.pallas import tpu as pltpu

F32 = jnp.float32
BF16 = jnp.bfloat16

D_MODEL = 2048
DEPTH = 2
GRID_W = 64
HEAD_DIM = 128
A_HEADS = 6
A_KV_HEADS = 2
A_GROUP = A_HEADS // A_KV_HEADS
A_BLOCK = 128
B_HEADS = 6
NA_ROWS = 8
NA_COLS = 16
C_GROUPS = 4
C_W = C_GROUPS * HEAD_DIM
CHUNK = 128
N_EXPERTS = 32
N_EXPERT_GROUPS = 4
EXPERTS_PER_GROUP = N_EXPERTS // N_EXPERT_GROUPS
TOP_K = 2
EXPERT_FF = 512
ROPE_BASE = 10000.0
LN_EPS = 1e-5
NEG_INF = -1e30
DEEPNORM_ALPHA = (2 * DEPTH) ** 0.25
ATTN_SCALE = HEAD_DIM ** -0.5

A_Q_W = A_HEADS * HEAD_DIM
A_KV_W = A_KV_HEADS * HEAD_DIM
B_W = B_HEADS * HEAD_DIM
OFF_AK = A_Q_W
OFF_AV = OFF_AK + A_KV_W
OFF_BQ = OFF_AV + A_KV_W
OFF_BK = OFF_BQ + B_W
OFF_BV = OFF_BK + B_W
OFF_C = OFF_BV + B_W
IN_COLS = OFF_C + 2 * C_W

VMEM_LIMIT_BYTES = 56 * 1024 * 1024
LANES = 128

ROW_TILE = 256
MATMUL_ROW_TILE = 512
PROJ_COL_TILE = 512
MOD_COL_TILE = 1024
MOE_TILE = 256

MOD_SH1, MOD_SC1, MOD_G1, MOD_SH2, MOD_SC2, MOD_G2 = range(6)


def _params(*sem):
    return pltpu.CompilerParams(dimension_semantics=sem, vmem_limit_bytes=VMEM_LIMIT_BYTES)


def _layer_norm(t, g, b):
    mu = jnp.mean(t, axis=-1, keepdims=True)
    d = t - mu
    var = jnp.mean(d * d, axis=-1, keepdims=True)
    return d * lax.rsqrt(var + LN_EPS) * g + b


def _sigmoid(v):
    return 1.0 / (1.0 + jnp.exp(-v))


def _dot_nt(a, b):
    return lax.dot_general(a, b, (((1,), (1,)), ((), ())), preferred_element_type=F32)


def _mod_kernel(c_ref, w_ref, b_ref, o_ref):
    w = w_ref[...]
    reps = w.shape[1] // LANES
    rows = []
    for r in range(2):
        cv = c_ref[r]
        s = cv * _sigmoid(cv)
        sb = jnp.concatenate([s] * reps, axis=1)
        rows.append(jnp.sum(w * sb, axis=0, keepdims=True) + b_ref[...])
    rows.append(jnp.zeros((6, w.shape[1]), F32))
    o_ref[...] = jnp.concatenate(rows, axis=0)


def _modulation(c, c_ctx, w_mod, b_mod):
    d = D_MODEL
    cb = jnp.stack([jnp.broadcast_to(c.reshape(d, 1), (d, LANES)),
                    jnp.broadcast_to(c_ctx.reshape(d, 1), (d, LANES))])
    n_out = 6 * d
    return pl.pallas_call(
        _mod_kernel,
        grid=(DEPTH, n_out // MOD_COL_TILE),
        in_specs=[pl.BlockSpec((2, d, LANES), lambda l, j: (0, 0, 0)),
                  pl.BlockSpec((None, d, MOD_COL_TILE), lambda l, j: (l, 0, j)),
                  pl.BlockSpec((None, 1, MOD_COL_TILE), lambda l, j: (l, 0, j))],
        out_specs=pl.BlockSpec((None, 8, MOD_COL_TILE), lambda l, j: (l, 0, j)),
        out_shape=jax.ShapeDtypeStruct((DEPTH, 8, n_out), F32),
        compiler_params=_params("arbitrary", "arbitrary"),
        name="modulation",
    )(cb, w_mod, b_mod.reshape(DEPTH, 1, n_out))


def _gelu_tanh(v):
    return 0.5 * v * (1.0 + jnp.tanh(np.sqrt(2.0 / np.pi).astype(np.float32) * (v + 0.044715 * (v * v * v))))


def _proj_kernel(x_ref, mod_ref, cos_ref, sin_ref, w_ref, o_ref, *, rope):
    x = x_ref[...]
    h = (x * (1.0 + mod_ref[MOD_SC1:MOD_SC1 + 1, :]) + mod_ref[MOD_SH1:MOD_SH1 + 1, :]).astype(BF16)
    tn = PROJ_COL_TILE
    for j in range(IN_COLS // tn):
        c0 = j * tn
        acc = jnp.dot(h, w_ref[:, c0:c0 + tn], preferred_element_type=F32)
        if c0 < OFF_AV:
            if rope:
                cos = cos_ref[...]
                sin = sin_ref[...]
                parts = []
                for hh in range(tn // HEAD_DIM):
                    a = acc[:, hh * HEAD_DIM:(hh + 1) * HEAD_DIM]
                    parts.append(a * cos + pltpu.roll(a, HEAD_DIM // 2, 1) * sin)
                acc = jnp.concatenate(parts, axis=1)
        elif c0 >= OFF_C:
            acc = _gelu_tanh(acc)
        o_ref[:, c0:c0 + tn] = acc.astype(BF16)


def _matmul_row_tile(*row_counts):
    return MATMUL_ROW_TILE if all(r % MATMUL_ROW_TILE == 0 for r in row_counts) else ROW_TILE


def _proj(x, x_row0, rows, mod, cos, sin, w_bf, layer, rope):
    d = D_MODEL
    tm = _matmul_row_tile(rows, x_row0)
    x_off = x_row0 // tm
    return pl.pallas_call(
        functools.partial(_proj_kernel, rope=rope),
        grid=(rows // tm,),
        in_specs=[pl.BlockSpec((tm, d), lambda i: (i + x_off, 0)),
                  pl.BlockSpec((8, d), lambda i: (0, 0)),
                  pl.BlockSpec((tm, HEAD_DIM), lambda i: (i, 0)),
                  pl.BlockSpec((tm, HEAD_DIM), lambda i: (i, 0)),
                  pl.BlockSpec((None, d, IN_COLS), lambda i: (layer, 0, 0), pipeline_mode=pl.Buffered(1))],
        out_specs=pl.BlockSpec((tm, IN_COLS), lambda i: (i, 0)),
        out_shape=jax.ShapeDtypeStruct((rows, IN_COLS), BF16),
        compiler_params=_params("arbitrary"),
        name="proj_rope" if rope else "proj_ctx",
    )(x, mod, cos, sin, w_bf)


LOG2E = 1.4426950408889634
LOGIT_SCALE = ATTN_SCALE * LOG2E


def _softmax_pv(s_parts, v_parts, sink):
    m = s_parts[0].max(axis=-1, keepdims=True)
    for s in s_parts[1:]:
        m = jnp.maximum(m, s.max(axis=-1, keepdims=True))
    if sink is not None:
        m = jnp.maximum(m, sink)
    acc = None
    for s, v in zip(s_parts, v_parts):
        e = jnp.exp2(s - m).astype(BF16)
        v1 = jnp.concatenate([v, jnp.ones((v.shape[0], LANES), BF16)], axis=1)
        pv = jnp.dot(e, v1, preferred_element_type=F32)
        acc = pv if acc is None else acc + pv
    denom = acc[:, HEAD_DIM:HEAD_DIM + 1]
    if sink is not None:
        denom = denom + jnp.exp2(sink - m)
    return acc[:, :HEAD_DIM] / denom


def _attn_a_kernel(sink_ref, q_ref, kp_ref, kc_ref, kn_ref, vp_ref, vc_ref, vn_ref, kx_ref, vx_ref, o_ref,
                   mask_ref, *, latent):
    i = pl.program_id(0)
    nb = pl.num_programs(0)
    nq = A_GROUP * A_BLOCK

    if latent:
        @pl.when(i == 0)
        def _():
            qi = lax.broadcasted_iota(jnp.int32, (nq, 3 * A_BLOCK), 0) % A_BLOCK
            jj = lax.broadcasted_iota(jnp.int32, (nq, 3 * A_BLOCK), 1)
            ok = (jj >= qi) & (jj <= qi + 2 * A_BLOCK)
            mask_ref[0] = jnp.where(ok & (jj >= A_BLOCK), 0.0, NEG_INF).astype(F32)
            mask_ref[1] = jnp.where(ok, 0.0, NEG_INF).astype(F32)
            mask_ref[2] = jnp.where(ok & (jj < 2 * A_BLOCK), 0.0, NEG_INF).astype(F32)

        which = jnp.where(i == 0, 0, jnp.where(i == nb - 1, 2, 1))

    for kh in range(A_KV_HEADS):
        hs = [kh * A_GROUP + g for g in range(A_GROUP)]
        q = jnp.concatenate([q_ref[:, h * HEAD_DIM:(h + 1) * HEAD_DIM] for h in hs], axis=0)
        sink = jnp.concatenate([jnp.full((A_BLOCK, 1), sink_ref[h] * LOG2E, F32) for h in hs], axis=0)
        ks = slice(kh * HEAD_DIM, (kh + 1) * HEAD_DIM)
        s_parts = [_dot_nt(q, kx_ref[:, ks]) * LOGIT_SCALE]
        v_parts = [vx_ref[:, ks]]
        if latent:
            kband = jnp.concatenate([kp_ref[:, ks], kc_ref[:, ks], kn_ref[:, ks]], axis=0)
            vband = jnp.concatenate([vp_ref[:, ks], vc_ref[:, ks], vn_ref[:, ks]], axis=0)
            s_parts.append(_dot_nt(q, kband) * LOGIT_SCALE + mask_ref[which])
            v_parts.append(vband)
        out = _softmax_pv(s_parts, v_parts, sink)
        for g, h in enumerate(hs):
            o_ref[:, h * HEAD_DIM:(h + 1) * HEAD_DIM] = out[g * A_BLOCK:(g + 1) * A_BLOCK].astype(BF16)


def _attn_a(p, pc, sink, latent):
    rows = p.shape[0]
    nb = rows // A_BLOCK
    assert nb >= 2
    kcol = OFF_AK // A_KV_W
    vcol = OFF_AV // A_KV_W

    def band(col, shift):
        return pl.BlockSpec((A_BLOCK, A_KV_W), lambda i, s: (jnp.clip(i + shift, 0, nb - 1), col))

    lctx = pc.shape[0]
    return pl.pallas_call(
        functools.partial(_attn_a_kernel, latent=latent),
        grid_spec=pltpu.PrefetchScalarGridSpec(
            num_scalar_prefetch=1,
            grid=(nb,),
            in_specs=[pl.BlockSpec((A_BLOCK, A_Q_W), lambda i, s: (i, 0)),
                      band(kcol, -1), band(kcol, 0), band(kcol, 1),
                      band(vcol, -1), band(vcol, 0), band(vcol, 1),
                      pl.BlockSpec((lctx, A_KV_W), lambda i, s: (0, kcol)),
                      pl.BlockSpec((lctx, A_KV_W), lambda i, s: (0, vcol))],
            out_specs=pl.BlockSpec((A_BLOCK, A_Q_W), lambda i, s: (i, 0)),
            scratch_shapes=[pltpu.VMEM((3, A_GROUP * A_BLOCK, 3 * A_BLOCK), F32)]),
        out_shape=jax.ShapeDtypeStruct((rows, A_Q_W), BF16),
        compiler_params=_params("arbitrary"),
        name="attn_a_latent" if latent else "attn_a_ctx",
    )(sink, p, p, p, p, p, p, p, pc, pc)


B_PAIR_W = 2 * HEAD_DIM
NA_TILE = 256
NA_GROUP_ROWS = NA_TILE // GRID_W
NA_WIN_ROWS = NA_ROWS + NA_GROUP_ROWS
NA_PAIRS = NA_WIN_ROWS // 2
NA_BIAS_OFFS = 2 * NA_ROWS
NA_BOTH, NA_LEFT, NA_RIGHT = range(3)


def _attn_b_kernel(q_ref, k_ref, v_ref, kx_ref, vx_ref, bias_ref, o_ref, *, latent, grid_rows):
    g = pl.program_id(1)
    if latent:
        r_base = g * NA_GROUP_ROWS
        w0 = jnp.clip(r_base - NA_ROWS // 2, 0, grid_rows - NA_WIN_ROWS)
        start = pl.multiple_of(w0 * GRID_W, GRID_W)
    for hh in range(2):
        hs = slice(hh * HEAD_DIM, (hh + 1) * HEAD_DIM)
        q = q_ref[:, hs]
        s_parts = [_dot_nt(q, kx_ref[:, hs]) * LOGIT_SCALE]
        v_parts = [vx_ref[:, hs]]
        if latent:
            kwin = k_ref[pl.ds(start, NA_WIN_ROWS * GRID_W), hs]
            vwin = v_ref[pl.ds(start, NA_WIN_ROWS * GRID_W), hs]
            bias_rows = []
            for rr in range(NA_GROUP_ROWS):
                r = r_base + rr
                r0 = jnp.clip(r - NA_ROWS // 2, 0, grid_rows - NA_ROWS)
                tiles = []
                for jp in range(NA_PAIRS):
                    ka = w0 + 2 * jp
                    in_a = (ka >= r0) & (ka < r0 + NA_ROWS)
                    in_b = (ka + 1 >= r0) & (ka + 1 < r0 + NA_ROWS)
                    variant = jnp.where(in_a, jnp.where(in_b, NA_BOTH, NA_LEFT), jnp.where(in_b, NA_RIGHT, NA_LEFT))
                    off = jnp.where(in_a | in_b, jnp.clip(ka - r + NA_ROWS, 0, NA_BIAS_OFFS - 1), 0)
                    tiles.append(bias_ref[hh, variant, off])
                bias_rows.append(jnp.concatenate(tiles, axis=1))
            bias = jnp.concatenate(bias_rows, axis=0)
            s_parts.append(_dot_nt(q, kwin) * LOGIT_SCALE + bias)
            v_parts.append(vwin)
        out = _softmax_pv(s_parts, v_parts, None)
        o_ref[:, hs] = out.astype(BF16)


def _na_bias_tables(rpb):
    cols = np.arange(GRID_W)
    c0 = np.clip(cols - NA_COLS // 2, 0, GRID_W - NA_COLS)
    rel = cols[None, :] - cols[:, None] + NA_COLS - 1
    ok = (cols[None, :] >= c0[:, None]) & (cols[None, :] < c0[:, None] + NA_COLS)
    onehot = (rel[None] == np.arange(2 * NA_COLS - 1)[:, None, None]).astype(np.float32)
    t = jnp.einsum("lhrd,dqk->lhrqk", rpb, onehot, precision=lax.Precision.HIGHEST)
    t = jnp.where(ok, t * LOG2E, NEG_INF).astype(F32)
    t = jnp.pad(t, ((0, 0), (0, 0), (1, 1), (0, 0), (0, 0)), constant_values=NEG_INF)
    pairs = jnp.concatenate([t[:, :, :-1], t[:, :, 1:]], axis=-1)
    keep = np.ones((3, 1, 1, 2 * GRID_W), bool)
    keep[NA_LEFT, :, :, GRID_W:] = False
    keep[NA_RIGHT, :, :, :GRID_W] = False
    return jnp.where(keep, pairs[:, :, None], NEG_INF)


def _attn_b(p, pc, bias_tabs, layer, latent):
    rows = p.shape[0]
    lctx = pc.shape[0]
    assert not latent or (rows // GRID_W >= NA_WIN_ROWS and rows % NA_TILE == 0)
    tile = NA_TILE if latent else ROW_TILE
    qcol = OFF_BQ // B_PAIR_W
    kcol = OFF_BK // B_PAIR_W
    vcol = OFF_BV // B_PAIR_W
    return pl.pallas_call(
        functools.partial(_attn_b_kernel, latent=latent, grid_rows=rows // GRID_W),
        grid=(B_HEADS // 2, rows // tile),
        in_specs=[pl.BlockSpec((tile, B_PAIR_W), lambda hp, g: (g, qcol + hp)),
                  pl.BlockSpec((rows, B_PAIR_W), lambda hp, g: (0, kcol + hp)),
                  pl.BlockSpec((rows, B_PAIR_W), lambda hp, g: (0, vcol + hp)),
                  pl.BlockSpec((lctx, B_PAIR_W), lambda hp, g: (0, kcol + hp)),
                  pl.BlockSpec((lctx, B_PAIR_W), lambda hp, g: (0, vcol + hp)),
                  pl.BlockSpec((None, 2, 3, NA_BIAS_OFFS, GRID_W, 2 * GRID_W),
                               lambda hp, g: (layer, hp, 0, 0, 0, 0))],
        out_specs=pl.BlockSpec((tile, B_PAIR_W), lambda hp, g: (g, hp)),
        out_shape=jax.ShapeDtypeStruct((rows, B_W), BF16),
        compiler_params=_params("arbitrary", "arbitrary"),
        name="attn_b_latent" if latent else "attn_b_ctx",
    )(p, p, p, pc, pc, bias_tabs)


def _sgu_kernel(u_ref, v_ref, g_ref, b_ref, w_ref, bs_ref, o_ref):
    for ch in range(ROW_TILE // CHUNK):
        rs = slice(ch * CHUNK, (ch + 1) * CHUNK)
        for grp in range(C_GROUPS):
            cs = slice(grp * HEAD_DIM, (grp + 1) * HEAD_DIM)
            vn = _layer_norm(v_ref[rs, cs].astype(F32), g_ref[:, cs], b_ref[:, cs])
            mixed = jnp.dot(w_ref[grp], vn.astype(BF16), preferred_element_type=F32) + bs_ref[grp]
            o_ref[rs, cs] = (u_ref[rs, cs].astype(F32) * mixed).astype(BF16)


def _sgu(p, ln_g, ln_b, w_bf, bs_b):
    rows = p.shape[0]
    ucol = OFF_C // C_W
    return pl.pallas_call(
        _sgu_kernel,
        grid=(rows // ROW_TILE,),
        in_specs=[pl.BlockSpec((ROW_TILE, C_W), lambda i: (i, ucol)),
                  pl.BlockSpec((ROW_TILE, C_W), lambda i: (i, ucol + 1)),
                  pl.BlockSpec((1, C_W), lambda i: (0, 0)),
                  pl.BlockSpec((1, C_W), lambda i: (0, 0)),
                  pl.BlockSpec((C_GROUPS, CHUNK, CHUNK), lambda i: (0, 0, 0)),
                  pl.BlockSpec((C_GROUPS, CHUNK, LANES), lambda i: (0, 0, 0))],
        out_specs=pl.BlockSpec((ROW_TILE, C_W), lambda i: (i, 0)),
        out_shape=jax.ShapeDtypeStruct((rows, C_W), BF16),
        compiler_params=_params("arbitrary"),
        name="sgu",
    )(p, p, ln_g, ln_b, w_bf, bs_b)


def _outproj_kernel(oa_ref, ob_ref, oc_ref, w_ref, x_ref, mod_ref, g_ref, b_ref, *rest):
    x1_ref, h2_ref = rest[-2], rest[-1]
    mix = jnp.dot(oa_ref[...], w_ref[0:A_Q_W, :], preferred_element_type=F32)
    mix += jnp.dot(ob_ref[...], w_ref[A_Q_W:A_Q_W + B_W, :], preferred_element_type=F32)
    mix += jnp.dot(oc_ref[...], w_ref[A_Q_W + B_W:, :], preferred_element_type=F32)
    t = DEEPNORM_ALPHA * x_ref[...] + mod_ref[MOD_G1:MOD_G1 + 1, :] * mix
    x1 = _layer_norm(t, g_ref[...], b_ref[...])
    x1_ref[...] = x1
    h2_ref[...] = x1 * (1.0 + mod_ref[MOD_SC2:MOD_SC2 + 1, :]) + mod_ref[MOD_SH2:MOD_SH2 + 1, :]


def _outproj(o_a, o_b, o_c, w_bf, layer, x, x_row0, mod, ln_g, ln_b, total_rows, out_row0, prev):
    rows = o_a.shape[0]
    d = D_MODEL
    tm = _matmul_row_tile(rows, x_row0, out_row0)
    x_off, out_off = x_row0 // tm, out_row0 // tm
    in_specs = [pl.BlockSpec((tm, A_Q_W), lambda i: (i, 0)),
                pl.BlockSpec((tm, B_W), lambda i: (i, 0)),
                pl.BlockSpec((tm, C_W), lambda i: (i, 0)),
                pl.BlockSpec((None, d, d), lambda i: (layer, 0, 0), pipeline_mode=pl.Buffered(1)),
                pl.BlockSpec((tm, d), lambda i: (i + x_off, 0)),
                pl.BlockSpec((8, d), lambda i: (0, 0)),
                pl.BlockSpec((1, d), lambda i: (0, 0)),
                pl.BlockSpec((1, d), lambda i: (0, 0))]
    args = [o_a, o_b, o_c, w_bf, x, mod, ln_g, ln_b]
    aliases = {}
    if prev is not None:
        in_specs += [pl.BlockSpec(memory_space=pl.ANY), pl.BlockSpec(memory_space=pl.ANY)]
        aliases = {len(args): 0, len(args) + 1: 1}
        args += list(prev)
    return pl.pallas_call(
        _outproj_kernel,
        grid=(rows // tm,),
        in_specs=in_specs,
        out_specs=[pl.BlockSpec((tm, d), lambda i: (i + out_off, 0))] * 2,
        out_shape=[jax.ShapeDtypeStruct((total_rows, d), F32)] * 2,
        input_output_aliases=aliases,
        compiler_params=_params("arbitrary"),
        name="outproj",
    )(*args)


def _top2_sublanes(vals, sub):
    m1 = vals.max(axis=0, keepdims=True)
    i1 = jnp.where(vals == m1, sub, vals.shape[0]).min(axis=0, keepdims=True)
    rest = jnp.where(sub == i1, -jnp.inf, vals)
    m2 = rest.max(axis=0, keepdims=True)
    i2 = jnp.where(rest == m2, sub, vals.shape[0]).min(axis=0, keepdims=True)
    return m1, i1, m2, i2


def _router_kernel(h_ref, wr_ref, rb_ref, e_ref, w_ref, rank_ref, cnt_ref, run_ref):
    i = pl.program_id(0)
    tm = ROW_TILE
    epg = EXPERTS_PER_GROUP

    @pl.when(i == 0)
    def _():
        run_ref[...] = jnp.zeros_like(run_ref)

    h = h_ref[...]
    h_hi = h.astype(BF16)
    h_mid = (h - h_hi.astype(F32)).astype(BF16)
    parts = _dot_nt(wr_ref[...], h_hi)
    parts_mid = _dot_nt(wr_ref[0:2 * N_EXPERTS, :], h_mid)
    logits = (parts[0:N_EXPERTS] + parts[N_EXPERTS:2 * N_EXPERTS] + parts[2 * N_EXPERTS:]
              + parts_mid[0:N_EXPERTS] + parts_mid[N_EXPERTS:])
    scores = _sigmoid(logits)
    biased = scores + jnp.concatenate([rb_ref[...]] * (tm // LANES), axis=1)
    sub = lax.broadcasted_iota(jnp.int32, (epg, tm), 0)

    best = None
    for g in range(N_EXPERT_GROUPS):
        m1, _, m2, _ = _top2_sublanes(biased[g * epg:(g + 1) * epg], sub)
        gs = m1 + m2
        if best is None:
            best, grp = gs, jnp.zeros((1, tm), jnp.int32)
            bsel, ssel = biased[0:epg], scores[0:epg]
        else:
            better = gs > best
            best = jnp.where(better, gs, best)
            grp = jnp.where(better, g, grp)
            bsel = jnp.where(better, biased[g * epg:(g + 1) * epg], bsel)
            ssel = jnp.where(better, scores[g * epg:(g + 1) * epg], ssel)
    _, i1, _, i2 = _top2_sublanes(bsel, sub)
    w1 = jnp.where(sub == i1, ssel, 0.0).sum(axis=0, keepdims=True)
    w2 = jnp.where(sub == i2, ssel, 0.0).sum(axis=0, keepdims=True)
    tot = w1 + w2
    e1 = grp * epg + i1
    e2 = grp * epg + i2

    eiota = lax.broadcasted_iota(jnp.int32, (N_EXPERTS, tm), 0)
    oh1 = (eiota == e1).astype(F32)
    oh2 = (eiota == e2).astype(F32)
    ohb = oh1 + oh2
    before = (lax.broadcasted_iota(jnp.int32, (tm, tm), 0) < lax.broadcasted_iota(jnp.int32, (tm, tm), 1))
    prefix = jnp.dot(ohb.astype(BF16), before.astype(BF16), preferred_element_type=F32)
    pos = run_ref[...] + prefix
    r1 = (oh1 * pos).sum(axis=0, keepdims=True)
    r2 = (oh2 * pos).sum(axis=0, keepdims=True)
    run_ref[...] = run_ref[...] + ohb.sum(axis=1, keepdims=True)

    e_ref[...] = jnp.concatenate([e1, e2], axis=0)
    w_ref[...] = jnp.concatenate([w1 / tot, w2 / tot], axis=0)
    rank_ref[...] = jnp.concatenate([r1, r2], axis=0).astype(jnp.int32)
    cnt_ref[...] = run_ref[:, 0:LANES]


def _router(h2, wr_t, rb_b):
    t = h2.shape[0]
    d = D_MODEL
    row2 = pl.BlockSpec((TOP_K, ROW_TILE), lambda i: (0, i))
    return pl.pallas_call(
        _router_kernel,
        grid=(t // ROW_TILE,),
        in_specs=[pl.BlockSpec((ROW_TILE, d), lambda i: (i, 0)),
                  pl.BlockSpec((3 * N_EXPERTS, d), lambda i: (0, 0)),
                  pl.BlockSpec((N_EXPERTS, LANES), lambda i: (0, 0))],
        out_specs=[row2, row2, row2, pl.BlockSpec((N_EXPERTS, LANES), lambda i: (0, 0))],
        out_shape=[jax.ShapeDtypeStruct((TOP_K, t), jnp.int32),
                   jax.ShapeDtypeStruct((TOP_K, t), F32),
                   jax.ShapeDtypeStruct((TOP_K, t), jnp.int32),
                   jax.ShapeDtypeStruct((N_EXPERTS, LANES), F32)],
        scratch_shapes=[pltpu.VMEM((N_EXPERTS, ROW_TILE), F32)],
        compiler_params=_params("arbitrary"),
        name="router",
    )(h2, wr_t, rb_b)


def _row_copy(src, row, dst, sem):
    return pltpu.make_async_copy(src.at[pl.ds(row, 1), :], dst, sem)


DISPATCH_SLOTS = 3


def _dispatch_kernel(slot_ref, h_hbm, xs_hbm, hbuf, lsem, ssem, *, n_tok):
    i = pl.program_id(0)
    nt = pl.num_programs(0)
    tm = ROW_TILE

    def load(blk):
        row0 = pl.multiple_of(blk * tm, tm)
        return pltpu.make_async_copy(h_hbm.at[pl.ds(row0, tm), :], hbuf.at[blk % DISPATCH_SLOTS],
                                     lsem.at[blk % DISPATCH_SLOTS])

    def wait_rows(par):
        for _ in range(TOP_K):
            pltpu.make_async_copy(hbuf.at[0], xs_hbm.at[pl.ds(0, tm), :], ssem.at[par]).wait()

    @pl.when(i == 0)
    def _():
        load(0).start()

    @pl.when(i + 1 < nt)
    def _():
        load(i + 1).start()

    load(i).wait()
    par = i % 2
    cur = hbuf.at[i % DISPATCH_SLOTS]
    for r in range(tm):
        for k in range(TOP_K):
            dst = xs_hbm.at[pl.ds(slot_ref[k * n_tok + i * tm + r], 1), :]
            pltpu.make_async_copy(cur.at[pl.ds(r, 1), :], dst, ssem.at[par]).start(priority=k)

    @pl.when(i > 0)
    def _():
        wait_rows(1 - par)

    @pl.when(i == nt - 1)
    def _():
        wait_rows(par)


def _dispatch(h2, slots, rows):
    t, d = h2.shape
    return pl.pallas_call(
        functools.partial(_dispatch_kernel, n_tok=t),
        grid_spec=pltpu.PrefetchScalarGridSpec(
            num_scalar_prefetch=1,
            grid=(t // ROW_TILE,),
            in_specs=[pl.BlockSpec(memory_space=pl.ANY)],
            out_specs=pl.BlockSpec(memory_space=pl.ANY),
            scratch_shapes=[pltpu.VMEM((DISPATCH_SLOTS, ROW_TILE, d), F32),
                            pltpu.SemaphoreType.DMA((DISPATCH_SLOTS,)),
                            pltpu.SemaphoreType.DMA((2,))]),
        out_shape=jax.ShapeDtypeStruct((rows, d), F32),
        compiler_params=_params("arbitrary"),
        name="dispatch",
    )(slots, h2)


def _experts_kernel(ts_ref, nu_ref, h_hbm, wg_hbm, wu_hbm, wd_hbm, ys_hbm,
                    xbuf, ybuf, gsem, ysem, wg_f, wu_f, wd_f, wsem, wg_bf, wu_bf, wd_bf, *, layer):
    e = pl.program_id(0)
    ne = pl.num_programs(0)
    n_used = nu_ref[0]
    tm = MOE_TILE
    tsub = tm

    def has_rows(ex):
        return ts_ref[ex + 1] > ts_ref[ex]

    def weight_copies(ex, wslot):
        return [pltpu.make_async_copy(src.at[layer, ex], dst.at[wslot], wsem.at[wslot])
                for src, dst in ((wg_hbm, wg_f), (wu_hbm, wu_f), (wd_hbm, wd_f))]

    def start_weights(ex, wslot):
        for cp in weight_copies(ex, wslot):
            cp.start(priority=1)

    def x_load(tile, slot):
        row0 = pl.multiple_of(tile * tsub, tsub)
        return pltpu.make_async_copy(h_hbm.at[pl.ds(row0, tsub), :], xbuf.at[slot], gsem.at[slot])

    def gather(tile, slot):
        x_load(tile, slot).start()

    def wait_gather(slot):
        x_load(0, slot).wait()

    def y_store(tile, slot):
        row0 = pl.multiple_of(tile * tsub, tsub)
        return pltpu.make_async_copy(ybuf.at[slot], ys_hbm.at[pl.ds(row0, tsub), :], ysem.at[slot])

    wslot = e % 2

    @pl.when((e == 0) & has_rows(0))
    def _():
        start_weights(0, 0)

    @pl.when((e == 0) & (n_used > 0))
    def _():
        gather(0, 0)

    nxt = jnp.minimum(e + 1, ne - 1)

    @pl.when((e + 1 < ne) & has_rows(nxt))
    def _():
        start_weights(nxt, 1 - wslot)

    @pl.when(has_rows(e))
    def _():
        for cp in weight_copies(e, wslot):
            cp.wait()
        wg_bf[...] = wg_f[wslot].astype(BF16)
        wu_bf[...] = wu_f[wslot].astype(BF16)
        wd_bf[...] = wd_f[wslot].astype(BF16)

    def tile_body(g, carry):
        slot = g % 2

        @pl.when(g >= 2)
        def _():
            y_store(g - 2, slot).wait()

        wait_gather(slot)
        x = xbuf[slot].astype(BF16)
        gather(jnp.minimum(g + 1, n_used - 1), 1 - slot)
        gate = jnp.dot(x, wg_bf[...], preferred_element_type=F32)
        up = jnp.dot(x, wu_bf[...], preferred_element_type=F32)
        hid = (gate * _sigmoid(gate)) * up
        ybuf[slot] = jnp.dot(hid.astype(BF16), wd_bf[...], preferred_element_type=F32)
        y_store(g, slot).start(priority=1)
        return carry

    lax.fori_loop(ts_ref[e], ts_ref[e + 1], tile_body, 0)

    @pl.when((e == pl.num_programs(0) - 1) & (n_used > 0))
    def _():
        wait_gather(n_used % 2)
        y_store(n_used - 1, (n_used - 1) % 2).wait()

        @pl.when(n_used >= 2)
        def _():
            y_store(n_used - 2, n_used % 2).wait()


def _experts(xs, tile_start, n_used, w_gate, w_up, w_down, layer):
    d = D_MODEL
    tm = MOE_TILE
    rows = xs.shape[0]

    hbm = pl.BlockSpec(memory_space=pl.ANY)
    return pl.pallas_call(
        functools.partial(_experts_kernel, layer=layer),
        grid_spec=pltpu.PrefetchScalarGridSpec(
            num_scalar_prefetch=2,
            grid=(N_EXPERTS,),
            in_specs=[hbm, hbm, hbm, hbm],
            out_specs=hbm,
            scratch_shapes=[pltpu.VMEM((2, tm, d), F32),
                            pltpu.VMEM((2, tm, d), F32),
                            pltpu.SemaphoreType.DMA((2,)),
                            pltpu.SemaphoreType.DMA((2,)),
                            pltpu.VMEM((2, d, EXPERT_FF), F32),
                            pltpu.VMEM((2, d, EXPERT_FF), F32),
                            pltpu.VMEM((2, EXPERT_FF, d), F32),
                            pltpu.SemaphoreType.DMA((2,)),
                            pltpu.VMEM((d, EXPERT_FF), BF16),
                            pltpu.VMEM((d, EXPERT_FF), BF16),
                            pltpu.VMEM((EXPERT_FF, d), BF16)]),
        out_shape=jax.ShapeDtypeStruct((rows, d), F32),
        compiler_params=_params("arbitrary"),
        name="experts",
    )(tile_start, n_used, xs, w_gate, w_up, w_down)


def _combine_kernel(slot_ref, ys_hbm, w_ref, x_ref, mod_ref, g_ref, b_ref, o_ref, ybuf, sem, *, n_tok):
    i = pl.program_id(0)
    nt = pl.num_programs(0)
    tm = ROW_TILE

    def issue(blk, buf):
        for r in range(tm):
            for k in range(TOP_K):
                row = slot_ref[k * n_tok + blk * tm + r]
                _row_copy(ys_hbm, row, ybuf.at[buf, k, pl.ds(r, 1), :], sem.at[buf]).start(priority=k)

    def wait(buf):
        for k in range(TOP_K):
            pltpu.make_async_copy(ys_hbm.at[pl.ds(0, tm), :], ybuf.at[buf, k], sem.at[buf]).wait()

    @pl.when(i == 0)
    def _():
        issue(0, 0)

    buf = i % 2
    wait(buf)
    issue(jnp.minimum(i + 1, nt - 1), 1 - buf)
    y = w_ref[:, 0:1] * ybuf[buf, 0] + w_ref[:, 1:2] * ybuf[buf, 1]
    t = DEEPNORM_ALPHA * x_ref[...] + mod_ref[MOD_G2:MOD_G2 + 1, :] * y
    o_ref[...] = _layer_norm(t, g_ref[...], b_ref[...])

    @pl.when(i == nt - 1)
    def _():
        wait(1 - buf)


def _combine(ys, slots, w_tok, x1, mods, n_lat_blocks, ln_g, ln_b):
    t = x1.shape[0]
    d = D_MODEL
    return pl.pallas_call(
        functools.partial(_combine_kernel, n_tok=t),
        grid_spec=pltpu.PrefetchScalarGridSpec(
            num_scalar_prefetch=1,
            grid=(t // ROW_TILE,),
            in_specs=[pl.BlockSpec(memory_space=pl.ANY),
                      pl.BlockSpec((ROW_TILE, TOP_K), lambda i, s: (i, 0)),
                      pl.BlockSpec((ROW_TILE, d), lambda i, s: (i, 0)),
                      pl.BlockSpec((None, 8, d), lambda i, s: (jnp.where(i < n_lat_blocks, 1, 0), 0, 0)),
                      pl.BlockSpec((1, d), lambda i, s: (0, 0)),
                      pl.BlockSpec((1, d), lambda i, s: (0, 0))],
            out_specs=pl.BlockSpec((ROW_TILE, d), lambda i, s: (i, 0)),
            scratch_shapes=[pltpu.VMEM((2, TOP_K, ROW_TILE, d), F32),
                            pltpu.SemaphoreType.DMA((2,))]),
        out_shape=jax.ShapeDtypeStruct((t, d), F32),
        compiler_params=_params("arbitrary"),
        name="combine",
    )(slots, ys, w_tok, x1, mods, ln_g, ln_b)


def _moe(h2, x1, mods, n_lat_blocks, wr_t, rb_b, w_gate, w_up, w_down, layer, ln_g, ln_b):
    t = x1.shape[0]
    tm = MOE_TILE
    e_idx, w_tok, rank, cnt = _router(h2, wr_t, rb_b)
    counts = cnt[:, 0].astype(jnp.int32)
    tiles_per = (counts + tm - 1) // tm
    tile_end = jnp.cumsum(tiles_per)
    n_used = tile_end[-1]
    row_off = (tile_end - tiles_per) * tm
    experts = jnp.arange(N_EXPERTS, dtype=jnp.int32)
    slots = jnp.sum(jnp.where(e_idx[:, :, None] == experts, row_off, 0), axis=-1) + rank
    max_tiles = (TOP_K * t + N_EXPERTS * (tm - 1)) // tm + 1
    tile_start = jnp.concatenate([tile_end - tiles_per, n_used.reshape(1)]).astype(jnp.int32)
    slots = slots.reshape(-1)
    xs = _dispatch(h2, slots, max_tiles * tm)
    ys = _experts(xs, tile_start, n_used.reshape(1).astype(jnp.int32), w_gate, w_up, w_down, layer)
    return _combine(ys, slots, w_tok.T, x1, mods, n_lat_blocks, ln_g, ln_b)


def _rope_tables(n):
    t = np.arange(n)
    row = (t // GRID_W).astype(np.float64)
    col = (t % GRID_W).astype(np.float64)
    n_freq = HEAD_DIM // 4
    inv_freq = ROPE_BASE ** (-np.arange(n_freq, dtype=np.float64) / n_freq)
    ang = np.concatenate([row[:, None] * inv_freq, col[:, None] * inv_freq], axis=-1)
    cos, sin = np.cos(ang), np.sin(ang)
    return (np.concatenate([cos, cos], axis=-1).astype(np.float32),
            np.concatenate([-sin, sin], axis=-1).astype(np.float32))


def kernel(x, c, ctx, c_ctx, w_mod, b_mod, w_in, attn_sink, na_rpb, sgu_ln_g, sgu_ln_b, sgu_w, sgu_b,
           w_out, ln1_g, ln1_b, w_router, router_bias, w_gate, w_up, w_down, ln2_g, ln2_b):
    batch, n, d = x.shape
    lctx = ctx.shape[1]
    assert batch == 1 and d == D_MODEL and n % ROW_TILE == 0 and lctx % ROW_TILE == 0
    n_lat_blocks = n // ROW_TILE

    mods = _modulation(c, c_ctx, w_mod, b_mod).reshape(DEPTH, 8, 6, d)
    mod_lat = jnp.pad(mods[:, 0], ((0, 0), (0, 2), (0, 0)))
    mod_ctx = jnp.pad(mods[:, 1], ((0, 0), (0, 2), (0, 0)))
    cos, sin = _rope_tables(n)
    wr_f = w_router.T
    def bf16_part(v):
        return lax.bitcast_convert_type(lax.bitcast_convert_type(v, jnp.int32) & jnp.int32(-65536), F32)

    wr_hi = bf16_part(wr_f)
    wr_mid = bf16_part(wr_f - wr_hi)
    wr_lo = wr_f - wr_hi - wr_mid
    wr_t = jnp.concatenate([wr_hi, wr_mid, wr_lo], axis=0).astype(BF16)
    rb_b = jnp.broadcast_to(router_bias.reshape(N_EXPERTS, 1), (N_EXPERTS, LANES))

    x_lat = x[0]
    x_ctx, ctx_row0 = ctx[0], 0
    w_in_bf = w_in.astype(BF16)
    w_out_bf = w_out.astype(BF16)
    bias_tabs = _na_bias_tables(na_rpb)
    for l in range(DEPTH):
        last = l == DEPTH - 1
        sgu_w_bf = sgu_w[l].astype(BF16)
        sgu_b_b = jnp.broadcast_to(sgu_b[l][:, :, None], (C_GROUPS, CHUNK, LANES))
        ln_g_c, ln_b_c = sgu_ln_g[l].reshape(1, C_W), sgu_ln_b[l].reshape(1, C_W)
        g1, b1 = ln1_g[l].reshape(1, d), ln1_b[l].reshape(1, d)
        g2, b2 = ln2_g[l].reshape(1, d), ln2_b[l].reshape(1, d)

        p = _proj(x_lat, 0, n, mod_lat[l], cos, sin, w_in_bf, l, rope=True)
        pc = _proj(x_ctx, ctx_row0, lctx, mod_ctx[l], cos, sin, w_in_bf, l, rope=False)
        o_a = _attn_a(p, pc, attn_sink[l], latent=True)
        o_b = _attn_b(p, pc, bias_tabs, l, latent=True)
        o_c = _sgu(p, ln_g_c, ln_b_c, sgu_w_bf, sgu_b_b)
        mod_both = jnp.stack([mod_ctx[l], mod_lat[l]])
        if last:
            x1, h2 = _outproj(o_a, o_b, o_c, w_out_bf, l, x_lat, 0, mod_lat[l], g1, b1, n, 0, None)
            x_lat = _moe(h2, x1, mod_both, n_lat_blocks, wr_t, rb_b, w_gate, w_up, w_down, l, g2, b2)
        else:
            oc_a = _attn_a(pc, pc, attn_sink[l], latent=False)
            oc_b = _attn_b(pc, pc, bias_tabs, l, latent=False)
            oc_c = _sgu(pc, ln_g_c, ln_b_c, sgu_w_bf, sgu_b_b)
            total = n + lctx
            prev = _outproj(o_a, o_b, o_c, w_out_bf, l, x_lat, 0, mod_lat[l], g1, b1, total, 0, None)
            x1, h2 = _outproj(oc_a, oc_b, oc_c, w_out_bf, l, x_ctx, ctx_row0, mod_ctx[l], g1, b1, total, n,
                              prev)
            x_all = _moe(h2, x1, mod_both, n_lat_blocks, wr_t, rb_b, w_gate, w_up, w_down, l, g2, b2)
            x_lat, x_ctx, ctx_row0 = x_all, x_all, n
    return x_lat.reshape(batch, n, d)
```

```python
import functools

import numpy as np
import jax
import jax.numpy as jnp
from jax import lax
from jax.experimental import pallas as pl
from jax.experimental.pallas import tpu as pltpu

F32 = jnp.float32
BF16 = jnp.bfloat16

D_MODEL = 2048
DEPTH = 2
GRID_W = 64
HEAD_DIM = 128
A_HEADS = 6
A_KV_HEADS = 2
A_GROUP = A_HEADS // A_KV_HEADS
A_BLOCK = 128
B_HEADS = 6
NA_ROWS = 8
NA_COLS = 16
C_GROUPS = 4
C_W = C_GROUPS * HEAD_DIM
CHUNK = 128
N_EXPERTS = 32
N_EXPERT_GROUPS = 4
EXPERTS_PER_GROUP = N_EXPERTS // N_EXPERT_GROUPS
TOP_K = 2
EXPERT_FF = 512
ROPE_BASE = 10000.0
LN_EPS = 1e-5
NEG_INF = -1e30
DEEPNORM_ALPHA = (2 * DEPTH) ** 0.25
ATTN_SCALE = HEAD_DIM ** -0.5

A_Q_W = A_HEADS * HEAD_DIM
A_KV_W = A_KV_HEADS * HEAD_DIM
B_W = B_HEADS * HEAD_DIM
OFF_AK = A_Q_W
OFF_AV = OFF_AK + A_KV_W
OFF_BQ = OFF_AV + A_KV_W
OFF_BK = OFF_BQ + B_W
OFF_BV = OFF_BK + B_W
OFF_C = OFF_BV + B_W
IN_COLS = OFF_C + 2 * C_W

VMEM_LIMIT_BYTES = 56 * 1024 * 1024
LANES = 128

ROW_TILE = 256
MATMUL_ROW_TILE = 512
PROJ_COL_TILE = 512
MOD_COL_TILE = 1024
MOE_TILE = 256

MOD_SH1, MOD_SC1, MOD_G1, MOD_SH2, MOD_SC2, MOD_G2 = range(6)


def _params(*sem):
    return pltpu.CompilerParams(dimension_semantics=sem, vmem_limit_bytes=VMEM_LIMIT_BYTES)


def _layer_norm(t, g, b):
    mu = jnp.mean(t, axis=-1, keepdims=True)
    d = t - mu
    var = jnp.mean(d * d, axis=-1, keepdims=True)
    return d * lax.rsqrt(var + LN_EPS) * g + b


def _sigmoid(v):
    return 1.0 / (1.0 + jnp.exp(-v))


def _dot_nt(a, b):
    return lax.dot_general(a, b, (((1,), (1,)), ((), ())), preferred_element_type=F32)


def _mod_kernel(c_ref, w_ref, b_ref, o_ref):
    w = w_ref[...]
    reps = w.shape[1] // LANES
    rows = []
    for r in range(2):
        cv = c_ref[r]
        s = cv * _sigmoid(cv)
        sb = jnp.concatenate([s] * reps, axis=1)
        rows.append(jnp.sum(w * sb, axis=0, keepdims=True) + b_ref[...])
    rows.append(jnp.zeros((6, w.shape[1]), F32))
    o_ref[...] = jnp.concatenate(rows, axis=0)


def _modulation(c, c_ctx, w_mod, b_mod):
    d = D_MODEL
    cb = jnp.stack([jnp.broadcast_to(c.reshape(d, 1), (d, LANES)),
                    jnp.broadcast_to(c_ctx.reshape(d, 1), (d, LANES))])
    n_out = 6 * d
    return pl.pallas_call(
        _mod_kernel,
        grid=(DEPTH, n_out // MOD_COL_TILE),
        in_specs=[pl.BlockSpec((2, d, LANES), lambda l, j: (0, 0, 0)),
                  pl.BlockSpec((None, d, MOD_COL_TILE), lambda l, j: (l, 0, j)),
                  pl.BlockSpec((None, 1, MOD_COL_TILE), lambda l, j: (l, 0, j))],
        out_specs=pl.BlockSpec((None, 8, MOD_COL_TILE), lambda l, j: (l, 0, j)),
        out_shape=jax.ShapeDtypeStruct((DEPTH, 8, n_out), F32),
        compiler_params=_params("arbitrary", "arbitrary"),
        name="modulation",
    )(cb, w_mod, b_mod.reshape(DEPTH, 1, n_out))


def _gelu_tanh(v):
    return 0.5 * v * (1.0 + jnp.tanh(np.sqrt(2.0 / np.pi).astype(np.float32) * (v + 0.044715 * (v * v * v))))


def _proj_kernel(x_ref, mod_ref, cos_ref, sin_ref, w_ref, o_ref, *, rope):
    x = x_ref[...]
    h = (x * (1.0 + mod_ref[MOD_SC1:MOD_SC1 + 1, :]) + mod_ref[MOD_SH1:MOD_SH1 + 1, :]).astype(BF16)
    tn = PROJ_COL_TILE
    for j in range(IN_COLS // tn):
        c0 = j * tn
        acc = jnp.dot(h, w_ref[:, c0:c0 + tn], preferred_element_type=F32)
        if c0 < OFF_AV:
            if rope:
                cos = cos_ref[...]
                sin = sin_ref[...]
                parts = []
                for hh in range(tn // HEAD_DIM):
                    a = acc[:, hh * HEAD_DIM:(hh + 1) * HEAD_DIM]
                    parts.append(a * cos + pltpu.roll(a, HEAD_DIM // 2, 1) * sin)
                acc = jnp.concatenate(parts, axis=1)
        elif c0 >= OFF_C:
            acc = _gelu_tanh(acc)
        o_ref[:, c0:c0 + tn] = acc.astype(BF16)


def _matmul_row_tile(*row_counts):
    return MATMUL_ROW_TILE if all(r % MATMUL_ROW_TILE == 0 for r in row_counts) else ROW_TILE


def _proj(x, x_row0, rows, mod, cos, sin, w_bf, layer, rope):
    d = D_MODEL
    tm = _matmul_row_tile(rows, x_row0)
    x_off = x_row0 // tm
    return pl.pallas_call(
        functools.partial(_proj_kernel, rope=rope),
        grid=(rows // tm,),
        in_specs=[pl.BlockSpec((tm, d), lambda i: (i + x_off, 0)),
                  pl.BlockSpec((8, d), lambda i: (0, 0)),
                  pl.BlockSpec((tm, HEAD_DIM), lambda i: (i, 0)),
                  pl.BlockSpec((tm, HEAD_DIM), lambda i: (i, 0)),
                  pl.BlockSpec((None, d, IN_COLS), lambda i: (layer, 0, 0), pipeline_mode=pl.Buffered(1))],
        out_specs=pl.BlockSpec((tm, IN_COLS), lambda i: (i, 0)),
        out_shape=jax.ShapeDtypeStruct((rows, IN_COLS), BF16),
        compiler_params=_params("arbitrary"),
        name="proj_rope" if rope else "proj_ctx",
    )(x, mod, cos, sin, w_bf)


LOG2E = 1.4426950408889634
LOGIT_SCALE = ATTN_SCALE * LOG2E


def _softmax_pv(problems):
    ms = []
    for s_parts, _, sink in problems:
        m = s_parts[0].max(axis=-1, keepdims=True)
        for s in s_parts[1:]:
            m = jnp.maximum(m, s.max(axis=-1, keepdims=True))
        ms.append(m if sink is None else jnp.maximum(m, sink))
    es = [[jnp.exp2(s - m).astype(BF16) for s in s_parts] for (s_parts, _, _), m in zip(problems, ms)]
    outs = []
    for (_, v_parts, sink), m, e_parts in zip(problems, ms, es):
        acc = None
        for e, v in zip(e_parts, v_parts):
            v1 = jnp.concatenate([v, jnp.ones((v.shape[0], LANES), BF16)], axis=1)
            pv = jnp.dot(e, v1, preferred_element_type=F32)
            acc = pv if acc is None else acc + pv
        denom = acc[:, HEAD_DIM:HEAD_DIM + 1]
        if sink is not None:
            denom = denom + jnp.exp2(sink - m)
        outs.append(acc[:, :HEAD_DIM] / denom)
    return outs


def _attn_a_kernel(sink_ref, q_ref, kp_ref, kc_ref, kn_ref, vp_ref, vc_ref, vn_ref, kx_ref, vx_ref, o_ref,
                   mask_ref, *, latent):
    i = pl.program_id(0)
    nb = pl.num_programs(0)
    nq = A_GROUP * A_BLOCK

    if latent:
        @pl.when(i == 0)
        def _():
            qi = lax.broadcasted_iota(jnp.int32, (nq, 3 * A_BLOCK), 0) % A_BLOCK
            jj = lax.broadcasted_iota(jnp.int32, (nq, 3 * A_BLOCK), 1)
            ok = (jj >= qi) & (jj <= qi + 2 * A_BLOCK)
            mask_ref[0] = jnp.where(ok & (jj >= A_BLOCK), 0.0, NEG_INF).astype(F32)
            mask_ref[1] = jnp.where(ok, 0.0, NEG_INF).astype(F32)
            mask_ref[2] = jnp.where(ok & (jj < 2 * A_BLOCK), 0.0, NEG_INF).astype(F32)

        which = jnp.where(i == 0, 0, jnp.where(i == nb - 1, 2, 1))

    problems = []
    for kh in range(A_KV_HEADS):
        hs = [kh * A_GROUP + g for g in range(A_GROUP)]
        q = jnp.concatenate([q_ref[:, h * HEAD_DIM:(h + 1) * HEAD_DIM] for h in hs], axis=0)
        sink = jnp.concatenate([jnp.full((A_BLOCK, 1), sink_ref[h] * LOG2E, F32) for h in hs], axis=0)
        ks = slice(kh * HEAD_DIM, (kh + 1) * HEAD_DIM)
        s_parts = [_dot_nt(q, kx_ref[:, ks]) * LOGIT_SCALE]
        v_parts = [vx_ref[:, ks]]
        if latent:
            kband = jnp.concatenate([kp_ref[:, ks], kc_ref[:, ks], kn_ref[:, ks]], axis=0)
            vband = jnp.concatenate([vp_ref[:, ks], vc_ref[:, ks], vn_ref[:, ks]], axis=0)
            s_parts.append(_dot_nt(q, kband) * LOGIT_SCALE + mask_ref[which])
            v_parts.append(vband)
        problems.append((s_parts, v_parts, sink))
    for kh, out in enumerate(_softmax_pv(problems)):
        for g in range(A_GROUP):
            h = kh * A_GROUP + g
            o_ref[:, h * HEAD_DIM:(h + 1) * HEAD_DIM] = out[g * A_BLOCK:(g + 1) * A_BLOCK].astype(BF16)


def _attn_a(p, pc, sink, latent):
    rows = p.shape[0]
    nb = rows // A_BLOCK
    assert nb >= 2
    kcol = OFF_AK // A_KV_W
    vcol = OFF_AV // A_KV_W

    def band(col, shift):
        return pl.BlockSpec((A_BLOCK, A_KV_W), lambda i, s: (jnp.clip(i + shift, 0, nb - 1), col))

    lctx = pc.shape[0]
    return pl.pallas_call(
        functools.partial(_attn_a_kernel, latent=latent),
        grid_spec=pltpu.PrefetchScalarGridSpec(
            num_scalar_prefetch=1,
            grid=(nb,),
            in_specs=[pl.BlockSpec((A_BLOCK, A_Q_W), lambda i, s: (i, 0)),
                      band(kcol, -1), band(kcol, 0), band(kcol, 1),
                      band(vcol, -1), band(vcol, 0), band(vcol, 1),
                      pl.BlockSpec((lctx, A_KV_W), lambda i, s: (0, kcol)),
                      pl.BlockSpec((lctx, A_KV_W), lambda i, s: (0, vcol))],
            out_specs=pl.BlockSpec((A_BLOCK, A_Q_W), lambda i, s: (i, 0)),
            scratch_shapes=[pltpu.VMEM((3, A_GROUP * A_BLOCK, 3 * A_BLOCK), F32)]),
        out_shape=jax.ShapeDtypeStruct((rows, A_Q_W), BF16),
        compiler_params=_params("arbitrary"),
        name="attn_a_latent" if latent else "attn_a_ctx",
    )(sink, p, p, p, p, p, p, p, pc, pc)


B_PAIR_W = 2 * HEAD_DIM
NA_TILE = 256
NA_GROUP_ROWS = NA_TILE // GRID_W
NA_WIN_ROWS = NA_ROWS + NA_GROUP_ROWS
NA_PAIRS = NA_WIN_ROWS // 2
NA_BIAS_OFFS = 2 * NA_ROWS
NA_BOTH, NA_LEFT, NA_RIGHT = range(3)


def _attn_b_kernel(q_ref, k_ref, v_ref, kx_ref, vx_ref, bias_ref, o_ref, *, latent, grid_rows):
    g = pl.program_id(1)
    if latent:
        r_base = g * NA_GROUP_ROWS
        w0 = jnp.clip(r_base - NA_ROWS // 2, 0, grid_rows - NA_WIN_ROWS)
        start = pl.multiple_of(w0 * GRID_W, GRID_W)
    problems = []
    for hh in range(2):
        hs = slice(hh * HEAD_DIM, (hh + 1) * HEAD_DIM)
        q = q_ref[:, hs]
        s_parts = [_dot_nt(q, kx_ref[:, hs]) * LOGIT_SCALE]
        v_parts = [vx_ref[:, hs]]
        if latent:
            kwin = k_ref[pl.ds(start, NA_WIN_ROWS * GRID_W), hs]
            vwin = v_ref[pl.ds(start, NA_WIN_ROWS * GRID_W), hs]
            bias_rows = []
            for rr in range(NA_GROUP_ROWS):
                r = r_base + rr
                r0 = jnp.clip(r - NA_ROWS // 2, 0, grid_rows - NA_ROWS)
                tiles = []
                for jp in range(NA_PAIRS):
                    ka = w0 + 2 * jp
                    in_a = (ka >= r0) & (ka < r0 + NA_ROWS)
                    in_b = (ka + 1 >= r0) & (ka + 1 < r0 + NA_ROWS)
                    variant = jnp.where(in_a, jnp.where(in_b, NA_BOTH, NA_LEFT), jnp.where(in_b, NA_RIGHT, NA_LEFT))
                    off = jnp.where(in_a | in_b, jnp.clip(ka - r + NA_ROWS, 0, NA_BIAS_OFFS - 1), 0)
                    tiles.append(bias_ref[hh, variant, off])
                bias_rows.append(jnp.concatenate(tiles, axis=1))
            bias = jnp.concatenate(bias_rows, axis=0)
            s_parts.append(_dot_nt(q, kwin) * LOGIT_SCALE + bias)
            v_parts.append(vwin)
        problems.append((s_parts, v_parts, None))
    for hh, out in enumerate(_softmax_pv(problems)):
        o_ref[:, hh * HEAD_DIM:(hh + 1) * HEAD_DIM] = out.astype(BF16)


def _na_bias_tables(rpb):
    cols = np.arange(GRID_W)
    c0 = np.clip(cols - NA_COLS // 2, 0, GRID_W - NA_COLS)
    rel = cols[None, :] - cols[:, None] + NA_COLS - 1
    ok = (cols[None, :] >= c0[:, None]) & (cols[None, :] < c0[:, None] + NA_COLS)
    onehot = (rel[None] == np.arange(2 * NA_COLS - 1)[:, None, None]).astype(np.float32)
    t = jnp.einsum("lhrd,dqk->lhrqk", rpb, onehot, precision=lax.Precision.HIGHEST)
    t = jnp.where(ok, t * LOG2E, NEG_INF).astype(F32)
    t = jnp.pad(t, ((0, 0), (0, 0), (1, 1), (0, 0), (0, 0)), constant_values=NEG_INF)
    pairs = jnp.concatenate([t[:, :, :-1], t[:, :, 1:]], axis=-1)
    keep = np.ones((3, 1, 1, 2 * GRID_W), bool)
    keep[NA_LEFT, :, :, GRID_W:] = False
    keep[NA_RIGHT, :, :, :GRID_W] = False
    return jnp.where(keep, pairs[:, :, None], NEG_INF)


def _attn_b(p, pc, bias_tabs, layer, latent):
    rows = p.shape[0]
    lctx = pc.shape[0]
    assert not latent or (rows // GRID_W >= NA_WIN_ROWS and rows % NA_TILE == 0)
    tile = NA_TILE if latent else ROW_TILE
    qcol = OFF_BQ // B_PAIR_W
    kcol = OFF_BK // B_PAIR_W
    vcol = OFF_BV // B_PAIR_W
    return pl.pallas_call(
        functools.partial(_attn_b_kernel, latent=latent, grid_rows=rows // GRID_W),
        grid=(B_HEADS // 2, rows // tile),
        in_specs=[pl.BlockSpec((tile, B_PAIR_W), lambda hp, g: (g, qcol + hp)),
                  pl.BlockSpec((rows, B_PAIR_W), lambda hp, g: (0, kcol + hp)),
                  pl.BlockSpec((rows, B_PAIR_W), lambda hp, g: (0, vcol + hp)),
                  pl.BlockSpec((lctx, B_PAIR_W), lambda hp, g: (0, kcol + hp)),
                  pl.BlockSpec((lctx, B_PAIR_W), lambda hp, g: (0, vcol + hp)),
                  pl.BlockSpec((None, 2, 3, NA_BIAS_OFFS, GRID_W, 2 * GRID_W),
                               lambda hp, g: (layer, hp, 0, 0, 0, 0))],
        out_specs=pl.BlockSpec((tile, B_PAIR_W), lambda hp, g: (g, hp)),
        out_shape=jax.ShapeDtypeStruct((rows, B_W), BF16),
        compiler_params=_params("arbitrary", "arbitrary"),
        name="attn_b_latent" if latent else "attn_b_ctx",
    )(p, p, p, pc, pc, bias_tabs)


def _sgu_kernel(u_ref, v_ref, g_ref, b_ref, w_ref, bs_ref, o_ref):
    for ch in range(ROW_TILE // CHUNK):
        rs = slice(ch * CHUNK, (ch + 1) * CHUNK)
        for grp in range(C_GROUPS):
            cs = slice(grp * HEAD_DIM, (grp + 1) * HEAD_DIM)
            vn = _layer_norm(v_ref[rs, cs].astype(F32), g_ref[:, cs], b_ref[:, cs])
            mixed = jnp.dot(w_ref[grp], vn.astype(BF16), preferred_element_type=F32) + bs_ref[grp]
            o_ref[rs, cs] = (u_ref[rs, cs].astype(F32) * mixed).astype(BF16)


def _sgu(p, ln_g, ln_b, w_bf, bs_b):
    rows = p.shape[0]
    ucol = OFF_C // C_W
    return pl.pallas_call(
        _sgu_kernel,
        grid=(rows // ROW_TILE,),
        in_specs=[pl.BlockSpec((ROW_TILE, C_W), lambda i: (i, ucol)),
                  pl.BlockSpec((ROW_TILE, C_W), lambda i: (i, ucol + 1)),
                  pl.BlockSpec((1, C_W), lambda i: (0, 0)),
                  pl.BlockSpec((1, C_W), lambda i: (0, 0)),
                  pl.BlockSpec((C_GROUPS, CHUNK, CHUNK), lambda i: (0, 0, 0)),
                  pl.BlockSpec((C_GROUPS, CHUNK, LANES), lambda i: (0, 0, 0))],
        out_specs=pl.BlockSpec((ROW_TILE, C_W), lambda i: (i, 0)),
        out_shape=jax.ShapeDtypeStruct((rows, C_W), BF16),
        compiler_params=_params("arbitrary"),
        name="sgu",
    )(p, p, ln_g, ln_b, w_bf, bs_b)


def _outproj_kernel(oa_ref, ob_ref, oc_ref, w_ref, x_ref, mod_ref, g_ref, b_ref, *rest):
    x1_ref, h2_ref = rest[-2], rest[-1]
    mix = jnp.dot(oa_ref[...], w_ref[0:A_Q_W, :], preferred_element_type=F32)
    mix += jnp.dot(ob_ref[...], w_ref[A_Q_W:A_Q_W + B_W, :], preferred_element_type=F32)
    mix += jnp.dot(oc_ref[...], w_ref[A_Q_W + B_W:, :], preferred_element_type=F32)
    t = DEEPNORM_ALPHA * x_ref[...] + mod_ref[MOD_G1:MOD_G1 + 1, :] * mix
    x1 = _layer_norm(t, g_ref[...], b_ref[...])
    x1_ref[...] = x1
    h2_ref[...] = x1 * (1.0 + mod_ref[MOD_SC2:MOD_SC2 + 1, :]) + mod_ref[MOD_SH2:MOD_SH2 + 1, :]


def _outproj(o_a, o_b, o_c, w_bf, layer, x, x_row0, mod, ln_g, ln_b, total_rows, out_row0, prev):
    rows = o_a.shape[0]
    d = D_MODEL
    tm = _matmul_row_tile(rows, x_row0, out_row0)
    x_off, out_off = x_row0 // tm, out_row0 // tm
    in_specs = [pl.BlockSpec((tm, A_Q_W), lambda i: (i, 0)),
                pl.BlockSpec((tm, B_W), lambda i: (i, 0)),
                pl.BlockSpec((tm, C_W), lambda i: (i, 0)),
                pl.BlockSpec((None, d, d), lambda i: (layer, 0, 0), pipeline_mode=pl.Buffered(1)),
                pl.BlockSpec((tm, d), lambda i: (i + x_off, 0)),
                pl.BlockSpec((8, d), lambda i: (0, 0)),
                pl.BlockSpec((1, d), lambda i: (0, 0)),
                pl.BlockSpec((1, d), lambda i: (0, 0))]
    args = [o_a, o_b, o_c, w_bf, x, mod, ln_g, ln_b]
    aliases = {}
    if prev is not None:
        in_specs += [pl.BlockSpec(memory_space=pl.ANY), pl.BlockSpec(memory_space=pl.ANY)]
        aliases = {len(args): 0, len(args) + 1: 1}
        args += list(prev)
    return pl.pallas_call(
        _outproj_kernel,
        grid=(rows // tm,),
        in_specs=in_specs,
        out_specs=[pl.BlockSpec((tm, d), lambda i: (i + out_off, 0))] * 2,
        out_shape=[jax.ShapeDtypeStruct((total_rows, d), F32)] * 2,
        input_output_aliases=aliases,
        compiler_params=_params("arbitrary"),
        name="outproj",
    )(*args)


def _top2_sublanes(vals, sub):
    m1 = vals.max(axis=0, keepdims=True)
    i1 = jnp.where(vals == m1, sub, vals.shape[0]).min(axis=0, keepdims=True)
    rest = jnp.where(sub == i1, -jnp.inf, vals)
    m2 = rest.max(axis=0, keepdims=True)
    i2 = jnp.where(rest == m2, sub, vals.shape[0]).min(axis=0, keepdims=True)
    return m1, i1, m2, i2


def _router_kernel(h_ref, wr_ref, rb_ref, e_ref, w_ref, rank_ref, cnt_ref, run_ref):
    i = pl.program_id(0)
    tm = ROW_TILE
    epg = EXPERTS_PER_GROUP

    @pl.when(i == 0)
    def _():
        run_ref[...] = jnp.zeros_like(run_ref)

    h = h_ref[...]
    h_hi = h.astype(BF16)
    h_mid = (h - h_hi.astype(F32)).astype(BF16)
    parts = _dot_nt(wr_ref[...], h_hi)
    parts_mid = _dot_nt(wr_ref[0:2 * N_EXPERTS, :], h_mid)
    logits = (parts[0:N_EXPERTS] + parts[N_EXPERTS:2 * N_EXPERTS] + parts[2 * N_EXPERTS:]
              + parts_mid[0:N_EXPERTS] + parts_mid[N_EXPERTS:])
    scores = _sigmoid(logits)
    biased = scores + jnp.concatenate([rb_ref[...]] * (tm // LANES), axis=1)
    sub = lax.broadcasted_iota(jnp.int32, (epg, tm), 0)

    best = None
    for g in range(N_EXPERT_GROUPS):
        m1, _, m2, _ = _top2_sublanes(biased[g * epg:(g + 1) * epg], sub)
        gs = m1 + m2
        if best is None:
            best, grp = gs, jnp.zeros((1, tm), jnp.int32)
            bsel, ssel = biased[0:epg], scores[0:epg]
        else:
            better = gs > best
            best = jnp.where(better, gs, best)
            grp = jnp.where(better, g, grp)
            bsel = jnp.where(better, biased[g * epg:(g + 1) * epg], bsel)
            ssel = jnp.where(better, scores[g * epg:(g + 1) * epg], ssel)
    _, i1, _, i2 = _top2_sublanes(bsel, sub)
    w1 = jnp.where(sub == i1, ssel, 0.0).sum(axis=0, keepdims=True)
    w2 = jnp.where(sub == i2, ssel, 0.0).sum(axis=0, keepdims=True)
    tot = w1 + w2
    e1 = grp * epg + i1
    e2 = grp * epg + i2

    eiota = lax.broadcasted_iota(jnp.int32, (N_EXPERTS, tm), 0)
    oh1 = (eiota == e1).astype(F32)
    oh2 = (eiota == e2).astype(F32)
    ohb = oh1 + oh2
    before = (lax.broadcasted_iota(jnp.int32, (tm, tm), 0) < lax.broadcasted_iota(jnp.int32, (tm, tm), 1))
    prefix = jnp.dot(ohb.astype(BF16), before.astype(BF16), preferred_element_type=F32)
    pos = run_ref[...] + prefix
    r1 = (oh1 * pos).sum(axis=0, keepdims=True)
    r2 = (oh2 * pos).sum(axis=0, keepdims=True)
    run_ref[...] = run_ref[...] + ohb.sum(axis=1, keepdims=True)

    e_ref[...] = jnp.concatenate([e1, e2], axis=0)
    w_ref[...] = jnp.concatenate([w1 / tot, w2 / tot], axis=0)
    rank_ref[...] = jnp.concatenate([r1, r2], axis=0).astype(jnp.int32)
    cnt_ref[...] = run_ref[:, 0:LANES]


def _router(h2, wr_t, rb_b):
    t = h2.shape[0]
    d = D_MODEL
    row2 = pl.BlockSpec((TOP_K, ROW_TILE), lambda i: (0, i))
    return pl.pallas_call(
        _router_kernel,
        grid=(t // ROW_TILE,),
        in_specs=[pl.BlockSpec((ROW_TILE, d), lambda i: (i, 0)),
                  pl.BlockSpec((3 * N_EXPERTS, d), lambda i: (0, 0)),
                  pl.BlockSpec((N_EXPERTS, LANES), lambda i: (0, 0))],
        out_specs=[row2, row2, row2, pl.BlockSpec((N_EXPERTS, LANES), lambda i: (0, 0))],
        out_shape=[jax.ShapeDtypeStruct((TOP_K, t), jnp.int32),
                   jax.ShapeDtypeStruct((TOP_K, t), F32),
                   jax.ShapeDtypeStruct((TOP_K, t), jnp.int32),
                   jax.ShapeDtypeStruct((N_EXPERTS, LANES), F32)],
        scratch_shapes=[pltpu.VMEM((N_EXPERTS, ROW_TILE), F32)],
        compiler_params=_params("arbitrary"),
        name="router",
    )(h2, wr_t, rb_b)


def _row_copy(src, row, dst, sem):
    return pltpu.make_async_copy(src.at[pl.ds(row, 1), :], dst, sem)


DISPATCH_SLOTS = 3


def _dispatch_kernel(slot_ref, h_hbm, xs_hbm, hbuf, lsem, ssem, *, n_tok):
    i = pl.program_id(0)
    nt = pl.num_programs(0)
    tm = ROW_TILE

    def load(blk):
        row0 = pl.multiple_of(blk * tm, tm)
        return pltpu.make_async_copy(h_hbm.at[pl.ds(row0, tm), :], hbuf.at[blk % DISPATCH_SLOTS],
                                     lsem.at[blk % DISPATCH_SLOTS])

    def wait_rows(par):
        for _ in range(TOP_K):
            pltpu.make_async_copy(hbuf.at[0], xs_hbm.at[pl.ds(0, tm), :], ssem.at[par]).wait()

    @pl.when(i == 0)
    def _():
        load(0).start()

    @pl.when(i + 1 < nt)
    def _():
        load(i + 1).start()

    load(i).wait()
    par = i % 2
    cur = hbuf.at[i % DISPATCH_SLOTS]
    for r in range(tm):
        for k in range(TOP_K):
            dst = xs_hbm.at[pl.ds(slot_ref[k * n_tok + i * tm + r], 1), :]
            pltpu.make_async_copy(cur.at[pl.ds(r, 1), :], dst, ssem.at[par]).start(priority=k)

    @pl.when(i > 0)
    def _():
        wait_rows(1 - par)

    @pl.when(i == nt - 1)
    def _():
        wait_rows(par)


def _dispatch(h2, slots, rows):
    t, d = h2.shape
    return pl.pallas_call(
        functools.partial(_dispatch_kernel, n_tok=t),
        grid_spec=pltpu.PrefetchScalarGridSpec(
            num_scalar_prefetch=1,
            grid=(t // ROW_TILE,),
            in_specs=[pl.BlockSpec(memory_space=pl.ANY)],
            out_specs=pl.BlockSpec(memory_space=pl.ANY),
            scratch_shapes=[pltpu.VMEM((DISPATCH_SLOTS, ROW_TILE, d), F32),
                            pltpu.SemaphoreType.DMA((DISPATCH_SLOTS,)),
                            pltpu.SemaphoreType.DMA((2,))]),
        out_shape=jax.ShapeDtypeStruct((rows, d), F32),
        compiler_params=_params("arbitrary"),
        name="dispatch",
    )(slots, h2)


def _experts_kernel(ts_ref, nu_ref, h_hbm, wg_hbm, wu_hbm, wd_hbm, ys_hbm,
                    xbuf, ybuf, gsem, ysem, wg_f, wu_f, wd_f, wsem, wg_bf, wu_bf, wd_bf, *, layer):
    e = pl.program_id(0)
    ne = pl.num_programs(0)
    n_used = nu_ref[0]
    tm = MOE_TILE
    tsub = tm

    def has_rows(ex):
        return ts_ref[ex + 1] > ts_ref[ex]

    def weight_copies(ex, wslot):
        return [pltpu.make_async_copy(src.at[layer, ex], dst.at[wslot], wsem.at[wslot])
                for src, dst in ((wg_hbm, wg_f), (wu_hbm, wu_f), (wd_hbm, wd_f))]

    def start_weights(ex, wslot):
        for cp in weight_copies(ex, wslot):
            cp.start(priority=1)

    def x_load(tile, slot):
        row0 = pl.multiple_of(tile * tsub, tsub)
        return pltpu.make_async_copy(h_hbm.at[pl.ds(row0, tsub), :], xbuf.at[slot], gsem.at[slot])

    def gather(tile, slot):
        x_load(tile, slot).start()

    def wait_gather(slot):
        x_load(0, slot).wait()

    def y_store(tile, slot):
        row0 = pl.multiple_of(tile * tsub, tsub)
        return pltpu.make_async_copy(ybuf.at[slot], ys_hbm.at[pl.ds(row0, tsub), :], ysem.at[slot])

    wslot = e % 2

    @pl.when((e == 0) & has_rows(0))
    def _():
        start_weights(0, 0)

    @pl.when((e == 0) & (n_used > 0))
    def _():
        gather(0, 0)

    nxt = jnp.minimum(e + 1, ne - 1)

    @pl.when((e + 1 < ne) & has_rows(nxt))
    def _():
        start_weights(nxt, 1 - wslot)

    @pl.when(has_rows(e))
    def _():
        for cp in weight_copies(e, wslot):
            cp.wait()
        wg_bf[...] = wg_f[wslot].astype(BF16)
        wu_bf[...] = wu_f[wslot].astype(BF16)
        wd_bf[...] = wd_f[wslot].astype(BF16)

    def tile_body(g, carry):
        slot = g % 2

        @pl.when(g >= 2)
        def _():
            y_store(g - 2, slot).wait()

        wait_gather(slot)
        x = xbuf[slot].astype(BF16)
        gather(jnp.minimum(g + 1, n_used - 1), 1 - slot)
        gate = jnp.dot(x, wg_bf[...], preferred_element_type=F32)
        up = jnp.dot(x, wu_bf[...], preferred_element_type=F32)
        hid = (gate * _sigmoid(gate)) * up
        ybuf[slot] = jnp.dot(hid.astype(BF16), wd_bf[...], preferred_element_type=F32)
        y_store(g, slot).start(priority=1)
        return carry

    lax.fori_loop(ts_ref[e], ts_ref[e + 1], tile_body, 0)

    @pl.when((e == pl.num_programs(0) - 1) & (n_used > 0))
    def _():
        wait_gather(n_used % 2)
        y_store(n_used - 1, (n_used - 1) % 2).wait()

        @pl.when(n_used >= 2)
        def _():
            y_store(n_used - 2, n_used % 2).wait()


def _experts(xs, tile_start, n_used, w_gate, w_up, w_down, layer):
    d = D_MODEL
    tm = MOE_TILE
    rows = xs.shape[0]

    hbm = pl.BlockSpec(memory_space=pl.ANY)
    return pl.pallas_call(
        functools.partial(_experts_kernel, layer=layer),
        grid_spec=pltpu.PrefetchScalarGridSpec(
            num_scalar_prefetch=2,
            grid=(N_EXPERTS,),
            in_specs=[hbm, hbm, hbm, hbm],
            out_specs=hbm,
            scratch_shapes=[pltpu.VMEM((2, tm, d), F32),
                            pltpu.VMEM((2, tm, d), F32),
                            pltpu.SemaphoreType.DMA((2,)),
                            pltpu.SemaphoreType.DMA((2,)),
                            pltpu.VMEM((2, d, EXPERT_FF), F32),
                            pltpu.VMEM((2, d, EXPERT_FF), F32),
                            pltpu.VMEM((2, EXPERT_FF, d), F32),
                            pltpu.SemaphoreType.DMA((2,)),
                            pltpu.VMEM((d, EXPERT_FF), BF16),
                            pltpu.VMEM((d, EXPERT_FF), BF16),
                            pltpu.VMEM((EXPERT_FF, d), BF16)]),
        out_shape=jax.ShapeDtypeStruct((rows, d), F32),
        compiler_params=_params("arbitrary"),
        name="experts",
    )(tile_start, n_used, xs, w_gate, w_up, w_down)


def _combine_kernel(slot_ref, ys_hbm, w_ref, x_ref, mod_ref, g_ref, b_ref, o_ref, ybuf, sem, *, n_tok):
    i = pl.program_id(0)
    nt = pl.num_programs(0)
    tm = ROW_TILE

    def issue(blk, buf):
        for r in range(tm):
            for k in range(TOP_K):
                row = slot_ref[k * n_tok + blk * tm + r]
                _row_copy(ys_hbm, row, ybuf.at[buf, k, pl.ds(r, 1), :], sem.at[buf]).start(priority=k)

    def wait(buf):
        for k in range(TOP_K):
            pltpu.make_async_copy(ys_hbm.at[pl.ds(0, tm), :], ybuf.at[buf, k], sem.at[buf]).wait()

    @pl.when(i == 0)
    def _():
        issue(0, 0)

    buf = i % 2
    wait(buf)
    issue(jnp.minimum(i + 1, nt - 1), 1 - buf)
    y = w_ref[:, 0:1] * ybuf[buf, 0] + w_ref[:, 1:2] * ybuf[buf, 1]
    t = DEEPNORM_ALPHA * x_ref[...] + mod_ref[MOD_G2:MOD_G2 + 1, :] * y
    o_ref[...] = _layer_norm(t, g_ref[...], b_ref[...])

    @pl.when(i == nt - 1)
    def _():
        wait(1 - buf)


def _combine(ys, slots, w_tok, x1, mods, n_lat_blocks, ln_g, ln_b):
    t = x1.shape[0]
    d = D_MODEL
    return pl.pallas_call(
        functools.partial(_combine_kernel, n_tok=t),
        grid_spec=pltpu.PrefetchScalarGridSpec(
            num_scalar_prefetch=1,
            grid=(t // ROW_TILE,),
            in_specs=[pl.BlockSpec(memory_space=pl.ANY),
                      pl.BlockSpec((ROW_TILE, TOP_K), lambda i, s: (i, 0)),
                      pl.BlockSpec((ROW_TILE, d), lambda i, s: (i, 0)),
                      pl.BlockSpec((None, 8, d), lambda i, s: (jnp.where(i < n_lat_blocks, 1, 0), 0, 0)),
                      pl.BlockSpec((1, d), lambda i, s: (0, 0)),
                      pl.BlockSpec((1, d), lambda i, s: (0, 0))],
            out_specs=pl.BlockSpec((ROW_TILE, d), lambda i, s: (i, 0)),
            scratch_shapes=[pltpu.VMEM((2, TOP_K, ROW_TILE, d), F32),
                            pltpu.SemaphoreType.DMA((2,))]),
        out_shape=jax.ShapeDtypeStruct((t, d), F32),
        compiler_params=_params("arbitrary"),
        name="combine",
    )(slots, ys, w_tok, x1, mods, ln_g, ln_b)


def _moe(h2, x1, mods, n_lat_blocks, wr_t, rb_b, w_gate, w_up, w_down, layer, ln_g, ln_b):
    t = x1.shape[0]
    tm = MOE_TILE
    e_idx, w_tok, rank, cnt = _router(h2, wr_t, rb_b)
    counts = cnt[:, 0].astype(jnp.int32)
    tiles_per = (counts + tm - 1) // tm
    tile_end = jnp.cumsum(tiles_per)
    n_used = tile_end[-1]
    row_off = (tile_end - tiles_per) * tm
    experts = jnp.arange(N_EXPERTS, dtype=jnp.int32)
    slots = jnp.sum(jnp.where(e_idx[:, :, None] == experts, row_off, 0), axis=-1) + rank
    max_tiles = (TOP_K * t + N_EXPERTS * (tm - 1)) // tm + 1
    tile_start = jnp.concatenate([tile_end - tiles_per, n_used.reshape(1)]).astype(jnp.int32)
    slots = slots.reshape(-1)
    xs = _dispatch(h2, slots, max_tiles * tm)
    ys = _experts(xs, tile_start, n_used.reshape(1).astype(jnp.int32), w_gate, w_up, w_down, layer)
    return _combine(ys, slots, w_tok.T, x1, mods, n_lat_blocks, ln_g, ln_b)


def _rope_tables(n):
    t = np.arange(n)
    row = (t // GRID_W).astype(np.float64)
    col = (t % GRID_W).astype(np.float64)
    n_freq = HEAD_DIM // 4
    inv_freq = ROPE_BASE ** (-np.arange(n_freq, dtype=np.float64) / n_freq)
    ang = np.concatenate([row[:, None] * inv_freq, col[:, None] * inv_freq], axis=-1)
    cos, sin = np.cos(ang), np.sin(ang)
    return (np.concatenate([cos, cos], axis=-1).astype(np.float32),
            np.concatenate([-sin, sin], axis=-1).astype(np.float32))


def kernel(x, c, ctx, c_ctx, w_mod, b_mod, w_in, attn_sink, na_rpb, sgu_ln_g, sgu_ln_b, sgu_w, sgu_b,
           w_out, ln1_g, ln1_b, w_router, router_bias, w_gate, w_up, w_down, ln2_g, ln2_b):
    batch, n, d = x.shape
    lctx = ctx.shape[1]
    assert batch == 1 and d == D_MODEL and n % ROW_TILE == 0 and lctx % ROW_TILE == 0
    n_lat_blocks = n // ROW_TILE

    mods = _modulation(c, c_ctx, w_mod, b_mod).reshape(DEPTH, 8, 6, d)
    mod_lat = jnp.pad(mods[:, 0], ((0, 0), (0, 2), (0, 0)))
    mod_ctx = jnp.pad(mods[:, 1], ((0, 0), (0, 2), (0, 0)))
    cos, sin = _rope_tables(n)
    wr_f = w_router.T
    def bf16_part(v):
        return lax.bitcast_convert_type(lax.bitcast_convert_type(v, jnp.int32) & jnp.int32(-65536), F32)

    wr_hi = bf16_part(wr_f)
    wr_mid = bf16_part(wr_f - wr_hi)
    wr_lo = wr_f - wr_hi - wr_mid
    wr_t = jnp.concatenate([wr_hi, wr_mid, wr_lo], axis=0).astype(BF16)
    rb_b = jnp.broadcast_to(router_bias.reshape(N_EXPERTS, 1), (N_EXPERTS, LANES))

    x_lat = x[0]
    x_ctx, ctx_row0 = ctx[0], 0
    w_in_bf = w_in.astype(BF16)
    w_out_bf = w_out.astype(BF16)
    bias_tabs = _na_bias_tables(na_rpb)
    for l in range(DEPTH):
        last = l == DEPTH - 1
        sgu_w_bf = sgu_w[l].astype(BF16)
        sgu_b_b = jnp.broadcast_to(sgu_b[l][:, :, None], (C_GROUPS, CHUNK, LANES))
        ln_g_c, ln_b_c = sgu_ln_g[l].reshape(1, C_W), sgu_ln_b[l].reshape(1, C_W)
        g1, b1 = ln1_g[l].reshape(1, d), ln1_b[l].reshape(1, d)
        g2, b2 = ln2_g[l].reshape(1, d), ln2_b[l].reshape(1, d)

        p = _proj(x_lat, 0, n, mod_lat[l], cos, sin, w_in_bf, l, rope=True)
        pc = _proj(x_ctx, ctx_row0, lctx, mod_ctx[l], cos, sin, w_in_bf, l, rope=False)
        o_a = _attn_a(p, pc, attn_sink[l], latent=True)
        o_b = _attn_b(p, pc, bias_tabs, l, latent=True)
        o_c = _sgu(p, ln_g_c, ln_b_c, sgu_w_bf, sgu_b_b)
        mod_both = jnp.stack([mod_ctx[l], mod_lat[l]])
        if last:
            x1, h2 = _outproj(o_a, o_b, o_c, w_out_bf, l, x_lat, 0, mod_lat[l], g1, b1, n, 0, None)
            x_lat = _moe(h2, x1, mod_both, n_lat_blocks, wr_t, rb_b, w_gate, w_up, w_down, l, g2, b2)
        else:
            oc_a = _attn_a(pc, pc, attn_sink[l], latent=False)
            oc_b = _attn_b(pc, pc, bias_tabs, l, latent=False)
            oc_c = _sgu(pc, ln_g_c, ln_b_c, sgu_w_bf, sgu_b_b)
            total = n + lctx
            prev = _outproj(o_a, o_b, o_c, w_out_bf, l, x_lat, 0, mod_lat[l], g1, b1, total, 0, None)
            x1, h2 = _outproj(oc_a, oc_b, oc_c, w_out_bf, l, x_ctx, ctx_row0, mod_ctx[l], g1, b1, total, n,
                              prev)
            x_all = _moe(h2, x1, mod_both, n_lat_blocks, wr_t, rb_b, w_gate, w_up, w_down, l, g2, b2)
            x_lat, x_ctx, ctx_row0 = x_all, x_all, n
    return x_lat.reshape(batch, n, d)
```

```python
import functools

import numpy as np
import jax
import jax.numpy as jnp
from jax import lax
from jax.experimental import pallas as pl
from jax.experimental.pallas import tpu as pltpu

F32 = jnp.float32
BF16 = jnp.bfloat16

D_MODEL = 2048
DEPTH = 2
GRID_W = 64
HEAD_DIM = 128
A_HEADS = 6
A_KV_HEADS = 2
A_GROUP = A_HEADS // A_KV_HEADS
A_BLOCK = 128
B_HEADS = 6
NA_ROWS = 8
NA_COLS = 16
C_GROUPS = 4
C_W = C_GROUPS * HEAD_DIM
CHUNK = 128
N_EXPERTS = 32
N_EXPERT_GROUPS = 4
EXPERTS_PER_GROUP = N_EXPERTS // N_EXPERT_GROUPS
TOP_K = 2
EXPERT_FF = 512
ROPE_BASE = 10000.0
LN_EPS = 1e-5
NEG_INF = -1e30
DEEPNORM_ALPHA = (2 * DEPTH) ** 0.25
ATTN_SCALE = HEAD_DIM ** -0.5

A_Q_W = A_HEADS * HEAD_DIM
A_KV_W = A_KV_HEADS * HEAD_DIM
B_W = B_HEADS * HEAD_DIM
OFF_AK = A_Q_W
OFF_AV = OFF_AK + A_KV_W
OFF_BQ = OFF_AV + A_KV_W
OFF_BK = OFF_BQ + B_W
OFF_BV = OFF_BK + B_W
OFF_C = OFF_BV + B_W
IN_COLS = OFF_C + 2 * C_W

VMEM_LIMIT_BYTES = 56 * 1024 * 1024
LANES = 128

ROW_TILE = 256
MATMUL_ROW_TILE = 512
PROJ_COL_TILE = 512
MOD_COL_TILE = 1024
MOE_TILE = 256

MOD_SH1, MOD_SC1, MOD_G1, MOD_SH2, MOD_SC2, MOD_G2 = range(6)


def _params(*sem):
    return pltpu.CompilerParams(dimension_semantics=sem, vmem_limit_bytes=VMEM_LIMIT_BYTES)


def _layer_norm(t, g, b):
    mu = jnp.mean(t, axis=-1, keepdims=True)
    d = t - mu
    var = jnp.mean(d * d, axis=-1, keepdims=True)
    return d * lax.rsqrt(var + LN_EPS) * g + b


def _sigmoid(v):
    return 1.0 / (1.0 + jnp.exp(-v))


def _dot_nt(a, b):
    return lax.dot_general(a, b, (((1,), (1,)), ((), ())), preferred_element_type=F32)


def _mod_kernel(c_ref, w_ref, b_ref, o_ref):
    w = w_ref[...]
    reps = w.shape[1] // LANES
    rows = []
    for r in range(2):
        cv = c_ref[r]
        s = cv * _sigmoid(cv)
        sb = jnp.concatenate([s] * reps, axis=1)
        rows.append(jnp.sum(w * sb, axis=0, keepdims=True) + b_ref[...])
    rows.append(jnp.zeros((6, w.shape[1]), F32))
    o_ref[...] = jnp.concatenate(rows, axis=0)


def _modulation(c, c_ctx, w_mod, b_mod):
    d = D_MODEL
    cb = jnp.stack([jnp.broadcast_to(c.reshape(d, 1), (d, LANES)),
                    jnp.broadcast_to(c_ctx.reshape(d, 1), (d, LANES))])
    n_out = 6 * d
    return pl.pallas_call(
        _mod_kernel,
        grid=(DEPTH, n_out // MOD_COL_TILE),
        in_specs=[pl.BlockSpec((2, d, LANES), lambda l, j: (0, 0, 0)),
                  pl.BlockSpec((None, d, MOD_COL_TILE), lambda l, j: (l, 0, j)),
                  pl.BlockSpec((None, 1, MOD_COL_TILE), lambda l, j: (l, 0, j))],
        out_specs=pl.BlockSpec((None, 8, MOD_COL_TILE), lambda l, j: (l, 0, j)),
        out_shape=jax.ShapeDtypeStruct((DEPTH, 8, n_out), F32),
        compiler_params=_params("arbitrary", "arbitrary"),
        name="modulation",
    )(cb, w_mod, b_mod.reshape(DEPTH, 1, n_out))


def _gelu_tanh(v):
    return 0.5 * v * (1.0 + jnp.tanh(np.sqrt(2.0 / np.pi).astype(np.float32) * (v + 0.044715 * (v * v * v))))


def _proj_kernel(x_ref, mod_ref, cos_ref, sin_ref, w_ref, o_ref, *, rope):
    x = x_ref[...]
    h = (x * (1.0 + mod_ref[MOD_SC1:MOD_SC1 + 1, :]) + mod_ref[MOD_SH1:MOD_SH1 + 1, :]).astype(BF16)
    tn = PROJ_COL_TILE
    for j in range(IN_COLS // tn):
        c0 = j * tn
        acc = jnp.dot(h, w_ref[:, c0:c0 + tn], preferred_element_type=F32)
        if c0 < OFF_AV:
            if rope:
                cos = cos_ref[...]
                sin = sin_ref[...]
                parts = []
                for hh in range(tn // HEAD_DIM):
                    a = acc[:, hh * HEAD_DIM:(hh + 1) * HEAD_DIM]
                    parts.append(a * cos + pltpu.roll(a, HEAD_DIM // 2, 1) * sin)
                acc = jnp.concatenate(parts, axis=1)
        elif c0 >= OFF_C:
            acc = _gelu_tanh(acc)
        o_ref[:, c0:c0 + tn] = acc.astype(BF16)


def _matmul_row_tile(*row_counts):
    return MATMUL_ROW_TILE if all(r % MATMUL_ROW_TILE == 0 for r in row_counts) else ROW_TILE


def _proj(x, x_row0, rows, mod, cos, sin, w_bf, layer, rope):
    d = D_MODEL
    tm = _matmul_row_tile(rows, x_row0)
    x_off = x_row0 // tm
    return pl.pallas_call(
        functools.partial(_proj_kernel, rope=rope),
        grid=(rows // tm,),
        in_specs=[pl.BlockSpec((tm, d), lambda i: (i + x_off, 0)),
                  pl.BlockSpec((8, d), lambda i: (0, 0)),
                  pl.BlockSpec((tm, HEAD_DIM), lambda i: (i, 0)),
                  pl.BlockSpec((tm, HEAD_DIM), lambda i: (i, 0)),
                  pl.BlockSpec((None, d, IN_COLS), lambda i: (layer, 0, 0), pipeline_mode=pl.Buffered(1))],
        out_specs=pl.BlockSpec((tm, IN_COLS), lambda i: (i, 0)),
        out_shape=jax.ShapeDtypeStruct((rows, IN_COLS), BF16),
        compiler_params=_params("arbitrary"),
        name="proj_rope" if rope else "proj_ctx",
    )(x, mod, cos, sin, w_bf)


LOG2E = 1.4426950408889634
LOGIT_SCALE = ATTN_SCALE * LOG2E


def _softmax_pv(problems):
    ms = []
    for s_parts, _, sink in problems:
        m = s_parts[0].max(axis=-1, keepdims=True)
        for s in s_parts[1:]:
            m = jnp.maximum(m, s.max(axis=-1, keepdims=True))
        ms.append(m if sink is None else jnp.maximum(m, sink))
    es = [[jnp.exp2(s - m).astype(BF16) for s in s_parts] for (s_parts, _, _), m in zip(problems, ms)]
    outs = []
    for (_, v_parts, sink), m, e_parts in zip(problems, ms, es):
        acc = None
        for e, v in zip(e_parts, v_parts):
            v1 = jnp.concatenate([v, jnp.ones((v.shape[0], LANES), BF16)], axis=1)
            pv = jnp.dot(e, v1, preferred_element_type=F32)
            acc = pv if acc is None else acc + pv
        denom = acc[:, HEAD_DIM:HEAD_DIM + 1]
        if sink is not None:
            denom = denom + jnp.exp2(sink - m)
        outs.append(acc[:, :HEAD_DIM] / denom)
    return outs


def _attn_a_kernel(sink_ref, q_ref, kp_ref, kc_ref, kn_ref, vp_ref, vc_ref, vn_ref, kx_ref, vx_ref, o_ref,
                   mask_ref, *, latent):
    i = pl.program_id(0)
    nb = pl.num_programs(0)
    nq = A_GROUP * A_BLOCK

    if latent:
        @pl.when(i == 0)
        def _():
            qi = lax.broadcasted_iota(jnp.int32, (nq, 3 * A_BLOCK), 0) % A_BLOCK
            jj = lax.broadcasted_iota(jnp.int32, (nq, 3 * A_BLOCK), 1)
            ok = (jj >= qi) & (jj <= qi + 2 * A_BLOCK)
            mask_ref[0] = jnp.where(ok & (jj >= A_BLOCK), 0.0, NEG_INF).astype(F32)
            mask_ref[1] = jnp.where(ok, 0.0, NEG_INF).astype(F32)
            mask_ref[2] = jnp.where(ok & (jj < 2 * A_BLOCK), 0.0, NEG_INF).astype(F32)

        which = jnp.where(i == 0, 0, jnp.where(i == nb - 1, 2, 1))

    problems = []
    for kh in range(A_KV_HEADS):
        hs = [kh * A_GROUP + g for g in range(A_GROUP)]
        q = jnp.concatenate([q_ref[:, h * HEAD_DIM:(h + 1) * HEAD_DIM] for h in hs], axis=0)
        sink = jnp.concatenate([jnp.full((A_BLOCK, 1), sink_ref[h] * LOG2E, F32) for h in hs], axis=0)
        ks = slice(kh * HEAD_DIM, (kh + 1) * HEAD_DIM)
        s_parts = [_dot_nt(q, kx_ref[:, ks]) * LOGIT_SCALE]
        v_parts = [vx_ref[:, ks]]
        if latent:
            kband = jnp.concatenate([kp_ref[:, ks], kc_ref[:, ks], kn_ref[:, ks]], axis=0)
            vband = jnp.concatenate([vp_ref[:, ks], vc_ref[:, ks], vn_ref[:, ks]], axis=0)
            s_parts.append(_dot_nt(q, kband) * LOGIT_SCALE + mask_ref[which])
            v_parts.append(vband)
        problems.append((s_parts, v_parts, sink))
    for kh, out in enumerate(_softmax_pv(problems)):
        for g in range(A_GROUP):
            h = kh * A_GROUP + g
            o_ref[:, h * HEAD_DIM:(h + 1) * HEAD_DIM] = out[g * A_BLOCK:(g + 1) * A_BLOCK].astype(BF16)


def _attn_a(p, pc, sink, latent):
    rows = p.shape[0]
    nb = rows // A_BLOCK
    assert nb >= 2
    kcol = OFF_AK // A_KV_W
    vcol = OFF_AV // A_KV_W

    def band(col, shift):
        return pl.BlockSpec((A_BLOCK, A_KV_W), lambda i, s: (jnp.clip(i + shift, 0, nb - 1), col))

    lctx = pc.shape[0]
    return pl.pallas_call(
        functools.partial(_attn_a_kernel, latent=latent),
        grid_spec=pltpu.PrefetchScalarGridSpec(
            num_scalar_prefetch=1,
            grid=(nb,),
            in_specs=[pl.BlockSpec((A_BLOCK, A_Q_W), lambda i, s: (i, 0)),
                      band(kcol, -1), band(kcol, 0), band(kcol, 1),
                      band(vcol, -1), band(vcol, 0), band(vcol, 1),
                      pl.BlockSpec((lctx, A_KV_W), lambda i, s: (0, kcol)),
                      pl.BlockSpec((lctx, A_KV_W), lambda i, s: (0, vcol))],
            out_specs=pl.BlockSpec((A_BLOCK, A_Q_W), lambda i, s: (i, 0)),
            scratch_shapes=[pltpu.VMEM((3, A_GROUP * A_BLOCK, 3 * A_BLOCK), F32)]),
        out_shape=jax.ShapeDtypeStruct((rows, A_Q_W), BF16),
        compiler_params=_params("arbitrary"),
        name="attn_a_latent" if latent else "attn_a_ctx",
    )(sink, p, p, p, p, p, p, p, pc, pc)


B_PAIR_W = 2 * HEAD_DIM
NA_TILE = 256
NA_GROUP_ROWS = NA_TILE // GRID_W
NA_WIN_ROWS = NA_ROWS + NA_GROUP_ROWS
NA_PAIRS = NA_WIN_ROWS // 2
NA_BIAS_OFFS = 2 * NA_ROWS
NA_BOTH, NA_LEFT, NA_RIGHT = range(3)


def _attn_b_kernel(*refs, latent, grid_rows):
    npair = B_HEADS // 2
    q_refs, k_refs, v_refs, kx_refs, vx_refs = (refs[j * npair:(j + 1) * npair] for j in range(5))
    bias_ref, o_ref = refs[5 * npair], refs[5 * npair + 1]
    g = pl.program_id(0)
    if latent:
        r_base = g * NA_GROUP_ROWS
        w0 = jnp.clip(r_base - NA_ROWS // 2, 0, grid_rows - NA_WIN_ROWS)
        start = pl.multiple_of(w0 * GRID_W, GRID_W)
    for hp in range(npair):
        problems = []
        for hh in range(2):
            hs = slice(hh * HEAD_DIM, (hh + 1) * HEAD_DIM)
            q = q_refs[hp][:, hs]
            s_parts = [_dot_nt(q, kx_refs[hp][:, hs]) * LOGIT_SCALE]
            v_parts = [vx_refs[hp][:, hs]]
            if latent:
                kwin = k_refs[hp][pl.ds(start, NA_WIN_ROWS * GRID_W), hs]
                vwin = v_refs[hp][pl.ds(start, NA_WIN_ROWS * GRID_W), hs]
                bias_rows = []
                for rr in range(NA_GROUP_ROWS):
                    r = r_base + rr
                    r0 = jnp.clip(r - NA_ROWS // 2, 0, grid_rows - NA_ROWS)
                    tiles = []
                    for jp in range(NA_PAIRS):
                        ka = w0 + 2 * jp
                        in_a = (ka >= r0) & (ka < r0 + NA_ROWS)
                        in_b = (ka + 1 >= r0) & (ka + 1 < r0 + NA_ROWS)
                        variant = jnp.where(in_a, jnp.where(in_b, NA_BOTH, NA_LEFT),
                                            jnp.where(in_b, NA_RIGHT, NA_LEFT))
                        off = jnp.where(in_a | in_b, jnp.clip(ka - r + NA_ROWS, 0, NA_BIAS_OFFS - 1), 0)
                        tiles.append(bias_ref[2 * hp + hh, variant, off])
                    bias_rows.append(jnp.concatenate(tiles, axis=1))
                bias = jnp.concatenate(bias_rows, axis=0)
                s_parts.append(_dot_nt(q, kwin) * LOGIT_SCALE + bias)
                v_parts.append(vwin)
            problems.append((s_parts, v_parts, None))
        for hh, out in enumerate(_softmax_pv(problems)):
            h = 2 * hp + hh
            o_ref[:, h * HEAD_DIM:(h + 1) * HEAD_DIM] = out.astype(BF16)


def _na_bias_tables(rpb):
    cols = np.arange(GRID_W)
    c0 = np.clip(cols - NA_COLS // 2, 0, GRID_W - NA_COLS)
    rel = cols[None, :] - cols[:, None] + NA_COLS - 1
    ok = (cols[None, :] >= c0[:, None]) & (cols[None, :] < c0[:, None] + NA_COLS)
    onehot = (rel[None] == np.arange(2 * NA_COLS - 1)[:, None, None]).astype(np.float32)
    t = jnp.einsum("lhrd,dqk->lhrqk", rpb, onehot, precision=lax.Precision.HIGHEST)
    t = jnp.where(ok, t * LOG2E, NEG_INF).astype(F32)
    t = jnp.pad(t, ((0, 0), (0, 0), (1, 1), (0, 0), (0, 0)), constant_values=NEG_INF)
    pairs = jnp.concatenate([t[:, :, :-1], t[:, :, 1:]], axis=-1)
    keep = np.ones((3, 1, 1, 2 * GRID_W), bool)
    keep[NA_LEFT, :, :, GRID_W:] = False
    keep[NA_RIGHT, :, :, :GRID_W] = False
    return jnp.where(keep, pairs[:, :, None], NEG_INF)


def _attn_b(p, pc, bias_tabs, layer, latent):
    rows = p.shape[0]
    lctx = pc.shape[0]
    assert not latent or (rows // GRID_W >= NA_WIN_ROWS and rows % NA_TILE == 0)
    tile = NA_TILE if latent else ROW_TILE
    npair = B_HEADS // 2
    qcol = OFF_BQ // B_PAIR_W
    kcol = OFF_BK // B_PAIR_W
    vcol = OFF_BV // B_PAIR_W

    def resident(nrows, col):
        return [pl.BlockSpec((nrows, B_PAIR_W), lambda g, c=col + hp: (0, c), pipeline_mode=pl.Buffered(1))
                for hp in range(npair)]

    return pl.pallas_call(
        functools.partial(_attn_b_kernel, latent=latent, grid_rows=rows // GRID_W),
        grid=(rows // tile,),
        in_specs=([pl.BlockSpec((tile, B_PAIR_W), lambda g, c=qcol + hp: (g, c)) for hp in range(npair)]
                  + resident(rows, kcol) + resident(rows, vcol) + resident(lctx, kcol) + resident(lctx, vcol)
                  + [pl.BlockSpec((None, B_HEADS, 3, NA_BIAS_OFFS, GRID_W, 2 * GRID_W),
                                  lambda g: (layer, 0, 0, 0, 0, 0), pipeline_mode=pl.Buffered(1))]),
        out_specs=pl.BlockSpec((tile, B_W), lambda g: (g, 0)),
        out_shape=jax.ShapeDtypeStruct((rows, B_W), BF16),
        compiler_params=_params("arbitrary"),
        name="attn_b_latent" if latent else "attn_b_ctx",
    )(*([p] * (3 * npair) + [pc] * (2 * npair) + [bias_tabs]))


def _sgu_kernel(u_ref, v_ref, g_ref, b_ref, w_ref, bs_ref, o_ref):
    for ch in range(ROW_TILE // CHUNK):
        rs = slice(ch * CHUNK, (ch + 1) * CHUNK)
        for grp in range(C_GROUPS):
            cs = slice(grp * HEAD_DIM, (grp + 1) * HEAD_DIM)
            vn = _layer_norm(v_ref[rs, cs].astype(F32), g_ref[:, cs], b_ref[:, cs])
            mixed = jnp.dot(w_ref[grp], vn.astype(BF16), preferred_element_type=F32) + bs_ref[grp]
            o_ref[rs, cs] = (u_ref[rs, cs].astype(F32) * mixed).astype(BF16)


def _sgu(p, ln_g, ln_b, w_bf, bs_b):
    rows = p.shape[0]
    ucol = OFF_C // C_W
    return pl.pallas_call(
        _sgu_kernel,
        grid=(rows // ROW_TILE,),
        in_specs=[pl.BlockSpec((ROW_TILE, C_W), lambda i: (i, ucol)),
                  pl.BlockSpec((ROW_TILE, C_W), lambda i: (i, ucol + 1)),
                  pl.BlockSpec((1, C_W), lambda i: (0, 0)),
                  pl.BlockSpec((1, C_W), lambda i: (0, 0)),
                  pl.BlockSpec((C_GROUPS, CHUNK, CHUNK), lambda i: (0, 0, 0)),
                  pl.BlockSpec((C_GROUPS, CHUNK, LANES), lambda i: (0, 0, 0))],
        out_specs=pl.BlockSpec((ROW_TILE, C_W), lambda i: (i, 0)),
        out_shape=jax.ShapeDtypeStruct((rows, C_W), BF16),
        compiler_params=_params("arbitrary"),
        name="sgu",
    )(p, p, ln_g, ln_b, w_bf, bs_b)


def _outproj_kernel(oa_ref, ob_ref, oc_ref, w_ref, x_ref, mod_ref, g_ref, b_ref, *rest):
    x1_ref, h2_ref = rest[-2], rest[-1]
    mix = jnp.dot(oa_ref[...], w_ref[0:A_Q_W, :], preferred_element_type=F32)
    mix += jnp.dot(ob_ref[...], w_ref[A_Q_W:A_Q_W + B_W, :], preferred_element_type=F32)
    mix += jnp.dot(oc_ref[...], w_ref[A_Q_W + B_W:, :], preferred_element_type=F32)
    t = DEEPNORM_ALPHA * x_ref[...] + mod_ref[MOD_G1:MOD_G1 + 1, :] * mix
    x1 = _layer_norm(t, g_ref[...], b_ref[...])
    x1_ref[...] = x1
    h2_ref[...] = x1 * (1.0 + mod_ref[MOD_SC2:MOD_SC2 + 1, :]) + mod_ref[MOD_SH2:MOD_SH2 + 1, :]


def _outproj(o_a, o_b, o_c, w_bf, layer, x, x_row0, mod, ln_g, ln_b, total_rows, out_row0, prev):
    rows = o_a.shape[0]
    d = D_MODEL
    tm = _matmul_row_tile(rows, x_row0, out_row0)
    x_off, out_off = x_row0 // tm, out_row0 // tm
    in_specs = [pl.BlockSpec((tm, A_Q_W), lambda i: (i, 0)),
                pl.BlockSpec((tm, B_W), lambda i: (i, 0)),
                pl.BlockSpec((tm, C_W), lambda i: (i, 0)),
                pl.BlockSpec((None, d, d), lambda i: (layer, 0, 0), pipeline_mode=pl.Buffered(1)),
                pl.BlockSpec((tm, d), lambda i: (i + x_off, 0)),
                pl.BlockSpec((8, d), lambda i: (0, 0)),
                pl.BlockSpec((1, d), lambda i: (0, 0)),
                pl.BlockSpec((1, d), lambda i: (0, 0))]
    args = [o_a, o_b, o_c, w_bf, x, mod, ln_g, ln_b]
    aliases = {}
    if prev is not None:
        in_specs += [pl.BlockSpec(memory_space=pl.ANY), pl.BlockSpec(memory_space=pl.ANY)]
        aliases = {len(args): 0, len(args) + 1: 1}
        args += list(prev)
    return pl.pallas_call(
        _outproj_kernel,
        grid=(rows // tm,),
        in_specs=in_specs,
        out_specs=[pl.BlockSpec((tm, d), lambda i: (i + out_off, 0))] * 2,
        out_shape=[jax.ShapeDtypeStruct((total_rows, d), F32)] * 2,
        input_output_aliases=aliases,
        compiler_params=_params("arbitrary"),
        name="outproj",
    )(*args)


def _top2_sublanes(vals, sub):
    m1 = vals.max(axis=0, keepdims=True)
    i1 = jnp.where(vals == m1, sub, vals.shape[0]).min(axis=0, keepdims=True)
    rest = jnp.where(sub == i1, -jnp.inf, vals)
    m2 = rest.max(axis=0, keepdims=True)
    i2 = jnp.where(rest == m2, sub, vals.shape[0]).min(axis=0, keepdims=True)
    return m1, i1, m2, i2


def _router_kernel(h_ref, wr_ref, rb_ref, e_ref, w_ref, rank_ref, cnt_ref, run_ref):
    i = pl.program_id(0)
    tm = ROW_TILE
    epg = EXPERTS_PER_GROUP

    @pl.when(i == 0)
    def _():
        run_ref[...] = jnp.zeros_like(run_ref)

    h = h_ref[...]
    h_hi = h.astype(BF16)
    h_mid = (h - h_hi.astype(F32)).astype(BF16)
    parts = _dot_nt(wr_ref[...], h_hi)
    parts_mid = _dot_nt(wr_ref[0:2 * N_EXPERTS, :], h_mid)
    logits = (parts[0:N_EXPERTS] + parts[N_EXPERTS:2 * N_EXPERTS] + parts[2 * N_EXPERTS:]
              + parts_mid[0:N_EXPERTS] + parts_mid[N_EXPERTS:])
    scores = _sigmoid(logits)
    biased = scores + jnp.concatenate([rb_ref[...]] * (tm // LANES), axis=1)
    sub = lax.broadcasted_iota(jnp.int32, (epg, tm), 0)

    best = None
    for g in range(N_EXPERT_GROUPS):
        m1, _, m2, _ = _top2_sublanes(biased[g * epg:(g + 1) * epg], sub)
        gs = m1 + m2
        if best is None:
            best, grp = gs, jnp.zeros((1, tm), jnp.int32)
            bsel, ssel = biased[0:epg], scores[0:epg]
        else:
            better = gs > best
            best = jnp.where(better, gs, best)
            grp = jnp.where(better, g, grp)
            bsel = jnp.where(better, biased[g * epg:(g + 1) * epg], bsel)
            ssel = jnp.where(better, scores[g * epg:(g + 1) * epg], ssel)
    _, i1, _, i2 = _top2_sublanes(bsel, sub)
    w1 = jnp.where(sub == i1, ssel, 0.0).sum(axis=0, keepdims=True)
    w2 = jnp.where(sub == i2, ssel, 0.0).sum(axis=0, keepdims=True)
    tot = w1 + w2
    e1 = grp * epg + i1
    e2 = grp * epg + i2

    eiota = lax.broadcasted_iota(jnp.int32, (N_EXPERTS, tm), 0)
    oh1 = (eiota == e1).astype(F32)
    oh2 = (eiota == e2).astype(F32)
    ohb = oh1 + oh2
    before = (lax.broadcasted_iota(jnp.int32, (tm, tm), 0) < lax.broadcasted_iota(jnp.int32, (tm, tm), 1))
    prefix = jnp.dot(ohb.astype(BF16), before.astype(BF16), preferred_element_type=F32)
    pos = run_ref[...] + prefix
    r1 = (oh1 * pos).sum(axis=0, keepdims=True)
    r2 = (oh2 * pos).sum(axis=0, keepdims=True)
    run_ref[...] = run_ref[...] + ohb.sum(axis=1, keepdims=True)

    e_ref[...] = jnp.concatenate([e1, e2], axis=0)
    w_ref[...] = jnp.concatenate([w1 / tot, w2 / tot], axis=0)
    rank_ref[...] = jnp.concatenate([r1, r2], axis=0).astype(jnp.int32)
    cnt_ref[...] = run_ref[:, 0:LANES]


def _router(h2, wr_t, rb_b):
    t = h2.shape[0]
    d = D_MODEL
    row2 = pl.BlockSpec((TOP_K, ROW_TILE), lambda i: (0, i))
    return pl.pallas_call(
        _router_kernel,
        grid=(t // ROW_TILE,),
        in_specs=[pl.BlockSpec((ROW_TILE, d), lambda i: (i, 0)),
                  pl.BlockSpec((3 * N_EXPERTS, d), lambda i: (0, 0)),
                  pl.BlockSpec((N_EXPERTS, LANES), lambda i: (0, 0))],
        out_specs=[row2, row2, row2, pl.BlockSpec((N_EXPERTS, LANES), lambda i: (0, 0))],
        out_shape=[jax.ShapeDtypeStruct((TOP_K, t), jnp.int32),
                   jax.ShapeDtypeStruct((TOP_K, t), F32),
                   jax.ShapeDtypeStruct((TOP_K, t), jnp.int32),
                   jax.ShapeDtypeStruct((N_EXPERTS, LANES), F32)],
        scratch_shapes=[pltpu.VMEM((N_EXPERTS, ROW_TILE), F32)],
        compiler_params=_params("arbitrary"),
        name="router",
    )(h2, wr_t, rb_b)


def _row_copy(src, row, dst, sem):
    return pltpu.make_async_copy(src.at[pl.ds(row, 1), :], dst, sem)


DISPATCH_SLOTS = 3


def _dispatch_kernel(slot_ref, h_hbm, xs_hbm, hbuf, lsem, ssem, *, n_tok):
    i = pl.program_id(0)
    nt = pl.num_programs(0)
    tm = ROW_TILE

    def load(blk):
        row0 = pl.multiple_of(blk * tm, tm)
        return pltpu.make_async_copy(h_hbm.at[pl.ds(row0, tm), :], hbuf.at[blk % DISPATCH_SLOTS],
                                     lsem.at[blk % DISPATCH_SLOTS])

    def wait_rows(par):
        for _ in range(TOP_K):
            pltpu.make_async_copy(hbuf.at[0], xs_hbm.at[pl.ds(0, tm), :], ssem.at[par]).wait()

    @pl.when(i == 0)
    def _():
        load(0).start()

    @pl.when(i + 1 < nt)
    def _():
        load(i + 1).start()

    load(i).wait()
    par = i % 2
    cur = hbuf.at[i % DISPATCH_SLOTS]
    for r in range(tm):
        for k in range(TOP_K):
            dst = xs_hbm.at[pl.ds(slot_ref[k * n_tok + i * tm + r], 1), :]
            pltpu.make_async_copy(cur.at[pl.ds(r, 1), :], dst, ssem.at[par]).start(priority=k)

    @pl.when(i > 0)
    def _():
        wait_rows(1 - par)

    @pl.when(i == nt - 1)
    def _():
        wait_rows(par)


def _dispatch(h2, slots, rows):
    t, d = h2.shape
    return pl.pallas_call(
        functools.partial(_dispatch_kernel, n_tok=t),
        grid_spec=pltpu.PrefetchScalarGridSpec(
            num_scalar_prefetch=1,
            grid=(t // ROW_TILE,),
            in_specs=[pl.BlockSpec(memory_space=pl.ANY)],
            out_specs=pl.BlockSpec(memory_space=pl.ANY),
            scratch_shapes=[pltpu.VMEM((DISPATCH_SLOTS, ROW_TILE, d), F32),
                            pltpu.SemaphoreType.DMA((DISPATCH_SLOTS,)),
                            pltpu.SemaphoreType.DMA((2,))]),
        out_shape=jax.ShapeDtypeStruct((rows, d), F32),
        compiler_params=_params("arbitrary"),
        name="dispatch",
    )(slots, h2)


def _experts_kernel(ts_ref, nu_ref, h_hbm, wg_hbm, wu_hbm, wd_hbm, ys_hbm,
                    xbuf, ybuf, gsem, ysem, wg_f, wu_f, wd_f, wsem, wg_bf, wu_bf, wd_bf, *, layer):
    e = pl.program_id(0)
    ne = pl.num_programs(0)
    n_used = nu_ref[0]
    tm = MOE_TILE
    tsub = tm

    def has_rows(ex):
        return ts_ref[ex + 1] > ts_ref[ex]

    def weight_copies(ex, wslot):
        return [pltpu.make_async_copy(src.at[layer, ex], dst.at[wslot], wsem.at[wslot])
                for src, dst in ((wg_hbm, wg_f), (wu_hbm, wu_f), (wd_hbm, wd_f))]

    def start_weights(ex, wslot):
        for cp in weight_copies(ex, wslot):
            cp.start(priority=1)

    def x_load(tile, slot):
        row0 = pl.multiple_of(tile * tsub, tsub)
        return pltpu.make_async_copy(h_hbm.at[pl.ds(row0, tsub), :], xbuf.at[slot], gsem.at[slot])

    def gather(tile, slot):
        x_load(tile, slot).start()

    def wait_gather(slot):
        x_load(0, slot).wait()

    def y_store(tile, slot):
        row0 = pl.multiple_of(tile * tsub, tsub)
        return pltpu.make_async_copy(ybuf.at[slot], ys_hbm.at[pl.ds(row0, tsub), :], ysem.at[slot])

    wslot = e % 2

    @pl.when((e == 0) & has_rows(0))
    def _():
        start_weights(0, 0)

    @pl.when((e == 0) & (n_used > 0))
    def _():
        gather(0, 0)

    nxt = jnp.minimum(e + 1, ne - 1)

    @pl.when((e + 1 < ne) & has_rows(nxt))
    def _():
        start_weights(nxt, 1 - wslot)

    @pl.when(has_rows(e))
    def _():
        for cp in weight_copies(e, wslot):
            cp.wait()
        wg_bf[...] = wg_f[wslot].astype(BF16)
        wu_bf[...] = wu_f[wslot].astype(BF16)
        wd_bf[...] = wd_f[wslot].astype(BF16)

    def tile_body(g, carry):
        slot = g % 2

        @pl.when(g >= 2)
        def _():
            y_store(g - 2, slot).wait()

        wait_gather(slot)
        x = xbuf[slot].astype(BF16)
        gather(jnp.minimum(g + 1, n_used - 1), 1 - slot)
        gate = jnp.dot(x, wg_bf[...], preferred_element_type=F32)
        up = jnp.dot(x, wu_bf[...], preferred_element_type=F32)
        hid = (gate * _sigmoid(gate)) * up
        ybuf[slot] = jnp.dot(hid.astype(BF16), wd_bf[...], preferred_element_type=F32)
        y_store(g, slot).start(priority=1)
        return carry

    lax.fori_loop(ts_ref[e], ts_ref[e + 1], tile_body, 0)

    @pl.when((e == pl.num_programs(0) - 1) & (n_used > 0))
    def _():
        wait_gather(n_used % 2)
        y_store(n_used - 1, (n_used - 1) % 2).wait()

        @pl.when(n_used >= 2)
        def _():
            y_store(n_used - 2, n_used % 2).wait()


def _experts(xs, tile_start, n_used, w_gate, w_up, w_down, layer):
    d = D_MODEL
    tm = MOE_TILE
    rows = xs.shape[0]

    hbm = pl.BlockSpec(memory_space=pl.ANY)
    return pl.pallas_call(
        functools.partial(_experts_kernel, layer=layer),
        grid_spec=pltpu.PrefetchScalarGridSpec(
            num_scalar_prefetch=2,
            grid=(N_EXPERTS,),
            in_specs=[hbm, hbm, hbm, hbm],
            out_specs=hbm,
            scratch_shapes=[pltpu.VMEM((2, tm, d), F32),
                            pltpu.VMEM((2, tm, d), F32),
                            pltpu.SemaphoreType.DMA((2,)),
                            pltpu.SemaphoreType.DMA((2,)),
                            pltpu.VMEM((2, d, EXPERT_FF), F32),
                            pltpu.VMEM((2, d, EXPERT_FF), F32),
                            pltpu.VMEM((2, EXPERT_FF, d), F32),
                            pltpu.SemaphoreType.DMA((2,)),
                            pltpu.VMEM((d, EXPERT_FF), BF16),
                            pltpu.VMEM((d, EXPERT_FF), BF16),
                            pltpu.VMEM((EXPERT_FF, d), BF16)]),
        out_shape=jax.ShapeDtypeStruct((rows, d), F32),
        compiler_params=_params("arbitrary"),
        name="experts",
    )(tile_start, n_used, xs, w_gate, w_up, w_down)


def _combine_kernel(slot_ref, ys_hbm, w_ref, x_ref, mod_ref, g_ref, b_ref, o_ref, ybuf, sem, *, n_tok):
    i = pl.program_id(0)
    nt = pl.num_programs(0)
    tm = ROW_TILE

    def issue(blk, buf):
        for r in range(tm):
            for k in range(TOP_K):
                row = slot_ref[k * n_tok + blk * tm + r]
                _row_copy(ys_hbm, row, ybuf.at[buf, k, pl.ds(r, 1), :], sem.at[buf]).start(priority=k)

    def wait(buf):
        for k in range(TOP_K):
            pltpu.make_async_copy(ys_hbm.at[pl.ds(0, tm), :], ybuf.at[buf, k], sem.at[buf]).wait()

    @pl.when(i == 0)
    def _():
        issue(0, 0)

    buf = i % 2
    wait(buf)
    issue(jnp.minimum(i + 1, nt - 1), 1 - buf)
    y = w_ref[:, 0:1] * ybuf[buf, 0] + w_ref[:, 1:2] * ybuf[buf, 1]
    t = DEEPNORM_ALPHA * x_ref[...] + mod_ref[MOD_G2:MOD_G2 + 1, :] * y
    o_ref[...] = _layer_norm(t, g_ref[...], b_ref[...])

    @pl.when(i == nt - 1)
    def _():
        wait(1 - buf)


def _combine(ys, slots, w_tok, x1, mods, n_lat_blocks, ln_g, ln_b):
    t = x1.shape[0]
    d = D_MODEL
    return pl.pallas_call(
        functools.partial(_combine_kernel, n_tok=t),
        grid_spec=pltpu.PrefetchScalarGridSpec(
            num_scalar_prefetch=1,
            grid=(t // ROW_TILE,),
            in_specs=[pl.BlockSpec(memory_space=pl.ANY),
                      pl.BlockSpec((ROW_TILE, TOP_K), lambda i, s: (i, 0)),
                      pl.BlockSpec((ROW_TILE, d), lambda i, s: (i, 0)),
                      pl.BlockSpec((None, 8, d), lambda i, s: (jnp.where(i < n_lat_blocks, 1, 0), 0, 0)),
                      pl.BlockSpec((1, d), lambda i, s: (0, 0)),
                      pl.BlockSpec((1, d), lambda i, s: (0, 0))],
            out_specs=pl.BlockSpec((ROW_TILE, d), lambda i, s: (i, 0)),
            scratch_shapes=[pltpu.VMEM((2, TOP_K, ROW_TILE, d), F32),
                            pltpu.SemaphoreType.DMA((2,))]),
        out_shape=jax.ShapeDtypeStruct((t, d), F32),
        compiler_params=_params("arbitrary"),
        name="combine",
    )(slots, ys, w_tok, x1, mods, ln_g, ln_b)


def _moe(h2, x1, mods, n_lat_blocks, wr_t, rb_b, w_gate, w_up, w_down, layer, ln_g, ln_b):
    t = x1.shape[0]
    tm = MOE_TILE
    e_idx, w_tok, rank, cnt = _router(h2, wr_t, rb_b)
    counts = cnt[:, 0].astype(jnp.int32)
    tiles_per = (counts + tm - 1) // tm
    tile_end = jnp.cumsum(tiles_per)
    n_used = tile_end[-1]
    row_off = (tile_end - tiles_per) * tm
    experts = jnp.arange(N_EXPERTS, dtype=jnp.int32)
    slots = jnp.sum(jnp.where(e_idx[:, :, None] == experts, row_off, 0), axis=-1) + rank
    max_tiles = (TOP_K * t + N_EXPERTS * (tm - 1)) // tm + 1
    tile_start = jnp.concatenate([tile_end - tiles_per, n_used.reshape(1)]).astype(jnp.int32)
    slots = slots.reshape(-1)
    xs = _dispatch(h2, slots, max_tiles * tm)
    ys = _experts(xs, tile_start, n_used.reshape(1).astype(jnp.int32), w_gate, w_up, w_down, layer)
    return _combine(ys, slots, w_tok.T, x1, mods, n_lat_blocks, ln_g, ln_b)


def _rope_tables(n):
    t = np.arange(n)
    row = (t // GRID_W).astype(np.float64)
    col = (t % GRID_W).astype(np.float64)
    n_freq = HEAD_DIM // 4
    inv_freq = ROPE_BASE ** (-np.arange(n_freq, dtype=np.float64) / n_freq)
    ang = np.concatenate([row[:, None] * inv_freq, col[:, None] * inv_freq], axis=-1)
    cos, sin = np.cos(ang), np.sin(ang)
    return (np.concatenate([cos, cos], axis=-1).astype(np.float32),
            np.concatenate([-sin, sin], axis=-1).astype(np.float32))


def kernel(x, c, ctx, c_ctx, w_mod, b_mod, w_in, attn_sink, na_rpb, sgu_ln_g, sgu_ln_b, sgu_w, sgu_b,
           w_out, ln1_g, ln1_b, w_router, router_bias, w_gate, w_up, w_down, ln2_g, ln2_b):
    batch, n, d = x.shape
    lctx = ctx.shape[1]
    assert batch == 1 and d == D_MODEL and n % ROW_TILE == 0 and lctx % ROW_TILE == 0
    n_lat_blocks = n // ROW_TILE

    mods = _modulation(c, c_ctx, w_mod, b_mod).reshape(DEPTH, 8, 6, d)
    mod_lat = jnp.pad(mods[:, 0], ((0, 0), (0, 2), (0, 0)))
    mod_ctx = jnp.pad(mods[:, 1], ((0, 0), (0, 2), (0, 0)))
    cos, sin = _rope_tables(n)
    wr_f = w_router.T
    def bf16_part(v):
        return lax.bitcast_convert_type(lax.bitcast_convert_type(v, jnp.int32) & jnp.int32(-65536), F32)

    wr_hi = bf16_part(wr_f)
    wr_mid = bf16_part(wr_f - wr_hi)
    wr_lo = wr_f - wr_hi - wr_mid
    wr_t = jnp.concatenate([wr_hi, wr_mid, wr_lo], axis=0).astype(BF16)
    rb_b = jnp.broadcast_to(router_bias.reshape(N_EXPERTS, 1), (N_EXPERTS, LANES))

    x_lat = x[0]
    x_ctx, ctx_row0 = ctx[0], 0
    w_in_bf = w_in.astype(BF16)
    w_out_bf = w_out.astype(BF16)
    bias_tabs = _na_bias_tables(na_rpb)
    for l in range(DEPTH):
        last = l == DEPTH - 1
        sgu_w_bf = sgu_w[l].astype(BF16)
        sgu_b_b = jnp.broadcast_to(sgu_b[l][:, :, None], (C_GROUPS, CHUNK, LANES))
        ln_g_c, ln_b_c = sgu_ln_g[l].reshape(1, C_W), sgu_ln_b[l].reshape(1, C_W)
        g1, b1 = ln1_g[l].reshape(1, d), ln1_b[l].reshape(1, d)
        g2, b2 = ln2_g[l].reshape(1, d), ln2_b[l].reshape(1, d)

        p = _proj(x_lat, 0, n, mod_lat[l], cos, sin, w_in_bf, l, rope=True)
        pc = _proj(x_ctx, ctx_row0, lctx, mod_ctx[l], cos, sin, w_in_bf, l, rope=False)
        o_a = _attn_a(p, pc, attn_sink[l], latent=True)
        o_b = _attn_b(p, pc, bias_tabs, l, latent=True)
        o_c = _sgu(p, ln_g_c, ln_b_c, sgu_w_bf, sgu_b_b)
        mod_both = jnp.stack([mod_ctx[l], mod_lat[l]])
        if last:
            x1, h2 = _outproj(o_a, o_b, o_c, w_out_bf, l, x_lat, 0, mod_lat[l], g1, b1, n, 0, None)
            x_lat = _moe(h2, x1, mod_both, n_lat_blocks, wr_t, rb_b, w_gate, w_up, w_down, l, g2, b2)
        else:
            oc_a = _attn_a(pc, pc, attn_sink[l], latent=False)
            oc_b = _attn_b(pc, pc, bias_tabs, l, latent=False)
            oc_c = _sgu(pc, ln_g_c, ln_b_c, sgu_w_bf, sgu_b_b)
            total = n + lctx
            prev = _outproj(o_a, o_b, o_c, w_out_bf, l, x_lat, 0, mod_lat[l], g1, b1, total, 0, None)
            x1, h2 = _outproj(oc_a, oc_b, oc_c, w_out_bf, l, x_ctx, ctx_row0, mod_ctx[l], g1, b1, total, n,
                              prev)
            x_all = _moe(h2, x1, mod_both, n_lat_blocks, wr_t, rb_b, w_gate, w_up, w_down, l, g2, b2)
            x_lat, x_ctx, ctx_row0 = x_all, x_all, n
    return x_lat.reshape(batch, n, d)
```

```python
import functools

import numpy as np
import jax
import jax.numpy as jnp
from jax import lax
from jax.experimental import pallas as pl
from jax.experimental.pallas import tpu as pltpu

F32 = jnp.float32
BF16 = jnp.bfloat16

D_MODEL = 2048
DEPTH = 2
GRID_W = 64
HEAD_DIM = 128
A_HEADS = 6
A_KV_HEADS = 2
A_GROUP = A_HEADS // A_KV_HEADS
A_BLOCK = 128
B_HEADS = 6
NA_ROWS = 8
NA_COLS = 16
C_GROUPS = 4
C_W = C_GROUPS * HEAD_DIM
CHUNK = 128
N_EXPERTS = 32
N_EXPERT_GROUPS = 4
EXPERTS_PER_GROUP = N_EXPERTS // N_EXPERT_GROUPS
TOP_K = 2
EXPERT_FF = 512
ROPE_BASE = 10000.0
LN_EPS = 1e-5
NEG_INF = -1e30
DEEPNORM_ALPHA = (2 * DEPTH) ** 0.25
ATTN_SCALE = HEAD_DIM ** -0.5

A_Q_W = A_HEADS * HEAD_DIM
A_KV_W = A_KV_HEADS * HEAD_DIM
B_W = B_HEADS * HEAD_DIM
OFF_AK = A_Q_W
OFF_AV = OFF_AK + A_KV_W
OFF_BQ = OFF_AV + A_KV_W
OFF_BK = OFF_BQ + B_W
OFF_BV = OFF_BK + B_W
OFF_C = OFF_BV + B_W
IN_COLS = OFF_C + 2 * C_W

VMEM_LIMIT_BYTES = 56 * 1024 * 1024
LANES = 128

ROW_TILE = 256
MATMUL_ROW_TILE = 512
PROJ_COL_TILE = 512
MOD_COL_TILE = 1024
MOE_TILE = 256

MOD_SH1, MOD_SC1, MOD_G1, MOD_SH2, MOD_SC2, MOD_G2 = range(6)


def _params(*sem):
    return pltpu.CompilerParams(dimension_semantics=sem, vmem_limit_bytes=VMEM_LIMIT_BYTES)


def _layer_norm(t, g, b):
    mu = jnp.mean(t, axis=-1, keepdims=True)
    d = t - mu
    var = jnp.mean(d * d, axis=-1, keepdims=True)
    return d * lax.rsqrt(var + LN_EPS) * g + b


def _sigmoid(v):
    return 1.0 / (1.0 + jnp.exp(-v))


def _dot_nt(a, b):
    return lax.dot_general(a, b, (((1,), (1,)), ((), ())), preferred_element_type=F32)


def _mod_kernel(c_ref, w_ref, b_ref, o_ref):
    w = w_ref[...]
    reps = w.shape[1] // LANES
    rows = []
    for r in range(2):
        cv = c_ref[r]
        s = cv * _sigmoid(cv)
        sb = jnp.concatenate([s] * reps, axis=1)
        rows.append(jnp.sum(w * sb, axis=0, keepdims=True) + b_ref[...])
    rows.append(jnp.zeros((6, w.shape[1]), F32))
    o_ref[...] = jnp.concatenate(rows, axis=0)


def _modulation(c, c_ctx, w_mod, b_mod):
    d = D_MODEL
    cb = jnp.stack([jnp.broadcast_to(c.reshape(d, 1), (d, LANES)),
                    jnp.broadcast_to(c_ctx.reshape(d, 1), (d, LANES))])
    n_out = 6 * d
    return pl.pallas_call(
        _mod_kernel,
        grid=(DEPTH, n_out // MOD_COL_TILE),
        in_specs=[pl.BlockSpec((2, d, LANES), lambda l, j: (0, 0, 0)),
                  pl.BlockSpec((None, d, MOD_COL_TILE), lambda l, j: (l, 0, j)),
                  pl.BlockSpec((None, 1, MOD_COL_TILE), lambda l, j: (l, 0, j))],
        out_specs=pl.BlockSpec((None, 8, MOD_COL_TILE), lambda l, j: (l, 0, j)),
        out_shape=jax.ShapeDtypeStruct((DEPTH, 8, n_out), F32),
        compiler_params=_params("arbitrary", "arbitrary"),
        name="modulation",
    )(cb, w_mod, b_mod.reshape(DEPTH, 1, n_out))


def _gelu_tanh(v):
    return 0.5 * v * (1.0 + jnp.tanh(np.sqrt(2.0 / np.pi).astype(np.float32) * (v + 0.044715 * (v * v * v))))


def _proj_kernel(x_ref, mod_ref, cos_ref, sin_ref, w_ref, o_ref, *, rope):
    x = x_ref[...]
    h = (x * (1.0 + mod_ref[MOD_SC1:MOD_SC1 + 1, :]) + mod_ref[MOD_SH1:MOD_SH1 + 1, :]).astype(BF16)
    tn = PROJ_COL_TILE
    for j in range(IN_COLS // tn):
        c0 = j * tn
        acc = jnp.dot(h, w_ref[:, c0:c0 + tn], preferred_element_type=F32)
        if c0 < OFF_AV:
            if rope:
                cos = cos_ref[...]
                sin = sin_ref[...]
                parts = []
                for hh in range(tn // HEAD_DIM):
                    a = acc[:, hh * HEAD_DIM:(hh + 1) * HEAD_DIM]
                    parts.append(a * cos + pltpu.roll(a, HEAD_DIM // 2, 1) * sin)
                acc = jnp.concatenate(parts, axis=1)
        elif c0 >= OFF_C:
            acc = _gelu_tanh(acc)
        o_ref[:, c0:c0 + tn] = acc.astype(BF16)


def _matmul_row_tile(*row_counts):
    return MATMUL_ROW_TILE if all(r % MATMUL_ROW_TILE == 0 for r in row_counts) else ROW_TILE


def _proj(x, x_row0, rows, mod, cos, sin, w_bf, layer, rope):
    d = D_MODEL
    tm = _matmul_row_tile(rows, x_row0)
    x_off = x_row0 // tm
    return pl.pallas_call(
        functools.partial(_proj_kernel, rope=rope),
        grid=(rows // tm,),
        in_specs=[pl.BlockSpec((tm, d), lambda i: (i + x_off, 0)),
                  pl.BlockSpec((8, d), lambda i: (0, 0)),
                  pl.BlockSpec((tm, HEAD_DIM), lambda i: (i, 0)),
                  pl.BlockSpec((tm, HEAD_DIM), lambda i: (i, 0)),
                  pl.BlockSpec((None, d, IN_COLS), lambda i: (layer, 0, 0), pipeline_mode=pl.Buffered(1))],
        out_specs=pl.BlockSpec((tm, IN_COLS), lambda i: (i, 0)),
        out_shape=jax.ShapeDtypeStruct((rows, IN_COLS), BF16),
        compiler_params=_params("arbitrary"),
        name="proj_rope" if rope else "proj_ctx",
    )(x, mod, cos, sin, w_bf)


LOG2E = 1.4426950408889634
LOGIT_SCALE = ATTN_SCALE * LOG2E


def _softmax_pv(problems):
    ms = []
    for s_parts, _, sink in problems:
        m = s_parts[0].max(axis=-1, keepdims=True)
        for s in s_parts[1:]:
            m = jnp.maximum(m, s.max(axis=-1, keepdims=True))
        ms.append(m if sink is None else jnp.maximum(m, sink))
    es = [[jnp.exp2(s - m).astype(BF16) for s in s_parts] for (s_parts, _, _), m in zip(problems, ms)]
    outs = []
    for (_, v_parts, sink), m, e_parts in zip(problems, ms, es):
        acc = None
        for e, v in zip(e_parts, v_parts):
            v1 = jnp.concatenate([v, jnp.ones((v.shape[0], LANES), BF16)], axis=1)
            pv = jnp.dot(e, v1, preferred_element_type=F32)
            acc = pv if acc is None else acc + pv
        denom = acc[:, HEAD_DIM:HEAD_DIM + 1]
        if sink is not None:
            denom = denom + jnp.exp2(sink - m)
        outs.append(acc[:, :HEAD_DIM] / denom)
    return outs


A_STEP_BLOCKS = 2


def _attn_a_kernel(sink_ref, q_ref, *refs, latent):
    nband = A_STEP_BLOCKS + 2
    k_refs, v_refs = refs[:nband], refs[nband:2 * nband]
    kx_ref, vx_ref, o_ref, mask_ref = refs[2 * nband:]
    i = pl.program_id(0)
    nb = pl.num_programs(0) * A_STEP_BLOCKS
    nq = A_GROUP * A_BLOCK

    if latent:
        @pl.when(i == 0)
        def _():
            qi = lax.broadcasted_iota(jnp.int32, (nq, 3 * A_BLOCK), 0) % A_BLOCK
            jj = lax.broadcasted_iota(jnp.int32, (nq, 3 * A_BLOCK), 1)
            ok = (jj >= qi) & (jj <= qi + 2 * A_BLOCK)
            mask_ref[0] = jnp.where(ok & (jj >= A_BLOCK), 0.0, NEG_INF).astype(F32)
            mask_ref[1] = jnp.where(ok, 0.0, NEG_INF).astype(F32)
            mask_ref[2] = jnp.where(ok & (jj < 2 * A_BLOCK), 0.0, NEG_INF).astype(F32)

    problems = []
    for kh in range(A_KV_HEADS):
        ks = slice(kh * HEAD_DIM, (kh + 1) * HEAD_DIM)
        sink = jnp.concatenate([jnp.full((A_BLOCK, 1), sink_ref[kh * A_GROUP + g] * LOG2E, F32)
                                for g in range(A_GROUP)], axis=0)
        for j in range(A_STEP_BLOCKS):
            rs = slice(j * A_BLOCK, (j + 1) * A_BLOCK)
            q = jnp.concatenate([q_ref[rs, (kh * A_GROUP + g) * HEAD_DIM:(kh * A_GROUP + g + 1) * HEAD_DIM]
                                 for g in range(A_GROUP)], axis=0)
            s_parts = [_dot_nt(q, kx_ref[:, ks]) * LOGIT_SCALE]
            v_parts = [vx_ref[:, ks]]
            if latent:
                blk = i * A_STEP_BLOCKS + j
                which = jnp.where(blk == 0, 0, jnp.where(blk == nb - 1, 2, 1))
                kband = jnp.concatenate([r[:, ks] for r in k_refs[j:j + 3]], axis=0)
                vband = jnp.concatenate([r[:, ks] for r in v_refs[j:j + 3]], axis=0)
                s_parts.append(_dot_nt(q, kband) * LOGIT_SCALE + mask_ref[which])
                v_parts.append(vband)
            problems.append((s_parts, v_parts, sink))
    outs = _softmax_pv(problems)
    for kh in range(A_KV_HEADS):
        for j in range(A_STEP_BLOCKS):
            out = outs[kh * A_STEP_BLOCKS + j]
            for g in range(A_GROUP):
                h = kh * A_GROUP + g
                o_ref[j * A_BLOCK:(j + 1) * A_BLOCK, h * HEAD_DIM:(h + 1) * HEAD_DIM] = (
                    out[g * A_BLOCK:(g + 1) * A_BLOCK].astype(BF16))


def _attn_a(p, pc, sink, latent):
    rows = p.shape[0]
    nb = rows // A_BLOCK
    assert nb >= 2 and nb % A_STEP_BLOCKS == 0
    kcol = OFF_AK // A_KV_W
    vcol = OFF_AV // A_KV_W
    tile = A_STEP_BLOCKS * A_BLOCK

    def band(col):
        return [pl.BlockSpec((A_BLOCK, A_KV_W),
                             lambda i, s, sh=shift: (jnp.clip(i * A_STEP_BLOCKS + sh, 0, nb - 1), col))
                for shift in range(-1, A_STEP_BLOCKS + 1)]

    lctx = pc.shape[0]
    nband = A_STEP_BLOCKS + 2
    return pl.pallas_call(
        functools.partial(_attn_a_kernel, latent=latent),
        grid_spec=pltpu.PrefetchScalarGridSpec(
            num_scalar_prefetch=1,
            grid=(nb // A_STEP_BLOCKS,),
            in_specs=([pl.BlockSpec((tile, A_Q_W), lambda i, s: (i, 0))] + band(kcol) + band(vcol)
                      + [pl.BlockSpec((lctx, A_KV_W), lambda i, s: (0, kcol)),
                         pl.BlockSpec((lctx, A_KV_W), lambda i, s: (0, vcol))]),
            out_specs=pl.BlockSpec((tile, A_Q_W), lambda i, s: (i, 0)),
            scratch_shapes=[pltpu.VMEM((3, A_GROUP * A_BLOCK, 3 * A_BLOCK), F32)]),
        out_shape=jax.ShapeDtypeStruct((rows, A_Q_W), BF16),
        compiler_params=_params("arbitrary"),
        name="attn_a_latent" if latent else "attn_a_ctx",
    )(*([sink] + [p] * (1 + 2 * nband) + [pc, pc]))


B_PAIR_W = 2 * HEAD_DIM
NA_TILE = 256
NA_GROUP_ROWS = NA_TILE // GRID_W
NA_WIN_ROWS = NA_ROWS + NA_GROUP_ROWS
NA_PAIRS = NA_WIN_ROWS // 2
NA_BIAS_OFFS = 2 * NA_ROWS
NA_BOTH, NA_LEFT, NA_RIGHT = range(3)


def _attn_b_kernel(*refs, latent, grid_rows):
    npair = B_HEADS // 2
    q_refs, k_refs, v_refs, kx_refs, vx_refs = (refs[j * npair:(j + 1) * npair] for j in range(5))
    bias_ref, o_ref = refs[5 * npair], refs[5 * npair + 1]
    g = pl.program_id(0)
    if latent:
        r_base = g * NA_GROUP_ROWS
        w0 = jnp.clip(r_base - NA_ROWS // 2, 0, grid_rows - NA_WIN_ROWS)
        start = pl.multiple_of(w0 * GRID_W, GRID_W)
    for hp in range(npair):
        problems = []
        for hh in range(2):
            hs = slice(hh * HEAD_DIM, (hh + 1) * HEAD_DIM)
            q = q_refs[hp][:, hs]
            s_parts = [_dot_nt(q, kx_refs[hp][:, hs]) * LOGIT_SCALE]
            v_parts = [vx_refs[hp][:, hs]]
            if latent:
                kwin = k_refs[hp][pl.ds(start, NA_WIN_ROWS * GRID_W), hs]
                vwin = v_refs[hp][pl.ds(start, NA_WIN_ROWS * GRID_W), hs]
                bias_rows = []
                for rr in range(NA_GROUP_ROWS):
                    r = r_base + rr
                    r0 = jnp.clip(r - NA_ROWS // 2, 0, grid_rows - NA_ROWS)
                    tiles = []
                    for jp in range(NA_PAIRS):
                        ka = w0 + 2 * jp
                        in_a = (ka >= r0) & (ka < r0 + NA_ROWS)
                        in_b = (ka + 1 >= r0) & (ka + 1 < r0 + NA_ROWS)
                        variant = jnp.where(in_a, jnp.where(in_b, NA_BOTH, NA_LEFT),
                                            jnp.where(in_b, NA_RIGHT, NA_LEFT))
                        off = jnp.where(in_a | in_b, jnp.clip(ka - r + NA_ROWS, 0, NA_BIAS_OFFS - 1), 0)
                        tiles.append(bias_ref[2 * hp + hh, variant, off])
                    bias_rows.append(jnp.concatenate(tiles, axis=1))
                bias = jnp.concatenate(bias_rows, axis=0)
                s_parts.append(_dot_nt(q, kwin) * LOGIT_SCALE + bias)
                v_parts.append(vwin)
            problems.append((s_parts, v_parts, None))
        for hh, out in enumerate(_softmax_pv(problems)):
            h = 2 * hp + hh
            o_ref[:, h * HEAD_DIM:(h + 1) * HEAD_DIM] = out.astype(BF16)


def _na_bias_tables(rpb):
    cols = np.arange(GRID_W)
    c0 = np.clip(cols - NA_COLS // 2, 0, GRID_W - NA_COLS)
    rel = cols[None, :] - cols[:, None] + NA_COLS - 1
    ok = (cols[None, :] >= c0[:, None]) & (cols[None, :] < c0[:, None] + NA_COLS)
    onehot = (rel[None] == np.arange(2 * NA_COLS - 1)[:, None, None]).astype(np.float32)
    t = jnp.einsum("lhrd,dqk->lhrqk", rpb, onehot, precision=lax.Precision.HIGHEST)
    t = jnp.where(ok, t * LOG2E, NEG_INF).astype(F32)
    t = jnp.pad(t, ((0, 0), (0, 0), (1, 1), (0, 0), (0, 0)), constant_values=NEG_INF)
    pairs = jnp.concatenate([t[:, :, :-1], t[:, :, 1:]], axis=-1)
    keep = np.ones((3, 1, 1, 2 * GRID_W), bool)
    keep[NA_LEFT, :, :, GRID_W:] = False
    keep[NA_RIGHT, :, :, :GRID_W] = False
    return jnp.where(keep, pairs[:, :, None], NEG_INF)


def _attn_b(p, pc, bias_tabs, layer, latent):
    rows = p.shape[0]
    lctx = pc.shape[0]
    assert not latent or (rows // GRID_W >= NA_WIN_ROWS and rows % NA_TILE == 0)
    tile = NA_TILE if latent else ROW_TILE
    npair = B_HEADS // 2
    qcol = OFF_BQ // B_PAIR_W
    kcol = OFF_BK // B_PAIR_W
    vcol = OFF_BV // B_PAIR_W

    def resident(nrows, col):
        return [pl.BlockSpec((nrows, B_PAIR_W), lambda g, c=col + hp: (0, c), pipeline_mode=pl.Buffered(1))
                for hp in range(npair)]

    return pl.pallas_call(
        functools.partial(_attn_b_kernel, latent=latent, grid_rows=rows // GRID_W),
        grid=(rows // tile,),
        in_specs=([pl.BlockSpec((tile, B_PAIR_W), lambda g, c=qcol + hp: (g, c)) for hp in range(npair)]
                  + resident(rows, kcol) + resident(rows, vcol) + resident(lctx, kcol) + resident(lctx, vcol)
                  + [pl.BlockSpec((None, B_HEADS, 3, NA_BIAS_OFFS, GRID_W, 2 * GRID_W),
                                  lambda g: (layer, 0, 0, 0, 0, 0), pipeline_mode=pl.Buffered(1))]),
        out_specs=pl.BlockSpec((tile, B_W), lambda g: (g, 0)),
        out_shape=jax.ShapeDtypeStruct((rows, B_W), BF16),
        compiler_params=_params("arbitrary"),
        name="attn_b_latent" if latent else "attn_b_ctx",
    )(*([p] * (3 * npair) + [pc] * (2 * npair) + [bias_tabs]))


def _sgu_kernel(u_ref, v_ref, g_ref, b_ref, w_ref, bs_ref, o_ref):
    for ch in range(ROW_TILE // CHUNK):
        rs = slice(ch * CHUNK, (ch + 1) * CHUNK)
        for grp in range(C_GROUPS):
            cs = slice(grp * HEAD_DIM, (grp + 1) * HEAD_DIM)
            vn = _layer_norm(v_ref[rs, cs].astype(F32), g_ref[:, cs], b_ref[:, cs])
            mixed = jnp.dot(w_ref[grp], vn.astype(BF16), preferred_element_type=F32) + bs_ref[grp]
            o_ref[rs, cs] = (u_ref[rs, cs].astype(F32) * mixed).astype(BF16)


def _sgu(p, ln_g, ln_b, w_bf, bs_b):
    rows = p.shape[0]
    ucol = OFF_C // C_W
    return pl.pallas_call(
        _sgu_kernel,
        grid=(rows // ROW_TILE,),
        in_specs=[pl.BlockSpec((ROW_TILE, C_W), lambda i: (i, ucol)),
                  pl.BlockSpec((ROW_TILE, C_W), lambda i: (i, ucol + 1)),
                  pl.BlockSpec((1, C_W), lambda i: (0, 0)),
                  pl.BlockSpec((1, C_W), lambda i: (0, 0)),
                  pl.BlockSpec((C_GROUPS, CHUNK, CHUNK), lambda i: (0, 0, 0)),
                  pl.BlockSpec((C_GROUPS, CHUNK, LANES), lambda i: (0, 0, 0))],
        out_specs=pl.BlockSpec((ROW_TILE, C_W), lambda i: (i, 0)),
        out_shape=jax.ShapeDtypeStruct((rows, C_W), BF16),
        compiler_params=_params("arbitrary"),
        name="sgu",
    )(p, p, ln_g, ln_b, w_bf, bs_b)


def _outproj_kernel(oa_ref, ob_ref, oc_ref, w_ref, x_ref, mod_ref, g_ref, b_ref, *rest):
    x1_ref, h2_ref = rest[-2], rest[-1]
    mix = jnp.dot(oa_ref[...], w_ref[0:A_Q_W, :], preferred_element_type=F32)
    mix += jnp.dot(ob_ref[...], w_ref[A_Q_W:A_Q_W + B_W, :], preferred_element_type=F32)
    mix += jnp.dot(oc_ref[...], w_ref[A_Q_W + B_W:, :], preferred_element_type=F32)
    t = DEEPNORM_ALPHA * x_ref[...] + mod_ref[MOD_G1:MOD_G1 + 1, :] * mix
    x1 = _layer_norm(t, g_ref[...], b_ref[...])
    x1_ref[...] = x1
    h2_ref[...] = x1 * (1.0 + mod_ref[MOD_SC2:MOD_SC2 + 1, :]) + mod_ref[MOD_SH2:MOD_SH2 + 1, :]


def _outproj(o_a, o_b, o_c, w_bf, layer, x, x_row0, mod, ln_g, ln_b, total_rows, out_row0, prev):
    rows = o_a.shape[0]
    d = D_MODEL
    tm = _matmul_row_tile(rows, x_row0, out_row0)
    x_off, out_off = x_row0 // tm, out_row0 // tm
    in_specs = [pl.BlockSpec((tm, A_Q_W), lambda i: (i, 0)),
                pl.BlockSpec((tm, B_W), lambda i: (i, 0)),
                pl.BlockSpec((tm, C_W), lambda i: (i, 0)),
                pl.BlockSpec((None, d, d), lambda i: (layer, 0, 0), pipeline_mode=pl.Buffered(1)),
                pl.BlockSpec((tm, d), lambda i: (i + x_off, 0)),
                pl.BlockSpec((8, d), lambda i: (0, 0)),
                pl.BlockSpec((1, d), lambda i: (0, 0)),
                pl.BlockSpec((1, d), lambda i: (0, 0))]
    args = [o_a, o_b, o_c, w_bf, x, mod, ln_g, ln_b]
    aliases = {}
    if prev is not None:
        in_specs += [pl.BlockSpec(memory_space=pl.ANY), pl.BlockSpec(memory_space=pl.ANY)]
        aliases = {len(args): 0, len(args) + 1: 1}
        args += list(prev)
    return pl.pallas_call(
        _outproj_kernel,
        grid=(rows // tm,),
        in_specs=in_specs,
        out_specs=[pl.BlockSpec((tm, d), lambda i: (i + out_off, 0))] * 2,
        out_shape=[jax.ShapeDtypeStruct((total_rows, d), F32)] * 2,
        input_output_aliases=aliases,
        compiler_params=_params("arbitrary"),
        name="outproj",
    )(*args)


def _top2_sublanes(vals, sub):
    m1 = vals.max(axis=0, keepdims=True)
    i1 = jnp.where(vals == m1, sub, vals.shape[0]).min(axis=0, keepdims=True)
    rest = jnp.where(sub == i1, -jnp.inf, vals)
    m2 = rest.max(axis=0, keepdims=True)
    i2 = jnp.where(rest == m2, sub, vals.shape[0]).min(axis=0, keepdims=True)
    return m1, i1, m2, i2


def _router_kernel(h_ref, wr_ref, rb_ref, e_ref, w_ref, rank_ref, cnt_ref, run_ref):
    i = pl.program_id(0)
    tm = ROW_TILE
    epg = EXPERTS_PER_GROUP

    @pl.when(i == 0)
    def _():
        run_ref[...] = jnp.zeros_like(run_ref)

    h = h_ref[...]
    h_hi = h.astype(BF16)
    h_mid = (h - h_hi.astype(F32)).astype(BF16)
    parts = _dot_nt(wr_ref[...], h_hi)
    parts_mid = _dot_nt(wr_ref[0:2 * N_EXPERTS, :], h_mid)
    logits = (parts[0:N_EXPERTS] + parts[N_EXPERTS:2 * N_EXPERTS] + parts[2 * N_EXPERTS:]
              + parts_mid[0:N_EXPERTS] + parts_mid[N_EXPERTS:])
    scores = _sigmoid(logits)
    biased = scores + jnp.concatenate([rb_ref[...]] * (tm // LANES), axis=1)
    sub = lax.broadcasted_iota(jnp.int32, (epg, tm), 0)

    best = None
    for g in range(N_EXPERT_GROUPS):
        m1, _, m2, _ = _top2_sublanes(biased[g * epg:(g + 1) * epg], sub)
        gs = m1 + m2
        if best is None:
            best, grp = gs, jnp.zeros((1, tm), jnp.int32)
            bsel, ssel = biased[0:epg], scores[0:epg]
        else:
            better = gs > best
            best = jnp.where(better, gs, best)
            grp = jnp.where(better, g, grp)
            bsel = jnp.where(better, biased[g * epg:(g + 1) * epg], bsel)
            ssel = jnp.where(better, scores[g * epg:(g + 1) * epg], ssel)
    _, i1, _, i2 = _top2_sublanes(bsel, sub)
    w1 = jnp.where(sub == i1, ssel, 0.0).sum(axis=0, keepdims=True)
    w2 = jnp.where(sub == i2, ssel, 0.0).sum(axis=0, keepdims=True)
    tot = w1 + w2
    e1 = grp * epg + i1
    e2 = grp * epg + i2

    eiota = lax.broadcasted_iota(jnp.int32, (N_EXPERTS, tm), 0)
    oh1 = (eiota == e1).astype(F32)
    oh2 = (eiota == e2).astype(F32)
    ohb = oh1 + oh2
    before = (lax.broadcasted_iota(jnp.int32, (tm, tm), 0) < lax.broadcasted_iota(jnp.int32, (tm, tm), 1))
    prefix = jnp.dot(ohb.astype(BF16), before.astype(BF16), preferred_element_type=F32)
    pos = run_ref[...] + prefix
    r1 = (oh1 * pos).sum(axis=0, keepdims=True)
    r2 = (oh2 * pos).sum(axis=0, keepdims=True)
    run_ref[...] = run_ref[...] + ohb.sum(axis=1, keepdims=True)

    e_ref[...] = jnp.concatenate([e1, e2], axis=0)
    w_ref[...] = jnp.concatenate([w1 / tot, w2 / tot], axis=0)
    rank_ref[...] = jnp.concatenate([r1, r2], axis=0).astype(jnp.int32)
    cnt_ref[...] = run_ref[:, 0:LANES]


def _router(h2, wr_t, rb_b):
    t = h2.shape[0]
    d = D_MODEL
    row2 = pl.BlockSpec((TOP_K, ROW_TILE), lambda i: (0, i))
    return pl.pallas_call(
        _router_kernel,
        grid=(t // ROW_TILE,),
        in_specs=[pl.BlockSpec((ROW_TILE, d), lambda i: (i, 0)),
                  pl.BlockSpec((3 * N_EXPERTS, d), lambda i: (0, 0)),
                  pl.BlockSpec((N_EXPERTS, LANES), lambda i: (0, 0))],
        out_specs=[row2, row2, row2, pl.BlockSpec((N_EXPERTS, LANES), lambda i: (0, 0))],
        out_shape=[jax.ShapeDtypeStruct((TOP_K, t), jnp.int32),
                   jax.ShapeDtypeStruct((TOP_K, t), F32),
                   jax.ShapeDtypeStruct((TOP_K, t), jnp.int32),
                   jax.ShapeDtypeStruct((N_EXPERTS, LANES), F32)],
        scratch_shapes=[pltpu.VMEM((N_EXPERTS, ROW_TILE), F32)],
        compiler_params=_params("arbitrary"),
        name="router",
    )(h2, wr_t, rb_b)


def _row_copy(src, row, dst, sem):
    return pltpu.make_async_copy(src.at[pl.ds(row, 1), :], dst, sem)


DISPATCH_SLOTS = 3


def _dispatch_kernel(slot_ref, h_hbm, xs_hbm, hbuf, lsem, ssem, *, n_tok):
    i = pl.program_id(0)
    nt = pl.num_programs(0)
    tm = ROW_TILE

    def load(blk):
        row0 = pl.multiple_of(blk * tm, tm)
        return pltpu.make_async_copy(h_hbm.at[pl.ds(row0, tm), :], hbuf.at[blk % DISPATCH_SLOTS],
                                     lsem.at[blk % DISPATCH_SLOTS])

    def wait_rows(par):
        for _ in range(TOP_K):
            pltpu.make_async_copy(hbuf.at[0], xs_hbm.at[pl.ds(0, tm), :], ssem.at[par]).wait()

    @pl.when(i == 0)
    def _():
        load(0).start()

    @pl.when(i + 1 < nt)
    def _():
        load(i + 1).start()

    load(i).wait()
    par = i % 2
    cur = hbuf.at[i % DISPATCH_SLOTS]
    for r in range(tm):
        for k in range(TOP_K):
            dst = xs_hbm.at[pl.ds(slot_ref[k * n_tok + i * tm + r], 1), :]
            pltpu.make_async_copy(cur.at[pl.ds(r, 1), :], dst, ssem.at[par]).start(priority=k)

    @pl.when(i > 0)
    def _():
        wait_rows(1 - par)

    @pl.when(i == nt - 1)
    def _():
        wait_rows(par)


def _dispatch(h2, slots, rows):
    t, d = h2.shape
    return pl.pallas_call(
        functools.partial(_dispatch_kernel, n_tok=t),
        grid_spec=pltpu.PrefetchScalarGridSpec(
            num_scalar_prefetch=1,
            grid=(t // ROW_TILE,),
            in_specs=[pl.BlockSpec(memory_space=pl.ANY)],
            out_specs=pl.BlockSpec(memory_space=pl.ANY),
            scratch_shapes=[pltpu.VMEM((DISPATCH_SLOTS, ROW_TILE, d), F32),
                            pltpu.SemaphoreType.DMA((DISPATCH_SLOTS,)),
                            pltpu.SemaphoreType.DMA((2,))]),
        out_shape=jax.ShapeDtypeStruct((rows, d), F32),
        compiler_params=_params("arbitrary"),
        name="dispatch",
    )(slots, h2)


def _experts_kernel(ts_ref, nu_ref, h_hbm, wg_hbm, wu_hbm, wd_hbm, ys_hbm,
                    xbuf, ybuf, gsem, ysem, wg_f, wu_f, wd_f, wsem, wg_bf, wu_bf, wd_bf, *, layer):
    e = pl.program_id(0)
    ne = pl.num_programs(0)
    n_used = nu_ref[0]
    tm = MOE_TILE
    tsub = tm

    def has_rows(ex):
        return ts_ref[ex + 1] > ts_ref[ex]

    def weight_copies(ex, wslot):
        return [pltpu.make_async_copy(src.at[layer, ex], dst.at[wslot], wsem.at[wslot])
                for src, dst in ((wg_hbm, wg_f), (wu_hbm, wu_f), (wd_hbm, wd_f))]

    def start_weights(ex, wslot):
        for cp in weight_copies(ex, wslot):
            cp.start(priority=1)

    def x_load(tile, slot):
        row0 = pl.multiple_of(tile * tsub, tsub)
        return pltpu.make_async_copy(h_hbm.at[pl.ds(row0, tsub), :], xbuf.at[slot], gsem.at[slot])

    def gather(tile, slot):
        x_load(tile, slot).start()

    def wait_gather(slot):
        x_load(0, slot).wait()

    def y_store(tile, slot):
        row0 = pl.multiple_of(tile * tsub, tsub)
        return pltpu.make_async_copy(ybuf.at[slot], ys_hbm.at[pl.ds(row0, tsub), :], ysem.at[slot])

    wslot = e % 2

    @pl.when((e == 0) & has_rows(0))
    def _():
        start_weights(0, 0)

    @pl.when((e == 0) & (n_used > 0))
    def _():
        gather(0, 0)

    nxt = jnp.minimum(e + 1, ne - 1)

    @pl.when((e + 1 < ne) & has_rows(nxt))
    def _():
        start_weights(nxt, 1 - wslot)

    @pl.when(has_rows(e))
    def _():
        for cp in weight_copies(e, wslot):
            cp.wait()
        wg_bf[...] = wg_f[wslot].astype(BF16)
        wu_bf[...] = wu_f[wslot].astype(BF16)
        wd_bf[...] = wd_f[wslot].astype(BF16)

    def tile_body(g, carry):
        slot = g % 2

        @pl.when(g >= 2)
        def _():
            y_store(g - 2, slot).wait()

        wait_gather(slot)
        x = xbuf[slot].astype(BF16)
        gather(jnp.minimum(g + 1, n_used - 1), 1 - slot)
        gate = jnp.dot(x, wg_bf[...], preferred_element_type=F32)
        up = jnp.dot(x, wu_bf[...], preferred_element_type=F32)
        hid = (gate * _sigmoid(gate)) * up
        ybuf[slot] = jnp.dot(hid.astype(BF16), wd_bf[...], preferred_element_type=F32)
        y_store(g, slot).start(priority=1)
        return carry

    lax.fori_loop(ts_ref[e], ts_ref[e + 1], tile_body, 0)

    @pl.when((e == pl.num_programs(0) - 1) & (n_used > 0))
    def _():
        wait_gather(n_used % 2)
        y_store(n_used - 1, (n_used - 1) % 2).wait()

        @pl.when(n_used >= 2)
        def _():
            y_store(n_used - 2, n_used % 2).wait()


def _experts(xs, tile_start, n_used, w_gate, w_up, w_down, layer):
    d = D_MODEL
    tm = MOE_TILE
    rows = xs.shape[0]

    hbm = pl.BlockSpec(memory_space=pl.ANY)
    return pl.pallas_call(
        functools.partial(_experts_kernel, layer=layer),
        grid_spec=pltpu.PrefetchScalarGridSpec(
            num_scalar_prefetch=2,
            grid=(N_EXPERTS,),
            in_specs=[hbm, hbm, hbm, hbm],
            out_specs=hbm,
            scratch_shapes=[pltpu.VMEM((2, tm, d), F32),
                            pltpu.VMEM((2, tm, d), F32),
                            pltpu.SemaphoreType.DMA((2,)),
                            pltpu.SemaphoreType.DMA((2,)),
                            pltpu.VMEM((2, d, EXPERT_FF), F32),
                            pltpu.VMEM((2, d, EXPERT_FF), F32),
                            pltpu.VMEM((2, EXPERT_FF, d), F32),
                            pltpu.SemaphoreType.DMA((2,)),
                            pltpu.VMEM((d, EXPERT_FF), BF16),
                            pltpu.VMEM((d, EXPERT_FF), BF16),
                            pltpu.VMEM((EXPERT_FF, d), BF16)]),
        out_shape=jax.ShapeDtypeStruct((rows, d), F32),
        compiler_params=_params("arbitrary"),
        name="experts",
    )(tile_start, n_used, xs, w_gate, w_up, w_down)


def _combine_kernel(slot_ref, ys_hbm, w_ref, x_ref, mod_ref, g_ref, b_ref, o_ref, ybuf, sem, *, n_tok):
    i = pl.program_id(0)
    nt = pl.num_programs(0)
    tm = ROW_TILE

    def issue(blk, buf):
        for r in range(tm):
            for k in range(TOP_K):
                row = slot_ref[k * n_tok + blk * tm + r]
                _row_copy(ys_hbm, row, ybuf.at[buf, k, pl.ds(r, 1), :], sem.at[buf]).start(priority=k)

    def wait(buf):
        for k in range(TOP_K):
            pltpu.make_async_copy(ys_hbm.at[pl.ds(0, tm), :], ybuf.at[buf, k], sem.at[buf]).wait()

    @pl.when(i == 0)
    def _():
        issue(0, 0)

    buf = i % 2
    wait(buf)
    issue(jnp.minimum(i + 1, nt - 1), 1 - buf)
    y = w_ref[:, 0:1] * ybuf[buf, 0] + w_ref[:, 1:2] * ybuf[buf, 1]
    t = DEEPNORM_ALPHA * x_ref[...] + mod_ref[MOD_G2:MOD_G2 + 1, :] * y
    o_ref[...] = _layer_norm(t, g_ref[...], b_ref[...])

    @pl.when(i == nt - 1)
    def _():
        wait(1 - buf)


def _combine(ys, slots, w_tok, x1, mods, n_lat_blocks, ln_g, ln_b):
    t = x1.shape[0]
    d = D_MODEL
    return pl.pallas_call(
        functools.partial(_combine_kernel, n_tok=t),
        grid_spec=pltpu.PrefetchScalarGridSpec(
            num_scalar_prefetch=1,
            grid=(t // ROW_TILE,),
            in_specs=[pl.BlockSpec(memory_space=pl.ANY),
                      pl.BlockSpec((ROW_TILE, TOP_K), lambda i, s: (i, 0)),
                      pl.BlockSpec((ROW_TILE, d), lambda i, s: (i, 0)),
                      pl.BlockSpec((None, 8, d), lambda i, s: (jnp.where(i < n_lat_blocks, 1, 0), 0, 0)),
                      pl.BlockSpec((1, d), lambda i, s: (0, 0)),
                      pl.BlockSpec((1, d), lambda i, s: (0, 0))],
            out_specs=pl.BlockSpec((ROW_TILE, d), lambda i, s: (i, 0)),
            scratch_shapes=[pltpu.VMEM((2, TOP_K, ROW_TILE, d), F32),
                            pltpu.SemaphoreType.DMA((2,))]),
        out_shape=jax.ShapeDtypeStruct((t, d), F32),
        compiler_params=_params("arbitrary"),
        name="combine",
    )(slots, ys, w_tok, x1, mods, ln_g, ln_b)


def _moe(h2, x1, mods, n_lat_blocks, wr_t, rb_b, w_gate, w_up, w_down, layer, ln_g, ln_b):
    t = x1.shape[0]
    tm = MOE_TILE
    e_idx, w_tok, rank, cnt = _router(h2, wr_t, rb_b)
    counts = cnt[:, 0].astype(jnp.int32)
    tiles_per = (counts + tm - 1) // tm
    tile_end = jnp.cumsum(tiles_per)
    n_used = tile_end[-1]
    row_off = (tile_end - tiles_per) * tm
    experts = jnp.arange(N_EXPERTS, dtype=jnp.int32)
    slots = jnp.sum(jnp.where(e_idx[:, :, None] == experts, row_off, 0), axis=-1) + rank
    max_tiles = (TOP_K * t + N_EXPERTS * (tm - 1)) // tm + 1
    tile_start = jnp.concatenate([tile_end - tiles_per, n_used.reshape(1)]).astype(jnp.int32)
    slots = slots.reshape(-1)
    xs = _dispatch(h2, slots, max_tiles * tm)
    ys = _experts(xs, tile_start, n_used.reshape(1).astype(jnp.int32), w_gate, w_up, w_down, layer)
    return _combine(ys, slots, w_tok.T, x1, mods, n_lat_blocks, ln_g, ln_b)


def _rope_tables(n):
    t = np.arange(n)
    row = (t // GRID_W).astype(np.float64)
    col = (t % GRID_W).astype(np.float64)
    n_freq = HEAD_DIM // 4
    inv_freq = ROPE_BASE ** (-np.arange(n_freq, dtype=np.float64) / n_freq)
    ang = np.concatenate([row[:, None] * inv_freq, col[:, None] * inv_freq], axis=-1)
    cos, sin = np.cos(ang), np.sin(ang)
    return (np.concatenate([cos, cos], axis=-1).astype(np.float32),
            np.concatenate([-sin, sin], axis=-1).astype(np.float32))


def kernel(x, c, ctx, c_ctx, w_mod, b_mod, w_in, attn_sink, na_rpb, sgu_ln_g, sgu_ln_b, sgu_w, sgu_b,
           w_out, ln1_g, ln1_b, w_router, router_bias, w_gate, w_up, w_down, ln2_g, ln2_b):
    batch, n, d = x.shape
    lctx = ctx.shape[1]
    assert batch == 1 and d == D_MODEL and n % ROW_TILE == 0 and lctx % ROW_TILE == 0
    n_lat_blocks = n // ROW_TILE

    mods = _modulation(c, c_ctx, w_mod, b_mod).reshape(DEPTH, 8, 6, d)
    mod_lat = jnp.pad(mods[:, 0], ((0, 0), (0, 2), (0, 0)))
    mod_ctx = jnp.pad(mods[:, 1], ((0, 0), (0, 2), (0, 0)))
    cos, sin = _rope_tables(n)
    wr_f = w_router.T
    def bf16_part(v):
        return lax.bitcast_convert_type(lax.bitcast_convert_type(v, jnp.int32) & jnp.int32(-65536), F32)

    wr_hi = bf16_part(wr_f)
    wr_mid = bf16_part(wr_f - wr_hi)
    wr_lo = wr_f - wr_hi - wr_mid
    wr_t = jnp.concatenate([wr_hi, wr_mid, wr_lo], axis=0).astype(BF16)
    rb_b = jnp.broadcast_to(router_bias.reshape(N_EXPERTS, 1), (N_EXPERTS, LANES))

    x_lat = x[0]
    x_ctx, ctx_row0 = ctx[0], 0
    w_in_bf = w_in.astype(BF16)
    w_out_bf = w_out.astype(BF16)
    bias_tabs = _na_bias_tables(na_rpb)
    for l in range(DEPTH):
        last = l == DEPTH - 1
        sgu_w_bf = sgu_w[l].astype(BF16)
        sgu_b_b = jnp.broadcast_to(sgu_b[l][:, :, None], (C_GROUPS, CHUNK, LANES))
        ln_g_c, ln_b_c = sgu_ln_g[l].reshape(1, C_W), sgu_ln_b[l].reshape(1, C_W)
        g1, b1 = ln1_g[l].reshape(1, d), ln1_b[l].reshape(1, d)
        g2, b2 = ln2_g[l].reshape(1, d), ln2_b[l].reshape(1, d)

        p = _proj(x_lat, 0, n, mod_lat[l], cos, sin, w_in_bf, l, rope=True)
        pc = _proj(x_ctx, ctx_row0, lctx, mod_ctx[l], cos, sin, w_in_bf, l, rope=False)
        o_a = _attn_a(p, pc, attn_sink[l], latent=True)
        o_b = _attn_b(p, pc, bias_tabs, l, latent=True)
        o_c = _sgu(p, ln_g_c, ln_b_c, sgu_w_bf, sgu_b_b)
        mod_both = jnp.stack([mod_ctx[l], mod_lat[l]])
        if last:
            x1, h2 = _outproj(o_a, o_b, o_c, w_out_bf, l, x_lat, 0, mod_lat[l], g1, b1, n, 0, None)
            x_lat = _moe(h2, x1, mod_both, n_lat_blocks, wr_t, rb_b, w_gate, w_up, w_down, l, g2, b2)
        else:
            oc_a = _attn_a(pc, pc, attn_sink[l], latent=False)
            oc_b = _attn_b(pc, pc, bias_tabs, l, latent=False)
            oc_c = _sgu(pc, ln_g_c, ln_b_c, sgu_w_bf, sgu_b_b)
            total = n + lctx
            prev = _outproj(o_a, o_b, o_c, w_out_bf, l, x_lat, 0, mod_lat[l], g1, b1, total, 0, None)
            x1, h2 = _outproj(oc_a, oc_b, oc_c, w_out_bf, l, x_ctx, ctx_row0, mod_ctx[l], g1, b1, total, n,
                              prev)
            x_all = _moe(h2, x1, mod_both, n_lat_blocks, wr_t, rb_b, w_gate, w_up, w_down, l, g2, b2)
            x_lat, x_ctx, ctx_row0 = x_all, x_all, n
    return x_lat.reshape(batch, n, d)
```

```python
import functools

import numpy as np
import jax
import jax.numpy as jnp
from jax import lax
from jax.experimental import pallas as pl
from jax.experimental.pallas import tpu as pltpu

F32 = jnp.float32
BF16 = jnp.bfloat16

D_MODEL = 2048
DEPTH = 2
GRID_W = 64
HEAD_DIM = 128
A_HEADS = 6
A_KV_HEADS = 2
A_GROUP = A_HEADS // A_KV_HEADS
A_BLOCK = 128
B_HEADS = 6
NA_ROWS = 8
NA_COLS = 16
C_GROUPS = 4
C_W = C_GROUPS * HEAD_DIM
CHUNK = 128
N_EXPERTS = 32
N_EXPERT_GROUPS = 4
EXPERTS_PER_GROUP = N_EXPERTS // N_EXPERT_GROUPS
TOP_K = 2
EXPERT_FF = 512
ROPE_BASE = 10000.0
LN_EPS = 1e-5
NEG_INF = -1e30
DEEPNORM_ALPHA = (2 * DEPTH) ** 0.25
ATTN_SCALE = HEAD_DIM ** -0.5

A_Q_W = A_HEADS * HEAD_DIM
A_KV_W = A_KV_HEADS * HEAD_DIM
B_W = B_HEADS * HEAD_DIM
OFF_AK = A_Q_W
OFF_AV = OFF_AK + A_KV_W
OFF_BQ = OFF_AV + A_KV_W
OFF_BK = OFF_BQ + B_W
OFF_BV = OFF_BK + B_W
OFF_C = OFF_BV + B_W
IN_COLS = OFF_C + 2 * C_W

VMEM_LIMIT_BYTES = 56 * 1024 * 1024
LANES = 128

ROW_TILE = 256
MATMUL_ROW_TILE = 512
PROJ_COL_TILE = 512
MOD_COL_TILE = 1024
MOE_TILE = 256

MOD_SH1, MOD_SC1, MOD_G1, MOD_SH2, MOD_SC2, MOD_G2 = range(6)


def _params(*sem):
    return pltpu.CompilerParams(dimension_semantics=sem, vmem_limit_bytes=VMEM_LIMIT_BYTES)


def _layer_norm(t, g, b):
    mu = jnp.mean(t, axis=-1, keepdims=True)
    d = t - mu
    var = jnp.mean(d * d, axis=-1, keepdims=True)
    return d * lax.rsqrt(var + LN_EPS) * g + b


def _sigmoid(v):
    return 1.0 / (1.0 + jnp.exp(-v))


def _dot_nt(a, b):
    return lax.dot_general(a, b, (((1,), (1,)), ((), ())), preferred_element_type=F32)


def _mod_kernel(c_ref, w_ref, b_ref, o_ref):
    w = w_ref[...]
    reps = w.shape[1] // LANES
    rows = []
    for r in range(2):
        cv = c_ref[r]
        s = cv * _sigmoid(cv)
        sb = jnp.concatenate([s] * reps, axis=1)
        rows.append(jnp.sum(w * sb, axis=0, keepdims=True) + b_ref[...])
    rows.append(jnp.zeros((6, w.shape[1]), F32))
    o_ref[...] = jnp.concatenate(rows, axis=0)


def _modulation(c, c_ctx, w_mod, b_mod):
    d = D_MODEL
    cb = jnp.stack([jnp.broadcast_to(c.reshape(d, 1), (d, LANES)),
                    jnp.broadcast_to(c_ctx.reshape(d, 1), (d, LANES))])
    n_out = 6 * d
    return pl.pallas_call(
        _mod_kernel,
        grid=(DEPTH, n_out // MOD_COL_TILE),
        in_specs=[pl.BlockSpec((2, d, LANES), lambda l, j: (0, 0, 0)),
                  pl.BlockSpec((None, d, MOD_COL_TILE), lambda l, j: (l, 0, j)),
                  pl.BlockSpec((None, 1, MOD_COL_TILE), lambda l, j: (l, 0, j))],
        out_specs=pl.BlockSpec((None, 8, MOD_COL_TILE), lambda l, j: (l, 0, j)),
        out_shape=jax.ShapeDtypeStruct((DEPTH, 8, n_out), F32),
        compiler_params=_params("arbitrary", "arbitrary"),
        name="modulation",
    )(cb, w_mod, b_mod.reshape(DEPTH, 1, n_out))


def _gelu_tanh(v):
    return 0.5 * v * (1.0 + jnp.tanh(np.sqrt(2.0 / np.pi).astype(np.float32) * (v + 0.044715 * (v * v * v))))


def _proj_kernel(x_ref, mod_ref, cos_ref, sin_ref, w_ref, o_ref, *, rope):
    x = x_ref[...]
    h = (x * (1.0 + mod_ref[MOD_SC1:MOD_SC1 + 1, :]) + mod_ref[MOD_SH1:MOD_SH1 + 1, :]).astype(BF16)
    tn = PROJ_COL_TILE
    for j in range(IN_COLS // tn):
        c0 = j * tn
        acc = jnp.dot(h, w_ref[:, c0:c0 + tn], preferred_element_type=F32)
        if c0 < OFF_AV:
            if rope:
                cos = cos_ref[...]
                sin = sin_ref[...]
                parts = []
                for hh in range(tn // HEAD_DIM):
                    a = acc[:, hh * HEAD_DIM:(hh + 1) * HEAD_DIM]
                    parts.append(a * cos + pltpu.roll(a, HEAD_DIM // 2, 1) * sin)
                acc = jnp.concatenate(parts, axis=1)
        elif c0 >= OFF_C:
            acc = _gelu_tanh(acc)
        o_ref[:, c0:c0 + tn] = acc.astype(BF16)


def _matmul_row_tile(*row_counts):
    return MATMUL_ROW_TILE if all(r % MATMUL_ROW_TILE == 0 for r in row_counts) else ROW_TILE


def _proj(x, x_row0, rows, mod, cos, sin, w_bf, layer, rope):
    d = D_MODEL
    tm = _matmul_row_tile(rows, x_row0)
    x_off = x_row0 // tm
    return pl.pallas_call(
        functools.partial(_proj_kernel, rope=rope),
        grid=(rows // tm,),
        in_specs=[pl.BlockSpec((tm, d), lambda i: (i + x_off, 0)),
                  pl.BlockSpec((8, d), lambda i: (0, 0)),
                  pl.BlockSpec((tm, HEAD_DIM), lambda i: (i, 0)),
                  pl.BlockSpec((tm, HEAD_DIM), lambda i: (i, 0)),
                  pl.BlockSpec((None, d, IN_COLS), lambda i: (layer, 0, 0), pipeline_mode=pl.Buffered(1))],
        out_specs=pl.BlockSpec((tm, IN_COLS), lambda i: (i, 0)),
        out_shape=jax.ShapeDtypeStruct((rows, IN_COLS), BF16),
        compiler_params=_params("arbitrary"),
        name="proj_rope" if rope else "proj_ctx",
    )(x, mod, cos, sin, w_bf)


LOG2E = 1.4426950408889634
LOGIT_SCALE = ATTN_SCALE * LOG2E


def _softmax_pv(problems):
    ms = []
    for s_parts, _, sink in problems:
        m = s_parts[0].max(axis=-1, keepdims=True)
        for s in s_parts[1:]:
            m = jnp.maximum(m, s.max(axis=-1, keepdims=True))
        ms.append(m if sink is None else jnp.maximum(m, sink))
    es = [[jnp.exp2(s - m).astype(BF16) for s in s_parts] for (s_parts, _, _), m in zip(problems, ms)]
    outs = []
    for (_, v_parts, sink), m, e_parts in zip(problems, ms, es):
        acc = None
        for e, v in zip(e_parts, v_parts):
            v1 = jnp.concatenate([v, jnp.ones((v.shape[0], LANES), BF16)], axis=1)
            pv = jnp.dot(e, v1, preferred_element_type=F32)
            acc = pv if acc is None else acc + pv
        denom = acc[:, HEAD_DIM:HEAD_DIM + 1]
        if sink is not None:
            denom = denom + jnp.exp2(sink - m)
        outs.append(acc[:, :HEAD_DIM] / denom)
    return outs


def _attn_a_kernel(sink_ref, q_ref, kp_ref, kc_ref, kn_ref, vp_ref, vc_ref, vn_ref, kx_ref, vx_ref,
                   u_ref, v_ref, sg_ref, sb_ref, sw_ref, sbs_ref, o_ref, oc_ref, mask_ref, *, latent):
    i = pl.program_id(0)
    nb = pl.num_programs(0)
    nq = A_GROUP * A_BLOCK
    _sgu_chunk(u_ref, v_ref, sg_ref, sb_ref, sw_ref, sbs_ref, oc_ref)

    if latent:
        @pl.when(i == 0)
        def _():
            qi = lax.broadcasted_iota(jnp.int32, (nq, 3 * A_BLOCK), 0) % A_BLOCK
            jj = lax.broadcasted_iota(jnp.int32, (nq, 3 * A_BLOCK), 1)
            ok = (jj >= qi) & (jj <= qi + 2 * A_BLOCK)
            mask_ref[0] = jnp.where(ok & (jj >= A_BLOCK), 0.0, NEG_INF).astype(F32)
            mask_ref[1] = jnp.where(ok, 0.0, NEG_INF).astype(F32)
            mask_ref[2] = jnp.where(ok & (jj < 2 * A_BLOCK), 0.0, NEG_INF).astype(F32)

        which = jnp.where(i == 0, 0, jnp.where(i == nb - 1, 2, 1))

    problems = []
    for kh in range(A_KV_HEADS):
        hs = [kh * A_GROUP + g for g in range(A_GROUP)]
        q = jnp.concatenate([q_ref[:, h * HEAD_DIM:(h + 1) * HEAD_DIM] for h in hs], axis=0)
        sink = jnp.concatenate([jnp.full((A_BLOCK, 1), sink_ref[h] * LOG2E, F32) for h in hs], axis=0)
        ks = slice(kh * HEAD_DIM, (kh + 1) * HEAD_DIM)
        s_parts = [_dot_nt(q, kx_ref[:, ks]) * LOGIT_SCALE]
        v_parts = [vx_ref[:, ks]]
        if latent:
            kband = jnp.concatenate([kp_ref[:, ks], kc_ref[:, ks], kn_ref[:, ks]], axis=0)
            vband = jnp.concatenate([vp_ref[:, ks], vc_ref[:, ks], vn_ref[:, ks]], axis=0)
            s_parts.append(_dot_nt(q, kband) * LOGIT_SCALE + mask_ref[which])
            v_parts.append(vband)
        problems.append((s_parts, v_parts, sink))
    for kh, out in enumerate(_softmax_pv(problems)):
        for g in range(A_GROUP):
            h = kh * A_GROUP + g
            o_ref[:, h * HEAD_DIM:(h + 1) * HEAD_DIM] = out[g * A_BLOCK:(g + 1) * A_BLOCK].astype(BF16)


def _attn_a_sgu(p, pc, sink, sgu_params, latent):
    assert CHUNK == A_BLOCK
    rows = p.shape[0]
    nb = rows // A_BLOCK
    assert nb >= 2
    kcol = OFF_AK // A_KV_W
    vcol = OFF_AV // A_KV_W
    ucol = OFF_C // C_W

    def band(col, shift):
        return pl.BlockSpec((A_BLOCK, A_KV_W), lambda i, s: (jnp.clip(i + shift, 0, nb - 1), col))

    lctx = pc.shape[0]
    return pl.pallas_call(
        functools.partial(_attn_a_kernel, latent=latent),
        grid_spec=pltpu.PrefetchScalarGridSpec(
            num_scalar_prefetch=1,
            grid=(nb,),
            in_specs=[pl.BlockSpec((A_BLOCK, A_Q_W), lambda i, s: (i, 0)),
                      band(kcol, -1), band(kcol, 0), band(kcol, 1),
                      band(vcol, -1), band(vcol, 0), band(vcol, 1),
                      pl.BlockSpec((lctx, A_KV_W), lambda i, s: (0, kcol)),
                      pl.BlockSpec((lctx, A_KV_W), lambda i, s: (0, vcol)),
                      pl.BlockSpec((CHUNK, C_W), lambda i, s: (i, ucol)),
                      pl.BlockSpec((CHUNK, C_W), lambda i, s: (i, ucol + 1)),
                      pl.BlockSpec((1, C_W), lambda i, s: (0, 0)),
                      pl.BlockSpec((1, C_W), lambda i, s: (0, 0)),
                      pl.BlockSpec((C_GROUPS, CHUNK, CHUNK), lambda i, s: (0, 0, 0)),
                      pl.BlockSpec((C_GROUPS, CHUNK, LANES), lambda i, s: (0, 0, 0))],
            out_specs=[pl.BlockSpec((A_BLOCK, A_Q_W), lambda i, s: (i, 0)),
                       pl.BlockSpec((CHUNK, C_W), lambda i, s: (i, 0))],
            scratch_shapes=[pltpu.VMEM((3, A_GROUP * A_BLOCK, 3 * A_BLOCK), F32)]),
        out_shape=[jax.ShapeDtypeStruct((rows, A_Q_W), BF16), jax.ShapeDtypeStruct((rows, C_W), BF16)],
        compiler_params=_params("arbitrary"),
        name="attn_a_latent" if latent else "attn_a_ctx",
    )(sink, p, p, p, p, p, p, p, pc, pc, p, p, *sgu_params)


B_PAIR_W = 2 * HEAD_DIM
NA_TILE = 256
NA_GROUP_ROWS = NA_TILE // GRID_W
NA_WIN_ROWS = NA_ROWS + NA_GROUP_ROWS
NA_PAIRS = NA_WIN_ROWS // 2
NA_BIAS_OFFS = 2 * NA_ROWS
NA_BOTH, NA_LEFT, NA_RIGHT = range(3)


def _attn_b_kernel(*refs, latent, grid_rows):
    npair = B_HEADS // 2
    q_refs, k_refs, v_refs, kx_refs, vx_refs = (refs[j * npair:(j + 1) * npair] for j in range(5))
    bias_ref, o_ref = refs[5 * npair], refs[5 * npair + 1]
    g = pl.program_id(0)
    if latent:
        r_base = g * NA_GROUP_ROWS
        w0 = jnp.clip(r_base - NA_ROWS // 2, 0, grid_rows - NA_WIN_ROWS)
        start = pl.multiple_of(w0 * GRID_W, GRID_W)
    for hp in range(npair):
        problems = []
        for hh in range(2):
            hs = slice(hh * HEAD_DIM, (hh + 1) * HEAD_DIM)
            q = q_refs[hp][:, hs]
            s_parts = [_dot_nt(q, kx_refs[hp][:, hs]) * LOGIT_SCALE]
            v_parts = [vx_refs[hp][:, hs]]
            if latent:
                kwin = k_refs[hp][pl.ds(start, NA_WIN_ROWS * GRID_W), hs]
                vwin = v_refs[hp][pl.ds(start, NA_WIN_ROWS * GRID_W), hs]
                bias_rows = []
                for rr in range(NA_GROUP_ROWS):
                    r = r_base + rr
                    r0 = jnp.clip(r - NA_ROWS // 2, 0, grid_rows - NA_ROWS)
                    tiles = []
                    for jp in range(NA_PAIRS):
                        ka = w0 + 2 * jp
                        in_a = (ka >= r0) & (ka < r0 + NA_ROWS)
                        in_b = (ka + 1 >= r0) & (ka + 1 < r0 + NA_ROWS)
                        variant = jnp.where(in_a, jnp.where(in_b, NA_BOTH, NA_LEFT),
                                            jnp.where(in_b, NA_RIGHT, NA_LEFT))
                        off = jnp.where(in_a | in_b, jnp.clip(ka - r + NA_ROWS, 0, NA_BIAS_OFFS - 1), 0)
                        tiles.append(bias_ref[2 * hp + hh, variant, off])
                    bias_rows.append(jnp.concatenate(tiles, axis=1))
                bias = jnp.concatenate(bias_rows, axis=0)
                s_parts.append(_dot_nt(q, kwin) * LOGIT_SCALE + bias)
                v_parts.append(vwin)
            problems.append((s_parts, v_parts, None))
        for hh, out in enumerate(_softmax_pv(problems)):
            h = 2 * hp + hh
            o_ref[:, h * HEAD_DIM:(h + 1) * HEAD_DIM] = out.astype(BF16)


def _na_bias_tables(rpb):
    cols = np.arange(GRID_W)
    c0 = np.clip(cols - NA_COLS // 2, 0, GRID_W - NA_COLS)
    rel = cols[None, :] - cols[:, None] + NA_COLS - 1
    ok = (cols[None, :] >= c0[:, None]) & (cols[None, :] < c0[:, None] + NA_COLS)
    onehot = (rel[None] == np.arange(2 * NA_COLS - 1)[:, None, None]).astype(np.float32)
    t = jnp.einsum("lhrd,dqk->lhrqk", rpb, onehot, precision=lax.Precision.HIGHEST)
    t = jnp.where(ok, t * LOG2E, NEG_INF).astype(F32)
    t = jnp.pad(t, ((0, 0), (0, 0), (1, 1), (0, 0), (0, 0)), constant_values=NEG_INF)
    pairs = jnp.concatenate([t[:, :, :-1], t[:, :, 1:]], axis=-1)
    keep = np.ones((3, 1, 1, 2 * GRID_W), bool)
    keep[NA_LEFT, :, :, GRID_W:] = False
    keep[NA_RIGHT, :, :, :GRID_W] = False
    return jnp.where(keep, pairs[:, :, None], NEG_INF)


def _attn_b(p, pc, bias_tabs, layer, latent):
    rows = p.shape[0]
    lctx = pc.shape[0]
    assert not latent or (rows // GRID_W >= NA_WIN_ROWS and rows % NA_TILE == 0)
    tile = NA_TILE if latent else ROW_TILE
    npair = B_HEADS // 2
    qcol = OFF_BQ // B_PAIR_W
    kcol = OFF_BK // B_PAIR_W
    vcol = OFF_BV // B_PAIR_W

    def resident(nrows, col):
        return [pl.BlockSpec((nrows, B_PAIR_W), lambda g, c=col + hp: (0, c), pipeline_mode=pl.Buffered(1))
                for hp in range(npair)]

    return pl.pallas_call(
        functools.partial(_attn_b_kernel, latent=latent, grid_rows=rows // GRID_W),
        grid=(rows // tile,),
        in_specs=([pl.BlockSpec((tile, B_PAIR_W), lambda g, c=qcol + hp: (g, c)) for hp in range(npair)]
                  + resident(rows, kcol) + resident(rows, vcol) + resident(lctx, kcol) + resident(lctx, vcol)
                  + [pl.BlockSpec((None, B_HEADS, 3, NA_BIAS_OFFS, GRID_W, 2 * GRID_W),
                                  lambda g: (layer, 0, 0, 0, 0, 0), pipeline_mode=pl.Buffered(1))]),
        out_specs=pl.BlockSpec((tile, B_W), lambda g: (g, 0)),
        out_shape=jax.ShapeDtypeStruct((rows, B_W), BF16),
        compiler_params=_params("arbitrary"),
        name="attn_b_latent" if latent else "attn_b_ctx",
    )(*([p] * (3 * npair) + [pc] * (2 * npair) + [bias_tabs]))


def _sgu_chunk(u_ref, v_ref, g_ref, b_ref, w_ref, bs_ref, o_ref):
    for grp in range(C_GROUPS):
        cs = slice(grp * HEAD_DIM, (grp + 1) * HEAD_DIM)
        vn = _layer_norm(v_ref[:, cs].astype(F32), g_ref[:, cs], b_ref[:, cs])
        mixed = jnp.dot(w_ref[grp], vn.astype(BF16), preferred_element_type=F32) + bs_ref[grp]
        o_ref[:, cs] = (u_ref[:, cs].astype(F32) * mixed).astype(BF16)


def _outproj_kernel(oa_ref, ob_ref, oc_ref, w_ref, x_ref, mod_ref, g_ref, b_ref, *rest):
    x1_ref, h2_ref = rest[-2], rest[-1]
    mix = jnp.dot(oa_ref[...], w_ref[0:A_Q_W, :], preferred_element_type=F32)
    mix += jnp.dot(ob_ref[...], w_ref[A_Q_W:A_Q_W + B_W, :], preferred_element_type=F32)
    mix += jnp.dot(oc_ref[...], w_ref[A_Q_W + B_W:, :], preferred_element_type=F32)
    t = DEEPNORM_ALPHA * x_ref[...] + mod_ref[MOD_G1:MOD_G1 + 1, :] * mix
    x1 = _layer_norm(t, g_ref[...], b_ref[...])
    x1_ref[...] = x1
    h2_ref[...] = x1 * (1.0 + mod_ref[MOD_SC2:MOD_SC2 + 1, :]) + mod_ref[MOD_SH2:MOD_SH2 + 1, :]


def _outproj(o_a, o_b, o_c, w_bf, layer, x, x_row0, mod, ln_g, ln_b, total_rows, out_row0, prev):
    rows = o_a.shape[0]
    d = D_MODEL
    tm = _matmul_row_tile(rows, x_row0, out_row0)
    x_off, out_off = x_row0 // tm, out_row0 // tm
    in_specs = [pl.BlockSpec((tm, A_Q_W), lambda i: (i, 0)),
                pl.BlockSpec((tm, B_W), lambda i: (i, 0)),
                pl.BlockSpec((tm, C_W), lambda i: (i, 0)),
                pl.BlockSpec((None, d, d), lambda i: (layer, 0, 0), pipeline_mode=pl.Buffered(1)),
                pl.BlockSpec((tm, d), lambda i: (i + x_off, 0)),
                pl.BlockSpec((8, d), lambda i: (0, 0)),
                pl.BlockSpec((1, d), lambda i: (0, 0)),
                pl.BlockSpec((1, d), lambda i: (0, 0))]
    args = [o_a, o_b, o_c, w_bf, x, mod, ln_g, ln_b]
    aliases = {}
    if prev is not None:
        in_specs += [pl.BlockSpec(memory_space=pl.ANY), pl.BlockSpec(memory_space=pl.ANY)]
        aliases = {len(args): 0, len(args) + 1: 1}
        args += list(prev)
    return pl.pallas_call(
        _outproj_kernel,
        grid=(rows // tm,),
        in_specs=in_specs,
        out_specs=[pl.BlockSpec((tm, d), lambda i: (i + out_off, 0))] * 2,
        out_shape=[jax.ShapeDtypeStruct((total_rows, d), F32)] * 2,
        input_output_aliases=aliases,
        compiler_params=_params("arbitrary"),
        name="outproj",
    )(*args)


def _top2_sublanes(vals, sub):
    m1 = vals.max(axis=0, keepdims=True)
    i1 = jnp.where(vals == m1, sub, vals.shape[0]).min(axis=0, keepdims=True)
    rest = jnp.where(sub == i1, -jnp.inf, vals)
    m2 = rest.max(axis=0, keepdims=True)
    i2 = jnp.where(rest == m2, sub, vals.shape[0]).min(axis=0, keepdims=True)
    return m1, i1, m2, i2


def _router_kernel(h_ref, wr_ref, rb_ref, e_ref, w_ref, rank_ref, cnt_ref, run_ref):
    i = pl.program_id(0)
    tm = ROW_TILE
    epg = EXPERTS_PER_GROUP

    @pl.when(i == 0)
    def _():
        run_ref[...] = jnp.zeros_like(run_ref)

    h = h_ref[...]
    h_hi = h.astype(BF16)
    h_mid = (h - h_hi.astype(F32)).astype(BF16)
    parts = _dot_nt(wr_ref[...], h_hi)
    parts_mid = _dot_nt(wr_ref[0:2 * N_EXPERTS, :], h_mid)
    logits = (parts[0:N_EXPERTS] + parts[N_EXPERTS:2 * N_EXPERTS] + parts[2 * N_EXPERTS:]
              + parts_mid[0:N_EXPERTS] + parts_mid[N_EXPERTS:])
    scores = _sigmoid(logits)
    biased = scores + jnp.concatenate([rb_ref[...]] * (tm // LANES), axis=1)
    sub = lax.broadcasted_iota(jnp.int32, (epg, tm), 0)

    best = None
    for g in range(N_EXPERT_GROUPS):
        m1, _, m2, _ = _top2_sublanes(biased[g * epg:(g + 1) * epg], sub)
        gs = m1 + m2
        if best is None:
            best, grp = gs, jnp.zeros((1, tm), jnp.int32)
            bsel, ssel = biased[0:epg], scores[0:epg]
        else:
            better = gs > best
            best = jnp.where(better, gs, best)
            grp = jnp.where(better, g, grp)
            bsel = jnp.where(better, biased[g * epg:(g + 1) * epg], bsel)
            ssel = jnp.where(better, scores[g * epg:(g + 1) * epg], ssel)
    _, i1, _, i2 = _top2_sublanes(bsel, sub)
    w1 = jnp.where(sub == i1, ssel, 0.0).sum(axis=0, keepdims=True)
    w2 = jnp.where(sub == i2, ssel, 0.0).sum(axis=0, keepdims=True)
    tot = w1 + w2
    e1 = grp * epg + i1
    e2 = grp * epg + i2

    eiota = lax.broadcasted_iota(jnp.int32, (N_EXPERTS, tm), 0)
    oh1 = (eiota == e1).astype(F32)
    oh2 = (eiota == e2).astype(F32)
    ohb = oh1 + oh2
    before = (lax.broadcasted_iota(jnp.int32, (tm, tm), 0) < lax.broadcasted_iota(jnp.int32, (tm, tm), 1))
    prefix = jnp.dot(ohb.astype(BF16), before.astype(BF16), preferred_element_type=F32)
    pos = run_ref[...] + prefix
    r1 = (oh1 * pos).sum(axis=0, keepdims=True)
    r2 = (oh2 * pos).sum(axis=0, keepdims=True)
    run_ref[...] = run_ref[...] + ohb.sum(axis=1, keepdims=True)

    e_ref[...] = jnp.concatenate([e1, e2], axis=0)
    w_ref[...] = jnp.concatenate([w1 / tot, w2 / tot], axis=0)
    rank_ref[...] = jnp.concatenate([r1, r2], axis=0).astype(jnp.int32)
    cnt_ref[...] = run_ref[:, 0:LANES]


def _router(h2, wr_t, rb_b):
    t = h2.shape[0]
    d = D_MODEL
    row2 = pl.BlockSpec((TOP_K, ROW_TILE), lambda i: (0, i))
    return pl.pallas_call(
        _router_kernel,
        grid=(t // ROW_TILE,),
        in_specs=[pl.BlockSpec((ROW_TILE, d), lambda i: (i, 0)),
                  pl.BlockSpec((3 * N_EXPERTS, d), lambda i: (0, 0)),
                  pl.BlockSpec((N_EXPERTS, LANES), lambda i: (0, 0))],
        out_specs=[row2, row2, row2, pl.BlockSpec((N_EXPERTS, LANES), lambda i: (0, 0))],
        out_shape=[jax.ShapeDtypeStruct((TOP_K, t), jnp.int32),
                   jax.ShapeDtypeStruct((TOP_K, t), F32),
                   jax.ShapeDtypeStruct((TOP_K, t), jnp.int32),
                   jax.ShapeDtypeStruct((N_EXPERTS, LANES), F32)],
        scratch_shapes=[pltpu.VMEM((N_EXPERTS, ROW_TILE), F32)],
        compiler_params=_params("arbitrary"),
        name="router",
    )(h2, wr_t, rb_b)


def _row_copy(src, row, dst, sem):
    return pltpu.make_async_copy(src.at[pl.ds(row, 1), :], dst, sem)


DISPATCH_SLOTS = 3


def _dispatch_kernel(slot_ref, h_hbm, xs_hbm, hbuf, lsem, ssem, *, n_tok):
    i = pl.program_id(0)
    nt = pl.num_programs(0)
    tm = ROW_TILE

    def load(blk):
        row0 = pl.multiple_of(blk * tm, tm)
        return pltpu.make_async_copy(h_hbm.at[pl.ds(row0, tm), :], hbuf.at[blk % DISPATCH_SLOTS],
                                     lsem.at[blk % DISPATCH_SLOTS])

    def wait_rows(par):
        for _ in range(TOP_K):
            pltpu.make_async_copy(hbuf.at[0], xs_hbm.at[pl.ds(0, tm), :], ssem.at[par]).wait()

    @pl.when(i == 0)
    def _():
        load(0).start()

    @pl.when(i + 1 < nt)
    def _():
        load(i + 1).start()

    load(i).wait()
    par = i % 2
    cur = hbuf.at[i % DISPATCH_SLOTS]
    for r in range(tm):
        for k in range(TOP_K):
            dst = xs_hbm.at[pl.ds(slot_ref[k * n_tok + i * tm + r], 1), :]
            pltpu.make_async_copy(cur.at[pl.ds(r, 1), :], dst, ssem.at[par]).start(priority=k)

    @pl.when(i > 0)
    def _():
        wait_rows(1 - par)

    @pl.when(i == nt - 1)
    def _():
        wait_rows(par)


def _dispatch(h2, slots, rows):
    t, d = h2.shape
    return pl.pallas_call(
        functools.partial(_dispatch_kernel, n_tok=t),
        grid_spec=pltpu.PrefetchScalarGridSpec(
            num_scalar_prefetch=1,
            grid=(t // ROW_TILE,),
            in_specs=[pl.BlockSpec(memory_space=pl.ANY)],
            out_specs=pl.BlockSpec(memory_space=pl.ANY),
            scratch_shapes=[pltpu.VMEM((DISPATCH_SLOTS, ROW_TILE, d), F32),
                            pltpu.SemaphoreType.DMA((DISPATCH_SLOTS,)),
                            pltpu.SemaphoreType.DMA((2,))]),
        out_shape=jax.ShapeDtypeStruct((rows, d), F32),
        compiler_params=_params("arbitrary"),
        name="dispatch",
    )(slots, h2)


def _experts_kernel(ts_ref, nu_ref, h_hbm, wg_hbm, wu_hbm, wd_hbm, ys_hbm,
                    xbuf, ybuf, gsem, ysem, wg_f, wu_f, wd_f, wsem, wg_bf, wu_bf, wd_bf, *, layer):
    e = pl.program_id(0)
    ne = pl.num_programs(0)
    n_used = nu_ref[0]
    tm = MOE_TILE
    tsub = tm

    def has_rows(ex):
        return ts_ref[ex + 1] > ts_ref[ex]

    def weight_copies(ex, wslot):
        return [pltpu.make_async_copy(src.at[layer, ex], dst.at[wslot], wsem.at[wslot])
                for src, dst in ((wg_hbm, wg_f), (wu_hbm, wu_f), (wd_hbm, wd_f))]

    def start_weights(ex, wslot):
        for cp in weight_copies(ex, wslot):
            cp.start(priority=1)

    def x_load(tile, slot):
        row0 = pl.multiple_of(tile * tsub, tsub)
        return pltpu.make_async_copy(h_hbm.at[pl.ds(row0, tsub), :], xbuf.at[slot], gsem.at[slot])

    def gather(tile, slot):
        x_load(tile, slot).start()

    def wait_gather(slot):
        x_load(0, slot).wait()

    def y_store(tile, slot):
        row0 = pl.multiple_of(tile * tsub, tsub)
        return pltpu.make_async_copy(ybuf.at[slot], ys_hbm.at[pl.ds(row0, tsub), :], ysem.at[slot])

    wslot = e % 2

    @pl.when((e == 0) & has_rows(0))
    def _():
        start_weights(0, 0)

    @pl.when((e == 0) & (n_used > 0))
    def _():
        gather(0, 0)

    nxt = jnp.minimum(e + 1, ne - 1)

    @pl.when((e + 1 < ne) & has_rows(nxt))
    def _():
        start_weights(nxt, 1 - wslot)

    @pl.when(has_rows(e))
    def _():
        for cp in weight_copies(e, wslot):
            cp.wait()
        wg_bf[...] = wg_f[wslot].astype(BF16)
        wu_bf[...] = wu_f[wslot].astype(BF16)
        wd_bf[...] = wd_f[wslot].astype(BF16)

    def tile_body(g, carry):
        slot = g % 2

        @pl.when(g >= 2)
        def _():
            y_store(g - 2, slot).wait()

        wait_gather(slot)
        x = xbuf[slot].astype(BF16)
        gather(jnp.minimum(g + 1, n_used - 1), 1 - slot)
        gate = jnp.dot(x, wg_bf[...], preferred_element_type=F32)
        up = jnp.dot(x, wu_bf[...], preferred_element_type=F32)
        hid = (gate * _sigmoid(gate)) * up
        ybuf[slot] = jnp.dot(hid.astype(BF16), wd_bf[...], preferred_element_type=F32)
        y_store(g, slot).start(priority=1)
        return carry

    lax.fori_loop(ts_ref[e], ts_ref[e + 1], tile_body, 0)

    @pl.when((e == pl.num_programs(0) - 1) & (n_used > 0))
    def _():
        wait_gather(n_used % 2)
        y_store(n_used - 1, (n_used - 1) % 2).wait()

        @pl.when(n_used >= 2)
        def _():
            y_store(n_used - 2, n_used % 2).wait()


def _experts(xs, tile_start, n_used, w_gate, w_up, w_down, layer):
    d = D_MODEL
    tm = MOE_TILE
    rows = xs.shape[0]

    hbm = pl.BlockSpec(memory_space=pl.ANY)
    return pl.pallas_call(
        functools.partial(_experts_kernel, layer=layer),
        grid_spec=pltpu.PrefetchScalarGridSpec(
            num_scalar_prefetch=2,
            grid=(N_EXPERTS,),
            in_specs=[hbm, hbm, hbm, hbm],
            out_specs=hbm,
            scratch_shapes=[pltpu.VMEM((2, tm, d), F32),
                            pltpu.VMEM((2, tm, d), F32),
                            pltpu.SemaphoreType.DMA((2,)),
                            pltpu.SemaphoreType.DMA((2,)),
                            pltpu.VMEM((2, d, EXPERT_FF), F32),
                            pltpu.VMEM((2, d, EXPERT_FF), F32),
                            pltpu.VMEM((2, EXPERT_FF, d), F32),
                            pltpu.SemaphoreType.DMA((2,)),
                            pltpu.VMEM((d, EXPERT_FF), BF16),
                            pltpu.VMEM((d, EXPERT_FF), BF16),
                            pltpu.VMEM((EXPERT_FF, d), BF16)]),
        out_shape=jax.ShapeDtypeStruct((rows, d), F32),
        compiler_params=_params("arbitrary"),
        name="experts",
    )(tile_start, n_used, xs, w_gate, w_up, w_down)


def _combine_kernel(slot_ref, ys_hbm, w_ref, x_ref, mod_ref, g_ref, b_ref, o_ref, ybuf, sem, *, n_tok):
    i = pl.program_id(0)
    nt = pl.num_programs(0)
    tm = ROW_TILE

    def issue(blk, buf):
        for r in range(tm):
            for k in range(TOP_K):
                row = slot_ref[k * n_tok + blk * tm + r]
                _row_copy(ys_hbm, row, ybuf.at[buf, k, pl.ds(r, 1), :], sem.at[buf]).start(priority=k)

    def wait(buf):
        for k in range(TOP_K):
            pltpu.make_async_copy(ys_hbm.at[pl.ds(0, tm), :], ybuf.at[buf, k], sem.at[buf]).wait()

    @pl.when(i == 0)
    def _():
        issue(0, 0)

    buf = i % 2
    wait(buf)
    issue(jnp.minimum(i + 1, nt - 1), 1 - buf)
    y = w_ref[:, 0:1] * ybuf[buf, 0] + w_ref[:, 1:2] * ybuf[buf, 1]
    t = DEEPNORM_ALPHA * x_ref[...] + mod_ref[MOD_G2:MOD_G2 + 1, :] * y
    o_ref[...] = _layer_norm(t, g_ref[...], b_ref[...])

    @pl.when(i == nt - 1)
    def _():
        wait(1 - buf)


def _combine(ys, slots, w_tok, x1, mods, n_lat_blocks, ln_g, ln_b):
    t = x1.shape[0]
    d = D_MODEL
    return pl.pallas_call(
        functools.partial(_combine_kernel, n_tok=t),
        grid_spec=pltpu.PrefetchScalarGridSpec(
            num_scalar_prefetch=1,
            grid=(t // ROW_TILE,),
            in_specs=[pl.BlockSpec(memory_space=pl.ANY),
                      pl.BlockSpec((ROW_TILE, TOP_K), lambda i, s: (i, 0)),
                      pl.BlockSpec((ROW_TILE, d), lambda i, s: (i, 0)),
                      pl.BlockSpec((None, 8, d), lambda i, s: (jnp.where(i < n_lat_blocks, 1, 0), 0, 0)),
                      pl.BlockSpec((1, d), lambda i, s: (0, 0)),
                      pl.BlockSpec((1, d), lambda i, s: (0, 0))],
            out_specs=pl.BlockSpec((ROW_TILE, d), lambda i, s: (i, 0)),
            scratch_shapes=[pltpu.VMEM((2, TOP_K, ROW_TILE, d), F32),
                            pltpu.SemaphoreType.DMA((2,))]),
        out_shape=jax.ShapeDtypeStruct((t, d), F32),
        compiler_params=_params("arbitrary"),
        name="combine",
    )(slots, ys, w_tok, x1, mods, ln_g, ln_b)


def _moe(h2, x1, mods, n_lat_blocks, wr_t, rb_b, w_gate, w_up, w_down, layer, ln_g, ln_b):
    t = x1.shape[0]
    tm = MOE_TILE
    e_idx, w_tok, rank, cnt = _router(h2, wr_t, rb_b)
    counts = cnt[:, 0].astype(jnp.int32)
    tiles_per = (counts + tm - 1) // tm
    tile_end = jnp.cumsum(tiles_per)
    n_used = tile_end[-1]
    row_off = (tile_end - tiles_per) * tm
    experts = jnp.arange(N_EXPERTS, dtype=jnp.int32)
    slots = jnp.sum(jnp.where(e_idx[:, :, None] == experts, row_off, 0), axis=-1) + rank
    max_tiles = (TOP_K * t + N_EXPERTS * (tm - 1)) // tm + 1
    tile_start = jnp.concatenate([tile_end - tiles_per, n_used.reshape(1)]).astype(jnp.int32)
    slots = slots.reshape(-1)
    xs = _dispatch(h2, slots, max_tiles * tm)
    ys = _experts(xs, tile_start, n_used.reshape(1).astype(jnp.int32), w_gate, w_up, w_down, layer)
    return _combine(ys, slots, w_tok.T, x1, mods, n_lat_blocks, ln_g, ln_b)


def _rope_tables(n):
    t = np.arange(n)
    row = (t // GRID_W).astype(np.float64)
    col = (t % GRID_W).astype(np.float64)
    n_freq = HEAD_DIM // 4
    inv_freq = ROPE_BASE ** (-np.arange(n_freq, dtype=np.float64) / n_freq)
    ang = np.concatenate([row[:, None] * inv_freq, col[:, None] * inv_freq], axis=-1)
    cos, sin = np.cos(ang), np.sin(ang)
    return (np.concatenate([cos, cos], axis=-1).astype(np.float32),
            np.concatenate([-sin, sin], axis=-1).astype(np.float32))


def kernel(x, c, ctx, c_ctx, w_mod, b_mod, w_in, attn_sink, na_rpb, sgu_ln_g, sgu_ln_b, sgu_w, sgu_b,
           w_out, ln1_g, ln1_b, w_router, router_bias, w_gate, w_up, w_down, ln2_g, ln2_b):
    batch, n, d = x.shape
    lctx = ctx.shape[1]
    assert batch == 1 and d == D_MODEL and n % ROW_TILE == 0 and lctx % ROW_TILE == 0
    n_lat_blocks = n // ROW_TILE

    mods = _modulation(c, c_ctx, w_mod, b_mod).reshape(DEPTH, 8, 6, d)
    mod_lat = jnp.pad(mods[:, 0], ((0, 0), (0, 2), (0, 0)))
    mod_ctx = jnp.pad(mods[:, 1], ((0, 0), (0, 2), (0, 0)))
    cos, sin = _rope_tables(n)
    wr_f = w_router.T
    def bf16_part(v):
        return lax.bitcast_convert_type(lax.bitcast_convert_type(v, jnp.int32) & jnp.int32(-65536), F32)

    wr_hi = bf16_part(wr_f)
    wr_mid = bf16_part(wr_f - wr_hi)
    wr_lo = wr_f - wr_hi - wr_mid
    wr_t = jnp.concatenate([wr_hi, wr_mid, wr_lo], axis=0).astype(BF16)
    rb_b = jnp.broadcast_to(router_bias.reshape(N_EXPERTS, 1), (N_EXPERTS, LANES))

    x_lat = x[0]
    x_ctx, ctx_row0 = ctx[0], 0
    w_in_bf = w_in.astype(BF16)
    w_out_bf = w_out.astype(BF16)
    bias_tabs = _na_bias_tables(na_rpb)
    for l in range(DEPTH):
        last = l == DEPTH - 1
        sgu_params = (sgu_ln_g[l].reshape(1, C_W), sgu_ln_b[l].reshape(1, C_W), sgu_w[l].astype(BF16),
                      jnp.broadcast_to(sgu_b[l][:, :, None], (C_GROUPS, CHUNK, LANES)))
        g1, b1 = ln1_g[l].reshape(1, d), ln1_b[l].reshape(1, d)
        g2, b2 = ln2_g[l].reshape(1, d), ln2_b[l].reshape(1, d)

        p = _proj(x_lat, 0, n, mod_lat[l], cos, sin, w_in_bf, l, rope=True)
        pc = _proj(x_ctx, ctx_row0, lctx, mod_ctx[l], cos, sin, w_in_bf, l, rope=False)
        o_a, o_c = _attn_a_sgu(p, pc, attn_sink[l], sgu_params, latent=True)
        o_b = _attn_b(p, pc, bias_tabs, l, latent=True)
        mod_both = jnp.stack([mod_ctx[l], mod_lat[l]])
        if last:
            x1, h2 = _outproj(o_a, o_b, o_c, w_out_bf, l, x_lat, 0, mod_lat[l], g1, b1, n, 0, None)
            x_lat = _moe(h2, x1, mod_both, n_lat_blocks, wr_t, rb_b, w_gate, w_up, w_down, l, g2, b2)
        else:
            oc_a, oc_c = _attn_a_sgu(pc, pc, attn_sink[l], sgu_params, latent=False)
            oc_b = _attn_b(pc, pc, bias_tabs, l, latent=False)
            total = n + lctx
            prev = _outproj(o_a, o_b, o_c, w_out_bf, l, x_lat, 0, mod_lat[l], g1, b1, total, 0, None)
            x1, h2 = _outproj(oc_a, oc_b, oc_c, w_out_bf, l, x_ctx, ctx_row0, mod_ctx[l], g1, b1, total, n,
                              prev)
            x_all = _moe(h2, x1, mod_both, n_lat_blocks, wr_t, rb_b, w_gate, w_up, w_down, l, g2, b2)
            x_lat, x_ctx, ctx_row0 = x_all, x_all, n
    return x_lat.reshape(batch, n, d)
```

```python
import functools

import numpy as np
import jax
import jax.numpy as jnp
from jax import lax
from jax.experimental import pallas as pl
from jax.experimental.pallas import tpu as pltpu

F32 = jnp.float32
BF16 = jnp.bfloat16

D_MODEL = 2048
DEPTH = 2
GRID_W = 64
HEAD_DIM = 128
A_HEADS = 6
A_KV_HEADS = 2
A_GROUP = A_HEADS // A_KV_HEADS
A_BLOCK = 128
B_HEADS = 6
NA_ROWS = 8
NA_COLS = 16
C_GROUPS = 4
C_W = C_GROUPS * HEAD_DIM
CHUNK = 128
N_EXPERTS = 32
N_EXPERT_GROUPS = 4
EXPERTS_PER_GROUP = N_EXPERTS // N_EXPERT_GROUPS
TOP_K = 2
EXPERT_FF = 512
ROPE_BASE = 10000.0
LN_EPS = 1e-5
NEG_INF = -1e30
DEEPNORM_ALPHA = (2 * DEPTH) ** 0.25
ATTN_SCALE = HEAD_DIM ** -0.5

A_Q_W = A_HEADS * HEAD_DIM
A_KV_W = A_KV_HEADS * HEAD_DIM
B_W = B_HEADS * HEAD_DIM
OFF_AK = A_Q_W
OFF_AV = OFF_AK + A_KV_W
OFF_BQ = OFF_AV + A_KV_W
OFF_BK = OFF_BQ + B_W
OFF_BV = OFF_BK + B_W
OFF_C = OFF_BV + B_W
IN_COLS = OFF_C + 2 * C_W

VMEM_LIMIT_BYTES = 56 * 1024 * 1024
LANES = 128

ROW_TILE = 256
MATMUL_ROW_TILE = 512
PROJ_COL_TILE = 512
MOD_COL_TILE = 1024
MOE_TILE = 256

MOD_SH1, MOD_SC1, MOD_G1, MOD_SH2, MOD_SC2, MOD_G2 = range(6)


def _params(*sem):
    return pltpu.CompilerParams(dimension_semantics=sem, vmem_limit_bytes=VMEM_LIMIT_BYTES)


def _layer_norm(t, g, b):
    mu = jnp.mean(t, axis=-1, keepdims=True)
    d = t - mu
    var = jnp.mean(d * d, axis=-1, keepdims=True)
    return d * lax.rsqrt(var + LN_EPS) * g + b


def _sigmoid(v):
    return 1.0 / (1.0 + jnp.exp(-v))


def _dot_nt(a, b):
    return lax.dot_general(a, b, (((1,), (1,)), ((), ())), preferred_element_type=F32)


def _mod_kernel(c_ref, w_ref, b_ref, o_ref):
    w = w_ref[...]
    reps = w.shape[1] // LANES
    rows = []
    for r in range(2):
        cv = c_ref[r]
        s = cv * _sigmoid(cv)
        sb = jnp.concatenate([s] * reps, axis=1)
        rows.append(jnp.sum(w * sb, axis=0, keepdims=True) + b_ref[...])
    rows.append(jnp.zeros((6, w.shape[1]), F32))
    o_ref[...] = jnp.concatenate(rows, axis=0)


def _modulation(c, c_ctx, w_mod, b_mod):
    d = D_MODEL
    cb = jnp.stack([jnp.broadcast_to(c.reshape(d, 1), (d, LANES)),
                    jnp.broadcast_to(c_ctx.reshape(d, 1), (d, LANES))])
    n_out = 6 * d
    return pl.pallas_call(
        _mod_kernel,
        grid=(DEPTH, n_out // MOD_COL_TILE),
        in_specs=[pl.BlockSpec((2, d, LANES), lambda l, j: (0, 0, 0)),
                  pl.BlockSpec((None, d, MOD_COL_TILE), lambda l, j: (l, 0, j)),
                  pl.BlockSpec((None, 1, MOD_COL_TILE), lambda l, j: (l, 0, j))],
        out_specs=pl.BlockSpec((None, 8, MOD_COL_TILE), lambda l, j: (l, 0, j)),
        out_shape=jax.ShapeDtypeStruct((DEPTH, 8, n_out), F32),
        compiler_params=_params("arbitrary", "arbitrary"),
        name="modulation",
    )(cb, w_mod, b_mod.reshape(DEPTH, 1, n_out))


def _gelu_tanh(v):
    return 0.5 * v * (1.0 + jnp.tanh(np.sqrt(2.0 / np.pi).astype(np.float32) * (v + 0.044715 * (v * v * v))))


def _proj_kernel(x_ref, mod_ref, cos_ref, sin_ref, w_ref, o_ref, *, rope):
    x = x_ref[...]
    h = (x * (1.0 + mod_ref[MOD_SC1:MOD_SC1 + 1, :]) + mod_ref[MOD_SH1:MOD_SH1 + 1, :]).astype(BF16)
    tn = PROJ_COL_TILE
    for j in range(IN_COLS // tn):
        c0 = j * tn
        acc = jnp.dot(h, w_ref[:, c0:c0 + tn], preferred_element_type=F32)
        if c0 < OFF_AV:
            if rope:
                cos = cos_ref[...]
                sin = sin_ref[...]
                parts = []
                for hh in range(tn // HEAD_DIM):
                    a = acc[:, hh * HEAD_DIM:(hh + 1) * HEAD_DIM]
                    parts.append(a * cos + pltpu.roll(a, HEAD_DIM // 2, 1) * sin)
                acc = jnp.concatenate(parts, axis=1)
        elif c0 >= OFF_C:
            acc = _gelu_tanh(acc)
        o_ref[:, c0:c0 + tn] = acc.astype(BF16)


def _matmul_row_tile(*row_counts):
    return MATMUL_ROW_TILE if all(r % MATMUL_ROW_TILE == 0 for r in row_counts) else ROW_TILE


def _proj(x, x_row0, rows, mod, cos, sin, w_bf, layer, rope):
    d = D_MODEL
    tm = _matmul_row_tile(rows, x_row0)
    x_off = x_row0 // tm
    return pl.pallas_call(
        functools.partial(_proj_kernel, rope=rope),
        grid=(rows // tm,),
        in_specs=[pl.BlockSpec((tm, d), lambda i: (i + x_off, 0)),
                  pl.BlockSpec((8, d), lambda i: (0, 0)),
                  pl.BlockSpec((tm, HEAD_DIM), lambda i: (i, 0)),
                  pl.BlockSpec((tm, HEAD_DIM), lambda i: (i, 0)),
                  pl.BlockSpec((None, d, IN_COLS), lambda i: (layer, 0, 0), pipeline_mode=pl.Buffered(1))],
        out_specs=pl.BlockSpec((tm, IN_COLS), lambda i: (i, 0)),
        out_shape=jax.ShapeDtypeStruct((rows, IN_COLS), BF16),
        compiler_params=_params("arbitrary"),
        name="proj_rope" if rope else "proj_ctx",
    )(x, mod, cos, sin, w_bf)


LOG2E = 1.4426950408889634
LOGIT_SCALE = ATTN_SCALE * LOG2E


def _softmax_pv(problems):
    ms = []
    for s_parts, _, sink in problems:
        m = s_parts[0].max(axis=-1, keepdims=True)
        for s in s_parts[1:]:
            m = jnp.maximum(m, s.max(axis=-1, keepdims=True))
        ms.append(m if sink is None else jnp.maximum(m, sink))
    es = [[jnp.exp2(s - m).astype(BF16) for s in s_parts] for (s_parts, _, _), m in zip(problems, ms)]
    outs = []
    for (_, v_parts, sink), m, e_parts in zip(problems, ms, es):
        acc = None
        for e, v in zip(e_parts, v_parts):
            v1 = jnp.concatenate([v, jnp.ones((v.shape[0], LANES), BF16)], axis=1)
            pv = jnp.dot(e, v1, preferred_element_type=F32)
            acc = pv if acc is None else acc + pv
        denom = acc[:, HEAD_DIM:HEAD_DIM + 1]
        if sink is not None:
            denom = denom + jnp.exp2(sink - m)
        outs.append(acc[:, :HEAD_DIM] / denom)
    return outs


A_STEP_BLOCKS = 2


def _attn_a_kernel(sink_ref, q_ref, *refs, latent):
    nband = A_STEP_BLOCKS + 2
    k_refs, v_refs = refs[:nband], refs[nband:2 * nband]
    kx_ref, vx_ref, u_ref, v_ref, sg_ref, sb_ref, sw_ref, sbs_ref, o_ref, oc_ref, mask_ref = refs[2 * nband:]
    i = pl.program_id(0)
    nb = pl.num_programs(0) * A_STEP_BLOCKS
    nq = A_GROUP * A_BLOCK

    if latent:
        @pl.when(i == 0)
        def _():
            qi = lax.broadcasted_iota(jnp.int32, (nq, 3 * A_BLOCK), 0) % A_BLOCK
            jj = lax.broadcasted_iota(jnp.int32, (nq, 3 * A_BLOCK), 1)
            ok = (jj >= qi) & (jj <= qi + 2 * A_BLOCK)
            mask_ref[0] = jnp.where(ok & (jj >= A_BLOCK), 0.0, NEG_INF).astype(F32)
            mask_ref[1] = jnp.where(ok, 0.0, NEG_INF).astype(F32)
            mask_ref[2] = jnp.where(ok & (jj < 2 * A_BLOCK), 0.0, NEG_INF).astype(F32)

    sinks = [jnp.concatenate([jnp.full((A_BLOCK, 1), sink_ref[kh * A_GROUP + g] * LOG2E, F32)
                              for g in range(A_GROUP)], axis=0) for kh in range(A_KV_HEADS)]
    for j in range(A_STEP_BLOCKS):
        rows = pl.ds(j * A_BLOCK, A_BLOCK)
        _sgu_chunk(u_ref.at[rows, :], v_ref.at[rows, :], sg_ref, sb_ref, sw_ref, sbs_ref, oc_ref.at[rows, :])
        if latent:
            blk = i * A_STEP_BLOCKS + j
            which = jnp.where(blk == 0, 0, jnp.where(blk == nb - 1, 2, 1))
        problems = []
        for kh in range(A_KV_HEADS):
            ks = slice(kh * HEAD_DIM, (kh + 1) * HEAD_DIM)
            q = jnp.concatenate([q_ref[rows, (kh * A_GROUP + g) * HEAD_DIM:(kh * A_GROUP + g + 1) * HEAD_DIM]
                                 for g in range(A_GROUP)], axis=0)
            s_parts = [_dot_nt(q, kx_ref[:, ks]) * LOGIT_SCALE]
            v_parts = [vx_ref[:, ks]]
            if latent:
                kband = jnp.concatenate([r[:, ks] for r in k_refs[j:j + 3]], axis=0)
                vband = jnp.concatenate([r[:, ks] for r in v_refs[j:j + 3]], axis=0)
                s_parts.append(_dot_nt(q, kband) * LOGIT_SCALE + mask_ref[which])
                v_parts.append(vband)
            problems.append((s_parts, v_parts, sinks[kh]))
        for kh, out in enumerate(_softmax_pv(problems)):
            for g in range(A_GROUP):
                h = kh * A_GROUP + g
                o_ref[rows, h * HEAD_DIM:(h + 1) * HEAD_DIM] = out[g * A_BLOCK:(g + 1) * A_BLOCK].astype(BF16)


def _attn_a_sgu(p, pc, sink, sgu_params, latent):
    assert CHUNK == A_BLOCK
    rows = p.shape[0]
    nb = rows // A_BLOCK
    assert nb >= 2 and nb % A_STEP_BLOCKS == 0
    kcol = OFF_AK // A_KV_W
    vcol = OFF_AV // A_KV_W
    ucol = OFF_C // C_W
    tile = A_STEP_BLOCKS * A_BLOCK
    nband = A_STEP_BLOCKS + 2

    def band(col):
        return [pl.BlockSpec((A_BLOCK, A_KV_W),
                             lambda i, s, sh=shift: (jnp.clip(i * A_STEP_BLOCKS + sh, 0, nb - 1), col))
                for shift in range(-1, A_STEP_BLOCKS + 1)]

    lctx = pc.shape[0]
    return pl.pallas_call(
        functools.partial(_attn_a_kernel, latent=latent),
        grid_spec=pltpu.PrefetchScalarGridSpec(
            num_scalar_prefetch=1,
            grid=(nb // A_STEP_BLOCKS,),
            in_specs=([pl.BlockSpec((tile, A_Q_W), lambda i, s: (i, 0))] + band(kcol) + band(vcol)
                      + [pl.BlockSpec((lctx, A_KV_W), lambda i, s: (0, kcol)),
                         pl.BlockSpec((lctx, A_KV_W), lambda i, s: (0, vcol)),
                         pl.BlockSpec((tile, C_W), lambda i, s: (i, ucol)),
                         pl.BlockSpec((tile, C_W), lambda i, s: (i, ucol + 1)),
                         pl.BlockSpec((1, C_W), lambda i, s: (0, 0)),
                         pl.BlockSpec((1, C_W), lambda i, s: (0, 0)),
                         pl.BlockSpec((C_GROUPS, CHUNK, CHUNK), lambda i, s: (0, 0, 0)),
                         pl.BlockSpec((C_GROUPS, CHUNK, LANES), lambda i, s: (0, 0, 0))]),
            out_specs=[pl.BlockSpec((tile, A_Q_W), lambda i, s: (i, 0)),
                       pl.BlockSpec((tile, C_W), lambda i, s: (i, 0))],
            scratch_shapes=[pltpu.VMEM((3, A_GROUP * A_BLOCK, 3 * A_BLOCK), F32)]),
        out_shape=[jax.ShapeDtypeStruct((rows, A_Q_W), BF16), jax.ShapeDtypeStruct((rows, C_W), BF16)],
        compiler_params=_params("arbitrary"),
        name="attn_a_latent" if latent else "attn_a_ctx",
    )(*([sink] + [p] * (1 + 2 * nband) + [pc, pc, p, p] + list(sgu_params)))


B_PAIR_W = 2 * HEAD_DIM
NA_TILE = 256
NA_GROUP_ROWS = NA_TILE // GRID_W
NA_WIN_ROWS = NA_ROWS + NA_GROUP_ROWS
NA_PAIRS = NA_WIN_ROWS // 2
NA_BIAS_OFFS = 2 * NA_ROWS
NA_BOTH, NA_LEFT, NA_RIGHT = range(3)


def _attn_b_kernel(*refs, latent, grid_rows):
    npair = B_HEADS // 2
    q_refs, k_refs, v_refs, kx_refs, vx_refs = (refs[j * npair:(j + 1) * npair] for j in range(5))
    bias_ref, o_ref = refs[5 * npair], refs[5 * npair + 1]
    g = pl.program_id(0)
    if latent:
        r_base = g * NA_GROUP_ROWS
        w0 = jnp.clip(r_base - NA_ROWS // 2, 0, grid_rows - NA_WIN_ROWS)
        start = pl.multiple_of(w0 * GRID_W, GRID_W)
    for hp in range(npair):
        problems = []
        for hh in range(2):
            hs = slice(hh * HEAD_DIM, (hh + 1) * HEAD_DIM)
            q = q_refs[hp][:, hs]
            s_parts = [_dot_nt(q, kx_refs[hp][:, hs]) * LOGIT_SCALE]
            v_parts = [vx_refs[hp][:, hs]]
            if latent:
                kwin = k_refs[hp][pl.ds(start, NA_WIN_ROWS * GRID_W), hs]
                vwin = v_refs[hp][pl.ds(start, NA_WIN_ROWS * GRID_W), hs]
                bias_rows = []
                for rr in range(NA_GROUP_ROWS):
                    r = r_base + rr
                    r0 = jnp.clip(r - NA_ROWS // 2, 0, grid_rows - NA_ROWS)
                    tiles = []
                    for jp in range(NA_PAIRS):
                        ka = w0 + 2 * jp
                        in_a = (ka >= r0) & (ka < r0 + NA_ROWS)
                        in_b = (ka + 1 >= r0) & (ka + 1 < r0 + NA_ROWS)
                        variant = jnp.where(in_a, jnp.where(in_b, NA_BOTH, NA_LEFT),
                                            jnp.where(in_b, NA_RIGHT, NA_LEFT))
                        off = jnp.where(in_a | in_b, jnp.clip(ka - r + NA_ROWS, 0, NA_BIAS_OFFS - 1), 0)
                        tiles.append(bias_ref[2 * hp + hh, variant, off])
                    bias_rows.append(jnp.concatenate(tiles, axis=1))
                bias = jnp.concatenate(bias_rows, axis=0)
                s_parts.append(_dot_nt(q, kwin) * LOGIT_SCALE + bias)
                v_parts.append(vwin)
            problems.append((s_parts, v_parts, None))
        for hh, out in enumerate(_softmax_pv(problems)):
            h = 2 * hp + hh
            o_ref[:, h * HEAD_DIM:(h + 1) * HEAD_DIM] = out.astype(BF16)


def _na_bias_tables(rpb):
    cols = np.arange(GRID_W)
    c0 = np.clip(cols - NA_COLS // 2, 0, GRID_W - NA_COLS)
    rel = cols[None, :] - cols[:, None] + NA_COLS - 1
    ok = (cols[None, :] >= c0[:, None]) & (cols[None, :] < c0[:, None] + NA_COLS)
    onehot = (rel[None] == np.arange(2 * NA_COLS - 1)[:, None, None]).astype(np.float32)
    t = jnp.einsum("lhrd,dqk->lhrqk", rpb, onehot, precision=lax.Precision.HIGHEST)
    t = jnp.where(ok, t * LOG2E, NEG_INF).astype(F32)
    t = jnp.pad(t, ((0, 0), (0, 0), (1, 1), (0, 0), (0, 0)), constant_values=NEG_INF)
    pairs = jnp.concatenate([t[:, :, :-1], t[:, :, 1:]], axis=-1)
    keep = np.ones((3, 1, 1, 2 * GRID_W), bool)
    keep[NA_LEFT, :, :, GRID_W:] = False
    keep[NA_RIGHT, :, :, :GRID_W] = False
    return jnp.where(keep, pairs[:, :, None], NEG_INF)


def _attn_b(p, pc, bias_tabs, layer, latent):
    rows = p.shape[0]
    lctx = pc.shape[0]
    assert not latent or (rows // GRID_W >= NA_WIN_ROWS and rows % NA_TILE == 0)
    tile = NA_TILE if latent else ROW_TILE
    npair = B_HEADS // 2
    qcol = OFF_BQ // B_PAIR_W
    kcol = OFF_BK // B_PAIR_W
    vcol = OFF_BV // B_PAIR_W

    def resident(nrows, col):
        return [pl.BlockSpec((nrows, B_PAIR_W), lambda g, c=col + hp: (0, c), pipeline_mode=pl.Buffered(1))
                for hp in range(npair)]

    return pl.pallas_call(
        functools.partial(_attn_b_kernel, latent=latent, grid_rows=rows // GRID_W),
        grid=(rows // tile,),
        in_specs=([pl.BlockSpec((tile, B_PAIR_W), lambda g, c=qcol + hp: (g, c)) for hp in range(npair)]
                  + resident(rows, kcol) + resident(rows, vcol) + resident(lctx, kcol) + resident(lctx, vcol)
                  + [pl.BlockSpec((None, B_HEADS, 3, NA_BIAS_OFFS, GRID_W, 2 * GRID_W),
                                  lambda g: (layer, 0, 0, 0, 0, 0), pipeline_mode=pl.Buffered(1))]),
        out_specs=pl.BlockSpec((tile, B_W), lambda g: (g, 0)),
        out_shape=jax.ShapeDtypeStruct((rows, B_W), BF16),
        compiler_params=_params("arbitrary"),
        name="attn_b_latent" if latent else "attn_b_ctx",
    )(*([p] * (3 * npair) + [pc] * (2 * npair) + [bias_tabs]))


def _sgu_chunk(u_ref, v_ref, g_ref, b_ref, w_ref, bs_ref, o_ref):
    for grp in range(C_GROUPS):
        cs = slice(grp * HEAD_DIM, (grp + 1) * HEAD_DIM)
        vn = _layer_norm(v_ref[:, cs].astype(F32), g_ref[:, cs], b_ref[:, cs])
        mixed = jnp.dot(w_ref[grp], vn.astype(BF16), preferred_element_type=F32) + bs_ref[grp]
        o_ref[:, cs] = (u_ref[:, cs].astype(F32) * mixed).astype(BF16)


def _outproj_kernel(oa_ref, ob_ref, oc_ref, w_ref, x_ref, mod_ref, g_ref, b_ref, *rest):
    x1_ref, h2_ref = rest[-2], rest[-1]
    mix = jnp.dot(oa_ref[...], w_ref[0:A_Q_W, :], preferred_element_type=F32)
    mix += jnp.dot(ob_ref[...], w_ref[A_Q_W:A_Q_W + B_W, :], preferred_element_type=F32)
    mix += jnp.dot(oc_ref[...], w_ref[A_Q_W + B_W:, :], preferred_element_type=F32)
    t = DEEPNORM_ALPHA * x_ref[...] + mod_ref[MOD_G1:MOD_G1 + 1, :] * mix
    x1 = _layer_norm(t, g_ref[...], b_ref[...])
    x1_ref[...] = x1
    h2_ref[...] = x1 * (1.0 + mod_ref[MOD_SC2:MOD_SC2 + 1, :]) + mod_ref[MOD_SH2:MOD_SH2 + 1, :]


def _outproj(o_a, o_b, o_c, w_bf, layer, x, x_row0, mod, ln_g, ln_b, total_rows, out_row0, prev):
    rows = o_a.shape[0]
    d = D_MODEL
    tm = _matmul_row_tile(rows, x_row0, out_row0)
    x_off, out_off = x_row0 // tm, out_row0 // tm
    in_specs = [pl.BlockSpec((tm, A_Q_W), lambda i: (i, 0)),
                pl.BlockSpec((tm, B_W), lambda i: (i, 0)),
                pl.BlockSpec((tm, C_W), lambda i: (i, 0)),
                pl.BlockSpec((None, d, d), lambda i: (layer, 0, 0), pipeline_mode=pl.Buffered(1)),
                pl.BlockSpec((tm, d), lambda i: (i + x_off, 0)),
                pl.BlockSpec((8, d), lambda i: (0, 0)),
                pl.BlockSpec((1, d), lambda i: (0, 0)),
                pl.BlockSpec((1, d), lambda i: (0, 0))]
    args = [o_a, o_b, o_c, w_bf, x, mod, ln_g, ln_b]
    aliases = {}
    if prev is not None:
        in_specs += [pl.BlockSpec(memory_space=pl.ANY), pl.BlockSpec(memory_space=pl.ANY)]
        aliases = {len(args): 0, len(args) + 1: 1}
        args += list(prev)
    return pl.pallas_call(
        _outproj_kernel,
        grid=(rows // tm,),
        in_specs=in_specs,
        out_specs=[pl.BlockSpec((tm, d), lambda i: (i + out_off, 0))] * 2,
        out_shape=[jax.ShapeDtypeStruct((total_rows, d), F32)] * 2,
        input_output_aliases=aliases,
        compiler_params=_params("arbitrary"),
        name="outproj",
    )(*args)


def _top2_sublanes(vals, sub):
    m1 = vals.max(axis=0, keepdims=True)
    i1 = jnp.where(vals == m1, sub, vals.shape[0]).min(axis=0, keepdims=True)
    rest = jnp.where(sub == i1, -jnp.inf, vals)
    m2 = rest.max(axis=0, keepdims=True)
    i2 = jnp.where(rest == m2, sub, vals.shape[0]).min(axis=0, keepdims=True)
    return m1, i1, m2, i2


def _router_kernel(h_ref, wr_ref, rb_ref, e_ref, w_ref, rank_ref, cnt_ref, run_ref):
    i = pl.program_id(0)
    tm = ROW_TILE
    epg = EXPERTS_PER_GROUP

    @pl.when(i == 0)
    def _():
        run_ref[...] = jnp.zeros_like(run_ref)

    h = h_ref[...]
    h_hi = h.astype(BF16)
    h_mid = (h - h_hi.astype(F32)).astype(BF16)
    parts = _dot_nt(wr_ref[...], h_hi)
    parts_mid = _dot_nt(wr_ref[0:2 * N_EXPERTS, :], h_mid)
    logits = (parts[0:N_EXPERTS] + parts[N_EXPERTS:2 * N_EXPERTS] + parts[2 * N_EXPERTS:]
              + parts_mid[0:N_EXPERTS] + parts_mid[N_EXPERTS:])
    scores = _sigmoid(logits)
    biased = scores + jnp.concatenate([rb_ref[...]] * (tm // LANES), axis=1)
    sub = lax.broadcasted_iota(jnp.int32, (epg, tm), 0)

    best = None
    for g in range(N_EXPERT_GROUPS):
        m1, _, m2, _ = _top2_sublanes(biased[g * epg:(g + 1) * epg], sub)
        gs = m1 + m2
        if best is None:
            best, grp = gs, jnp.zeros((1, tm), jnp.int32)
            bsel, ssel = biased[0:epg], scores[0:epg]
        else:
            better = gs > best
            best = jnp.where(better, gs, best)
            grp = jnp.where(better, g, grp)
            bsel = jnp.where(better, biased[g * epg:(g + 1) * epg], bsel)
            ssel = jnp.where(better, scores[g * epg:(g + 1) * epg], ssel)
    _, i1, _, i2 = _top2_sublanes(bsel, sub)
    w1 = jnp.where(sub == i1, ssel, 0.0).sum(axis=0, keepdims=True)
    w2 = jnp.where(sub == i2, ssel, 0.0).sum(axis=0, keepdims=True)
    tot = w1 + w2
    e1 = grp * epg + i1
    e2 = grp * epg + i2

    eiota = lax.broadcasted_iota(jnp.int32, (N_EXPERTS, tm), 0)
    oh1 = (eiota == e1).astype(F32)
    oh2 = (eiota == e2).astype(F32)
    ohb = oh1 + oh2
    before = (lax.broadcasted_iota(jnp.int32, (tm, tm), 0) < lax.broadcasted_iota(jnp.int32, (tm, tm), 1))
    prefix = jnp.dot(ohb.astype(BF16), before.astype(BF16), preferred_element_type=F32)
    pos = run_ref[...] + prefix
    r1 = (oh1 * pos).sum(axis=0, keepdims=True)
    r2 = (oh2 * pos).sum(axis=0, keepdims=True)
    run_ref[...] = run_ref[...] + ohb.sum(axis=1, keepdims=True)

    e_ref[...] = jnp.concatenate([e1, e2], axis=0)
    w_ref[...] = jnp.concatenate([w1 / tot, w2 / tot], axis=0)
    rank_ref[...] = jnp.concatenate([r1, r2], axis=0).astype(jnp.int32)
    cnt_ref[...] = run_ref[:, 0:LANES]


def _router(h2, wr_t, rb_b):
    t = h2.shape[0]
    d = D_MODEL
    row2 = pl.BlockSpec((TOP_K, ROW_TILE), lambda i: (0, i))
    return pl.pallas_call(
        _router_kernel,
        grid=(t // ROW_TILE,),
        in_specs=[pl.BlockSpec((ROW_TILE, d), lambda i: (i, 0)),
                  pl.BlockSpec((3 * N_EXPERTS, d), lambda i: (0, 0)),
                  pl.BlockSpec((N_EXPERTS, LANES), lambda i: (0, 0))],
        out_specs=[row2, row2, row2, pl.BlockSpec((N_EXPERTS, LANES), lambda i: (0, 0))],
        out_shape=[jax.ShapeDtypeStruct((TOP_K, t), jnp.int32),
                   jax.ShapeDtypeStruct((TOP_K, t), F32),
                   jax.ShapeDtypeStruct((TOP_K, t), jnp.int32),
                   jax.ShapeDtypeStruct((N_EXPERTS, LANES), F32)],
        scratch_shapes=[pltpu.VMEM((N_EXPERTS, ROW_TILE), F32)],
        compiler_params=_params("arbitrary"),
        name="router",
    )(h2, wr_t, rb_b)


def _row_copy(src, row, dst, sem):
    return pltpu.make_async_copy(src.at[pl.ds(row, 1), :], dst, sem)


DISPATCH_SLOTS = 3


def _dispatch_kernel(slot_ref, h_hbm, xs_hbm, hbuf, lsem, ssem, *, n_tok):
    i = pl.program_id(0)
    nt = pl.num_programs(0)
    tm = ROW_TILE

    def load(blk):
        row0 = pl.multiple_of(blk * tm, tm)
        return pltpu.make_async_copy(h_hbm.at[pl.ds(row0, tm), :], hbuf.at[blk % DISPATCH_SLOTS],
                                     lsem.at[blk % DISPATCH_SLOTS])

    def wait_rows(par):
        for _ in range(TOP_K):
            pltpu.make_async_copy(hbuf.at[0], xs_hbm.at[pl.ds(0, tm), :], ssem.at[par]).wait()

    @pl.when(i == 0)
    def _():
        load(0).start()

    @pl.when(i + 1 < nt)
    def _():
        load(i + 1).start()

    load(i).wait()
    par = i % 2
    cur = hbuf.at[i % DISPATCH_SLOTS]
    for r in range(tm):
        for k in range(TOP_K):
            dst = xs_hbm.at[pl.ds(slot_ref[k * n_tok + i * tm + r], 1), :]
            pltpu.make_async_copy(cur.at[pl.ds(r, 1), :], dst, ssem.at[par]).start(priority=k)

    @pl.when(i > 0)
    def _():
        wait_rows(1 - par)

    @pl.when(i == nt - 1)
    def _():
        wait_rows(par)


def _dispatch(h2, slots, rows):
    t, d = h2.shape
    return pl.pallas_call(
        functools.partial(_dispatch_kernel, n_tok=t),
        grid_spec=pltpu.PrefetchScalarGridSpec(
            num_scalar_prefetch=1,
            grid=(t // ROW_TILE,),
            in_specs=[pl.BlockSpec(memory_space=pl.ANY)],
            out_specs=pl.BlockSpec(memory_space=pl.ANY),
            scratch_shapes=[pltpu.VMEM((DISPATCH_SLOTS, ROW_TILE, d), F32),
                            pltpu.SemaphoreType.DMA((DISPATCH_SLOTS,)),
                            pltpu.SemaphoreType.DMA((2,))]),
        out_shape=jax.ShapeDtypeStruct((rows, d), F32),
        compiler_params=_params("arbitrary"),
        name="dispatch",
    )(slots, h2)


def _experts_kernel(ts_ref, nu_ref, h_hbm, wg_hbm, wu_hbm, wd_hbm, ys_hbm,
                    xbuf, ybuf, gsem, ysem, wg_f, wu_f, wd_f, wsem, wg_bf, wu_bf, wd_bf, *, layer):
    e = pl.program_id(0)
    ne = pl.num_programs(0)
    n_used = nu_ref[0]
    tm = MOE_TILE
    tsub = tm

    def has_rows(ex):
        return ts_ref[ex + 1] > ts_ref[ex]

    def weight_copies(ex, wslot):
        return [pltpu.make_async_copy(src.at[layer, ex], dst.at[wslot], wsem.at[wslot])
                for src, dst in ((wg_hbm, wg_f), (wu_hbm, wu_f), (wd_hbm, wd_f))]

    def start_weights(ex, wslot):
        for cp in weight_copies(ex, wslot):
            cp.start(priority=1)

    def x_load(tile, slot):
        row0 = pl.multiple_of(tile * tsub, tsub)
        return pltpu.make_async_copy(h_hbm.at[pl.ds(row0, tsub), :], xbuf.at[slot], gsem.at[slot])

    def gather(tile, slot):
        x_load(tile, slot).start()

    def wait_gather(slot):
        x_load(0, slot).wait()

    def y_store(tile, slot):
        row0 = pl.multiple_of(tile * tsub, tsub)
        return pltpu.make_async_copy(ybuf.at[slot], ys_hbm.at[pl.ds(row0, tsub), :], ysem.at[slot])

    wslot = e % 2

    @pl.when((e == 0) & has_rows(0))
    def _():
        start_weights(0, 0)

    @pl.when((e == 0) & (n_used > 0))
    def _():
        gather(0, 0)

    nxt = jnp.minimum(e + 1, ne - 1)

    @pl.when((e + 1 < ne) & has_rows(nxt))
    def _():
        start_weights(nxt, 1 - wslot)

    @pl.when(has_rows(e))
    def _():
        for cp in weight_copies(e, wslot):
            cp.wait()
        wg_bf[...] = wg_f[wslot].astype(BF16)
        wu_bf[...] = wu_f[wslot].astype(BF16)
        wd_bf[...] = wd_f[wslot].astype(BF16)

    def tile_body(g, carry):
        slot = g % 2

        @pl.when(g >= 2)
        def _():
            y_store(g - 2, slot).wait()

        wait_gather(slot)
        x = xbuf[slot].astype(BF16)
        gather(jnp.minimum(g + 1, n_used - 1), 1 - slot)
        gate = jnp.dot(x, wg_bf[...], preferred_element_type=F32)
        up = jnp.dot(x, wu_bf[...], preferred_element_type=F32)
        hid = (gate * _sigmoid(gate)) * up
        ybuf[slot] = jnp.dot(hid.astype(BF16), wd_bf[...], preferred_element_type=F32)
        y_store(g, slot).start(priority=1)
        return carry

    lax.fori_loop(ts_ref[e], ts_ref[e + 1], tile_body, 0)

    @pl.when((e == pl.num_programs(0) - 1) & (n_used > 0))
    def _():
        wait_gather(n_used % 2)
        y_store(n_used - 1, (n_used - 1) % 2).wait()

        @pl.when(n_used >= 2)
        def _():
            y_store(n_used - 2, n_used % 2).wait()


def _experts(xs, tile_start, n_used, w_gate, w_up, w_down, layer):
    d = D_MODEL
    tm = MOE_TILE
    rows = xs.shape[0]

    hbm = pl.BlockSpec(memory_space=pl.ANY)
    return pl.pallas_call(
        functools.partial(_experts_kernel, layer=layer),
        grid_spec=pltpu.PrefetchScalarGridSpec(
            num_scalar_prefetch=2,
            grid=(N_EXPERTS,),
            in_specs=[hbm, hbm, hbm, hbm],
            out_specs=hbm,
            scratch_shapes=[pltpu.VMEM((2, tm, d), F32),
                            pltpu.VMEM((2, tm, d), F32),
                            pltpu.SemaphoreType.DMA((2,)),
                            pltpu.SemaphoreType.DMA((2,)),
                            pltpu.VMEM((2, d, EXPERT_FF), F32),
                            pltpu.VMEM((2, d, EXPERT_FF), F32),
                            pltpu.VMEM((2, EXPERT_FF, d), F32),
                            pltpu.SemaphoreType.DMA((2,)),
                            pltpu.VMEM((d, EXPERT_FF), BF16),
                            pltpu.VMEM((d, EXPERT_FF), BF16),
                            pltpu.VMEM((EXPERT_FF, d), BF16)]),
        out_shape=jax.ShapeDtypeStruct((rows, d), F32),
        compiler_params=_params("arbitrary"),
        name="experts",
    )(tile_start, n_used, xs, w_gate, w_up, w_down)


def _combine_kernel(slot_ref, ys_hbm, w_ref, x_ref, mod_ref, g_ref, b_ref, o_ref, ybuf, sem, *, n_tok):
    i = pl.program_id(0)
    nt = pl.num_programs(0)
    tm = ROW_TILE

    def issue(blk, buf):
        for r in range(tm):
            for k in range(TOP_K):
                row = slot_ref[k * n_tok + blk * tm + r]
                _row_copy(ys_hbm, row, ybuf.at[buf, k, pl.ds(r, 1), :], sem.at[buf]).start(priority=k)

    def wait(buf):
        for k in range(TOP_K):
            pltpu.make_async_copy(ys_hbm.at[pl.ds(0, tm), :], ybuf.at[buf, k], sem.at[buf]).wait()

    @pl.when(i == 0)
    def _():
        issue(0, 0)

    buf = i % 2
    wait(buf)
    issue(jnp.minimum(i + 1, nt - 1), 1 - buf)
    y = w_ref[:, 0:1] * ybuf[buf, 0] + w_ref[:, 1:2] * ybuf[buf, 1]
    t = DEEPNORM_ALPHA * x_ref[...] + mod_ref[MOD_G2:MOD_G2 + 1, :] * y
    o_ref[...] = _layer_norm(t, g_ref[...], b_ref[...])

    @pl.when(i == nt - 1)
    def _():
        wait(1 - buf)


def _combine(ys, slots, w_tok, x1, mods, n_lat_blocks, ln_g, ln_b):
    t = x1.shape[0]
    d = D_MODEL
    return pl.pallas_call(
        functools.partial(_combine_kernel, n_tok=t),
        grid_spec=pltpu.PrefetchScalarGridSpec(
            num_scalar_prefetch=1,
            grid=(t // ROW_TILE,),
            in_specs=[pl.BlockSpec(memory_space=pl.ANY),
                      pl.BlockSpec((ROW_TILE, TOP_K), lambda i, s: (i, 0)),
                      pl.BlockSpec((ROW_TILE, d), lambda i, s: (i, 0)),
                      pl.BlockSpec((None, 8, d), lambda i, s: (jnp.where(i < n_lat_blocks, 1, 0), 0, 0)),
                      pl.BlockSpec((1, d), lambda i, s: (0, 0)),
                      pl.BlockSpec((1, d), lambda i, s: (0, 0))],
            out_specs=pl.BlockSpec((ROW_TILE, d), lambda i, s: (i, 0)),
            scratch_shapes=[pltpu.VMEM((2, TOP_K, ROW_TILE, d), F32),
                            pltpu.SemaphoreType.DMA((2,))]),
        out_shape=jax.ShapeDtypeStruct((t, d), F32),
        compiler_params=_params("arbitrary"),
        name="combine",
    )(slots, ys, w_tok, x1, mods, ln_g, ln_b)


def _moe(h2, x1, mods, n_lat_blocks, wr_t, rb_b, w_gate, w_up, w_down, layer, ln_g, ln_b):
    t = x1.shape[0]
    tm = MOE_TILE
    e_idx, w_tok, rank, cnt = _router(h2, wr_t, rb_b)
    counts = cnt[:, 0].astype(jnp.int32)
    tiles_per = (counts + tm - 1) // tm
    tile_end = jnp.cumsum(tiles_per)
    n_used = tile_end[-1]
    row_off = (tile_end - tiles_per) * tm
    experts = jnp.arange(N_EXPERTS, dtype=jnp.int32)
    slots = jnp.sum(jnp.where(e_idx[:, :, None] == experts, row_off, 0), axis=-1) + rank
    max_tiles = (TOP_K * t + N_EXPERTS * (tm - 1)) // tm + 1
    tile_start = jnp.concatenate([tile_end - tiles_per, n_used.reshape(1)]).astype(jnp.int32)
    slots = slots.reshape(-1)
    xs = _dispatch(h2, slots, max_tiles * tm)
    ys = _experts(xs, tile_start, n_used.reshape(1).astype(jnp.int32), w_gate, w_up, w_down, layer)
    return _combine(ys, slots, w_tok.T, x1, mods, n_lat_blocks, ln_g, ln_b)


def _rope_tables(n):
    t = np.arange(n)
    row = (t // GRID_W).astype(np.float64)
    col = (t % GRID_W).astype(np.float64)
    n_freq = HEAD_DIM // 4
    inv_freq = ROPE_BASE ** (-np.arange(n_freq, dtype=np.float64) / n_freq)
    ang = np.concatenate([row[:, None] * inv_freq, col[:, None] * inv_freq], axis=-1)
    cos, sin = np.cos(ang), np.sin(ang)
    return (np.concatenate([cos, cos], axis=-1).astype(np.float32),
            np.concatenate([-sin, sin], axis=-1).astype(np.float32))


def kernel(x, c, ctx, c_ctx, w_mod, b_mod, w_in, attn_sink, na_rpb, sgu_ln_g, sgu_ln_b, sgu_w, sgu_b,
           w_out, ln1_g, ln1_b, w_router, router_bias, w_gate, w_up, w_down, ln2_g, ln2_b):
    batch, n, d = x.shape
    lctx = ctx.shape[1]
    assert batch == 1 and d == D_MODEL and n % ROW_TILE == 0 and lctx % ROW_TILE == 0
    n_lat_blocks = n // ROW_TILE

    mods = _modulation(c, c_ctx, w_mod, b_mod).reshape(DEPTH, 8, 6, d)
    mod_lat = jnp.pad(mods[:, 0], ((0, 0), (0, 2), (0, 0)))
    mod_ctx = jnp.pad(mods[:, 1], ((0, 0), (0, 2), (0, 0)))
    cos, sin = _rope_tables(n)
    wr_f = w_router.T
    def bf16_part(v):
        return lax.bitcast_convert_type(lax.bitcast_convert_type(v, jnp.int32) & jnp.int32(-65536), F32)

    wr_hi = bf16_part(wr_f)
    wr_mid = bf16_part(wr_f - wr_hi)
    wr_lo = wr_f - wr_hi - wr_mid
    wr_t = jnp.concatenate([wr_hi, wr_mid, wr_lo], axis=0).astype(BF16)
    rb_b = jnp.broadcast_to(router_bias.reshape(N_EXPERTS, 1), (N_EXPERTS, LANES))

    x_lat = x[0]
    x_ctx, ctx_row0 = ctx[0], 0
    w_in_bf = w_in.astype(BF16)
    w_out_bf = w_out.astype(BF16)
    bias_tabs = _na_bias_tables(na_rpb)
    for l in range(DEPTH):
        last = l == DEPTH - 1
        sgu_params = (sgu_ln_g[l].reshape(1, C_W), sgu_ln_b[l].reshape(1, C_W), sgu_w[l].astype(BF16),
                      jnp.broadcast_to(sgu_b[l][:, :, None], (C_GROUPS, CHUNK, LANES)))
        g1, b1 = ln1_g[l].reshape(1, d), ln1_b[l].reshape(1, d)
        g2, b2 = ln2_g[l].reshape(1, d), ln2_b[l].reshape(1, d)

        p = _proj(x_lat, 0, n, mod_lat[l], cos, sin, w_in_bf, l, rope=True)
        pc = _proj(x_ctx, ctx_row0, lctx, mod_ctx[l], cos, sin, w_in_bf, l, rope=False)
        o_a, o_c = _attn_a_sgu(p, pc, attn_sink[l], sgu_params, latent=True)
        o_b = _attn_b(p, pc, bias_tabs, l, latent=True)
        mod_both = jnp.stack([mod_ctx[l], mod_lat[l]])
        if last:
            x1, h2 = _outproj(o_a, o_b, o_c, w_out_bf, l, x_lat, 0, mod_lat[l], g1, b1, n, 0, None)
            x_lat = _moe(h2, x1, mod_both, n_lat_blocks, wr_t, rb_b, w_gate, w_up, w_down, l, g2, b2)
        else:
            oc_a, oc_c = _attn_a_sgu(pc, pc, attn_sink[l], sgu_params, latent=False)
            oc_b = _attn_b(pc, pc, bias_tabs, l, latent=False)
            total = n + lctx
            prev = _outproj(o_a, o_b, o_c, w_out_bf, l, x_lat, 0, mod_lat[l], g1, b1, total, 0, None)
            x1, h2 = _outproj(oc_a, oc_b, oc_c, w_out_bf, l, x_ctx, ctx_row0, mod_ctx[l], g1, b1, total, n,
                              prev)
            x_all = _moe(h2, x1, mod_both, n_lat_blocks, wr_t, rb_b, w_gate, w_up, w_down, l, g2, b2)
            x_lat, x_ctx, ctx_row0 = x_all, x_all, n
    return x_lat.reshape(batch, n, d)
```

```python
import functools

import numpy as np
import jax
import jax.numpy as jnp
from jax import lax
from jax.experimental import pallas as pl
from jax.experimental.pallas import tpu as pltpu

F32 = jnp.float32
BF16 = jnp.bfloat16

D_MODEL = 2048
DEPTH = 2
GRID_W = 64
HEAD_DIM = 128
A_HEADS = 6
A_KV_HEADS = 2
A_GROUP = A_HEADS // A_KV_HEADS
A_BLOCK = 128
B_HEADS = 6
NA_ROWS = 8
NA_COLS = 16
C_GROUPS = 4
C_W = C_GROUPS * HEAD_DIM
CHUNK = 128
N_EXPERTS = 32
N_EXPERT_GROUPS = 4
EXPERTS_PER_GROUP = N_EXPERTS // N_EXPERT_GROUPS
TOP_K = 2
EXPERT_FF = 512
ROPE_BASE = 10000.0
LN_EPS = 1e-5
NEG_INF = -1e30
DEEPNORM_ALPHA = (2 * DEPTH) ** 0.25
ATTN_SCALE = HEAD_DIM ** -0.5

A_Q_W = A_HEADS * HEAD_DIM
A_KV_W = A_KV_HEADS * HEAD_DIM
B_W = B_HEADS * HEAD_DIM
OFF_AK = A_Q_W
OFF_AV = OFF_AK + A_KV_W
OFF_BQ = OFF_AV + A_KV_W
OFF_BK = OFF_BQ + B_W
OFF_BV = OFF_BK + B_W
OFF_C = OFF_BV + B_W
IN_COLS = OFF_C + 2 * C_W

VMEM_LIMIT_BYTES = 56 * 1024 * 1024
LANES = 128

ROW_TILE = 256
MATMUL_ROW_TILE = 512
PROJ_COL_TILE = 512
MOD_COL_TILE = 1024
MOE_TILE = 256

MOD_SH1, MOD_SC1, MOD_G1, MOD_SH2, MOD_SC2, MOD_G2 = range(6)


def _params(*sem):
    return pltpu.CompilerParams(dimension_semantics=sem, vmem_limit_bytes=VMEM_LIMIT_BYTES)


def _layer_norm(t, g, b):
    mu = jnp.mean(t, axis=-1, keepdims=True)
    d = t - mu
    var = jnp.mean(d * d, axis=-1, keepdims=True)
    return d * lax.rsqrt(var + LN_EPS) * g + b


def _sigmoid(v):
    return 1.0 / (1.0 + jnp.exp(-v))


def _dot_nt(a, b):
    return lax.dot_general(a, b, (((1,), (1,)), ((), ())), preferred_element_type=F32)


def _mod_kernel(c_ref, w_ref, b_ref, o_ref):
    w = w_ref[...]
    reps = w.shape[1] // LANES
    rows = []
    for r in range(2):
        cv = c_ref[r]
        s = cv * _sigmoid(cv)
        sb = jnp.concatenate([s] * reps, axis=1)
        rows.append(jnp.sum(w * sb, axis=0, keepdims=True) + b_ref[...])
    rows.append(jnp.zeros((6, w.shape[1]), F32))
    o_ref[...] = jnp.concatenate(rows, axis=0)


def _modulation(c, c_ctx, w_mod, b_mod):
    d = D_MODEL
    cb = jnp.stack([jnp.broadcast_to(c.reshape(d, 1), (d, LANES)),
                    jnp.broadcast_to(c_ctx.reshape(d, 1), (d, LANES))])
    n_out = 6 * d
    return pl.pallas_call(
        _mod_kernel,
        grid=(DEPTH, n_out // MOD_COL_TILE),
        in_specs=[pl.BlockSpec((2, d, LANES), lambda l, j: (0, 0, 0)),
                  pl.BlockSpec((None, d, MOD_COL_TILE), lambda l, j: (l, 0, j)),
                  pl.BlockSpec((None, 1, MOD_COL_TILE), lambda l, j: (l, 0, j))],
        out_specs=pl.BlockSpec((None, 8, MOD_COL_TILE), lambda l, j: (l, 0, j)),
        out_shape=jax.ShapeDtypeStruct((DEPTH, 8, n_out), F32),
        compiler_params=_params("arbitrary", "arbitrary"),
        name="modulation",
    )(cb, w_mod, b_mod.reshape(DEPTH, 1, n_out))


def _gelu_tanh(v):
    return 0.5 * v * (1.0 + jnp.tanh(np.sqrt(2.0 / np.pi).astype(np.float32) * (v + 0.044715 * (v * v * v))))


def _proj_kernel(x_ref, mod_ref, cos_ref, sin_ref, w_ref, o_ref, *, rope):
    x = x_ref[...]
    h = (x * (1.0 + mod_ref[MOD_SC1:MOD_SC1 + 1, :]) + mod_ref[MOD_SH1:MOD_SH1 + 1, :]).astype(BF16)
    tn = PROJ_COL_TILE
    for j in range(IN_COLS // tn):
        c0 = j * tn
        acc = jnp.dot(h, w_ref[:, c0:c0 + tn], preferred_element_type=F32)
        if c0 < OFF_AV:
            if rope:
                cos = cos_ref[...]
                sin = sin_ref[...]
                parts = []
                for hh in range(tn // HEAD_DIM):
                    a = acc[:, hh * HEAD_DIM:(hh + 1) * HEAD_DIM]
                    parts.append(a * cos + pltpu.roll(a, HEAD_DIM // 2, 1) * sin)
                acc = jnp.concatenate(parts, axis=1)
        elif c0 >= OFF_C:
            acc = _gelu_tanh(acc)
        o_ref[:, c0:c0 + tn] = acc.astype(BF16)


def _matmul_row_tile(*row_counts):
    return MATMUL_ROW_TILE if all(r % MATMUL_ROW_TILE == 0 for r in row_counts) else ROW_TILE


def _proj(x, x_row0, rows, mod, cos, sin, w_bf, layer, rope):
    d = D_MODEL
    tm = _matmul_row_tile(rows, x_row0)
    x_off = x_row0 // tm
    return pl.pallas_call(
        functools.partial(_proj_kernel, rope=rope),
        grid=(rows // tm,),
        in_specs=[pl.BlockSpec((tm, d), lambda i: (i + x_off, 0)),
                  pl.BlockSpec((8, d), lambda i: (0, 0)),
                  pl.BlockSpec((tm, HEAD_DIM), lambda i: (i, 0)),
                  pl.BlockSpec((tm, HEAD_DIM), lambda i: (i, 0)),
                  pl.BlockSpec((None, d, IN_COLS), lambda i: (layer, 0, 0), pipeline_mode=pl.Buffered(1))],
        out_specs=pl.BlockSpec((tm, IN_COLS), lambda i: (i, 0)),
        out_shape=jax.ShapeDtypeStruct((rows, IN_COLS), BF16),
        compiler_params=_params("arbitrary"),
        name="proj_rope" if rope else "proj_ctx",
    )(x, mod, cos, sin, w_bf)


LOG2E = 1.4426950408889634
LOGIT_SCALE = ATTN_SCALE * LOG2E


def _softmax_pv(problems):
    ms = []
    for s_parts, _, sink in problems:
        m = s_parts[0].max(axis=-1, keepdims=True)
        for s in s_parts[1:]:
            m = jnp.maximum(m, s.max(axis=-1, keepdims=True))
        ms.append(m if sink is None else jnp.maximum(m, sink))
    es = [[jnp.exp2(s - m).astype(BF16) for s in s_parts] for (s_parts, _, _), m in zip(problems, ms)]
    outs = []
    for (_, v_parts, sink), m, e_parts in zip(problems, ms, es):
        acc = None
        for e, v in zip(e_parts, v_parts):
            v1 = jnp.concatenate([v, jnp.ones((v.shape[0], LANES), BF16)], axis=1)
            pv = jnp.dot(e, v1, preferred_element_type=F32)
            acc = pv if acc is None else acc + pv
        denom = acc[:, HEAD_DIM:HEAD_DIM + 1]
        if sink is not None:
            denom = denom + jnp.exp2(sink - m)
        outs.append(acc[:, :HEAD_DIM] / denom)
    return outs


A_STEP_BLOCKS = 4


def _attn_a_kernel(sink_ref, q_ref, *refs, latent, step_blocks):
    nband = step_blocks + 2
    k_refs, v_refs = refs[:nband], refs[nband:2 * nband]
    kx_ref, vx_ref, u_ref, v_ref, sg_ref, sb_ref, sw_ref, sbs_ref, o_ref, oc_ref, mask_ref = refs[2 * nband:]
    i = pl.program_id(0)
    nb = pl.num_programs(0) * step_blocks
    nq = A_GROUP * A_BLOCK

    if latent:
        @pl.when(i == 0)
        def _():
            qi = lax.broadcasted_iota(jnp.int32, (nq, 3 * A_BLOCK), 0) % A_BLOCK
            jj = lax.broadcasted_iota(jnp.int32, (nq, 3 * A_BLOCK), 1)
            ok = (jj >= qi) & (jj <= qi + 2 * A_BLOCK)
            mask_ref[0] = jnp.where(ok & (jj >= A_BLOCK), 0.0, NEG_INF).astype(F32)
            mask_ref[1] = jnp.where(ok, 0.0, NEG_INF).astype(F32)
            mask_ref[2] = jnp.where(ok & (jj < 2 * A_BLOCK), 0.0, NEG_INF).astype(F32)

    sinks = [jnp.concatenate([jnp.full((A_BLOCK, 1), sink_ref[kh * A_GROUP + g] * LOG2E, F32)
                              for g in range(A_GROUP)], axis=0) for kh in range(A_KV_HEADS)]
    for j in range(step_blocks):
        rows = pl.ds(j * A_BLOCK, A_BLOCK)
        _sgu_chunk(u_ref.at[rows, :], v_ref.at[rows, :], sg_ref, sb_ref, sw_ref, sbs_ref, oc_ref.at[rows, :])
        if latent:
            blk = i * step_blocks + j
            which = jnp.where(blk == 0, 0, jnp.where(blk == nb - 1, 2, 1))
        problems = []
        for kh in range(A_KV_HEADS):
            ks = slice(kh * HEAD_DIM, (kh + 1) * HEAD_DIM)
            q = jnp.concatenate([q_ref[rows, (kh * A_GROUP + g) * HEAD_DIM:(kh * A_GROUP + g + 1) * HEAD_DIM]
                                 for g in range(A_GROUP)], axis=0)
            s_parts = [_dot_nt(q, kx_ref[:, ks]) * LOGIT_SCALE]
            v_parts = [vx_ref[:, ks]]
            if latent:
                kband = jnp.concatenate([r[:, ks] for r in k_refs[j:j + 3]], axis=0)
                vband = jnp.concatenate([r[:, ks] for r in v_refs[j:j + 3]], axis=0)
                s_parts.append(_dot_nt(q, kband) * LOGIT_SCALE + mask_ref[which])
                v_parts.append(vband)
            problems.append((s_parts, v_parts, sinks[kh]))
        for kh, out in enumerate(_softmax_pv(problems)):
            for g in range(A_GROUP):
                h = kh * A_GROUP + g
                o_ref[rows, h * HEAD_DIM:(h + 1) * HEAD_DIM] = out[g * A_BLOCK:(g + 1) * A_BLOCK].astype(BF16)


def _attn_a_sgu(p, pc, sink, sgu_params, latent):
    assert CHUNK == A_BLOCK
    rows = p.shape[0]
    nb = rows // A_BLOCK
    step_blocks = A_STEP_BLOCKS if nb % A_STEP_BLOCKS == 0 else 2
    assert nb >= 2 and nb % step_blocks == 0
    kcol = OFF_AK // A_KV_W
    vcol = OFF_AV // A_KV_W
    ucol = OFF_C // C_W
    tile = step_blocks * A_BLOCK
    nband = step_blocks + 2

    def band(col):
        return [pl.BlockSpec((A_BLOCK, A_KV_W),
                             lambda i, s, sh=shift: (jnp.clip(i * step_blocks + sh, 0, nb - 1), col))
                for shift in range(-1, step_blocks + 1)]

    lctx = pc.shape[0]
    return pl.pallas_call(
        functools.partial(_attn_a_kernel, latent=latent, step_blocks=step_blocks),
        grid_spec=pltpu.PrefetchScalarGridSpec(
            num_scalar_prefetch=1,
            grid=(nb // step_blocks,),
            in_specs=([pl.BlockSpec((tile, A_Q_W), lambda i, s: (i, 0))] + band(kcol) + band(vcol)
                      + [pl.BlockSpec((lctx, A_KV_W), lambda i, s: (0, kcol)),
                         pl.BlockSpec((lctx, A_KV_W), lambda i, s: (0, vcol)),
                         pl.BlockSpec((tile, C_W), lambda i, s: (i, ucol)),
                         pl.BlockSpec((tile, C_W), lambda i, s: (i, ucol + 1)),
                         pl.BlockSpec((1, C_W), lambda i, s: (0, 0)),
                         pl.BlockSpec((1, C_W), lambda i, s: (0, 0)),
                         pl.BlockSpec((C_GROUPS, CHUNK, CHUNK), lambda i, s: (0, 0, 0)),
                         pl.BlockSpec((C_GROUPS, CHUNK, LANES), lambda i, s: (0, 0, 0))]),
            out_specs=[pl.BlockSpec((tile, A_Q_W), lambda i, s: (i, 0)),
                       pl.BlockSpec((tile, C_W), lambda i, s: (i, 0))],
            scratch_shapes=[pltpu.VMEM((3, A_GROUP * A_BLOCK, 3 * A_BLOCK), F32)]),
        out_shape=[jax.ShapeDtypeStruct((rows, A_Q_W), BF16), jax.ShapeDtypeStruct((rows, C_W), BF16)],
        compiler_params=_params("arbitrary"),
        name="attn_a_latent" if latent else "attn_a_ctx",
    )(*([sink] + [p] * (1 + 2 * nband) + [pc, pc, p, p] + list(sgu_params)))


B_PAIR_W = 2 * HEAD_DIM
NA_TILE = 256
NA_GROUP_ROWS = NA_TILE // GRID_W
NA_WIN_ROWS = NA_ROWS + NA_GROUP_ROWS
NA_PAIRS = NA_WIN_ROWS // 2
NA_BIAS_OFFS = 2 * NA_ROWS
NA_BOTH, NA_LEFT, NA_RIGHT = range(3)


def _attn_b_kernel(*refs, latent, grid_rows):
    npair = B_HEADS // 2
    q_refs, k_refs, v_refs, kx_refs, vx_refs = (refs[j * npair:(j + 1) * npair] for j in range(5))
    bias_ref, o_ref = refs[5 * npair], refs[5 * npair + 1]
    g = pl.program_id(0)
    if latent:
        r_base = g * NA_GROUP_ROWS
        w0 = jnp.clip(r_base - NA_ROWS // 2, 0, grid_rows - NA_WIN_ROWS)
        start = pl.multiple_of(w0 * GRID_W, GRID_W)
    for hp in range(npair):
        problems = []
        for hh in range(2):
            hs = slice(hh * HEAD_DIM, (hh + 1) * HEAD_DIM)
            q = q_refs[hp][:, hs]
            s_parts = [_dot_nt(q, kx_refs[hp][:, hs]) * LOGIT_SCALE]
            v_parts = [vx_refs[hp][:, hs]]
            if latent:
                kwin = k_refs[hp][pl.ds(start, NA_WIN_ROWS * GRID_W), hs]
                vwin = v_refs[hp][pl.ds(start, NA_WIN_ROWS * GRID_W), hs]
                bias_rows = []
                for rr in range(NA_GROUP_ROWS):
                    r = r_base + rr
                    r0 = jnp.clip(r - NA_ROWS // 2, 0, grid_rows - NA_ROWS)
                    tiles = []
                    for jp in range(NA_PAIRS):
                        ka = w0 + 2 * jp
                        in_a = (ka >= r0) & (ka < r0 + NA_ROWS)
                        in_b = (ka + 1 >= r0) & (ka + 1 < r0 + NA_ROWS)
                        variant = jnp.where(in_a, jnp.where(in_b, NA_BOTH, NA_LEFT),
                                            jnp.where(in_b, NA_RIGHT, NA_LEFT))
                        off = jnp.where(in_a | in_b, jnp.clip(ka - r + NA_ROWS, 0, NA_BIAS_OFFS - 1), 0)
                        tiles.append(bias_ref[2 * hp + hh, variant, off])
                    bias_rows.append(jnp.concatenate(tiles, axis=1))
                bias = jnp.concatenate(bias_rows, axis=0)
                s_parts.append(_dot_nt(q, kwin) * LOGIT_SCALE + bias)
                v_parts.append(vwin)
            problems.append((s_parts, v_parts, None))
        for hh, out in enumerate(_softmax_pv(problems)):
            h = 2 * hp + hh
            o_ref[:, h * HEAD_DIM:(h + 1) * HEAD_DIM] = out.astype(BF16)


def _na_bias_tables(rpb):
    cols = np.arange(GRID_W)
    c0 = np.clip(cols - NA_COLS // 2, 0, GRID_W - NA_COLS)
    rel = cols[None, :] - cols[:, None] + NA_COLS - 1
    ok = (cols[None, :] >= c0[:, None]) & (cols[None, :] < c0[:, None] + NA_COLS)
    onehot = (rel[None] == np.arange(2 * NA_COLS - 1)[:, None, None]).astype(np.float32)
    t = jnp.einsum("lhrd,dqk->lhrqk", rpb, onehot, precision=lax.Precision.HIGHEST)
    t = jnp.where(ok, t * LOG2E, NEG_INF).astype(F32)
    t = jnp.pad(t, ((0, 0), (0, 0), (1, 1), (0, 0), (0, 0)), constant_values=NEG_INF)
    pairs = jnp.concatenate([t[:, :, :-1], t[:, :, 1:]], axis=-1)
    keep = np.ones((3, 1, 1, 2 * GRID_W), bool)
    keep[NA_LEFT, :, :, GRID_W:] = False
    keep[NA_RIGHT, :, :, :GRID_W] = False
    return jnp.where(keep, pairs[:, :, None], NEG_INF)


def _attn_b(p, pc, bias_tabs, layer, latent):
    rows = p.shape[0]
    lctx = pc.shape[0]
    assert not latent or (rows // GRID_W >= NA_WIN_ROWS and rows % NA_TILE == 0)
    tile = NA_TILE if latent else ROW_TILE
    npair = B_HEADS // 2
    qcol = OFF_BQ // B_PAIR_W
    kcol = OFF_BK // B_PAIR_W
    vcol = OFF_BV // B_PAIR_W

    def resident(nrows, col):
        return [pl.BlockSpec((nrows, B_PAIR_W), lambda g, c=col + hp: (0, c), pipeline_mode=pl.Buffered(1))
                for hp in range(npair)]

    return pl.pallas_call(
        functools.partial(_attn_b_kernel, latent=latent, grid_rows=rows // GRID_W),
        grid=(rows // tile,),
        in_specs=([pl.BlockSpec((tile, B_PAIR_W), lambda g, c=qcol + hp: (g, c)) for hp in range(npair)]
                  + resident(rows, kcol) + resident(rows, vcol) + resident(lctx, kcol) + resident(lctx, vcol)
                  + [pl.BlockSpec((None, B_HEADS, 3, NA_BIAS_OFFS, GRID_W, 2 * GRID_W),
                                  lambda g: (layer, 0, 0, 0, 0, 0), pipeline_mode=pl.Buffered(1))]),
        out_specs=pl.BlockSpec((tile, B_W), lambda g: (g, 0)),
        out_shape=jax.ShapeDtypeStruct((rows, B_W), BF16),
        compiler_params=_params("arbitrary"),
        name="attn_b_latent" if latent else "attn_b_ctx",
    )(*([p] * (3 * npair) + [pc] * (2 * npair) + [bias_tabs]))


def _sgu_chunk(u_ref, v_ref, g_ref, b_ref, w_ref, bs_ref, o_ref):
    for grp in range(C_GROUPS):
        cs = slice(grp * HEAD_DIM, (grp + 1) * HEAD_DIM)
        vn = _layer_norm(v_ref[:, cs].astype(F32), g_ref[:, cs], b_ref[:, cs])
        mixed = jnp.dot(w_ref[grp], vn.astype(BF16), preferred_element_type=F32) + bs_ref[grp]
        o_ref[:, cs] = (u_ref[:, cs].astype(F32) * mixed).astype(BF16)


def _outproj_kernel(oa_ref, ob_ref, oc_ref, w_ref, x_ref, mod_ref, g_ref, b_ref, *rest):
    x1_ref, h2_ref = rest[-2], rest[-1]
    mix = jnp.dot(oa_ref[...], w_ref[0:A_Q_W, :], preferred_element_type=F32)
    mix += jnp.dot(ob_ref[...], w_ref[A_Q_W:A_Q_W + B_W, :], preferred_element_type=F32)
    mix += jnp.dot(oc_ref[...], w_ref[A_Q_W + B_W:, :], preferred_element_type=F32)
    t = DEEPNORM_ALPHA * x_ref[...] + mod_ref[MOD_G1:MOD_G1 + 1, :] * mix
    x1 = _layer_norm(t, g_ref[...], b_ref[...])
    x1_ref[...] = x1
    h2_ref[...] = x1 * (1.0 + mod_ref[MOD_SC2:MOD_SC2 + 1, :]) + mod_ref[MOD_SH2:MOD_SH2 + 1, :]


def _outproj(o_a, o_b, o_c, w_bf, layer, x, x_row0, mod, ln_g, ln_b, total_rows, out_row0, prev):
    rows = o_a.shape[0]
    d = D_MODEL
    tm = _matmul_row_tile(rows, x_row0, out_row0)
    x_off, out_off = x_row0 // tm, out_row0 // tm
    in_specs = [pl.BlockSpec((tm, A_Q_W), lambda i: (i, 0)),
                pl.BlockSpec((tm, B_W), lambda i: (i, 0)),
                pl.BlockSpec((tm, C_W), lambda i: (i, 0)),
                pl.BlockSpec((None, d, d), lambda i: (layer, 0, 0), pipeline_mode=pl.Buffered(1)),
                pl.BlockSpec((tm, d), lambda i: (i + x_off, 0)),
                pl.BlockSpec((8, d), lambda i: (0, 0)),
                pl.BlockSpec((1, d), lambda i: (0, 0)),
                pl.BlockSpec((1, d), lambda i: (0, 0))]
    args = [o_a, o_b, o_c, w_bf, x, mod, ln_g, ln_b]
    aliases = {}
    if prev is not None:
        in_specs += [pl.BlockSpec(memory_space=pl.ANY), pl.BlockSpec(memory_space=pl.ANY)]
        aliases = {len(args): 0, len(args) + 1: 1}
        args += list(prev)
    return pl.pallas_call(
        _outproj_kernel,
        grid=(rows // tm,),
        in_specs=in_specs,
        out_specs=[pl.BlockSpec((tm, d), lambda i: (i + out_off, 0))] * 2,
        out_shape=[jax.ShapeDtypeStruct((total_rows, d), F32)] * 2,
        input_output_aliases=aliases,
        compiler_params=_params("arbitrary"),
        name="outproj",
    )(*args)


def _top2_sublanes(vals, sub):
    m1 = vals.max(axis=0, keepdims=True)
    i1 = jnp.where(vals == m1, sub, vals.shape[0]).min(axis=0, keepdims=True)
    rest = jnp.where(sub == i1, -jnp.inf, vals)
    m2 = rest.max(axis=0, keepdims=True)
    i2 = jnp.where(rest == m2, sub, vals.shape[0]).min(axis=0, keepdims=True)
    return m1, i1, m2, i2


def _router_kernel(h_ref, wr_ref, rb_ref, e_ref, w_ref, rank_ref, cnt_ref, run_ref):
    i = pl.program_id(0)
    tm = ROW_TILE
    epg = EXPERTS_PER_GROUP

    @pl.when(i == 0)
    def _():
        run_ref[...] = jnp.zeros_like(run_ref)

    h = h_ref[...]
    h_hi = h.astype(BF16)
    h_mid = (h - h_hi.astype(F32)).astype(BF16)
    parts = _dot_nt(wr_ref[...], h_hi)
    parts_mid = _dot_nt(wr_ref[0:2 * N_EXPERTS, :], h_mid)
    logits = (parts[0:N_EXPERTS] + parts[N_EXPERTS:2 * N_EXPERTS] + parts[2 * N_EXPERTS:]
              + parts_mid[0:N_EXPERTS] + parts_mid[N_EXPERTS:])
    scores = _sigmoid(logits)
    biased = scores + jnp.concatenate([rb_ref[...]] * (tm // LANES), axis=1)
    sub = lax.broadcasted_iota(jnp.int32, (epg, tm), 0)

    best = None
    for g in range(N_EXPERT_GROUPS):
        m1, _, m2, _ = _top2_sublanes(biased[g * epg:(g + 1) * epg], sub)
        gs = m1 + m2
        if best is None:
            best, grp = gs, jnp.zeros((1, tm), jnp.int32)
            bsel, ssel = biased[0:epg], scores[0:epg]
        else:
            better = gs > best
            best = jnp.where(better, gs, best)
            grp = jnp.where(better, g, grp)
            bsel = jnp.where(better, biased[g * epg:(g + 1) * epg], bsel)
            ssel = jnp.where(better, scores[g * epg:(g + 1) * epg], ssel)
    _, i1, _, i2 = _top2_sublanes(bsel, sub)
    w1 = jnp.where(sub == i1, ssel, 0.0).sum(axis=0, keepdims=True)
    w2 = jnp.where(sub == i2, ssel, 0.0).sum(axis=0, keepdims=True)
    tot = w1 + w2
    e1 = grp * epg + i1
    e2 = grp * epg + i2

    eiota = lax.broadcasted_iota(jnp.int32, (N_EXPERTS, tm), 0)
    oh1 = (eiota == e1).astype(F32)
    oh2 = (eiota == e2).astype(F32)
    ohb = oh1 + oh2
    before = (lax.broadcasted_iota(jnp.int32, (tm, tm), 0) < lax.broadcasted_iota(jnp.int32, (tm, tm), 1))
    prefix = jnp.dot(ohb.astype(BF16), before.astype(BF16), preferred_element_type=F32)
    pos = run_ref[...] + prefix
    r1 = (oh1 * pos).sum(axis=0, keepdims=True)
    r2 = (oh2 * pos).sum(axis=0, keepdims=True)
    run_ref[...] = run_ref[...] + ohb.sum(axis=1, keepdims=True)

    e_ref[...] = jnp.concatenate([e1, e2], axis=0)
    w_ref[...] = jnp.concatenate([w1 / tot, w2 / tot], axis=0)
    rank_ref[...] = jnp.concatenate([r1, r2], axis=0).astype(jnp.int32)
    cnt_ref[...] = run_ref[:, 0:LANES]


def _router(h2, wr_t, rb_b):
    t = h2.shape[0]
    d = D_MODEL
    row2 = pl.BlockSpec((TOP_K, ROW_TILE), lambda i: (0, i))
    return pl.pallas_call(
        _router_kernel,
        grid=(t // ROW_TILE,),
        in_specs=[pl.BlockSpec((ROW_TILE, d), lambda i: (i, 0)),
                  pl.BlockSpec((3 * N_EXPERTS, d), lambda i: (0, 0)),
                  pl.BlockSpec((N_EXPERTS, LANES), lambda i: (0, 0))],
        out_specs=[row2, row2, row2, pl.BlockSpec((N_EXPERTS, LANES), lambda i: (0, 0))],
        out_shape=[jax.ShapeDtypeStruct((TOP_K, t), jnp.int32),
                   jax.ShapeDtypeStruct((TOP_K, t), F32),
                   jax.ShapeDtypeStruct((TOP_K, t), jnp.int32),
                   jax.ShapeDtypeStruct((N_EXPERTS, LANES), F32)],
        scratch_shapes=[pltpu.VMEM((N_EXPERTS, ROW_TILE), F32)],
        compiler_params=_params("arbitrary"),
        name="router",
    )(h2, wr_t, rb_b)


def _row_copy(src, row, dst, sem):
    return pltpu.make_async_copy(src.at[pl.ds(row, 1), :], dst, sem)


DISPATCH_SLOTS = 3


def _dispatch_kernel(slot_ref, h_hbm, xs_hbm, hbuf, lsem, ssem, *, n_tok):
    i = pl.program_id(0)
    nt = pl.num_programs(0)
    tm = ROW_TILE

    def load(blk):
        row0 = pl.multiple_of(blk * tm, tm)
        return pltpu.make_async_copy(h_hbm.at[pl.ds(row0, tm), :], hbuf.at[blk % DISPATCH_SLOTS],
                                     lsem.at[blk % DISPATCH_SLOTS])

    def wait_rows(par):
        for _ in range(TOP_K):
            pltpu.make_async_copy(hbuf.at[0], xs_hbm.at[pl.ds(0, tm), :], ssem.at[par]).wait()

    @pl.when(i == 0)
    def _():
        load(0).start()

    @pl.when(i + 1 < nt)
    def _():
        load(i + 1).start()

    load(i).wait()
    par = i % 2
    cur = hbuf.at[i % DISPATCH_SLOTS]
    for r in range(tm):
        for k in range(TOP_K):
            dst = xs_hbm.at[pl.ds(slot_ref[k * n_tok + i * tm + r], 1), :]
            pltpu.make_async_copy(cur.at[pl.ds(r, 1), :], dst, ssem.at[par]).start(priority=k)

    @pl.when(i > 0)
    def _():
        wait_rows(1 - par)

    @pl.when(i == nt - 1)
    def _():
        wait_rows(par)


def _dispatch(h2, slots, rows):
    t, d = h2.shape
    return pl.pallas_call(
        functools.partial(_dispatch_kernel, n_tok=t),
        grid_spec=pltpu.PrefetchScalarGridSpec(
            num_scalar_prefetch=1,
            grid=(t // ROW_TILE,),
            in_specs=[pl.BlockSpec(memory_space=pl.ANY)],
            out_specs=pl.BlockSpec(memory_space=pl.ANY),
            scratch_shapes=[pltpu.VMEM((DISPATCH_SLOTS, ROW_TILE, d), F32),
                            pltpu.SemaphoreType.DMA((DISPATCH_SLOTS,)),
                            pltpu.SemaphoreType.DMA((2,))]),
        out_shape=jax.ShapeDtypeStruct((rows, d), F32),
        compiler_params=_params("arbitrary"),
        name="dispatch",
    )(slots, h2)


def _experts_kernel(ts_ref, nu_ref, h_hbm, wg_hbm, wu_hbm, wd_hbm, ys_hbm,
                    xbuf, ybuf, gsem, ysem, wg_f, wu_f, wd_f, wsem, wg_bf, wu_bf, wd_bf, *, layer):
    e = pl.program_id(0)
    ne = pl.num_programs(0)
    n_used = nu_ref[0]
    tm = MOE_TILE
    tsub = tm

    def has_rows(ex):
        return ts_ref[ex + 1] > ts_ref[ex]

    def weight_copies(ex, wslot):
        return [pltpu.make_async_copy(src.at[layer, ex], dst.at[wslot], wsem.at[wslot])
                for src, dst in ((wg_hbm, wg_f), (wu_hbm, wu_f), (wd_hbm, wd_f))]

    def start_weights(ex, wslot):
        for cp in weight_copies(ex, wslot):
            cp.start(priority=1)

    def x_load(tile, slot):
        row0 = pl.multiple_of(tile * tsub, tsub)
        return pltpu.make_async_copy(h_hbm.at[pl.ds(row0, tsub), :], xbuf.at[slot], gsem.at[slot])

    def gather(tile, slot):
        x_load(tile, slot).start()

    def wait_gather(slot):
        x_load(0, slot).wait()

    def y_store(tile, slot):
        row0 = pl.multiple_of(tile * tsub, tsub)
        return pltpu.make_async_copy(ybuf.at[slot], ys_hbm.at[pl.ds(row0, tsub), :], ysem.at[slot])

    wslot = e % 2

    @pl.when((e == 0) & has_rows(0))
    def _():
        start_weights(0, 0)

    @pl.when((e == 0) & (n_used > 0))
    def _():
        gather(0, 0)

    nxt = jnp.minimum(e + 1, ne - 1)

    @pl.when((e + 1 < ne) & has_rows(nxt))
    def _():
        start_weights(nxt, 1 - wslot)

    @pl.when(has_rows(e))
    def _():
        for cp in weight_copies(e, wslot):
            cp.wait()
        wg_bf[...] = wg_f[wslot].astype(BF16)
        wu_bf[...] = wu_f[wslot].astype(BF16)
        wd_bf[...] = wd_f[wslot].astype(BF16)

    def tile_body(g, carry):
        slot = g % 2

        @pl.when(g >= 2)
        def _():
            y_store(g - 2, slot).wait()

        wait_gather(slot)
        x = xbuf[slot].astype(BF16)
        gather(jnp.minimum(g + 1, n_used - 1), 1 - slot)
        gate = jnp.dot(x, wg_bf[...], preferred_element_type=F32)
        up = jnp.dot(x, wu_bf[...], preferred_element_type=F32)
        hid = (gate * _sigmoid(gate)) * up
        ybuf[slot] = jnp.dot(hid.astype(BF16), wd_bf[...], preferred_element_type=F32)
        y_store(g, slot).start(priority=1)
        return carry

    lax.fori_loop(ts_ref[e], ts_ref[e + 1], tile_body, 0)

    @pl.when((e == pl.num_programs(0) - 1) & (n_used > 0))
    def _():
        wait_gather(n_used % 2)
        y_store(n_used - 1, (n_used - 1) % 2).wait()

        @pl.when(n_used >= 2)
        def _():
            y_store(n_used - 2, n_used % 2).wait()


def _experts(xs, tile_start, n_used, w_gate, w_up, w_down, layer):
    d = D_MODEL
    tm = MOE_TILE
    rows = xs.shape[0]

    hbm = pl.BlockSpec(memory_space=pl.ANY)
    return pl.pallas_call(
        functools.partial(_experts_kernel, layer=layer),
        grid_spec=pltpu.PrefetchScalarGridSpec(
            num_scalar_prefetch=2,
            grid=(N_EXPERTS,),
            in_specs=[hbm, hbm, hbm, hbm],
            out_specs=hbm,
            scratch_shapes=[pltpu.VMEM((2, tm, d), F32),
                            pltpu.VMEM((2, tm, d), F32),
                            pltpu.SemaphoreType.DMA((2,)),
                            pltpu.SemaphoreType.DMA((2,)),
                            pltpu.VMEM((2, d, EXPERT_FF), F32),
                            pltpu.VMEM((2, d, EXPERT_FF), F32),
                            pltpu.VMEM((2, EXPERT_FF, d), F32),
                            pltpu.SemaphoreType.DMA((2,)),
                            pltpu.VMEM((d, EXPERT_FF), BF16),
                            pltpu.VMEM((d, EXPERT_FF), BF16),
                            pltpu.VMEM((EXPERT_FF, d), BF16)]),
        out_shape=jax.ShapeDtypeStruct((rows, d), F32),
        compiler_params=_params("arbitrary"),
        name="experts",
    )(tile_start, n_used, xs, w_gate, w_up, w_down)


def _combine_kernel(slot_ref, ys_hbm, w_ref, x_ref, mod_ref, g_ref, b_ref, o_ref, ybuf, sem, *, n_tok):
    i = pl.program_id(0)
    nt = pl.num_programs(0)
    tm = ROW_TILE

    def issue(blk, buf):
        for r in range(tm):
            for k in range(TOP_K):
                row = slot_ref[k * n_tok + blk * tm + r]
                _row_copy(ys_hbm, row, ybuf.at[buf, k, pl.ds(r, 1), :], sem.at[buf]).start(priority=k)

    def wait(buf):
        for k in range(TOP_K):
            pltpu.make_async_copy(ys_hbm.at[pl.ds(0, tm), :], ybuf.at[buf, k], sem.at[buf]).wait()

    @pl.when(i == 0)
    def _():
        issue(0, 0)

    buf = i % 2
    wait(buf)
    issue(jnp.minimum(i + 1, nt - 1), 1 - buf)
    y = w_ref[:, 0:1] * ybuf[buf, 0] + w_ref[:, 1:2] * ybuf[buf, 1]
    t = DEEPNORM_ALPHA * x_ref[...] + mod_ref[MOD_G2:MOD_G2 + 1, :] * y
    o_ref[...] = _layer_norm(t, g_ref[...], b_ref[...])

    @pl.when(i == nt - 1)
    def _():
        wait(1 - buf)


def _combine(ys, slots, w_tok, x1, mods, n_lat_blocks, ln_g, ln_b):
    t = x1.shape[0]
    d = D_MODEL
    return pl.pallas_call(
        functools.partial(_combine_kernel, n_tok=t),
        grid_spec=pltpu.PrefetchScalarGridSpec(
            num_scalar_prefetch=1,
            grid=(t // ROW_TILE,),
            in_specs=[pl.BlockSpec(memory_space=pl.ANY),
                      pl.BlockSpec((ROW_TILE, TOP_K), lambda i, s: (i, 0)),
                      pl.BlockSpec((ROW_TILE, d), lambda i, s: (i, 0)),
                      pl.BlockSpec((None, 8, d), lambda i, s: (jnp.where(i < n_lat_blocks, 1, 0), 0, 0)),
                      pl.BlockSpec((1, d), lambda i, s: (0, 0)),
                      pl.BlockSpec((1, d), lambda i, s: (0, 0))],
            out_specs=pl.BlockSpec((ROW_TILE, d), lambda i, s: (i, 0)),
            scratch_shapes=[pltpu.VMEM((2, TOP_K, ROW_TILE, d), F32),
                            pltpu.SemaphoreType.DMA((2,))]),
        out_shape=jax.ShapeDtypeStruct((t, d), F32),
        compiler_params=_params("arbitrary"),
        name="combine",
    )(slots, ys, w_tok, x1, mods, ln_g, ln_b)


def _moe(h2, x1, mods, n_lat_blocks, wr_t, rb_b, w_gate, w_up, w_down, layer, ln_g, ln_b):
    t = x1.shape[0]
    tm = MOE_TILE
    e_idx, w_tok, rank, cnt = _router(h2, wr_t, rb_b)
    counts = cnt[:, 0].astype(jnp.int32)
    tiles_per = (counts + tm - 1) // tm
    tile_end = jnp.cumsum(tiles_per)
    n_used = tile_end[-1]
    row_off = (tile_end - tiles_per) * tm
    experts = jnp.arange(N_EXPERTS, dtype=jnp.int32)
    slots = jnp.sum(jnp.where(e_idx[:, :, None] == experts, row_off, 0), axis=-1) + rank
    max_tiles = (TOP_K * t + N_EXPERTS * (tm - 1)) // tm + 1
    tile_start = jnp.concatenate([tile_end - tiles_per, n_used.reshape(1)]).astype(jnp.int32)
    slots = slots.reshape(-1)
    xs = _dispatch(h2, slots, max_tiles * tm)
    ys = _experts(xs, tile_start, n_used.reshape(1).astype(jnp.int32), w_gate, w_up, w_down, layer)
    return _combine(ys, slots, w_tok.T, x1, mods, n_lat_blocks, ln_g, ln_b)


def _rope_tables(n):
    t = np.arange(n)
    row = (t // GRID_W).astype(np.float64)
    col = (t % GRID_W).astype(np.float64)
    n_freq = HEAD_DIM // 4
    inv_freq = ROPE_BASE ** (-np.arange(n_freq, dtype=np.float64) / n_freq)
    ang = np.concatenate([row[:, None] * inv_freq, col[:, None] * inv_freq], axis=-1)
    cos, sin = np.cos(ang), np.sin(ang)
    return (np.concatenate([cos, cos], axis=-1).astype(np.float32),
            np.concatenate([-sin, sin], axis=-1).astype(np.float32))


def kernel(x, c, ctx, c_ctx, w_mod, b_mod, w_in, attn_sink, na_rpb, sgu_ln_g, sgu_ln_b, sgu_w, sgu_b,
           w_out, ln1_g, ln1_b, w_router, router_bias, w_gate, w_up, w_down, ln2_g, ln2_b):
    batch, n, d = x.shape
    lctx = ctx.shape[1]
    assert batch == 1 and d == D_MODEL and n % ROW_TILE == 0 and lctx % ROW_TILE == 0
    n_lat_blocks = n // ROW_TILE

    mods = _modulation(c, c_ctx, w_mod, b_mod).reshape(DEPTH, 8, 6, d)
    mod_lat = jnp.pad(mods[:, 0], ((0, 0), (0, 2), (0, 0)))
    mod_ctx = jnp.pad(mods[:, 1], ((0, 0), (0, 2), (0, 0)))
    cos, sin = _rope_tables(n)
    wr_f = w_router.T
    def bf16_part(v):
        return lax.bitcast_convert_type(lax.bitcast_convert_type(v, jnp.int32) & jnp.int32(-65536), F32)

    wr_hi = bf16_part(wr_f)
    wr_mid = bf16_part(wr_f - wr_hi)
    wr_lo = wr_f - wr_hi - wr_mid
    wr_t = jnp.concatenate([wr_hi, wr_mid, wr_lo], axis=0).astype(BF16)
    rb_b = jnp.broadcast_to(router_bias.reshape(N_EXPERTS, 1), (N_EXPERTS, LANES))

    x_lat = x[0]
    x_ctx, ctx_row0 = ctx[0], 0
    w_in_bf = w_in.astype(BF16)
    w_out_bf = w_out.astype(BF16)
    bias_tabs = _na_bias_tables(na_rpb)
    for l in range(DEPTH):
        last = l == DEPTH - 1
        sgu_params = (sgu_ln_g[l].reshape(1, C_W), sgu_ln_b[l].reshape(1, C_W), sgu_w[l].astype(BF16),
                      jnp.broadcast_to(sgu_b[l][:, :, None], (C_GROUPS, CHUNK, LANES)))
        g1, b1 = ln1_g[l].reshape(1, d), ln1_b[l].reshape(1, d)
        g2, b2 = ln2_g[l].reshape(1, d), ln2_b[l].reshape(1, d)

        p = _proj(x_lat, 0, n, mod_lat[l], cos, sin, w_in_bf, l, rope=True)
        pc = _proj(x_ctx, ctx_row0, lctx, mod_ctx[l], cos, sin, w_in_bf, l, rope=False)
        o_a, o_c = _attn_a_sgu(p, pc, attn_sink[l], sgu_params, latent=True)
        o_b = _attn_b(p, pc, bias_tabs, l, latent=True)
        mod_both = jnp.stack([mod_ctx[l], mod_lat[l]])
        if last:
            x1, h2 = _outproj(o_a, o_b, o_c, w_out_bf, l, x_lat, 0, mod_lat[l], g1, b1, n, 0, None)
            x_lat = _moe(h2, x1, mod_both, n_lat_blocks, wr_t, rb_b, w_gate, w_up, w_down, l, g2, b2)
        else:
            oc_a, oc_c = _attn_a_sgu(pc, pc, attn_sink[l], sgu_params, latent=False)
            oc_b = _attn_b(pc, pc, bias_tabs, l, latent=False)
            total = n + lctx
            prev = _outproj(o_a, o_b, o_c, w_out_bf, l, x_lat, 0, mod_lat[l], g1, b1, total, 0, None)
            x1, h2 = _outproj(oc_a, oc_b, oc_c, w_out_bf, l, x_ctx, ctx_row0, mod_ctx[l], g1, b1, total, n,
                              prev)
            x_all = _moe(h2, x1, mod_both, n_lat_blocks, wr_t, rb_b, w_gate, w_up, w_down, l, g2, b2)
            x_lat, x_ctx, ctx_row0 = x_all, x_all, n
    return x_lat.reshape(batch, n, d)
```

```python
import functools

import numpy as np
import jax
import jax.numpy as jnp
from jax import lax
from jax.experimental import pallas as pl
from jax.experimental.pallas import tpu as pltpu

F32 = jnp.float32
BF16 = jnp.bfloat16

D_MODEL = 2048
DEPTH = 2
GRID_W = 64
HEAD_DIM = 128
A_HEADS = 6
A_KV_HEADS = 2
A_GROUP = A_HEADS // A_KV_HEADS
A_BLOCK = 128
B_HEADS = 6
NA_ROWS = 8
NA_COLS = 16
C_GROUPS = 4
C_W = C_GROUPS * HEAD_DIM
CHUNK = 128
N_EXPERTS = 32
N_EXPERT_GROUPS = 4
EXPERTS_PER_GROUP = N_EXPERTS // N_EXPERT_GROUPS
TOP_K = 2
EXPERT_FF = 512
ROPE_BASE = 10000.0
LN_EPS = 1e-5
NEG_INF = -1e30
DEEPNORM_ALPHA = (2 * DEPTH) ** 0.25
ATTN_SCALE = HEAD_DIM ** -0.5

A_Q_W = A_HEADS * HEAD_DIM
A_KV_W = A_KV_HEADS * HEAD_DIM
B_W = B_HEADS * HEAD_DIM
OFF_AK = A_Q_W
OFF_AV = OFF_AK + A_KV_W
OFF_BQ = OFF_AV + A_KV_W
OFF_BK = OFF_BQ + B_W
OFF_BV = OFF_BK + B_W
OFF_C = OFF_BV + B_W
IN_COLS = OFF_C + 2 * C_W

VMEM_LIMIT_BYTES = 56 * 1024 * 1024
LANES = 128

ROW_TILE = 256
MATMUL_ROW_TILE = 512
PROJ_COL_TILE = 512
MOD_COL_TILE = 1024
MOE_TILE = 256

MOD_SH1, MOD_SC1, MOD_G1, MOD_SH2, MOD_SC2, MOD_G2 = range(6)


def _params(*sem):
    return pltpu.CompilerParams(dimension_semantics=sem, vmem_limit_bytes=VMEM_LIMIT_BYTES)


def _layer_norm(t, g, b):
    mu = jnp.mean(t, axis=-1, keepdims=True)
    d = t - mu
    var = jnp.mean(d * d, axis=-1, keepdims=True)
    return d * lax.rsqrt(var + LN_EPS) * g + b


def _sigmoid(v):
    return 1.0 / (1.0 + jnp.exp(-v))


def _dot_nt(a, b):
    return lax.dot_general(a, b, (((1,), (1,)), ((), ())), preferred_element_type=F32)


def _mod_kernel(c_ref, w_ref, b_ref, o_ref):
    w = w_ref[...]
    reps = w.shape[1] // LANES
    rows = []
    for r in range(2):
        cv = c_ref[r]
        s = cv * _sigmoid(cv)
        sb = jnp.concatenate([s] * reps, axis=1)
        rows.append(jnp.sum(w * sb, axis=0, keepdims=True) + b_ref[...])
    rows.append(jnp.zeros((6, w.shape[1]), F32))
    o_ref[...] = jnp.concatenate(rows, axis=0)


def _modulation(c, c_ctx, w_mod, b_mod):
    d = D_MODEL
    cb = jnp.stack([jnp.broadcast_to(c.reshape(d, 1), (d, LANES)),
                    jnp.broadcast_to(c_ctx.reshape(d, 1), (d, LANES))])
    n_out = 6 * d
    return pl.pallas_call(
        _mod_kernel,
        grid=(DEPTH, n_out // MOD_COL_TILE),
        in_specs=[pl.BlockSpec((2, d, LANES), lambda l, j: (0, 0, 0)),
                  pl.BlockSpec((None, d, MOD_COL_TILE), lambda l, j: (l, 0, j)),
                  pl.BlockSpec((None, 1, MOD_COL_TILE), lambda l, j: (l, 0, j))],
        out_specs=pl.BlockSpec((None, 8, MOD_COL_TILE), lambda l, j: (l, 0, j)),
        out_shape=jax.ShapeDtypeStruct((DEPTH, 8, n_out), F32),
        compiler_params=_params("arbitrary", "arbitrary"),
        name="modulation",
    )(cb, w_mod, b_mod.reshape(DEPTH, 1, n_out))


def _gelu_tanh(v):
    return 0.5 * v * (1.0 + jnp.tanh(np.sqrt(2.0 / np.pi).astype(np.float32) * (v + 0.044715 * (v * v * v))))


def _proj_kernel(x_ref, mod_ref, cos_ref, sin_ref, w_ref, o_ref, *, rope):
    x = x_ref[...]
    h = (x * (1.0 + mod_ref[MOD_SC1:MOD_SC1 + 1, :]) + mod_ref[MOD_SH1:MOD_SH1 + 1, :]).astype(BF16)
    tn = PROJ_COL_TILE
    for j in range(IN_COLS // tn):
        c0 = j * tn
        acc = jnp.dot(h, w_ref[:, c0:c0 + tn], preferred_element_type=F32)
        if c0 < OFF_AV:
            if rope:
                cos = cos_ref[...]
                sin = sin_ref[...]
                parts = []
                for hh in range(tn // HEAD_DIM):
                    a = acc[:, hh * HEAD_DIM:(hh + 1) * HEAD_DIM]
                    parts.append(a * cos + pltpu.roll(a, HEAD_DIM // 2, 1) * sin)
                acc = jnp.concatenate(parts, axis=1)
        elif c0 >= OFF_C:
            acc = _gelu_tanh(acc)
        o_ref[:, c0:c0 + tn] = acc.astype(BF16)


def _matmul_row_tile(*row_counts):
    return MATMUL_ROW_TILE if all(r % MATMUL_ROW_TILE == 0 for r in row_counts) else ROW_TILE


def _proj(x, x_row0, rows, mod, cos, sin, w_bf, layer, rope):
    d = D_MODEL
    tm = _matmul_row_tile(rows, x_row0)
    x_off = x_row0 // tm
    return pl.pallas_call(
        functools.partial(_proj_kernel, rope=rope),
        grid=(rows // tm,),
        in_specs=[pl.BlockSpec((tm, d), lambda i: (i + x_off, 0)),
                  pl.BlockSpec((8, d), lambda i: (0, 0)),
                  pl.BlockSpec((tm, HEAD_DIM), lambda i: (i, 0)),
                  pl.BlockSpec((tm, HEAD_DIM), lambda i: (i, 0)),
                  pl.BlockSpec((None, d, IN_COLS), lambda i: (layer, 0, 0), pipeline_mode=pl.Buffered(1))],
        out_specs=pl.BlockSpec((tm, IN_COLS), lambda i: (i, 0)),
        out_shape=jax.ShapeDtypeStruct((rows, IN_COLS), BF16),
        compiler_params=_params("arbitrary"),
        name="proj_rope" if rope else "proj_ctx",
    )(x, mod, cos, sin, w_bf)


LOG2E = 1.4426950408889634
LOGIT_SCALE = ATTN_SCALE * LOG2E


def _softmax_pv(problems):
    ms = []
    for s_parts, _, sink in problems:
        m = s_parts[0].max(axis=-1, keepdims=True)
        for s in s_parts[1:]:
            m = jnp.maximum(m, s.max(axis=-1, keepdims=True))
        ms.append(m if sink is None else jnp.maximum(m, sink))
    es = [[jnp.exp2(s - m).astype(BF16) for s in s_parts] for (s_parts, _, _), m in zip(problems, ms)]
    outs = []
    for (_, v_parts, sink), m, e_parts in zip(problems, ms, es):
        acc = None
        for e, v in zip(e_parts, v_parts):
            v1 = jnp.concatenate([v, jnp.ones((v.shape[0], LANES), BF16)], axis=1)
            pv = jnp.dot(e, v1, preferred_element_type=F32)
            acc = pv if acc is None else acc + pv
        denom = acc[:, HEAD_DIM:HEAD_DIM + 1]
        if sink is not None:
            denom = denom + jnp.exp2(sink - m)
        outs.append(acc[:, :HEAD_DIM] / denom)
    return outs


A_STEP_BLOCKS = 4


def _attn_a_kernel(sink_ref, q_ref, *refs, latent, step_blocks):
    nband = step_blocks + 2
    k_refs, v_refs = refs[:nband], refs[nband:2 * nband]
    kx_ref, vx_ref, u_ref, v_ref, sg_ref, sb_ref, sw_ref, sbs_ref, o_ref, oc_ref, mask_ref = refs[2 * nband:]
    i = pl.program_id(0)
    nb = pl.num_programs(0) * step_blocks
    nq = A_GROUP * A_BLOCK

    if latent:
        @pl.when(i == 0)
        def _():
            qi = lax.broadcasted_iota(jnp.int32, (nq, 3 * A_BLOCK), 0) % A_BLOCK
            jj = lax.broadcasted_iota(jnp.int32, (nq, 3 * A_BLOCK), 1)
            ok = (jj >= qi) & (jj <= qi + 2 * A_BLOCK)
            mask_ref[0] = jnp.where(ok & (jj >= A_BLOCK), 0.0, NEG_INF).astype(F32)
            mask_ref[1] = jnp.where(ok, 0.0, NEG_INF).astype(F32)
            mask_ref[2] = jnp.where(ok & (jj < 2 * A_BLOCK), 0.0, NEG_INF).astype(F32)

    sinks = [jnp.concatenate([jnp.full((A_BLOCK, 1), sink_ref[kh * A_GROUP + g] * LOG2E, F32)
                              for g in range(A_GROUP)], axis=0) for kh in range(A_KV_HEADS)]
    for j in range(step_blocks):
        rows = pl.ds(j * A_BLOCK, A_BLOCK)
        _sgu_chunk(u_ref.at[rows, :], v_ref.at[rows, :], sg_ref, sb_ref, sw_ref, sbs_ref, oc_ref.at[rows, :])
        if latent:
            blk = i * step_blocks + j
            which = jnp.where(blk == 0, 0, jnp.where(blk == nb - 1, 2, 1))
        problems = []
        for kh in range(A_KV_HEADS):
            ks = slice(kh * HEAD_DIM, (kh + 1) * HEAD_DIM)
            q = jnp.concatenate([q_ref[rows, (kh * A_GROUP + g) * HEAD_DIM:(kh * A_GROUP + g + 1) * HEAD_DIM]
                                 for g in range(A_GROUP)], axis=0)
            s_parts = [_dot_nt(q, kx_ref[:, ks]) * LOGIT_SCALE]
            v_parts = [vx_ref[:, ks]]
            if latent:
                kband = jnp.concatenate([r[:, ks] for r in k_refs[j:j + 3]], axis=0)
                vband = jnp.concatenate([r[:, ks] for r in v_refs[j:j + 3]], axis=0)
                s_parts.append(_dot_nt(q, kband) * LOGIT_SCALE + mask_ref[which])
                v_parts.append(vband)
            problems.append((s_parts, v_parts, sinks[kh]))
        for kh, out in enumerate(_softmax_pv(problems)):
            for g in range(A_GROUP):
                h = kh * A_GROUP + g
                o_ref[rows, h * HEAD_DIM:(h + 1) * HEAD_DIM] = out[g * A_BLOCK:(g + 1) * A_BLOCK].astype(BF16)


def _attn_a_sgu(p, pc, sink, sgu_params, latent):
    assert CHUNK == A_BLOCK
    rows = p.shape[0]
    nb = rows // A_BLOCK
    step_blocks = A_STEP_BLOCKS if nb % A_STEP_BLOCKS == 0 else 2
    assert nb >= 2 and nb % step_blocks == 0
    kcol = OFF_AK // A_KV_W
    vcol = OFF_AV // A_KV_W
    ucol = OFF_C // C_W
    tile = step_blocks * A_BLOCK
    nband = step_blocks + 2

    def band(col):
        return [pl.BlockSpec((A_BLOCK, A_KV_W),
                             lambda i, s, sh=shift: (jnp.clip(i * step_blocks + sh, 0, nb - 1), col))
                for shift in range(-1, step_blocks + 1)]

    lctx = pc.shape[0]
    return pl.pallas_call(
        functools.partial(_attn_a_kernel, latent=latent, step_blocks=step_blocks),
        grid_spec=pltpu.PrefetchScalarGridSpec(
            num_scalar_prefetch=1,
            grid=(nb // step_blocks,),
            in_specs=([pl.BlockSpec((tile, A_Q_W), lambda i, s: (i, 0))] + band(kcol) + band(vcol)
                      + [pl.BlockSpec((lctx, A_KV_W), lambda i, s: (0, kcol)),
                         pl.BlockSpec((lctx, A_KV_W), lambda i, s: (0, vcol)),
                         pl.BlockSpec((tile, C_W), lambda i, s: (i, ucol)),
                         pl.BlockSpec((tile, C_W), lambda i, s: (i, ucol + 1)),
                         pl.BlockSpec((1, C_W), lambda i, s: (0, 0)),
                         pl.BlockSpec((1, C_W), lambda i, s: (0, 0)),
                         pl.BlockSpec((C_GROUPS, CHUNK, CHUNK), lambda i, s: (0, 0, 0)),
                         pl.BlockSpec((C_GROUPS, CHUNK, LANES), lambda i, s: (0, 0, 0))]),
            out_specs=[pl.BlockSpec((tile, A_Q_W), lambda i, s: (i, 0)),
                       pl.BlockSpec((tile, C_W), lambda i, s: (i, 0))],
            scratch_shapes=[pltpu.VMEM((3, A_GROUP * A_BLOCK, 3 * A_BLOCK), F32)]),
        out_shape=[jax.ShapeDtypeStruct((rows, A_Q_W), BF16), jax.ShapeDtypeStruct((rows, C_W), BF16)],
        compiler_params=_params("arbitrary"),
        name="attn_a_latent" if latent else "attn_a_ctx",
    )(*([sink] + [p] * (1 + 2 * nband) + [pc, pc, p, p] + list(sgu_params)))


B_PAIR_W = 2 * HEAD_DIM
NA_TILE = 256
NA_STEP_GROUPS = 2
NA_GROUP_ROWS = NA_TILE // GRID_W
NA_WIN_ROWS = NA_ROWS + NA_GROUP_ROWS
NA_PAIRS = NA_WIN_ROWS // 2
NA_BIAS_OFFS = 2 * NA_ROWS
NA_BOTH, NA_LEFT, NA_RIGHT = range(3)


def _attn_b_kernel(*refs, latent, grid_rows, step_groups):
    npair = B_HEADS // 2
    q_refs, k_refs, v_refs, kx_refs, vx_refs = (refs[j * npair:(j + 1) * npair] for j in range(5))
    bias_ref, o_ref = refs[5 * npair], refs[5 * npair + 1]
    g = pl.program_id(0)
    for jg in range(step_groups):
        rows = pl.ds(jg * NA_TILE, NA_TILE) if latent else slice(None)
        if latent:
            r_base = (g * step_groups + jg) * NA_GROUP_ROWS
            w0 = jnp.clip(r_base - NA_ROWS // 2, 0, grid_rows - NA_WIN_ROWS)
            start = pl.multiple_of(w0 * GRID_W, GRID_W)
        for hp in range(npair):
            problems = []
            for hh in range(2):
                hs = slice(hh * HEAD_DIM, (hh + 1) * HEAD_DIM)
                q = q_refs[hp][rows, hs]
                s_parts = [_dot_nt(q, kx_refs[hp][:, hs]) * LOGIT_SCALE]
                v_parts = [vx_refs[hp][:, hs]]
                if latent:
                    kwin = k_refs[hp][pl.ds(start, NA_WIN_ROWS * GRID_W), hs]
                    vwin = v_refs[hp][pl.ds(start, NA_WIN_ROWS * GRID_W), hs]
                    bias_rows = []
                    for rr in range(NA_GROUP_ROWS):
                        r = r_base + rr
                        r0 = jnp.clip(r - NA_ROWS // 2, 0, grid_rows - NA_ROWS)
                        tiles = []
                        for jp in range(NA_PAIRS):
                            ka = w0 + 2 * jp
                            in_a = (ka >= r0) & (ka < r0 + NA_ROWS)
                            in_b = (ka + 1 >= r0) & (ka + 1 < r0 + NA_ROWS)
                            variant = jnp.where(in_a, jnp.where(in_b, NA_BOTH, NA_LEFT),
                                                jnp.where(in_b, NA_RIGHT, NA_LEFT))
                            off = jnp.where(in_a | in_b, jnp.clip(ka - r + NA_ROWS, 0, NA_BIAS_OFFS - 1), 0)
                            tiles.append(bias_ref[2 * hp + hh, variant, off])
                        bias_rows.append(jnp.concatenate(tiles, axis=1))
                    bias = jnp.concatenate(bias_rows, axis=0)
                    s_parts.append(_dot_nt(q, kwin) * LOGIT_SCALE + bias)
                    v_parts.append(vwin)
                problems.append((s_parts, v_parts, None))
            for hh, out in enumerate(_softmax_pv(problems)):
                h = 2 * hp + hh
                o_ref[rows, h * HEAD_DIM:(h + 1) * HEAD_DIM] = out.astype(BF16)


def _na_bias_tables(rpb):
    cols = np.arange(GRID_W)
    c0 = np.clip(cols - NA_COLS // 2, 0, GRID_W - NA_COLS)
    rel = cols[None, :] - cols[:, None] + NA_COLS - 1
    ok = (cols[None, :] >= c0[:, None]) & (cols[None, :] < c0[:, None] + NA_COLS)
    onehot = (rel[None] == np.arange(2 * NA_COLS - 1)[:, None, None]).astype(np.float32)
    t = jnp.einsum("lhrd,dqk->lhrqk", rpb, onehot, precision=lax.Precision.HIGHEST)
    t = jnp.where(ok, t * LOG2E, NEG_INF).astype(F32)
    t = jnp.pad(t, ((0, 0), (0, 0), (1, 1), (0, 0), (0, 0)), constant_values=NEG_INF)
    pairs = jnp.concatenate([t[:, :, :-1], t[:, :, 1:]], axis=-1)
    keep = np.ones((3, 1, 1, 2 * GRID_W), bool)
    keep[NA_LEFT, :, :, GRID_W:] = False
    keep[NA_RIGHT, :, :, :GRID_W] = False
    return jnp.where(keep, pairs[:, :, None], NEG_INF)


def _attn_b(p, pc, bias_tabs, layer, latent):
    rows = p.shape[0]
    lctx = pc.shape[0]
    assert not latent or (rows // GRID_W >= NA_WIN_ROWS and rows % (NA_STEP_GROUPS * NA_TILE) == 0)
    step_groups = NA_STEP_GROUPS if latent else 1
    tile = step_groups * NA_TILE if latent else ROW_TILE
    npair = B_HEADS // 2
    qcol = OFF_BQ // B_PAIR_W
    kcol = OFF_BK // B_PAIR_W
    vcol = OFF_BV // B_PAIR_W

    def resident(nrows, col):
        return [pl.BlockSpec((nrows, B_PAIR_W), lambda g, c=col + hp: (0, c), pipeline_mode=pl.Buffered(1))
                for hp in range(npair)]

    return pl.pallas_call(
        functools.partial(_attn_b_kernel, latent=latent, grid_rows=rows // GRID_W, step_groups=step_groups),
        grid=(rows // tile,),
        in_specs=([pl.BlockSpec((tile, B_PAIR_W), lambda g, c=qcol + hp: (g, c)) for hp in range(npair)]
                  + resident(rows, kcol) + resident(rows, vcol) + resident(lctx, kcol) + resident(lctx, vcol)
                  + [pl.BlockSpec((None, B_HEADS, 3, NA_BIAS_OFFS, GRID_W, 2 * GRID_W),
                                  lambda g: (layer, 0, 0, 0, 0, 0), pipeline_mode=pl.Buffered(1))]),
        out_specs=pl.BlockSpec((tile, B_W), lambda g: (g, 0)),
        out_shape=jax.ShapeDtypeStruct((rows, B_W), BF16),
        compiler_params=_params("arbitrary"),
        name="attn_b_latent" if latent else "attn_b_ctx",
    )(*([p] * (3 * npair) + [pc] * (2 * npair) + [bias_tabs]))


def _sgu_chunk(u_ref, v_ref, g_ref, b_ref, w_ref, bs_ref, o_ref):
    for grp in range(C_GROUPS):
        cs = slice(grp * HEAD_DIM, (grp + 1) * HEAD_DIM)
        vn = _layer_norm(v_ref[:, cs].astype(F32), g_ref[:, cs], b_ref[:, cs])
        mixed = jnp.dot(w_ref[grp], vn.astype(BF16), preferred_element_type=F32) + bs_ref[grp]
        o_ref[:, cs] = (u_ref[:, cs].astype(F32) * mixed).astype(BF16)


def _outproj_kernel(oa_ref, ob_ref, oc_ref, w_ref, x_ref, mod_ref, g_ref, b_ref, *rest):
    x1_ref, h2_ref = rest[-2], rest[-1]
    mix = jnp.dot(oa_ref[...], w_ref[0:A_Q_W, :], preferred_element_type=F32)
    mix += jnp.dot(ob_ref[...], w_ref[A_Q_W:A_Q_W + B_W, :], preferred_element_type=F32)
    mix += jnp.dot(oc_ref[...], w_ref[A_Q_W + B_W:, :], preferred_element_type=F32)
    t = DEEPNORM_ALPHA * x_ref[...] + mod_ref[MOD_G1:MOD_G1 + 1, :] * mix
    x1 = _layer_norm(t, g_ref[...], b_ref[...])
    x1_ref[...] = x1
    h2_ref[...] = x1 * (1.0 + mod_ref[MOD_SC2:MOD_SC2 + 1, :]) + mod_ref[MOD_SH2:MOD_SH2 + 1, :]


def _outproj(o_a, o_b, o_c, w_bf, layer, x, x_row0, mod, ln_g, ln_b, total_rows, out_row0, prev):
    rows = o_a.shape[0]
    d = D_MODEL
    tm = _matmul_row_tile(rows, x_row0, out_row0)
    x_off, out_off = x_row0 // tm, out_row0 // tm
    in_specs = [pl.BlockSpec((tm, A_Q_W), lambda i: (i, 0)),
                pl.BlockSpec((tm, B_W), lambda i: (i, 0)),
                pl.BlockSpec((tm, C_W), lambda i: (i, 0)),
                pl.BlockSpec((None, d, d), lambda i: (layer, 0, 0), pipeline_mode=pl.Buffered(1)),
                pl.BlockSpec((tm, d), lambda i: (i + x_off, 0)),
                pl.BlockSpec((8, d), lambda i: (0, 0)),
                pl.BlockSpec((1, d), lambda i: (0, 0)),
                pl.BlockSpec((1, d), lambda i: (0, 0))]
    args = [o_a, o_b, o_c, w_bf, x, mod, ln_g, ln_b]
    aliases = {}
    if prev is not None:
        in_specs += [pl.BlockSpec(memory_space=pl.ANY), pl.BlockSpec(memory_space=pl.ANY)]
        aliases = {len(args): 0, len(args) + 1: 1}
        args += list(prev)
    return pl.pallas_call(
        _outproj_kernel,
        grid=(rows // tm,),
        in_specs=in_specs,
        out_specs=[pl.BlockSpec((tm, d), lambda i: (i + out_off, 0))] * 2,
        out_shape=[jax.ShapeDtypeStruct((total_rows, d), F32)] * 2,
        input_output_aliases=aliases,
        compiler_params=_params("arbitrary"),
        name="outproj",
    )(*args)


def _top2_sublanes(vals, sub):
    m1 = vals.max(axis=0, keepdims=True)
    i1 = jnp.where(vals == m1, sub, vals.shape[0]).min(axis=0, keepdims=True)
    rest = jnp.where(sub == i1, -jnp.inf, vals)
    m2 = rest.max(axis=0, keepdims=True)
    i2 = jnp.where(rest == m2, sub, vals.shape[0]).min(axis=0, keepdims=True)
    return m1, i1, m2, i2


def _router_kernel(h_ref, wr_ref, rb_ref, e_ref, w_ref, rank_ref, cnt_ref, run_ref):
    i = pl.program_id(0)
    tm = ROW_TILE
    epg = EXPERTS_PER_GROUP

    @pl.when(i == 0)
    def _():
        run_ref[...] = jnp.zeros_like(run_ref)

    h = h_ref[...]
    h_hi = h.astype(BF16)
    h_mid = (h - h_hi.astype(F32)).astype(BF16)
    parts = _dot_nt(wr_ref[...], h_hi)
    parts_mid = _dot_nt(wr_ref[0:2 * N_EXPERTS, :], h_mid)
    logits = (parts[0:N_EXPERTS] + parts[N_EXPERTS:2 * N_EXPERTS] + parts[2 * N_EXPERTS:]
              + parts_mid[0:N_EXPERTS] + parts_mid[N_EXPERTS:])
    scores = _sigmoid(logits)
    biased = scores + jnp.concatenate([rb_ref[...]] * (tm // LANES), axis=1)
    sub = lax.broadcasted_iota(jnp.int32, (epg, tm), 0)

    best = None
    for g in range(N_EXPERT_GROUPS):
        m1, _, m2, _ = _top2_sublanes(biased[g * epg:(g + 1) * epg], sub)
        gs = m1 + m2
        if best is None:
            best, grp = gs, jnp.zeros((1, tm), jnp.int32)
            bsel, ssel = biased[0:epg], scores[0:epg]
        else:
            better = gs > best
            best = jnp.where(better, gs, best)
            grp = jnp.where(better, g, grp)
            bsel = jnp.where(better, biased[g * epg:(g + 1) * epg], bsel)
            ssel = jnp.where(better, scores[g * epg:(g + 1) * epg], ssel)
    _, i1, _, i2 = _top2_sublanes(bsel, sub)
    w1 = jnp.where(sub == i1, ssel, 0.0).sum(axis=0, keepdims=True)
    w2 = jnp.where(sub == i2, ssel, 0.0).sum(axis=0, keepdims=True)
    tot = w1 + w2
    e1 = grp * epg + i1
    e2 = grp * epg + i2

    eiota = lax.broadcasted_iota(jnp.int32, (N_EXPERTS, tm), 0)
    oh1 = (eiota == e1).astype(F32)
    oh2 = (eiota == e2).astype(F32)
    ohb = oh1 + oh2
    before = (lax.broadcasted_iota(jnp.int32, (tm, tm), 0) < lax.broadcasted_iota(jnp.int32, (tm, tm), 1))
    prefix = jnp.dot(ohb.astype(BF16), before.astype(BF16), preferred_element_type=F32)
    pos = run_ref[...] + prefix
    r1 = (oh1 * pos).sum(axis=0, keepdims=True)
    r2 = (oh2 * pos).sum(axis=0, keepdims=True)
    run_ref[...] = run_ref[...] + ohb.sum(axis=1, keepdims=True)

    e_ref[...] = jnp.concatenate([e1, e2], axis=0)
    w_ref[...] = jnp.concatenate([w1 / tot, w2 / tot], axis=0)
    rank_ref[...] = jnp.concatenate([r1, r2], axis=0).astype(jnp.int32)
    cnt_ref[...] = run_ref[:, 0:LANES]


def _router(h2, wr_t, rb_b):
    t = h2.shape[0]
    d = D_MODEL
    row2 = pl.BlockSpec((TOP_K, ROW_TILE), lambda i: (0, i))
    return pl.pallas_call(
        _router_kernel,
        grid=(t // ROW_TILE,),
        in_specs=[pl.BlockSpec((ROW_TILE, d), lambda i: (i, 0)),
                  pl.BlockSpec((3 * N_EXPERTS, d), lambda i: (0, 0)),
                  pl.BlockSpec((N_EXPERTS, LANES), lambda i: (0, 0))],
        out_specs=[row2, row2, row2, pl.BlockSpec((N_EXPERTS, LANES), lambda i: (0, 0))],
        out_shape=[jax.ShapeDtypeStruct((TOP_K, t), jnp.int32),
                   jax.ShapeDtypeStruct((TOP_K, t), F32),
                   jax.ShapeDtypeStruct((TOP_K, t), jnp.int32),
                   jax.ShapeDtypeStruct((N_EXPERTS, LANES), F32)],
        scratch_shapes=[pltpu.VMEM((N_EXPERTS, ROW_TILE), F32)],
        compiler_params=_params("arbitrary"),
        name="router",
    )(h2, wr_t, rb_b)


def _row_copy(src, row, dst, sem):
    return pltpu.make_async_copy(src.at[pl.ds(row, 1), :], dst, sem)


DISPATCH_SLOTS = 3


def _dispatch_kernel(slot_ref, h_hbm, xs_hbm, hbuf, lsem, ssem, *, n_tok):
    i = pl.program_id(0)
    nt = pl.num_programs(0)
    tm = ROW_TILE

    def load(blk):
        row0 = pl.multiple_of(blk * tm, tm)
        return pltpu.make_async_copy(h_hbm.at[pl.ds(row0, tm), :], hbuf.at[blk % DISPATCH_SLOTS],
                                     lsem.at[blk % DISPATCH_SLOTS])

    def wait_rows(par):
        for _ in range(TOP_K):
            pltpu.make_async_copy(hbuf.at[0], xs_hbm.at[pl.ds(0, tm), :], ssem.at[par]).wait()

    @pl.when(i == 0)
    def _():
        load(0).start()

    @pl.when(i + 1 < nt)
    def _():
        load(i + 1).start()

    load(i).wait()
    par = i % 2
    cur = hbuf.at[i % DISPATCH_SLOTS]
    for r in range(tm):
        for k in range(TOP_K):
            dst = xs_hbm.at[pl.ds(slot_ref[k * n_tok + i * tm + r], 1), :]
            pltpu.make_async_copy(cur.at[pl.ds(r, 1), :], dst, ssem.at[par]).start(priority=k)

    @pl.when(i > 0)
    def _():
        wait_rows(1 - par)

    @pl.when(i == nt - 1)
    def _():
        wait_rows(par)


def _dispatch(h2, slots, rows):
    t, d = h2.shape
    return pl.pallas_call(
        functools.partial(_dispatch_kernel, n_tok=t),
        grid_spec=pltpu.PrefetchScalarGridSpec(
            num_scalar_prefetch=1,
            grid=(t // ROW_TILE,),
            in_specs=[pl.BlockSpec(memory_space=pl.ANY)],
            out_specs=pl.BlockSpec(memory_space=pl.ANY),
            scratch_shapes=[pltpu.VMEM((DISPATCH_SLOTS, ROW_TILE, d), F32),
                            pltpu.SemaphoreType.DMA((DISPATCH_SLOTS,)),
                            pltpu.SemaphoreType.DMA((2,))]),
        out_shape=jax.ShapeDtypeStruct((rows, d), F32),
        compiler_params=_params("arbitrary"),
        name="dispatch",
    )(slots, h2)


def _experts_kernel(ts_ref, nu_ref, h_hbm, wg_hbm, wu_hbm, wd_hbm, ys_hbm,
                    xbuf, ybuf, gsem, ysem, wg_f, wu_f, wd_f, wsem, wg_bf, wu_bf, wd_bf, *, layer):
    e = pl.program_id(0)
    ne = pl.num_programs(0)
    n_used = nu_ref[0]
    tm = MOE_TILE
    tsub = tm

    def has_rows(ex):
        return ts_ref[ex + 1] > ts_ref[ex]

    def weight_copies(ex, wslot):
        return [pltpu.make_async_copy(src.at[layer, ex], dst.at[wslot], wsem.at[wslot])
                for src, dst in ((wg_hbm, wg_f), (wu_hbm, wu_f), (wd_hbm, wd_f))]

    def start_weights(ex, wslot):
        for cp in weight_copies(ex, wslot):
            cp.start(priority=1)

    def x_load(tile, slot):
        row0 = pl.multiple_of(tile * tsub, tsub)
        return pltpu.make_async_copy(h_hbm.at[pl.ds(row0, tsub), :], xbuf.at[slot], gsem.at[slot])

    def gather(tile, slot):
        x_load(tile, slot).start()

    def wait_gather(slot):
        x_load(0, slot).wait()

    def y_store(tile, slot):
        row0 = pl.multiple_of(tile * tsub, tsub)
        return pltpu.make_async_copy(ybuf.at[slot], ys_hbm.at[pl.ds(row0, tsub), :], ysem.at[slot])

    wslot = e % 2

    @pl.when((e == 0) & has_rows(0))
    def _():
        start_weights(0, 0)

    @pl.when((e == 0) & (n_used > 0))
    def _():
        gather(0, 0)

    nxt = jnp.minimum(e + 1, ne - 1)

    @pl.when((e + 1 < ne) & has_rows(nxt))
    def _():
        start_weights(nxt, 1 - wslot)

    @pl.when(has_rows(e))
    def _():
        for cp in weight_copies(e, wslot):
            cp.wait()
        wg_bf[...] = wg_f[wslot].astype(BF16)
        wu_bf[...] = wu_f[wslot].astype(BF16)
        wd_bf[...] = wd_f[wslot].astype(BF16)

    def tile_body(g, carry):
        slot = g % 2

        @pl.when(g >= 2)
        def _():
            y_store(g - 2, slot).wait()

        wait_gather(slot)
        x = xbuf[slot].astype(BF16)
        gather(jnp.minimum(g + 1, n_used - 1), 1 - slot)
        gate = jnp.dot(x, wg_bf[...], preferred_element_type=F32)
        up = jnp.dot(x, wu_bf[...], preferred_element_type=F32)
        hid = (gate * _sigmoid(gate)) * up
        ybuf[slot] = jnp.dot(hid.astype(BF16), wd_bf[...], preferred_element_type=F32)
        y_store(g, slot).start(priority=1)
        return carry

    lax.fori_loop(ts_ref[e], ts_ref[e + 1], tile_body, 0)

    @pl.when((e == pl.num_programs(0) - 1) & (n_used > 0))
    def _():
        wait_gather(n_used % 2)
        y_store(n_used - 1, (n_used - 1) % 2).wait()

        @pl.when(n_used >= 2)
        def _():
            y_store(n_used - 2, n_used % 2).wait()


def _experts(xs, tile_start, n_used, w_gate, w_up, w_down, layer):
    d = D_MODEL
    tm = MOE_TILE
    rows = xs.shape[0]

    hbm = pl.BlockSpec(memory_space=pl.ANY)
    return pl.pallas_call(
        functools.partial(_experts_kernel, layer=layer),
        grid_spec=pltpu.PrefetchScalarGridSpec(
            num_scalar_prefetch=2,
            grid=(N_EXPERTS,),
            in_specs=[hbm, hbm, hbm, hbm],
            out_specs=hbm,
            scratch_shapes=[pltpu.VMEM((2, tm, d), F32),
                            pltpu.VMEM((2, tm, d), F32),
                            pltpu.SemaphoreType.DMA((2,)),
                            pltpu.SemaphoreType.DMA((2,)),
                            pltpu.VMEM((2, d, EXPERT_FF), F32),
                            pltpu.VMEM((2, d, EXPERT_FF), F32),
                            pltpu.VMEM((2, EXPERT_FF, d), F32),
                            pltpu.SemaphoreType.DMA((2,)),
                            pltpu.VMEM((d, EXPERT_FF), BF16),
                            pltpu.VMEM((d, EXPERT_FF), BF16),
                            pltpu.VMEM((EXPERT_FF, d), BF16)]),
        out_shape=jax.ShapeDtypeStruct((rows, d), F32),
        compiler_params=_params("arbitrary"),
        name="experts",
    )(tile_start, n_used, xs, w_gate, w_up, w_down)


def _combine_kernel(slot_ref, ys_hbm, w_ref, x_ref, mod_ref, g_ref, b_ref, o_ref, ybuf, sem, *, n_tok):
    i = pl.program_id(0)
    nt = pl.num_programs(0)
    tm = ROW_TILE

    def issue(blk, buf):
        for r in range(tm):
            for k in range(TOP_K):
                row = slot_ref[k * n_tok + blk * tm + r]
                _row_copy(ys_hbm, row, ybuf.at[buf, k, pl.ds(r, 1), :], sem.at[buf]).start(priority=k)

    def wait(buf):
        for k in range(TOP_K):
            pltpu.make_async_copy(ys_hbm.at[pl.ds(0, tm), :], ybuf.at[buf, k], sem.at[buf]).wait()

    @pl.when(i == 0)
    def _():
        issue(0, 0)

    buf = i % 2
    wait(buf)
    issue(jnp.minimum(i + 1, nt - 1), 1 - buf)
    y = w_ref[:, 0:1] * ybuf[buf, 0] + w_ref[:, 1:2] * ybuf[buf, 1]
    t = DEEPNORM_ALPHA * x_ref[...] + mod_ref[MOD_G2:MOD_G2 + 1, :] * y
    o_ref[...] = _layer_norm(t, g_ref[...], b_ref[...])

    @pl.when(i == nt - 1)
    def _():
        wait(1 - buf)


def _combine(ys, slots, w_tok, x1, mods, n_lat_blocks, ln_g, ln_b):
    t = x1.shape[0]
    d = D_MODEL
    return pl.pallas_call(
        functools.partial(_combine_kernel, n_tok=t),
        grid_spec=pltpu.PrefetchScalarGridSpec(
            num_scalar_prefetch=1,
            grid=(t // ROW_TILE,),
            in_specs=[pl.BlockSpec(memory_space=pl.ANY),
                      pl.BlockSpec((ROW_TILE, TOP_K), lambda i, s: (i, 0)),
                      pl.BlockSpec((ROW_TILE, d), lambda i, s: (i, 0)),
                      pl.BlockSpec((None, 8, d), lambda i, s: (jnp.where(i < n_lat_blocks, 1, 0), 0, 0)),
                      pl.BlockSpec((1, d), lambda i, s: (0, 0)),
                      pl.BlockSpec((1, d), lambda i, s: (0, 0))],
            out_specs=pl.BlockSpec((ROW_TILE, d), lambda i, s: (i, 0)),
            scratch_shapes=[pltpu.VMEM((2, TOP_K, ROW_TILE, d), F32),
                            pltpu.SemaphoreType.DMA((2,))]),
        out_shape=jax.ShapeDtypeStruct((t, d), F32),
        compiler_params=_params("arbitrary"),
        name="combine",
    )(slots, ys, w_tok, x1, mods, ln_g, ln_b)


def _moe(h2, x1, mods, n_lat_blocks, wr_t, rb_b, w_gate, w_up, w_down, layer, ln_g, ln_b):
    t = x1.shape[0]
    tm = MOE_TILE
    e_idx, w_tok, rank, cnt = _router(h2, wr_t, rb_b)
    counts = cnt[:, 0].astype(jnp.int32)
    tiles_per = (counts + tm - 1) // tm
    tile_end = jnp.cumsum(tiles_per)
    n_used = tile_end[-1]
    row_off = (tile_end - tiles_per) * tm
    experts = jnp.arange(N_EXPERTS, dtype=jnp.int32)
    slots = jnp.sum(jnp.where(e_idx[:, :, None] == experts, row_off, 0), axis=-1) + rank
    max_tiles = (TOP_K * t + N_EXPERTS * (tm - 1)) // tm + 1
    tile_start = jnp.concatenate([tile_end - tiles_per, n_used.reshape(1)]).astype(jnp.int32)
    slots = slots.reshape(-1)
    xs = _dispatch(h2, slots, max_tiles * tm)
    ys = _experts(xs, tile_start, n_used.reshape(1).astype(jnp.int32), w_gate, w_up, w_down, layer)
    return _combine(ys, slots, w_tok.T, x1, mods, n_lat_blocks, ln_g, ln_b)


def _rope_tables(n):
    t = np.arange(n)
    row = (t // GRID_W).astype(np.float64)
    col = (t % GRID_W).astype(np.float64)
    n_freq = HEAD_DIM // 4
    inv_freq = ROPE_BASE ** (-np.arange(n_freq, dtype=np.float64) / n_freq)
    ang = np.concatenate([row[:, None] * inv_freq, col[:, None] * inv_freq], axis=-1)
    cos, sin = np.cos(ang), np.sin(ang)
    return (np.concatenate([cos, cos], axis=-1).astype(np.float32),
            np.concatenate([-sin, sin], axis=-1).astype(np.float32))


def kernel(x, c, ctx, c_ctx, w_mod, b_mod, w_in, attn_sink, na_rpb, sgu_ln_g, sgu_ln_b, sgu_w, sgu_b,
           w_out, ln1_g, ln1_b, w_router, router_bias, w_gate, w_up, w_down, ln2_g, ln2_b):
    batch, n, d = x.shape
    lctx = ctx.shape[1]
    assert batch == 1 and d == D_MODEL and n % ROW_TILE == 0 and lctx % ROW_TILE == 0
    n_lat_blocks = n // ROW_TILE

    mods = _modulation(c, c_ctx, w_mod, b_mod).reshape(DEPTH, 8, 6, d)
    mod_lat = jnp.pad(mods[:, 0], ((0, 0), (0, 2), (0, 0)))
    mod_ctx = jnp.pad(mods[:, 1], ((0, 0), (0, 2), (0, 0)))
    cos, sin = _rope_tables(n)
    wr_f = w_router.T
    def bf16_part(v):
        return lax.bitcast_convert_type(lax.bitcast_convert_type(v, jnp.int32) & jnp.int32(-65536), F32)

    wr_hi = bf16_part(wr_f)
    wr_mid = bf16_part(wr_f - wr_hi)
    wr_lo = wr_f - wr_hi - wr_mid
    wr_t = jnp.concatenate([wr_hi, wr_mid, wr_lo], axis=0).astype(BF16)
    rb_b = jnp.broadcast_to(router_bias.reshape(N_EXPERTS, 1), (N_EXPERTS, LANES))

    x_lat = x[0]
    x_ctx, ctx_row0 = ctx[0], 0
    w_in_bf = w_in.astype(BF16)
    w_out_bf = w_out.astype(BF16)
    bias_tabs = _na_bias_tables(na_rpb)
    for l in range(DEPTH):
        last = l == DEPTH - 1
        sgu_params = (sgu_ln_g[l].reshape(1, C_W), sgu_ln_b[l].reshape(1, C_W), sgu_w[l].astype(BF16),
                      jnp.broadcast_to(sgu_b[l][:, :, None], (C_GROUPS, CHUNK, LANES)))
        g1, b1 = ln1_g[l].reshape(1, d), ln1_b[l].reshape(1, d)
        g2, b2 = ln2_g[l].reshape(1, d), ln2_b[l].reshape(1, d)

        p = _proj(x_lat, 0, n, mod_lat[l], cos, sin, w_in_bf, l, rope=True)
        pc = _proj(x_ctx, ctx_row0, lctx, mod_ctx[l], cos, sin, w_in_bf, l, rope=False)
        o_a, o_c = _attn_a_sgu(p, pc, attn_sink[l], sgu_params, latent=True)
        o_b = _attn_b(p, pc, bias_tabs, l, latent=True)
        mod_both = jnp.stack([mod_ctx[l], mod_lat[l]])
        if last:
            x1, h2 = _outproj(o_a, o_b, o_c, w_out_bf, l, x_lat, 0, mod_lat[l], g1, b1, n, 0, None)
            x_lat = _moe(h2, x1, mod_both, n_lat_blocks, wr_t, rb_b, w_gate, w_up, w_down, l, g2, b2)
        else:
            oc_a, oc_c = _attn_a_sgu(pc, pc, attn_sink[l], sgu_params, latent=False)
            oc_b = _attn_b(pc, pc, bias_tabs, l, latent=False)
            total = n + lctx
            prev = _outproj(o_a, o_b, o_c, w_out_bf, l, x_lat, 0, mod_lat[l], g1, b1, total, 0, None)
            x1, h2 = _outproj(oc_a, oc_b, oc_c, w_out_bf, l, x_ctx, ctx_row0, mod_ctx[l], g1, b1, total, n,
                              prev)
            x_all = _moe(h2, x1, mod_both, n_lat_blocks, wr_t, rb_b, w_gate, w_up, w_down, l, g2, b2)
            x_lat, x_ctx, ctx_row0 = x_all, x_all, n
    return x_lat.reshape(batch, n, d)
```

```python
import functools

import numpy as np
import jax
import jax.numpy as jnp
from jax import lax
from jax.experimental import pallas as pl
from jax.experimental.pallas import tpu as pltpu

F32 = jnp.float32
BF16 = jnp.bfloat16

D_MODEL = 2048
DEPTH = 2
GRID_W = 64
HEAD_DIM = 128
A_HEADS = 6
A_KV_HEADS = 2
A_GROUP = A_HEADS // A_KV_HEADS
A_BLOCK = 128
B_HEADS = 6
NA_ROWS = 8
NA_COLS = 16
C_GROUPS = 4
C_W = C_GROUPS * HEAD_DIM
CHUNK = 128
N_EXPERTS = 32
N_EXPERT_GROUPS = 4
EXPERTS_PER_GROUP = N_EXPERTS // N_EXPERT_GROUPS
TOP_K = 2
EXPERT_FF = 512
ROPE_BASE = 10000.0
LN_EPS = 1e-5
NEG_INF = -1e30
DEEPNORM_ALPHA = (2 * DEPTH) ** 0.25
ATTN_SCALE = HEAD_DIM ** -0.5

A_Q_W = A_HEADS * HEAD_DIM
A_KV_W = A_KV_HEADS * HEAD_DIM
B_W = B_HEADS * HEAD_DIM
OFF_AK = A_Q_W
OFF_AV = OFF_AK + A_KV_W
OFF_BQ = OFF_AV + A_KV_W
OFF_BK = OFF_BQ + B_W
OFF_BV = OFF_BK + B_W
OFF_C = OFF_BV + B_W
IN_COLS = OFF_C + 2 * C_W

VMEM_LIMIT_BYTES = 56 * 1024 * 1024
LANES = 128

ROW_TILE = 256
MATMUL_ROW_TILE = 512
PROJ_COL_TILE = 512
MOD_COL_TILE = 1024
MOE_TILE = 256

MOD_SH1, MOD_SC1, MOD_G1, MOD_SH2, MOD_SC2, MOD_G2 = range(6)


def _params(*sem):
    return pltpu.CompilerParams(dimension_semantics=sem, vmem_limit_bytes=VMEM_LIMIT_BYTES)


def _layer_norm(t, g, b):
    mu = jnp.mean(t, axis=-1, keepdims=True)
    d = t - mu
    var = jnp.mean(d * d, axis=-1, keepdims=True)
    return d * lax.rsqrt(var + LN_EPS) * g + b


def _sigmoid(v):
    return 1.0 / (1.0 + jnp.exp(-v))


def _dot_nt(a, b):
    return lax.dot_general(a, b, (((1,), (1,)), ((), ())), preferred_element_type=F32)


def _mod_kernel(c_ref, w_ref, b_ref, o_ref):
    w = w_ref[...]
    reps = w.shape[1] // LANES
    rows = []
    for r in range(2):
        cv = c_ref[r]
        s = cv * _sigmoid(cv)
        sb = jnp.concatenate([s] * reps, axis=1)
        rows.append(jnp.sum(w * sb, axis=0, keepdims=True) + b_ref[...])
    rows.append(jnp.zeros((6, w.shape[1]), F32))
    o_ref[...] = jnp.concatenate(rows, axis=0)


def _modulation(c, c_ctx, w_mod, b_mod):
    d = D_MODEL
    cb = jnp.stack([jnp.broadcast_to(c.reshape(d, 1), (d, LANES)),
                    jnp.broadcast_to(c_ctx.reshape(d, 1), (d, LANES))])
    n_out = 6 * d
    return pl.pallas_call(
        _mod_kernel,
        grid=(DEPTH, n_out // MOD_COL_TILE),
        in_specs=[pl.BlockSpec((2, d, LANES), lambda l, j: (0, 0, 0)),
                  pl.BlockSpec((None, d, MOD_COL_TILE), lambda l, j: (l, 0, j)),
                  pl.BlockSpec((None, 1, MOD_COL_TILE), lambda l, j: (l, 0, j))],
        out_specs=pl.BlockSpec((None, 8, MOD_COL_TILE), lambda l, j: (l, 0, j)),
        out_shape=jax.ShapeDtypeStruct((DEPTH, 8, n_out), F32),
        compiler_params=_params("arbitrary", "arbitrary"),
        name="modulation",
    )(cb, w_mod, b_mod.reshape(DEPTH, 1, n_out))


def _gelu_tanh(v):
    return 0.5 * v * (1.0 + jnp.tanh(np.sqrt(2.0 / np.pi).astype(np.float32) * (v + 0.044715 * (v * v * v))))


def _proj_kernel(x_ref, mod_ref, cos_ref, sin_ref, w_ref, o_ref, *, rope):
    x = x_ref[...]
    h = (x * (1.0 + mod_ref[MOD_SC1:MOD_SC1 + 1, :]) + mod_ref[MOD_SH1:MOD_SH1 + 1, :]).astype(BF16)
    tn = PROJ_COL_TILE
    for j in range(IN_COLS // tn):
        c0 = j * tn
        acc = jnp.dot(h, w_ref[:, c0:c0 + tn], preferred_element_type=F32)
        if c0 < OFF_AV:
            if rope:
                cos = cos_ref[...]
                sin = sin_ref[...]
                parts = []
                for hh in range(tn // HEAD_DIM):
                    a = acc[:, hh * HEAD_DIM:(hh + 1) * HEAD_DIM]
                    parts.append(a * cos + pltpu.roll(a, HEAD_DIM // 2, 1) * sin)
                acc = jnp.concatenate(parts, axis=1)
        elif c0 >= OFF_C:
            acc = _gelu_tanh(acc)
        o_ref[:, c0:c0 + tn] = acc.astype(BF16)


def _matmul_row_tile(*row_counts):
    return MATMUL_ROW_TILE if all(r % MATMUL_ROW_TILE == 0 for r in row_counts) else ROW_TILE


def _proj(x, x_row0, rows, mod, cos, sin, w_bf, layer, rope):
    d = D_MODEL
    tm = _matmul_row_tile(rows, x_row0)
    x_off = x_row0 // tm
    return pl.pallas_call(
        functools.partial(_proj_kernel, rope=rope),
        grid=(rows // tm,),
        in_specs=[pl.BlockSpec((tm, d), lambda i: (i + x_off, 0)),
                  pl.BlockSpec((8, d), lambda i: (0, 0)),
                  pl.BlockSpec((tm, HEAD_DIM), lambda i: (i, 0)),
                  pl.BlockSpec((tm, HEAD_DIM), lambda i: (i, 0)),
                  pl.BlockSpec((None, d, IN_COLS), lambda i: (layer, 0, 0), pipeline_mode=pl.Buffered(1))],
        out_specs=pl.BlockSpec((tm, IN_COLS), lambda i: (i, 0)),
        out_shape=jax.ShapeDtypeStruct((rows, IN_COLS), BF16),
        compiler_params=_params("arbitrary"),
        name="proj_rope" if rope else "proj_ctx",
    )(x, mod, cos, sin, w_bf)


LOG2E = 1.4426950408889634
LOGIT_SCALE = ATTN_SCALE * LOG2E


def _softmax_pv(problems):
    ms = []
    for s_parts, _, sink in problems:
        m = s_parts[0].max(axis=-1, keepdims=True)
        for s in s_parts[1:]:
            m = jnp.maximum(m, s.max(axis=-1, keepdims=True))
        ms.append(m if sink is None else jnp.maximum(m, sink))
    es = [[jnp.exp2(s - m).astype(BF16) for s in s_parts] for (s_parts, _, _), m in zip(problems, ms)]
    outs = []
    for (_, v_parts, sink), m, e_parts in zip(problems, ms, es):
        acc = None
        for e, v in zip(e_parts, v_parts):
            v1 = jnp.concatenate([v, jnp.ones((v.shape[0], LANES), BF16)], axis=1)
            pv = jnp.dot(e, v1, preferred_element_type=F32)
            acc = pv if acc is None else acc + pv
        denom = acc[:, HEAD_DIM:HEAD_DIM + 1]
        if sink is not None:
            denom = denom + jnp.exp2(sink - m)
        outs.append(acc[:, :HEAD_DIM] / denom)
    return outs


A_STEP_BLOCKS = 4


def _attn_a_kernel(sink_ref, q_ref, *refs, latent, step_blocks):
    nband = step_blocks + 2
    k_refs, v_refs = refs[:nband], refs[nband:2 * nband]
    kx_ref, vx_ref, u_ref, v_ref, sg_ref, sb_ref, sw_ref, sbs_ref, o_ref, oc_ref, mask_ref = refs[2 * nband:]
    i = pl.program_id(0)
    nb = pl.num_programs(0) * step_blocks
    nq = A_GROUP * A_BLOCK

    if latent:
        @pl.when(i == 0)
        def _():
            qi = lax.broadcasted_iota(jnp.int32, (nq, 3 * A_BLOCK), 0) % A_BLOCK
            jj = lax.broadcasted_iota(jnp.int32, (nq, 3 * A_BLOCK), 1)
            ok = (jj >= qi) & (jj <= qi + 2 * A_BLOCK)
            mask_ref[0] = jnp.where(ok & (jj >= A_BLOCK), 0.0, NEG_INF).astype(F32)
            mask_ref[1] = jnp.where(ok, 0.0, NEG_INF).astype(F32)
            mask_ref[2] = jnp.where(ok & (jj < 2 * A_BLOCK), 0.0, NEG_INF).astype(F32)

    sinks = [jnp.concatenate([jnp.full((A_BLOCK, 1), sink_ref[kh * A_GROUP + g] * LOG2E, F32)
                              for g in range(A_GROUP)], axis=0) for kh in range(A_KV_HEADS)]
    for j in range(step_blocks):
        rows = pl.ds(j * A_BLOCK, A_BLOCK)
        _sgu_chunk(u_ref.at[rows, :], v_ref.at[rows, :], sg_ref, sb_ref, sw_ref, sbs_ref, oc_ref.at[rows, :])
        if latent:
            blk = i * step_blocks + j
            which = jnp.where(blk == 0, 0, jnp.where(blk == nb - 1, 2, 1))
        problems = []
        for kh in range(A_KV_HEADS):
            ks = slice(kh * HEAD_DIM, (kh + 1) * HEAD_DIM)
            q = jnp.concatenate([q_ref[rows, (kh * A_GROUP + g) * HEAD_DIM:(kh * A_GROUP + g + 1) * HEAD_DIM]
                                 for g in range(A_GROUP)], axis=0)
            s_parts = [_dot_nt(q, kx_ref[:, ks]) * LOGIT_SCALE]
            v_parts = [vx_ref[:, ks]]
            if latent:
                kband = jnp.concatenate([r[:, ks] for r in k_refs[j:j + 3]], axis=0)
                vband = jnp.concatenate([r[:, ks] for r in v_refs[j:j + 3]], axis=0)
                s_parts.append(_dot_nt(q, kband) * LOGIT_SCALE + mask_ref[which])
                v_parts.append(vband)
            problems.append((s_parts, v_parts, sinks[kh]))
        for kh, out in enumerate(_softmax_pv(problems)):
            for g in range(A_GROUP):
                h = kh * A_GROUP + g
                o_ref[rows, h * HEAD_DIM:(h + 1) * HEAD_DIM] = out[g * A_BLOCK:(g + 1) * A_BLOCK].astype(BF16)


def _attn_a_sgu(p, pc, sink, sgu_params, latent):
    assert CHUNK == A_BLOCK
    rows = p.shape[0]
    nb = rows // A_BLOCK
    step_blocks = A_STEP_BLOCKS if nb % A_STEP_BLOCKS == 0 else 2
    assert nb >= 2 and nb % step_blocks == 0
    kcol = OFF_AK // A_KV_W
    vcol = OFF_AV // A_KV_W
    ucol = OFF_C // C_W
    tile = step_blocks * A_BLOCK
    nband = step_blocks + 2

    def band(col):
        return [pl.BlockSpec((A_BLOCK, A_KV_W),
                             lambda i, s, sh=shift: (jnp.clip(i * step_blocks + sh, 0, nb - 1), col))
                for shift in range(-1, step_blocks + 1)]

    lctx = pc.shape[0]
    return pl.pallas_call(
        functools.partial(_attn_a_kernel, latent=latent, step_blocks=step_blocks),
        grid_spec=pltpu.PrefetchScalarGridSpec(
            num_scalar_prefetch=1,
            grid=(nb // step_blocks,),
            in_specs=([pl.BlockSpec((tile, A_Q_W), lambda i, s: (i, 0))] + band(kcol) + band(vcol)
                      + [pl.BlockSpec((lctx, A_KV_W), lambda i, s: (0, kcol)),
                         pl.BlockSpec((lctx, A_KV_W), lambda i, s: (0, vcol)),
                         pl.BlockSpec((tile, C_W), lambda i, s: (i, ucol)),
                         pl.BlockSpec((tile, C_W), lambda i, s: (i, ucol + 1)),
                         pl.BlockSpec((1, C_W), lambda i, s: (0, 0)),
                         pl.BlockSpec((1, C_W), lambda i, s: (0, 0)),
                         pl.BlockSpec((C_GROUPS, CHUNK, CHUNK), lambda i, s: (0, 0, 0)),
                         pl.BlockSpec((C_GROUPS, CHUNK, LANES), lambda i, s: (0, 0, 0))]),
            out_specs=[pl.BlockSpec((tile, A_Q_W), lambda i, s: (i, 0)),
                       pl.BlockSpec((tile, C_W), lambda i, s: (i, 0))],
            scratch_shapes=[pltpu.VMEM((3, A_GROUP * A_BLOCK, 3 * A_BLOCK), F32)]),
        out_shape=[jax.ShapeDtypeStruct((rows, A_Q_W), BF16), jax.ShapeDtypeStruct((rows, C_W), BF16)],
        compiler_params=_params("arbitrary"),
        name="attn_a_latent" if latent else "attn_a_ctx",
    )(*([sink] + [p] * (1 + 2 * nband) + [pc, pc, p, p] + list(sgu_params)))


B_PAIR_W = 2 * HEAD_DIM
NA_TILE = 256
NA_STEP_GROUPS = 4
NA_GROUP_ROWS = NA_TILE // GRID_W
NA_WIN_ROWS = NA_ROWS + NA_GROUP_ROWS
NA_PAIRS = NA_WIN_ROWS // 2
NA_BIAS_OFFS = 2 * NA_ROWS
NA_BOTH, NA_LEFT, NA_RIGHT = range(3)


def _attn_b_kernel(*refs, latent, grid_rows, step_groups):
    npair = B_HEADS // 2
    q_refs, k_refs, v_refs, kx_refs, vx_refs = (refs[j * npair:(j + 1) * npair] for j in range(5))
    bias_ref, o_ref = refs[5 * npair], refs[5 * npair + 1]
    g = pl.program_id(0)
    for jg in range(step_groups):
        rows = pl.ds(jg * NA_TILE, NA_TILE) if latent else slice(None)
        if latent:
            r_base = (g * step_groups + jg) * NA_GROUP_ROWS
            w0 = jnp.clip(r_base - NA_ROWS // 2, 0, grid_rows - NA_WIN_ROWS)
            start = pl.multiple_of(w0 * GRID_W, GRID_W)
        for hp in range(npair):
            problems = []
            for hh in range(2):
                hs = slice(hh * HEAD_DIM, (hh + 1) * HEAD_DIM)
                q = q_refs[hp][rows, hs]
                s_parts = [_dot_nt(q, kx_refs[hp][:, hs]) * LOGIT_SCALE]
                v_parts = [vx_refs[hp][:, hs]]
                if latent:
                    kwin = k_refs[hp][pl.ds(start, NA_WIN_ROWS * GRID_W), hs]
                    vwin = v_refs[hp][pl.ds(start, NA_WIN_ROWS * GRID_W), hs]
                    bias_rows = []
                    for rr in range(NA_GROUP_ROWS):
                        r = r_base + rr
                        r0 = jnp.clip(r - NA_ROWS // 2, 0, grid_rows - NA_ROWS)
                        tiles = []
                        for jp in range(NA_PAIRS):
                            ka = w0 + 2 * jp
                            in_a = (ka >= r0) & (ka < r0 + NA_ROWS)
                            in_b = (ka + 1 >= r0) & (ka + 1 < r0 + NA_ROWS)
                            variant = jnp.where(in_a, jnp.where(in_b, NA_BOTH, NA_LEFT),
                                                jnp.where(in_b, NA_RIGHT, NA_LEFT))
                            off = jnp.where(in_a | in_b, jnp.clip(ka - r + NA_ROWS, 0, NA_BIAS_OFFS - 1), 0)
                            tiles.append(bias_ref[2 * hp + hh, variant, off])
                        bias_rows.append(jnp.concatenate(tiles, axis=1))
                    bias = jnp.concatenate(bias_rows, axis=0)
                    s_parts.append(_dot_nt(q, kwin) * LOGIT_SCALE + bias)
                    v_parts.append(vwin)
                problems.append((s_parts, v_parts, None))
            for hh, out in enumerate(_softmax_pv(problems)):
                h = 2 * hp + hh
                o_ref[rows, h * HEAD_DIM:(h + 1) * HEAD_DIM] = out.astype(BF16)


def _na_bias_tables(rpb):
    cols = np.arange(GRID_W)
    c0 = np.clip(cols - NA_COLS // 2, 0, GRID_W - NA_COLS)
    rel = cols[None, :] - cols[:, None] + NA_COLS - 1
    ok = (cols[None, :] >= c0[:, None]) & (cols[None, :] < c0[:, None] + NA_COLS)
    onehot = (rel[None] == np.arange(2 * NA_COLS - 1)[:, None, None]).astype(np.float32)
    t = jnp.einsum("lhrd,dqk->lhrqk", rpb, onehot, precision=lax.Precision.HIGHEST)
    t = jnp.where(ok, t * LOG2E, NEG_INF).astype(F32)
    t = jnp.pad(t, ((0, 0), (0, 0), (1, 1), (0, 0), (0, 0)), constant_values=NEG_INF)
    pairs = jnp.concatenate([t[:, :, :-1], t[:, :, 1:]], axis=-1)
    keep = np.ones((3, 1, 1, 2 * GRID_W), bool)
    keep[NA_LEFT, :, :, GRID_W:] = False
    keep[NA_RIGHT, :, :, :GRID_W] = False
    return jnp.where(keep, pairs[:, :, None], NEG_INF)


def _attn_b(p, pc, bias_tabs, layer, latent):
    rows = p.shape[0]
    lctx = pc.shape[0]
    assert not latent or (rows // GRID_W >= NA_WIN_ROWS and rows % (NA_STEP_GROUPS * NA_TILE) == 0)
    step_groups = NA_STEP_GROUPS if latent else 1
    tile = step_groups * NA_TILE if latent else ROW_TILE
    npair = B_HEADS // 2
    qcol = OFF_BQ // B_PAIR_W
    kcol = OFF_BK // B_PAIR_W
    vcol = OFF_BV // B_PAIR_W

    def resident(nrows, col):
        return [pl.BlockSpec((nrows, B_PAIR_W), lambda g, c=col + hp: (0, c), pipeline_mode=pl.Buffered(1))
                for hp in range(npair)]

    return pl.pallas_call(
        functools.partial(_attn_b_kernel, latent=latent, grid_rows=rows // GRID_W, step_groups=step_groups),
        grid=(rows // tile,),
        in_specs=([pl.BlockSpec((tile, B_PAIR_W), lambda g, c=qcol + hp: (g, c)) for hp in range(npair)]
                  + resident(rows, kcol) + resident(rows, vcol) + resident(lctx, kcol) + resident(lctx, vcol)
                  + [pl.BlockSpec((None, B_HEADS, 3, NA_BIAS_OFFS, GRID_W, 2 * GRID_W),
                                  lambda g: (layer, 0, 0, 0, 0, 0), pipeline_mode=pl.Buffered(1))]),
        out_specs=pl.BlockSpec((tile, B_W), lambda g: (g, 0)),
        out_shape=jax.ShapeDtypeStruct((rows, B_W), BF16),
        compiler_params=_params("arbitrary"),
        name="attn_b_latent" if latent else "attn_b_ctx",
    )(*([p] * (3 * npair) + [pc] * (2 * npair) + [bias_tabs]))


def _sgu_chunk(u_ref, v_ref, g_ref, b_ref, w_ref, bs_ref, o_ref):
    for grp in range(C_GROUPS):
        cs = slice(grp * HEAD_DIM, (grp + 1) * HEAD_DIM)
        vn = _layer_norm(v_ref[:, cs].astype(F32), g_ref[:, cs], b_ref[:, cs])
        mixed = jnp.dot(w_ref[grp], vn.astype(BF16), preferred_element_type=F32) + bs_ref[grp]
        o_ref[:, cs] = (u_ref[:, cs].astype(F32) * mixed).astype(BF16)


def _outproj_kernel(oa_ref, ob_ref, oc_ref, w_ref, x_ref, mod_ref, g_ref, b_ref, *rest):
    x1_ref, h2_ref = rest[-2], rest[-1]
    mix = jnp.dot(oa_ref[...], w_ref[0:A_Q_W, :], preferred_element_type=F32)
    mix += jnp.dot(ob_ref[...], w_ref[A_Q_W:A_Q_W + B_W, :], preferred_element_type=F32)
    mix += jnp.dot(oc_ref[...], w_ref[A_Q_W + B_W:, :], preferred_element_type=F32)
    t = DEEPNORM_ALPHA * x_ref[...] + mod_ref[MOD_G1:MOD_G1 + 1, :] * mix
    x1 = _layer_norm(t, g_ref[...], b_ref[...])
    x1_ref[...] = x1
    h2_ref[...] = x1 * (1.0 + mod_ref[MOD_SC2:MOD_SC2 + 1, :]) + mod_ref[MOD_SH2:MOD_SH2 + 1, :]


def _outproj(o_a, o_b, o_c, w_bf, layer, x, x_row0, mod, ln_g, ln_b, total_rows, out_row0, prev):
    rows = o_a.shape[0]
    d = D_MODEL
    tm = _matmul_row_tile(rows, x_row0, out_row0)
    x_off, out_off = x_row0 // tm, out_row0 // tm
    in_specs = [pl.BlockSpec((tm, A_Q_W), lambda i: (i, 0)),
                pl.BlockSpec((tm, B_W), lambda i: (i, 0)),
                pl.BlockSpec((tm, C_W), lambda i: (i, 0)),
                pl.BlockSpec((None, d, d), lambda i: (layer, 0, 0), pipeline_mode=pl.Buffered(1)),
                pl.BlockSpec((tm, d), lambda i: (i + x_off, 0)),
                pl.BlockSpec((8, d), lambda i: (0, 0)),
                pl.BlockSpec((1, d), lambda i: (0, 0)),
                pl.BlockSpec((1, d), lambda i: (0, 0))]
    args = [o_a, o_b, o_c, w_bf, x, mod, ln_g, ln_b]
    aliases = {}
    if prev is not None:
        in_specs += [pl.BlockSpec(memory_space=pl.ANY), pl.BlockSpec(memory_space=pl.ANY)]
        aliases = {len(args): 0, len(args) + 1: 1}
        args += list(prev)
    return pl.pallas_call(
        _outproj_kernel,
        grid=(rows // tm,),
        in_specs=in_specs,
        out_specs=[pl.BlockSpec((tm, d), lambda i: (i + out_off, 0))] * 2,
        out_shape=[jax.ShapeDtypeStruct((total_rows, d), F32)] * 2,
        input_output_aliases=aliases,
        compiler_params=_params("arbitrary"),
        name="outproj",
    )(*args)


def _top2_sublanes(vals, sub):
    m1 = vals.max(axis=0, keepdims=True)
    i1 = jnp.where(vals == m1, sub, vals.shape[0]).min(axis=0, keepdims=True)
    rest = jnp.where(sub == i1, -jnp.inf, vals)
    m2 = rest.max(axis=0, keepdims=True)
    i2 = jnp.where(rest == m2, sub, vals.shape[0]).min(axis=0, keepdims=True)
    return m1, i1, m2, i2


def _router_kernel(h_ref, wr_ref, rb_ref, e_ref, w_ref, rank_ref, cnt_ref, run_ref):
    i = pl.program_id(0)
    tm = ROW_TILE
    epg = EXPERTS_PER_GROUP

    @pl.when(i == 0)
    def _():
        run_ref[...] = jnp.zeros_like(run_ref)

    h = h_ref[...]
    h_hi = h.astype(BF16)
    h_mid = (h - h_hi.astype(F32)).astype(BF16)
    parts = _dot_nt(wr_ref[...], h_hi)
    parts_mid = _dot_nt(wr_ref[0:2 * N_EXPERTS, :], h_mid)
    logits = (parts[0:N_EXPERTS] + parts[N_EXPERTS:2 * N_EXPERTS] + parts[2 * N_EXPERTS:]
              + parts_mid[0:N_EXPERTS] + parts_mid[N_EXPERTS:])
    scores = _sigmoid(logits)
    biased = scores + jnp.concatenate([rb_ref[...]] * (tm // LANES), axis=1)
    sub = lax.broadcasted_iota(jnp.int32, (epg, tm), 0)

    best = None
    for g in range(N_EXPERT_GROUPS):
        m1, _, m2, _ = _top2_sublanes(biased[g * epg:(g + 1) * epg], sub)
        gs = m1 + m2
        if best is None:
            best, grp = gs, jnp.zeros((1, tm), jnp.int32)
            bsel, ssel = biased[0:epg], scores[0:epg]
        else:
            better = gs > best
            best = jnp.where(better, gs, best)
            grp = jnp.where(better, g, grp)
            bsel = jnp.where(better, biased[g * epg:(g + 1) * epg], bsel)
            ssel = jnp.where(better, scores[g * epg:(g + 1) * epg], ssel)
    _, i1, _, i2 = _top2_sublanes(bsel, sub)
    w1 = jnp.where(sub == i1, ssel, 0.0).sum(axis=0, keepdims=True)
    w2 = jnp.where(sub == i2, ssel, 0.0).sum(axis=0, keepdims=True)
    tot = w1 + w2
    e1 = grp * epg + i1
    e2 = grp * epg + i2

    eiota = lax.broadcasted_iota(jnp.int32, (N_EXPERTS, tm), 0)
    oh1 = (eiota == e1).astype(F32)
    oh2 = (eiota == e2).astype(F32)
    ohb = oh1 + oh2
    before = (lax.broadcasted_iota(jnp.int32, (tm, tm), 0) < lax.broadcasted_iota(jnp.int32, (tm, tm), 1))
    prefix = jnp.dot(ohb.astype(BF16), before.astype(BF16), preferred_element_type=F32)
    pos = run_ref[...] + prefix
    r1 = (oh1 * pos).sum(axis=0, keepdims=True)
    r2 = (oh2 * pos).sum(axis=0, keepdims=True)
    run_ref[...] = run_ref[...] + ohb.sum(axis=1, keepdims=True)

    e_ref[...] = jnp.concatenate([e1, e2], axis=0)
    w_ref[...] = jnp.concatenate([w1 / tot, w2 / tot], axis=0)
    rank_ref[...] = jnp.concatenate([r1, r2], axis=0).astype(jnp.int32)
    cnt_ref[...] = run_ref[:, 0:LANES]


def _router(h2, wr_t, rb_b):
    t = h2.shape[0]
    d = D_MODEL
    row2 = pl.BlockSpec((TOP_K, ROW_TILE), lambda i: (0, i))
    return pl.pallas_call(
        _router_kernel,
        grid=(t // ROW_TILE,),
        in_specs=[pl.BlockSpec((ROW_TILE, d), lambda i: (i, 0)),
                  pl.BlockSpec((3 * N_EXPERTS, d), lambda i: (0, 0)),
                  pl.BlockSpec((N_EXPERTS, LANES), lambda i: (0, 0))],
        out_specs=[row2, row2, row2, pl.BlockSpec((N_EXPERTS, LANES), lambda i: (0, 0))],
        out_shape=[jax.ShapeDtypeStruct((TOP_K, t), jnp.int32),
                   jax.ShapeDtypeStruct((TOP_K, t), F32),
                   jax.ShapeDtypeStruct((TOP_K, t), jnp.int32),
                   jax.ShapeDtypeStruct((N_EXPERTS, LANES), F32)],
        scratch_shapes=[pltpu.VMEM((N_EXPERTS, ROW_TILE), F32)],
        compiler_params=_params("arbitrary"),
        name="router",
    )(h2, wr_t, rb_b)


def _row_copy(src, row, dst, sem):
    return pltpu.make_async_copy(src.at[pl.ds(row, 1), :], dst, sem)


DISPATCH_SLOTS = 3


def _dispatch_kernel(slot_ref, h_hbm, xs_hbm, hbuf, lsem, ssem, *, n_tok):
    i = pl.program_id(0)
    nt = pl.num_programs(0)
    tm = ROW_TILE

    def load(blk):
        row0 = pl.multiple_of(blk * tm, tm)
        return pltpu.make_async_copy(h_hbm.at[pl.ds(row0, tm), :], hbuf.at[blk % DISPATCH_SLOTS],
                                     lsem.at[blk % DISPATCH_SLOTS])

    def wait_rows(par):
        for _ in range(TOP_K):
            pltpu.make_async_copy(hbuf.at[0], xs_hbm.at[pl.ds(0, tm), :], ssem.at[par]).wait()

    @pl.when(i == 0)
    def _():
        load(0).start()

    @pl.when(i + 1 < nt)
    def _():
        load(i + 1).start()

    load(i).wait()
    par = i % 2
    cur = hbuf.at[i % DISPATCH_SLOTS]
    for r in range(tm):
        for k in range(TOP_K):
            dst = xs_hbm.at[pl.ds(slot_ref[k * n_tok + i * tm + r], 1), :]
            pltpu.make_async_copy(cur.at[pl.ds(r, 1), :], dst, ssem.at[par]).start(priority=k)

    @pl.when(i > 0)
    def _():
        wait_rows(1 - par)

    @pl.when(i == nt - 1)
    def _():
        wait_rows(par)


def _dispatch(h2, slots, rows):
    t, d = h2.shape
    return pl.pallas_call(
        functools.partial(_dispatch_kernel, n_tok=t),
        grid_spec=pltpu.PrefetchScalarGridSpec(
            num_scalar_prefetch=1,
            grid=(t // ROW_TILE,),
            in_specs=[pl.BlockSpec(memory_space=pl.ANY)],
            out_specs=pl.BlockSpec(memory_space=pl.ANY),
            scratch_shapes=[pltpu.VMEM((DISPATCH_SLOTS, ROW_TILE, d), F32),
                            pltpu.SemaphoreType.DMA((DISPATCH_SLOTS,)),
                            pltpu.SemaphoreType.DMA((2,))]),
        out_shape=jax.ShapeDtypeStruct((rows, d), F32),
        compiler_params=_params("arbitrary"),
        name="dispatch",
    )(slots, h2)


def _experts_kernel(ts_ref, nu_ref, h_hbm, wg_hbm, wu_hbm, wd_hbm, ys_hbm,
                    xbuf, ybuf, gsem, ysem, wg_f, wu_f, wd_f, wsem, wg_bf, wu_bf, wd_bf, *, layer):
    e = pl.program_id(0)
    ne = pl.num_programs(0)
    n_used = nu_ref[0]
    tm = MOE_TILE
    tsub = tm

    def has_rows(ex):
        return ts_ref[ex + 1] > ts_ref[ex]

    def weight_copies(ex, wslot):
        return [pltpu.make_async_copy(src.at[layer, ex], dst.at[wslot], wsem.at[wslot])
                for src, dst in ((wg_hbm, wg_f), (wu_hbm, wu_f), (wd_hbm, wd_f))]

    def start_weights(ex, wslot):
        for cp in weight_copies(ex, wslot):
            cp.start(priority=1)

    def x_load(tile, slot):
        row0 = pl.multiple_of(tile * tsub, tsub)
        return pltpu.make_async_copy(h_hbm.at[pl.ds(row0, tsub), :], xbuf.at[slot], gsem.at[slot])

    def gather(tile, slot):
        x_load(tile, slot).start()

    def wait_gather(slot):
        x_load(0, slot).wait()

    def y_store(tile, slot):
        row0 = pl.multiple_of(tile * tsub, tsub)
        return pltpu.make_async_copy(ybuf.at[slot], ys_hbm.at[pl.ds(row0, tsub), :], ysem.at[slot])

    wslot = e % 2

    @pl.when((e == 0) & has_rows(0))
    def _():
        start_weights(0, 0)

    @pl.when((e == 0) & (n_used > 0))
    def _():
        gather(0, 0)

    nxt = jnp.minimum(e + 1, ne - 1)

    @pl.when((e + 1 < ne) & has_rows(nxt))
    def _():
        start_weights(nxt, 1 - wslot)

    @pl.when(has_rows(e))
    def _():
        for cp in weight_copies(e, wslot):
            cp.wait()
        wg_bf[...] = wg_f[wslot].astype(BF16)
        wu_bf[...] = wu_f[wslot].astype(BF16)
        wd_bf[...] = wd_f[wslot].astype(BF16)

    def tile_body(g, carry):
        slot = g % 2

        @pl.when(g >= 2)
        def _():
            y_store(g - 2, slot).wait()

        wait_gather(slot)
        x = xbuf[slot].astype(BF16)
        gather(jnp.minimum(g + 1, n_used - 1), 1 - slot)
        gate = jnp.dot(x, wg_bf[...], preferred_element_type=F32)
        up = jnp.dot(x, wu_bf[...], preferred_element_type=F32)
        hid = (gate * _sigmoid(gate)) * up
        ybuf[slot] = jnp.dot(hid.astype(BF16), wd_bf[...], preferred_element_type=F32)
        y_store(g, slot).start(priority=1)
        return carry

    lax.fori_loop(ts_ref[e], ts_ref[e + 1], tile_body, 0)

    @pl.when((e == pl.num_programs(0) - 1) & (n_used > 0))
    def _():
        wait_gather(n_used % 2)
        y_store(n_used - 1, (n_used - 1) % 2).wait()

        @pl.when(n_used >= 2)
        def _():
            y_store(n_used - 2, n_used % 2).wait()


def _experts(xs, tile_start, n_used, w_gate, w_up, w_down, layer):
    d = D_MODEL
    tm = MOE_TILE
    rows = xs.shape[0]

    hbm = pl.BlockSpec(memory_space=pl.ANY)
    return pl.pallas_call(
        functools.partial(_experts_kernel, layer=layer),
        grid_spec=pltpu.PrefetchScalarGridSpec(
            num_scalar_prefetch=2,
            grid=(N_EXPERTS,),
            in_specs=[hbm, hbm, hbm, hbm],
            out_specs=hbm,
            scratch_shapes=[pltpu.VMEM((2, tm, d), F32),
                            pltpu.VMEM((2, tm, d), F32),
                            pltpu.SemaphoreType.DMA((2,)),
                            pltpu.SemaphoreType.DMA((2,)),
                            pltpu.VMEM((2, d, EXPERT_FF), F32),
                            pltpu.VMEM((2, d, EXPERT_FF), F32),
                            pltpu.VMEM((2, EXPERT_FF, d), F32),
                            pltpu.SemaphoreType.DMA((2,)),
                            pltpu.VMEM((d, EXPERT_FF), BF16),
                            pltpu.VMEM((d, EXPERT_FF), BF16),
                            pltpu.VMEM((EXPERT_FF, d), BF16)]),
        out_shape=jax.ShapeDtypeStruct((rows, d), F32),
        compiler_params=_params("arbitrary"),
        name="experts",
    )(tile_start, n_used, xs, w_gate, w_up, w_down)


def _combine_kernel(slot_ref, ys_hbm, w_ref, x_ref, mod_ref, g_ref, b_ref, o_ref, ybuf, sem, *, n_tok):
    i = pl.program_id(0)
    nt = pl.num_programs(0)
    tm = ROW_TILE

    def issue(blk, buf):
        for r in range(tm):
            for k in range(TOP_K):
                row = slot_ref[k * n_tok + blk * tm + r]
                _row_copy(ys_hbm, row, ybuf.at[buf, k, pl.ds(r, 1), :], sem.at[buf]).start(priority=k)

    def wait(buf):
        for k in range(TOP_K):
            pltpu.make_async_copy(ys_hbm.at[pl.ds(0, tm), :], ybuf.at[buf, k], sem.at[buf]).wait()

    @pl.when(i == 0)
    def _():
        issue(0, 0)

    buf = i % 2
    wait(buf)
    issue(jnp.minimum(i + 1, nt - 1), 1 - buf)
    y = w_ref[:, 0:1] * ybuf[buf, 0] + w_ref[:, 1:2] * ybuf[buf, 1]
    t = DEEPNORM_ALPHA * x_ref[...] + mod_ref[MOD_G2:MOD_G2 + 1, :] * y
    o_ref[...] = _layer_norm(t, g_ref[...], b_ref[...])

    @pl.when(i == nt - 1)
    def _():
        wait(1 - buf)


def _combine(ys, slots, w_tok, x1, mods, n_lat_blocks, ln_g, ln_b):
    t = x1.shape[0]
    d = D_MODEL
    return pl.pallas_call(
        functools.partial(_combine_kernel, n_tok=t),
        grid_spec=pltpu.PrefetchScalarGridSpec(
            num_scalar_prefetch=1,
            grid=(t // ROW_TILE,),
            in_specs=[pl.BlockSpec(memory_space=pl.ANY),
                      pl.BlockSpec((ROW_TILE, TOP_K), lambda i, s: (i, 0)),
                      pl.BlockSpec((ROW_TILE, d), lambda i, s: (i, 0)),
                      pl.BlockSpec((None, 8, d), lambda i, s: (jnp.where(i < n_lat_blocks, 1, 0), 0, 0)),
                      pl.BlockSpec((1, d), lambda i, s: (0, 0)),
                      pl.BlockSpec((1, d), lambda i, s: (0, 0))],
            out_specs=pl.BlockSpec((ROW_TILE, d), lambda i, s: (i, 0)),
            scratch_shapes=[pltpu.VMEM((2, TOP_K, ROW_TILE, d), F32),
                            pltpu.SemaphoreType.DMA((2,))]),
        out_shape=jax.ShapeDtypeStruct((t, d), F32),
        compiler_params=_params("arbitrary"),
        name="combine",
    )(slots, ys, w_tok, x1, mods, ln_g, ln_b)


def _moe(h2, x1, mods, n_lat_blocks, wr_t, rb_b, w_gate, w_up, w_down, layer, ln_g, ln_b):
    t = x1.shape[0]
    tm = MOE_TILE
    e_idx, w_tok, rank, cnt = _router(h2, wr_t, rb_b)
    counts = cnt[:, 0].astype(jnp.int32)
    tiles_per = (counts + tm - 1) // tm
    tile_end = jnp.cumsum(tiles_per)
    n_used = tile_end[-1]
    row_off = (tile_end - tiles_per) * tm
    experts = jnp.arange(N_EXPERTS, dtype=jnp.int32)
    slots = jnp.sum(jnp.where(e_idx[:, :, None] == experts, row_off, 0), axis=-1) + rank
    max_tiles = (TOP_K * t + N_EXPERTS * (tm - 1)) // tm + 1
    tile_start = jnp.concatenate([tile_end - tiles_per, n_used.reshape(1)]).astype(jnp.int32)
    slots = slots.reshape(-1)
    xs = _dispatch(h2, slots, max_tiles * tm)
    ys = _experts(xs, tile_start, n_used.reshape(1).astype(jnp.int32), w_gate, w_up, w_down, layer)
    return _combine(ys, slots, w_tok.T, x1, mods, n_lat_blocks, ln_g, ln_b)


def _rope_tables(n):
    t = np.arange(n)
    row = (t // GRID_W).astype(np.float64)
    col = (t % GRID_W).astype(np.float64)
    n_freq = HEAD_DIM // 4
    inv_freq = ROPE_BASE ** (-np.arange(n_freq, dtype=np.float64) / n_freq)
    ang = np.concatenate([row[:, None] * inv_freq, col[:, None] * inv_freq], axis=-1)
    cos, sin = np.cos(ang), np.sin(ang)
    return (np.concatenate([cos, cos], axis=-1).astype(np.float32),
            np.concatenate([-sin, sin], axis=-1).astype(np.float32))


def kernel(x, c, ctx, c_ctx, w_mod, b_mod, w_in, attn_sink, na_rpb, sgu_ln_g, sgu_ln_b, sgu_w, sgu_b,
           w_out, ln1_g, ln1_b, w_router, router_bias, w_gate, w_up, w_down, ln2_g, ln2_b):
    batch, n, d = x.shape
    lctx = ctx.shape[1]
    assert batch == 1 and d == D_MODEL and n % ROW_TILE == 0 and lctx % ROW_TILE == 0
    n_lat_blocks = n // ROW_TILE

    mods = _modulation(c, c_ctx, w_mod, b_mod).reshape(DEPTH, 8, 6, d)
    mod_lat = jnp.pad(mods[:, 0], ((0, 0), (0, 2), (0, 0)))
    mod_ctx = jnp.pad(mods[:, 1], ((0, 0), (0, 2), (0, 0)))
    cos, sin = _rope_tables(n)
    wr_f = w_router.T
    def bf16_part(v):
        return lax.bitcast_convert_type(lax.bitcast_convert_type(v, jnp.int32) & jnp.int32(-65536), F32)

    wr_hi = bf16_part(wr_f)
    wr_mid = bf16_part(wr_f - wr_hi)
    wr_lo = wr_f - wr_hi - wr_mid
    wr_t = jnp.concatenate([wr_hi, wr_mid, wr_lo], axis=0).astype(BF16)
    rb_b = jnp.broadcast_to(router_bias.reshape(N_EXPERTS, 1), (N_EXPERTS, LANES))

    x_lat = x[0]
    x_ctx, ctx_row0 = ctx[0], 0
    w_in_bf = w_in.astype(BF16)
    w_out_bf = w_out.astype(BF16)
    bias_tabs = _na_bias_tables(na_rpb)
    for l in range(DEPTH):
        last = l == DEPTH - 1
        sgu_params = (sgu_ln_g[l].reshape(1, C_W), sgu_ln_b[l].reshape(1, C_W), sgu_w[l].astype(BF16),
                      jnp.broadcast_to(sgu_b[l][:, :, None], (C_GROUPS, CHUNK, LANES)))
        g1, b1 = ln1_g[l].reshape(1, d), ln1_b[l].reshape(1, d)
        g2, b2 = ln2_g[l].reshape(1, d), ln2_b[l].reshape(1, d)

        p = _proj(x_lat, 0, n, mod_lat[l], cos, sin, w_in_bf, l, rope=True)
        pc = _proj(x_ctx, ctx_row0, lctx, mod_ctx[l], cos, sin, w_in_bf, l, rope=False)
        o_a, o_c = _attn_a_sgu(p, pc, attn_sink[l], sgu_params, latent=True)
        o_b = _attn_b(p, pc, bias_tabs, l, latent=True)
        mod_both = jnp.stack([mod_ctx[l], mod_lat[l]])
        if last:
            x1, h2 = _outproj(o_a, o_b, o_c, w_out_bf, l, x_lat, 0, mod_lat[l], g1, b1, n, 0, None)
            x_lat = _moe(h2, x1, mod_both, n_lat_blocks, wr_t, rb_b, w_gate, w_up, w_down, l, g2, b2)
        else:
            oc_a, oc_c = _attn_a_sgu(pc, pc, attn_sink[l], sgu_params, latent=False)
            oc_b = _attn_b(pc, pc, bias_tabs, l, latent=False)
            total = n + lctx
            prev = _outproj(o_a, o_b, o_c, w_out_bf, l, x_lat, 0, mod_lat[l], g1, b1, total, 0, None)
            x1, h2 = _outproj(oc_a, oc_b, oc_c, w_out_bf, l, x_ctx, ctx_row0, mod_ctx[l], g1, b1, total, n,
                              prev)
            x_all = _moe(h2, x1, mod_both, n_lat_blocks, wr_t, rb_b, w_gate, w_up, w_down, l, g2, b2)
            x_lat, x_ctx, ctx_row0 = x_all, x_all, n
    return x_lat.reshape(batch, n, d)
```

```python
import functools

import numpy as np
import jax
import jax.numpy as jnp
from jax import lax
from jax.experimental import pallas as pl
from jax.experimental.pallas import tpu as pltpu

F32 = jnp.float32
BF16 = jnp.bfloat16

D_MODEL = 2048
DEPTH = 2
GRID_W = 64
HEAD_DIM = 128
A_HEADS = 6
A_KV_HEADS = 2
A_GROUP = A_HEADS // A_KV_HEADS
A_BLOCK = 128
B_HEADS = 6
NA_ROWS = 8
NA_COLS = 16
C_GROUPS = 4
C_W = C_GROUPS * HEAD_DIM
CHUNK = 128
N_EXPERTS = 32
N_EXPERT_GROUPS = 4
EXPERTS_PER_GROUP = N_EXPERTS // N_EXPERT_GROUPS
TOP_K = 2
EXPERT_FF = 512
ROPE_BASE = 10000.0
LN_EPS = 1e-5
NEG_INF = -1e30
DEEPNORM_ALPHA = (2 * DEPTH) ** 0.25
ATTN_SCALE = HEAD_DIM ** -0.5

A_Q_W = A_HEADS * HEAD_DIM
A_KV_W = A_KV_HEADS * HEAD_DIM
B_W = B_HEADS * HEAD_DIM
OFF_AK = A_Q_W
OFF_AV = OFF_AK + A_KV_W
OFF_BQ = OFF_AV + A_KV_W
OFF_BK = OFF_BQ + B_W
OFF_BV = OFF_BK + B_W
OFF_C = OFF_BV + B_W
IN_COLS = OFF_C + 2 * C_W

VMEM_LIMIT_BYTES = 56 * 1024 * 1024
LANES = 128

ROW_TILE = 256
MATMUL_ROW_TILE = 512
PROJ_COL_TILE = 512
MOD_COL_TILE = 1024
MOE_TILE = 256

MOD_SH1, MOD_SC1, MOD_G1, MOD_SH2, MOD_SC2, MOD_G2 = range(6)


def _params(*sem):
    return pltpu.CompilerParams(dimension_semantics=sem, vmem_limit_bytes=VMEM_LIMIT_BYTES)


def _layer_norm(t, g, b):
    mu = jnp.mean(t, axis=-1, keepdims=True)
    d = t - mu
    var = jnp.mean(d * d, axis=-1, keepdims=True)
    return d * lax.rsqrt(var + LN_EPS) * g + b


def _sigmoid(v):
    return 1.0 / (1.0 + jnp.exp(-v))


def _dot_nt(a, b):
    return lax.dot_general(a, b, (((1,), (1,)), ((), ())), preferred_element_type=F32)


def _mod_kernel(c_ref, w_ref, b_ref, o_ref):
    w = w_ref[...]
    reps = w.shape[1] // LANES
    rows = []
    for r in range(2):
        cv = c_ref[r]
        s = cv * _sigmoid(cv)
        sb = jnp.concatenate([s] * reps, axis=1)
        rows.append(jnp.sum(w * sb, axis=0, keepdims=True) + b_ref[...])
    rows.append(jnp.zeros((6, w.shape[1]), F32))
    o_ref[...] = jnp.concatenate(rows, axis=0)


def _modulation(c, c_ctx, w_mod, b_mod):
    d = D_MODEL
    cb = jnp.stack([jnp.broadcast_to(c.reshape(d, 1), (d, LANES)),
                    jnp.broadcast_to(c_ctx.reshape(d, 1), (d, LANES))])
    n_out = 6 * d
    return pl.pallas_call(
        _mod_kernel,
        grid=(DEPTH, n_out // MOD_COL_TILE),
        in_specs=[pl.BlockSpec((2, d, LANES), lambda l, j: (0, 0, 0)),
                  pl.BlockSpec((None, d, MOD_COL_TILE), lambda l, j: (l, 0, j)),
                  pl.BlockSpec((None, 1, MOD_COL_TILE), lambda l, j: (l, 0, j))],
        out_specs=pl.BlockSpec((None, 8, MOD_COL_TILE), lambda l, j: (l, 0, j)),
        out_shape=jax.ShapeDtypeStruct((DEPTH, 8, n_out), F32),
        compiler_params=_params("arbitrary", "arbitrary"),
        name="modulation",
    )(cb, w_mod, b_mod.reshape(DEPTH, 1, n_out))


def _gelu_tanh(v):
    return 0.5 * v * (1.0 + jnp.tanh(np.sqrt(2.0 / np.pi).astype(np.float32) * (v + 0.044715 * (v * v * v))))


def _proj_kernel(x_ref, mod_ref, cos_ref, sin_ref, w_ref, o_ref, *, rope):
    x = x_ref[...]
    h = (x * (1.0 + mod_ref[MOD_SC1:MOD_SC1 + 1, :]) + mod_ref[MOD_SH1:MOD_SH1 + 1, :]).astype(BF16)
    tn = PROJ_COL_TILE
    for j in range(IN_COLS // tn):
        c0 = j * tn
        acc = jnp.dot(h, w_ref[:, c0:c0 + tn], preferred_element_type=F32)
        if c0 < OFF_AV:
            if rope:
                cos = cos_ref[...]
                sin = sin_ref[...]
                parts = []
                for hh in range(tn // HEAD_DIM):
                    a = acc[:, hh * HEAD_DIM:(hh + 1) * HEAD_DIM]
                    parts.append(a * cos + pltpu.roll(a, HEAD_DIM // 2, 1) * sin)
                acc = jnp.concatenate(parts, axis=1)
        elif c0 >= OFF_C:
            acc = _gelu_tanh(acc)
        o_ref[:, c0:c0 + tn] = acc.astype(BF16)


def _matmul_row_tile(*row_counts):
    return MATMUL_ROW_TILE if all(r % MATMUL_ROW_TILE == 0 for r in row_counts) else ROW_TILE


def _proj(x, x_row0, rows, mod, cos, sin, w_bf, layer, rope):
    d = D_MODEL
    tm = _matmul_row_tile(rows, x_row0)
    x_off = x_row0 // tm
    return pl.pallas_call(
        functools.partial(_proj_kernel, rope=rope),
        grid=(rows // tm,),
        in_specs=[pl.BlockSpec((tm, d), lambda i: (i + x_off, 0)),
                  pl.BlockSpec((8, d), lambda i: (0, 0)),
                  pl.BlockSpec((tm, HEAD_DIM), lambda i: (i, 0)),
                  pl.BlockSpec((tm, HEAD_DIM), lambda i: (i, 0)),
                  pl.BlockSpec((None, d, IN_COLS), lambda i: (layer, 0, 0), pipeline_mode=pl.Buffered(1))],
        out_specs=pl.BlockSpec((tm, IN_COLS), lambda i: (i, 0)),
        out_shape=jax.ShapeDtypeStruct((rows, IN_COLS), BF16),
        compiler_params=_params("arbitrary"),
        name="proj_rope" if rope else "proj_ctx",
    )(x, mod, cos, sin, w_bf)


LOG2E = 1.4426950408889634
LOGIT_SCALE = ATTN_SCALE * LOG2E


def _softmax_pv(problems):
    ms = []
    for s_parts, _, sink in problems:
        m = s_parts[0].max(axis=-1, keepdims=True)
        for s in s_parts[1:]:
            m = jnp.maximum(m, s.max(axis=-1, keepdims=True))
        ms.append(m if sink is None else jnp.maximum(m, sink))
    es = [[jnp.exp2(s - m).astype(BF16) for s in s_parts] for (s_parts, _, _), m in zip(problems, ms)]
    outs = []
    for (_, v_parts, sink), m, e_parts in zip(problems, ms, es):
        acc = None
        for e, v in zip(e_parts, v_parts):
            v1 = jnp.concatenate([v, jnp.ones((v.shape[0], LANES), BF16)], axis=1)
            pv = jnp.dot(e, v1, preferred_element_type=F32)
            acc = pv if acc is None else acc + pv
        denom = acc[:, HEAD_DIM:HEAD_DIM + 1]
        if sink is not None:
            denom = denom + jnp.exp2(sink - m)
        outs.append(acc[:, :HEAD_DIM] / denom)
    return outs


A_STEP_BLOCKS = 4


def _attn_a_kernel(sink_ref, q_ref, *refs, latent, step_blocks):
    nband = step_blocks + 2
    k_refs, v_refs = refs[:nband], refs[nband:2 * nband]
    kx_ref, vx_ref, u_ref, v_ref, sg_ref, sb_ref, sw_ref, sbs_ref, o_ref, oc_ref, mask_ref = refs[2 * nband:]
    i = pl.program_id(0)
    nb = pl.num_programs(0) * step_blocks
    nq = A_GROUP * A_BLOCK

    if latent:
        @pl.when(i == 0)
        def _():
            qi = lax.broadcasted_iota(jnp.int32, (nq, 3 * A_BLOCK), 0) % A_BLOCK
            jj = lax.broadcasted_iota(jnp.int32, (nq, 3 * A_BLOCK), 1)
            ok = (jj >= qi) & (jj <= qi + 2 * A_BLOCK)
            mask_ref[0] = jnp.where(ok & (jj >= A_BLOCK), 0.0, NEG_INF).astype(F32)
            mask_ref[1] = jnp.where(ok, 0.0, NEG_INF).astype(F32)
            mask_ref[2] = jnp.where(ok & (jj < 2 * A_BLOCK), 0.0, NEG_INF).astype(F32)

    sinks = [jnp.concatenate([jnp.full((A_BLOCK, 1), sink_ref[kh * A_GROUP + g] * LOG2E, F32)
                              for g in range(A_GROUP)], axis=0) for kh in range(A_KV_HEADS)]
    for j in range(step_blocks):
        rows = pl.ds(j * A_BLOCK, A_BLOCK)
        _sgu_chunk(u_ref.at[rows, :], v_ref.at[rows, :], sg_ref, sb_ref, sw_ref, sbs_ref, oc_ref.at[rows, :])
        if latent:
            blk = i * step_blocks + j
            which = jnp.where(blk == 0, 0, jnp.where(blk == nb - 1, 2, 1))
        problems = []
        for kh in range(A_KV_HEADS):
            ks = slice(kh * HEAD_DIM, (kh + 1) * HEAD_DIM)
            q = jnp.concatenate([q_ref[rows, (kh * A_GROUP + g) * HEAD_DIM:(kh * A_GROUP + g + 1) * HEAD_DIM]
                                 for g in range(A_GROUP)], axis=0)
            s_parts = [_dot_nt(q, kx_ref[:, ks]) * LOGIT_SCALE]
            v_parts = [vx_ref[:, ks]]
            if latent:
                kband = jnp.concatenate([r[:, ks] for r in k_refs[j:j + 3]], axis=0)
                vband = jnp.concatenate([r[:, ks] for r in v_refs[j:j + 3]], axis=0)
                s_parts.append(_dot_nt(q, kband) * LOGIT_SCALE + mask_ref[which])
                v_parts.append(vband)
            problems.append((s_parts, v_parts, sinks[kh]))
        for kh, out in enumerate(_softmax_pv(problems)):
            for g in range(A_GROUP):
                h = kh * A_GROUP + g
                o_ref[rows, h * HEAD_DIM:(h + 1) * HEAD_DIM] = out[g * A_BLOCK:(g + 1) * A_BLOCK].astype(BF16)


def _attn_a_sgu(p, pc, sink, sgu_params, latent):
    assert CHUNK == A_BLOCK
    rows = p.shape[0]
    nb = rows // A_BLOCK
    step_blocks = A_STEP_BLOCKS if nb % A_STEP_BLOCKS == 0 else 2
    assert nb >= 2 and nb % step_blocks == 0
    kcol = OFF_AK // A_KV_W
    vcol = OFF_AV // A_KV_W
    ucol = OFF_C // C_W
    tile = step_blocks * A_BLOCK
    nband = step_blocks + 2

    def band(col):
        return [pl.BlockSpec((A_BLOCK, A_KV_W),
                             lambda i, s, sh=shift: (jnp.clip(i * step_blocks + sh, 0, nb - 1), col))
                for shift in range(-1, step_blocks + 1)]

    lctx = pc.shape[0]
    return pl.pallas_call(
        functools.partial(_attn_a_kernel, latent=latent, step_blocks=step_blocks),
        grid_spec=pltpu.PrefetchScalarGridSpec(
            num_scalar_prefetch=1,
            grid=(nb // step_blocks,),
            in_specs=([pl.BlockSpec((tile, A_Q_W), lambda i, s: (i, 0))] + band(kcol) + band(vcol)
                      + [pl.BlockSpec((lctx, A_KV_W), lambda i, s: (0, kcol)),
                         pl.BlockSpec((lctx, A_KV_W), lambda i, s: (0, vcol)),
                         pl.BlockSpec((tile, C_W), lambda i, s: (i, ucol)),
                         pl.BlockSpec((tile, C_W), lambda i, s: (i, ucol + 1)),
                         pl.BlockSpec((1, C_W), lambda i, s: (0, 0)),
                         pl.BlockSpec((1, C_W), lambda i, s: (0, 0)),
                         pl.BlockSpec((C_GROUPS, CHUNK, CHUNK), lambda i, s: (0, 0, 0)),
                         pl.BlockSpec((C_GROUPS, CHUNK, LANES), lambda i, s: (0, 0, 0))]),
            out_specs=[pl.BlockSpec((tile, A_Q_W), lambda i, s: (i, 0)),
                       pl.BlockSpec((tile, C_W), lambda i, s: (i, 0))],
            scratch_shapes=[pltpu.VMEM((3, A_GROUP * A_BLOCK, 3 * A_BLOCK), F32)]),
        out_shape=[jax.ShapeDtypeStruct((rows, A_Q_W), BF16), jax.ShapeDtypeStruct((rows, C_W), BF16)],
        compiler_params=_params("arbitrary"),
        name="attn_a_latent" if latent else "attn_a_ctx",
    )(*([sink] + [p] * (1 + 2 * nband) + [pc, pc, p, p] + list(sgu_params)))


B_PAIR_W = 2 * HEAD_DIM
NA_TILE = 256
NA_STEP_GROUPS = 2
NA_GROUP_ROWS = NA_TILE // GRID_W
NA_WIN_ROWS = NA_ROWS + NA_GROUP_ROWS
NA_PAIRS = NA_WIN_ROWS // 2
NA_BIAS_OFFS = 2 * NA_ROWS
NA_BOTH, NA_LEFT, NA_RIGHT = range(3)


def _attn_b_kernel(*refs, latent, grid_rows, step_groups):
    npair = B_HEADS // 2
    q_refs, k_refs, v_refs, kx_refs, vx_refs = (refs[j * npair:(j + 1) * npair] for j in range(5))
    bias_ref, o_ref = refs[5 * npair], refs[5 * npair + 1]
    g = pl.program_id(0)
    for jg in range(step_groups):
        rows = pl.ds(jg * NA_TILE, NA_TILE) if latent else slice(None)
        if latent:
            r_base = (g * step_groups + jg) * NA_GROUP_ROWS
            w0 = jnp.clip(r_base - NA_ROWS // 2, 0, grid_rows - NA_WIN_ROWS)
            start = pl.multiple_of(w0 * GRID_W, GRID_W)
        for hp in range(npair):
            problems = []
            for hh in range(2):
                hs = slice(hh * HEAD_DIM, (hh + 1) * HEAD_DIM)
                q = q_refs[hp][rows, hs]
                s_parts = [_dot_nt(q, kx_refs[hp][:, hs]) * LOGIT_SCALE]
                v_parts = [vx_refs[hp][:, hs]]
                if latent:
                    kwin = k_refs[hp][pl.ds(start, NA_WIN_ROWS * GRID_W), hs]
                    vwin = v_refs[hp][pl.ds(start, NA_WIN_ROWS * GRID_W), hs]
                    bias_rows = []
                    for rr in range(NA_GROUP_ROWS):
                        r = r_base + rr
                        r0 = jnp.clip(r - NA_ROWS // 2, 0, grid_rows - NA_ROWS)
                        tiles = []
                        for jp in range(NA_PAIRS):
                            ka = w0 + 2 * jp
                            in_a = (ka >= r0) & (ka < r0 + NA_ROWS)
                            in_b = (ka + 1 >= r0) & (ka + 1 < r0 + NA_ROWS)
                            variant = jnp.where(in_a, jnp.where(in_b, NA_BOTH, NA_LEFT),
                                                jnp.where(in_b, NA_RIGHT, NA_LEFT))
                            off = jnp.where(in_a | in_b, jnp.clip(ka - r + NA_ROWS, 0, NA_BIAS_OFFS - 1), 0)
                            tiles.append(bias_ref[2 * hp + hh, variant, off])
                        bias_rows.append(jnp.concatenate(tiles, axis=1))
                    bias = jnp.concatenate(bias_rows, axis=0)
                    s_parts.append(_dot_nt(q, kwin) * LOGIT_SCALE + bias)
                    v_parts.append(vwin)
                problems.append((s_parts, v_parts, None))
            for hh, out in enumerate(_softmax_pv(problems)):
                h = 2 * hp + hh
                o_ref[rows, h * HEAD_DIM:(h + 1) * HEAD_DIM] = out.astype(BF16)


def _na_bias_tables(rpb):
    cols = np.arange(GRID_W)
    c0 = np.clip(cols - NA_COLS // 2, 0, GRID_W - NA_COLS)
    rel = cols[None, :] - cols[:, None] + NA_COLS - 1
    ok = (cols[None, :] >= c0[:, None]) & (cols[None, :] < c0[:, None] + NA_COLS)
    onehot = (rel[None] == np.arange(2 * NA_COLS - 1)[:, None, None]).astype(np.float32)
    t = jnp.einsum("lhrd,dqk->lhrqk", rpb, onehot, precision=lax.Precision.HIGHEST)
    t = jnp.where(ok, t * LOG2E, NEG_INF).astype(F32)
    t = jnp.pad(t, ((0, 0), (0, 0), (1, 1), (0, 0), (0, 0)), constant_values=NEG_INF)
    pairs = jnp.concatenate([t[:, :, :-1], t[:, :, 1:]], axis=-1)
    keep = np.ones((3, 1, 1, 2 * GRID_W), bool)
    keep[NA_LEFT, :, :, GRID_W:] = False
    keep[NA_RIGHT, :, :, :GRID_W] = False
    return jnp.where(keep, pairs[:, :, None], NEG_INF)


def _attn_b(p, pc, bias_tabs, layer, latent):
    rows = p.shape[0]
    lctx = pc.shape[0]
    assert not latent or (rows // GRID_W >= NA_WIN_ROWS and rows % (NA_STEP_GROUPS * NA_TILE) == 0)
    step_groups = NA_STEP_GROUPS if latent else 1
    tile = step_groups * NA_TILE if latent else ROW_TILE
    npair = B_HEADS // 2
    qcol = OFF_BQ // B_PAIR_W
    kcol = OFF_BK // B_PAIR_W
    vcol = OFF_BV // B_PAIR_W

    def resident(nrows, col):
        return [pl.BlockSpec((nrows, B_PAIR_W), lambda g, c=col + hp: (0, c), pipeline_mode=pl.Buffered(1))
                for hp in range(npair)]

    return pl.pallas_call(
        functools.partial(_attn_b_kernel, latent=latent, grid_rows=rows // GRID_W, step_groups=step_groups),
        grid=(rows // tile,),
        in_specs=([pl.BlockSpec((tile, B_PAIR_W), lambda g, c=qcol + hp: (g, c)) for hp in range(npair)]
                  + resident(rows, kcol) + resident(rows, vcol) + resident(lctx, kcol) + resident(lctx, vcol)
                  + [pl.BlockSpec((None, B_HEADS, 3, NA_BIAS_OFFS, GRID_W, 2 * GRID_W),
                                  lambda g: (layer, 0, 0, 0, 0, 0), pipeline_mode=pl.Buffered(1))]),
        out_specs=pl.BlockSpec((tile, B_W), lambda g: (g, 0)),
        out_shape=jax.ShapeDtypeStruct((rows, B_W), BF16),
        compiler_params=_params("arbitrary"),
        name="attn_b_latent" if latent else "attn_b_ctx",
    )(*([p] * (3 * npair) + [pc] * (2 * npair) + [bias_tabs]))


def _sgu_chunk(u_ref, v_ref, g_ref, b_ref, w_ref, bs_ref, o_ref):
    for grp in range(C_GROUPS):
        cs = slice(grp * HEAD_DIM, (grp + 1) * HEAD_DIM)
        vn = _layer_norm(v_ref[:, cs].astype(F32), g_ref[:, cs], b_ref[:, cs])
        mixed = jnp.dot(w_ref[grp], vn.astype(BF16), preferred_element_type=F32) + bs_ref[grp]
        o_ref[:, cs] = (u_ref[:, cs].astype(F32) * mixed).astype(BF16)


def _outproj_kernel(oa_ref, ob_ref, oc_ref, w_ref, x_ref, mod_ref, g_ref, b_ref, *rest):
    x1_ref, h2_ref = rest[-2], rest[-1]
    mix = jnp.dot(oa_ref[...], w_ref[0:A_Q_W, :], preferred_element_type=F32)
    mix += jnp.dot(ob_ref[...], w_ref[A_Q_W:A_Q_W + B_W, :], preferred_element_type=F32)
    mix += jnp.dot(oc_ref[...], w_ref[A_Q_W + B_W:, :], preferred_element_type=F32)
    t = DEEPNORM_ALPHA * x_ref[...] + mod_ref[MOD_G1:MOD_G1 + 1, :] * mix
    x1 = _layer_norm(t, g_ref[...], b_ref[...])
    x1_ref[...] = x1
    h2_ref[...] = x1 * (1.0 + mod_ref[MOD_SC2:MOD_SC2 + 1, :]) + mod_ref[MOD_SH2:MOD_SH2 + 1, :]


def _outproj(o_a, o_b, o_c, w_bf, layer, x, x_row0, mod, ln_g, ln_b, total_rows, out_row0, prev):
    rows = o_a.shape[0]
    d = D_MODEL
    tm = _matmul_row_tile(rows, x_row0, out_row0)
    x_off, out_off = x_row0 // tm, out_row0 // tm
    in_specs = [pl.BlockSpec((tm, A_Q_W), lambda i: (i, 0)),
                pl.BlockSpec((tm, B_W), lambda i: (i, 0)),
                pl.BlockSpec((tm, C_W), lambda i: (i, 0)),
                pl.BlockSpec((None, d, d), lambda i: (layer, 0, 0), pipeline_mode=pl.Buffered(1)),
                pl.BlockSpec((tm, d), lambda i: (i + x_off, 0)),
                pl.BlockSpec((8, d), lambda i: (0, 0)),
                pl.BlockSpec((1, d), lambda i: (0, 0)),
                pl.BlockSpec((1, d), lambda i: (0, 0))]
    args = [o_a, o_b, o_c, w_bf, x, mod, ln_g, ln_b]
    aliases = {}
    if prev is not None:
        in_specs += [pl.BlockSpec(memory_space=pl.ANY), pl.BlockSpec(memory_space=pl.ANY)]
        aliases = {len(args): 0, len(args) + 1: 1}
        args += list(prev)
    return pl.pallas_call(
        _outproj_kernel,
        grid=(rows // tm,),
        in_specs=in_specs,
        out_specs=[pl.BlockSpec((tm, d), lambda i: (i + out_off, 0))] * 2,
        out_shape=[jax.ShapeDtypeStruct((total_rows, d), F32)] * 2,
        input_output_aliases=aliases,
        compiler_params=_params("arbitrary"),
        name="outproj",
    )(*args)


def _top2_sublanes(vals, sub):
    m1 = vals.max(axis=0, keepdims=True)
    i1 = jnp.where(vals == m1, sub, vals.shape[0]).min(axis=0, keepdims=True)
    rest = jnp.where(sub == i1, -jnp.inf, vals)
    m2 = rest.max(axis=0, keepdims=True)
    i2 = jnp.where(rest == m2, sub, vals.shape[0]).min(axis=0, keepdims=True)
    return m1, i1, m2, i2


def _router_kernel(h_ref, wr_ref, rb_ref, e_ref, w_ref, rank_ref, cnt_ref, run_ref):
    i = pl.program_id(0)
    tm = ROW_TILE
    epg = EXPERTS_PER_GROUP

    @pl.when(i == 0)
    def _():
        run_ref[...] = jnp.zeros_like(run_ref)

    h = h_ref[...]
    h_hi = h.astype(BF16)
    h_mid = (h - h_hi.astype(F32)).astype(BF16)
    parts = _dot_nt(wr_ref[...], h_hi)
    parts_mid = _dot_nt(wr_ref[0:2 * N_EXPERTS, :], h_mid)
    logits = (parts[0:N_EXPERTS] + parts[N_EXPERTS:2 * N_EXPERTS] + parts[2 * N_EXPERTS:]
              + parts_mid[0:N_EXPERTS] + parts_mid[N_EXPERTS:])
    scores = _sigmoid(logits)
    biased = scores + jnp.concatenate([rb_ref[...]] * (tm // LANES), axis=1)
    sub = lax.broadcasted_iota(jnp.int32, (epg, tm), 0)

    best = None
    for g in range(N_EXPERT_GROUPS):
        m1, _, m2, _ = _top2_sublanes(biased[g * epg:(g + 1) * epg], sub)
        gs = m1 + m2
        if best is None:
            best, grp = gs, jnp.zeros((1, tm), jnp.int32)
            bsel, ssel = biased[0:epg], scores[0:epg]
        else:
            better = gs > best
            best = jnp.where(better, gs, best)
            grp = jnp.where(better, g, grp)
            bsel = jnp.where(better, biased[g * epg:(g + 1) * epg], bsel)
            ssel = jnp.where(better, scores[g * epg:(g + 1) * epg], ssel)
    _, i1, _, i2 = _top2_sublanes(bsel, sub)
    w1 = jnp.where(sub == i1, ssel, 0.0).sum(axis=0, keepdims=True)
    w2 = jnp.where(sub == i2, ssel, 0.0).sum(axis=0, keepdims=True)
    tot = w1 + w2
    e1 = grp * epg + i1
    e2 = grp * epg + i2

    eiota = lax.broadcasted_iota(jnp.int32, (N_EXPERTS, tm), 0)
    oh1 = (eiota == e1).astype(F32)
    oh2 = (eiota == e2).astype(F32)
    ohb = oh1 + oh2
    before = (lax.broadcasted_iota(jnp.int32, (tm, tm), 0) < lax.broadcasted_iota(jnp.int32, (tm, tm), 1))
    prefix = jnp.dot(ohb.astype(BF16), before.astype(BF16), preferred_element_type=F32)
    pos = run_ref[...] + prefix
    r1 = (oh1 * pos).sum(axis=0, keepdims=True)
    r2 = (oh2 * pos).sum(axis=0, keepdims=True)
    run_ref[...] = run_ref[...] + ohb.sum(axis=1, keepdims=True)

    e_ref[...] = jnp.concatenate([e1, e2], axis=0)
    w_ref[...] = jnp.concatenate([w1 / tot, w2 / tot], axis=0)
    rank_ref[...] = jnp.concatenate([r1, r2], axis=0).astype(jnp.int32)
    cnt_ref[...] = run_ref[:, 0:LANES]


def _router(h2, wr_t, rb_b):
    t = h2.shape[0]
    d = D_MODEL
    row2 = pl.BlockSpec((TOP_K, ROW_TILE), lambda i: (0, i))
    return pl.pallas_call(
        _router_kernel,
        grid=(t // ROW_TILE,),
        in_specs=[pl.BlockSpec((ROW_TILE, d), lambda i: (i, 0)),
                  pl.BlockSpec((3 * N_EXPERTS, d), lambda i: (0, 0)),
                  pl.BlockSpec((N_EXPERTS, LANES), lambda i: (0, 0))],
        out_specs=[row2, row2, row2, pl.BlockSpec((N_EXPERTS, LANES), lambda i: (0, 0))],
        out_shape=[jax.ShapeDtypeStruct((TOP_K, t), jnp.int32),
                   jax.ShapeDtypeStruct((TOP_K, t), F32),
                   jax.ShapeDtypeStruct((TOP_K, t), jnp.int32),
                   jax.ShapeDtypeStruct((N_EXPERTS, LANES), F32)],
        scratch_shapes=[pltpu.VMEM((N_EXPERTS, ROW_TILE), F32)],
        compiler_params=_params("arbitrary"),
        name="router",
    )(h2, wr_t, rb_b)


def _row_copy(src, row, dst, sem):
    return pltpu.make_async_copy(src.at[pl.ds(row, 1), :], dst, sem)


DISPATCH_SLOTS = 3


def _dispatch_kernel(slot_ref, h_hbm, xs_hbm, hbuf, lsem, ssem, *, n_tok):
    i = pl.program_id(0)
    nt = pl.num_programs(0)
    tm = ROW_TILE

    def load(blk):
        row0 = pl.multiple_of(blk * tm, tm)
        return pltpu.make_async_copy(h_hbm.at[pl.ds(row0, tm), :], hbuf.at[blk % DISPATCH_SLOTS],
                                     lsem.at[blk % DISPATCH_SLOTS])

    def wait_rows(par):
        for _ in range(TOP_K):
            pltpu.make_async_copy(hbuf.at[0], xs_hbm.at[pl.ds(0, tm), :], ssem.at[par]).wait()

    @pl.when(i == 0)
    def _():
        load(0).start()

    @pl.when(i + 1 < nt)
    def _():
        load(i + 1).start()

    load(i).wait()
    par = i % 2
    cur = hbuf.at[i % DISPATCH_SLOTS]
    for r in range(tm):
        for k in range(TOP_K):
            dst = xs_hbm.at[pl.ds(slot_ref[k * n_tok + i * tm + r], 1), :]
            pltpu.make_async_copy(cur.at[pl.ds(r, 1), :], dst, ssem.at[par]).start(priority=k)

    @pl.when(i > 0)
    def _():
        wait_rows(1 - par)

    @pl.when(i == nt - 1)
    def _():
        wait_rows(par)


def _dispatch(h2, slots, rows):
    t, d = h2.shape
    return pl.pallas_call(
        functools.partial(_dispatch_kernel, n_tok=t),
        grid_spec=pltpu.PrefetchScalarGridSpec(
            num_scalar_prefetch=1,
            grid=(t // ROW_TILE,),
            in_specs=[pl.BlockSpec(memory_space=pl.ANY)],
            out_specs=pl.BlockSpec(memory_space=pl.ANY),
            scratch_shapes=[pltpu.VMEM((DISPATCH_SLOTS, ROW_TILE, d), F32),
                            pltpu.SemaphoreType.DMA((DISPATCH_SLOTS,)),
                            pltpu.SemaphoreType.DMA((2,))]),
        out_shape=jax.ShapeDtypeStruct((rows, d), F32),
        compiler_params=_params("arbitrary"),
        name="dispatch",
    )(slots, h2)


def _experts_kernel(ts_ref, nu_ref, h_hbm, wg_hbm, wu_hbm, wd_hbm, ys_hbm,
                    xbuf, ybuf, gsem, ysem, wg_f, wu_f, wd_f, wsem, wg_bf, wu_bf, wd_bf, *, layer):
    e = pl.program_id(0)
    ne = pl.num_programs(0)
    n_used = nu_ref[0]
    tm = MOE_TILE
    tsub = tm

    def has_rows(ex):
        return ts_ref[ex + 1] > ts_ref[ex]

    def weight_copies(ex, wslot):
        return [pltpu.make_async_copy(src.at[layer, ex], dst.at[wslot], wsem.at[wslot])
                for src, dst in ((wg_hbm, wg_f), (wu_hbm, wu_f), (wd_hbm, wd_f))]

    def start_weights(ex, wslot):
        for cp in weight_copies(ex, wslot):
            cp.start(priority=1)

    def x_load(tile, slot):
        row0 = pl.multiple_of(tile * tsub, tsub)
        return pltpu.make_async_copy(h_hbm.at[pl.ds(row0, tsub), :], xbuf.at[slot], gsem.at[slot])

    def gather(tile, slot):
        x_load(tile, slot).start()

    def wait_gather(slot):
        x_load(0, slot).wait()

    def y_store(tile, slot):
        row0 = pl.multiple_of(tile * tsub, tsub)
        return pltpu.make_async_copy(ybuf.at[slot], ys_hbm.at[pl.ds(row0, tsub), :], ysem.at[slot])

    wslot = e % 2

    @pl.when((e == 0) & has_rows(0))
    def _():
        start_weights(0, 0)

    @pl.when((e == 0) & (n_used > 0))
    def _():
        gather(0, 0)

    nxt = jnp.minimum(e + 1, ne - 1)

    @pl.when((e + 1 < ne) & has_rows(nxt))
    def _():
        start_weights(nxt, 1 - wslot)

    @pl.when(has_rows(e))
    def _():
        for cp in weight_copies(e, wslot):
            cp.wait()
        wg_bf[...] = wg_f[wslot].astype(BF16)
        wu_bf[...] = wu_f[wslot].astype(BF16)
        wd_bf[...] = wd_f[wslot].astype(BF16)

    def tile_body(g, carry):
        slot = g % 2

        @pl.when(g >= 2)
        def _():
            y_store(g - 2, slot).wait()

        wait_gather(slot)
        x = xbuf[slot].astype(BF16)
        gather(jnp.minimum(g + 1, n_used - 1), 1 - slot)
        gate = jnp.dot(x, wg_bf[...], preferred_element_type=F32)
        up = jnp.dot(x, wu_bf[...], preferred_element_type=F32)
        hid = (gate * _sigmoid(gate)) * up
        ybuf[slot] = jnp.dot(hid.astype(BF16), wd_bf[...], preferred_element_type=F32)
        y_store(g, slot).start(priority=1)
        return carry

    lax.fori_loop(ts_ref[e], ts_ref[e + 1], tile_body, 0)

    @pl.when((e == pl.num_programs(0) - 1) & (n_used > 0))
    def _():
        wait_gather(n_used % 2)
        y_store(n_used - 1, (n_used - 1) % 2).wait()

        @pl.when(n_used >= 2)
        def _():
            y_store(n_used - 2, n_used % 2).wait()


def _experts(xs, tile_start, n_used, w_gate, w_up, w_down, layer):
    d = D_MODEL
    tm = MOE_TILE
    rows = xs.shape[0]

    hbm = pl.BlockSpec(memory_space=pl.ANY)
    return pl.pallas_call(
        functools.partial(_experts_kernel, layer=layer),
        grid_spec=pltpu.PrefetchScalarGridSpec(
            num_scalar_prefetch=2,
            grid=(N_EXPERTS,),
            in_specs=[hbm, hbm, hbm, hbm],
            out_specs=hbm,
            scratch_shapes=[pltpu.VMEM((2, tm, d), F32),
                            pltpu.VMEM((2, tm, d), F32),
                            pltpu.SemaphoreType.DMA((2,)),
                            pltpu.SemaphoreType.DMA((2,)),
                            pltpu.VMEM((2, d, EXPERT_FF), F32),
                            pltpu.VMEM((2, d, EXPERT_FF), F32),
                            pltpu.VMEM((2, EXPERT_FF, d), F32),
                            pltpu.SemaphoreType.DMA((2,)),
                            pltpu.VMEM((d, EXPERT_FF), BF16),
                            pltpu.VMEM((d, EXPERT_FF), BF16),
                            pltpu.VMEM((EXPERT_FF, d), BF16)]),
        out_shape=jax.ShapeDtypeStruct((rows, d), F32),
        compiler_params=_params("arbitrary"),
        name="experts",
    )(tile_start, n_used, xs, w_gate, w_up, w_down)


COMBINE_BUFFERS = 3


def _combine_kernel(slot_ref, ys_hbm, w_ref, x_ref, mod_ref, g_ref, b_ref, o_ref, ybuf, sem, *, n_tok):
    i = pl.program_id(0)
    nt = pl.num_programs(0)
    tm = ROW_TILE

    def issue(blk, buf):
        for r in range(tm):
            for k in range(TOP_K):
                row = slot_ref[k * n_tok + blk * tm + r]
                _row_copy(ys_hbm, row, ybuf.at[buf, k, pl.ds(r, 1), :], sem.at[buf]).start(priority=k)

    def wait(buf):
        for k in range(TOP_K):
            pltpu.make_async_copy(ys_hbm.at[pl.ds(0, tm), :], ybuf.at[buf, k], sem.at[buf]).wait()

    @pl.when(i == 0)
    def _():
        issue(0, 0)

    @pl.when((i == 0) & (nt > 1))
    def _():
        issue(1, 1)

    buf = i % COMBINE_BUFFERS
    wait(buf)

    @pl.when(i + 2 < nt)
    def _():
        issue(i + 2, (i + 2) % COMBINE_BUFFERS)

    y = w_ref[:, 0:1] * ybuf[buf, 0] + w_ref[:, 1:2] * ybuf[buf, 1]
    t = DEEPNORM_ALPHA * x_ref[...] + mod_ref[MOD_G2:MOD_G2 + 1, :] * y
    o_ref[...] = _layer_norm(t, g_ref[...], b_ref[...])


def _combine(ys, slots, w_tok, x1, mods, n_lat_blocks, ln_g, ln_b):
    t = x1.shape[0]
    d = D_MODEL
    return pl.pallas_call(
        functools.partial(_combine_kernel, n_tok=t),
        grid_spec=pltpu.PrefetchScalarGridSpec(
            num_scalar_prefetch=1,
            grid=(t // ROW_TILE,),
            in_specs=[pl.BlockSpec(memory_space=pl.ANY),
                      pl.BlockSpec((ROW_TILE, TOP_K), lambda i, s: (i, 0)),
                      pl.BlockSpec((ROW_TILE, d), lambda i, s: (i, 0)),
                      pl.BlockSpec((None, 8, d), lambda i, s: (jnp.where(i < n_lat_blocks, 1, 0), 0, 0)),
                      pl.BlockSpec((1, d), lambda i, s: (0, 0)),
                      pl.BlockSpec((1, d), lambda i, s: (0, 0))],
            out_specs=pl.BlockSpec((ROW_TILE, d), lambda i, s: (i, 0)),
            scratch_shapes=[pltpu.VMEM((COMBINE_BUFFERS, TOP_K, ROW_TILE, d), F32),
                            pltpu.SemaphoreType.DMA((COMBINE_BUFFERS,))]),
        out_shape=jax.ShapeDtypeStruct((t, d), F32),
        compiler_params=_params("arbitrary"),
        name="combine",
    )(slots, ys, w_tok, x1, mods, ln_g, ln_b)


def _moe(h2, x1, mods, n_lat_blocks, wr_t, rb_b, w_gate, w_up, w_down, layer, ln_g, ln_b):
    t = x1.shape[0]
    tm = MOE_TILE
    e_idx, w_tok, rank, cnt = _router(h2, wr_t, rb_b)
    counts = cnt[:, 0].astype(jnp.int32)
    tiles_per = (counts + tm - 1) // tm
    tile_end = jnp.cumsum(tiles_per)
    n_used = tile_end[-1]
    row_off = (tile_end - tiles_per) * tm
    experts = jnp.arange(N_EXPERTS, dtype=jnp.int32)
    slots = jnp.sum(jnp.where(e_idx[:, :, None] == experts, row_off, 0), axis=-1) + rank
    max_tiles = (TOP_K * t + N_EXPERTS * (tm - 1)) // tm + 1
    tile_start = jnp.concatenate([tile_end - tiles_per, n_used.reshape(1)]).astype(jnp.int32)
    slots = slots.reshape(-1)
    xs = _dispatch(h2, slots, max_tiles * tm)
    ys = _experts(xs, tile_start, n_used.reshape(1).astype(jnp.int32), w_gate, w_up, w_down, layer)
    return _combine(ys, slots, w_tok.T, x1, mods, n_lat_blocks, ln_g, ln_b)


def _rope_tables(n):
    t = np.arange(n)
    row = (t // GRID_W).astype(np.float64)
    col = (t % GRID_W).astype(np.float64)
    n_freq = HEAD_DIM // 4
    inv_freq = ROPE_BASE ** (-np.arange(n_freq, dtype=np.float64) / n_freq)
    ang = np.concatenate([row[:, None] * inv_freq, col[:, None] * inv_freq], axis=-1)
    cos, sin = np.cos(ang), np.sin(ang)
    return (np.concatenate([cos, cos], axis=-1).astype(np.float32),
            np.concatenate([-sin, sin], axis=-1).astype(np.float32))


def kernel(x, c, ctx, c_ctx, w_mod, b_mod, w_in, attn_sink, na_rpb, sgu_ln_g, sgu_ln_b, sgu_w, sgu_b,
           w_out, ln1_g, ln1_b, w_router, router_bias, w_gate, w_up, w_down, ln2_g, ln2_b):
    batch, n, d = x.shape
    lctx = ctx.shape[1]
    assert batch == 1 and d == D_MODEL and n % ROW_TILE == 0 and lctx % ROW_TILE == 0
    n_lat_blocks = n // ROW_TILE

    mods = _modulation(c, c_ctx, w_mod, b_mod).reshape(DEPTH, 8, 6, d)
    mod_lat = jnp.pad(mods[:, 0], ((0, 0), (0, 2), (0, 0)))
    mod_ctx = jnp.pad(mods[:, 1], ((0, 0), (0, 2), (0, 0)))
    cos, sin = _rope_tables(n)
    wr_f = w_router.T
    def bf16_part(v):
        return lax.bitcast_convert_type(lax.bitcast_convert_type(v, jnp.int32) & jnp.int32(-65536), F32)

    wr_hi = bf16_part(wr_f)
    wr_mid = bf16_part(wr_f - wr_hi)
    wr_lo = wr_f - wr_hi - wr_mid
    wr_t = jnp.concatenate([wr_hi, wr_mid, wr_lo], axis=0).astype(BF16)
    rb_b = jnp.broadcast_to(router_bias.reshape(N_EXPERTS, 1), (N_EXPERTS, LANES))

    x_lat = x[0]
    x_ctx, ctx_row0 = ctx[0], 0
    w_in_bf = w_in.astype(BF16)
    w_out_bf = w_out.astype(BF16)
    bias_tabs = _na_bias_tables(na_rpb)
    for l in range(DEPTH):
        last = l == DEPTH - 1
        sgu_params = (sgu_ln_g[l].reshape(1, C_W), sgu_ln_b[l].reshape(1, C_W), sgu_w[l].astype(BF16),
                      jnp.broadcast_to(sgu_b[l][:, :, None], (C_GROUPS, CHUNK, LANES)))
        g1, b1 = ln1_g[l].reshape(1, d), ln1_b[l].reshape(1, d)
        g2, b2 = ln2_g[l].reshape(1, d), ln2_b[l].reshape(1, d)

        p = _proj(x_lat, 0, n, mod_lat[l], cos, sin, w_in_bf, l, rope=True)
        pc = _proj(x_ctx, ctx_row0, lctx, mod_ctx[l], cos, sin, w_in_bf, l, rope=False)
        o_a, o_c = _attn_a_sgu(p, pc, attn_sink[l], sgu_params, latent=True)
        o_b = _attn_b(p, pc, bias_tabs, l, latent=True)
        mod_both = jnp.stack([mod_ctx[l], mod_lat[l]])
        if last:
            x1, h2 = _outproj(o_a, o_b, o_c, w_out_bf, l, x_lat, 0, mod_lat[l], g1, b1, n, 0, None)
            x_lat = _moe(h2, x1, mod_both, n_lat_blocks, wr_t, rb_b, w_gate, w_up, w_down, l, g2, b2)
        else:
            oc_a, oc_c = _attn_a_sgu(pc, pc, attn_sink[l], sgu_params, latent=False)
            oc_b = _attn_b(pc, pc, bias_tabs, l, latent=False)
            total = n + lctx
            prev = _outproj(o_a, o_b, o_c, w_out_bf, l, x_lat, 0, mod_lat[l], g1, b1, total, 0, None)
            x1, h2 = _outproj(oc_a, oc_b, oc_c, w_out_bf, l, x_ctx, ctx_row0, mod_ctx[l], g1, b1, total, n,
                              prev)
            x_all = _moe(h2, x1, mod_both, n_lat_blocks, wr_t, rb_b, w_gate, w_up, w_down, l, g2, b2)
            x_lat, x_ctx, ctx_row0 = x_all, x_all, n
    return x_lat.reshape(batch, n, d)
```
